```python
import math
import jax
import jax.numpy as jnp
from jax import lax
import numpy as np

D_MODEL = 1024
BATCH = 32
SEQ = 256
DEPTH = 2
DEC_BATCH = 4
DEC_SEQ = 2048
PAST_LEN = 256

GRID_W = 64
N_EVEN = (DEPTH + 1) // 2
N_ODD = DEPTH // 2
EPS = 1e-6
H_A = 4
DK_A = 64
DV_A = 128
Q_A = H_A * DK_A
V_A = H_A * DV_A
GATE_RANK = 16
GATE_TAU = 16.0
GLA_CHUNK = 64
H_B = 4
DQK_B = 64
DV_B = 128
QK_B = H_B * 2 * DQK_B
V_B = H_B * DV_B
Q_BLOCK = 128
ROPE_BASE = 10000.0
AB_IN = 2 * Q_A + 2 * V_A + 2 * GATE_RANK + 2 * QK_B + V_B
SGU_DIM = 1024
SGU_GROUPS = 4
SGU_CHUNK = 128
N_EXPERTS = 64
TOP_K = 8
N_GROUPS = 8
TOPK_GROUPS = 4
D_EXPERT = 256
D_SHARED = 256
ROUTED_SCALE = 2.5

kernel_name = 'hybrid_diffusion_gla_diffattn_sgu_moe_step'


def rmsnorm(x, g):
    xf = x.astype(jnp.float32)
    y = xf * lax.rsqrt(jnp.mean(xf * xf, axis=-1, keepdims=True) + EPS)
    return (y * g.astype(jnp.float32)).astype(x.dtype)


def modulation(cond, w_ada, b_ada):
    m = jax.nn.silu(cond) @ w_ada + b_ada
    return jnp.split(m[:, None, :], 6, axis=-1)


def modulate(x, shift, scale):
    return x * (1.0 + scale) + shift


def axial_rope_tables(n):
    rows = n // GRID_W
    row = jnp.repeat(jnp.arange(rows), GRID_W).astype(jnp.float32)
    col = jnp.tile(jnp.arange(GRID_W), rows).astype(jnp.float32)
    half = DQK_B // 2
    inv = ROPE_BASE ** (-jnp.arange(0, half, 2, dtype=jnp.float32) / half)
    ang_r = row[:, None] * inv
    ang_c = col[:, None] * inv
    ang = jnp.concatenate([ang_r, ang_r, ang_c, ang_c], axis=-1)
    return jnp.cos(ang), jnp.sin(ang)


def apply_rope(x, cos, sin):
    def rot(t):
        t1, t2 = jnp.split(t, 2, axis=-1)
        return jnp.concatenate([-t2, t1], axis=-1)
    xr, xc = jnp.split(x, 2, axis=-1)
    x_rot = jnp.concatenate([rot(xr), rot(xc)], axis=-1)
    return (x * cos + x_rot * sin).astype(x.dtype)


def gla_scan(q, k, v, log_a, s0):
    B, H, L, _ = q.shape
    n = L // GLA_CHUNK

    def chunks(t):
        t = t.astype(jnp.float32)
        return jnp.moveaxis(t.reshape(B, H, n, GLA_CHUNK, t.shape[-1]), 2, 0)

    lower = jnp.tril(jnp.ones((GLA_CHUNK, GLA_CHUNK), dtype=bool))

    def step(s, inp):
        qc, kc, vc, ac = inp
        b = jnp.cumsum(ac, axis=2)
        b_end = b[:, :, -1:, :]
        q_in = qc * jnp.exp(b)
        attn = jnp.where(lower, jnp.einsum('bhik,bhjk->bhij', q_in, kc * jnp.exp(-b)), 0.0)
        o = jnp.einsum('bhij,bhjv->bhiv', attn, vc) + jnp.einsum('bhik,bhkv->bhiv', q_in, s)
        s = jnp.exp(b_end[:, :, 0, :, None]) * s + jnp.einsum('bhjk,bhjv->bhkv', kc * jnp.exp(b_end - b), vc)
        return s, o

    s_fin, o = lax.scan(step, s0.astype(jnp.float32), tuple(chunks(t) for t in (q, k, v, log_a)))
    o = jnp.moveaxis(o, 0, 2).reshape(B, H, L, -1)
    return o.astype(v.dtype), s_fin


def gla_bidir(q, k, v, la_f, la_b, s0):
    o_f, s_f = gla_scan(q, k, v, la_f, s0[:, 0])
    flip = lambda t: jnp.flip(t, axis=2)
    o_b, s_b = gla_scan(flip(q), flip(k), flip(v), flip(la_b), s0[:, 1])
    return o_f + flip(o_b), jnp.stack([s_f, s_b], axis=1)


def diff_attention(q, k, v, lam):
    B, H, _, Lq, d = q.shape
    nb = Lq // Q_BLOCK
    qb = jnp.moveaxis(q.reshape(B, H, 2, nb, Q_BLOCK, d), 3, 0)
    scale = d ** -0.5

    def block(qi):
        s = jnp.einsum('bhmqd,bhmkd->bhmqk', qi, k).astype(jnp.float32) * scale
        p = jax.nn.softmax(s, axis=-1)
        w = p[:, :, 0] - lam * p[:, :, 1]
        return jnp.einsum('bhqk,bhkv->bhqv', w.astype(v.dtype), v)

    o = lax.map(block, qb)
    return jnp.moveaxis(o, 0, 2).reshape(B, H, Lq, -1)


def _ab_split_points():
    sizes = (Q_A, Q_A, V_A, V_A, GATE_RANK, GATE_RANK, QK_B, QK_B)
    return tuple(int(s) for s in np.cumsum(sizes))


def mixer_ab(h, w_in, w_g2, b_g2, gla_g, lam_p, diff_g, w_o, lam_init, s0, ctx_k, ctx_v, rope):
    B, L, _ = h.shape
    p = h @ w_in
    q_a, k_a, v_a, r_a, gl_f, gl_b, q_b, k_b, v_b = jnp.split(p, _ab_split_points(), axis=-1)
    heads = lambda t, n: t.reshape(B, L, n, -1).transpose(0, 2, 1, 3)
    la_f = heads(jax.nn.log_sigmoid(gl_f @ w_g2[0] + b_g2[0]) / GATE_TAU, H_A)
    la_b = heads(jax.nn.log_sigmoid(gl_b @ w_g2[1] + b_g2[1]) / GATE_TAU, H_A)
    o_a, s_fin = gla_bidir(heads(q_a, H_A) * DK_A ** -0.5, heads(k_a, H_A), heads(v_a, H_A), la_f, la_b, s0)
    o_a = rmsnorm(o_a, gla_g).transpose(0, 2, 1, 3).reshape(B, L, V_A) * jax.nn.silu(r_a)
    qd = q_b.reshape(B, L, H_B, 2, DQK_B).transpose(0, 2, 3, 1, 4)
    kd = k_b.reshape(B, L, H_B, 2, DQK_B).transpose(0, 2, 3, 1, 4)
    vd = heads(v_b, H_B)
    if rope is None:
        k_all, v_all = kd, vd
    else:
        cos, sin = rope
        qd = apply_rope(qd, cos, sin)
        kd = apply_rope(kd, cos, sin)
        k_all = jnp.concatenate([ctx_k, kd], axis=3)
        v_all = jnp.concatenate([ctx_v, vd], axis=2)
    lam = jnp.exp(jnp.sum(lam_p[0] * lam_p[1])) - jnp.exp(jnp.sum(lam_p[2] * lam_p[3])) + lam_init
    o_b = diff_attention(qd, k_all, v_all, lam)
    o_b = (rmsnorm(o_b, diff_g) * (1.0 - lam_init)).transpose(0, 2, 1, 3).reshape(B, L, V_B)
    out = jnp.concatenate([o_a, o_b], axis=-1) @ w_o
    return out, s_fin.astype(h.dtype), kd, vd


def chunk_sgu(h, w_in, b_in, v_g, w_s, b_s, w_out):
    B, L, _ = h.shape
    z = jax.nn.gelu(h @ w_in + b_in)
    u, v = jnp.split(z, 2, axis=-1)
    v = rmsnorm(v, v_g)
    n = L // SGU_CHUNK
    vg = v.reshape(B, n, SGU_CHUNK, SGU_GROUPS, SGU_DIM // SGU_GROUPS)
    vs = jnp.einsum('gpq,bnqgc->bnpgc', w_s, vg) + b_s.T[None, None, :, :, None]
    return (u * vs.reshape(B, L, SGU_DIM)) @ w_out


def moe(h, w_router, e_bias, w_gate, w_up, w_down, ws_gate, ws_up, ws_down):
    B, L, D = h.shape
    t = h.reshape(B * L, D)
    scores = jax.nn.sigmoid((t @ w_router).astype(jnp.float32))
    sel = scores + e_bias.astype(jnp.float32)
    grp = sel.reshape(-1, N_GROUPS, N_EXPERTS // N_GROUPS)
    grp_score = jnp.sum(lax.top_k(grp, 2)[0], axis=-1)
    _, top_g = lax.top_k(grp_score, TOPK_GROUPS)
    g_mask = jnp.sum(jax.nn.one_hot(top_g, N_GROUPS, dtype=jnp.float32), axis=1)
    e_mask = jnp.repeat(g_mask, N_EXPERTS // N_GROUPS, axis=-1)
    _, top_e = lax.top_k(jnp.where(e_mask > 0, sel, -jnp.inf), TOP_K)
    w = jnp.take_along_axis(scores, top_e, axis=-1)
    w = w / jnp.sum(w, axis=-1, keepdims=True) * ROUTED_SCALE
    gates = jnp.einsum('tk,tke->te', w, jax.nn.one_hot(top_e, N_EXPERTS, dtype=w.dtype))

    def expert(acc, prm):
        wg, wu, wd, g = prm
        hid = jax.nn.silu(t @ wg) * (t @ wu)
        return acc + (hid * g[:, None].astype(hid.dtype)) @ wd, None

    y, _ = lax.scan(expert, jnp.zeros_like(t), (w_gate, w_up, w_down, gates.T))
    y = y + (jax.nn.silu(t @ ws_gate) * (t @ ws_up)) @ ws_down
    return y.reshape(B, L, D)


def setup_inputs(seed: int = 0) -> dict:
    key = jax.random.key(seed)
    ks = iter(jax.random.split(key, 48))
    nrm = lambda shape, scale: jax.random.normal(next(ks), shape, jnp.float32) * scale
    gain = lambda shape: 1.0 + nrm(shape, 0.05)
    D = D_MODEL
    return {
        'x_prompt': nrm((BATCH, SEQ, D), 1.0),
        'x_sample': nrm((DEC_BATCH, DEC_SEQ, D), 1.0),
        'c': nrm((DEC_BATCH, D), 1.0),
        'c_ctx': nrm((D,), 1.0),
        'state_gla': nrm((DEC_BATCH, N_EVEN, 2, H_A, DK_A, DV_A), 1.0),
        'cache_k': nrm((DEC_BATCH, N_EVEN, H_B, 2, PAST_LEN, DQK_B), 1.0),
        'cache_v': nrm((DEC_BATCH, N_EVEN, H_B, PAST_LEN, DV_B), 1.0),
        'ada_w': nrm((DEPTH, D, 6 * D), 0.5 * D ** -0.5),
        'ada_b': nrm((DEPTH, 6 * D), 0.02),
        'norm_pre_mix': gain((DEPTH, D)),
        'norm_post_mix': gain((DEPTH, D)),
        'norm_pre_ffn': gain((DEPTH, D)),
        'norm_post_ffn': gain((DEPTH, D)),
        'ab_w_in': nrm((N_EVEN, D, AB_IN), D ** -0.5),
        'gla_w_g2': nrm((N_EVEN, 2, GATE_RANK, Q_A), GATE_RANK ** -0.5),
        'gla_b_g2': nrm((N_EVEN, 2, Q_A), 0.1),
        'gla_norm_g': gain((N_EVEN, DV_A)),
        'diff_lambda': nrm((N_EVEN, 4, DQK_B), 0.1),
        'diff_norm_g': gain((N_EVEN, DV_B)),
        'ab_w_out': nrm((N_EVEN, V_A + V_B, D), (V_A + V_B) ** -0.5),
        'sgu_w_in': nrm((N_ODD, D, 2 * SGU_DIM), D ** -0.5),
        'sgu_b_in': nrm((N_ODD, 2 * SGU_DIM), 0.02),
        'sgu_norm_g': gain((N_ODD, SGU_DIM)),
        'sgu_w_s': nrm((N_ODD, SGU_GROUPS, SGU_CHUNK, SGU_CHUNK), SGU_CHUNK ** -0.5),
        'sgu_b_s': 1.0 + nrm((N_ODD, SGU_GROUPS, SGU_CHUNK), 0.1),
        'sgu_w_out': nrm((N_ODD, SGU_DIM, D), SGU_DIM ** -0.5),
        'moe_w_router': nrm((DEPTH, D, N_EXPERTS), D ** -0.5),
        'moe_e_bias': nrm((DEPTH, N_EXPERTS), 0.01),
        'moe_w_gate': nrm((DEPTH, N_EXPERTS, D, D_EXPERT), D ** -0.5),
        'moe_w_up': nrm((DEPTH, N_EXPERTS, D, D_EXPERT), D ** -0.5),
        'moe_w_down': nrm((DEPTH, N_EXPERTS, D_EXPERT, D), D_EXPERT ** -0.5),
        'moe_ws_gate': nrm((DEPTH, D, D_SHARED), D ** -0.5),
        'moe_ws_up': nrm((DEPTH, D, D_SHARED), D ** -0.5),
        'moe_ws_down': nrm((DEPTH, D_SHARED, D), D_SHARED ** -0.5),
    }


def reference(x_prompt, x_sample, c, c_ctx, state_gla, cache_k, cache_v,
              ada_w, ada_b, norm_pre_mix, norm_post_mix, norm_pre_ffn, norm_post_ffn,
              ab_w_in, gla_w_g2, gla_b_g2, gla_norm_g, diff_lambda, diff_norm_g, ab_w_out,
              sgu_w_in, sgu_b_in, sgu_norm_g, sgu_w_s, sgu_b_s, sgu_w_out,
              moe_w_router, moe_e_bias, moe_w_gate, moe_w_up, moe_w_down,
              moe_ws_gate, moe_ws_up, moe_ws_down):
    rope = axial_rope_tables(x_sample.shape[1])
    ctx_s0 = jnp.zeros((x_prompt.shape[0], 2, H_A, DK_A, DV_A), jnp.float32)
    xp, xs = x_prompt, x_sample
    new_s, new_k, new_v = [], [], []
    for l in range(DEPTH):
        mp = modulation(c_ctx[None, :], ada_w[l], ada_b[l])
        ms = modulation(c, ada_w[l], ada_b[l])
        hp = modulate(rmsnorm(xp, norm_pre_mix[l]), mp[0], mp[1])
        hs = modulate(rmsnorm(xs, norm_pre_mix[l]), ms[0], ms[1])
        if l % 2 == 0:
            e = l // 2
            lam_init = 0.8 - 0.6 * math.exp(-0.3 * l)
            w = (ab_w_in[e], gla_w_g2[e], gla_b_g2[e], gla_norm_g[e], diff_lambda[e],
                 diff_norm_g[e], ab_w_out[e], lam_init)
            op, s_ctx, k_ctx, v_ctx = mixer_ab(hp, *w, ctx_s0, None, None, None)
            os_, _, _, _ = mixer_ab(hs, *w, state_gla[:, e], cache_k[:, e], cache_v[:, e], rope)
            new_s.append(s_ctx)
            new_k.append(k_ctx)
            new_v.append(v_ctx)
        else:
            o = l // 2
            w = (sgu_w_in[o], sgu_b_in[o], sgu_norm_g[o], sgu_w_s[o], sgu_b_s[o], sgu_w_out[o])
            op = chunk_sgu(hp, *w)
            os_ = chunk_sgu(hs, *w)
        xp = xp + mp[2] * rmsnorm(op, norm_post_mix[l])
        xs = xs + ms[2] * rmsnorm(os_, norm_post_mix[l])
        fw = (moe_w_router[l], moe_e_bias[l], moe_w_gate[l], moe_w_up[l], moe_w_down[l],
              moe_ws_gate[l], moe_ws_up[l], moe_ws_down[l])
        fp = moe(modulate(rmsnorm(xp, norm_pre_ffn[l]), mp[3], mp[4]), *fw)
        fs = moe(modulate(rmsnorm(xs, norm_pre_ffn[l]), ms[3], ms[4]), *fw)
        xp = xp + mp[5] * rmsnorm(fp, norm_post_ffn[l])
        xs = xs + ms[5] * rmsnorm(fs, norm_post_ffn[l])
    new_state_gla = jnp.stack(new_s, axis=1)
    new_cache_k = jnp.stack(new_k, axis=1)
    new_cache_v = jnp.stack(new_v, axis=1)
    return (xp, xs, new_state_gla, new_cache_k, new_cache_v)
```

```python
import functools
import math

import jax
import jax.numpy as jnp
from jax import lax
from jax.experimental import pallas as pl
from jax.experimental.pallas import tpu as pltpu

F32 = jnp.float32
BF16 = jnp.bfloat16
I32 = jnp.int32

D = 1024
BATCH, SEQ = 32, 256
DEC_BATCH, DEC_SEQ = 4, 2048
PAST_LEN = 256
GRID_W = 64
EPS = 1e-6
TP = BATCH * SEQ
TS = DEC_BATCH * DEC_SEQ
T = TP + TS
H_A, DK_A, DV_A = 4, 64, 128
Q_A, V_A = H_A * DK_A, H_A * DV_A
GATE_RANK, GATE_TAU, GLA_CHUNK = 16, 16.0, 64
H_B, DQK_B, DV_B = 4, 64, 128
QK_B, V_B = H_B * 2 * DQK_B, H_B * DV_B
ROPE_BASE = 10000.0
SGU_DIM, SGU_GROUPS, SGU_CHUNK = 1024, 4, 128
N_EXPERTS, TOP_K, N_GROUPS, TOPK_GROUPS = 64, 8, 8, 4
GROUP_SIZE = N_EXPERTS // N_GROUPS
D_EXPERT, D_SHARED = 256, 256
ROUTED_SCALE = 2.5

TM = 512
NPT = TP // TM
TILES_PER_DEC = DEC_SEQ // TM
SEG = 256
NSEG = T // SEG
NSEG_P = TP // SEG
SEG_PER_DEC = DEC_SEQ // SEG
TR = 512
TD = 256
GM = 256
NT_MAX = T * TOP_K // GM + N_EXPERTS
SP = NT_MAX * GM
GL_PAD = 128
VMEM_LIMIT = 56 * 1024 * 1024
NEG_INF = float("-inf")


def _bdot(a, b):
    return jnp.dot(a.astype(BF16), b.astype(BF16), preferred_element_type=F32)


def _bdot_nt(a, b):
    return lax.dot_general(a.astype(BF16), b.astype(BF16), (((1,), (1,)), ((), ())),
                           preferred_element_type=F32)


def _bdot_tn(a, b):
    return lax.dot_general(a.astype(BF16), b.astype(BF16), (((0,), (0,)), ((), ())),
                           preferred_element_type=F32)


def _split3(x):
    x1 = x.astype(BF16)
    r1 = x - x1.astype(F32)
    x2 = r1.astype(BF16)
    x3 = (r1 - x2.astype(F32)).astype(BF16)
    return x1, x2, x3


def _rms(x, g):
    return x * lax.rsqrt(jnp.mean(x * x, axis=-1, keepdims=True) + EPS) * g


def _silu(x):
    return x * jax.nn.sigmoid(x)


def _mod_row(i):
    return jnp.where(i < NPT, 0, 1 + (i - NPT) // TILES_PER_DEC)


def _params(sem, limit=VMEM_LIMIT):
    return pltpu.CompilerParams(dimension_semantics=sem, vmem_limit_bytes=limit)


def _mod_kernel(c_ref, w_ref, b_ref, o_ref):
    o_ref[0] = _bdot(_silu(c_ref[...]), w_ref[0]) + b_ref[0]


def _modulation(cond, ada_w, ada_b):
    depth = ada_w.shape[0]
    nj = 6
    out = pl.pallas_call(
        _mod_kernel,
        grid=(depth, nj),
        in_specs=[
            pl.BlockSpec((8, D), lambda l, j: (0, 0)),
            pl.BlockSpec((1, D, D), lambda l, j: (l, 0, j)),
            pl.BlockSpec((1, 1, D), lambda l, j: (l, 0, j)),
        ],
        out_specs=pl.BlockSpec((1, 8, D), lambda l, j: (l, 0, j)),
        out_shape=jax.ShapeDtypeStruct((depth, 8, 6 * D), F32),
        compiler_params=_params(("arbitrary", "arbitrary")),
        name="adaln_modulation",
    )(cond, ada_w, ada_b.reshape(depth, 1, 6 * D))
    return out.reshape(depth, 8, 6, D)


_C_A, _C_R, _C_GL, _C_Q, _C_K, _C_V, _C_END = 0, 1024, 1536, 1664, 2176, 2688, 3200


def _rope(x, cos, sin):
    lane = lax.broadcasted_iota(I32, x.shape, 1)
    first = (lane % 32) < 16
    n = x.shape[1]
    xr = jnp.where(first, -pltpu.roll(x, n - 16, 1), pltpu.roll(x, 16, 1))
    return x * cos + xr * sin


def _in_kernel(x_ref, mod_ref, g_ref, w_ref, cos_ref, sin_ref,
               a_ref, r_ref, gl_ref, q_ref, k_ref, v_ref):
    i = pl.program_id(0)
    m = mod_ref[0, 0]
    h = _rms(x_ref[...], g_ref[...]) * (1.0 + m[1:2]) + m[0:1]
    hb = h.astype(BF16)

    def proj(c0, c1):
        return jnp.dot(hb, w_ref[:, c0:c1], preferred_element_type=F32)

    a_ref[...] = proj(_C_A, _C_R)
    r_ref[...] = proj(_C_R, _C_GL)
    gl_ref[...] = proj(_C_GL, _C_Q)
    v_ref[...] = proj(_C_V, _C_END)
    q = proj(_C_Q, _C_K)
    k = proj(_C_K, _C_V)

    @pl.when(i < NPT)
    def _():
        q_ref[...] = q
        k_ref[...] = k

    @pl.when(i >= NPT)
    def _():
        cos = cos_ref[...]
        sin = sin_ref[...]
        q_ref[...] = _rope(q, cos, sin)
        k_ref[...] = _rope(k, cos, sin)


def _in_proj(x, mod, l, g, w_pad, cos, sin):
    tok = lambda width: pl.BlockSpec((TM, width), lambda i: (i, 0))
    rope_spec = pl.BlockSpec((TM, QK_B), lambda i: (jnp.maximum(i - NPT, 0) % TILES_PER_DEC, 0))
    widths = (1024, 512, GL_PAD, 512, 512, 512)
    return pl.pallas_call(
        _in_kernel,
        grid=(T // TM,),
        in_specs=[
            tok(D),
            pl.BlockSpec((1, 1, 6, D), lambda i: (l, _mod_row(i), 0, 0)),
            pl.BlockSpec((1, D), lambda i: (0, 0)),
            pl.BlockSpec((D, _C_END), lambda i: (0, 0)),
            rope_spec, rope_spec,
        ],
        out_specs=[tok(w) for w in widths],
        out_shape=[jax.ShapeDtypeStruct((T, w), F32) for w in widths],
        compiler_params=_params(("arbitrary",)),
        name="mixer_ab_in_proj",
    )(x, mod, g, w_pad, cos, sin)


def _log_sigmoid(x):
    return jnp.minimum(x, 0.0) - jnp.log1p(jnp.exp(-jnp.abs(x)))


def _gla_kernel(af_ref, ab_ref, glf_ref, glb_ref, wg_ref, bg_ref, s0_ref,
                of_ref, ob_ref, sfin_ref, st_ref):
    i = pl.program_id(0)

    @pl.when(i < NSEG_P)
    def _():
        st_ref[...] = jnp.zeros_like(st_ref)

    @pl.when(jnp.logical_and(i >= NSEG_P, (i - NSEG_P) % SEG_PER_DEC == 0))
    def _():
        st_ref[...] = s0_ref[0]

    r = lax.broadcasted_iota(I32, (SEG, SEG), 0)
    c = lax.broadcasted_iota(I32, (SEG, SEG), 1)
    same = (r // GLA_CHUNK) == (c // GLA_CHUNK)
    rc = lax.broadcasted_iota(I32, (GLA_CHUNK, GLA_CHUNK), 0)
    cc = lax.broadcasted_iota(I32, (GLA_CHUNK, GLA_CHUNK), 1)
    nchunk = SEG // GLA_CHUNK

    for d, (a_ref, gl_ref, o_ref) in enumerate(((af_ref, glf_ref, of_ref), (ab_ref, glb_ref, ob_ref))):
        fwd = d == 0
        gcol = gl_ref[:, d * GATE_RANK:(d + 1) * GATE_RANK]
        la = _log_sigmoid(_bdot(gcol, wg_ref[d]) + bg_ref[d]) / GATE_TAU
        tri = jnp.where(jnp.logical_and(same, (c <= r) if fwd else (c >= r)), 1.0, 0.0).astype(BF16)
        l1, l2, l3 = _split3(la)
        b_all = (jnp.dot(tri, l1, preferred_element_type=F32)
                 + jnp.dot(tri, l2, preferred_element_type=F32)
                 + jnp.dot(tri, l3, preferred_element_type=F32))
        keep = (cc <= rc) if fwd else (cc >= rc)
        order = range(nchunk) if fwd else range(nchunk - 1, -1, -1)
        for ch in order:
            r0 = ch * GLA_CHUNK
            rows = slice(r0, r0 + GLA_CHUNK)
            for h in range(H_A):
                kc = slice(h * DK_A, (h + 1) * DK_A)
                q = a_ref[rows, h * DK_A:(h + 1) * DK_A] * (DK_A ** -0.5)
                k = a_ref[rows, Q_A + h * DK_A:Q_A + (h + 1) * DK_A]
                v = a_ref[rows, 2 * Q_A + h * DV_A:2 * Q_A + (h + 1) * DV_A]
                b = b_all[rows, kc]
                b_end = b[GLA_CHUNK - 1:GLA_CHUNK, :] if fwd else b[0:1, :]
                q_in = q * jnp.exp(b)
                attn = jnp.where(keep, _bdot_nt(q_in, k * jnp.exp(-b)), 0.0)
                s_t = st_ref[d, h]
                o_ref[rows, h * DV_A:(h + 1) * DV_A] = _bdot(attn, v) + _bdot_nt(q_in, s_t)
                st_ref[d, h] = s_t * jnp.exp(b_end) + _bdot_tn(v, k * jnp.exp(b_end - b))

    @pl.when(i < NSEG_P)
    def _():
        sfin_ref[0] = st_ref[...]


def _seg_bwd(i):
    j = i - NSEG_P
    return jnp.where(i < NSEG_P, i, NSEG_P + (j // SEG_PER_DEC) * SEG_PER_DEC + (SEG_PER_DEC - 1 - j % SEG_PER_DEC))


def _gla(a, gl, wg, bg, s0_t):
    seg = lambda width, f: pl.BlockSpec((SEG, width), lambda i: (f(i), 0))
    ident = lambda i: i
    st_block = (1, 2, H_A, DV_A, DK_A)
    return pl.pallas_call(
        _gla_kernel,
        grid=(NSEG,),
        in_specs=[
            seg(D, ident), seg(D, _seg_bwd), seg(GL_PAD, ident), seg(GL_PAD, _seg_bwd),
            pl.BlockSpec((2, GATE_RANK, Q_A), lambda i: (0, 0, 0)),
            pl.BlockSpec((2, 1, Q_A), lambda i: (0, 0, 0)),
            pl.BlockSpec(st_block, lambda i: (jnp.maximum(i - NSEG_P, 0) // SEG_PER_DEC, 0, 0, 0, 0)),
        ],
        out_specs=[
            seg(V_A, ident), seg(V_A, _seg_bwd),
            pl.BlockSpec(st_block, lambda i: (jnp.minimum(i, NSEG_P - 1), 0, 0, 0, 0)),
        ],
        out_shape=[
            jax.ShapeDtypeStruct((T, V_A), F32),
            jax.ShapeDtypeStruct((T, V_A), F32),
            jax.ShapeDtypeStruct((BATCH, 2, H_A, DV_A, DK_A), F32),
        ],
        scratch_shapes=[pltpu.VMEM((2, H_A, DV_A, DK_A), F32)],
        compiler_params=_params(("arbitrary",)),
        name="gla_bidir",
    )(a, a, gl, gl, wg, bg, s0_t)


def _diff_lambda(lam_ref, lam_init):
    lp = lam_ref[...]
    s01 = jnp.sum(lp[0:1] * lp[1:2], axis=1, keepdims=True)
    s23 = jnp.sum(lp[2:3] * lp[3:4], axis=1, keepdims=True)
    return jnp.exp(s01) - jnp.exp(s23) + lam_init


def _attn_prompt_kernel(lam_init, q_ref, k_ref, v_ref, lam_ref, o_ref):
    lam = _diff_lambda(lam_ref, lam_init)
    scale = DQK_B ** -0.5
    for h in range(H_B):
        ps = []
        for m in range(2):
            cols = slice((2 * h + m) * DQK_B, (2 * h + m + 1) * DQK_B)
            s = _bdot_nt(q_ref[:, cols], k_ref[:, cols]) * scale
            e = jnp.exp(s - jnp.max(s, axis=1, keepdims=True))
            ps.append(e / jnp.sum(e, axis=1, keepdims=True))
        w = ps[0] - lam * ps[1]
        o_ref[:, h * DV_B:(h + 1) * DV_B] = _bdot(w, v_ref[:, h * DV_B:(h + 1) * DV_B])


def _attn_sample_kernel(lam_init, q_ref, k_ref, v_ref, ck_ref, cv_ref, lam_ref, o_ref):
    lam = _diff_lambda(lam_ref, lam_init)
    scale = DQK_B ** -0.5
    for h in range(H_B):
        parts = []
        for m in range(2):
            cols = slice((2 * h + m) * DQK_B, (2 * h + m + 1) * DQK_B)
            q = q_ref[:, cols]
            sc = _bdot_nt(q, ck_ref[0, 0, h, m]) * scale
            sn = _bdot_nt(q, k_ref[:, cols]) * scale
            mx = jnp.maximum(jnp.max(sc, axis=1, keepdims=True), jnp.max(sn, axis=1, keepdims=True))
            ec = jnp.exp(sc - mx)
            en = jnp.exp(sn - mx)
            den = jnp.sum(ec, axis=1, keepdims=True) + jnp.sum(en, axis=1, keepdims=True)
            parts.append((ec / den, en / den))
        wc = parts[0][0] - lam * parts[1][0]
        wn = parts[0][1] - lam * parts[1][1]
        o_ref[:, h * DV_B:(h + 1) * DV_B] = (_bdot(wc, cv_ref[0, 0, h])
                                             + _bdot(wn, v_ref[:, h * DV_B:(h + 1) * DV_B]))


QB = SEQ
NQB_DEC = DEC_SEQ // QB


def _attn_kernel(lam_init, q_ref, kp_ref, vp_ref, ks_ref, vs_ref, ck_ref, cv_ref, lam_ref, o_ref):
    i = pl.program_id(0)

    @pl.when(i < BATCH)
    def _():
        _attn_prompt_kernel(lam_init, q_ref, kp_ref, vp_ref, lam_ref, o_ref)

    @pl.when(i >= BATCH)
    def _():
        _attn_sample_kernel(lam_init, q_ref, ks_ref, vs_ref, ck_ref, cv_ref, lam_ref, o_ref)


def _diff_attention(q, k, v, cache_k, cache_v, lam_p, lam_init):
    blk = lambda rows, f: pl.BlockSpec((rows, 512), f)
    dec_b = lambda i: jnp.maximum(i - BATCH, 0) // NQB_DEC
    own = lambda i: (i, 0)
    prompt_kv = lambda i: (jnp.minimum(i, BATCH - 1), 0)
    dec_kv = lambda i: (TP // DEC_SEQ + dec_b(i), 0)
    return pl.pallas_call(
        functools.partial(_attn_kernel, lam_init),
        grid=(BATCH + DEC_BATCH * NQB_DEC,),
        in_specs=[
            blk(QB, own), blk(SEQ, prompt_kv), blk(SEQ, prompt_kv), blk(DEC_SEQ, dec_kv), blk(DEC_SEQ, dec_kv),
            pl.BlockSpec((1, 1, H_B, 2, PAST_LEN, DQK_B), lambda i: (dec_b(i), 0, 0, 0, 0, 0)),
            pl.BlockSpec((1, 1, H_B, PAST_LEN, DV_B), lambda i: (dec_b(i), 0, 0, 0, 0)),
            pl.BlockSpec((4, DQK_B), lambda i: (0, 0)),
        ],
        out_specs=blk(QB, own),
        out_shape=jax.ShapeDtypeStruct((T, V_B), F32),
        compiler_params=_params(("arbitrary",)),
        name="diff_attention",
    )(q, k, v, k, v, cache_k, cache_v, lam_p)


def _head_rms(x, g, nheads, width):
    return jnp.concatenate([_rms(x[:, h * width:(h + 1) * width], g) for h in range(nheads)], axis=1)


def _mix_out_kernel(lam_init, of_ref, ob_ref, r_ref, oatt_ref, x_ref, mod_ref,
                    gg_ref, dg_ref, wo_ref, gp_ref, o_ref):
    m = mod_ref[0, 0]
    o_a = _head_rms(of_ref[...] + ob_ref[...], gg_ref[...], H_A, DV_A) * _silu(r_ref[...])
    o_b = _head_rms(oatt_ref[...], dg_ref[...], H_B, DV_B) * (1.0 - lam_init)
    out = _bdot(o_a, wo_ref[0:V_A, :]) + _bdot(o_b, wo_ref[V_A:V_A + V_B, :])
    o_ref[...] = x_ref[...] + m[2:3] * _rms(out, gp_ref[...])


def _mix_out(lam_init, o_f, o_b, r_a, o_att, x, mod, l, gla_g, diff_g, w_o, g_post):
    tok = lambda width: pl.BlockSpec((TM, width), lambda i: (i, 0))
    vec = lambda width: pl.BlockSpec((1, width), lambda i: (0, 0))
    return pl.pallas_call(
        functools.partial(_mix_out_kernel, lam_init),
        grid=(T // TM,),
        in_specs=[
            tok(512), tok(512), tok(512), tok(512), tok(D),
            pl.BlockSpec((1, 1, 6, D), lambda i: (l, _mod_row(i), 0, 0)),
            vec(DV_A), vec(DV_B),
            pl.BlockSpec((V_A + V_B, D), lambda i: (0, 0)),
            vec(D),
        ],
        out_specs=tok(D),
        out_shape=jax.ShapeDtypeStruct((T, D), F32),
        compiler_params=_params(("arbitrary",)),
        name="mixer_ab_out",
    )(o_f, o_b, r_a, o_att, x, mod, gla_g, diff_g, w_o, g_post)


def _gelu_tanh(x):
    return 0.5 * x * (1.0 + jnp.tanh(math.sqrt(2.0 / math.pi) * (x + 0.044715 * (x * x * x))))


def _sgu_kernel(x_ref, mod_ref, gpre_ref, win_ref, bin_ref, vg_ref, ws_ref, bs_ref,
                wout_ref, gpost_ref, o_ref, t_ref):
    m = mod_ref[0, 0]
    x = x_ref[...]
    h = _rms(x, gpre_ref[...]) * (1.0 + m[1:2]) + m[0:1]
    z = _gelu_tanh(_bdot(h, win_ref[...]) + bin_ref[...])
    v = _rms(z[:, SGU_DIM:], vg_ref[...])
    gw = SGU_DIM // SGU_GROUPS
    for ch in range(TM // SGU_CHUNK):
        rows = slice(ch * SGU_CHUNK, (ch + 1) * SGU_CHUNK)
        for g in range(SGU_GROUPS):
            cols = slice(g * gw, (g + 1) * gw)
            vs = _bdot(ws_ref[g], v[rows, cols]) + bs_ref[:, g:g + 1]
            t_ref[rows, cols] = (z[rows, cols] * vs).astype(BF16)
    out = jnp.dot(t_ref[...], wout_ref[...], preferred_element_type=F32)
    o_ref[...] = x + m[2:3] * _rms(out, gpost_ref[...])


def _sgu(x, mod, l, g_pre, w_in, b_in, v_g, w_s, b_s_t, w_out, g_post):
    tok = pl.BlockSpec((TM, D), lambda i: (i, 0))
    full = lambda *shape: pl.BlockSpec(shape, lambda i: (0,) * len(shape))
    return pl.pallas_call(
        _sgu_kernel,
        grid=(T // TM,),
        in_specs=[
            tok,
            pl.BlockSpec((1, 1, 6, D), lambda i: (l, _mod_row(i), 0, 0)),
            full(1, D), full(D, 2 * SGU_DIM), full(1, 2 * SGU_DIM), full(1, SGU_DIM),
            full(SGU_GROUPS, SGU_CHUNK, SGU_CHUNK), full(SGU_CHUNK, SGU_GROUPS),
            full(SGU_DIM, D), full(1, D),
        ],
        out_specs=tok,
        out_shape=jax.ShapeDtypeStruct((T, D), F32),
        scratch_shapes=[pltpu.VMEM((TM, SGU_DIM), BF16)],
        compiler_params=_params(("arbitrary",)),
        name="sgu_mixer",
    )(x, mod, g_pre, w_in, b_in, v_g, w_s, b_s_t, w_out, g_post)


def _router_kernel(x_ref, mod_ref, g_ref, wr_ref, eb_ref,
                   h_ref, te_ref, wn_ref, rk_ref, cnt_ref, carry_ref):
    i = pl.program_id(0)

    @pl.when(i == 0)
    def _():
        carry_ref[...] = jnp.zeros_like(carry_ref)

    m = mod_ref[0, 0]
    h = _rms(x_ref[...], g_ref[...]) * (1.0 + m[4:5]) + m[3:4]
    h_ref[...] = h
    h1, h2, _ = _split3(h)
    w1, w2, _ = _split3(wr_ref[...])
    nt = lambda a, b: lax.dot_general(a, b, (((1,), (1,)), ((), ())), preferred_element_type=F32)
    logits = nt(w1, h1) + nt(w1, h2) + nt(w2, h1)
    scores = jax.nn.sigmoid(logits)
    sel = scores + eb_ref[...]

    row8 = lax.broadcasted_iota(I32, (GROUP_SIZE, TR), 0)
    gscore = []
    for g in range(N_GROUPS):
        xg = sel[g * GROUP_SIZE:(g + 1) * GROUP_SIZE]
        m1 = jnp.max(xg, axis=0, keepdims=True)
        i1 = jnp.min(jnp.where(xg == m1, row8, GROUP_SIZE), axis=0, keepdims=True)
        m2 = jnp.max(jnp.where(row8 == i1, NEG_INF, xg), axis=0, keepdims=True)
        gscore.append(m1 + m2)
    pieces = []
    for g in range(N_GROUPS):
        rank = jnp.zeros((1, TR), I32)
        for g2 in range(N_GROUPS):
            if g2 == g:
                continue
            beats = (gscore[g2] >= gscore[g]) if g2 < g else (gscore[g2] > gscore[g])
            rank = rank + beats.astype(I32)
        pieces.append(jnp.where(rank < TOPK_GROUPS, sel[g * GROUP_SIZE:(g + 1) * GROUP_SIZE], NEG_INF))
    cur = jnp.concatenate(pieces, axis=0)

    row = lax.broadcasted_iota(I32, (N_EXPERTS, TR), 0)
    idxs, ws = [], []
    for _ in range(TOP_K):
        mx = jnp.max(cur, axis=0, keepdims=True)
        idx = jnp.min(jnp.where(cur == mx, row, N_EXPERTS), axis=0, keepdims=True)
        hit = row == idx
        ws.append(jnp.sum(jnp.where(hit, scores, 0.0), axis=0, keepdims=True))
        cur = jnp.where(hit, NEG_INF, cur)
        idxs.append(idx)
    mask = jnp.zeros((N_EXPERTS, TR), F32)
    for idx in idxs:
        mask = mask + (row == idx).astype(F32)
    wsum = ws[0]
    for wk in ws[1:]:
        wsum = wsum + wk

    tj = lax.broadcasted_iota(I32, (TR, TR), 0)
    ti = lax.broadcasted_iota(I32, (TR, TR), 1)
    upper = jnp.where(tj < ti, 1.0, 0.0).astype(BF16)
    pos = carry_ref[...] + jnp.dot(mask.astype(BF16), upper, preferred_element_type=F32)
    for k in range(TOP_K):
        hit = row == idxs[k]
        te_ref[k:k + 1, :] = idxs[k]
        wn_ref[k:k + 1, :] = ws[k] / wsum * ROUTED_SCALE
        rk_ref[k:k + 1, :] = jnp.sum(jnp.where(hit, pos, 0.0), axis=0, keepdims=True).astype(I32)
    carry_ref[...] = carry_ref[...] + jnp.sum(mask, axis=1, keepdims=True)
    cnt_ref[...] = carry_ref[...]


def _router(x, mod, l, g, wr_t, e_bias):
    kt = lambda dtype: jax.ShapeDtypeStruct((TOP_K, T), dtype)
    kt_spec = pl.BlockSpec((TOP_K, TR), lambda i: (0, i))
    tiles_per_dec = DEC_SEQ // TR
    mod_row = lambda i: jnp.where(i < TP // TR, 0, 1 + (i - TP // TR) // tiles_per_dec)
    return pl.pallas_call(
        _router_kernel,
        grid=(T // TR,),
        in_specs=[
            pl.BlockSpec((TR, D), lambda i: (i, 0)),
            pl.BlockSpec((1, 1, 6, D), lambda i: (l, mod_row(i), 0, 0)),
            pl.BlockSpec((1, D), lambda i: (0, 0)),
            pl.BlockSpec((N_EXPERTS, D), lambda i: (0, 0)),
            pl.BlockSpec((N_EXPERTS, 1), lambda i: (0, 0)),
        ],
        out_specs=[
            pl.BlockSpec((TR, D), lambda i: (i, 0)),
            kt_spec, kt_spec, kt_spec,
            pl.BlockSpec((N_EXPERTS, 1), lambda i: (0, 0)),
        ],
        out_shape=[
            jax.ShapeDtypeStruct((T, D), F32), kt(I32), kt(F32), kt(I32),
            jax.ShapeDtypeStruct((N_EXPERTS, 1), F32),
        ],
        scratch_shapes=[pltpu.VMEM((N_EXPERTS, 1), F32)],
        compiler_params=_params(("arbitrary",)),
        name="moe_router",
    )(x, mod, g, wr_t, e_bias)


SUBLANES = 8
_PAD_BITS = tuple(1 << b for b in range(3, GM.bit_length() - 1))


def _dispatch_kernel(pad_start_ref, pad_len_ref, slot_ref, h_ref, xg_ref, zero_ref, sem):
    i = pl.program_id(0)

    @pl.when(i == 0)
    def _():
        zero_ref[...] = jnp.zeros_like(zero_ref)

        def pad_copies(e):
            start = pad_start_ref[e]
            n = pad_len_ref[e]
            n_head = jnp.minimum((-start) & (SUBLANES - 1), n)
            body = start + n_head
            m = n - n_head
            copies = []
            for k in range(SUBLANES - 1):
                copies.append((k < n_head, pltpu.make_async_copy(
                    zero_ref.at[pl.ds(0, 1)], xg_ref.at[pl.ds(start + k, 1)], sem)))
            for bit in _PAD_BITS:
                lo = pl.multiple_of(body + (m & ~(2 * bit - 1)), SUBLANES)
                copies.append(((m & bit) != 0, pltpu.make_async_copy(
                    zero_ref.at[pl.ds(0, bit)], xg_ref.at[pl.ds(lo, bit)], sem)))
            return copies

        def start_e(e, carry):
            for on, cp in pad_copies(e):
                @pl.when(on)
                def _():
                    cp.start()
            return carry

        def wait_e(e, carry):
            for on, cp in pad_copies(e):
                @pl.when(on)
                def _():
                    cp.wait()
            return carry

        lax.fori_loop(0, N_EXPERTS, start_e, 0)
        lax.fori_loop(0, N_EXPERTS, wait_e, 0)

    def row_copy(t, k):
        return pltpu.make_async_copy(h_ref.at[pl.ds(t, 1)], xg_ref.at[pl.ds(slot_ref[k, t], 1)], sem)

    def start_t(t, carry):
        for k in range(TOP_K):
            row_copy(t, k).start()
        return carry

    def wait_t(t, carry):
        for k in range(TOP_K):
            row_copy(t, k).wait()
        return carry

    lax.fori_loop(0, TD, start_t, 0)
    lax.fori_loop(0, TD, wait_t, 0)


def _dispatch(pad_start, pad_len, slot, h):
    grid_spec = pltpu.PrefetchScalarGridSpec(
        num_scalar_prefetch=2,
        grid=(T // TD,),
        in_specs=[
            pl.BlockSpec((TOP_K, TD), lambda i, *_: (0, i), memory_space=pltpu.SMEM),
            pl.BlockSpec((TD, D), lambda i, *_: (i, 0)),
        ],
        out_specs=pl.BlockSpec(memory_space=pl.ANY),
        scratch_shapes=[pltpu.VMEM((GM, D), F32), pltpu.SemaphoreType.DMA],
    )
    return pl.pallas_call(
        _dispatch_kernel,
        grid_spec=grid_spec,
        out_shape=jax.ShapeDtypeStruct((SP, D), F32),
        compiler_params=_params(("arbitrary",)),
        name="moe_dispatch",
    )(pad_start, pad_len, slot, h)


def _gmm_kernel(tile_e_ref, tile_blk_ref, nvalid_ref, x_ref, wg_ref, wu_ref, wd_ref,
                y_ref, wgu_scr, wd_scr):
    j = pl.program_id(0)

    @pl.when(j < nvalid_ref[0])
    def _():
        prev = tile_e_ref[jnp.maximum(j - 1, 0)]

        @pl.when(jnp.logical_or(j == 0, tile_e_ref[j] != prev))
        def _():
            wgu_scr[:, 0:D_EXPERT] = wg_ref[0].astype(BF16)
            wgu_scr[:, D_EXPERT:2 * D_EXPERT] = wu_ref[0].astype(BF16)
            wd_scr[...] = wd_ref[0].astype(BF16)

        gu = jnp.dot(x_ref[...].astype(BF16), wgu_scr[...], preferred_element_type=F32)
        hid = _silu(gu[:, 0:D_EXPERT]) * gu[:, D_EXPERT:2 * D_EXPERT]
        y_ref[...] = jnp.dot(hid.astype(BF16), wd_scr[...], preferred_element_type=F32)


def _gmm(tile_e, tile_blk, nvalid, xg, w_gate, w_up, w_down):
    row_tile = pl.BlockSpec((GM, D), lambda j, te, tb, nv: (tb[j], 0))
    grid_spec = pltpu.PrefetchScalarGridSpec(
        num_scalar_prefetch=3,
        grid=(NT_MAX,),
        in_specs=[
            row_tile,
            pl.BlockSpec((1, D, D_EXPERT), lambda j, te, tb, nv: (te[j], 0, 0)),
            pl.BlockSpec((1, D, D_EXPERT), lambda j, te, tb, nv: (te[j], 0, 0)),
            pl.BlockSpec((1, D_EXPERT, D), lambda j, te, tb, nv: (te[j], 0, 0)),
        ],
        out_specs=row_tile,
        scratch_shapes=[pltpu.VMEM((D, 2 * D_EXPERT), BF16), pltpu.VMEM((D_EXPERT, D), BF16)],
    )
    return pl.pallas_call(
        _gmm_kernel,
        grid_spec=grid_spec,
        out_shape=jax.ShapeDtypeStruct((SP, D), F32),
        compiler_params=_params(("arbitrary",)),
        name="moe_grouped_matmul",
    )(tile_e, tile_blk, nvalid, xg, w_gate, w_up, w_down)


def _combine_kernel(slot_ref, wn_ref, h_ref, x_ref, mod_ref, gp_ref, wsg_ref, wsu_ref, wsd_ref,
                    yg_ref, o_ref, ybuf, sem):
    def row_copy(t, k):
        return pltpu.make_async_copy(yg_ref.at[pl.ds(slot_ref[k, t], 1)], ybuf.at[k, pl.ds(t, 1)], sem)

    def start_t(t, carry):
        for k in range(TOP_K):
            row_copy(t, k).start()
        return carry

    def wait_t(t, carry):
        for k in range(TOP_K):
            row_copy(t, k).wait()
        return carry

    lax.fori_loop(0, TD, start_t, 0)

    hb = h_ref[...].astype(BF16)
    hid = (_silu(jnp.dot(hb, wsg_ref[...], preferred_element_type=F32))
           * jnp.dot(hb, wsu_ref[...], preferred_element_type=F32))
    acc = jnp.dot(hid.astype(BF16), wsd_ref[...], preferred_element_type=F32)

    r = lax.broadcasted_iota(I32, (TD, TD), 0)
    c = lax.broadcasted_iota(I32, (TD, TD), 1)
    eye = jnp.where(r == c, 1.0, 0.0).astype(BF16)
    nt = lambda a, b: lax.dot_general(a, b, (((1,), (1,)), ((), ())), preferred_element_type=F32)
    w1, w2, w3 = _split3(wn_ref[...])
    w_t = nt(eye, w1) + nt(eye, w2) + nt(eye, w3)

    lax.fori_loop(0, TD, wait_t, 0)
    for k in range(TOP_K):
        acc = acc + ybuf[k] * w_t[:, k:k + 1]
    m = mod_ref[0, 0]
    o_ref[...] = x_ref[...] + m[5:6] * _rms(acc, gp_ref[...])


def _combine(slot, wn, h, x, mod, l, g_post, ws_gate, ws_up, ws_down, yg):
    tok = pl.BlockSpec((TD, D), lambda i: (i, 0))
    tiles_per_dec = DEC_SEQ // TD
    mod_row = lambda i: jnp.where(i < TP // TD, 0, 1 + (i - TP // TD) // tiles_per_dec)
    full = lambda *shape: pl.BlockSpec(shape, lambda i: (0,) * len(shape))
    return pl.pallas_call(
        _combine_kernel,
        grid=(T // TD,),
        in_specs=[
            pl.BlockSpec((TOP_K, TD), lambda i: (0, i), memory_space=pltpu.SMEM),
            pl.BlockSpec((TOP_K, TD), lambda i: (0, i)),
            tok, tok,
            pl.BlockSpec((1, 1, 6, D), lambda i: (l, mod_row(i), 0, 0)),
            full(1, D), full(D, D_SHARED), full(D, D_SHARED), full(D_SHARED, D),
            pl.BlockSpec(memory_space=pl.ANY),
        ],
        out_specs=tok,
        out_shape=jax.ShapeDtypeStruct((T, D), F32),
        scratch_shapes=[pltpu.VMEM((TOP_K, TD, D), F32), pltpu.SemaphoreType.DMA],
        compiler_params=_params(("arbitrary",)),
        name="moe_combine",
    )(slot, wn, h, x, mod, g_post, ws_gate, ws_up, ws_down, yg)


def _moe_layer(x, mod, l, g_pre, g_post, w_router, e_bias, w_gate, w_up, w_down,
               ws_gate, ws_up, ws_down):
    h, top_e, wn, rk, cnt = _router(x, mod, l, g_pre, w_router.T, e_bias.reshape(N_EXPERTS, 1))
    cnt = cnt.reshape(N_EXPERTS).astype(I32)
    padded = (cnt + GM - 1) // GM * GM
    ends = jnp.cumsum(padded)
    offs = ends - padded
    slot = offs[top_e] + rk
    nvalid = ends[-1] // GM
    tile_start = jnp.arange(NT_MAX, dtype=I32) * GM
    tile_raw = jnp.sum((tile_start[:, None] >= ends[None, :]).astype(I32), axis=1)
    last = jnp.maximum(nvalid - 1, 0)
    tile_blk = jnp.minimum(jnp.arange(NT_MAX, dtype=I32), last)
    tile_e = jnp.minimum(tile_raw, N_EXPERTS - 1)
    tile_e = jnp.where(jnp.arange(NT_MAX) <= last, tile_e, tile_e[last])
    xg = _dispatch(offs + cnt, padded - cnt, slot, h)
    yg = _gmm(tile_e, tile_blk, nvalid.reshape(1), xg, w_gate, w_up, w_down)
    return _combine(slot, wn, h, x, mod, l, g_post,
                    ws_gate.astype(BF16), ws_up.astype(BF16), ws_down.astype(BF16), yg)


def _rope_tables():
    n = DEC_SEQ
    rows = n // GRID_W
    row = jnp.repeat(jnp.arange(rows), GRID_W).astype(F32)
    col = jnp.tile(jnp.arange(GRID_W), rows).astype(F32)
    half = DQK_B // 2
    inv = ROPE_BASE ** (-jnp.arange(0, half, 2, dtype=F32) / half)
    ang_r = row[:, None] * inv
    ang_c = col[:, None] * inv
    ang = jnp.concatenate([ang_r, ang_r, ang_c, ang_c], axis=-1)
    reps = QK_B // DQK_B
    return jnp.tile(jnp.cos(ang), (1, reps)), jnp.tile(jnp.sin(ang), (1, reps))


def _pad_in_proj(w):
    s = [0, Q_A, 2 * Q_A, 2 * Q_A + V_A, 2 * Q_A + 2 * V_A]
    s += [s[-1] + GATE_RANK, s[-1] + 2 * GATE_RANK]
    s += [s[-1] + QK_B, s[-1] + 2 * QK_B, s[-1] + 2 * QK_B + V_B]
    gates = jnp.pad(w[:, s[4]:s[6]], ((0, 0), (0, GL_PAD - 2 * GATE_RANK)))
    return jnp.concatenate([w[:, s[0]:s[4]], gates, w[:, s[6]:s[9]]], axis=1).astype(BF16)


def kernel(x_prompt, x_sample, c, c_ctx, state_gla, cache_k, cache_v, ada_w, ada_b, norm_pre_mix, norm_post_mix, norm_pre_ffn, norm_post_ffn, ab_w_in, gla_w_g2, gla_b_g2, gla_norm_g, diff_lambda, diff_norm_g, ab_w_out, sgu_w_in, sgu_b_in, sgu_norm_g, sgu_w_s, sgu_b_s, sgu_w_out, moe_w_router, moe_e_bias, moe_w_gate, moe_w_up, moe_w_down, moe_ws_gate, moe_ws_up, moe_ws_down):
    depth = ada_w.shape[0]
    x = jnp.concatenate([x_prompt.reshape(TP, D), x_sample.reshape(TS, D)], axis=0)
    cond = jnp.concatenate([c_ctx[None, :], c, jnp.zeros((8 - 1 - DEC_BATCH, D), F32)], axis=0)
    mod = _modulation(cond, ada_w, ada_b)
    cos, sin = _rope_tables()
    vec = lambda a: a.reshape(1, -1)
    new_s = new_k = new_v = None
    for l in range(depth):
        if l % 2 == 0:
            e = l // 2
            lam_init = 0.8 - 0.6 * math.exp(-0.3 * l)
            a, r_a, gl, q_b, k_b, v_b = _in_proj(x, mod, l, vec(norm_pre_mix[l]), _pad_in_proj(ab_w_in[e]), cos, sin)
            s0_t = jnp.swapaxes(state_gla[:, e], -1, -2)
            o_f, o_bw, s_fin_t = _gla(a, gl, gla_w_g2[e], gla_b_g2[e].reshape(2, 1, Q_A), s0_t)
            o_att = _diff_attention(q_b, k_b, v_b, cache_k, cache_v, diff_lambda[e], lam_init)
            x = _mix_out(lam_init, o_f, o_bw, r_a, o_att, x, mod, l, vec(gla_norm_g[e]), vec(diff_norm_g[e]),
                         ab_w_out[e].astype(BF16), vec(norm_post_mix[l]))
            new_s = jnp.swapaxes(s_fin_t, -1, -2)
            new_k = k_b[:TP].reshape(BATCH, SEQ, H_B, 2, DQK_B).transpose(0, 2, 3, 1, 4)
            new_v = v_b[:TP].reshape(BATCH, SEQ, H_B, DV_B).transpose(0, 2, 1, 3)
        else:
            o = l // 2
            x = _sgu(x, mod, l, vec(norm_pre_mix[l]), sgu_w_in[o].astype(BF16), vec(sgu_b_in[o]),
                     vec(sgu_norm_g[o]), sgu_w_s[o], sgu_b_s[o].T, sgu_w_out[o].astype(BF16),
                     vec(norm_post_mix[l]))
        x = _moe_layer(x, mod, l, vec(norm_pre_ffn[l]), vec(norm_post_ffn[l]), moe_w_router[l], moe_e_bias[l],
                       moe_w_gate[l], moe_w_up[l], moe_w_down[l], moe_ws_gate[l], moe_ws_up[l], moe_ws_down[l])
    y_prompt = x[:TP].reshape(BATCH, SEQ, D)
    y_sample = x[TP:].reshape(DEC_BATCH, DEC_SEQ, D)
    return (y_prompt, y_sample, new_s[:, None], new_k[:, None], new_v[:, None])
```

```python
import functools
import math

import jax
import jax.numpy as jnp
from jax import lax
from jax.experimental import pallas as pl
from jax.experimental.pallas import tpu as pltpu

F32 = jnp.float32
BF16 = jnp.bfloat16
I32 = jnp.int32

D = 1024
BATCH, SEQ = 32, 256
DEC_BATCH, DEC_SEQ = 4, 2048
PAST_LEN = 256
GRID_W = 64
EPS = 1e-6
TP = BATCH * SEQ
TS = DEC_BATCH * DEC_SEQ
T = TP + TS
H_A, DK_A, DV_A = 4, 64, 128
Q_A, V_A = H_A * DK_A, H_A * DV_A
GATE_RANK, GATE_TAU, GLA_CHUNK = 16, 16.0, 64
H_B, DQK_B, DV_B = 4, 64, 128
QK_B, V_B = H_B * 2 * DQK_B, H_B * DV_B
ROPE_BASE = 10000.0
SGU_DIM, SGU_GROUPS, SGU_CHUNK = 1024, 4, 128
N_EXPERTS, TOP_K, N_GROUPS, TOPK_GROUPS = 64, 8, 8, 4
GROUP_SIZE = N_EXPERTS // N_GROUPS
D_EXPERT, D_SHARED = 256, 256
ROUTED_SCALE = 2.5

TM = 512
NPT = TP // TM
TILES_PER_DEC = DEC_SEQ // TM
SEG = 256
NSEG = T // SEG
NSEG_P = TP // SEG
SEG_PER_DEC = DEC_SEQ // SEG
TR = 512
TD = 256
GM = 256
NT_MAX = T * TOP_K // GM + N_EXPERTS
SP = NT_MAX * GM
GL_PAD = 128
VMEM_LIMIT = 56 * 1024 * 1024
NEG_INF = float("-inf")


def _bdot(a, b):
    return jnp.dot(a.astype(BF16), b.astype(BF16), preferred_element_type=F32)


def _bdot_nt(a, b):
    return lax.dot_general(a.astype(BF16), b.astype(BF16), (((1,), (1,)), ((), ())),
                           preferred_element_type=F32)


def _bdot_tn(a, b):
    return lax.dot_general(a.astype(BF16), b.astype(BF16), (((0,), (0,)), ((), ())),
                           preferred_element_type=F32)


def _split3(x):
    x1 = x.astype(BF16)
    r1 = x - x1.astype(F32)
    x2 = r1.astype(BF16)
    x3 = (r1 - x2.astype(F32)).astype(BF16)
    return x1, x2, x3


def _rms(x, g):
    return x * lax.rsqrt(jnp.mean(x * x, axis=-1, keepdims=True) + EPS) * g


def _silu(x):
    return x * jax.nn.sigmoid(x)


def _mod_row(i):
    return jnp.where(i < NPT, 0, 1 + (i - NPT) // TILES_PER_DEC)


def _params(sem, limit=VMEM_LIMIT):
    return pltpu.CompilerParams(dimension_semantics=sem, vmem_limit_bytes=limit)


def _mod_kernel(c_ref, w_ref, b_ref, o_ref):
    o_ref[0] = _bdot(_silu(c_ref[...]), w_ref[0]) + b_ref[0]


def _modulation(cond, ada_w, ada_b):
    depth = ada_w.shape[0]
    nj = 6
    out = pl.pallas_call(
        _mod_kernel,
        grid=(depth, nj),
        in_specs=[
            pl.BlockSpec((8, D), lambda l, j: (0, 0)),
            pl.BlockSpec((1, D, D), lambda l, j: (l, 0, j)),
            pl.BlockSpec((1, 1, D), lambda l, j: (l, 0, j)),
        ],
        out_specs=pl.BlockSpec((1, 8, D), lambda l, j: (l, 0, j)),
        out_shape=jax.ShapeDtypeStruct((depth, 8, 6 * D), F32),
        compiler_params=_params(("arbitrary", "arbitrary")),
        name="adaln_modulation",
    )(cond, ada_w, ada_b.reshape(depth, 1, 6 * D))
    return out.reshape(depth, 8, 6, D)


_C_A, _C_R, _C_GL, _C_Q, _C_K, _C_V, _C_END = 0, 1024, 1536, 1664, 2176, 2688, 3200


def _rope(x, cos, sin):
    lane = lax.broadcasted_iota(I32, x.shape, 1)
    first = (lane % 32) < 16
    n = x.shape[1]
    xr = jnp.where(first, -pltpu.roll(x, n - 16, 1), pltpu.roll(x, 16, 1))
    return x * cos + xr * sin


def _in_kernel(x_ref, mod_ref, g_ref, w_ref, cos_ref, sin_ref,
               a_ref, r_ref, gl_ref, q_ref, k_ref, v_ref):
    i = pl.program_id(0)
    m = mod_ref[0, 0]
    h = _rms(x_ref[...], g_ref[...]) * (1.0 + m[1:2]) + m[0:1]
    hb = h.astype(BF16)

    def proj(c0, c1):
        return jnp.dot(hb, w_ref[:, c0:c1], preferred_element_type=F32)

    a_ref[...] = proj(_C_A, _C_R)
    r_ref[...] = proj(_C_R, _C_GL)
    gl_ref[...] = proj(_C_GL, _C_Q)
    v_ref[...] = proj(_C_V, _C_END)
    q = proj(_C_Q, _C_K)
    k = proj(_C_K, _C_V)

    @pl.when(i < NPT)
    def _():
        q_ref[...] = q
        k_ref[...] = k

    @pl.when(i >= NPT)
    def _():
        cos = cos_ref[...]
        sin = sin_ref[...]
        q_ref[...] = _rope(q, cos, sin)
        k_ref[...] = _rope(k, cos, sin)


def _in_proj(x, mod, l, g, w_pad, cos, sin):
    tok = lambda width: pl.BlockSpec((TM, width), lambda i: (i, 0))
    rope_spec = pl.BlockSpec((TM, QK_B), lambda i: (jnp.maximum(i - NPT, 0) % TILES_PER_DEC, 0))
    widths = (1024, 512, GL_PAD, 512, 512, 512)
    return pl.pallas_call(
        _in_kernel,
        grid=(T // TM,),
        in_specs=[
            tok(D),
            pl.BlockSpec((1, 1, 6, D), lambda i: (l, _mod_row(i), 0, 0)),
            pl.BlockSpec((1, D), lambda i: (0, 0)),
            pl.BlockSpec((D, _C_END), lambda i: (0, 0)),
            rope_spec, rope_spec,
        ],
        out_specs=[tok(w) for w in widths],
        out_shape=[jax.ShapeDtypeStruct((T, w), F32) for w in widths],
        compiler_params=_params(("arbitrary",)),
        name="mixer_ab_in_proj",
    )(x, mod, g, w_pad, cos, sin)


def _log_sigmoid(x):
    return jnp.minimum(x, 0.0) - jnp.log1p(jnp.exp(-jnp.abs(x)))


def _gla_kernel(af_ref, ab_ref, glf_ref, glb_ref, wg_ref, bg_ref, s0_ref,
                of_ref, ob_ref, sfin_ref, st_ref):
    i = pl.program_id(0)

    @pl.when(i < NSEG_P)
    def _():
        st_ref[...] = jnp.zeros_like(st_ref)

    @pl.when(jnp.logical_and(i >= NSEG_P, (i - NSEG_P) % SEG_PER_DEC == 0))
    def _():
        st_ref[...] = s0_ref[0]

    r = lax.broadcasted_iota(I32, (SEG, SEG), 0)
    c = lax.broadcasted_iota(I32, (SEG, SEG), 1)
    same = (r // GLA_CHUNK) == (c // GLA_CHUNK)
    rc = lax.broadcasted_iota(I32, (GLA_CHUNK, GLA_CHUNK), 0)
    cc = lax.broadcasted_iota(I32, (GLA_CHUNK, GLA_CHUNK), 1)
    nchunk = SEG // GLA_CHUNK

    for d, (a_ref, gl_ref, o_ref) in enumerate(((af_ref, glf_ref, of_ref), (ab_ref, glb_ref, ob_ref))):
        fwd = d == 0
        gcol = gl_ref[:, d * GATE_RANK:(d + 1) * GATE_RANK]
        la = _log_sigmoid(_bdot(gcol, wg_ref[d]) + bg_ref[d]) / GATE_TAU
        tri = jnp.where(jnp.logical_and(same, (c <= r) if fwd else (c >= r)), 1.0, 0.0).astype(BF16)
        l1, l2, l3 = _split3(la)
        b_all = (jnp.dot(tri, l1, preferred_element_type=F32)
                 + jnp.dot(tri, l2, preferred_element_type=F32)
                 + jnp.dot(tri, l3, preferred_element_type=F32))
        keep = (cc <= rc) if fwd else (cc >= rc)
        order = range(nchunk) if fwd else range(nchunk - 1, -1, -1)
        for ch in order:
            r0 = ch * GLA_CHUNK
            rows = slice(r0, r0 + GLA_CHUNK)
            for h in range(H_A):
                kc = slice(h * DK_A, (h + 1) * DK_A)
                q = a_ref[rows, h * DK_A:(h + 1) * DK_A] * (DK_A ** -0.5)
                k = a_ref[rows, Q_A + h * DK_A:Q_A + (h + 1) * DK_A]
                v = a_ref[rows, 2 * Q_A + h * DV_A:2 * Q_A + (h + 1) * DV_A]
                b = b_all[rows, kc]
                b_end = b[GLA_CHUNK - 1:GLA_CHUNK, :] if fwd else b[0:1, :]
                q_in = q * jnp.exp(b)
                attn = jnp.where(keep, _bdot_nt(q_in, k * jnp.exp(-b)), 0.0)
                s_t = st_ref[d, h]
                o_ref[rows, h * DV_A:(h + 1) * DV_A] = _bdot(attn, v) + _bdot_nt(q_in, s_t)
                st_ref[d, h] = s_t * jnp.exp(b_end) + _bdot_tn(v, k * jnp.exp(b_end - b))

    @pl.when(i < NSEG_P)
    def _():
        sfin_ref[0] = st_ref[...]


def _seg_bwd(i):
    j = i - NSEG_P
    return jnp.where(i < NSEG_P, i, NSEG_P + (j // SEG_PER_DEC) * SEG_PER_DEC + (SEG_PER_DEC - 1 - j % SEG_PER_DEC))


def _gla(a, gl, wg, bg, s0_t):
    seg = lambda width, f: pl.BlockSpec((SEG, width), lambda i: (f(i), 0))
    ident = lambda i: i
    st_block = (1, 2, H_A, DV_A, DK_A)
    return pl.pallas_call(
        _gla_kernel,
        grid=(NSEG,),
        in_specs=[
            seg(D, ident), seg(D, _seg_bwd), seg(GL_PAD, ident), seg(GL_PAD, _seg_bwd),
            pl.BlockSpec((2, GATE_RANK, Q_A), lambda i: (0, 0, 0)),
            pl.BlockSpec((2, 1, Q_A), lambda i: (0, 0, 0)),
            pl.BlockSpec(st_block, lambda i: (jnp.maximum(i - NSEG_P, 0) // SEG_PER_DEC, 0, 0, 0, 0)),
        ],
        out_specs=[
            seg(V_A, ident), seg(V_A, _seg_bwd),
            pl.BlockSpec(st_block, lambda i: (jnp.minimum(i, NSEG_P - 1), 0, 0, 0, 0)),
        ],
        out_shape=[
            jax.ShapeDtypeStruct((T, V_A), F32),
            jax.ShapeDtypeStruct((T, V_A), F32),
            jax.ShapeDtypeStruct((BATCH, 2, H_A, DV_A, DK_A), F32),
        ],
        scratch_shapes=[pltpu.VMEM((2, H_A, DV_A, DK_A), F32)],
        compiler_params=_params(("arbitrary",)),
        name="gla_bidir",
    )(a, a, gl, gl, wg, bg, s0_t)


def _diff_lambda(lam_ref, lam_init):
    lp = lam_ref[...]
    s01 = jnp.sum(lp[0:1] * lp[1:2], axis=1, keepdims=True)
    s23 = jnp.sum(lp[2:3] * lp[3:4], axis=1, keepdims=True)
    return jnp.exp(s01) - jnp.exp(s23) + lam_init


def _attn_prompt_kernel(lam_init, q_ref, k_ref, v_ref, lam_ref, o_ref):
    lam = _diff_lambda(lam_ref, lam_init)
    scale = DQK_B ** -0.5
    for h in range(H_B):
        ps = []
        for m in range(2):
            cols = slice((2 * h + m) * DQK_B, (2 * h + m + 1) * DQK_B)
            s = _bdot_nt(q_ref[:, cols], k_ref[:, cols]) * scale
            e = jnp.exp(s - jnp.max(s, axis=1, keepdims=True))
            ps.append(e / jnp.sum(e, axis=1, keepdims=True))
        w = ps[0] - lam * ps[1]
        o_ref[:, h * DV_B:(h + 1) * DV_B] = _bdot(w, v_ref[:, h * DV_B:(h + 1) * DV_B])


def _attn_sample_kernel(lam_init, q_ref, k_ref, v_ref, ck_ref, cv_ref, lam_ref, o_ref):
    lam = _diff_lambda(lam_ref, lam_init)
    scale = DQK_B ** -0.5
    for h in range(H_B):
        parts = []
        for m in range(2):
            cols = slice((2 * h + m) * DQK_B, (2 * h + m + 1) * DQK_B)
            q = q_ref[:, cols]
            sc = _bdot_nt(q, ck_ref[0, 0, h, m]) * scale
            sn = _bdot_nt(q, k_ref[:, cols]) * scale
            mx = jnp.maximum(jnp.max(sc, axis=1, keepdims=True), jnp.max(sn, axis=1, keepdims=True))
            ec = jnp.exp(sc - mx)
            en = jnp.exp(sn - mx)
            den = jnp.sum(ec, axis=1, keepdims=True) + jnp.sum(en, axis=1, keepdims=True)
            parts.append((ec / den, en / den))
        wc = parts[0][0] - lam * parts[1][0]
        wn = parts[0][1] - lam * parts[1][1]
        o_ref[:, h * DV_B:(h + 1) * DV_B] = (_bdot(wc, cv_ref[0, 0, h])
                                             + _bdot(wn, v_ref[:, h * DV_B:(h + 1) * DV_B]))


QB = SEQ
NQB_DEC = DEC_SEQ // QB


def _attn_kernel(lam_init, q_ref, kp_ref, vp_ref, ks_ref, vs_ref, ck_ref, cv_ref, lam_ref, o_ref):
    i = pl.program_id(0)

    @pl.when(i < BATCH)
    def _():
        _attn_prompt_kernel(lam_init, q_ref, kp_ref, vp_ref, lam_ref, o_ref)

    @pl.when(i >= BATCH)
    def _():
        _attn_sample_kernel(lam_init, q_ref, ks_ref, vs_ref, ck_ref, cv_ref, lam_ref, o_ref)


def _diff_attention(q, k, v, cache_k, cache_v, lam_p, lam_init):
    blk = lambda rows, f: pl.BlockSpec((rows, 512), f)
    dec_b = lambda i: jnp.maximum(i - BATCH, 0) // NQB_DEC
    own = lambda i: (i, 0)
    prompt_kv = lambda i: (jnp.minimum(i, BATCH - 1), 0)
    dec_kv = lambda i: (TP // DEC_SEQ + dec_b(i), 0)
    return pl.pallas_call(
        functools.partial(_attn_kernel, lam_init),
        grid=(BATCH + DEC_BATCH * NQB_DEC,),
        in_specs=[
            blk(QB, own), blk(SEQ, prompt_kv), blk(SEQ, prompt_kv), blk(DEC_SEQ, dec_kv), blk(DEC_SEQ, dec_kv),
            pl.BlockSpec((1, 1, H_B, 2, PAST_LEN, DQK_B), lambda i: (dec_b(i), 0, 0, 0, 0, 0)),
            pl.BlockSpec((1, 1, H_B, PAST_LEN, DV_B), lambda i: (dec_b(i), 0, 0, 0, 0)),
            pl.BlockSpec((4, DQK_B), lambda i: (0, 0)),
        ],
        out_specs=blk(QB, own),
        out_shape=jax.ShapeDtypeStruct((T, V_B), F32),
        compiler_params=_params(("arbitrary",)),
        name="diff_attention",
    )(q, k, v, k, v, cache_k, cache_v, lam_p)


def _head_rms(x, g, nheads, width):
    return jnp.concatenate([_rms(x[:, h * width:(h + 1) * width], g) for h in range(nheads)], axis=1)


def _mix_out_kernel(lam_init, of_ref, ob_ref, r_ref, oatt_ref, x_ref, mod_ref,
                    gg_ref, dg_ref, wo_ref, gp_ref, o_ref):
    m = mod_ref[0, 0]
    o_a = _head_rms(of_ref[...] + ob_ref[...], gg_ref[...], H_A, DV_A) * _silu(r_ref[...])
    o_b = _head_rms(oatt_ref[...], dg_ref[...], H_B, DV_B) * (1.0 - lam_init)
    out = _bdot(o_a, wo_ref[0:V_A, :]) + _bdot(o_b, wo_ref[V_A:V_A + V_B, :])
    o_ref[...] = x_ref[...] + m[2:3] * _rms(out, gp_ref[...])


def _mix_out(lam_init, o_f, o_b, r_a, o_att, x, mod, l, gla_g, diff_g, w_o, g_post):
    tok = lambda width: pl.BlockSpec((TM, width), lambda i: (i, 0))
    vec = lambda width: pl.BlockSpec((1, width), lambda i: (0, 0))
    return pl.pallas_call(
        functools.partial(_mix_out_kernel, lam_init),
        grid=(T // TM,),
        in_specs=[
            tok(512), tok(512), tok(512), tok(512), tok(D),
            pl.BlockSpec((1, 1, 6, D), lambda i: (l, _mod_row(i), 0, 0)),
            vec(DV_A), vec(DV_B),
            pl.BlockSpec((V_A + V_B, D), lambda i: (0, 0)),
            vec(D),
        ],
        out_specs=tok(D),
        out_shape=jax.ShapeDtypeStruct((T, D), F32),
        compiler_params=_params(("arbitrary",)),
        name="mixer_ab_out",
    )(o_f, o_b, r_a, o_att, x, mod, gla_g, diff_g, w_o, g_post)


def _gelu_tanh(x):
    return 0.5 * x * (1.0 + jnp.tanh(math.sqrt(2.0 / math.pi) * (x + 0.044715 * (x * x * x))))


def _sgu_kernel(x_ref, mod_ref, gpre_ref, win_ref, bin_ref, vg_ref, ws_ref, bs_ref,
                wout_ref, gpost_ref, o_ref, t_ref):
    m = mod_ref[0, 0]
    x = x_ref[...]
    h = _rms(x, gpre_ref[...]) * (1.0 + m[1:2]) + m[0:1]
    z = _gelu_tanh(_bdot(h, win_ref[...]) + bin_ref[...])
    v = _rms(z[:, SGU_DIM:], vg_ref[...])
    gw = SGU_DIM // SGU_GROUPS
    for ch in range(TM // SGU_CHUNK):
        rows = slice(ch * SGU_CHUNK, (ch + 1) * SGU_CHUNK)
        for g in range(SGU_GROUPS):
            cols = slice(g * gw, (g + 1) * gw)
            vs = _bdot(ws_ref[g], v[rows, cols]) + bs_ref[:, g:g + 1]
            t_ref[rows, cols] = (z[rows, cols] * vs).astype(BF16)
    out = jnp.dot(t_ref[...], wout_ref[...], preferred_element_type=F32)
    o_ref[...] = x + m[2:3] * _rms(out, gpost_ref[...])


def _sgu(x, mod, l, g_pre, w_in, b_in, v_g, w_s, b_s_t, w_out, g_post):
    tok = pl.BlockSpec((TM, D), lambda i: (i, 0))
    full = lambda *shape: pl.BlockSpec(shape, lambda i: (0,) * len(shape))
    return pl.pallas_call(
        _sgu_kernel,
        grid=(T // TM,),
        in_specs=[
            tok,
            pl.BlockSpec((1, 1, 6, D), lambda i: (l, _mod_row(i), 0, 0)),
            full(1, D), full(D, 2 * SGU_DIM), full(1, 2 * SGU_DIM), full(1, SGU_DIM),
            full(SGU_GROUPS, SGU_CHUNK, SGU_CHUNK), full(SGU_CHUNK, SGU_GROUPS),
            full(SGU_DIM, D), full(1, D),
        ],
        out_specs=tok,
        out_shape=jax.ShapeDtypeStruct((T, D), F32),
        scratch_shapes=[pltpu.VMEM((TM, SGU_DIM), BF16)],
        compiler_params=_params(("arbitrary",)),
        name="sgu_mixer",
    )(x, mod, g_pre, w_in, b_in, v_g, w_s, b_s_t, w_out, g_post)


def _router_kernel(x_ref, mod_ref, g_ref, wr_ref, eb_ref,
                   h_ref, te_ref, wn_ref, rk_ref, cnt_ref, carry_ref):
    i = pl.program_id(0)

    @pl.when(i == 0)
    def _():
        carry_ref[...] = jnp.zeros_like(carry_ref)

    m = mod_ref[0, 0]
    h = _rms(x_ref[...], g_ref[...]) * (1.0 + m[4:5]) + m[3:4]
    h_ref[...] = h
    h1, h2, _ = _split3(h)
    w1, w2, _ = _split3(wr_ref[...])
    nt = lambda a, b: lax.dot_general(a, b, (((1,), (1,)), ((), ())), preferred_element_type=F32)
    logits = nt(w1, h1) + nt(w1, h2) + nt(w2, h1)
    scores = jax.nn.sigmoid(logits)
    sel = scores + eb_ref[...]

    row8 = lax.broadcasted_iota(I32, (GROUP_SIZE, TR), 0)
    gscore = []
    for g in range(N_GROUPS):
        xg = sel[g * GROUP_SIZE:(g + 1) * GROUP_SIZE]
        m1 = jnp.max(xg, axis=0, keepdims=True)
        i1 = jnp.min(jnp.where(xg == m1, row8, GROUP_SIZE), axis=0, keepdims=True)
        m2 = jnp.max(jnp.where(row8 == i1, NEG_INF, xg), axis=0, keepdims=True)
        gscore.append(m1 + m2)
    pieces = []
    for g in range(N_GROUPS):
        rank = jnp.zeros((1, TR), I32)
        for g2 in range(N_GROUPS):
            if g2 == g:
                continue
            beats = (gscore[g2] >= gscore[g]) if g2 < g else (gscore[g2] > gscore[g])
            rank = rank + beats.astype(I32)
        pieces.append(jnp.where(rank < TOPK_GROUPS, sel[g * GROUP_SIZE:(g + 1) * GROUP_SIZE], NEG_INF))
    cur = jnp.concatenate(pieces, axis=0)

    row = lax.broadcasted_iota(I32, (N_EXPERTS, TR), 0)
    idxs, ws = [], []
    for _ in range(TOP_K):
        mx = jnp.max(cur, axis=0, keepdims=True)
        idx = jnp.min(jnp.where(cur == mx, row, N_EXPERTS), axis=0, keepdims=True)
        hit = row == idx
        ws.append(jnp.sum(jnp.where(hit, scores, 0.0), axis=0, keepdims=True))
        cur = jnp.where(hit, NEG_INF, cur)
        idxs.append(idx)
    mask = jnp.zeros((N_EXPERTS, TR), F32)
    for idx in idxs:
        mask = mask + (row == idx).astype(F32)
    wsum = ws[0]
    for wk in ws[1:]:
        wsum = wsum + wk

    tj = lax.broadcasted_iota(I32, (TR, TR), 0)
    ti = lax.broadcasted_iota(I32, (TR, TR), 1)
    upper = jnp.where(tj < ti, 1.0, 0.0).astype(BF16)
    pos = carry_ref[...] + jnp.dot(mask.astype(BF16), upper, preferred_element_type=F32)
    for k in range(TOP_K):
        hit = row == idxs[k]
        te_ref[k:k + 1, :] = idxs[k]
        wn_ref[k:k + 1, :] = ws[k] / wsum * ROUTED_SCALE
        rk_ref[k:k + 1, :] = jnp.sum(jnp.where(hit, pos, 0.0), axis=0, keepdims=True).astype(I32)
    carry_ref[...] = carry_ref[...] + jnp.sum(mask, axis=1, keepdims=True)
    cnt_ref[...] = carry_ref[...]


def _router(x, mod, l, g, wr_t, e_bias):
    kt = lambda dtype: jax.ShapeDtypeStruct((TOP_K, T), dtype)
    kt_spec = pl.BlockSpec((TOP_K, TR), lambda i: (0, i))
    tiles_per_dec = DEC_SEQ // TR
    mod_row = lambda i: jnp.where(i < TP // TR, 0, 1 + (i - TP // TR) // tiles_per_dec)
    return pl.pallas_call(
        _router_kernel,
        grid=(T // TR,),
        in_specs=[
            pl.BlockSpec((TR, D), lambda i: (i, 0)),
            pl.BlockSpec((1, 1, 6, D), lambda i: (l, mod_row(i), 0, 0)),
            pl.BlockSpec((1, D), lambda i: (0, 0)),
            pl.BlockSpec((N_EXPERTS, D), lambda i: (0, 0)),
            pl.BlockSpec((N_EXPERTS, 1), lambda i: (0, 0)),
        ],
        out_specs=[
            pl.BlockSpec((TR, D), lambda i: (i, 0)),
            kt_spec, kt_spec, kt_spec,
            pl.BlockSpec((N_EXPERTS, 1), lambda i: (0, 0)),
        ],
        out_shape=[
            jax.ShapeDtypeStruct((T, D), F32), kt(I32), kt(F32), kt(I32),
            jax.ShapeDtypeStruct((N_EXPERTS, 1), F32),
        ],
        scratch_shapes=[pltpu.VMEM((N_EXPERTS, 1), F32)],
        compiler_params=_params(("arbitrary",)),
        name="moe_router",
    )(x, mod, g, wr_t, e_bias)


SUBLANES = 8
_PAD_BITS = tuple(1 << b for b in range(3, GM.bit_length() - 1))


def _dispatch_kernel(pad_start_ref, pad_len_ref, slot_ref, h_ref, xg_ref, zero_ref, sem):
    i = pl.program_id(0)

    @pl.when(i == 0)
    def _():
        zero_ref[...] = jnp.zeros_like(zero_ref)

        def pad_copies(e):
            start = pad_start_ref[e]
            n = pad_len_ref[e]
            n_head = jnp.minimum((-start) & (SUBLANES - 1), n)
            body = start + n_head
            m = n - n_head
            copies = []
            for k in range(SUBLANES - 1):
                copies.append((k < n_head, pltpu.make_async_copy(
                    zero_ref.at[pl.ds(0, 1)], xg_ref.at[pl.ds(start + k, 1)], sem)))
            for bit in _PAD_BITS:
                lo = pl.multiple_of(body + (m & ~(2 * bit - 1)), SUBLANES)
                copies.append(((m & bit) != 0, pltpu.make_async_copy(
                    zero_ref.at[pl.ds(0, bit)], xg_ref.at[pl.ds(lo, bit)], sem)))
            return copies

        def start_e(e, carry):
            for on, cp in pad_copies(e):
                @pl.when(on)
                def _():
                    cp.start()
            return carry

        def wait_e(e, carry):
            for on, cp in pad_copies(e):
                @pl.when(on)
                def _():
                    cp.wait()
            return carry

        lax.fori_loop(0, N_EXPERTS, start_e, 0)
        lax.fori_loop(0, N_EXPERTS, wait_e, 0)

    def row_copy(t, k):
        return pltpu.make_async_copy(h_ref.at[pl.ds(t, 1)], xg_ref.at[pl.ds(slot_ref[k, t], 1)], sem)

    def start_t(t, carry):
        for k in range(TOP_K):
            row_copy(t, k).start()
        return carry

    def wait_t(t, carry):
        for k in range(TOP_K):
            row_copy(t, k).wait()
        return carry

    lax.fori_loop(0, TD, start_t, 0)
    lax.fori_loop(0, TD, wait_t, 0)


def _dispatch(pad_start, pad_len, slot, h):
    grid_spec = pltpu.PrefetchScalarGridSpec(
        num_scalar_prefetch=2,
        grid=(T // TD,),
        in_specs=[
            pl.BlockSpec((TOP_K, TD), lambda i, *_: (0, i), memory_space=pltpu.SMEM),
            pl.BlockSpec((TD, D), lambda i, *_: (i, 0)),
        ],
        out_specs=pl.BlockSpec(memory_space=pl.ANY),
        scratch_shapes=[pltpu.VMEM((GM, D), F32), pltpu.SemaphoreType.DMA],
    )
    return pl.pallas_call(
        _dispatch_kernel,
        grid_spec=grid_spec,
        out_shape=jax.ShapeDtypeStruct((SP, D), F32),
        compiler_params=_params(("arbitrary",)),
        name="moe_dispatch",
    )(pad_start, pad_len, slot, h)


def _gmm_kernel(tile_e_ref, tile_blk_ref, nvalid_ref, x_ref, wg_ref, wu_ref, wd_ref,
                y_ref, wgu_scr, wd_scr):
    j = pl.program_id(0)

    @pl.when(j < nvalid_ref[0])
    def _():
        prev = tile_e_ref[jnp.maximum(j - 1, 0)]

        @pl.when(jnp.logical_or(j == 0, tile_e_ref[j] != prev))
        def _():
            wgu_scr[:, 0:D_EXPERT] = wg_ref[0].astype(BF16)
            wgu_scr[:, D_EXPERT:2 * D_EXPERT] = wu_ref[0].astype(BF16)
            wd_scr[...] = wd_ref[0].astype(BF16)

        gu = jnp.dot(x_ref[...].astype(BF16), wgu_scr[...], preferred_element_type=F32)
        hid = _silu(gu[:, 0:D_EXPERT]) * gu[:, D_EXPERT:2 * D_EXPERT]
        y_ref[...] = jnp.dot(hid.astype(BF16), wd_scr[...], preferred_element_type=F32)


def _gmm(tile_e, tile_blk, nvalid, xg, w_gate, w_up, w_down):
    row_tile = pl.BlockSpec((GM, D), lambda j, te, tb, nv: (tb[j], 0))
    grid_spec = pltpu.PrefetchScalarGridSpec(
        num_scalar_prefetch=3,
        grid=(NT_MAX,),
        in_specs=[
            row_tile,
            pl.BlockSpec((1, D, D_EXPERT), lambda j, te, tb, nv: (te[j], 0, 0)),
            pl.BlockSpec((1, D, D_EXPERT), lambda j, te, tb, nv: (te[j], 0, 0)),
            pl.BlockSpec((1, D_EXPERT, D), lambda j, te, tb, nv: (te[j], 0, 0)),
        ],
        out_specs=row_tile,
        scratch_shapes=[pltpu.VMEM((D, 2 * D_EXPERT), BF16), pltpu.VMEM((D_EXPERT, D), BF16)],
    )
    return pl.pallas_call(
        _gmm_kernel,
        grid_spec=grid_spec,
        out_shape=jax.ShapeDtypeStruct((SP, D), F32),
        compiler_params=_params(("arbitrary",)),
        name="moe_grouped_matmul",
    )(tile_e, tile_blk, nvalid, xg, w_gate, w_up, w_down)


def _combine_kernel(slot_ref, wn_ref, h_ref, x_ref, mod_ref, gp_ref, wsg_ref, wsu_ref, wsd_ref,
                    yg_ref, o_ref, ybuf, sem):
    def row_copy(t, k):
        return pltpu.make_async_copy(yg_ref.at[pl.ds(slot_ref[k, t], 1)], ybuf.at[k, pl.ds(t, 1)], sem)

    def start_t(t, carry):
        for k in range(TOP_K):
            row_copy(t, k).start()
        return carry

    def wait_t(t, carry):
        for k in range(TOP_K):
            row_copy(t, k).wait()
        return carry

    lax.fori_loop(0, TD, start_t, 0)

    hb = h_ref[...].astype(BF16)
    hid = (_silu(jnp.dot(hb, wsg_ref[...], preferred_element_type=F32))
           * jnp.dot(hb, wsu_ref[...], preferred_element_type=F32))
    acc = jnp.dot(hid.astype(BF16), wsd_ref[...], preferred_element_type=F32)

    r = lax.broadcasted_iota(I32, (TD, TD), 0)
    c = lax.broadcasted_iota(I32, (TD, TD), 1)
    eye = jnp.where(r == c, 1.0, 0.0).astype(BF16)
    nt = lambda a, b: lax.dot_general(a, b, (((1,), (1,)), ((), ())), preferred_element_type=F32)
    w1, w2, w3 = _split3(wn_ref[...])
    w_t = nt(eye, w1) + nt(eye, w2) + nt(eye, w3)

    lax.fori_loop(0, TD, wait_t, 0)
    for k in range(TOP_K):
        acc = acc + ybuf[k] * w_t[:, k:k + 1]
    m = mod_ref[0, 0]
    o_ref[...] = x_ref[...] + m[5:6] * _rms(acc, gp_ref[...])


def _combine(slot, wn, h, x, mod, l, g_post, ws_gate, ws_up, ws_down, yg):
    tok = pl.BlockSpec((TD, D), lambda i: (i, 0))
    tiles_per_dec = DEC_SEQ // TD
    mod_row = lambda i: jnp.where(i < TP // TD, 0, 1 + (i - TP // TD) // tiles_per_dec)
    full = lambda *shape: pl.BlockSpec(shape, lambda i: (0,) * len(shape))
    return pl.pallas_call(
        _combine_kernel,
        grid=(T // TD,),
        in_specs=[
            pl.BlockSpec((TOP_K, TD), lambda i: (0, i), memory_space=pltpu.SMEM),
            pl.BlockSpec((TOP_K, TD), lambda i: (0, i)),
            tok, tok,
            pl.BlockSpec((1, 1, 6, D), lambda i: (l, mod_row(i), 0, 0)),
            full(1, D), full(D, D_SHARED), full(D, D_SHARED), full(D_SHARED, D),
            pl.BlockSpec(memory_space=pl.ANY),
        ],
        out_specs=tok,
        out_shape=jax.ShapeDtypeStruct((T, D), F32),
        scratch_shapes=[pltpu.VMEM((TOP_K, TD, D), F32), pltpu.SemaphoreType.DMA],
        compiler_params=_params(("arbitrary",)),
        name="moe_combine",
    )(slot, wn, h, x, mod, g_post, ws_gate, ws_up, ws_down, yg)


def _moe_layer(x, mod, l, g_pre, g_post, w_router, e_bias, w_gate, w_up, w_down,
               ws_gate, ws_up, ws_down):
    h, top_e, wn, rk, cnt = _router(x, mod, l, g_pre, w_router.T, e_bias.reshape(N_EXPERTS, 1))
    cnt = cnt.reshape(N_EXPERTS).astype(I32)
    padded = (cnt + GM - 1) // GM * GM
    ends = jnp.cumsum(padded)
    offs = ends - padded
    eid = jnp.arange(N_EXPERTS, dtype=I32)[:, None, None]
    slot = rk + jnp.sum(jnp.where(top_e[None] == eid, offs[:, None, None], 0), axis=0)
    nvalid = ends[-1] // GM
    tile_start = jnp.arange(NT_MAX, dtype=I32) * GM
    tile_raw = jnp.sum((tile_start[:, None] >= ends[None, :]).astype(I32), axis=1)
    last = jnp.maximum(nvalid - 1, 0)
    tile_blk = jnp.minimum(jnp.arange(NT_MAX, dtype=I32), last)
    tile_e = jnp.minimum(tile_raw, N_EXPERTS - 1)
    tile_e = jnp.where(jnp.arange(NT_MAX) <= last, tile_e, tile_e[last])
    xg = _dispatch(offs + cnt, padded - cnt, slot, h)
    yg = _gmm(tile_e, tile_blk, nvalid.reshape(1), xg, w_gate, w_up, w_down)
    return _combine(slot, wn, h, x, mod, l, g_post,
                    ws_gate.astype(BF16), ws_up.astype(BF16), ws_down.astype(BF16), yg)


def _rope_tables():
    n = DEC_SEQ
    rows = n // GRID_W
    row = jnp.repeat(jnp.arange(rows), GRID_W).astype(F32)
    col = jnp.tile(jnp.arange(GRID_W), rows).astype(F32)
    half = DQK_B // 2
    inv = ROPE_BASE ** (-jnp.arange(0, half, 2, dtype=F32) / half)
    ang_r = row[:, None] * inv
    ang_c = col[:, None] * inv
    ang = jnp.concatenate([ang_r, ang_r, ang_c, ang_c], axis=-1)
    reps = QK_B // DQK_B
    return jnp.tile(jnp.cos(ang), (1, reps)), jnp.tile(jnp.sin(ang), (1, reps))


def _pad_in_proj(w):
    s = [0, Q_A, 2 * Q_A, 2 * Q_A + V_A, 2 * Q_A + 2 * V_A]
    s += [s[-1] + GATE_RANK, s[-1] + 2 * GATE_RANK]
    s += [s[-1] + QK_B, s[-1] + 2 * QK_B, s[-1] + 2 * QK_B + V_B]
    gates = jnp.pad(w[:, s[4]:s[6]], ((0, 0), (0, GL_PAD - 2 * GATE_RANK)))
    return jnp.concatenate([w[:, s[0]:s[4]], gates, w[:, s[6]:s[9]]], axis=1).astype(BF16)


def kernel(x_prompt, x_sample, c, c_ctx, state_gla, cache_k, cache_v, ada_w, ada_b, norm_pre_mix, norm_post_mix, norm_pre_ffn, norm_post_ffn, ab_w_in, gla_w_g2, gla_b_g2, gla_norm_g, diff_lambda, diff_norm_g, ab_w_out, sgu_w_in, sgu_b_in, sgu_norm_g, sgu_w_s, sgu_b_s, sgu_w_out, moe_w_router, moe_e_bias, moe_w_gate, moe_w_up, moe_w_down, moe_ws_gate, moe_ws_up, moe_ws_down):
    depth = ada_w.shape[0]
    x = jnp.concatenate([x_prompt.reshape(TP, D), x_sample.reshape(TS, D)], axis=0)
    cond = jnp.concatenate([c_ctx[None, :], c, jnp.zeros((8 - 1 - DEC_BATCH, D), F32)], axis=0)
    mod = _modulation(cond, ada_w, ada_b)
    cos, sin = _rope_tables()
    vec = lambda a: a.reshape(1, -1)
    new_s = new_k = new_v = None
    for l in range(depth):
        if l % 2 == 0:
            e = l // 2
            lam_init = 0.8 - 0.6 * math.exp(-0.3 * l)
            a, r_a, gl, q_b, k_b, v_b = _in_proj(x, mod, l, vec(norm_pre_mix[l]), _pad_in_proj(ab_w_in[e]), cos, sin)
            s0_t = jnp.swapaxes(state_gla[:, e], -1, -2)
            o_f, o_bw, s_fin_t = _gla(a, gl, gla_w_g2[e], gla_b_g2[e].reshape(2, 1, Q_A), s0_t)
            o_att = _diff_attention(q_b, k_b, v_b, cache_k, cache_v, diff_lambda[e], lam_init)
            x = _mix_out(lam_init, o_f, o_bw, r_a, o_att, x, mod, l, vec(gla_norm_g[e]), vec(diff_norm_g[e]),
                         ab_w_out[e].astype(BF16), vec(norm_post_mix[l]))
            new_s = jnp.swapaxes(s_fin_t, -1, -2)
            new_k = k_b[:TP].reshape(BATCH, SEQ, H_B, 2, DQK_B).transpose(0, 2, 3, 1, 4)
            new_v = v_b[:TP].reshape(BATCH, SEQ, H_B, DV_B).transpose(0, 2, 1, 3)
        else:
            o = l // 2
            x = _sgu(x, mod, l, vec(norm_pre_mix[l]), sgu_w_in[o].astype(BF16), vec(sgu_b_in[o]),
                     vec(sgu_norm_g[o]), sgu_w_s[o], sgu_b_s[o].T, sgu_w_out[o].astype(BF16),
                     vec(norm_post_mix[l]))
        x = _moe_layer(x, mod, l, vec(norm_pre_ffn[l]), vec(norm_post_ffn[l]), moe_w_router[l], moe_e_bias[l],
                       moe_w_gate[l], moe_w_up[l], moe_w_down[l], moe_ws_gate[l], moe_ws_up[l], moe_ws_down[l])
    y_prompt = x[:TP].reshape(BATCH, SEQ, D)
    y_sample = x[TP:].reshape(DEC_BATCH, DEC_SEQ, D)
    return (y_prompt, y_sample, new_s[:, None], new_k[:, None], new_v[:, None])
```

```python
import functools
import math

import jax
import jax.numpy as jnp
from jax import lax
from jax.experimental import pallas as pl
from jax.experimental.pallas import tpu as pltpu

F32 = jnp.float32
BF16 = jnp.bfloat16
I32 = jnp.int32

D = 1024
BATCH, SEQ = 32, 256
DEC_BATCH, DEC_SEQ = 4, 2048
PAST_LEN = 256
GRID_W = 64
EPS = 1e-6
TP = BATCH * SEQ
TS = DEC_BATCH * DEC_SEQ
T = TP + TS
H_A, DK_A, DV_A = 4, 64, 128
Q_A, V_A = H_A * DK_A, H_A * DV_A
GATE_RANK, GATE_TAU, GLA_CHUNK = 16, 16.0, 64
H_B, DQK_B, DV_B = 4, 64, 128
QK_B, V_B = H_B * 2 * DQK_B, H_B * DV_B
ROPE_BASE = 10000.0
SGU_DIM, SGU_GROUPS, SGU_CHUNK = 1024, 4, 128
N_EXPERTS, TOP_K, N_GROUPS, TOPK_GROUPS = 64, 8, 8, 4
GROUP_SIZE = N_EXPERTS // N_GROUPS
D_EXPERT, D_SHARED = 256, 256
ROUTED_SCALE = 2.5

TM = 512
NPT = TP // TM
TILES_PER_DEC = DEC_SEQ // TM
SEG = 256
NSEG = T // SEG
NSEG_P = TP // SEG
SEG_PER_DEC = DEC_SEQ // SEG
TR = 512
TD = 256
GM = 256
NT_MAX = T * TOP_K // GM + N_EXPERTS
SP = NT_MAX * GM
GL_PAD = 128
VMEM_LIMIT = 56 * 1024 * 1024
NEG_INF = float("-inf")


def _bdot(a, b):
    return jnp.dot(a.astype(BF16), b.astype(BF16), preferred_element_type=F32)


def _bdot_nt(a, b):
    return lax.dot_general(a.astype(BF16), b.astype(BF16), (((1,), (1,)), ((), ())),
                           preferred_element_type=F32)


def _bdot_tn(a, b):
    return lax.dot_general(a.astype(BF16), b.astype(BF16), (((0,), (0,)), ((), ())),
                           preferred_element_type=F32)


def _split3(x):
    x1 = x.astype(BF16)
    r1 = x - x1.astype(F32)
    x2 = r1.astype(BF16)
    x3 = (r1 - x2.astype(F32)).astype(BF16)
    return x1, x2, x3


def _rms(x, g):
    return x * lax.rsqrt(jnp.mean(x * x, axis=-1, keepdims=True) + EPS) * g


def _silu(x):
    return x * jax.nn.sigmoid(x)


def _mod_row(i):
    return jnp.where(i < NPT, 0, 1 + (i - NPT) // TILES_PER_DEC)


def _params(sem, limit=VMEM_LIMIT):
    return pltpu.CompilerParams(dimension_semantics=sem, vmem_limit_bytes=limit)


def _mod_kernel(c_ref, w_ref, b_ref, o_ref):
    o_ref[0] = _bdot(_silu(c_ref[...]), w_ref[0]) + b_ref[0]


def _modulation(cond, ada_w, ada_b):
    depth = ada_w.shape[0]
    nj = 6
    out = pl.pallas_call(
        _mod_kernel,
        grid=(depth, nj),
        in_specs=[
            pl.BlockSpec((8, D), lambda l, j: (0, 0)),
            pl.BlockSpec((1, D, D), lambda l, j: (l, 0, j)),
            pl.BlockSpec((1, 1, D), lambda l, j: (l, 0, j)),
        ],
        out_specs=pl.BlockSpec((1, 8, D), lambda l, j: (l, 0, j)),
        out_shape=jax.ShapeDtypeStruct((depth, 8, 6 * D), F32),
        compiler_params=_params(("arbitrary", "arbitrary")),
        name="adaln_modulation",
    )(cond, ada_w, ada_b.reshape(depth, 1, 6 * D))
    return out.reshape(depth, 8, 6, D)


_C_A, _C_R, _C_GL, _C_Q, _C_K, _C_V, _C_END = 0, 1024, 1536, 1664, 2176, 2688, 3200


def _rope(x, cos, sin):
    lane = lax.broadcasted_iota(I32, x.shape, 1)
    first = (lane % 32) < 16
    n = x.shape[1]
    xr = jnp.where(first, -pltpu.roll(x, n - 16, 1), pltpu.roll(x, 16, 1))
    return x * cos + xr * sin


def _in_kernel(x_ref, mod_ref, g_ref, w_ref, cos_ref, sin_ref,
               a_ref, r_ref, gl_ref, q_ref, k_ref, v_ref):
    i = pl.program_id(0)
    m = mod_ref[0, 0]
    h = _rms(x_ref[...], g_ref[...]) * (1.0 + m[1:2]) + m[0:1]
    hb = h.astype(BF16)

    def proj(c0, c1):
        return jnp.dot(hb, w_ref[:, c0:c1], preferred_element_type=F32)

    a_ref[...] = proj(_C_A, _C_R)
    r_ref[...] = proj(_C_R, _C_GL)
    gl_ref[...] = proj(_C_GL, _C_Q)
    v_ref[...] = proj(_C_V, _C_END)
    q = proj(_C_Q, _C_K)
    k = proj(_C_K, _C_V)

    @pl.when(i < NPT)
    def _():
        q_ref[...] = q
        k_ref[...] = k

    @pl.when(i >= NPT)
    def _():
        cos = cos_ref[...]
        sin = sin_ref[...]
        q_ref[...] = _rope(q, cos, sin)
        k_ref[...] = _rope(k, cos, sin)


def _in_proj(x, mod, l, g, w_pad, cos, sin):
    tok = lambda width: pl.BlockSpec((TM, width), lambda i: (i, 0))
    rope_spec = pl.BlockSpec((TM, QK_B), lambda i: (jnp.maximum(i - NPT, 0) % TILES_PER_DEC, 0))
    widths = (1024, 512, GL_PAD, 512, 512, 512)
    return pl.pallas_call(
        _in_kernel,
        grid=(T // TM,),
        in_specs=[
            tok(D),
            pl.BlockSpec((1, 1, 6, D), lambda i: (l, _mod_row(i), 0, 0)),
            pl.BlockSpec((1, D), lambda i: (0, 0)),
            pl.BlockSpec((D, _C_END), lambda i: (0, 0)),
            rope_spec, rope_spec,
        ],
        out_specs=[tok(w) for w in widths],
        out_shape=[jax.ShapeDtypeStruct((T, w), F32) for w in widths],
        compiler_params=_params(("arbitrary",)),
        name="mixer_ab_in_proj",
    )(x, mod, g, w_pad, cos, sin)


def _log_sigmoid(x):
    return jnp.minimum(x, 0.0) - jnp.log1p(jnp.exp(-jnp.abs(x)))


def _gla_kernel(af_ref, ab_ref, glf_ref, glb_ref, wg_ref, bg_ref, s0_ref,
                of_ref, ob_ref, sfin_ref, st_ref):
    i = pl.program_id(0)

    @pl.when(i < NSEG_P)
    def _():
        st_ref[...] = jnp.zeros_like(st_ref)

    @pl.when(jnp.logical_and(i >= NSEG_P, (i - NSEG_P) % SEG_PER_DEC == 0))
    def _():
        st_ref[...] = s0_ref[0]

    r = lax.broadcasted_iota(I32, (SEG, SEG), 0)
    c = lax.broadcasted_iota(I32, (SEG, SEG), 1)
    same = (r // GLA_CHUNK) == (c // GLA_CHUNK)
    rc = lax.broadcasted_iota(I32, (GLA_CHUNK, GLA_CHUNK), 0)
    cc = lax.broadcasted_iota(I32, (GLA_CHUNK, GLA_CHUNK), 1)
    nchunk = SEG // GLA_CHUNK

    for d, (a_ref, gl_ref, o_ref) in enumerate(((af_ref, glf_ref, of_ref), (ab_ref, glb_ref, ob_ref))):
        fwd = d == 0
        gcol = gl_ref[:, d * GATE_RANK:(d + 1) * GATE_RANK]
        la = _log_sigmoid(_bdot(gcol, wg_ref[d]) + bg_ref[d]) / GATE_TAU
        tri = jnp.where(jnp.logical_and(same, (c <= r) if fwd else (c >= r)), 1.0, 0.0).astype(BF16)
        l1, l2, l3 = _split3(la)
        b_all = (jnp.dot(tri, l1, preferred_element_type=F32)
                 + jnp.dot(tri, l2, preferred_element_type=F32)
                 + jnp.dot(tri, l3, preferred_element_type=F32))
        keep = (cc <= rc) if fwd else (cc >= rc)
        order = range(nchunk) if fwd else range(nchunk - 1, -1, -1)
        for ch in order:
            r0 = ch * GLA_CHUNK
            rows = slice(r0, r0 + GLA_CHUNK)
            for h in range(H_A):
                kc = slice(h * DK_A, (h + 1) * DK_A)
                q = a_ref[rows, h * DK_A:(h + 1) * DK_A] * (DK_A ** -0.5)
                k = a_ref[rows, Q_A + h * DK_A:Q_A + (h + 1) * DK_A]
                v = a_ref[rows, 2 * Q_A + h * DV_A:2 * Q_A + (h + 1) * DV_A]
                b = b_all[rows, kc]
                b_end = b[GLA_CHUNK - 1:GLA_CHUNK, :] if fwd else b[0:1, :]
                q_in = q * jnp.exp(b)
                attn = jnp.where(keep, _bdot_nt(q_in, k * jnp.exp(-b)), 0.0)
                s_t = st_ref[d, h]
                o_ref[rows, h * DV_A:(h + 1) * DV_A] = _bdot(attn, v) + _bdot_nt(q_in, s_t)
                st_ref[d, h] = s_t * jnp.exp(b_end) + _bdot_tn(v, k * jnp.exp(b_end - b))

    @pl.when(i < NSEG_P)
    def _():
        sfin_ref[0] = st_ref[...]


def _seg_bwd(i):
    j = i - NSEG_P
    return jnp.where(i < NSEG_P, i, NSEG_P + (j // SEG_PER_DEC) * SEG_PER_DEC + (SEG_PER_DEC - 1 - j % SEG_PER_DEC))


def _gla(a, gl, wg, bg, s0_t):
    seg = lambda width, f: pl.BlockSpec((SEG, width), lambda i: (f(i), 0))
    ident = lambda i: i
    st_block = (1, 2, H_A, DV_A, DK_A)
    return pl.pallas_call(
        _gla_kernel,
        grid=(NSEG,),
        in_specs=[
            seg(D, ident), seg(D, _seg_bwd), seg(GL_PAD, ident), seg(GL_PAD, _seg_bwd),
            pl.BlockSpec((2, GATE_RANK, Q_A), lambda i: (0, 0, 0)),
            pl.BlockSpec((2, 1, Q_A), lambda i: (0, 0, 0)),
            pl.BlockSpec(st_block, lambda i: (jnp.maximum(i - NSEG_P, 0) // SEG_PER_DEC, 0, 0, 0, 0)),
        ],
        out_specs=[
            seg(V_A, ident), seg(V_A, _seg_bwd),
            pl.BlockSpec(st_block, lambda i: (jnp.minimum(i, NSEG_P - 1), 0, 0, 0, 0)),
        ],
        out_shape=[
            jax.ShapeDtypeStruct((T, V_A), F32),
            jax.ShapeDtypeStruct((T, V_A), F32),
            jax.ShapeDtypeStruct((BATCH, 2, H_A, DV_A, DK_A), F32),
        ],
        scratch_shapes=[pltpu.VMEM((2, H_A, DV_A, DK_A), F32)],
        compiler_params=_params(("arbitrary",)),
        name="gla_bidir",
    )(a, a, gl, gl, wg, bg, s0_t)


def _diff_lambda(lam_ref, lam_init):
    lp = lam_ref[...]
    s01 = jnp.sum(lp[0:1] * lp[1:2], axis=1, keepdims=True)
    s23 = jnp.sum(lp[2:3] * lp[3:4], axis=1, keepdims=True)
    return jnp.exp(s01) - jnp.exp(s23) + lam_init


def _attn_prompt_kernel(lam_init, q_ref, k_ref, v_ref, lam_ref, o_ref):
    lam = _diff_lambda(lam_ref, lam_init)
    scale = DQK_B ** -0.5
    for h in range(H_B):
        ps = []
        for m in range(2):
            cols = slice((2 * h + m) * DQK_B, (2 * h + m + 1) * DQK_B)
            s = _bdot_nt(q_ref[:, cols], k_ref[:, cols]) * scale
            e = jnp.exp(s - jnp.max(s, axis=1, keepdims=True))
            ps.append(e / jnp.sum(e, axis=1, keepdims=True))
        w = ps[0] - lam * ps[1]
        o_ref[:, h * DV_B:(h + 1) * DV_B] = _bdot(w, v_ref[:, h * DV_B:(h + 1) * DV_B])


def _attn_sample_kernel(lam_init, q_ref, k_ref, v_ref, ck_ref, cv_ref, lam_ref, o_ref):
    lam = _diff_lambda(lam_ref, lam_init)
    scale = DQK_B ** -0.5
    for h in range(H_B):
        parts = []
        for m in range(2):
            cols = slice((2 * h + m) * DQK_B, (2 * h + m + 1) * DQK_B)
            q = q_ref[:, cols]
            sc = _bdot_nt(q, ck_ref[0, 0, h, m]) * scale
            sn = _bdot_nt(q, k_ref[:, cols]) * scale
            mx = jnp.maximum(jnp.max(sc, axis=1, keepdims=True), jnp.max(sn, axis=1, keepdims=True))
            ec = jnp.exp(sc - mx)
            en = jnp.exp(sn - mx)
            den = jnp.sum(ec, axis=1, keepdims=True) + jnp.sum(en, axis=1, keepdims=True)
            parts.append((ec / den, en / den))
        wc = parts[0][0] - lam * parts[1][0]
        wn = parts[0][1] - lam * parts[1][1]
        o_ref[:, h * DV_B:(h + 1) * DV_B] = (_bdot(wc, cv_ref[0, 0, h])
                                             + _bdot(wn, v_ref[:, h * DV_B:(h + 1) * DV_B]))


QB = SEQ
NQB_DEC = DEC_SEQ // QB


def _attn_kernel(lam_init, q_ref, kp_ref, vp_ref, ks_ref, vs_ref, ck_ref, cv_ref, lam_ref, o_ref):
    i = pl.program_id(0)

    @pl.when(i < BATCH)
    def _():
        _attn_prompt_kernel(lam_init, q_ref, kp_ref, vp_ref, lam_ref, o_ref)

    @pl.when(i >= BATCH)
    def _():
        _attn_sample_kernel(lam_init, q_ref, ks_ref, vs_ref, ck_ref, cv_ref, lam_ref, o_ref)


def _diff_attention(q, k, v, cache_k, cache_v, lam_p, lam_init):
    blk = lambda rows, f: pl.BlockSpec((rows, 512), f)
    dec_b = lambda i: jnp.maximum(i - BATCH, 0) // NQB_DEC
    own = lambda i: (i, 0)
    prompt_kv = lambda i: (jnp.minimum(i, BATCH - 1), 0)
    dec_kv = lambda i: (TP // DEC_SEQ + dec_b(i), 0)
    return pl.pallas_call(
        functools.partial(_attn_kernel, lam_init),
        grid=(BATCH + DEC_BATCH * NQB_DEC,),
        in_specs=[
            blk(QB, own), blk(SEQ, prompt_kv), blk(SEQ, prompt_kv), blk(DEC_SEQ, dec_kv), blk(DEC_SEQ, dec_kv),
            pl.BlockSpec((1, 1, H_B, 2, PAST_LEN, DQK_B), lambda i: (dec_b(i), 0, 0, 0, 0, 0)),
            pl.BlockSpec((1, 1, H_B, PAST_LEN, DV_B), lambda i: (dec_b(i), 0, 0, 0, 0)),
            pl.BlockSpec((4, DQK_B), lambda i: (0, 0)),
        ],
        out_specs=blk(QB, own),
        out_shape=jax.ShapeDtypeStruct((T, V_B), F32),
        compiler_params=_params(("arbitrary",)),
        name="diff_attention",
    )(q, k, v, k, v, cache_k, cache_v, lam_p)


def _head_rms(x, g, nheads, width):
    return jnp.concatenate([_rms(x[:, h * width:(h + 1) * width], g) for h in range(nheads)], axis=1)


def _mix_out_kernel(lam_init, of_ref, ob_ref, r_ref, oatt_ref, x_ref, mod_ref,
                    gg_ref, dg_ref, wo_ref, gp_ref, o_ref):
    m = mod_ref[0, 0]
    o_a = _head_rms(of_ref[...] + ob_ref[...], gg_ref[...], H_A, DV_A) * _silu(r_ref[...])
    o_b = _head_rms(oatt_ref[...], dg_ref[...], H_B, DV_B) * (1.0 - lam_init)
    out = _bdot(o_a, wo_ref[0:V_A, :]) + _bdot(o_b, wo_ref[V_A:V_A + V_B, :])
    o_ref[...] = x_ref[...] + m[2:3] * _rms(out, gp_ref[...])


def _mix_out(lam_init, o_f, o_b, r_a, o_att, x, mod, l, gla_g, diff_g, w_o, g_post):
    tok = lambda width: pl.BlockSpec((TM, width), lambda i: (i, 0))
    vec = lambda width: pl.BlockSpec((1, width), lambda i: (0, 0))
    return pl.pallas_call(
        functools.partial(_mix_out_kernel, lam_init),
        grid=(T // TM,),
        in_specs=[
            tok(512), tok(512), tok(512), tok(512), tok(D),
            pl.BlockSpec((1, 1, 6, D), lambda i: (l, _mod_row(i), 0, 0)),
            vec(DV_A), vec(DV_B),
            pl.BlockSpec((V_A + V_B, D), lambda i: (0, 0)),
            vec(D),
        ],
        out_specs=tok(D),
        out_shape=jax.ShapeDtypeStruct((T, D), F32),
        compiler_params=_params(("arbitrary",)),
        name="mixer_ab_out",
    )(o_f, o_b, r_a, o_att, x, mod, gla_g, diff_g, w_o, g_post)


def _gelu_tanh(x):
    return 0.5 * x * (1.0 + jnp.tanh(math.sqrt(2.0 / math.pi) * (x + 0.044715 * (x * x * x))))


def _sgu_kernel(x_ref, mod_ref, gpre_ref, win_ref, bin_ref, vg_ref, ws_ref, bs_ref,
                wout_ref, gpost_ref, o_ref, t_ref):
    m = mod_ref[0, 0]
    x = x_ref[...]
    h = _rms(x, gpre_ref[...]) * (1.0 + m[1:2]) + m[0:1]
    z = _gelu_tanh(_bdot(h, win_ref[...]) + bin_ref[...])
    v = _rms(z[:, SGU_DIM:], vg_ref[...])
    gw = SGU_DIM // SGU_GROUPS
    for ch in range(TM // SGU_CHUNK):
        rows = slice(ch * SGU_CHUNK, (ch + 1) * SGU_CHUNK)
        for g in range(SGU_GROUPS):
            cols = slice(g * gw, (g + 1) * gw)
            vs = _bdot(ws_ref[g], v[rows, cols]) + bs_ref[:, g:g + 1]
            t_ref[rows, cols] = (z[rows, cols] * vs).astype(BF16)
    out = jnp.dot(t_ref[...], wout_ref[...], preferred_element_type=F32)
    o_ref[...] = x + m[2:3] * _rms(out, gpost_ref[...])


def _sgu(x, mod, l, g_pre, w_in, b_in, v_g, w_s, b_s_t, w_out, g_post):
    tok = pl.BlockSpec((TM, D), lambda i: (i, 0))
    full = lambda *shape: pl.BlockSpec(shape, lambda i: (0,) * len(shape))
    return pl.pallas_call(
        _sgu_kernel,
        grid=(T // TM,),
        in_specs=[
            tok,
            pl.BlockSpec((1, 1, 6, D), lambda i: (l, _mod_row(i), 0, 0)),
            full(1, D), full(D, 2 * SGU_DIM), full(1, 2 * SGU_DIM), full(1, SGU_DIM),
            full(SGU_GROUPS, SGU_CHUNK, SGU_CHUNK), full(SGU_CHUNK, SGU_GROUPS),
            full(SGU_DIM, D), full(1, D),
        ],
        out_specs=tok,
        out_shape=jax.ShapeDtypeStruct((T, D), F32),
        scratch_shapes=[pltpu.VMEM((TM, SGU_DIM), BF16)],
        compiler_params=_params(("arbitrary",)),
        name="sgu_mixer",
    )(x, mod, g_pre, w_in, b_in, v_g, w_s, b_s_t, w_out, g_post)


LANES = 128
ROWS_PER_TOKEN = D // LANES


def _store_token_tiles(ref, x):
    n = x.shape[0]
    for c in range(ROWS_PER_TOKEN):
        ref[pl.ds(c, n, stride=ROWS_PER_TOKEN), :] = x[:, c * LANES:(c + 1) * LANES]


def _load_token_tiles(ref, n):
    return jnp.concatenate([ref[pl.ds(c, n, stride=ROWS_PER_TOKEN), :] for c in range(ROWS_PER_TOKEN)], axis=1)


def _token_rows(t):
    return pl.ds(pl.multiple_of(t * ROWS_PER_TOKEN, ROWS_PER_TOKEN), ROWS_PER_TOKEN)


def _router_kernel(x_ref, mod_ref, g_ref, wr_ref, eb_ref,
                   h_ref, te_ref, wn_ref, rk_ref, cnt_ref, carry_ref):
    i = pl.program_id(0)

    @pl.when(i == 0)
    def _():
        carry_ref[...] = jnp.zeros_like(carry_ref)

    m = mod_ref[0, 0]
    h = _rms(x_ref[...], g_ref[...]) * (1.0 + m[4:5]) + m[3:4]
    _store_token_tiles(h_ref, h)
    h1, h2, _ = _split3(h)
    w1, w2, _ = _split3(wr_ref[...])
    nt = lambda a, b: lax.dot_general(a, b, (((1,), (1,)), ((), ())), preferred_element_type=F32)
    logits = nt(w1, h1) + nt(w1, h2) + nt(w2, h1)
    scores = jax.nn.sigmoid(logits)
    sel = scores + eb_ref[...]

    row8 = lax.broadcasted_iota(I32, (GROUP_SIZE, TR), 0)
    gscore = []
    for g in range(N_GROUPS):
        xg = sel[g * GROUP_SIZE:(g + 1) * GROUP_SIZE]
        m1 = jnp.max(xg, axis=0, keepdims=True)
        i1 = jnp.min(jnp.where(xg == m1, row8, GROUP_SIZE), axis=0, keepdims=True)
        m2 = jnp.max(jnp.where(row8 == i1, NEG_INF, xg), axis=0, keepdims=True)
        gscore.append(m1 + m2)
    pieces = []
    for g in range(N_GROUPS):
        rank = jnp.zeros((1, TR), I32)
        for g2 in range(N_GROUPS):
            if g2 == g:
                continue
            beats = (gscore[g2] >= gscore[g]) if g2 < g else (gscore[g2] > gscore[g])
            rank = rank + beats.astype(I32)
        pieces.append(jnp.where(rank < TOPK_GROUPS, sel[g * GROUP_SIZE:(g + 1) * GROUP_SIZE], NEG_INF))
    cur = jnp.concatenate(pieces, axis=0)

    row = lax.broadcasted_iota(I32, (N_EXPERTS, TR), 0)
    idxs, ws = [], []
    for _ in range(TOP_K):
        mx = jnp.max(cur, axis=0, keepdims=True)
        idx = jnp.min(jnp.where(cur == mx, row, N_EXPERTS), axis=0, keepdims=True)
        hit = row == idx
        ws.append(jnp.sum(jnp.where(hit, scores, 0.0), axis=0, keepdims=True))
        cur = jnp.where(hit, NEG_INF, cur)
        idxs.append(idx)
    mask = jnp.zeros((N_EXPERTS, TR), F32)
    for idx in idxs:
        mask = mask + (row == idx).astype(F32)
    wsum = ws[0]
    for wk in ws[1:]:
        wsum = wsum + wk

    tj = lax.broadcasted_iota(I32, (TR, TR), 0)
    ti = lax.broadcasted_iota(I32, (TR, TR), 1)
    upper = jnp.where(tj < ti, 1.0, 0.0).astype(BF16)
    pos = carry_ref[...] + jnp.dot(mask.astype(BF16), upper, preferred_element_type=F32)
    for k in range(TOP_K):
        hit = row == idxs[k]
        te_ref[k:k + 1, :] = idxs[k]
        wn_ref[k:k + 1, :] = ws[k] / wsum * ROUTED_SCALE
        rk_ref[k:k + 1, :] = jnp.sum(jnp.where(hit, pos, 0.0), axis=0, keepdims=True).astype(I32)
    carry_ref[...] = carry_ref[...] + jnp.sum(mask, axis=1, keepdims=True)
    cnt_ref[...] = carry_ref[...]


def _router(x, mod, l, g, wr_t, e_bias):
    kt = lambda dtype: jax.ShapeDtypeStruct((TOP_K, T), dtype)
    kt_spec = pl.BlockSpec((TOP_K, TR), lambda i: (0, i))
    tiles_per_dec = DEC_SEQ // TR
    mod_row = lambda i: jnp.where(i < TP // TR, 0, 1 + (i - TP // TR) // tiles_per_dec)
    return pl.pallas_call(
        _router_kernel,
        grid=(T // TR,),
        in_specs=[
            pl.BlockSpec((TR, D), lambda i: (i, 0)),
            pl.BlockSpec((1, 1, 6, D), lambda i: (l, mod_row(i), 0, 0)),
            pl.BlockSpec((1, D), lambda i: (0, 0)),
            pl.BlockSpec((N_EXPERTS, D), lambda i: (0, 0)),
            pl.BlockSpec((N_EXPERTS, 1), lambda i: (0, 0)),
        ],
        out_specs=[
            pl.BlockSpec((TR * ROWS_PER_TOKEN, LANES), lambda i: (i, 0)),
            kt_spec, kt_spec, kt_spec,
            pl.BlockSpec((N_EXPERTS, 1), lambda i: (0, 0)),
        ],
        out_shape=[
            jax.ShapeDtypeStruct((T * ROWS_PER_TOKEN, LANES), F32), kt(I32), kt(F32), kt(I32),
            jax.ShapeDtypeStruct((N_EXPERTS, 1), F32),
        ],
        scratch_shapes=[pltpu.VMEM((N_EXPERTS, 1), F32)],
        compiler_params=_params(("arbitrary",)),
        name="moe_router",
    )(x, mod, g, wr_t, e_bias)


_PAD_BITS = tuple(1 << b for b in range(GM.bit_length() - 1))


def _dispatch_kernel(pad_start_ref, pad_len_ref, slot_ref, h_ref, xg_ref, zero_ref, sem):
    i = pl.program_id(0)

    @pl.when(i == 0)
    def _():
        zero_ref[...] = jnp.zeros_like(zero_ref)

        def pad_copies(e):
            start = pad_start_ref[e]
            n = pad_len_ref[e]
            copies = []
            for bit in _PAD_BITS:
                first = start + (n & ~(2 * bit - 1))
                lo = pl.multiple_of(first * ROWS_PER_TOKEN, ROWS_PER_TOKEN)
                rows = bit * ROWS_PER_TOKEN
                copies.append(((n & bit) != 0, pltpu.make_async_copy(
                    zero_ref.at[pl.ds(0, rows)], xg_ref.at[pl.ds(lo, rows)], sem)))
            return copies

        def start_e(e, carry):
            for on, cp in pad_copies(e):
                @pl.when(on)
                def _():
                    cp.start()
            return carry

        def wait_e(e, carry):
            for on, cp in pad_copies(e):
                @pl.when(on)
                def _():
                    cp.wait()
            return carry

        lax.fori_loop(0, N_EXPERTS, start_e, 0)
        lax.fori_loop(0, N_EXPERTS, wait_e, 0)

    def row_copy(t, k):
        return pltpu.make_async_copy(h_ref.at[_token_rows(t)], xg_ref.at[_token_rows(slot_ref[k, t])], sem)

    def start_t(t, carry):
        for k in range(TOP_K):
            row_copy(t, k).start(priority=k % 2)
        return carry

    def wait_t(t, carry):
        for k in range(TOP_K):
            row_copy(t, k).wait()
        return carry

    lax.fori_loop(0, TD, start_t, 0)
    lax.fori_loop(0, TD, wait_t, 0)


def _dispatch(pad_start, pad_len, slot, h):
    grid_spec = pltpu.PrefetchScalarGridSpec(
        num_scalar_prefetch=2,
        grid=(T // TD,),
        in_specs=[
            pl.BlockSpec((TOP_K, TD), lambda i, *_: (0, i), memory_space=pltpu.SMEM),
            pl.BlockSpec((TD * ROWS_PER_TOKEN, LANES), lambda i, *_: (i, 0)),
        ],
        out_specs=pl.BlockSpec(memory_space=pl.ANY),
        scratch_shapes=[pltpu.VMEM((GM // 2 * ROWS_PER_TOKEN, LANES), F32), pltpu.SemaphoreType.DMA],
    )
    return pl.pallas_call(
        _dispatch_kernel,
        grid_spec=grid_spec,
        out_shape=jax.ShapeDtypeStruct((SP * ROWS_PER_TOKEN, LANES), F32),
        compiler_params=_params(("arbitrary",)),
        name="moe_dispatch",
    )(pad_start, pad_len, slot, h)


def _gmm_kernel(tile_e_ref, tile_blk_ref, nvalid_ref, x_ref, wg_ref, wu_ref, wd_ref,
                y_ref, wgu_scr, wd_scr):
    j = pl.program_id(0)

    @pl.when(j < nvalid_ref[0])
    def _():
        prev = tile_e_ref[jnp.maximum(j - 1, 0)]

        @pl.when(jnp.logical_or(j == 0, tile_e_ref[j] != prev))
        def _():
            wgu_scr[:, 0:D_EXPERT] = wg_ref[0].astype(BF16)
            wgu_scr[:, D_EXPERT:2 * D_EXPERT] = wu_ref[0].astype(BF16)
            wd_scr[...] = wd_ref[0].astype(BF16)

        x = _load_token_tiles(x_ref, GM)
        gu = jnp.dot(x.astype(BF16), wgu_scr[...], preferred_element_type=F32)
        hid = _silu(gu[:, 0:D_EXPERT]) * gu[:, D_EXPERT:2 * D_EXPERT]
        _store_token_tiles(y_ref, jnp.dot(hid.astype(BF16), wd_scr[...], preferred_element_type=F32))


def _gmm(tile_e, tile_blk, nvalid, xg, w_gate, w_up, w_down):
    row_tile = pl.BlockSpec((GM * ROWS_PER_TOKEN, LANES), lambda j, te, tb, nv: (tb[j], 0))
    grid_spec = pltpu.PrefetchScalarGridSpec(
        num_scalar_prefetch=3,
        grid=(NT_MAX,),
        in_specs=[
            row_tile,
            pl.BlockSpec((1, D, D_EXPERT), lambda j, te, tb, nv: (te[j], 0, 0)),
            pl.BlockSpec((1, D, D_EXPERT), lambda j, te, tb, nv: (te[j], 0, 0)),
            pl.BlockSpec((1, D_EXPERT, D), lambda j, te, tb, nv: (te[j], 0, 0)),
        ],
        out_specs=row_tile,
        scratch_shapes=[pltpu.VMEM((D, 2 * D_EXPERT), BF16), pltpu.VMEM((D_EXPERT, D), BF16)],
    )
    return pl.pallas_call(
        _gmm_kernel,
        grid_spec=grid_spec,
        out_shape=jax.ShapeDtypeStruct((SP * ROWS_PER_TOKEN, LANES), F32),
        compiler_params=_params(("arbitrary",)),
        name="moe_grouped_matmul",
    )(tile_e, tile_blk, nvalid, xg, w_gate, w_up, w_down)


def _combine_kernel(slot_ref, slot_next_ref, wn_ref, x_ref, mod_ref, gpre_ref, gp_ref, wsg_ref, wsu_ref, wsd_ref,
                    yg_ref, o_ref, ybuf, sems):
    i = pl.program_id(0)
    last = pl.num_programs(0) - 1
    buf = i % 2

    def row_copy(slots, step, t, k):
        b = step % 2
        return pltpu.make_async_copy(yg_ref.at[_token_rows(slots[k, t])], ybuf.at[b, k, _token_rows(t)], sems.at[b])

    def start_step(slots, step):
        def body(t, carry):
            for k in range(TOP_K):
                row_copy(slots, step, t, k).start(priority=k % 2)
            return carry
        lax.fori_loop(0, TD, body, 0)

    def wait_step(slots, step):
        def body(t, carry):
            for k in range(TOP_K):
                row_copy(slots, step, t, k).wait()
            return carry
        lax.fori_loop(0, TD, body, 0)

    @pl.when(i == 0)
    def _():
        start_step(slot_ref, i)

    @pl.when(i < last)
    def _():
        start_step(slot_next_ref, i + 1)

    m = mod_ref[0, 0]
    x = x_ref[...]
    hb = (_rms(x, gpre_ref[...]) * (1.0 + m[4:5]) + m[3:4]).astype(BF16)
    hid = (_silu(jnp.dot(hb, wsg_ref[...], preferred_element_type=F32))
           * jnp.dot(hb, wsu_ref[...], preferred_element_type=F32))
    acc = jnp.dot(hid.astype(BF16), wsd_ref[...], preferred_element_type=F32)

    r = lax.broadcasted_iota(I32, (TD, TD), 0)
    c = lax.broadcasted_iota(I32, (TD, TD), 1)
    eye = jnp.where(r == c, 1.0, 0.0).astype(BF16)
    nt = lambda a, b: lax.dot_general(a, b, (((1,), (1,)), ((), ())), preferred_element_type=F32)
    w1, w2, w3 = _split3(wn_ref[...])
    w_t = nt(eye, w1) + nt(eye, w2) + nt(eye, w3)

    wait_step(slot_ref, i)
    for k in range(TOP_K):
        acc = acc + _load_token_tiles(ybuf.at[buf, k], TD) * w_t[:, k:k + 1]
    o_ref[...] = x + m[5:6] * _rms(acc, gp_ref[...])


def _combine(slot, wn, x, mod, l, g_pre, g_post, ws_gate, ws_up, ws_down, yg):
    tok = pl.BlockSpec((TD, D), lambda i: (i, 0))
    tiles_per_dec = DEC_SEQ // TD
    mod_row = lambda i: jnp.where(i < TP // TD, 0, 1 + (i - TP // TD) // tiles_per_dec)
    full = lambda *shape: pl.BlockSpec(shape, lambda i: (0,) * len(shape))
    return pl.pallas_call(
        _combine_kernel,
        grid=(T // TD,),
        in_specs=[
            pl.BlockSpec((TOP_K, TD), lambda i: (0, i), memory_space=pltpu.SMEM),
            pl.BlockSpec((TOP_K, TD), lambda i: (0, jnp.minimum(i + 1, T // TD - 1)), memory_space=pltpu.SMEM),
            pl.BlockSpec((TOP_K, TD), lambda i: (0, i)),
            tok,
            pl.BlockSpec((1, 1, 6, D), lambda i: (l, mod_row(i), 0, 0)),
            full(1, D), full(1, D), full(D, D_SHARED), full(D, D_SHARED), full(D_SHARED, D),
            pl.BlockSpec(memory_space=pl.ANY),
        ],
        out_specs=tok,
        out_shape=jax.ShapeDtypeStruct((T, D), F32),
        scratch_shapes=[pltpu.VMEM((2, TOP_K, TD * ROWS_PER_TOKEN, LANES), F32), pltpu.SemaphoreType.DMA((2,))],
        compiler_params=_params(("arbitrary",)),
        name="moe_combine",
    )(slot, slot, wn, x, mod, g_pre, g_post, ws_gate, ws_up, ws_down, yg)


def _moe_layer(x, mod, l, g_pre, g_post, w_router, e_bias, w_gate, w_up, w_down,
               ws_gate, ws_up, ws_down):
    h, top_e, wn, rk, cnt = _router(x, mod, l, g_pre, w_router.T, e_bias.reshape(N_EXPERTS, 1))
    cnt = cnt.reshape(N_EXPERTS).astype(I32)
    padded = (cnt + GM - 1) // GM * GM
    ends = jnp.cumsum(padded)
    offs = ends - padded
    eid = jnp.arange(N_EXPERTS, dtype=I32)[:, None, None]
    slot = rk + jnp.sum(jnp.where(top_e[None] == eid, offs[:, None, None], 0), axis=0)
    nvalid = ends[-1] // GM
    tile_start = jnp.arange(NT_MAX, dtype=I32) * GM
    tile_raw = jnp.sum((tile_start[:, None] >= ends[None, :]).astype(I32), axis=1)
    last = jnp.maximum(nvalid - 1, 0)
    tile_blk = jnp.minimum(jnp.arange(NT_MAX, dtype=I32), last)
    tile_e = jnp.minimum(tile_raw, N_EXPERTS - 1)
    tile_e = jnp.where(jnp.arange(NT_MAX) <= last, tile_e, tile_e[last])
    xg = _dispatch(offs + cnt, padded - cnt, slot, h)
    yg = _gmm(tile_e, tile_blk, nvalid.reshape(1), xg, w_gate, w_up, w_down)
    return _combine(slot, wn, x, mod, l, g_pre, g_post,
                    ws_gate.astype(BF16), ws_up.astype(BF16), ws_down.astype(BF16), yg)


def _rope_tables():
    n = DEC_SEQ
    rows = n // GRID_W
    row = jnp.repeat(jnp.arange(rows), GRID_W).astype(F32)
    col = jnp.tile(jnp.arange(GRID_W), rows).astype(F32)
    half = DQK_B // 2
    inv = ROPE_BASE ** (-jnp.arange(0, half, 2, dtype=F32) / half)
    ang_r = row[:, None] * inv
    ang_c = col[:, None] * inv
    ang = jnp.concatenate([ang_r, ang_r, ang_c, ang_c], axis=-1)
    reps = QK_B // DQK_B
    return jnp.tile(jnp.cos(ang), (1, reps)), jnp.tile(jnp.sin(ang), (1, reps))


def _pad_in_proj(w):
    s = [0, Q_A, 2 * Q_A, 2 * Q_A + V_A, 2 * Q_A + 2 * V_A]
    s += [s[-1] + GATE_RANK, s[-1] + 2 * GATE_RANK]
    s += [s[-1] + QK_B, s[-1] + 2 * QK_B, s[-1] + 2 * QK_B + V_B]
    gates = jnp.pad(w[:, s[4]:s[6]], ((0, 0), (0, GL_PAD - 2 * GATE_RANK)))
    return jnp.concatenate([w[:, s[0]:s[4]], gates, w[:, s[6]:s[9]]], axis=1).astype(BF16)


def kernel(x_prompt, x_sample, c, c_ctx, state_gla, cache_k, cache_v, ada_w, ada_b, norm_pre_mix, norm_post_mix, norm_pre_ffn, norm_post_ffn, ab_w_in, gla_w_g2, gla_b_g2, gla_norm_g, diff_lambda, diff_norm_g, ab_w_out, sgu_w_in, sgu_b_in, sgu_norm_g, sgu_w_s, sgu_b_s, sgu_w_out, moe_w_router, moe_e_bias, moe_w_gate, moe_w_up, moe_w_down, moe_ws_gate, moe_ws_up, moe_ws_down):
    depth = ada_w.shape[0]
    x = jnp.concatenate([x_prompt.reshape(TP, D), x_sample.reshape(TS, D)], axis=0)
    cond = jnp.concatenate([c_ctx[None, :], c, jnp.zeros((8 - 1 - DEC_BATCH, D), F32)], axis=0)
    mod = _modulation(cond, ada_w, ada_b)
    cos, sin = _rope_tables()
    vec = lambda a: a.reshape(1, -1)
    new_s = new_k = new_v = None
    for l in range(depth):
        if l % 2 == 0:
            e = l // 2
            lam_init = 0.8 - 0.6 * math.exp(-0.3 * l)
            a, r_a, gl, q_b, k_b, v_b = _in_proj(x, mod, l, vec(norm_pre_mix[l]), _pad_in_proj(ab_w_in[e]), cos, sin)
            s0_t = jnp.swapaxes(state_gla[:, e], -1, -2)
            o_f, o_bw, s_fin_t = _gla(a, gl, gla_w_g2[e], gla_b_g2[e].reshape(2, 1, Q_A), s0_t)
            o_att = _diff_attention(q_b, k_b, v_b, cache_k, cache_v, diff_lambda[e], lam_init)
            x = _mix_out(lam_init, o_f, o_bw, r_a, o_att, x, mod, l, vec(gla_norm_g[e]), vec(diff_norm_g[e]),
                         ab_w_out[e].astype(BF16), vec(norm_post_mix[l]))
            new_s = jnp.swapaxes(s_fin_t, -1, -2)
            new_k = k_b[:TP].reshape(BATCH, SEQ, H_B, 2, DQK_B).transpose(0, 2, 3, 1, 4)
            new_v = v_b[:TP].reshape(BATCH, SEQ, H_B, DV_B).transpose(0, 2, 1, 3)
        else:
            o = l // 2
            x = _sgu(x, mod, l, vec(norm_pre_mix[l]), sgu_w_in[o].astype(BF16), vec(sgu_b_in[o]),
                     vec(sgu_norm_g[o]), sgu_w_s[o], sgu_b_s[o].T, sgu_w_out[o].astype(BF16),
                     vec(norm_post_mix[l]))
        x = _moe_layer(x, mod, l, vec(norm_pre_ffn[l]), vec(norm_post_ffn[l]), moe_w_router[l], moe_e_bias[l],
                       moe_w_gate[l], moe_w_up[l], moe_w_down[l], moe_ws_gate[l], moe_ws_up[l], moe_ws_down[l])
    y_prompt = x[:TP].reshape(BATCH, SEQ, D)
    y_sample = x[TP:].reshape(DEC_BATCH, DEC_SEQ, D)
    return (y_prompt, y_sample, new_s[:, None], new_k[:, None], new_v[:, None])
```

```python
import functools
import math

import jax
import jax.numpy as jnp
from jax import lax
from jax.experimental import pallas as pl
from jax.experimental.pallas import tpu as pltpu
from jax.experimental.pallas import tpu_sc as plsc

F32 = jnp.float32
BF16 = jnp.bfloat16
I32 = jnp.int32

D = 1024
BATCH, SEQ = 32, 256
DEC_BATCH, DEC_SEQ = 4, 2048
PAST_LEN = 256
GRID_W = 64
EPS = 1e-6
TP = BATCH * SEQ
TS = DEC_BATCH * DEC_SEQ
T = TP + TS
H_A, DK_A, DV_A = 4, 64, 128
Q_A, V_A = H_A * DK_A, H_A * DV_A
GATE_RANK, GATE_TAU, GLA_CHUNK = 16, 16.0, 64
H_B, DQK_B, DV_B = 4, 64, 128
QK_B, V_B = H_B * 2 * DQK_B, H_B * DV_B
ROPE_BASE = 10000.0
SGU_DIM, SGU_GROUPS, SGU_CHUNK = 1024, 4, 128
N_EXPERTS, TOP_K, N_GROUPS, TOPK_GROUPS = 64, 8, 8, 4
GROUP_SIZE = N_EXPERTS // N_GROUPS
D_EXPERT, D_SHARED = 256, 256
ROUTED_SCALE = 2.5

TM = 512
NPT = TP // TM
TILES_PER_DEC = DEC_SEQ // TM
SEG = 256
NSEG = T // SEG
NSEG_P = TP // SEG
SEG_PER_DEC = DEC_SEQ // SEG
TR = 512
TD = 256
GM = 256
NT_MAX = T * TOP_K // GM + N_EXPERTS
SP = NT_MAX * GM
GL_PAD = 128
VMEM_LIMIT = 56 * 1024 * 1024
NEG_INF = float("-inf")


def _bdot(a, b):
    return jnp.dot(a.astype(BF16), b.astype(BF16), preferred_element_type=F32)


def _bdot_nt(a, b):
    return lax.dot_general(a.astype(BF16), b.astype(BF16), (((1,), (1,)), ((), ())),
                           preferred_element_type=F32)


def _bdot_tn(a, b):
    return lax.dot_general(a.astype(BF16), b.astype(BF16), (((0,), (0,)), ((), ())),
                           preferred_element_type=F32)


def _split3(x):
    x1 = x.astype(BF16)
    r1 = x - x1.astype(F32)
    x2 = r1.astype(BF16)
    x3 = (r1 - x2.astype(F32)).astype(BF16)
    return x1, x2, x3


def _rms(x, g):
    return x * lax.rsqrt(jnp.mean(x * x, axis=-1, keepdims=True) + EPS) * g


def _silu(x):
    return x * jax.nn.sigmoid(x)


def _mod_row(i):
    return jnp.where(i < NPT, 0, 1 + (i - NPT) // TILES_PER_DEC)


def _params(sem, limit=VMEM_LIMIT):
    return pltpu.CompilerParams(dimension_semantics=sem, vmem_limit_bytes=limit)


def _mod_kernel(c_ref, w_ref, b_ref, o_ref):
    o_ref[0] = _bdot(_silu(c_ref[...]), w_ref[0]) + b_ref[0]


def _modulation(cond, ada_w, ada_b):
    depth = ada_w.shape[0]
    nj = 6
    out = pl.pallas_call(
        _mod_kernel,
        grid=(depth, nj),
        in_specs=[
            pl.BlockSpec((8, D), lambda l, j: (0, 0)),
            pl.BlockSpec((1, D, D), lambda l, j: (l, 0, j)),
            pl.BlockSpec((1, 1, D), lambda l, j: (l, 0, j)),
        ],
        out_specs=pl.BlockSpec((1, 8, D), lambda l, j: (l, 0, j)),
        out_shape=jax.ShapeDtypeStruct((depth, 8, 6 * D), F32),
        compiler_params=_params(("arbitrary", "arbitrary")),
        name="adaln_modulation",
    )(cond, ada_w, ada_b.reshape(depth, 1, 6 * D))
    return out.reshape(depth, 8, 6, D)


_C_A, _C_R, _C_GL, _C_Q, _C_K, _C_V, _C_END = 0, 1024, 1536, 1664, 2176, 2688, 3200


def _rope(x, cos, sin):
    lane = lax.broadcasted_iota(I32, x.shape, 1)
    first = (lane % 32) < 16
    n = x.shape[1]
    xr = jnp.where(first, -pltpu.roll(x, n - 16, 1), pltpu.roll(x, 16, 1))
    return x * cos + xr * sin


def _in_kernel(x_ref, mod_ref, g_ref, w_ref, cos_ref, sin_ref,
               a_ref, r_ref, gl_ref, q_ref, k_ref, v_ref):
    i = pl.program_id(0)
    m = mod_ref[0, 0]
    h = _rms(x_ref[...], g_ref[...]) * (1.0 + m[1:2]) + m[0:1]
    hb = h.astype(BF16)

    def proj(c0, c1):
        return jnp.dot(hb, w_ref[:, c0:c1], preferred_element_type=F32)

    a_ref[...] = proj(_C_A, _C_R)
    r_ref[...] = proj(_C_R, _C_GL)
    gl_ref[...] = proj(_C_GL, _C_Q)
    v_ref[...] = proj(_C_V, _C_END)
    q = proj(_C_Q, _C_K)
    k = proj(_C_K, _C_V)

    @pl.when(i < NPT)
    def _():
        q_ref[...] = q
        k_ref[...] = k

    @pl.when(i >= NPT)
    def _():
        cos = cos_ref[...]
        sin = sin_ref[...]
        q_ref[...] = _rope(q, cos, sin)
        k_ref[...] = _rope(k, cos, sin)


def _in_proj(x, mod, l, g, w_pad, cos, sin):
    tok = lambda width: pl.BlockSpec((TM, width), lambda i: (i, 0))
    rope_spec = pl.BlockSpec((TM, QK_B), lambda i: (jnp.maximum(i - NPT, 0) % TILES_PER_DEC, 0))
    widths = (1024, 512, GL_PAD, 512, 512, 512)
    return pl.pallas_call(
        _in_kernel,
        grid=(T // TM,),
        in_specs=[
            tok(D),
            pl.BlockSpec((1, 1, 6, D), lambda i: (l, _mod_row(i), 0, 0)),
            pl.BlockSpec((1, D), lambda i: (0, 0)),
            pl.BlockSpec((D, _C_END), lambda i: (0, 0)),
            rope_spec, rope_spec,
        ],
        out_specs=[tok(w) for w in widths],
        out_shape=[jax.ShapeDtypeStruct((T, w), F32) for w in widths],
        compiler_params=_params(("arbitrary",)),
        name="mixer_ab_in_proj",
    )(x, mod, g, w_pad, cos, sin)


def _log_sigmoid(x):
    return jnp.minimum(x, 0.0) - jnp.log1p(jnp.exp(-jnp.abs(x)))


def _gla_kernel(af_ref, ab_ref, glf_ref, glb_ref, wg_ref, bg_ref, s0_ref,
                of_ref, ob_ref, sfin_ref, st_ref):
    i = pl.program_id(0)

    @pl.when(i < NSEG_P)
    def _():
        st_ref[...] = jnp.zeros_like(st_ref)

    @pl.when(jnp.logical_and(i >= NSEG_P, (i - NSEG_P) % SEG_PER_DEC == 0))
    def _():
        st_ref[...] = s0_ref[0]

    r = lax.broadcasted_iota(I32, (SEG, SEG), 0)
    c = lax.broadcasted_iota(I32, (SEG, SEG), 1)
    same = (r // GLA_CHUNK) == (c // GLA_CHUNK)
    rc = lax.broadcasted_iota(I32, (GLA_CHUNK, GLA_CHUNK), 0)
    cc = lax.broadcasted_iota(I32, (GLA_CHUNK, GLA_CHUNK), 1)
    nchunk = SEG // GLA_CHUNK

    for d, (a_ref, gl_ref, o_ref) in enumerate(((af_ref, glf_ref, of_ref), (ab_ref, glb_ref, ob_ref))):
        fwd = d == 0
        gcol = gl_ref[:, d * GATE_RANK:(d + 1) * GATE_RANK]
        la = _log_sigmoid(_bdot(gcol, wg_ref[d]) + bg_ref[d]) / GATE_TAU
        tri = jnp.where(jnp.logical_and(same, (c <= r) if fwd else (c >= r)), 1.0, 0.0).astype(BF16)
        l1, l2, l3 = _split3(la)
        b_all = (jnp.dot(tri, l1, preferred_element_type=F32)
                 + jnp.dot(tri, l2, preferred_element_type=F32)
                 + jnp.dot(tri, l3, preferred_element_type=F32))
        keep = (cc <= rc) if fwd else (cc >= rc)
        order = range(nchunk) if fwd else range(nchunk - 1, -1, -1)
        for ch in order:
            r0 = ch * GLA_CHUNK
            rows = slice(r0, r0 + GLA_CHUNK)
            for h in range(H_A):
                kc = slice(h * DK_A, (h + 1) * DK_A)
                q = a_ref[rows, h * DK_A:(h + 1) * DK_A] * (DK_A ** -0.5)
                k = a_ref[rows, Q_A + h * DK_A:Q_A + (h + 1) * DK_A]
                v = a_ref[rows, 2 * Q_A + h * DV_A:2 * Q_A + (h + 1) * DV_A]
                b = b_all[rows, kc]
                b_end = b[GLA_CHUNK - 1:GLA_CHUNK, :] if fwd else b[0:1, :]
                q_in = q * jnp.exp(b)
                attn = jnp.where(keep, _bdot_nt(q_in, k * jnp.exp(-b)), 0.0)
                s_t = st_ref[d, h]
                o_ref[rows, h * DV_A:(h + 1) * DV_A] = _bdot(attn, v) + _bdot_nt(q_in, s_t)
                st_ref[d, h] = s_t * jnp.exp(b_end) + _bdot_tn(v, k * jnp.exp(b_end - b))

    @pl.when(i < NSEG_P)
    def _():
        sfin_ref[0] = st_ref[...]


def _seg_bwd(i):
    j = i - NSEG_P
    return jnp.where(i < NSEG_P, i, NSEG_P + (j // SEG_PER_DEC) * SEG_PER_DEC + (SEG_PER_DEC - 1 - j % SEG_PER_DEC))


def _gla(a, gl, wg, bg, s0_t):
    seg = lambda width, f: pl.BlockSpec((SEG, width), lambda i: (f(i), 0))
    ident = lambda i: i
    st_block = (1, 2, H_A, DV_A, DK_A)
    return pl.pallas_call(
        _gla_kernel,
        grid=(NSEG,),
        in_specs=[
            seg(D, ident), seg(D, _seg_bwd), seg(GL_PAD, ident), seg(GL_PAD, _seg_bwd),
            pl.BlockSpec((2, GATE_RANK, Q_A), lambda i: (0, 0, 0)),
            pl.BlockSpec((2, 1, Q_A), lambda i: (0, 0, 0)),
            pl.BlockSpec(st_block, lambda i: (jnp.maximum(i - NSEG_P, 0) // SEG_PER_DEC, 0, 0, 0, 0)),
        ],
        out_specs=[
            seg(V_A, ident), seg(V_A, _seg_bwd),
            pl.BlockSpec(st_block, lambda i: (jnp.minimum(i, NSEG_P - 1), 0, 0, 0, 0)),
        ],
        out_shape=[
            jax.ShapeDtypeStruct((T, V_A), F32),
            jax.ShapeDtypeStruct((T, V_A), F32),
            jax.ShapeDtypeStruct((BATCH, 2, H_A, DV_A, DK_A), F32),
        ],
        scratch_shapes=[pltpu.VMEM((2, H_A, DV_A, DK_A), F32)],
        compiler_params=_params(("arbitrary",)),
        name="gla_bidir",
    )(a, a, gl, gl, wg, bg, s0_t)


def _diff_lambda(lam_ref, lam_init):
    lp = lam_ref[...]
    s01 = jnp.sum(lp[0:1] * lp[1:2], axis=1, keepdims=True)
    s23 = jnp.sum(lp[2:3] * lp[3:4], axis=1, keepdims=True)
    return jnp.exp(s01) - jnp.exp(s23) + lam_init


def _attn_prompt_kernel(lam_init, q_ref, k_ref, v_ref, lam_ref, o_ref):
    lam = _diff_lambda(lam_ref, lam_init)
    scale = DQK_B ** -0.5
    for h in range(H_B):
        ps = []
        for m in range(2):
            cols = slice((2 * h + m) * DQK_B, (2 * h + m + 1) * DQK_B)
            s = _bdot_nt(q_ref[:, cols], k_ref[:, cols]) * scale
            e = jnp.exp(s - jnp.max(s, axis=1, keepdims=True))
            ps.append(e / jnp.sum(e, axis=1, keepdims=True))
        w = ps[0] - lam * ps[1]
        o_ref[:, h * DV_B:(h + 1) * DV_B] = _bdot(w, v_ref[:, h * DV_B:(h + 1) * DV_B])


def _attn_sample_kernel(lam_init, q_ref, k_ref, v_ref, ck_ref, cv_ref, lam_ref, o_ref):
    lam = _diff_lambda(lam_ref, lam_init)
    scale = DQK_B ** -0.5
    for h in range(H_B):
        parts = []
        for m in range(2):
            cols = slice((2 * h + m) * DQK_B, (2 * h + m + 1) * DQK_B)
            q = q_ref[:, cols]
            sc = _bdot_nt(q, ck_ref[0, 0, h, m]) * scale
            sn = _bdot_nt(q, k_ref[:, cols]) * scale
            mx = jnp.maximum(jnp.max(sc, axis=1, keepdims=True), jnp.max(sn, axis=1, keepdims=True))
            ec = jnp.exp(sc - mx)
            en = jnp.exp(sn - mx)
            den = jnp.sum(ec, axis=1, keepdims=True) + jnp.sum(en, axis=1, keepdims=True)
            parts.append((ec / den, en / den))
        wc = parts[0][0] - lam * parts[1][0]
        wn = parts[0][1] - lam * parts[1][1]
        o_ref[:, h * DV_B:(h + 1) * DV_B] = (_bdot(wc, cv_ref[0, 0, h])
                                             + _bdot(wn, v_ref[:, h * DV_B:(h + 1) * DV_B]))


QB = SEQ
NQB_DEC = DEC_SEQ // QB


def _attn_kernel(lam_init, q_ref, kp_ref, vp_ref, ks_ref, vs_ref, ck_ref, cv_ref, lam_ref, o_ref):
    i = pl.program_id(0)

    @pl.when(i < BATCH)
    def _():
        _attn_prompt_kernel(lam_init, q_ref, kp_ref, vp_ref, lam_ref, o_ref)

    @pl.when(i >= BATCH)
    def _():
        _attn_sample_kernel(lam_init, q_ref, ks_ref, vs_ref, ck_ref, cv_ref, lam_ref, o_ref)


def _diff_attention(q, k, v, cache_k, cache_v, lam_p, lam_init):
    blk = lambda rows, f: pl.BlockSpec((rows, 512), f)
    dec_b = lambda i: jnp.maximum(i - BATCH, 0) // NQB_DEC
    own = lambda i: (i, 0)
    prompt_kv = lambda i: (jnp.minimum(i, BATCH - 1), 0)
    dec_kv = lambda i: (TP // DEC_SEQ + dec_b(i), 0)
    return pl.pallas_call(
        functools.partial(_attn_kernel, lam_init),
        grid=(BATCH + DEC_BATCH * NQB_DEC,),
        in_specs=[
            blk(QB, own), blk(SEQ, prompt_kv), blk(SEQ, prompt_kv), blk(DEC_SEQ, dec_kv), blk(DEC_SEQ, dec_kv),
            pl.BlockSpec((1, 1, H_B, 2, PAST_LEN, DQK_B), lambda i: (dec_b(i), 0, 0, 0, 0, 0)),
            pl.BlockSpec((1, 1, H_B, PAST_LEN, DV_B), lambda i: (dec_b(i), 0, 0, 0, 0)),
            pl.BlockSpec((4, DQK_B), lambda i: (0, 0)),
        ],
        out_specs=blk(QB, own),
        out_shape=jax.ShapeDtypeStruct((T, V_B), F32),
        compiler_params=_params(("arbitrary",)),
        name="diff_attention",
    )(q, k, v, k, v, cache_k, cache_v, lam_p)


def _head_rms(x, g, nheads, width):
    return jnp.concatenate([_rms(x[:, h * width:(h + 1) * width], g) for h in range(nheads)], axis=1)


def _mix_out_kernel(lam_init, of_ref, ob_ref, r_ref, oatt_ref, x_ref, mod_ref,
                    gg_ref, dg_ref, wo_ref, gp_ref, o_ref):
    m = mod_ref[0, 0]
    o_a = _head_rms(of_ref[...] + ob_ref[...], gg_ref[...], H_A, DV_A) * _silu(r_ref[...])
    o_b = _head_rms(oatt_ref[...], dg_ref[...], H_B, DV_B) * (1.0 - lam_init)
    out = _bdot(o_a, wo_ref[0:V_A, :]) + _bdot(o_b, wo_ref[V_A:V_A + V_B, :])
    o_ref[...] = x_ref[...] + m[2:3] * _rms(out, gp_ref[...])


def _mix_out(lam_init, o_f, o_b, r_a, o_att, x, mod, l, gla_g, diff_g, w_o, g_post):
    tok = lambda width: pl.BlockSpec((TM, width), lambda i: (i, 0))
    vec = lambda width: pl.BlockSpec((1, width), lambda i: (0, 0))
    return pl.pallas_call(
        functools.partial(_mix_out_kernel, lam_init),
        grid=(T // TM,),
        in_specs=[
            tok(512), tok(512), tok(512), tok(512), tok(D),
            pl.BlockSpec((1, 1, 6, D), lambda i: (l, _mod_row(i), 0, 0)),
            vec(DV_A), vec(DV_B),
            pl.BlockSpec((V_A + V_B, D), lambda i: (0, 0)),
            vec(D),
        ],
        out_specs=tok(D),
        out_shape=jax.ShapeDtypeStruct((T, D), F32),
        compiler_params=_params(("arbitrary",)),
        name="mixer_ab_out",
    )(o_f, o_b, r_a, o_att, x, mod, gla_g, diff_g, w_o, g_post)


def _gelu_tanh(x):
    return 0.5 * x * (1.0 + jnp.tanh(math.sqrt(2.0 / math.pi) * (x + 0.044715 * (x * x * x))))


def _sgu_kernel(x_ref, mod_ref, gpre_ref, win_ref, bin_ref, vg_ref, ws_ref, bs_ref,
                wout_ref, gpost_ref, o_ref, t_ref):
    m = mod_ref[0, 0]
    x = x_ref[...]
    h = _rms(x, gpre_ref[...]) * (1.0 + m[1:2]) + m[0:1]
    z = _gelu_tanh(_bdot(h, win_ref[...]) + bin_ref[...])
    v = _rms(z[:, SGU_DIM:], vg_ref[...])
    gw = SGU_DIM // SGU_GROUPS
    for ch in range(TM // SGU_CHUNK):
        rows = slice(ch * SGU_CHUNK, (ch + 1) * SGU_CHUNK)
        for g in range(SGU_GROUPS):
            cols = slice(g * gw, (g + 1) * gw)
            vs = _bdot(ws_ref[g], v[rows, cols]) + bs_ref[:, g:g + 1]
            t_ref[rows, cols] = (z[rows, cols] * vs).astype(BF16)
    out = jnp.dot(t_ref[...], wout_ref[...], preferred_element_type=F32)
    o_ref[...] = x + m[2:3] * _rms(out, gpost_ref[...])


def _sgu(x, mod, l, g_pre, w_in, b_in, v_g, w_s, b_s_t, w_out, g_post):
    tok = pl.BlockSpec((TM, D), lambda i: (i, 0))
    full = lambda *shape: pl.BlockSpec(shape, lambda i: (0,) * len(shape))
    return pl.pallas_call(
        _sgu_kernel,
        grid=(T // TM,),
        in_specs=[
            tok,
            pl.BlockSpec((1, 1, 6, D), lambda i: (l, _mod_row(i), 0, 0)),
            full(1, D), full(D, 2 * SGU_DIM), full(1, 2 * SGU_DIM), full(1, SGU_DIM),
            full(SGU_GROUPS, SGU_CHUNK, SGU_CHUNK), full(SGU_CHUNK, SGU_GROUPS),
            full(SGU_DIM, D), full(1, D),
        ],
        out_specs=tok,
        out_shape=jax.ShapeDtypeStruct((T, D), F32),
        scratch_shapes=[pltpu.VMEM((TM, SGU_DIM), BF16)],
        compiler_params=_params(("arbitrary",)),
        name="sgu_mixer",
    )(x, mod, g_pre, w_in, b_in, v_g, w_s, b_s_t, w_out, g_post)


LANES = 128
ROWS_PER_TOKEN = D // LANES


def _store_token_tiles(ref, x):
    n = x.shape[0]
    for c in range(ROWS_PER_TOKEN):
        ref[pl.ds(c, n, stride=ROWS_PER_TOKEN), :] = x[:, c * LANES:(c + 1) * LANES]


def _load_token_tiles(ref, n):
    return jnp.concatenate([ref[pl.ds(c, n, stride=ROWS_PER_TOKEN), :] for c in range(ROWS_PER_TOKEN)], axis=1)


def _token_rows(t):
    return pl.ds(pl.multiple_of(t * ROWS_PER_TOKEN, ROWS_PER_TOKEN), ROWS_PER_TOKEN)


def _router_kernel(x_ref, mod_ref, g_ref, wr_ref, eb_ref,
                   h_ref, te_ref, wn_ref, rk_ref, cnt_ref, carry_ref):
    i = pl.program_id(0)

    @pl.when(i == 0)
    def _():
        carry_ref[...] = jnp.zeros_like(carry_ref)

    m = mod_ref[0, 0]
    h = _rms(x_ref[...], g_ref[...]) * (1.0 + m[4:5]) + m[3:4]
    _store_token_tiles(h_ref, h)
    h1, h2, _ = _split3(h)
    w1, w2, _ = _split3(wr_ref[...])
    nt = lambda a, b: lax.dot_general(a, b, (((1,), (1,)), ((), ())), preferred_element_type=F32)
    logits = nt(w1, h1) + nt(w1, h2) + nt(w2, h1)
    scores = jax.nn.sigmoid(logits)
    sel = scores + eb_ref[...]

    row8 = lax.broadcasted_iota(I32, (GROUP_SIZE, TR), 0)
    gscore = []
    for g in range(N_GROUPS):
        xg = sel[g * GROUP_SIZE:(g + 1) * GROUP_SIZE]
        m1 = jnp.max(xg, axis=0, keepdims=True)
        i1 = jnp.min(jnp.where(xg == m1, row8, GROUP_SIZE), axis=0, keepdims=True)
        m2 = jnp.max(jnp.where(row8 == i1, NEG_INF, xg), axis=0, keepdims=True)
        gscore.append(m1 + m2)
    pieces = []
    for g in range(N_GROUPS):
        rank = jnp.zeros((1, TR), I32)
        for g2 in range(N_GROUPS):
            if g2 == g:
                continue
            beats = (gscore[g2] >= gscore[g]) if g2 < g else (gscore[g2] > gscore[g])
            rank = rank + beats.astype(I32)
        pieces.append(jnp.where(rank < TOPK_GROUPS, sel[g * GROUP_SIZE:(g + 1) * GROUP_SIZE], NEG_INF))
    cur = jnp.concatenate(pieces, axis=0)

    row = lax.broadcasted_iota(I32, (N_EXPERTS, TR), 0)
    idxs, ws = [], []
    for _ in range(TOP_K):
        mx = jnp.max(cur, axis=0, keepdims=True)
        idx = jnp.min(jnp.where(cur == mx, row, N_EXPERTS), axis=0, keepdims=True)
        hit = row == idx
        ws.append(jnp.sum(jnp.where(hit, scores, 0.0), axis=0, keepdims=True))
        cur = jnp.where(hit, NEG_INF, cur)
        idxs.append(idx)
    mask = jnp.zeros((N_EXPERTS, TR), F32)
    for idx in idxs:
        mask = mask + (row == idx).astype(F32)
    wsum = ws[0]
    for wk in ws[1:]:
        wsum = wsum + wk

    tj = lax.broadcasted_iota(I32, (TR, TR), 0)
    ti = lax.broadcasted_iota(I32, (TR, TR), 1)
    upper = jnp.where(tj < ti, 1.0, 0.0).astype(BF16)
    pos = carry_ref[...] + jnp.dot(mask.astype(BF16), upper, preferred_element_type=F32)
    for k in range(TOP_K):
        hit = row == idxs[k]
        te_ref[k:k + 1, :] = idxs[k]
        wn_ref[k:k + 1, :] = ws[k] / wsum * ROUTED_SCALE
        rk_ref[k:k + 1, :] = jnp.sum(jnp.where(hit, pos, 0.0), axis=0, keepdims=True).astype(I32)
    carry_ref[...] = carry_ref[...] + jnp.sum(mask, axis=1, keepdims=True)
    cnt_ref[...] = carry_ref[...]


def _router(x, mod, l, g, wr_t, e_bias):
    kt = lambda dtype: jax.ShapeDtypeStruct((TOP_K, T), dtype)
    kt_spec = pl.BlockSpec((TOP_K, TR), lambda i: (0, i))
    tiles_per_dec = DEC_SEQ // TR
    mod_row = lambda i: jnp.where(i < TP // TR, 0, 1 + (i - TP // TR) // tiles_per_dec)
    return pl.pallas_call(
        _router_kernel,
        grid=(T // TR,),
        in_specs=[
            pl.BlockSpec((TR, D), lambda i: (i, 0)),
            pl.BlockSpec((1, 1, 6, D), lambda i: (l, mod_row(i), 0, 0)),
            pl.BlockSpec((1, D), lambda i: (0, 0)),
            pl.BlockSpec((N_EXPERTS, D), lambda i: (0, 0)),
            pl.BlockSpec((N_EXPERTS, 1), lambda i: (0, 0)),
        ],
        out_specs=[
            pl.BlockSpec((TR * ROWS_PER_TOKEN, LANES), lambda i: (i, 0)),
            kt_spec, kt_spec, kt_spec,
            pl.BlockSpec((N_EXPERTS, 1), lambda i: (0, 0)),
        ],
        out_shape=[
            jax.ShapeDtypeStruct((T * ROWS_PER_TOKEN, LANES), F32), kt(I32), kt(F32), kt(I32),
            jax.ShapeDtypeStruct((N_EXPERTS, 1), F32),
        ],
        scratch_shapes=[pltpu.VMEM((N_EXPERTS, 1), F32)],
        compiler_params=_params(("arbitrary",)),
        name="moe_router",
    )(x, mod, g, wr_t, e_bias)


_PAD_BITS = tuple(1 << b for b in range(GM.bit_length() - 1))


def _pad_fill_kernel(pad_start_ref, pad_len_ref, xg_in_ref, xg_ref, zero_ref, sem):
    del xg_in_ref
    zero_ref[...] = jnp.zeros_like(zero_ref)

    def pad_copies(e):
        start = pad_start_ref[e]
        n = pad_len_ref[e]
        copies = []
        for bit in _PAD_BITS:
            first = start + (n & ~(2 * bit - 1))
            lo = pl.multiple_of(first * ROWS_PER_TOKEN, ROWS_PER_TOKEN)
            rows = bit * ROWS_PER_TOKEN
            copies.append(((n & bit) != 0, pltpu.make_async_copy(
                zero_ref.at[pl.ds(0, rows)], xg_ref.at[pl.ds(lo, rows)], sem)))
        return copies

    def start_e(e, carry):
        for on, cp in pad_copies(e):
            @pl.when(on)
            def _():
                cp.start()
        return carry

    def wait_e(e, carry):
        for on, cp in pad_copies(e):
            @pl.when(on)
            def _():
                cp.wait()
        return carry

    lax.fori_loop(0, N_EXPERTS, start_e, 0)
    lax.fori_loop(0, N_EXPERTS, wait_e, 0)


def _pad_fill(pad_start, pad_len, xg):
    grid_spec = pltpu.PrefetchScalarGridSpec(
        num_scalar_prefetch=2,
        grid=(1,),
        in_specs=[pl.BlockSpec(memory_space=pl.ANY)],
        out_specs=pl.BlockSpec(memory_space=pl.ANY),
        scratch_shapes=[pltpu.VMEM((GM // 2 * ROWS_PER_TOKEN, LANES), F32), pltpu.SemaphoreType.DMA],
    )
    return pl.pallas_call(
        _pad_fill_kernel,
        grid_spec=grid_spec,
        out_shape=jax.ShapeDtypeStruct(xg.shape, xg.dtype),
        input_output_aliases={2: 0},
        compiler_params=_params(("arbitrary",)),
        name="moe_pad_fill",
    )(pad_start, pad_len, xg)


SC_CORES, SC_SUBCORES = 2, 16
SC_WORKERS = SC_CORES * SC_SUBCORES
SC_W = 32


def _sc_worker_id():
    return lax.axis_index("s") * SC_CORES + lax.axis_index("c")


def _sc_dispatch(h3, slot3):
    nchunk = T // SC_WORKERS // SC_W
    mesh = plsc.VectorSubcoreMesh(core_axis_name="c", subcore_axis_name="s")
    tile = (SC_W, ROWS_PER_TOKEN, LANES)

    @functools.partial(
        pl.kernel, mesh=mesh,
        out_type=jax.ShapeDtypeStruct((SP, ROWS_PER_TOKEN, LANES), F32),
        scratch_types=[pltpu.VMEM((TOP_K, SC_W), I32), pltpu.VMEM((TOP_K, SC_W), I32),
                       pltpu.VMEM(tile, F32), pltpu.VMEM(tile, F32),
                       pltpu.SemaphoreType.DMA((2,)), pltpu.SemaphoreType.DMA((2,))],
    )
    def k(h_hbm, slot_hbm, xg_hbm, idx0, idx1, rows0, rows1, lsem, ssem):
        first = _sc_worker_id() * nchunk
        idx = (idx0, idx1)
        rows = (rows0, rows1)

        def loads(j, b):
            blk = first + j
            tok = pl.multiple_of(blk * SC_W, SC_W)
            return (pltpu.make_async_copy(slot_hbm.at[blk], idx[b], lsem.at[b]),
                    pltpu.make_async_copy(h_hbm.at[pl.ds(tok, SC_W)], rows[b], lsem.at[b]))

        def scatters(b):
            return [pltpu.make_async_copy(rows[b], xg_hbm.at[idx[b].at[kk]], ssem.at[b]) for kk in range(TOP_K)]

        for cp in loads(0, 0):
            cp.start()

        @pl.loop(0, nchunk, step=2)
        def _(j):
            for b in (0, 1):
                jj = j + b
                for cp in loads(jj, b):
                    cp.wait()
                for cp in scatters(b):
                    cp.start()

                @pl.when(jj + 1 < nchunk)
                def _():
                    @pl.when(jj >= 1)
                    def _():
                        for cp in scatters(1 - b):
                            cp.wait()
                    for cp in loads(jj + 1, 1 - b):
                        cp.start()

        for b in (0, 1):
            for cp in scatters(b):
                cp.wait()

    return k(h3, slot3)


def _gmm_kernel(tile_e_ref, tile_blk_ref, nvalid_ref, x_ref, wg_ref, wu_ref, wd_ref,
                y_ref, wgu_scr, wd_scr):
    j = pl.program_id(0)

    @pl.when(j < nvalid_ref[0])
    def _():
        prev = tile_e_ref[jnp.maximum(j - 1, 0)]

        @pl.when(jnp.logical_or(j == 0, tile_e_ref[j] != prev))
        def _():
            wgu_scr[:, 0:D_EXPERT] = wg_ref[0].astype(BF16)
            wgu_scr[:, D_EXPERT:2 * D_EXPERT] = wu_ref[0].astype(BF16)
            wd_scr[...] = wd_ref[0].astype(BF16)

        x = _load_token_tiles(x_ref, GM)
        gu = jnp.dot(x.astype(BF16), wgu_scr[...], preferred_element_type=F32)
        hid = _silu(gu[:, 0:D_EXPERT]) * gu[:, D_EXPERT:2 * D_EXPERT]
        _store_token_tiles(y_ref, jnp.dot(hid.astype(BF16), wd_scr[...], preferred_element_type=F32))


def _gmm(tile_e, tile_blk, nvalid, xg, w_gate, w_up, w_down):
    row_tile = pl.BlockSpec((GM * ROWS_PER_TOKEN, LANES), lambda j, te, tb, nv: (tb[j], 0))
    grid_spec = pltpu.PrefetchScalarGridSpec(
        num_scalar_prefetch=3,
        grid=(NT_MAX,),
        in_specs=[
            row_tile,
            pl.BlockSpec((1, D, D_EXPERT), lambda j, te, tb, nv: (te[j], 0, 0)),
            pl.BlockSpec((1, D, D_EXPERT), lambda j, te, tb, nv: (te[j], 0, 0)),
            pl.BlockSpec((1, D_EXPERT, D), lambda j, te, tb, nv: (te[j], 0, 0)),
        ],
        out_specs=row_tile,
        scratch_shapes=[pltpu.VMEM((D, 2 * D_EXPERT), BF16), pltpu.VMEM((D_EXPERT, D), BF16)],
    )
    return pl.pallas_call(
        _gmm_kernel,
        grid_spec=grid_spec,
        out_shape=jax.ShapeDtypeStruct((SP * ROWS_PER_TOKEN, LANES), F32),
        compiler_params=_params(("arbitrary",)),
        name="moe_grouped_matmul",
    )(tile_e, tile_blk, nvalid, xg, w_gate, w_up, w_down)


def _combine_kernel(slot_ref, slot_next_ref, wn_ref, x_ref, mod_ref, gpre_ref, gp_ref, wsg_ref, wsu_ref, wsd_ref,
                    yg_ref, o_ref, ybuf, sems):
    i = pl.program_id(0)
    last = pl.num_programs(0) - 1
    buf = i % 2

    def row_copy(slots, step, t, k):
        b = step % 2
        return pltpu.make_async_copy(yg_ref.at[_token_rows(slots[k, t])], ybuf.at[b, k, _token_rows(t)], sems.at[b])

    def start_step(slots, step):
        def body(t, carry):
            for k in range(TOP_K):
                row_copy(slots, step, t, k).start(priority=k % 2)
            return carry
        lax.fori_loop(0, TD, body, 0)

    def wait_step(slots, step):
        def body(t, carry):
            for k in range(TOP_K):
                row_copy(slots, step, t, k).wait()
            return carry
        lax.fori_loop(0, TD, body, 0)

    @pl.when(i == 0)
    def _():
        start_step(slot_ref, i)

    @pl.when(i < last)
    def _():
        start_step(slot_next_ref, i + 1)

    m = mod_ref[0, 0]
    x = x_ref[...]
    hb = (_rms(x, gpre_ref[...]) * (1.0 + m[4:5]) + m[3:4]).astype(BF16)
    hid = (_silu(jnp.dot(hb, wsg_ref[...], preferred_element_type=F32))
           * jnp.dot(hb, wsu_ref[...], preferred_element_type=F32))
    acc = jnp.dot(hid.astype(BF16), wsd_ref[...], preferred_element_type=F32)

    r = lax.broadcasted_iota(I32, (TD, TD), 0)
    c = lax.broadcasted_iota(I32, (TD, TD), 1)
    eye = jnp.where(r == c, 1.0, 0.0).astype(BF16)
    nt = lambda a, b: lax.dot_general(a, b, (((1,), (1,)), ((), ())), preferred_element_type=F32)
    w1, w2, w3 = _split3(wn_ref[...])
    w_t = nt(eye, w1) + nt(eye, w2) + nt(eye, w3)

    wait_step(slot_ref, i)
    for k in range(TOP_K):
        acc = acc + _load_token_tiles(ybuf.at[buf, k], TD) * w_t[:, k:k + 1]
    o_ref[...] = x + m[5:6] * _rms(acc, gp_ref[...])


def _combine(slot, wn, x, mod, l, g_pre, g_post, ws_gate, ws_up, ws_down, yg):
    tok = pl.BlockSpec((TD, D), lambda i: (i, 0))
    tiles_per_dec = DEC_SEQ // TD
    mod_row = lambda i: jnp.where(i < TP // TD, 0, 1 + (i - TP // TD) // tiles_per_dec)
    full = lambda *shape: pl.BlockSpec(shape, lambda i: (0,) * len(shape))
    return pl.pallas_call(
        _combine_kernel,
        grid=(T // TD,),
        in_specs=[
            pl.BlockSpec((TOP_K, TD), lambda i: (0, i), memory_space=pltpu.SMEM),
            pl.BlockSpec((TOP_K, TD), lambda i: (0, jnp.minimum(i + 1, T // TD - 1)), memory_space=pltpu.SMEM),
            pl.BlockSpec((TOP_K, TD), lambda i: (0, i)),
            tok,
            pl.BlockSpec((1, 1, 6, D), lambda i: (l, mod_row(i), 0, 0)),
            full(1, D), full(1, D), full(D, D_SHARED), full(D, D_SHARED), full(D_SHARED, D),
            pl.BlockSpec(memory_space=pl.ANY),
        ],
        out_specs=tok,
        out_shape=jax.ShapeDtypeStruct((T, D), F32),
        scratch_shapes=[pltpu.VMEM((2, TOP_K, TD * ROWS_PER_TOKEN, LANES), F32), pltpu.SemaphoreType.DMA((2,))],
        compiler_params=_params(("arbitrary",)),
        name="moe_combine",
    )(slot, slot, wn, x, mod, g_pre, g_post, ws_gate, ws_up, ws_down, yg)


def _moe_layer(x, mod, l, g_pre, g_post, w_router, e_bias, w_gate, w_up, w_down,
               ws_gate, ws_up, ws_down):
    h, top_e, wn, rk, cnt = _router(x, mod, l, g_pre, w_router.T, e_bias.reshape(N_EXPERTS, 1))
    cnt = cnt.reshape(N_EXPERTS).astype(I32)
    padded = (cnt + GM - 1) // GM * GM
    ends = jnp.cumsum(padded)
    offs = ends - padded
    eid = jnp.arange(N_EXPERTS, dtype=I32)[:, None, None]
    slot = rk + jnp.sum(jnp.where(top_e[None] == eid, offs[:, None, None], 0), axis=0)
    nvalid = ends[-1] // GM
    tile_start = jnp.arange(NT_MAX, dtype=I32) * GM
    tile_raw = jnp.sum((tile_start[:, None] >= ends[None, :]).astype(I32), axis=1)
    last = jnp.maximum(nvalid - 1, 0)
    tile_blk = jnp.minimum(jnp.arange(NT_MAX, dtype=I32), last)
    tile_e = jnp.minimum(tile_raw, N_EXPERTS - 1)
    tile_e = jnp.where(jnp.arange(NT_MAX) <= last, tile_e, tile_e[last])
    slot3 = slot.reshape(TOP_K, T // SC_W, SC_W).transpose(1, 0, 2)
    xg = _sc_dispatch(h.reshape(T, ROWS_PER_TOKEN, LANES), slot3)
    xg = _pad_fill(offs + cnt, padded - cnt, xg.reshape(SP * ROWS_PER_TOKEN, LANES))
    yg = _gmm(tile_e, tile_blk, nvalid.reshape(1), xg, w_gate, w_up, w_down)
    return _combine(slot, wn, x, mod, l, g_pre, g_post,
                    ws_gate.astype(BF16), ws_up.astype(BF16), ws_down.astype(BF16), yg)


def _rope_tables():
    n = DEC_SEQ
    rows = n // GRID_W
    row = jnp.repeat(jnp.arange(rows), GRID_W).astype(F32)
    col = jnp.tile(jnp.arange(GRID_W), rows).astype(F32)
    half = DQK_B // 2
    inv = ROPE_BASE ** (-jnp.arange(0, half, 2, dtype=F32) / half)
    ang_r = row[:, None] * inv
    ang_c = col[:, None] * inv
    ang = jnp.concatenate([ang_r, ang_r, ang_c, ang_c], axis=-1)
    reps = QK_B // DQK_B
    return jnp.tile(jnp.cos(ang), (1, reps)), jnp.tile(jnp.sin(ang), (1, reps))


def _pad_in_proj(w):
    s = [0, Q_A, 2 * Q_A, 2 * Q_A + V_A, 2 * Q_A + 2 * V_A]
    s += [s[-1] + GATE_RANK, s[-1] + 2 * GATE_RANK]
    s += [s[-1] + QK_B, s[-1] + 2 * QK_B, s[-1] + 2 * QK_B + V_B]
    gates = jnp.pad(w[:, s[4]:s[6]], ((0, 0), (0, GL_PAD - 2 * GATE_RANK)))
    return jnp.concatenate([w[:, s[0]:s[4]], gates, w[:, s[6]:s[9]]], axis=1).astype(BF16)


def kernel(x_prompt, x_sample, c, c_ctx, state_gla, cache_k, cache_v, ada_w, ada_b, norm_pre_mix, norm_post_mix, norm_pre_ffn, norm_post_ffn, ab_w_in, gla_w_g2, gla_b_g2, gla_norm_g, diff_lambda, diff_norm_g, ab_w_out, sgu_w_in, sgu_b_in, sgu_norm_g, sgu_w_s, sgu_b_s, sgu_w_out, moe_w_router, moe_e_bias, moe_w_gate, moe_w_up, moe_w_down, moe_ws_gate, moe_ws_up, moe_ws_down):
    depth = ada_w.shape[0]
    x = jnp.concatenate([x_prompt.reshape(TP, D), x_sample.reshape(TS, D)], axis=0)
    cond = jnp.concatenate([c_ctx[None, :], c, jnp.zeros((8 - 1 - DEC_BATCH, D), F32)], axis=0)
    mod = _modulation(cond, ada_w, ada_b)
    cos, sin = _rope_tables()
    vec = lambda a: a.reshape(1, -1)
    new_s = new_k = new_v = None
    for l in range(depth):
        if l % 2 == 0:
            e = l // 2
            lam_init = 0.8 - 0.6 * math.exp(-0.3 * l)
            a, r_a, gl, q_b, k_b, v_b = _in_proj(x, mod, l, vec(norm_pre_mix[l]), _pad_in_proj(ab_w_in[e]), cos, sin)
            s0_t = jnp.swapaxes(state_gla[:, e], -1, -2)
            o_f, o_bw, s_fin_t = _gla(a, gl, gla_w_g2[e], gla_b_g2[e].reshape(2, 1, Q_A), s0_t)
            o_att = _diff_attention(q_b, k_b, v_b, cache_k, cache_v, diff_lambda[e], lam_init)
            x = _mix_out(lam_init, o_f, o_bw, r_a, o_att, x, mod, l, vec(gla_norm_g[e]), vec(diff_norm_g[e]),
                         ab_w_out[e].astype(BF16), vec(norm_post_mix[l]))
            new_s = jnp.swapaxes(s_fin_t, -1, -2)
            new_k = k_b[:TP].reshape(BATCH, SEQ, H_B, 2, DQK_B).transpose(0, 2, 3, 1, 4)
            new_v = v_b[:TP].reshape(BATCH, SEQ, H_B, DV_B).transpose(0, 2, 1, 3)
        else:
            o = l // 2
            x = _sgu(x, mod, l, vec(norm_pre_mix[l]), sgu_w_in[o].astype(BF16), vec(sgu_b_in[o]),
                     vec(sgu_norm_g[o]), sgu_w_s[o], sgu_b_s[o].T, sgu_w_out[o].astype(BF16),
                     vec(norm_post_mix[l]))
        x = _moe_layer(x, mod, l, vec(norm_pre_ffn[l]), vec(norm_post_ffn[l]), moe_w_router[l], moe_e_bias[l],
                       moe_w_gate[l], moe_w_up[l], moe_w_down[l], moe_ws_gate[l], moe_ws_up[l], moe_ws_down[l])
    y_prompt = x[:TP].reshape(BATCH, SEQ, D)
    y_sample = x[TP:].reshape(DEC_BATCH, DEC_SEQ, D)
    return (y_prompt, y_sample, new_s[:, None], new_k[:, None], new_v[:, None])
```

```python
import functools
import math

import jax
import jax.numpy as jnp
from jax import lax
from jax.experimental import pallas as pl
from jax.experimental.pallas import tpu as pltpu
from jax.experimental.pallas import tpu_sc as plsc

F32 = jnp.float32
BF16 = jnp.bfloat16
I32 = jnp.int32

D = 1024
BATCH, SEQ = 32, 256
DEC_BATCH, DEC_SEQ = 4, 2048
PAST_LEN = 256
GRID_W = 64
EPS = 1e-6
TP = BATCH * SEQ
TS = DEC_BATCH * DEC_SEQ
T = TP + TS
H_A, DK_A, DV_A = 4, 64, 128
Q_A, V_A = H_A * DK_A, H_A * DV_A
GATE_RANK, GATE_TAU, GLA_CHUNK = 16, 16.0, 64
H_B, DQK_B, DV_B = 4, 64, 128
QK_B, V_B = H_B * 2 * DQK_B, H_B * DV_B
ROPE_BASE = 10000.0
SGU_DIM, SGU_GROUPS, SGU_CHUNK = 1024, 4, 128
N_EXPERTS, TOP_K, N_GROUPS, TOPK_GROUPS = 64, 8, 8, 4
GROUP_SIZE = N_EXPERTS // N_GROUPS
D_EXPERT, D_SHARED = 256, 256
ROUTED_SCALE = 2.5

TM = 512
NPT = TP // TM
TILES_PER_DEC = DEC_SEQ // TM
SEG = 256
NSEG = T // SEG
NSEG_P = TP // SEG
SEG_PER_DEC = DEC_SEQ // SEG
TR = 512
TD = 256
GM = 256
NT_MAX = T * TOP_K // GM + N_EXPERTS
SP = NT_MAX * GM
GL_PAD = 128
VMEM_LIMIT = 56 * 1024 * 1024
NEG_INF = float("-inf")


def _bdot(a, b):
    return jnp.dot(a.astype(BF16), b.astype(BF16), preferred_element_type=F32)


def _bdot_nt(a, b):
    return lax.dot_general(a.astype(BF16), b.astype(BF16), (((1,), (1,)), ((), ())),
                           preferred_element_type=F32)


def _bdot_tn(a, b):
    return lax.dot_general(a.astype(BF16), b.astype(BF16), (((0,), (0,)), ((), ())),
                           preferred_element_type=F32)


def _split3(x):
    x1 = x.astype(BF16)
    r1 = x - x1.astype(F32)
    x2 = r1.astype(BF16)
    x3 = (r1 - x2.astype(F32)).astype(BF16)
    return x1, x2, x3


def _rms(x, g):
    return x * lax.rsqrt(jnp.mean(x * x, axis=-1, keepdims=True) + EPS) * g


def _silu(x):
    return x * jax.nn.sigmoid(x)


def _mod_row(i):
    return jnp.where(i < NPT, 0, 1 + (i - NPT) // TILES_PER_DEC)


def _params(sem, limit=VMEM_LIMIT):
    return pltpu.CompilerParams(dimension_semantics=sem, vmem_limit_bytes=limit)


def _mod_kernel(c_ref, w_ref, b_ref, o_ref):
    o_ref[0] = _bdot(_silu(c_ref[...]), w_ref[0]) + b_ref[0]


def _modulation(cond, ada_w, ada_b):
    depth = ada_w.shape[0]
    nj = 6
    out = pl.pallas_call(
        _mod_kernel,
        grid=(depth, nj),
        in_specs=[
            pl.BlockSpec((8, D), lambda l, j: (0, 0)),
            pl.BlockSpec((1, D, D), lambda l, j: (l, 0, j)),
            pl.BlockSpec((1, 1, D), lambda l, j: (l, 0, j)),
        ],
        out_specs=pl.BlockSpec((1, 8, D), lambda l, j: (l, 0, j)),
        out_shape=jax.ShapeDtypeStruct((depth, 8, 6 * D), F32),
        compiler_params=_params(("arbitrary", "arbitrary")),
        name="adaln_modulation",
    )(cond, ada_w, ada_b.reshape(depth, 1, 6 * D))
    return out.reshape(depth, 8, 6, D)


_C_A, _C_R, _C_GL, _C_Q, _C_K, _C_V, _C_END = 0, 1024, 1536, 1664, 2176, 2688, 3200


def _rope(x, cos, sin):
    lane = lax.broadcasted_iota(I32, x.shape, 1)
    first = (lane % 32) < 16
    n = x.shape[1]
    xr = jnp.where(first, -pltpu.roll(x, n - 16, 1), pltpu.roll(x, 16, 1))
    return x * cos + xr * sin


def _in_kernel(x_ref, mod_ref, g_ref, w_ref, cos_ref, sin_ref,
               a_ref, r_ref, gl_ref, q_ref, k_ref, v_ref):
    i = pl.program_id(0)
    m = mod_ref[0, 0]
    h = _rms(x_ref[...], g_ref[...]) * (1.0 + m[1:2]) + m[0:1]
    hb = h.astype(BF16)

    def proj(c0, c1):
        return jnp.dot(hb, w_ref[:, c0:c1], preferred_element_type=F32)

    a_ref[...] = proj(_C_A, _C_R)
    r_ref[...] = proj(_C_R, _C_GL)
    gl_ref[...] = proj(_C_GL, _C_Q)
    v_ref[...] = proj(_C_V, _C_END)
    q = proj(_C_Q, _C_K)
    k = proj(_C_K, _C_V)

    @pl.when(i < NPT)
    def _():
        q_ref[...] = q
        k_ref[...] = k

    @pl.when(i >= NPT)
    def _():
        cos = cos_ref[...]
        sin = sin_ref[...]
        q_ref[...] = _rope(q, cos, sin)
        k_ref[...] = _rope(k, cos, sin)


def _in_proj(x, mod, l, g, w_pad, cos, sin):
    tok = lambda width: pl.BlockSpec((TM, width), lambda i: (i, 0))
    rope_spec = pl.BlockSpec((TM, QK_B), lambda i: (jnp.maximum(i - NPT, 0) % TILES_PER_DEC, 0))
    widths = (1024, 512, GL_PAD, 512, 512, 512)
    return pl.pallas_call(
        _in_kernel,
        grid=(T // TM,),
        in_specs=[
            tok(D),
            pl.BlockSpec((1, 1, 6, D), lambda i: (l, _mod_row(i), 0, 0)),
            pl.BlockSpec((1, D), lambda i: (0, 0)),
            pl.BlockSpec((D, _C_END), lambda i: (0, 0)),
            rope_spec, rope_spec,
        ],
        out_specs=[tok(w) for w in widths],
        out_shape=[jax.ShapeDtypeStruct((T, w), F32) for w in widths],
        compiler_params=_params(("arbitrary",)),
        name="mixer_ab_in_proj",
    )(x, mod, g, w_pad, cos, sin)


def _log_sigmoid(x):
    return jnp.minimum(x, 0.0) - jnp.log1p(jnp.exp(-jnp.abs(x)))


def _gla_kernel(af_ref, ab_ref, glf_ref, glb_ref, wg_ref, bg_ref, s0_ref,
                of_ref, ob_ref, sfin_ref, st_ref):
    i = pl.program_id(0)

    @pl.when(i < NSEG_P)
    def _():
        st_ref[...] = jnp.zeros_like(st_ref)

    @pl.when(jnp.logical_and(i >= NSEG_P, (i - NSEG_P) % SEG_PER_DEC == 0))
    def _():
        st_ref[...] = s0_ref[0]

    r = lax.broadcasted_iota(I32, (SEG, SEG), 0)
    c = lax.broadcasted_iota(I32, (SEG, SEG), 1)
    same = (r // GLA_CHUNK) == (c // GLA_CHUNK)
    rc = lax.broadcasted_iota(I32, (GLA_CHUNK, GLA_CHUNK), 0)
    cc = lax.broadcasted_iota(I32, (GLA_CHUNK, GLA_CHUNK), 1)
    nchunk = SEG // GLA_CHUNK

    for d, (a_ref, gl_ref, o_ref) in enumerate(((af_ref, glf_ref, of_ref), (ab_ref, glb_ref, ob_ref))):
        fwd = d == 0
        gcol = gl_ref[:, d * GATE_RANK:(d + 1) * GATE_RANK]
        la = _log_sigmoid(_bdot(gcol, wg_ref[d]) + bg_ref[d]) / GATE_TAU
        tri = jnp.where(jnp.logical_and(same, (c <= r) if fwd else (c >= r)), 1.0, 0.0).astype(BF16)
        l1, l2, l3 = _split3(la)
        b_all = (jnp.dot(tri, l1, preferred_element_type=F32)
                 + jnp.dot(tri, l2, preferred_element_type=F32)
                 + jnp.dot(tri, l3, preferred_element_type=F32))
        keep = (cc <= rc) if fwd else (cc >= rc)
        order = range(nchunk) if fwd else range(nchunk - 1, -1, -1)
        for ch in order:
            r0 = ch * GLA_CHUNK
            rows = slice(r0, r0 + GLA_CHUNK)
            for h in range(H_A):
                kc = slice(h * DK_A, (h + 1) * DK_A)
                q = a_ref[rows, h * DK_A:(h + 1) * DK_A] * (DK_A ** -0.5)
                k = a_ref[rows, Q_A + h * DK_A:Q_A + (h + 1) * DK_A]
                v = a_ref[rows, 2 * Q_A + h * DV_A:2 * Q_A + (h + 1) * DV_A]
                b = b_all[rows, kc]
                b_end = b[GLA_CHUNK - 1:GLA_CHUNK, :] if fwd else b[0:1, :]
                q_in = q * jnp.exp(b)
                attn = jnp.where(keep, _bdot_nt(q_in, k * jnp.exp(-b)), 0.0)
                s_t = st_ref[d, h]
                o_ref[rows, h * DV_A:(h + 1) * DV_A] = _bdot(attn, v) + _bdot_nt(q_in, s_t)
                st_ref[d, h] = s_t * jnp.exp(b_end) + _bdot_tn(v, k * jnp.exp(b_end - b))

    @pl.when(i < NSEG_P)
    def _():
        sfin_ref[0] = st_ref[...]


def _seg_bwd(i):
    j = i - NSEG_P
    return jnp.where(i < NSEG_P, i, NSEG_P + (j // SEG_PER_DEC) * SEG_PER_DEC + (SEG_PER_DEC - 1 - j % SEG_PER_DEC))


def _gla(a, gl, wg, bg, s0_t):
    seg = lambda width, f: pl.BlockSpec((SEG, width), lambda i: (f(i), 0))
    ident = lambda i: i
    st_block = (1, 2, H_A, DV_A, DK_A)
    return pl.pallas_call(
        _gla_kernel,
        grid=(NSEG,),
        in_specs=[
            seg(D, ident), seg(D, _seg_bwd), seg(GL_PAD, ident), seg(GL_PAD, _seg_bwd),
            pl.BlockSpec((2, GATE_RANK, Q_A), lambda i: (0, 0, 0)),
            pl.BlockSpec((2, 1, Q_A), lambda i: (0, 0, 0)),
            pl.BlockSpec(st_block, lambda i: (jnp.maximum(i - NSEG_P, 0) // SEG_PER_DEC, 0, 0, 0, 0)),
        ],
        out_specs=[
            seg(V_A, ident), seg(V_A, _seg_bwd),
            pl.BlockSpec(st_block, lambda i: (jnp.minimum(i, NSEG_P - 1), 0, 0, 0, 0)),
        ],
        out_shape=[
            jax.ShapeDtypeStruct((T, V_A), F32),
            jax.ShapeDtypeStruct((T, V_A), F32),
            jax.ShapeDtypeStruct((BATCH, 2, H_A, DV_A, DK_A), F32),
        ],
        scratch_shapes=[pltpu.VMEM((2, H_A, DV_A, DK_A), F32)],
        compiler_params=_params(("arbitrary",)),
        name="gla_bidir",
    )(a, a, gl, gl, wg, bg, s0_t)


def _diff_lambda(lam_ref, lam_init):
    lp = lam_ref[...]
    s01 = jnp.sum(lp[0:1] * lp[1:2], axis=1, keepdims=True)
    s23 = jnp.sum(lp[2:3] * lp[3:4], axis=1, keepdims=True)
    return jnp.exp(s01) - jnp.exp(s23) + lam_init


def _attn_prompt_kernel(lam_init, q_ref, k_ref, v_ref, lam_ref, o_ref):
    lam = _diff_lambda(lam_ref, lam_init)
    scale = DQK_B ** -0.5
    for h in range(H_B):
        ps = []
        for m in range(2):
            cols = slice((2 * h + m) * DQK_B, (2 * h + m + 1) * DQK_B)
            s = _bdot_nt(q_ref[:, cols], k_ref[:, cols]) * scale
            e = jnp.exp(s - jnp.max(s, axis=1, keepdims=True))
            ps.append(e / jnp.sum(e, axis=1, keepdims=True))
        w = ps[0] - lam * ps[1]
        o_ref[:, h * DV_B:(h + 1) * DV_B] = _bdot(w, v_ref[:, h * DV_B:(h + 1) * DV_B])


def _attn_sample_kernel(lam_init, q_ref, k_ref, v_ref, ck_ref, cv_ref, lam_ref, o_ref):
    lam = _diff_lambda(lam_ref, lam_init)
    scale = DQK_B ** -0.5
    for h in range(H_B):
        parts = []
        for m in range(2):
            cols = slice((2 * h + m) * DQK_B, (2 * h + m + 1) * DQK_B)
            q = q_ref[:, cols]
            sc = _bdot_nt(q, ck_ref[0, 0, h, m]) * scale
            sn = _bdot_nt(q, k_ref[:, cols]) * scale
            mx = jnp.maximum(jnp.max(sc, axis=1, keepdims=True), jnp.max(sn, axis=1, keepdims=True))
            ec = jnp.exp(sc - mx)
            en = jnp.exp(sn - mx)
            den = jnp.sum(ec, axis=1, keepdims=True) + jnp.sum(en, axis=1, keepdims=True)
            parts.append((ec / den, en / den))
        wc = parts[0][0] - lam * parts[1][0]
        wn = parts[0][1] - lam * parts[1][1]
        o_ref[:, h * DV_B:(h + 1) * DV_B] = (_bdot(wc, cv_ref[0, 0, h])
                                             + _bdot(wn, v_ref[:, h * DV_B:(h + 1) * DV_B]))


QB = SEQ
NQB_DEC = DEC_SEQ // QB


def _attn_kernel(lam_init, q_ref, kp_ref, vp_ref, ks_ref, vs_ref, ck_ref, cv_ref, lam_ref, o_ref):
    i = pl.program_id(0)

    @pl.when(i < BATCH)
    def _():
        _attn_prompt_kernel(lam_init, q_ref, kp_ref, vp_ref, lam_ref, o_ref)

    @pl.when(i >= BATCH)
    def _():
        _attn_sample_kernel(lam_init, q_ref, ks_ref, vs_ref, ck_ref, cv_ref, lam_ref, o_ref)


def _diff_attention(q, k, v, cache_k, cache_v, lam_p, lam_init):
    blk = lambda rows, f: pl.BlockSpec((rows, 512), f)
    dec_b = lambda i: jnp.maximum(i - BATCH, 0) // NQB_DEC
    own = lambda i: (i, 0)
    prompt_kv = lambda i: (jnp.minimum(i, BATCH - 1), 0)
    dec_kv = lambda i: (TP // DEC_SEQ + dec_b(i), 0)
    return pl.pallas_call(
        functools.partial(_attn_kernel, lam_init),
        grid=(BATCH + DEC_BATCH * NQB_DEC,),
        in_specs=[
            blk(QB, own), blk(SEQ, prompt_kv), blk(SEQ, prompt_kv), blk(DEC_SEQ, dec_kv), blk(DEC_SEQ, dec_kv),
            pl.BlockSpec((1, 1, H_B, 2, PAST_LEN, DQK_B), lambda i: (dec_b(i), 0, 0, 0, 0, 0)),
            pl.BlockSpec((1, 1, H_B, PAST_LEN, DV_B), lambda i: (dec_b(i), 0, 0, 0, 0)),
            pl.BlockSpec((4, DQK_B), lambda i: (0, 0)),
        ],
        out_specs=blk(QB, own),
        out_shape=jax.ShapeDtypeStruct((T, V_B), F32),
        compiler_params=_params(("arbitrary",)),
        name="diff_attention",
    )(q, k, v, k, v, cache_k, cache_v, lam_p)


def _head_rms(x, g, nheads, width):
    return jnp.concatenate([_rms(x[:, h * width:(h + 1) * width], g) for h in range(nheads)], axis=1)


def _mix_out_kernel(lam_init, of_ref, ob_ref, r_ref, oatt_ref, x_ref, mod_ref,
                    gg_ref, dg_ref, wo_ref, gp_ref, o_ref):
    m = mod_ref[0, 0]
    o_a = _head_rms(of_ref[...] + ob_ref[...], gg_ref[...], H_A, DV_A) * _silu(r_ref[...])
    o_b = _head_rms(oatt_ref[...], dg_ref[...], H_B, DV_B) * (1.0 - lam_init)
    out = _bdot(o_a, wo_ref[0:V_A, :]) + _bdot(o_b, wo_ref[V_A:V_A + V_B, :])
    o_ref[...] = x_ref[...] + m[2:3] * _rms(out, gp_ref[...])


def _mix_out(lam_init, o_f, o_b, r_a, o_att, x, mod, l, gla_g, diff_g, w_o, g_post):
    tok = lambda width: pl.BlockSpec((TM, width), lambda i: (i, 0))
    vec = lambda width: pl.BlockSpec((1, width), lambda i: (0, 0))
    return pl.pallas_call(
        functools.partial(_mix_out_kernel, lam_init),
        grid=(T // TM,),
        in_specs=[
            tok(512), tok(512), tok(512), tok(512), tok(D),
            pl.BlockSpec((1, 1, 6, D), lambda i: (l, _mod_row(i), 0, 0)),
            vec(DV_A), vec(DV_B),
            pl.BlockSpec((V_A + V_B, D), lambda i: (0, 0)),
            vec(D),
        ],
        out_specs=tok(D),
        out_shape=jax.ShapeDtypeStruct((T, D), F32),
        compiler_params=_params(("arbitrary",)),
        name="mixer_ab_out",
    )(o_f, o_b, r_a, o_att, x, mod, gla_g, diff_g, w_o, g_post)


def _gelu_tanh(x):
    return 0.5 * x * (1.0 + jnp.tanh(math.sqrt(2.0 / math.pi) * (x + 0.044715 * (x * x * x))))


def _sgu_kernel(x_ref, mod_ref, gpre_ref, win_ref, bin_ref, vg_ref, ws_ref, bs_ref,
                wout_ref, gpost_ref, o_ref, t_ref):
    m = mod_ref[0, 0]
    x = x_ref[...]
    h = _rms(x, gpre_ref[...]) * (1.0 + m[1:2]) + m[0:1]
    z = _gelu_tanh(_bdot(h, win_ref[...]) + bin_ref[...])
    v = _rms(z[:, SGU_DIM:], vg_ref[...])
    gw = SGU_DIM // SGU_GROUPS
    for ch in range(TM // SGU_CHUNK):
        rows = slice(ch * SGU_CHUNK, (ch + 1) * SGU_CHUNK)
        for g in range(SGU_GROUPS):
            cols = slice(g * gw, (g + 1) * gw)
            vs = _bdot(ws_ref[g], v[rows, cols]) + bs_ref[:, g:g + 1]
            t_ref[rows, cols] = (z[rows, cols] * vs).astype(BF16)
    out = jnp.dot(t_ref[...], wout_ref[...], preferred_element_type=F32)
    o_ref[...] = x + m[2:3] * _rms(out, gpost_ref[...])


def _sgu(x, mod, l, g_pre, w_in, b_in, v_g, w_s, b_s_t, w_out, g_post):
    tok = pl.BlockSpec((TM, D), lambda i: (i, 0))
    full = lambda *shape: pl.BlockSpec(shape, lambda i: (0,) * len(shape))
    return pl.pallas_call(
        _sgu_kernel,
        grid=(T // TM,),
        in_specs=[
            tok,
            pl.BlockSpec((1, 1, 6, D), lambda i: (l, _mod_row(i), 0, 0)),
            full(1, D), full(D, 2 * SGU_DIM), full(1, 2 * SGU_DIM), full(1, SGU_DIM),
            full(SGU_GROUPS, SGU_CHUNK, SGU_CHUNK), full(SGU_CHUNK, SGU_GROUPS),
            full(SGU_DIM, D), full(1, D),
        ],
        out_specs=tok,
        out_shape=jax.ShapeDtypeStruct((T, D), F32),
        scratch_shapes=[pltpu.VMEM((TM, SGU_DIM), BF16)],
        compiler_params=_params(("arbitrary",)),
        name="sgu_mixer",
    )(x, mod, g_pre, w_in, b_in, v_g, w_s, b_s_t, w_out, g_post)


LANES = 128
ROWS_PER_TOKEN = D // LANES


def _store_token_tiles(ref, x):
    n = x.shape[0]
    for c in range(ROWS_PER_TOKEN):
        ref[pl.ds(c, n, stride=ROWS_PER_TOKEN), :] = x[:, c * LANES:(c + 1) * LANES]


def _load_token_tiles(ref, n):
    return jnp.concatenate([ref[pl.ds(c, n, stride=ROWS_PER_TOKEN), :] for c in range(ROWS_PER_TOKEN)], axis=1)


def _token_rows(t):
    return pl.ds(pl.multiple_of(t * ROWS_PER_TOKEN, ROWS_PER_TOKEN), ROWS_PER_TOKEN)


def _router_kernel(x_ref, mod_ref, g_ref, wr_ref, eb_ref,
                   h_ref, te_ref, wn_ref, rk_ref, cnt_ref, carry_ref):
    i = pl.program_id(0)

    @pl.when(i == 0)
    def _():
        carry_ref[...] = jnp.zeros_like(carry_ref)

    m = mod_ref[0, 0]
    h = _rms(x_ref[...], g_ref[...]) * (1.0 + m[4:5]) + m[3:4]
    _store_token_tiles(h_ref, h)
    h1, h2, _ = _split3(h)
    w1, w2, _ = _split3(wr_ref[...])
    nt = lambda a, b: lax.dot_general(a, b, (((1,), (1,)), ((), ())), preferred_element_type=F32)
    logits = nt(w1, h1) + nt(w1, h2) + nt(w2, h1)
    scores = jax.nn.sigmoid(logits)
    sel = scores + eb_ref[...]

    row8 = lax.broadcasted_iota(I32, (GROUP_SIZE, TR), 0)
    gscore = []
    for g in range(N_GROUPS):
        xg = sel[g * GROUP_SIZE:(g + 1) * GROUP_SIZE]
        m1 = jnp.max(xg, axis=0, keepdims=True)
        i1 = jnp.min(jnp.where(xg == m1, row8, GROUP_SIZE), axis=0, keepdims=True)
        m2 = jnp.max(jnp.where(row8 == i1, NEG_INF, xg), axis=0, keepdims=True)
        gscore.append(m1 + m2)
    pieces = []
    for g in range(N_GROUPS):
        rank = jnp.zeros((1, TR), I32)
        for g2 in range(N_GROUPS):
            if g2 == g:
                continue
            beats = (gscore[g2] >= gscore[g]) if g2 < g else (gscore[g2] > gscore[g])
            rank = rank + beats.astype(I32)
        pieces.append(jnp.where(rank < TOPK_GROUPS, sel[g * GROUP_SIZE:(g + 1) * GROUP_SIZE], NEG_INF))
    cur = jnp.concatenate(pieces, axis=0)

    row = lax.broadcasted_iota(I32, (N_EXPERTS, TR), 0)
    idxs, ws = [], []
    for _ in range(TOP_K):
        mx = jnp.max(cur, axis=0, keepdims=True)
        idx = jnp.min(jnp.where(cur == mx, row, N_EXPERTS), axis=0, keepdims=True)
        hit = row == idx
        ws.append(jnp.sum(jnp.where(hit, scores, 0.0), axis=0, keepdims=True))
        cur = jnp.where(hit, NEG_INF, cur)
        idxs.append(idx)
    mask = jnp.zeros((N_EXPERTS, TR), F32)
    for idx in idxs:
        mask = mask + (row == idx).astype(F32)
    wsum = ws[0]
    for wk in ws[1:]:
        wsum = wsum + wk

    tj = lax.broadcasted_iota(I32, (TR, TR), 0)
    ti = lax.broadcasted_iota(I32, (TR, TR), 1)
    upper = jnp.where(tj < ti, 1.0, 0.0).astype(BF16)
    pos = carry_ref[...] + jnp.dot(mask.astype(BF16), upper, preferred_element_type=F32)
    for k in range(TOP_K):
        hit = row == idxs[k]
        te_ref[k:k + 1, :] = idxs[k]
        wn_ref[k:k + 1, :] = ws[k] / wsum * ROUTED_SCALE
        rk_ref[k:k + 1, :] = jnp.sum(jnp.where(hit, pos, 0.0), axis=0, keepdims=True).astype(I32)
    carry_ref[...] = carry_ref[...] + jnp.sum(mask, axis=1, keepdims=True)
    cnt_ref[...] = carry_ref[...]


def _router(x, mod, l, g, wr_t, e_bias):
    kt = lambda dtype: jax.ShapeDtypeStruct((TOP_K, T), dtype)
    kt_spec = pl.BlockSpec((TOP_K, TR), lambda i: (0, i))
    tiles_per_dec = DEC_SEQ // TR
    mod_row = lambda i: jnp.where(i < TP // TR, 0, 1 + (i - TP // TR) // tiles_per_dec)
    return pl.pallas_call(
        _router_kernel,
        grid=(T // TR,),
        in_specs=[
            pl.BlockSpec((TR, D), lambda i: (i, 0)),
            pl.BlockSpec((1, 1, 6, D), lambda i: (l, mod_row(i), 0, 0)),
            pl.BlockSpec((1, D), lambda i: (0, 0)),
            pl.BlockSpec((N_EXPERTS, D), lambda i: (0, 0)),
            pl.BlockSpec((N_EXPERTS, 1), lambda i: (0, 0)),
        ],
        out_specs=[
            pl.BlockSpec((TR * ROWS_PER_TOKEN, LANES), lambda i: (i, 0)),
            kt_spec, kt_spec, kt_spec,
            pl.BlockSpec((N_EXPERTS, 1), lambda i: (0, 0)),
        ],
        out_shape=[
            jax.ShapeDtypeStruct((T * ROWS_PER_TOKEN, LANES), F32), kt(I32), kt(F32), kt(I32),
            jax.ShapeDtypeStruct((N_EXPERTS, 1), F32),
        ],
        scratch_shapes=[pltpu.VMEM((N_EXPERTS, 1), F32)],
        compiler_params=_params(("arbitrary",)),
        name="moe_router",
    )(x, mod, g, wr_t, e_bias)


_PAD_BITS = tuple(1 << b for b in range(GM.bit_length() - 1))


def _pad_fill_kernel(pad_start_ref, pad_len_ref, xg_in_ref, xg_ref, zero_ref, sem):
    del xg_in_ref
    zero_ref[...] = jnp.zeros_like(zero_ref)

    def pad_copies(e):
        start = pad_start_ref[e]
        n = pad_len_ref[e]
        copies = []
        for bit in _PAD_BITS:
            first = start + (n & ~(2 * bit - 1))
            lo = pl.multiple_of(first * ROWS_PER_TOKEN, ROWS_PER_TOKEN)
            rows = bit * ROWS_PER_TOKEN
            copies.append(((n & bit) != 0, pltpu.make_async_copy(
                zero_ref.at[pl.ds(0, rows)], xg_ref.at[pl.ds(lo, rows)], sem)))
        return copies

    def start_e(e, carry):
        for on, cp in pad_copies(e):
            @pl.when(on)
            def _():
                cp.start()
        return carry

    def wait_e(e, carry):
        for on, cp in pad_copies(e):
            @pl.when(on)
            def _():
                cp.wait()
        return carry

    lax.fori_loop(0, N_EXPERTS, start_e, 0)
    lax.fori_loop(0, N_EXPERTS, wait_e, 0)


def _pad_fill(pad_start, pad_len, xg):
    grid_spec = pltpu.PrefetchScalarGridSpec(
        num_scalar_prefetch=2,
        grid=(1,),
        in_specs=[pl.BlockSpec(memory_space=pl.ANY)],
        out_specs=pl.BlockSpec(memory_space=pl.ANY),
        scratch_shapes=[pltpu.VMEM((GM // 2 * ROWS_PER_TOKEN, LANES), F32), pltpu.SemaphoreType.DMA],
    )
    return pl.pallas_call(
        _pad_fill_kernel,
        grid_spec=grid_spec,
        out_shape=jax.ShapeDtypeStruct(xg.shape, xg.dtype),
        input_output_aliases={2: 0},
        compiler_params=_params(("arbitrary",)),
        name="moe_pad_fill",
    )(pad_start, pad_len, xg)


SC_CORES, SC_SUBCORES = 2, 16
SC_WORKERS = SC_CORES * SC_SUBCORES
SC_W = 32


def _sc_worker_id():
    return lax.axis_index("s") * SC_CORES + lax.axis_index("c")


def _sc_dispatch(h3, slot3):
    nchunk = T // SC_WORKERS // SC_W
    mesh = plsc.VectorSubcoreMesh(core_axis_name="c", subcore_axis_name="s")
    tile = (SC_W, ROWS_PER_TOKEN, LANES)

    @functools.partial(
        pl.kernel, mesh=mesh,
        out_type=jax.ShapeDtypeStruct((SP, ROWS_PER_TOKEN, LANES), F32),
        scratch_types=[pltpu.VMEM((TOP_K, SC_W), I32), pltpu.VMEM((TOP_K, SC_W), I32),
                       pltpu.VMEM(tile, F32), pltpu.VMEM(tile, F32),
                       pltpu.SemaphoreType.DMA((2,)), pltpu.SemaphoreType.DMA((2,))],
    )
    def k(h_hbm, slot_hbm, xg_hbm, idx0, idx1, rows0, rows1, lsem, ssem):
        first = _sc_worker_id() * nchunk
        idx = (idx0, idx1)
        rows = (rows0, rows1)

        def loads(j, b):
            blk = first + j
            tok = pl.multiple_of(blk * SC_W, SC_W)
            return (pltpu.make_async_copy(slot_hbm.at[blk], idx[b], lsem.at[b]),
                    pltpu.make_async_copy(h_hbm.at[pl.ds(tok, SC_W)], rows[b], lsem.at[b]))

        def scatters(b):
            return [pltpu.make_async_copy(rows[b], xg_hbm.at[idx[b].at[kk]], ssem.at[b]) for kk in range(TOP_K)]

        for cp in loads(0, 0):
            cp.start()

        @pl.loop(0, nchunk, step=2)
        def _(j):
            for b in (0, 1):
                jj = j + b
                for cp in loads(jj, b):
                    cp.wait()
                for cp in scatters(b):
                    cp.start()

                @pl.when(jj + 1 < nchunk)
                def _():
                    @pl.when(jj >= 1)
                    def _():
                        for cp in scatters(1 - b):
                            cp.wait()
                    for cp in loads(jj + 1, 1 - b):
                        cp.start()

        for b in (0, 1):
            for cp in scatters(b):
                cp.wait()

    return k(h3, slot3)


def _gmm_kernel(tile_e_ref, tile_blk_ref, nvalid_ref, x_ref, wg_ref, wu_ref, wd_ref,
                y_ref, wgu_scr, wd_scr):
    j = pl.program_id(0)

    @pl.when(j < nvalid_ref[0])
    def _():
        prev = tile_e_ref[jnp.maximum(j - 1, 0)]

        @pl.when(jnp.logical_or(j == 0, tile_e_ref[j] != prev))
        def _():
            wgu_scr[:, 0:D_EXPERT] = wg_ref[0].astype(BF16)
            wgu_scr[:, D_EXPERT:2 * D_EXPERT] = wu_ref[0].astype(BF16)
            wd_scr[...] = wd_ref[0].astype(BF16)

        x = _load_token_tiles(x_ref, GM)
        gu = jnp.dot(x.astype(BF16), wgu_scr[...], preferred_element_type=F32)
        hid = _silu(gu[:, 0:D_EXPERT]) * gu[:, D_EXPERT:2 * D_EXPERT]
        _store_token_tiles(y_ref, jnp.dot(hid.astype(BF16), wd_scr[...], preferred_element_type=F32))


def _gmm(tile_e, tile_blk, nvalid, xg, w_gate, w_up, w_down):
    row_tile = pl.BlockSpec((GM * ROWS_PER_TOKEN, LANES), lambda j, te, tb, nv: (tb[j], 0))
    grid_spec = pltpu.PrefetchScalarGridSpec(
        num_scalar_prefetch=3,
        grid=(NT_MAX,),
        in_specs=[
            row_tile,
            pl.BlockSpec((1, D, D_EXPERT), lambda j, te, tb, nv: (te[j], 0, 0)),
            pl.BlockSpec((1, D, D_EXPERT), lambda j, te, tb, nv: (te[j], 0, 0)),
            pl.BlockSpec((1, D_EXPERT, D), lambda j, te, tb, nv: (te[j], 0, 0)),
        ],
        out_specs=row_tile,
        scratch_shapes=[pltpu.VMEM((D, 2 * D_EXPERT), BF16), pltpu.VMEM((D_EXPERT, D), BF16)],
    )
    return pl.pallas_call(
        _gmm_kernel,
        grid_spec=grid_spec,
        out_shape=jax.ShapeDtypeStruct((SP * ROWS_PER_TOKEN, LANES), F32),
        compiler_params=_params(("arbitrary",)),
        name="moe_grouped_matmul",
    )(tile_e, tile_blk, nvalid, xg, w_gate, w_up, w_down)


def _sc_gather(table3, idx):
    n_idx = idx.shape[0]
    per_w = n_idx // SC_WORKERS
    nchunk = per_w // SC_W
    mesh = plsc.VectorSubcoreMesh(core_axis_name="c", subcore_axis_name="s")
    tile = (SC_W, ROWS_PER_TOKEN, LANES)

    @functools.partial(
        pl.kernel, mesh=mesh,
        out_type=jax.ShapeDtypeStruct((n_idx, ROWS_PER_TOKEN, LANES), F32),
        scratch_types=[pltpu.VMEM((per_w,), I32), pltpu.VMEM(tile, F32), pltpu.VMEM(tile, F32),
                       pltpu.SemaphoreType.DMA((2,)), pltpu.SemaphoreType.DMA((2,))],
    )
    def k(table_hbm, idx_hbm, out_hbm, idx_v, rows0, rows1, gsem, wsem):
        base = pl.multiple_of(_sc_worker_id() * per_w, per_w)
        rows = (rows0, rows1)
        pltpu.sync_copy(idx_hbm.at[pl.ds(base, per_w)], idx_v)

        def gather(j, b):
            ids = idx_v.at[pl.ds(pl.multiple_of(j * SC_W, SC_W), SC_W)]
            return pltpu.make_async_copy(table_hbm.at[ids], rows[b], gsem.at[b])

        def write(j, b):
            dst = out_hbm.at[pl.ds(pl.multiple_of(base + j * SC_W, SC_W), SC_W)]
            return pltpu.make_async_copy(rows[b], dst, wsem.at[b])

        gather(0, 0).start()

        @pl.loop(0, nchunk, step=2)
        def _(j):
            for b in (0, 1):
                jj = j + b
                gather(jj, b).wait()
                write(jj, b).start()

                @pl.when(jj + 1 < nchunk)
                def _():
                    @pl.when(jj >= 1)
                    def _():
                        write(jj - 1, 1 - b).wait()
                    gather(jj + 1, 1 - b).start()

        write(nchunk - 2, 0).wait()
        write(nchunk - 1, 1).wait()

    return k(table3, idx)


def _combine_kernel(wn_ref, x_ref, mod_ref, gpre_ref, gp_ref, wsg_ref, wsu_ref, wsd_ref, *rest):
    y_refs, o_ref = rest[:TOP_K], rest[TOP_K]
    m = mod_ref[0, 0]
    x = x_ref[...]
    hb = (_rms(x, gpre_ref[...]) * (1.0 + m[4:5]) + m[3:4]).astype(BF16)
    hid = (_silu(jnp.dot(hb, wsg_ref[...], preferred_element_type=F32))
           * jnp.dot(hb, wsu_ref[...], preferred_element_type=F32))
    acc = jnp.dot(hid.astype(BF16), wsd_ref[...], preferred_element_type=F32)

    r = lax.broadcasted_iota(I32, (TD, TD), 0)
    c = lax.broadcasted_iota(I32, (TD, TD), 1)
    eye = jnp.where(r == c, 1.0, 0.0).astype(BF16)
    nt = lambda a, b: lax.dot_general(a, b, (((1,), (1,)), ((), ())), preferred_element_type=F32)
    w1, w2, w3 = _split3(wn_ref[...])
    w_t = nt(eye, w1) + nt(eye, w2) + nt(eye, w3)

    for k in range(TOP_K):
        acc = acc + _load_token_tiles(y_refs[k], TD) * w_t[:, k:k + 1]
    o_ref[...] = x + m[5:6] * _rms(acc, gp_ref[...])


def _combine(wn, x, mod, l, g_pre, g_post, ws_gate, ws_up, ws_down, ybuf):
    tok = pl.BlockSpec((TD, D), lambda i: (i, 0))
    tiles_per_dec = DEC_SEQ // TD
    mod_row = lambda i: jnp.where(i < TP // TD, 0, 1 + (i - TP // TD) // tiles_per_dec)
    full = lambda *shape: pl.BlockSpec(shape, lambda i: (0,) * len(shape))
    y_spec = lambda k: pl.BlockSpec((TD * ROWS_PER_TOKEN, LANES), lambda i: (k * (T // TD) + i, 0))
    return pl.pallas_call(
        _combine_kernel,
        grid=(T // TD,),
        in_specs=[
            pl.BlockSpec((TOP_K, TD), lambda i: (0, i)),
            tok,
            pl.BlockSpec((1, 1, 6, D), lambda i: (l, mod_row(i), 0, 0)),
            full(1, D), full(1, D), full(D, D_SHARED), full(D, D_SHARED), full(D_SHARED, D),
        ] + [y_spec(k) for k in range(TOP_K)],
        out_specs=tok,
        out_shape=jax.ShapeDtypeStruct((T, D), F32),
        compiler_params=_params(("arbitrary",)),
        name="moe_combine",
    )(wn, x, mod, g_pre, g_post, ws_gate, ws_up, ws_down, *([ybuf] * TOP_K))


def _moe_layer(x, mod, l, g_pre, g_post, w_router, e_bias, w_gate, w_up, w_down,
               ws_gate, ws_up, ws_down):
    h, top_e, wn, rk, cnt = _router(x, mod, l, g_pre, w_router.T, e_bias.reshape(N_EXPERTS, 1))
    cnt = cnt.reshape(N_EXPERTS).astype(I32)
    padded = (cnt + GM - 1) // GM * GM
    ends = jnp.cumsum(padded)
    offs = ends - padded
    eid = jnp.arange(N_EXPERTS, dtype=I32)[:, None, None]
    slot = rk + jnp.sum(jnp.where(top_e[None] == eid, offs[:, None, None], 0), axis=0)
    nvalid = ends[-1] // GM
    tile_start = jnp.arange(NT_MAX, dtype=I32) * GM
    tile_raw = jnp.sum((tile_start[:, None] >= ends[None, :]).astype(I32), axis=1)
    last = jnp.maximum(nvalid - 1, 0)
    tile_blk = jnp.minimum(jnp.arange(NT_MAX, dtype=I32), last)
    tile_e = jnp.minimum(tile_raw, N_EXPERTS - 1)
    tile_e = jnp.where(jnp.arange(NT_MAX) <= last, tile_e, tile_e[last])
    slot3 = slot.reshape(TOP_K, T // SC_W, SC_W).transpose(1, 0, 2)
    xg = _sc_dispatch(h.reshape(T, ROWS_PER_TOKEN, LANES), slot3)
    xg = _pad_fill(offs + cnt, padded - cnt, xg.reshape(SP * ROWS_PER_TOKEN, LANES))
    yg = _gmm(tile_e, tile_blk, nvalid.reshape(1), xg, w_gate, w_up, w_down)
    ybuf = _sc_gather(yg.reshape(SP, ROWS_PER_TOKEN, LANES), slot.reshape(TOP_K * T))
    return _combine(wn, x, mod, l, g_pre, g_post,
                    ws_gate.astype(BF16), ws_up.astype(BF16), ws_down.astype(BF16),
                    ybuf.reshape(TOP_K * T * ROWS_PER_TOKEN, LANES))


def _rope_tables():
    n = DEC_SEQ
    rows = n // GRID_W
    row = jnp.repeat(jnp.arange(rows), GRID_W).astype(F32)
    col = jnp.tile(jnp.arange(GRID_W), rows).astype(F32)
    half = DQK_B // 2
    inv = ROPE_BASE ** (-jnp.arange(0, half, 2, dtype=F32) / half)
    ang_r = row[:, None] * inv
    ang_c = col[:, None] * inv
    ang = jnp.concatenate([ang_r, ang_r, ang_c, ang_c], axis=-1)
    reps = QK_B // DQK_B
    return jnp.tile(jnp.cos(ang), (1, reps)), jnp.tile(jnp.sin(ang), (1, reps))


def _pad_in_proj(w):
    s = [0, Q_A, 2 * Q_A, 2 * Q_A + V_A, 2 * Q_A + 2 * V_A]
    s += [s[-1] + GATE_RANK, s[-1] + 2 * GATE_RANK]
    s += [s[-1] + QK_B, s[-1] + 2 * QK_B, s[-1] + 2 * QK_B + V_B]
    gates = jnp.pad(w[:, s[4]:s[6]], ((0, 0), (0, GL_PAD - 2 * GATE_RANK)))
    return jnp.concatenate([w[:, s[0]:s[4]], gates, w[:, s[6]:s[9]]], axis=1).astype(BF16)


def kernel(x_prompt, x_sample, c, c_ctx, state_gla, cache_k, cache_v, ada_w, ada_b, norm_pre_mix, norm_post_mix, norm_pre_ffn, norm_post_ffn, ab_w_in, gla_w_g2, gla_b_g2, gla_norm_g, diff_lambda, diff_norm_g, ab_w_out, sgu_w_in, sgu_b_in, sgu_norm_g, sgu_w_s, sgu_b_s, sgu_w_out, moe_w_router, moe_e_bias, moe_w_gate, moe_w_up, moe_w_down, moe_ws_gate, moe_ws_up, moe_ws_down):
    depth = ada_w.shape[0]
    x = jnp.concatenate([x_prompt.reshape(TP, D), x_sample.reshape(TS, D)], axis=0)
    cond = jnp.concatenate([c_ctx[None, :], c, jnp.zeros((8 - 1 - DEC_BATCH, D), F32)], axis=0)
    mod = _modulation(cond, ada_w, ada_b)
    cos, sin = _rope_tables()
    vec = lambda a: a.reshape(1, -1)
    new_s = new_k = new_v = None
    for l in range(depth):
        if l % 2 == 0:
            e = l // 2
            lam_init = 0.8 - 0.6 * math.exp(-0.3 * l)
            a, r_a, gl, q_b, k_b, v_b = _in_proj(x, mod, l, vec(norm_pre_mix[l]), _pad_in_proj(ab_w_in[e]), cos, sin)
            s0_t = jnp.swapaxes(state_gla[:, e], -1, -2)
            o_f, o_bw, s_fin_t = _gla(a, gl, gla_w_g2[e], gla_b_g2[e].reshape(2, 1, Q_A), s0_t)
            o_att = _diff_attention(q_b, k_b, v_b, cache_k, cache_v, diff_lambda[e], lam_init)
            x = _mix_out(lam_init, o_f, o_bw, r_a, o_att, x, mod, l, vec(gla_norm_g[e]), vec(diff_norm_g[e]),
                         ab_w_out[e].astype(BF16), vec(norm_post_mix[l]))
            new_s = jnp.swapaxes(s_fin_t, -1, -2)
            new_k = k_b[:TP].reshape(BATCH, SEQ, H_B, 2, DQK_B).transpose(0, 2, 3, 1, 4)
            new_v = v_b[:TP].reshape(BATCH, SEQ, H_B, DV_B).transpose(0, 2, 1, 3)
        else:
            o = l // 2
            x = _sgu(x, mod, l, vec(norm_pre_mix[l]), sgu_w_in[o].astype(BF16), vec(sgu_b_in[o]),
                     vec(sgu_norm_g[o]), sgu_w_s[o], sgu_b_s[o].T, sgu_w_out[o].astype(BF16),
                     vec(norm_post_mix[l]))
        x = _moe_layer(x, mod, l, vec(norm_pre_ffn[l]), vec(norm_post_ffn[l]), moe_w_router[l], moe_e_bias[l],
                       moe_w_gate[l], moe_w_up[l], moe_w_down[l], moe_ws_gate[l], moe_ws_up[l], moe_ws_down[l])
    y_prompt = x[:TP].reshape(BATCH, SEQ, D)
    y_sample = x[TP:].reshape(DEC_BATCH, DEC_SEQ, D)
    return (y_prompt, y_sample, new_s[:, None], new_k[:, None], new_v[:, None])
```

```python
import functools
import math

import jax
import jax.numpy as jnp
from jax import lax
from jax.experimental import pallas as pl
from jax.experimental.pallas import tpu as pltpu
from jax.experimental.pallas import tpu_sc as plsc

F32 = jnp.float32
BF16 = jnp.bfloat16
I32 = jnp.int32

D = 1024
BATCH, SEQ = 32, 256
DEC_BATCH, DEC_SEQ = 4, 2048
PAST_LEN = 256
GRID_W = 64
EPS = 1e-6
TP = BATCH * SEQ
TS = DEC_BATCH * DEC_SEQ
T = TP + TS
H_A, DK_A, DV_A = 4, 64, 128
Q_A, V_A = H_A * DK_A, H_A * DV_A
GATE_RANK, GATE_TAU, GLA_CHUNK = 16, 16.0, 64
H_B, DQK_B, DV_B = 4, 64, 128
QK_B, V_B = H_B * 2 * DQK_B, H_B * DV_B
ROPE_BASE = 10000.0
SGU_DIM, SGU_GROUPS, SGU_CHUNK = 1024, 4, 128
N_EXPERTS, TOP_K, N_GROUPS, TOPK_GROUPS = 64, 8, 8, 4
GROUP_SIZE = N_EXPERTS // N_GROUPS
D_EXPERT, D_SHARED = 256, 256
ROUTED_SCALE = 2.5

TM = 512
NPT = TP // TM
TILES_PER_DEC = DEC_SEQ // TM
SEG = 256
NSEG = T // SEG
NSEG_P = TP // SEG
SEG_PER_DEC = DEC_SEQ // SEG
TR = 512
TD = 256
GM = 256
NT_MAX = T * TOP_K // GM + N_EXPERTS
SP = NT_MAX * GM
GL_PAD = 128
VMEM_LIMIT = 56 * 1024 * 1024
NEG_INF = float("-inf")


def _bdot(a, b):
    return jnp.dot(a.astype(BF16), b.astype(BF16), preferred_element_type=F32)


def _bdot_nt(a, b):
    return lax.dot_general(a.astype(BF16), b.astype(BF16), (((1,), (1,)), ((), ())),
                           preferred_element_type=F32)


def _bdot_tn(a, b):
    return lax.dot_general(a.astype(BF16), b.astype(BF16), (((0,), (0,)), ((), ())),
                           preferred_element_type=F32)


def _split3(x):
    x1 = x.astype(BF16)
    r1 = x - x1.astype(F32)
    x2 = r1.astype(BF16)
    x3 = (r1 - x2.astype(F32)).astype(BF16)
    return x1, x2, x3


def _rms(x, g):
    return x * lax.rsqrt(jnp.mean(x * x, axis=-1, keepdims=True) + EPS) * g


def _silu(x):
    return x * jax.nn.sigmoid(x)


def _mod_row(i):
    return jnp.where(i < NPT, 0, 1 + (i - NPT) // TILES_PER_DEC)


def _params(sem, limit=VMEM_LIMIT):
    return pltpu.CompilerParams(dimension_semantics=sem, vmem_limit_bytes=limit)


def _mod_kernel(c_ref, w_ref, b_ref, o_ref):
    o_ref[0] = _bdot(_silu(c_ref[...]), w_ref[0]) + b_ref[0]


def _modulation(cond, ada_w, ada_b):
    depth = ada_w.shape[0]
    nj = 6
    out = pl.pallas_call(
        _mod_kernel,
        grid=(depth, nj),
        in_specs=[
            pl.BlockSpec((8, D), lambda l, j: (0, 0)),
            pl.BlockSpec((1, D, D), lambda l, j: (l, 0, j)),
            pl.BlockSpec((1, 1, D), lambda l, j: (l, 0, j)),
        ],
        out_specs=pl.BlockSpec((1, 8, D), lambda l, j: (l, 0, j)),
        out_shape=jax.ShapeDtypeStruct((depth, 8, 6 * D), F32),
        compiler_params=_params(("arbitrary", "arbitrary")),
        name="adaln_modulation",
    )(cond, ada_w, ada_b.reshape(depth, 1, 6 * D))
    return out.reshape(depth, 8, 6, D)


_C_A, _C_R, _C_GL, _C_Q, _C_K, _C_V, _C_END = 0, 1024, 1536, 1664, 2176, 2688, 3200


def _rope(x, cos, sin):
    lane = lax.broadcasted_iota(I32, x.shape, 1)
    first = (lane % 32) < 16
    n = x.shape[1]
    xr = jnp.where(first, -pltpu.roll(x, n - 16, 1), pltpu.roll(x, 16, 1))
    return x * cos + xr * sin


def _stream_specs():
    return [pl.BlockSpec((TM, D), lambda i: (jnp.minimum(i, NPT - 1), 0)),
            pl.BlockSpec((TM, D), lambda i: (jnp.maximum(i - NPT, 0), 0))]


def _stream_tile(xp_ref, xs_ref):
    return jnp.where(pl.program_id(0) < NPT, xp_ref[...], xs_ref[...])


def _in_kernel(xp_ref, xs_ref, mod_ref, g_ref, w_ref, cos_ref, sin_ref,
               a_ref, r_ref, gl_ref, q_ref, k_ref, v_ref):
    i = pl.program_id(0)
    m = mod_ref[0, 0]
    h = _rms(_stream_tile(xp_ref, xs_ref), g_ref[...]) * (1.0 + m[1:2]) + m[0:1]
    hb = h.astype(BF16)

    def proj(c0, c1):
        return jnp.dot(hb, w_ref[:, c0:c1], preferred_element_type=F32)

    a_ref[...] = proj(_C_A, _C_R)
    r_ref[...] = proj(_C_R, _C_GL)
    gl_ref[...] = proj(_C_GL, _C_Q)
    v_ref[...] = proj(_C_V, _C_END)
    q = proj(_C_Q, _C_K)
    k = proj(_C_K, _C_V)

    @pl.when(i < NPT)
    def _():
        q_ref[...] = q
        k_ref[...] = k

    @pl.when(i >= NPT)
    def _():
        cos = cos_ref[...]
        sin = sin_ref[...]
        q_ref[...] = _rope(q, cos, sin)
        k_ref[...] = _rope(k, cos, sin)


def _in_proj(xp, xs, mod, l, g, w_pad, cos, sin):
    tok = lambda width: pl.BlockSpec((TM, width), lambda i: (i, 0))
    rope_spec = pl.BlockSpec((TM, QK_B), lambda i: (jnp.maximum(i - NPT, 0) % TILES_PER_DEC, 0))
    widths = (1024, 512, GL_PAD, 512, 512, 512)
    return pl.pallas_call(
        _in_kernel,
        grid=(T // TM,),
        in_specs=_stream_specs() + [
            pl.BlockSpec((1, 1, 6, D), lambda i: (l, _mod_row(i), 0, 0)),
            pl.BlockSpec((1, D), lambda i: (0, 0)),
            pl.BlockSpec((D, _C_END), lambda i: (0, 0)),
            rope_spec, rope_spec,
        ],
        out_specs=[tok(w) for w in widths],
        out_shape=[jax.ShapeDtypeStruct((T, w), F32) for w in widths],
        compiler_params=_params(("arbitrary",)),
        name="mixer_ab_in_proj",
    )(xp, xs, mod, g, w_pad, cos, sin)


def _log_sigmoid(x):
    return jnp.minimum(x, 0.0) - jnp.log1p(jnp.exp(-jnp.abs(x)))


def _gla_kernel(af_ref, ab_ref, glf_ref, glb_ref, wg_ref, bg_ref, s0_ref,
                of_ref, ob_ref, sfin_ref, st_ref):
    i = pl.program_id(0)

    @pl.when(i < NSEG_P)
    def _():
        st_ref[...] = jnp.zeros_like(st_ref)

    @pl.when(jnp.logical_and(i >= NSEG_P, (i - NSEG_P) % SEG_PER_DEC == 0))
    def _():
        st_ref[...] = s0_ref[0]

    r = lax.broadcasted_iota(I32, (SEG, SEG), 0)
    c = lax.broadcasted_iota(I32, (SEG, SEG), 1)
    same = (r // GLA_CHUNK) == (c // GLA_CHUNK)
    rc = lax.broadcasted_iota(I32, (GLA_CHUNK, GLA_CHUNK), 0)
    cc = lax.broadcasted_iota(I32, (GLA_CHUNK, GLA_CHUNK), 1)
    nchunk = SEG // GLA_CHUNK

    for d, (a_ref, gl_ref, o_ref) in enumerate(((af_ref, glf_ref, of_ref), (ab_ref, glb_ref, ob_ref))):
        fwd = d == 0
        gcol = gl_ref[:, d * GATE_RANK:(d + 1) * GATE_RANK]
        la = _log_sigmoid(_bdot(gcol, wg_ref[d]) + bg_ref[d]) / GATE_TAU
        tri = jnp.where(jnp.logical_and(same, (c <= r) if fwd else (c >= r)), 1.0, 0.0).astype(BF16)
        l1, l2, l3 = _split3(la)
        b_all = (jnp.dot(tri, l1, preferred_element_type=F32)
                 + jnp.dot(tri, l2, preferred_element_type=F32)
                 + jnp.dot(tri, l3, preferred_element_type=F32))
        keep = (cc <= rc) if fwd else (cc >= rc)
        order = range(nchunk) if fwd else range(nchunk - 1, -1, -1)
        for ch in order:
            r0 = ch * GLA_CHUNK
            rows = slice(r0, r0 + GLA_CHUNK)
            for h in range(H_A):
                kc = slice(h * DK_A, (h + 1) * DK_A)
                q = a_ref[rows, h * DK_A:(h + 1) * DK_A] * (DK_A ** -0.5)
                k = a_ref[rows, Q_A + h * DK_A:Q_A + (h + 1) * DK_A]
                v = a_ref[rows, 2 * Q_A + h * DV_A:2 * Q_A + (h + 1) * DV_A]
                b = b_all[rows, kc]
                b_end = b[GLA_CHUNK - 1:GLA_CHUNK, :] if fwd else b[0:1, :]
                q_in = q * jnp.exp(b)
                attn = jnp.where(keep, _bdot_nt(q_in, k * jnp.exp(-b)), 0.0)
                s_t = st_ref[d, h]
                o_ref[rows, h * DV_A:(h + 1) * DV_A] = _bdot(attn, v) + _bdot_nt(q_in, s_t)
                st_ref[d, h] = s_t * jnp.exp(b_end) + _bdot_tn(v, k * jnp.exp(b_end - b))

    @pl.when(i < NSEG_P)
    def _():
        sfin_ref[0] = st_ref[...]


def _seg_bwd(i):
    j = i - NSEG_P
    return jnp.where(i < NSEG_P, i, NSEG_P + (j // SEG_PER_DEC) * SEG_PER_DEC + (SEG_PER_DEC - 1 - j % SEG_PER_DEC))


def _gla(a, gl, wg, bg, s0_t):
    seg = lambda width, f: pl.BlockSpec((SEG, width), lambda i: (f(i), 0))
    ident = lambda i: i
    st_block = (1, 2, H_A, DV_A, DK_A)
    return pl.pallas_call(
        _gla_kernel,
        grid=(NSEG,),
        in_specs=[
            seg(D, ident), seg(D, _seg_bwd), seg(GL_PAD, ident), seg(GL_PAD, _seg_bwd),
            pl.BlockSpec((2, GATE_RANK, Q_A), lambda i: (0, 0, 0)),
            pl.BlockSpec((2, 1, Q_A), lambda i: (0, 0, 0)),
            pl.BlockSpec(st_block, lambda i: (jnp.maximum(i - NSEG_P, 0) // SEG_PER_DEC, 0, 0, 0, 0)),
        ],
        out_specs=[
            seg(V_A, ident), seg(V_A, _seg_bwd),
            pl.BlockSpec(st_block, lambda i: (jnp.minimum(i, NSEG_P - 1), 0, 0, 0, 0)),
        ],
        out_shape=[
            jax.ShapeDtypeStruct((T, V_A), F32),
            jax.ShapeDtypeStruct((T, V_A), F32),
            jax.ShapeDtypeStruct((BATCH, 2, H_A, DV_A, DK_A), F32),
        ],
        scratch_shapes=[pltpu.VMEM((2, H_A, DV_A, DK_A), F32)],
        compiler_params=_params(("arbitrary",)),
        name="gla_bidir",
    )(a, a, gl, gl, wg, bg, s0_t)


def _diff_lambda(lam_ref, lam_init):
    lp = lam_ref[...]
    s01 = jnp.sum(lp[0:1] * lp[1:2], axis=1, keepdims=True)
    s23 = jnp.sum(lp[2:3] * lp[3:4], axis=1, keepdims=True)
    return jnp.exp(s01) - jnp.exp(s23) + lam_init


def _attn_prompt_kernel(lam_init, q_ref, k_ref, v_ref, lam_ref, o_ref):
    lam = _diff_lambda(lam_ref, lam_init)
    scale = DQK_B ** -0.5
    for h in range(H_B):
        ps = []
        for m in range(2):
            cols = slice((2 * h + m) * DQK_B, (2 * h + m + 1) * DQK_B)
            s = _bdot_nt(q_ref[:, cols], k_ref[:, cols]) * scale
            e = jnp.exp(s - jnp.max(s, axis=1, keepdims=True))
            ps.append(e / jnp.sum(e, axis=1, keepdims=True))
        w = ps[0] - lam * ps[1]
        o_ref[:, h * DV_B:(h + 1) * DV_B] = _bdot(w, v_ref[:, h * DV_B:(h + 1) * DV_B])


def _attn_sample_kernel(lam_init, q_ref, k_ref, v_ref, ck_ref, cv_ref, lam_ref, o_ref):
    lam = _diff_lambda(lam_ref, lam_init)
    scale = DQK_B ** -0.5
    for h in range(H_B):
        parts = []
        for m in range(2):
            cols = slice((2 * h + m) * DQK_B, (2 * h + m + 1) * DQK_B)
            q = q_ref[:, cols]
            sc = _bdot_nt(q, ck_ref[0, 0, h, m]) * scale
            sn = _bdot_nt(q, k_ref[:, cols]) * scale
            mx = jnp.maximum(jnp.max(sc, axis=1, keepdims=True), jnp.max(sn, axis=1, keepdims=True))
            ec = jnp.exp(sc - mx)
            en = jnp.exp(sn - mx)
            den = jnp.sum(ec, axis=1, keepdims=True) + jnp.sum(en, axis=1, keepdims=True)
            parts.append((ec / den, en / den))
        wc = parts[0][0] - lam * parts[1][0]
        wn = parts[0][1] - lam * parts[1][1]
        o_ref[:, h * DV_B:(h + 1) * DV_B] = (_bdot(wc, cv_ref[0, 0, h])
                                             + _bdot(wn, v_ref[:, h * DV_B:(h + 1) * DV_B]))


QB = SEQ
NQB_DEC = DEC_SEQ // QB


def _attn_kernel(lam_init, q_ref, kp_ref, vp_ref, ks_ref, vs_ref, ck_ref, cv_ref, lam_ref, o_ref):
    i = pl.program_id(0)

    @pl.when(i < BATCH)
    def _():
        _attn_prompt_kernel(lam_init, q_ref, kp_ref, vp_ref, lam_ref, o_ref)

    @pl.when(i >= BATCH)
    def _():
        _attn_sample_kernel(lam_init, q_ref, ks_ref, vs_ref, ck_ref, cv_ref, lam_ref, o_ref)


def _diff_attention(q, k, v, cache_k, cache_v, lam_p, lam_init):
    blk = lambda rows, f: pl.BlockSpec((rows, 512), f)
    dec_b = lambda i: jnp.maximum(i - BATCH, 0) // NQB_DEC
    own = lambda i: (i, 0)
    prompt_kv = lambda i: (jnp.minimum(i, BATCH - 1), 0)
    dec_kv = lambda i: (TP // DEC_SEQ + dec_b(i), 0)
    return pl.pallas_call(
        functools.partial(_attn_kernel, lam_init),
        grid=(BATCH + DEC_BATCH * NQB_DEC,),
        in_specs=[
            blk(QB, own), blk(SEQ, prompt_kv), blk(SEQ, prompt_kv), blk(DEC_SEQ, dec_kv), blk(DEC_SEQ, dec_kv),
            pl.BlockSpec((1, 1, H_B, 2, PAST_LEN, DQK_B), lambda i: (dec_b(i), 0, 0, 0, 0, 0)),
            pl.BlockSpec((1, 1, H_B, PAST_LEN, DV_B), lambda i: (dec_b(i), 0, 0, 0, 0)),
            pl.BlockSpec((4, DQK_B), lambda i: (0, 0)),
        ],
        out_specs=blk(QB, own),
        out_shape=jax.ShapeDtypeStruct((T, V_B), F32),
        compiler_params=_params(("arbitrary",)),
        name="diff_attention",
    )(q, k, v, k, v, cache_k, cache_v, lam_p)


def _head_rms(x, g, nheads, width):
    return jnp.concatenate([_rms(x[:, h * width:(h + 1) * width], g) for h in range(nheads)], axis=1)


def _mix_out_kernel(lam_init, of_ref, ob_ref, r_ref, oatt_ref, xp_ref, xs_ref, mod_ref,
                    gg_ref, dg_ref, wo_ref, gp_ref, o_ref):
    m = mod_ref[0, 0]
    o_a = _head_rms(of_ref[...] + ob_ref[...], gg_ref[...], H_A, DV_A) * _silu(r_ref[...])
    o_b = _head_rms(oatt_ref[...], dg_ref[...], H_B, DV_B) * (1.0 - lam_init)
    out = _bdot(o_a, wo_ref[0:V_A, :]) + _bdot(o_b, wo_ref[V_A:V_A + V_B, :])
    o_ref[...] = _stream_tile(xp_ref, xs_ref) + m[2:3] * _rms(out, gp_ref[...])


def _mix_out(lam_init, o_f, o_b, r_a, o_att, xp, xs, mod, l, gla_g, diff_g, w_o, g_post):
    tok = lambda width: pl.BlockSpec((TM, width), lambda i: (i, 0))
    vec = lambda width: pl.BlockSpec((1, width), lambda i: (0, 0))
    return pl.pallas_call(
        functools.partial(_mix_out_kernel, lam_init),
        grid=(T // TM,),
        in_specs=[
            tok(512), tok(512), tok(512), tok(512), *_stream_specs(),
            pl.BlockSpec((1, 1, 6, D), lambda i: (l, _mod_row(i), 0, 0)),
            vec(DV_A), vec(DV_B),
            pl.BlockSpec((V_A + V_B, D), lambda i: (0, 0)),
            vec(D),
        ],
        out_specs=tok(D),
        out_shape=jax.ShapeDtypeStruct((T, D), F32),
        compiler_params=_params(("arbitrary",)),
        name="mixer_ab_out",
    )(o_f, o_b, r_a, o_att, xp, xs, mod, gla_g, diff_g, w_o, g_post)


def _gelu_tanh(x):
    return 0.5 * x * (1.0 + jnp.tanh(math.sqrt(2.0 / math.pi) * (x + 0.044715 * (x * x * x))))


def _sgu_kernel(x_ref, mod_ref, gpre_ref, win_ref, bin_ref, vg_ref, ws_ref, bs_ref,
                wout_ref, gpost_ref, o_ref, t_ref):
    m = mod_ref[0, 0]
    x = x_ref[...]
    h = _rms(x, gpre_ref[...]) * (1.0 + m[1:2]) + m[0:1]
    z = _gelu_tanh(_bdot(h, win_ref[...]) + bin_ref[...])
    v = _rms(z[:, SGU_DIM:], vg_ref[...])
    gw = SGU_DIM // SGU_GROUPS
    for ch in range(TM // SGU_CHUNK):
        rows = slice(ch * SGU_CHUNK, (ch + 1) * SGU_CHUNK)
        for g in range(SGU_GROUPS):
            cols = slice(g * gw, (g + 1) * gw)
            vs = _bdot(ws_ref[g], v[rows, cols]) + bs_ref[:, g:g + 1]
            t_ref[rows, cols] = (z[rows, cols] * vs).astype(BF16)
    out = jnp.dot(t_ref[...], wout_ref[...], preferred_element_type=F32)
    o_ref[...] = x + m[2:3] * _rms(out, gpost_ref[...])


def _sgu(x, mod, l, g_pre, w_in, b_in, v_g, w_s, b_s_t, w_out, g_post):
    tok = pl.BlockSpec((TM, D), lambda i: (i, 0))
    full = lambda *shape: pl.BlockSpec(shape, lambda i: (0,) * len(shape))
    return pl.pallas_call(
        _sgu_kernel,
        grid=(T // TM,),
        in_specs=[
            tok,
            pl.BlockSpec((1, 1, 6, D), lambda i: (l, _mod_row(i), 0, 0)),
            full(1, D), full(D, 2 * SGU_DIM), full(1, 2 * SGU_DIM), full(1, SGU_DIM),
            full(SGU_GROUPS, SGU_CHUNK, SGU_CHUNK), full(SGU_CHUNK, SGU_GROUPS),
            full(SGU_DIM, D), full(1, D),
        ],
        out_specs=tok,
        out_shape=jax.ShapeDtypeStruct((T, D), F32),
        scratch_shapes=[pltpu.VMEM((TM, SGU_DIM), BF16)],
        compiler_params=_params(("arbitrary",)),
        name="sgu_mixer",
    )(x, mod, g_pre, w_in, b_in, v_g, w_s, b_s_t, w_out, g_post)


LANES = 128
ROWS_PER_TOKEN = D // LANES


def _store_token_tiles(ref, x):
    n = x.shape[0]
    for c in range(ROWS_PER_TOKEN):
        ref[pl.ds(c, n, stride=ROWS_PER_TOKEN), :] = x[:, c * LANES:(c + 1) * LANES]


def _load_token_tiles(ref, n):
    return jnp.concatenate([ref[pl.ds(c, n, stride=ROWS_PER_TOKEN), :] for c in range(ROWS_PER_TOKEN)], axis=1)


def _token_rows(t):
    return pl.ds(pl.multiple_of(t * ROWS_PER_TOKEN, ROWS_PER_TOKEN), ROWS_PER_TOKEN)


def _router_kernel(x_ref, mod_ref, g_ref, wr_ref, eb_ref,
                   h_ref, te_ref, wn_ref, rk_ref, cnt_ref, carry_ref):
    i = pl.program_id(0)

    @pl.when(i == 0)
    def _():
        carry_ref[...] = jnp.zeros_like(carry_ref)

    m = mod_ref[0, 0]
    h = _rms(x_ref[...], g_ref[...]) * (1.0 + m[4:5]) + m[3:4]
    _store_token_tiles(h_ref, h)
    h1, h2, _ = _split3(h)
    w1, w2, _ = _split3(wr_ref[...])
    nt = lambda a, b: lax.dot_general(a, b, (((1,), (1,)), ((), ())), preferred_element_type=F32)
    logits = nt(w1, h1) + nt(w1, h2) + nt(w2, h1)
    scores = jax.nn.sigmoid(logits)
    sel = scores + eb_ref[...]

    row8 = lax.broadcasted_iota(I32, (GROUP_SIZE, TR), 0)
    gscore = []
    for g in range(N_GROUPS):
        xg = sel[g * GROUP_SIZE:(g + 1) * GROUP_SIZE]
        m1 = jnp.max(xg, axis=0, keepdims=True)
        i1 = jnp.min(jnp.where(xg == m1, row8, GROUP_SIZE), axis=0, keepdims=True)
        m2 = jnp.max(jnp.where(row8 == i1, NEG_INF, xg), axis=0, keepdims=True)
        gscore.append(m1 + m2)
    pieces = []
    for g in range(N_GROUPS):
        rank = jnp.zeros((1, TR), I32)
        for g2 in range(N_GROUPS):
            if g2 == g:
                continue
            beats = (gscore[g2] >= gscore[g]) if g2 < g else (gscore[g2] > gscore[g])
            rank = rank + beats.astype(I32)
        pieces.append(jnp.where(rank < TOPK_GROUPS, sel[g * GROUP_SIZE:(g + 1) * GROUP_SIZE], NEG_INF))
    cur = jnp.concatenate(pieces, axis=0)

    row = lax.broadcasted_iota(I32, (N_EXPERTS, TR), 0)
    idxs, ws = [], []
    for _ in range(TOP_K):
        mx = jnp.max(cur, axis=0, keepdims=True)
        idx = jnp.min(jnp.where(cur == mx, row, N_EXPERTS), axis=0, keepdims=True)
        hit = row == idx
        ws.append(jnp.sum(jnp.where(hit, scores, 0.0), axis=0, keepdims=True))
        cur = jnp.where(hit, NEG_INF, cur)
        idxs.append(idx)
    mask = jnp.zeros((N_EXPERTS, TR), F32)
    for idx in idxs:
        mask = mask + (row == idx).astype(F32)
    wsum = ws[0]
    for wk in ws[1:]:
        wsum = wsum + wk

    tj = lax.broadcasted_iota(I32, (TR, TR), 0)
    ti = lax.broadcasted_iota(I32, (TR, TR), 1)
    upper = jnp.where(tj < ti, 1.0, 0.0).astype(BF16)
    pos = carry_ref[...] + jnp.dot(mask.astype(BF16), upper, preferred_element_type=F32)
    for k in range(TOP_K):
        hit = row == idxs[k]
        te_ref[k:k + 1, :] = idxs[k]
        wn_ref[k:k + 1, :] = ws[k] / wsum * ROUTED_SCALE
        rk_ref[k:k + 1, :] = jnp.sum(jnp.where(hit, pos, 0.0), axis=0, keepdims=True).astype(I32)
    carry_ref[...] = carry_ref[...] + jnp.sum(mask, axis=1, keepdims=True)
    cnt_ref[...] = carry_ref[...]


def _router(x, mod, l, g, wr_t, e_bias):
    kt = lambda dtype: jax.ShapeDtypeStruct((TOP_K, T), dtype)
    kt_spec = pl.BlockSpec((TOP_K, TR), lambda i: (0, i))
    tiles_per_dec = DEC_SEQ // TR
    mod_row = lambda i: jnp.where(i < TP // TR, 0, 1 + (i - TP // TR) // tiles_per_dec)
    return pl.pallas_call(
        _router_kernel,
        grid=(T // TR,),
        in_specs=[
            pl.BlockSpec((TR, D), lambda i: (i, 0)),
            pl.BlockSpec((1, 1, 6, D), lambda i: (l, mod_row(i), 0, 0)),
            pl.BlockSpec((1, D), lambda i: (0, 0)),
            pl.BlockSpec((N_EXPERTS, D), lambda i: (0, 0)),
            pl.BlockSpec((N_EXPERTS, 1), lambda i: (0, 0)),
        ],
        out_specs=[
            pl.BlockSpec((TR * ROWS_PER_TOKEN, LANES), lambda i: (i, 0)),
            kt_spec, kt_spec, kt_spec,
            pl.BlockSpec((N_EXPERTS, 1), lambda i: (0, 0)),
        ],
        out_shape=[
            jax.ShapeDtypeStruct((T * ROWS_PER_TOKEN, LANES), F32), kt(I32), kt(F32), kt(I32),
            jax.ShapeDtypeStruct((N_EXPERTS, 1), F32),
        ],
        scratch_shapes=[pltpu.VMEM((N_EXPERTS, 1), F32)],
        compiler_params=_params(("arbitrary",)),
        name="moe_router",
    )(x, mod, g, wr_t, e_bias)


_PAD_BITS = tuple(1 << b for b in range(GM.bit_length() - 1))


def _pad_fill_kernel(pad_start_ref, pad_len_ref, xg_in_ref, xg_ref, zero_ref, sem):
    del xg_in_ref
    zero_ref[...] = jnp.zeros_like(zero_ref)

    def pad_copies(e):
        start = pad_start_ref[e]
        n = pad_len_ref[e]
        copies = []
        for bit in _PAD_BITS:
            first = start + (n & ~(2 * bit - 1))
            lo = pl.multiple_of(first * ROWS_PER_TOKEN, ROWS_PER_TOKEN)
            rows = bit * ROWS_PER_TOKEN
            copies.append(((n & bit) != 0, pltpu.make_async_copy(
                zero_ref.at[pl.ds(0, rows)], xg_ref.at[pl.ds(lo, rows)], sem)))
        return copies

    def start_e(e, carry):
        for on, cp in pad_copies(e):
            @pl.when(on)
            def _():
                cp.start()
        return carry

    def wait_e(e, carry):
        for on, cp in pad_copies(e):
            @pl.when(on)
            def _():
                cp.wait()
        return carry

    lax.fori_loop(0, N_EXPERTS, start_e, 0)
    lax.fori_loop(0, N_EXPERTS, wait_e, 0)


def _pad_fill(pad_start, pad_len, xg):
    grid_spec = pltpu.PrefetchScalarGridSpec(
        num_scalar_prefetch=2,
        grid=(1,),
        in_specs=[pl.BlockSpec(memory_space=pl.ANY)],
        out_specs=pl.BlockSpec(memory_space=pl.ANY),
        scratch_shapes=[pltpu.VMEM((GM // 2 * ROWS_PER_TOKEN, LANES), F32), pltpu.SemaphoreType.DMA],
    )
    return pl.pallas_call(
        _pad_fill_kernel,
        grid_spec=grid_spec,
        out_shape=jax.ShapeDtypeStruct(xg.shape, xg.dtype),
        input_output_aliases={2: 0},
        compiler_params=_params(("arbitrary",)),
        name="moe_pad_fill",
    )(pad_start, pad_len, xg)


SC_CORES, SC_SUBCORES = 2, 16
SC_WORKERS = SC_CORES * SC_SUBCORES
SC_W = 32


def _sc_worker_id():
    return lax.axis_index("s") * SC_CORES + lax.axis_index("c")


def _sc_dispatch(h3, slot3):
    nchunk = T // SC_WORKERS // SC_W
    mesh = plsc.VectorSubcoreMesh(core_axis_name="c", subcore_axis_name="s")
    tile = (SC_W, ROWS_PER_TOKEN, LANES)

    @functools.partial(
        pl.kernel, mesh=mesh,
        out_type=jax.ShapeDtypeStruct((SP, ROWS_PER_TOKEN, LANES), F32),
        scratch_types=[pltpu.VMEM((TOP_K, SC_W), I32), pltpu.VMEM((TOP_K, SC_W), I32),
                       pltpu.VMEM(tile, F32), pltpu.VMEM(tile, F32),
                       pltpu.SemaphoreType.DMA((2,)), pltpu.SemaphoreType.DMA((2,))],
    )
    def k(h_hbm, slot_hbm, xg_hbm, idx0, idx1, rows0, rows1, lsem, ssem):
        first = _sc_worker_id() * nchunk
        idx = (idx0, idx1)
        rows = (rows0, rows1)

        def loads(j, b):
            blk = first + j
            tok = pl.multiple_of(blk * SC_W, SC_W)
            return (pltpu.make_async_copy(slot_hbm.at[blk], idx[b], lsem.at[b]),
                    pltpu.make_async_copy(h_hbm.at[pl.ds(tok, SC_W)], rows[b], lsem.at[b]))

        def scatters(b):
            return [pltpu.make_async_copy(rows[b], xg_hbm.at[idx[b].at[kk]], ssem.at[b]) for kk in range(TOP_K)]

        for cp in loads(0, 0):
            cp.start()

        @pl.loop(0, nchunk, step=2)
        def _(j):
            for b in (0, 1):
                jj = j + b
                for cp in loads(jj, b):
                    cp.wait()
                for cp in scatters(b):
                    cp.start()

                @pl.when(jj + 1 < nchunk)
                def _():
                    @pl.when(jj >= 1)
                    def _():
                        for cp in scatters(1 - b):
                            cp.wait()
                    for cp in loads(jj + 1, 1 - b):
                        cp.start()

        for b in (0, 1):
            for cp in scatters(b):
                cp.wait()

    return k(h3, slot3)


def _gmm_kernel(tile_e_ref, tile_blk_ref, nvalid_ref, x_ref, wg_ref, wu_ref, wd_ref,
                y_ref, wgu_scr, wd_scr):
    j = pl.program_id(0)

    @pl.when(j < nvalid_ref[0])
    def _():
        prev = tile_e_ref[jnp.maximum(j - 1, 0)]

        @pl.when(jnp.logical_or(j == 0, tile_e_ref[j] != prev))
        def _():
            wgu_scr[:, 0:D_EXPERT] = wg_ref[0, 0].astype(BF16)
            wgu_scr[:, D_EXPERT:2 * D_EXPERT] = wu_ref[0, 0].astype(BF16)
            wd_scr[...] = wd_ref[0, 0].astype(BF16)

        x = _load_token_tiles(x_ref, GM)
        gu = jnp.dot(x.astype(BF16), wgu_scr[...], preferred_element_type=F32)
        hid = _silu(gu[:, 0:D_EXPERT]) * gu[:, D_EXPERT:2 * D_EXPERT]
        _store_token_tiles(y_ref, jnp.dot(hid.astype(BF16), wd_scr[...], preferred_element_type=F32))


def _gmm(tile_e, tile_blk, nvalid, xg, l, w_gate, w_up, w_down):
    row_tile = pl.BlockSpec((GM * ROWS_PER_TOKEN, LANES), lambda j, te, tb, nv: (tb[j], 0))
    grid_spec = pltpu.PrefetchScalarGridSpec(
        num_scalar_prefetch=3,
        grid=(NT_MAX,),
        in_specs=[
            row_tile,
            pl.BlockSpec((1, 1, D, D_EXPERT), lambda j, te, tb, nv: (l, te[j], 0, 0)),
            pl.BlockSpec((1, 1, D, D_EXPERT), lambda j, te, tb, nv: (l, te[j], 0, 0)),
            pl.BlockSpec((1, 1, D_EXPERT, D), lambda j, te, tb, nv: (l, te[j], 0, 0)),
        ],
        out_specs=row_tile,
        scratch_shapes=[pltpu.VMEM((D, 2 * D_EXPERT), BF16), pltpu.VMEM((D_EXPERT, D), BF16)],
    )
    return pl.pallas_call(
        _gmm_kernel,
        grid_spec=grid_spec,
        out_shape=jax.ShapeDtypeStruct((SP * ROWS_PER_TOKEN, LANES), F32),
        compiler_params=_params(("arbitrary",)),
        name="moe_grouped_matmul",
    )(tile_e, tile_blk, nvalid, xg, w_gate, w_up, w_down)


def _sc_gather(table3, idx):
    n_idx = idx.shape[0]
    per_w = n_idx // SC_WORKERS
    nchunk = per_w // SC_W
    mesh = plsc.VectorSubcoreMesh(core_axis_name="c", subcore_axis_name="s")
    tile = (SC_W, ROWS_PER_TOKEN, LANES)

    @functools.partial(
        pl.kernel, mesh=mesh,
        out_type=jax.ShapeDtypeStruct((n_idx, ROWS_PER_TOKEN, LANES), F32),
        scratch_types=[pltpu.VMEM((per_w,), I32), pltpu.VMEM(tile, F32), pltpu.VMEM(tile, F32),
                       pltpu.SemaphoreType.DMA((2,)), pltpu.SemaphoreType.DMA((2,))],
    )
    def k(table_hbm, idx_hbm, out_hbm, idx_v, rows0, rows1, gsem, wsem):
        base = pl.multiple_of(_sc_worker_id() * per_w, per_w)
        rows = (rows0, rows1)
        pltpu.sync_copy(idx_hbm.at[pl.ds(base, per_w)], idx_v)

        def gather(j, b):
            ids = idx_v.at[pl.ds(pl.multiple_of(j * SC_W, SC_W), SC_W)]
            return pltpu.make_async_copy(table_hbm.at[ids], rows[b], gsem.at[b])

        def write(j, b):
            dst = out_hbm.at[pl.ds(pl.multiple_of(base + j * SC_W, SC_W), SC_W)]
            return pltpu.make_async_copy(rows[b], dst, wsem.at[b])

        gather(0, 0).start()

        @pl.loop(0, nchunk, step=2)
        def _(j):
            for b in (0, 1):
                jj = j + b
                gather(jj, b).wait()
                write(jj, b).start()

                @pl.when(jj + 1 < nchunk)
                def _():
                    @pl.when(jj >= 1)
                    def _():
                        write(jj - 1, 1 - b).wait()
                    gather(jj + 1, 1 - b).start()

        write(nchunk - 2, 0).wait()
        write(nchunk - 1, 1).wait()

    return k(table3, idx)


def _combine_kernel(wn_ref, x_ref, mod_ref, gpre_ref, gp_ref, wsg_ref, wsu_ref, wsd_ref, *rest):
    y_refs, o_refs = rest[:TOP_K], rest[TOP_K:]
    m = mod_ref[0, 0]
    x = x_ref[...]
    hb = (_rms(x, gpre_ref[...]) * (1.0 + m[4:5]) + m[3:4]).astype(BF16)
    hid = (_silu(jnp.dot(hb, wsg_ref[...], preferred_element_type=F32))
           * jnp.dot(hb, wsu_ref[...], preferred_element_type=F32))
    acc = jnp.dot(hid.astype(BF16), wsd_ref[...], preferred_element_type=F32)

    r = lax.broadcasted_iota(I32, (TD, TD), 0)
    c = lax.broadcasted_iota(I32, (TD, TD), 1)
    eye = jnp.where(r == c, 1.0, 0.0).astype(BF16)
    nt = lambda a, b: lax.dot_general(a, b, (((1,), (1,)), ((), ())), preferred_element_type=F32)
    w1, w2, w3 = _split3(wn_ref[...])
    w_t = nt(eye, w1) + nt(eye, w2) + nt(eye, w3)

    for k in range(TOP_K):
        acc = acc + _load_token_tiles(y_refs[k], TD) * w_t[:, k:k + 1]
    out = x + m[5:6] * _rms(acc, gp_ref[...])
    if len(o_refs) == 1:
        o_refs[0][...] = out
    else:
        i = pl.program_id(0)

        @pl.when(i < TP // TD)
        def _():
            o_refs[0][...] = out

        @pl.when(i >= TP // TD)
        def _():
            o_refs[1][...] = out


def _combine(wn, x, mod, l, g_pre, g_post, ws_gate, ws_up, ws_down, ybuf, split_streams):
    tok = pl.BlockSpec((TD, D), lambda i: (i, 0))
    tiles_per_dec = DEC_SEQ // TD
    npd = TP // TD
    mod_row = lambda i: jnp.where(i < npd, 0, 1 + (i - npd) // tiles_per_dec)
    if split_streams:
        out_specs = [pl.BlockSpec((TD, D), lambda i: (jnp.minimum(i, npd - 1), 0)),
                     pl.BlockSpec((TD, D), lambda i: (jnp.maximum(i - npd, 0), 0))]
        out_shape = [jax.ShapeDtypeStruct((TP, D), F32), jax.ShapeDtypeStruct((TS, D), F32)]
    else:
        out_specs, out_shape = tok, jax.ShapeDtypeStruct((T, D), F32)
    full = lambda *shape: pl.BlockSpec(shape, lambda i: (0,) * len(shape))
    y_spec = lambda k: pl.BlockSpec((TD * ROWS_PER_TOKEN, LANES), lambda i: (k * (T // TD) + i, 0))
    return pl.pallas_call(
        _combine_kernel,
        grid=(T // TD,),
        in_specs=[
            pl.BlockSpec((TOP_K, TD), lambda i: (0, i)),
            tok,
            pl.BlockSpec((1, 1, 6, D), lambda i: (l, mod_row(i), 0, 0)),
            full(1, D), full(1, D), full(D, D_SHARED), full(D, D_SHARED), full(D_SHARED, D),
        ] + [y_spec(k) for k in range(TOP_K)],
        out_specs=out_specs,
        out_shape=out_shape,
        compiler_params=_params(("arbitrary",)),
        name="moe_combine",
    )(wn, x, mod, g_pre, g_post, ws_gate, ws_up, ws_down, *([ybuf] * TOP_K))


def _moe_layer(x, mod, l, g_pre, g_post, w_router, e_bias, w_gate, w_up, w_down,
               ws_gate, ws_up, ws_down, split_streams):
    h, top_e, wn, rk, cnt = _router(x, mod, l, g_pre, w_router.T, e_bias.reshape(N_EXPERTS, 1))
    cnt = cnt.reshape(N_EXPERTS).astype(I32)
    padded = (cnt + GM - 1) // GM * GM
    ends = jnp.cumsum(padded)
    offs = ends - padded
    eid = jnp.arange(N_EXPERTS, dtype=I32)[:, None, None]
    slot = rk + jnp.sum(jnp.where(top_e[None] == eid, offs[:, None, None], 0), axis=0)
    nvalid = ends[-1] // GM
    tile_start = jnp.arange(NT_MAX, dtype=I32) * GM
    tile_raw = jnp.sum((tile_start[:, None] >= ends[None, :]).astype(I32), axis=1)
    last = jnp.maximum(nvalid - 1, 0)
    tile_blk = jnp.minimum(jnp.arange(NT_MAX, dtype=I32), last)
    tile_e = jnp.minimum(tile_raw, N_EXPERTS - 1)
    tile_e = jnp.where(jnp.arange(NT_MAX) <= last, tile_e, tile_e[last])
    slot3 = slot.reshape(TOP_K, T // SC_W, SC_W).transpose(1, 0, 2)
    xg = _sc_dispatch(h.reshape(T, ROWS_PER_TOKEN, LANES), slot3)
    xg = _pad_fill(offs + cnt, padded - cnt, xg.reshape(SP * ROWS_PER_TOKEN, LANES))
    yg = _gmm(tile_e, tile_blk, nvalid.reshape(1), xg, l, w_gate, w_up, w_down)
    ybuf = _sc_gather(yg.reshape(SP, ROWS_PER_TOKEN, LANES), slot.reshape(TOP_K * T))
    return _combine(wn, x, mod, l, g_pre, g_post,
                    ws_gate.astype(BF16), ws_up.astype(BF16), ws_down.astype(BF16),
                    ybuf.reshape(TOP_K * T * ROWS_PER_TOKEN, LANES), split_streams)


def _rope_tables():
    n = DEC_SEQ
    rows = n // GRID_W
    row = jnp.repeat(jnp.arange(rows), GRID_W).astype(F32)
    col = jnp.tile(jnp.arange(GRID_W), rows).astype(F32)
    half = DQK_B // 2
    inv = ROPE_BASE ** (-jnp.arange(0, half, 2, dtype=F32) / half)
    ang_r = row[:, None] * inv
    ang_c = col[:, None] * inv
    ang = jnp.concatenate([ang_r, ang_r, ang_c, ang_c], axis=-1)
    reps = QK_B // DQK_B
    return jnp.tile(jnp.cos(ang), (1, reps)), jnp.tile(jnp.sin(ang), (1, reps))


def _pad_in_proj(w):
    s = [0, Q_A, 2 * Q_A, 2 * Q_A + V_A, 2 * Q_A + 2 * V_A]
    s += [s[-1] + GATE_RANK, s[-1] + 2 * GATE_RANK]
    s += [s[-1] + QK_B, s[-1] + 2 * QK_B, s[-1] + 2 * QK_B + V_B]
    gates = jnp.pad(w[:, s[4]:s[6]], ((0, 0), (0, GL_PAD - 2 * GATE_RANK)))
    return jnp.concatenate([w[:, s[0]:s[4]], gates, w[:, s[6]:s[9]]], axis=1).astype(BF16)


def kernel(x_prompt, x_sample, c, c_ctx, state_gla, cache_k, cache_v, ada_w, ada_b, norm_pre_mix, norm_post_mix, norm_pre_ffn, norm_post_ffn, ab_w_in, gla_w_g2, gla_b_g2, gla_norm_g, diff_lambda, diff_norm_g, ab_w_out, sgu_w_in, sgu_b_in, sgu_norm_g, sgu_w_s, sgu_b_s, sgu_w_out, moe_w_router, moe_e_bias, moe_w_gate, moe_w_up, moe_w_down, moe_ws_gate, moe_ws_up, moe_ws_down):
    depth = ada_w.shape[0]
    assert depth == 2, "layer 0 reads the two input streams, the last layer writes them back"
    xp, xs = x_prompt.reshape(TP, D), x_sample.reshape(TS, D)
    x = None
    cond = jnp.concatenate([c_ctx[None, :], c, jnp.zeros((8 - 1 - DEC_BATCH, D), F32)], axis=0)
    mod = _modulation(cond, ada_w, ada_b)
    cos, sin = _rope_tables()
    vec = lambda a: a.reshape(1, -1)
    new_s = new_k = new_v = None
    for l in range(depth):
        if l % 2 == 0:
            e = l // 2
            lam_init = 0.8 - 0.6 * math.exp(-0.3 * l)
            a, r_a, gl, q_b, k_b, v_b = _in_proj(xp, xs, mod, l, vec(norm_pre_mix[l]), _pad_in_proj(ab_w_in[e]), cos, sin)
            s0_t = jnp.swapaxes(state_gla[:, e], -1, -2)
            o_f, o_bw, s_fin_t = _gla(a, gl, gla_w_g2[e], gla_b_g2[e].reshape(2, 1, Q_A), s0_t)
            o_att = _diff_attention(q_b, k_b, v_b, cache_k, cache_v, diff_lambda[e], lam_init)
            x = _mix_out(lam_init, o_f, o_bw, r_a, o_att, xp, xs, mod, l, vec(gla_norm_g[e]), vec(diff_norm_g[e]),
                         ab_w_out[e].astype(BF16), vec(norm_post_mix[l]))
            new_s = jnp.swapaxes(s_fin_t, -1, -2)
            new_k = k_b[:TP].reshape(BATCH, SEQ, H_B, 2, DQK_B).transpose(0, 2, 3, 1, 4)
            new_v = v_b[:TP].reshape(BATCH, SEQ, H_B, DV_B).transpose(0, 2, 1, 3)
        else:
            o = l // 2
            x = _sgu(x, mod, l, vec(norm_pre_mix[l]), sgu_w_in[o].astype(BF16), vec(sgu_b_in[o]),
                     vec(sgu_norm_g[o]), sgu_w_s[o], sgu_b_s[o].T, sgu_w_out[o].astype(BF16),
                     vec(norm_post_mix[l]))
        x = _moe_layer(x, mod, l, vec(norm_pre_ffn[l]), vec(norm_post_ffn[l]), moe_w_router[l], moe_e_bias[l],
                       moe_w_gate, moe_w_up, moe_w_down, moe_ws_gate[l], moe_ws_up[l], moe_ws_down[l],
                       split_streams=(l == depth - 1))
    y_prompt = x[0].reshape(BATCH, SEQ, D)
    y_sample = x[1].reshape(DEC_BATCH, DEC_SEQ, D)
    return (y_prompt, y_sample, new_s[:, None], new_k[:, None], new_v[:, None])
```

```python
import functools
import math

import jax
import jax.numpy as jnp
from jax import lax
from jax.experimental import pallas as pl
from jax.experimental.pallas import tpu as pltpu
from jax.experimental.pallas import tpu_sc as plsc

F32 = jnp.float32
BF16 = jnp.bfloat16
I32 = jnp.int32

D = 1024
BATCH, SEQ = 32, 256
DEC_BATCH, DEC_SEQ = 4, 2048
PAST_LEN = 256
GRID_W = 64
EPS = 1e-6
TP = BATCH * SEQ
TS = DEC_BATCH * DEC_SEQ
T = TP + TS
H_A, DK_A, DV_A = 4, 64, 128
Q_A, V_A = H_A * DK_A, H_A * DV_A
GATE_RANK, GATE_TAU, GLA_CHUNK = 16, 16.0, 64
H_B, DQK_B, DV_B = 4, 64, 128
QK_B, V_B = H_B * 2 * DQK_B, H_B * DV_B
ROPE_BASE = 10000.0
SGU_DIM, SGU_GROUPS, SGU_CHUNK = 1024, 4, 128
N_EXPERTS, TOP_K, N_GROUPS, TOPK_GROUPS = 64, 8, 8, 4
GROUP_SIZE = N_EXPERTS // N_GROUPS
D_EXPERT, D_SHARED = 256, 256
ROUTED_SCALE = 2.5

TM = 512
NPT = TP // TM
TILES_PER_DEC = DEC_SEQ // TM
SEG = 256
NSEG = T // SEG
NSEG_P = TP // SEG
SEG_PER_DEC = DEC_SEQ // SEG
TR = 512
TD = 256
GM = 256
NT_MAX = T * TOP_K // GM + N_EXPERTS
SP = NT_MAX * GM
GL_PAD = 128
VMEM_LIMIT = 56 * 1024 * 1024
NEG_INF = float("-inf")


def _bdot(a, b):
    return jnp.dot(a.astype(BF16), b.astype(BF16), preferred_element_type=F32)


def _bdot_nt(a, b):
    return lax.dot_general(a.astype(BF16), b.astype(BF16), (((1,), (1,)), ((), ())),
                           preferred_element_type=F32)


def _bdot_tn(a, b):
    return lax.dot_general(a.astype(BF16), b.astype(BF16), (((0,), (0,)), ((), ())),
                           preferred_element_type=F32)


def _split3(x):
    x1 = x.astype(BF16)
    r1 = x - x1.astype(F32)
    x2 = r1.astype(BF16)
    x3 = (r1 - x2.astype(F32)).astype(BF16)
    return x1, x2, x3


def _rms(x, g):
    return x * lax.rsqrt(jnp.mean(x * x, axis=-1, keepdims=True) + EPS) * g


def _silu(x):
    return x * jax.nn.sigmoid(x)


def _mod_row(i):
    return jnp.where(i < NPT, 0, 1 + (i - NPT) // TILES_PER_DEC)


def _params(sem, limit=VMEM_LIMIT):
    return pltpu.CompilerParams(dimension_semantics=sem, vmem_limit_bytes=limit)


def _mod_kernel(c_ref, w_ref, b_ref, o_ref):
    o_ref[0] = _bdot(_silu(c_ref[...]), w_ref[0]) + b_ref[0]


def _modulation(cond, ada_w, ada_b):
    depth = ada_w.shape[0]
    nj = 6
    out = pl.pallas_call(
        _mod_kernel,
        grid=(depth, nj),
        in_specs=[
            pl.BlockSpec((8, D), lambda l, j: (0, 0)),
            pl.BlockSpec((1, D, D), lambda l, j: (l, 0, j)),
            pl.BlockSpec((1, 1, D), lambda l, j: (l, 0, j)),
        ],
        out_specs=pl.BlockSpec((1, 8, D), lambda l, j: (l, 0, j)),
        out_shape=jax.ShapeDtypeStruct((depth, 8, 6 * D), F32),
        compiler_params=_params(("arbitrary", "arbitrary")),
        name="adaln_modulation",
    )(cond, ada_w, ada_b.reshape(depth, 1, 6 * D))
    return out.reshape(depth, 8, 6, D)


_C_A, _C_R, _C_GL, _C_Q, _C_K, _C_V, _C_END = 0, 1024, 1536, 1664, 2176, 2688, 3200


def _rope(x, cos, sin):
    lane = lax.broadcasted_iota(I32, x.shape, 1)
    first = (lane % 32) < 16
    n = x.shape[1]
    xr = jnp.where(first, -pltpu.roll(x, n - 16, 1), pltpu.roll(x, 16, 1))
    return x * cos + xr * sin


def _stream_specs():
    return [pl.BlockSpec((TM, D), lambda i: (jnp.minimum(i, NPT - 1), 0)),
            pl.BlockSpec((TM, D), lambda i: (jnp.maximum(i - NPT, 0), 0))]


def _stream_tile(xp_ref, xs_ref):
    return jnp.where(pl.program_id(0) < NPT, xp_ref[...], xs_ref[...])


def _in_kernel(xp_ref, xs_ref, mod_ref, g_ref, w_ref, cos_ref, sin_ref,
               a_ref, r_ref, gl_ref, q_ref, k_ref, v_ref):
    i = pl.program_id(0)
    m = mod_ref[0, 0]
    h = _rms(_stream_tile(xp_ref, xs_ref), g_ref[...]) * (1.0 + m[1:2]) + m[0:1]
    hb = h.astype(BF16)

    def proj(c0, c1):
        return jnp.dot(hb, w_ref[:, c0:c1], preferred_element_type=F32)

    a_ref[...] = proj(_C_A, _C_R)
    r_ref[...] = proj(_C_R, _C_GL)
    gl_ref[...] = proj(_C_GL, _C_Q)
    v_ref[...] = proj(_C_V, _C_END)
    q = proj(_C_Q, _C_K)
    k = proj(_C_K, _C_V)

    @pl.when(i < NPT)
    def _():
        q_ref[...] = q
        k_ref[...] = k

    @pl.when(i >= NPT)
    def _():
        cos = cos_ref[...]
        sin = sin_ref[...]
        q_ref[...] = _rope(q, cos, sin)
        k_ref[...] = _rope(k, cos, sin)


def _in_proj(xp, xs, mod, l, g, w_pad, cos, sin):
    tok = lambda width: pl.BlockSpec((TM, width), lambda i: (i, 0))
    rope_spec = pl.BlockSpec((TM, QK_B), lambda i: (jnp.maximum(i - NPT, 0) % TILES_PER_DEC, 0))
    widths = (1024, 512, GL_PAD, 512, 512, 512)
    return pl.pallas_call(
        _in_kernel,
        grid=(T // TM,),
        in_specs=_stream_specs() + [
            pl.BlockSpec((1, 1, 6, D), lambda i: (l, _mod_row(i), 0, 0)),
            pl.BlockSpec((1, D), lambda i: (0, 0)),
            pl.BlockSpec((D, _C_END), lambda i: (0, 0)),
            rope_spec, rope_spec,
        ],
        out_specs=[tok(w) for w in widths],
        out_shape=[jax.ShapeDtypeStruct((T, w), F32) for w in widths],
        compiler_params=_params(("arbitrary",)),
        name="mixer_ab_in_proj",
    )(xp, xs, mod, g, w_pad, cos, sin)


def _log_sigmoid(x):
    return jnp.minimum(x, 0.0) - jnp.log1p(jnp.exp(-jnp.abs(x)))


def _gla_kernel(af_ref, ab_ref, glf_ref, glb_ref, wg_ref, bg_ref, s0_ref,
                of_ref, ob_ref, sfin_ref, st_ref):
    i = pl.program_id(0)

    @pl.when(i < NSEG_P)
    def _():
        st_ref[...] = jnp.zeros_like(st_ref)

    @pl.when(jnp.logical_and(i >= NSEG_P, (i - NSEG_P) % SEG_PER_DEC == 0))
    def _():
        st_ref[...] = s0_ref[0]

    r = lax.broadcasted_iota(I32, (SEG, SEG), 0)
    c = lax.broadcasted_iota(I32, (SEG, SEG), 1)
    same = (r // GLA_CHUNK) == (c // GLA_CHUNK)
    rc = lax.broadcasted_iota(I32, (GLA_CHUNK, GLA_CHUNK), 0)
    cc = lax.broadcasted_iota(I32, (GLA_CHUNK, GLA_CHUNK), 1)
    nchunk = SEG // GLA_CHUNK

    for d, (a_ref, gl_ref, o_ref) in enumerate(((af_ref, glf_ref, of_ref), (ab_ref, glb_ref, ob_ref))):
        fwd = d == 0
        gcol = gl_ref[:, d * GATE_RANK:(d + 1) * GATE_RANK]
        la = _log_sigmoid(_bdot(gcol, wg_ref[d]) + bg_ref[d]) / GATE_TAU
        tri = jnp.where(jnp.logical_and(same, (c <= r) if fwd else (c >= r)), 1.0, 0.0).astype(BF16)
        l1, l2, l3 = _split3(la)
        b_all = (jnp.dot(tri, l1, preferred_element_type=F32)
                 + jnp.dot(tri, l2, preferred_element_type=F32)
                 + jnp.dot(tri, l3, preferred_element_type=F32))
        keep = (cc <= rc) if fwd else (cc >= rc)
        order = range(nchunk) if fwd else range(nchunk - 1, -1, -1)
        for ch in order:
            r0 = ch * GLA_CHUNK
            rows = slice(r0, r0 + GLA_CHUNK)
            for h in range(H_A):
                kc = slice(h * DK_A, (h + 1) * DK_A)
                q = a_ref[rows, h * DK_A:(h + 1) * DK_A] * (DK_A ** -0.5)
                k = a_ref[rows, Q_A + h * DK_A:Q_A + (h + 1) * DK_A]
                v = a_ref[rows, 2 * Q_A + h * DV_A:2 * Q_A + (h + 1) * DV_A]
                b = b_all[rows, kc]
                b_end = b[GLA_CHUNK - 1:GLA_CHUNK, :] if fwd else b[0:1, :]
                q_in = q * jnp.exp(b)
                attn = jnp.where(keep, _bdot_nt(q_in, k * jnp.exp(-b)), 0.0)
                s_t = st_ref[d, h]
                o_ref[rows, h * DV_A:(h + 1) * DV_A] = _bdot(attn, v) + _bdot_nt(q_in, s_t)
                st_ref[d, h] = s_t * jnp.exp(b_end) + _bdot_tn(v, k * jnp.exp(b_end - b))

    @pl.when(i < NSEG_P)
    def _():
        sfin_ref[0] = st_ref[...]


def _seg_bwd(i):
    j = i - NSEG_P
    return jnp.where(i < NSEG_P, i, NSEG_P + (j // SEG_PER_DEC) * SEG_PER_DEC + (SEG_PER_DEC - 1 - j % SEG_PER_DEC))


def _gla(a, gl, wg, bg, s0_t):
    seg = lambda width, f: pl.BlockSpec((SEG, width), lambda i: (f(i), 0))
    ident = lambda i: i
    st_block = (1, 2, H_A, DV_A, DK_A)
    return pl.pallas_call(
        _gla_kernel,
        grid=(NSEG,),
        in_specs=[
            seg(D, ident), seg(D, _seg_bwd), seg(GL_PAD, ident), seg(GL_PAD, _seg_bwd),
            pl.BlockSpec((2, GATE_RANK, Q_A), lambda i: (0, 0, 0)),
            pl.BlockSpec((2, 1, Q_A), lambda i: (0, 0, 0)),
            pl.BlockSpec(st_block, lambda i: (jnp.maximum(i - NSEG_P, 0) // SEG_PER_DEC, 0, 0, 0, 0)),
        ],
        out_specs=[
            seg(V_A, ident), seg(V_A, _seg_bwd),
            pl.BlockSpec(st_block, lambda i: (jnp.minimum(i, NSEG_P - 1), 0, 0, 0, 0)),
        ],
        out_shape=[
            jax.ShapeDtypeStruct((T, V_A), F32),
            jax.ShapeDtypeStruct((T, V_A), F32),
            jax.ShapeDtypeStruct((BATCH, 2, H_A, DV_A, DK_A), F32),
        ],
        scratch_shapes=[pltpu.VMEM((2, H_A, DV_A, DK_A), F32)],
        compiler_params=_params(("arbitrary",)),
        name="gla_bidir",
    )(a, a, gl, gl, wg, bg, s0_t)


def _diff_lambda(lam_ref, lam_init):
    lp = lam_ref[...]
    s01 = jnp.sum(lp[0:1] * lp[1:2], axis=1, keepdims=True)
    s23 = jnp.sum(lp[2:3] * lp[3:4], axis=1, keepdims=True)
    return jnp.exp(s01) - jnp.exp(s23) + lam_init


def _attn_prompt_kernel(lam_init, q_ref, k_ref, v_ref, lam_ref, o_ref):
    lam = _diff_lambda(lam_ref, lam_init)
    scale = DQK_B ** -0.5
    for h in range(H_B):
        ps = []
        for m in range(2):
            cols = slice((2 * h + m) * DQK_B, (2 * h + m + 1) * DQK_B)
            s = _bdot_nt(q_ref[:, cols], k_ref[:, cols]) * scale
            e = jnp.exp(s - jnp.max(s, axis=1, keepdims=True))
            ps.append(e / jnp.sum(e, axis=1, keepdims=True))
        w = ps[0] - lam * ps[1]
        o_ref[:, h * DV_B:(h + 1) * DV_B] = _bdot(w, v_ref[:, h * DV_B:(h + 1) * DV_B])


def _attn_sample_kernel(lam_init, q_ref, k_ref, v_ref, ck_ref, cv_ref, lam_ref, o_ref):
    lam = _diff_lambda(lam_ref, lam_init)
    scale = DQK_B ** -0.5
    for h in range(H_B):
        parts = []
        for m in range(2):
            cols = slice((2 * h + m) * DQK_B, (2 * h + m + 1) * DQK_B)
            q = q_ref[:, cols]
            sc = _bdot_nt(q, ck_ref[0, 0, h, m]) * scale
            sn = _bdot_nt(q, k_ref[:, cols]) * scale
            mx = jnp.maximum(jnp.max(sc, axis=1, keepdims=True), jnp.max(sn, axis=1, keepdims=True))
            ec = jnp.exp(sc - mx)
            en = jnp.exp(sn - mx)
            den = jnp.sum(ec, axis=1, keepdims=True) + jnp.sum(en, axis=1, keepdims=True)
            parts.append((ec / den, en / den))
        wc = parts[0][0] - lam * parts[1][0]
        wn = parts[0][1] - lam * parts[1][1]
        o_ref[:, h * DV_B:(h + 1) * DV_B] = (_bdot(wc, cv_ref[0, 0, h])
                                             + _bdot(wn, v_ref[:, h * DV_B:(h + 1) * DV_B]))


QB = SEQ
NQB_DEC = DEC_SEQ // QB


def _attn_kernel(lam_init, q_ref, kp_ref, vp_ref, ks_ref, vs_ref, ck_ref, cv_ref, lam_ref, o_ref):
    i = pl.program_id(0)

    @pl.when(i < BATCH)
    def _():
        _attn_prompt_kernel(lam_init, q_ref, kp_ref, vp_ref, lam_ref, o_ref)

    @pl.when(i >= BATCH)
    def _():
        _attn_sample_kernel(lam_init, q_ref, ks_ref, vs_ref, ck_ref, cv_ref, lam_ref, o_ref)


def _diff_attention(q, k, v, cache_k, cache_v, lam_p, lam_init):
    blk = lambda rows, f: pl.BlockSpec((rows, 512), f)
    dec_b = lambda i: jnp.maximum(i - BATCH, 0) // NQB_DEC
    own = lambda i: (i, 0)
    prompt_kv = lambda i: (jnp.minimum(i, BATCH - 1), 0)
    dec_kv = lambda i: (TP // DEC_SEQ + dec_b(i), 0)
    return pl.pallas_call(
        functools.partial(_attn_kernel, lam_init),
        grid=(BATCH + DEC_BATCH * NQB_DEC,),
        in_specs=[
            blk(QB, own), blk(SEQ, prompt_kv), blk(SEQ, prompt_kv), blk(DEC_SEQ, dec_kv), blk(DEC_SEQ, dec_kv),
            pl.BlockSpec((1, 1, H_B, 2, PAST_LEN, DQK_B), lambda i: (dec_b(i), 0, 0, 0, 0, 0)),
            pl.BlockSpec((1, 1, H_B, PAST_LEN, DV_B), lambda i: (dec_b(i), 0, 0, 0, 0)),
            pl.BlockSpec((4, DQK_B), lambda i: (0, 0)),
        ],
        out_specs=blk(QB, own),
        out_shape=jax.ShapeDtypeStruct((T, V_B), F32),
        compiler_params=_params(("arbitrary",)),
        name="diff_attention",
    )(q, k, v, k, v, cache_k, cache_v, lam_p)


def _head_rms(x, g, nheads, width):
    return jnp.concatenate([_rms(x[:, h * width:(h + 1) * width], g) for h in range(nheads)], axis=1)


def _mix_out_kernel(lam_init, of_ref, ob_ref, r_ref, oatt_ref, xp_ref, xs_ref, mod_ref,
                    gg_ref, dg_ref, wo_ref, gp_ref, o_ref):
    m = mod_ref[0, 0]
    o_a = _head_rms(of_ref[...] + ob_ref[...], gg_ref[...], H_A, DV_A) * _silu(r_ref[...])
    o_b = _head_rms(oatt_ref[...], dg_ref[...], H_B, DV_B) * (1.0 - lam_init)
    out = _bdot(o_a, wo_ref[0:V_A, :]) + _bdot(o_b, wo_ref[V_A:V_A + V_B, :])
    o_ref[...] = _stream_tile(xp_ref, xs_ref) + m[2:3] * _rms(out, gp_ref[...])


def _mix_out(lam_init, o_f, o_b, r_a, o_att, xp, xs, mod, l, gla_g, diff_g, w_o, g_post):
    tok = lambda width: pl.BlockSpec((TM, width), lambda i: (i, 0))
    vec = lambda width: pl.BlockSpec((1, width), lambda i: (0, 0))
    return pl.pallas_call(
        functools.partial(_mix_out_kernel, lam_init),
        grid=(T // TM,),
        in_specs=[
            tok(512), tok(512), tok(512), tok(512), *_stream_specs(),
            pl.BlockSpec((1, 1, 6, D), lambda i: (l, _mod_row(i), 0, 0)),
            vec(DV_A), vec(DV_B),
            pl.BlockSpec((V_A + V_B, D), lambda i: (0, 0)),
            vec(D),
        ],
        out_specs=tok(D),
        out_shape=jax.ShapeDtypeStruct((T, D), F32),
        compiler_params=_params(("arbitrary",)),
        name="mixer_ab_out",
    )(o_f, o_b, r_a, o_att, xp, xs, mod, gla_g, diff_g, w_o, g_post)


def _gelu_tanh(x):
    return 0.5 * x * (1.0 + jnp.tanh(math.sqrt(2.0 / math.pi) * (x + 0.044715 * (x * x * x))))


def _sgu_kernel(x_ref, mod_ref, gpre_ref, win_ref, bin_ref, vg_ref, ws_ref, bs_ref,
                wout_ref, gpost_ref, o_ref, t_ref):
    m = mod_ref[0, 0]
    x = x_ref[...]
    h = _rms(x, gpre_ref[...]) * (1.0 + m[1:2]) + m[0:1]
    z = _gelu_tanh(_bdot(h, win_ref[...]) + bin_ref[...])
    v = _rms(z[:, SGU_DIM:], vg_ref[...])
    gw = SGU_DIM // SGU_GROUPS
    for ch in range(TM // SGU_CHUNK):
        rows = slice(ch * SGU_CHUNK, (ch + 1) * SGU_CHUNK)
        for g in range(SGU_GROUPS):
            cols = slice(g * gw, (g + 1) * gw)
            vs = _bdot(ws_ref[g], v[rows, cols]) + bs_ref[:, g:g + 1]
            t_ref[rows, cols] = (z[rows, cols] * vs).astype(BF16)
    out = jnp.dot(t_ref[...], wout_ref[...], preferred_element_type=F32)
    o_ref[...] = x + m[2:3] * _rms(out, gpost_ref[...])


def _sgu(x, mod, l, g_pre, w_in, b_in, v_g, w_s, b_s_t, w_out, g_post):
    tok = pl.BlockSpec((TM, D), lambda i: (i, 0))
    full = lambda *shape: pl.BlockSpec(shape, lambda i: (0,) * len(shape))
    return pl.pallas_call(
        _sgu_kernel,
        grid=(T // TM,),
        in_specs=[
            tok,
            pl.BlockSpec((1, 1, 6, D), lambda i: (l, _mod_row(i), 0, 0)),
            full(1, D), full(D, 2 * SGU_DIM), full(1, 2 * SGU_DIM), full(1, SGU_DIM),
            full(SGU_GROUPS, SGU_CHUNK, SGU_CHUNK), full(SGU_CHUNK, SGU_GROUPS),
            full(SGU_DIM, D), full(1, D),
        ],
        out_specs=tok,
        out_shape=jax.ShapeDtypeStruct((T, D), F32),
        scratch_shapes=[pltpu.VMEM((TM, SGU_DIM), BF16)],
        compiler_params=_params(("arbitrary",)),
        name="sgu_mixer",
    )(x, mod, g_pre, w_in, b_in, v_g, w_s, b_s_t, w_out, g_post)


LANES = 128
U32 = jnp.uint32
PACKED = D // 2
ROWS_PER_TOKEN = PACKED // LANES
HIGH_HALF = 0xFFFF0000


def _pack_rows(x):
    bits = lax.bitcast_convert_type(x.astype(BF16).astype(F32), U32)
    return bits[:, :PACKED] | (bits[:, PACKED:] >> 16)


def _unpack_rows(u):
    return (lax.bitcast_convert_type(u & U32(HIGH_HALF), F32), lax.bitcast_convert_type(u << 16, F32))


def _store_token_tiles(ref, u):
    n = u.shape[0]
    for c in range(ROWS_PER_TOKEN):
        ref[pl.ds(c, n, stride=ROWS_PER_TOKEN), :] = u[:, c * LANES:(c + 1) * LANES]


def _load_token_tiles(ref, n):
    return jnp.concatenate([ref[pl.ds(c, n, stride=ROWS_PER_TOKEN), :] for c in range(ROWS_PER_TOKEN)], axis=1)


def _router_kernel(x_ref, mod_ref, g_ref, wr_ref, eb_ref,
                   h_ref, te_ref, wn_ref, rk_ref, cnt_ref, carry_ref):
    i = pl.program_id(0)

    @pl.when(i == 0)
    def _():
        carry_ref[...] = jnp.zeros_like(carry_ref)

    m = mod_ref[0, 0]
    h = _rms(x_ref[...], g_ref[...]) * (1.0 + m[4:5]) + m[3:4]
    _store_token_tiles(h_ref, _pack_rows(h))
    h1, h2, _ = _split3(h)
    w1, w2, _ = _split3(wr_ref[...])
    nt = lambda a, b: lax.dot_general(a, b, (((1,), (1,)), ((), ())), preferred_element_type=F32)
    logits = nt(w1, h1) + nt(w1, h2) + nt(w2, h1)
    scores = jax.nn.sigmoid(logits)
    sel = scores + eb_ref[...]

    row8 = lax.broadcasted_iota(I32, (GROUP_SIZE, TR), 0)
    gscore = []
    for g in range(N_GROUPS):
        xg = sel[g * GROUP_SIZE:(g + 1) * GROUP_SIZE]
        m1 = jnp.max(xg, axis=0, keepdims=True)
        i1 = jnp.min(jnp.where(xg == m1, row8, GROUP_SIZE), axis=0, keepdims=True)
        m2 = jnp.max(jnp.where(row8 == i1, NEG_INF, xg), axis=0, keepdims=True)
        gscore.append(m1 + m2)
    pieces = []
    for g in range(N_GROUPS):
        rank = jnp.zeros((1, TR), I32)
        for g2 in range(N_GROUPS):
            if g2 == g:
                continue
            beats = (gscore[g2] >= gscore[g]) if g2 < g else (gscore[g2] > gscore[g])
            rank = rank + beats.astype(I32)
        pieces.append(jnp.where(rank < TOPK_GROUPS, sel[g * GROUP_SIZE:(g + 1) * GROUP_SIZE], NEG_INF))
    cur = jnp.concatenate(pieces, axis=0)

    row = lax.broadcasted_iota(I32, (N_EXPERTS, TR), 0)
    idxs, ws = [], []
    for _ in range(TOP_K):
        mx = jnp.max(cur, axis=0, keepdims=True)
        idx = jnp.min(jnp.where(cur == mx, row, N_EXPERTS), axis=0, keepdims=True)
        hit = row == idx
        ws.append(jnp.sum(jnp.where(hit, scores, 0.0), axis=0, keepdims=True))
        cur = jnp.where(hit, NEG_INF, cur)
        idxs.append(idx)
    mask = jnp.zeros((N_EXPERTS, TR), F32)
    for idx in idxs:
        mask = mask + (row == idx).astype(F32)
    wsum = ws[0]
    for wk in ws[1:]:
        wsum = wsum + wk

    tj = lax.broadcasted_iota(I32, (TR, TR), 0)
    ti = lax.broadcasted_iota(I32, (TR, TR), 1)
    upper = jnp.where(tj < ti, 1.0, 0.0).astype(BF16)
    pos = carry_ref[...] + jnp.dot(mask.astype(BF16), upper, preferred_element_type=F32)
    for k in range(TOP_K):
        hit = row == idxs[k]
        te_ref[k:k + 1, :] = idxs[k]
        wn_ref[k:k + 1, :] = ws[k] / wsum * ROUTED_SCALE
        rk_ref[k:k + 1, :] = jnp.sum(jnp.where(hit, pos, 0.0), axis=0, keepdims=True).astype(I32)
    carry_ref[...] = carry_ref[...] + jnp.sum(mask, axis=1, keepdims=True)
    cnt_ref[...] = carry_ref[...]


def _router(x, mod, l, g, wr_t, e_bias):
    kt = lambda dtype: jax.ShapeDtypeStruct((TOP_K, T), dtype)
    kt_spec = pl.BlockSpec((TOP_K, TR), lambda i: (0, i))
    tiles_per_dec = DEC_SEQ // TR
    mod_row = lambda i: jnp.where(i < TP // TR, 0, 1 + (i - TP // TR) // tiles_per_dec)
    return pl.pallas_call(
        _router_kernel,
        grid=(T // TR,),
        in_specs=[
            pl.BlockSpec((TR, D), lambda i: (i, 0)),
            pl.BlockSpec((1, 1, 6, D), lambda i: (l, mod_row(i), 0, 0)),
            pl.BlockSpec((1, D), lambda i: (0, 0)),
            pl.BlockSpec((N_EXPERTS, D), lambda i: (0, 0)),
            pl.BlockSpec((N_EXPERTS, 1), lambda i: (0, 0)),
        ],
        out_specs=[
            pl.BlockSpec((TR * ROWS_PER_TOKEN, LANES), lambda i: (i, 0)),
            kt_spec, kt_spec, kt_spec,
            pl.BlockSpec((N_EXPERTS, 1), lambda i: (0, 0)),
        ],
        out_shape=[
            jax.ShapeDtypeStruct((T * ROWS_PER_TOKEN, LANES), U32), kt(I32), kt(F32), kt(I32),
            jax.ShapeDtypeStruct((N_EXPERTS, 1), F32),
        ],
        scratch_shapes=[pltpu.VMEM((N_EXPERTS, 1), F32)],
        compiler_params=_params(("arbitrary",)),
        name="moe_router",
    )(x, mod, g, wr_t, e_bias)


_PAD_BITS = tuple(1 << b for b in range(GM.bit_length() - 1))


def _pad_fill_kernel(pad_start_ref, pad_len_ref, xg_in_ref, xg_ref, zero_ref, sem):
    del xg_in_ref
    zero_ref[...] = jnp.zeros_like(zero_ref)

    def pad_copies(e):
        start = pad_start_ref[e]
        n = pad_len_ref[e]
        copies = []
        for bit in _PAD_BITS:
            first = start + (n & ~(2 * bit - 1))
            copies.append(((n & bit) != 0, pltpu.make_async_copy(
                zero_ref.at[pl.ds(0, bit)], xg_ref.at[pl.ds(first, bit)], sem)))
        return copies

    def start_e(e, carry):
        for on, cp in pad_copies(e):
            @pl.when(on)
            def _():
                cp.start()
        return carry

    def wait_e(e, carry):
        for on, cp in pad_copies(e):
            @pl.when(on)
            def _():
                cp.wait()
        return carry

    lax.fori_loop(0, N_EXPERTS, start_e, 0)
    lax.fori_loop(0, N_EXPERTS, wait_e, 0)


def _pad_fill(pad_start, pad_len, xg):
    grid_spec = pltpu.PrefetchScalarGridSpec(
        num_scalar_prefetch=2,
        grid=(1,),
        in_specs=[pl.BlockSpec(memory_space=pl.ANY)],
        out_specs=pl.BlockSpec(memory_space=pl.ANY),
        scratch_shapes=[pltpu.VMEM((GM // 2, ROWS_PER_TOKEN, LANES), xg.dtype), pltpu.SemaphoreType.DMA],
    )
    return pl.pallas_call(
        _pad_fill_kernel,
        grid_spec=grid_spec,
        out_shape=jax.ShapeDtypeStruct(xg.shape, xg.dtype),
        input_output_aliases={2: 0},
        compiler_params=_params(("arbitrary",)),
        name="moe_pad_fill",
    )(pad_start, pad_len, xg)


SC_CORES, SC_SUBCORES = 2, 16
SC_WORKERS = SC_CORES * SC_SUBCORES
SC_W = 64


def _sc_worker_id():
    return lax.axis_index("s") * SC_CORES + lax.axis_index("c")


def _sc_dispatch(h3, slot3):
    nchunk = T // SC_WORKERS // SC_W
    mesh = plsc.VectorSubcoreMesh(core_axis_name="c", subcore_axis_name="s")
    tile = (SC_W, ROWS_PER_TOKEN, LANES)

    @functools.partial(
        pl.kernel, mesh=mesh,
        out_type=jax.ShapeDtypeStruct((SP, ROWS_PER_TOKEN, LANES), h3.dtype),
        scratch_types=[pltpu.VMEM((TOP_K, SC_W), I32), pltpu.VMEM((TOP_K, SC_W), I32),
                       pltpu.VMEM(tile, h3.dtype), pltpu.VMEM(tile, h3.dtype),
                       pltpu.SemaphoreType.DMA((2,)), pltpu.SemaphoreType.DMA((2,))],
    )
    def k(h_hbm, slot_hbm, xg_hbm, idx0, idx1, rows0, rows1, lsem, ssem):
        first = _sc_worker_id() * nchunk
        idx = (idx0, idx1)
        rows = (rows0, rows1)

        def loads(j, b):
            blk = first + j
            tok = pl.multiple_of(blk * SC_W, SC_W)
            return (pltpu.make_async_copy(slot_hbm.at[blk], idx[b], lsem.at[b]),
                    pltpu.make_async_copy(h_hbm.at[pl.ds(tok, SC_W)], rows[b], lsem.at[b]))

        def scatters(b):
            return [pltpu.make_async_copy(rows[b], xg_hbm.at[idx[b].at[kk]], ssem.at[b]) for kk in range(TOP_K)]

        for cp in loads(0, 0):
            cp.start()

        @pl.loop(0, nchunk, step=2)
        def _(j):
            for b in (0, 1):
                jj = j + b
                for cp in loads(jj, b):
                    cp.wait()
                for cp in scatters(b):
                    cp.start()

                @pl.when(jj + 1 < nchunk)
                def _():
                    @pl.when(jj >= 1)
                    def _():
                        for cp in scatters(1 - b):
                            cp.wait()
                    for cp in loads(jj + 1, 1 - b):
                        cp.start()

        for b in (0, 1):
            for cp in scatters(b):
                cp.wait()

    return k(h3, slot3)


def _gmm_kernel(tile_e_ref, tile_blk_ref, nvalid_ref, x_ref, wg_ref, wu_ref, wd_ref,
                y_ref, wgu_scr, wd_scr):
    j = pl.program_id(0)

    @pl.when(j < nvalid_ref[0])
    def _():
        prev = tile_e_ref[jnp.maximum(j - 1, 0)]

        @pl.when(jnp.logical_or(j == 0, tile_e_ref[j] != prev))
        def _():
            wgu_scr[:, 0:D_EXPERT] = wg_ref[0, 0].astype(BF16)
            wgu_scr[:, D_EXPERT:2 * D_EXPERT] = wu_ref[0, 0].astype(BF16)
            wd_scr[...] = wd_ref[0, 0].astype(BF16)

        x_hi, x_lo = _unpack_rows(_load_token_tiles(x_ref, GM))
        gu = (jnp.dot(x_hi.astype(BF16), wgu_scr[0:PACKED, :], preferred_element_type=F32)
              + jnp.dot(x_lo.astype(BF16), wgu_scr[PACKED:D, :], preferred_element_type=F32))
        hid = _silu(gu[:, 0:D_EXPERT]) * gu[:, D_EXPERT:2 * D_EXPERT]
        y = jnp.dot(hid.astype(BF16), wd_scr[...], preferred_element_type=F32)
        _store_token_tiles(y_ref, _pack_rows(y))


def _gmm(tile_e, tile_blk, nvalid, xg, l, w_gate, w_up, w_down):
    row_tile = pl.BlockSpec((GM * ROWS_PER_TOKEN, LANES), lambda j, te, tb, nv: (tb[j], 0))
    grid_spec = pltpu.PrefetchScalarGridSpec(
        num_scalar_prefetch=3,
        grid=(NT_MAX,),
        in_specs=[
            row_tile,
            pl.BlockSpec((1, 1, D, D_EXPERT), lambda j, te, tb, nv: (l, te[j], 0, 0)),
            pl.BlockSpec((1, 1, D, D_EXPERT), lambda j, te, tb, nv: (l, te[j], 0, 0)),
            pl.BlockSpec((1, 1, D_EXPERT, D), lambda j, te, tb, nv: (l, te[j], 0, 0)),
        ],
        out_specs=row_tile,
        scratch_shapes=[pltpu.VMEM((D, 2 * D_EXPERT), BF16), pltpu.VMEM((D_EXPERT, D), BF16)],
    )
    return pl.pallas_call(
        _gmm_kernel,
        grid_spec=grid_spec,
        out_shape=jax.ShapeDtypeStruct((SP * ROWS_PER_TOKEN, LANES), U32),
        compiler_params=_params(("arbitrary",)),
        name="moe_grouped_matmul",
    )(tile_e, tile_blk, nvalid, xg, w_gate, w_up, w_down)


def _sc_gather(table3, idx):
    n_idx = idx.shape[0]
    per_w = n_idx // SC_WORKERS
    nchunk = per_w // SC_W
    mesh = plsc.VectorSubcoreMesh(core_axis_name="c", subcore_axis_name="s")
    tile = (SC_W, ROWS_PER_TOKEN, LANES)

    @functools.partial(
        pl.kernel, mesh=mesh,
        out_type=jax.ShapeDtypeStruct((n_idx, ROWS_PER_TOKEN, LANES), table3.dtype),
        scratch_types=[pltpu.VMEM((per_w,), I32), pltpu.VMEM(tile, table3.dtype), pltpu.VMEM(tile, table3.dtype),
                       pltpu.SemaphoreType.DMA((2,)), pltpu.SemaphoreType.DMA((2,))],
    )
    def k(table_hbm, idx_hbm, out_hbm, idx_v, rows0, rows1, gsem, wsem):
        base = pl.multiple_of(_sc_worker_id() * per_w, per_w)
        rows = (rows0, rows1)
        pltpu.sync_copy(idx_hbm.at[pl.ds(base, per_w)], idx_v)

        def gather(j, b):
            ids = idx_v.at[pl.ds(pl.multiple_of(j * SC_W, SC_W), SC_W)]
            return pltpu.make_async_copy(table_hbm.at[ids], rows[b], gsem.at[b])

        def write(j, b):
            dst = out_hbm.at[pl.ds(pl.multiple_of(base + j * SC_W, SC_W), SC_W)]
            return pltpu.make_async_copy(rows[b], dst, wsem.at[b])

        gather(0, 0).start()

        @pl.loop(0, nchunk, step=2)
        def _(j):
            for b in (0, 1):
                jj = j + b
                gather(jj, b).wait()
                write(jj, b).start()

                @pl.when(jj + 1 < nchunk)
                def _():
                    @pl.when(jj >= 1)
                    def _():
                        write(jj - 1, 1 - b).wait()
                    gather(jj + 1, 1 - b).start()

        write(nchunk - 2, 0).wait()
        write(nchunk - 1, 1).wait()

    return k(table3, idx)


def _combine_kernel(wn_ref, x_ref, mod_ref, gpre_ref, gp_ref, wsg_ref, wsu_ref, wsd_ref, *rest):
    y_refs, o_refs = rest[:TOP_K], rest[TOP_K:]
    m = mod_ref[0, 0]
    x = x_ref[...]
    hb = (_rms(x, gpre_ref[...]) * (1.0 + m[4:5]) + m[3:4]).astype(BF16)
    hid = (_silu(jnp.dot(hb, wsg_ref[...], preferred_element_type=F32))
           * jnp.dot(hb, wsu_ref[...], preferred_element_type=F32))
    acc = jnp.dot(hid.astype(BF16), wsd_ref[...], preferred_element_type=F32)

    r = lax.broadcasted_iota(I32, (TD, TD), 0)
    c = lax.broadcasted_iota(I32, (TD, TD), 1)
    eye = jnp.where(r == c, 1.0, 0.0).astype(BF16)
    nt = lambda a, b: lax.dot_general(a, b, (((1,), (1,)), ((), ())), preferred_element_type=F32)
    w1, w2, w3 = _split3(wn_ref[...])
    w_t = nt(eye, w1) + nt(eye, w2) + nt(eye, w3)

    acc_hi = acc[:, :PACKED]
    acc_lo = acc[:, PACKED:]
    for k in range(TOP_K):
        y_hi, y_lo = _unpack_rows(_load_token_tiles(y_refs[k], TD))
        acc_hi = acc_hi + y_hi * w_t[:, k:k + 1]
        acc_lo = acc_lo + y_lo * w_t[:, k:k + 1]
    acc = jnp.concatenate([acc_hi, acc_lo], axis=1)
    out = x + m[5:6] * _rms(acc, gp_ref[...])
    if len(o_refs) == 1:
        o_refs[0][...] = out
    else:
        i = pl.program_id(0)

        @pl.when(i < TP // TD)
        def _():
            o_refs[0][...] = out

        @pl.when(i >= TP // TD)
        def _():
            o_refs[1][...] = out


def _combine(wn, x, mod, l, g_pre, g_post, ws_gate, ws_up, ws_down, ybuf, split_streams):
    tok = pl.BlockSpec((TD, D), lambda i: (i, 0))
    tiles_per_dec = DEC_SEQ // TD
    npd = TP // TD
    mod_row = lambda i: jnp.where(i < npd, 0, 1 + (i - npd) // tiles_per_dec)
    if split_streams:
        out_specs = [pl.BlockSpec((TD, D), lambda i: (jnp.minimum(i, npd - 1), 0)),
                     pl.BlockSpec((TD, D), lambda i: (jnp.maximum(i - npd, 0), 0))]
        out_shape = [jax.ShapeDtypeStruct((TP, D), F32), jax.ShapeDtypeStruct((TS, D), F32)]
    else:
        out_specs, out_shape = tok, jax.ShapeDtypeStruct((T, D), F32)
    full = lambda *shape: pl.BlockSpec(shape, lambda i: (0,) * len(shape))
    y_spec = lambda k: pl.BlockSpec((TD * ROWS_PER_TOKEN, LANES), lambda i: (k * (T // TD) + i, 0))
    return pl.pallas_call(
        _combine_kernel,
        grid=(T // TD,),
        in_specs=[
            pl.BlockSpec((TOP_K, TD), lambda i: (0, i)),
            tok,
            pl.BlockSpec((1, 1, 6, D), lambda i: (l, mod_row(i), 0, 0)),
            full(1, D), full(1, D), full(D, D_SHARED), full(D, D_SHARED), full(D_SHARED, D),
        ] + [y_spec(k) for k in range(TOP_K)],
        out_specs=out_specs,
        out_shape=out_shape,
        compiler_params=_params(("arbitrary",)),
        name="moe_combine",
    )(wn, x, mod, g_pre, g_post, ws_gate, ws_up, ws_down, *([ybuf] * TOP_K))


def _moe_layer(x, mod, l, g_pre, g_post, w_router, e_bias, w_gate, w_up, w_down,
               ws_gate, ws_up, ws_down, split_streams):
    h, top_e, wn, rk, cnt = _router(x, mod, l, g_pre, w_router.T, e_bias.reshape(N_EXPERTS, 1))
    cnt = cnt.reshape(N_EXPERTS).astype(I32)
    padded = (cnt + GM - 1) // GM * GM
    ends = jnp.cumsum(padded)
    offs = ends - padded
    eid = jnp.arange(N_EXPERTS, dtype=I32)[:, None, None]
    slot = rk + jnp.sum(jnp.where(top_e[None] == eid, offs[:, None, None], 0), axis=0)
    nvalid = ends[-1] // GM
    tile_start = jnp.arange(NT_MAX, dtype=I32) * GM
    tile_raw = jnp.sum((tile_start[:, None] >= ends[None, :]).astype(I32), axis=1)
    last = jnp.maximum(nvalid - 1, 0)
    tile_blk = jnp.minimum(jnp.arange(NT_MAX, dtype=I32), last)
    tile_e = jnp.minimum(tile_raw, N_EXPERTS - 1)
    tile_e = jnp.where(jnp.arange(NT_MAX) <= last, tile_e, tile_e[last])
    slot3 = slot.reshape(TOP_K, T // SC_W, SC_W).transpose(1, 0, 2)
    xg = _sc_dispatch(h.reshape(T, ROWS_PER_TOKEN, LANES), slot3)
    xg = _pad_fill(offs + cnt, padded - cnt, xg).reshape(SP * ROWS_PER_TOKEN, LANES)
    yg = _gmm(tile_e, tile_blk, nvalid.reshape(1), xg, l, w_gate, w_up, w_down)
    ybuf = _sc_gather(yg.reshape(SP, ROWS_PER_TOKEN, LANES), slot.reshape(TOP_K * T))
    return _combine(wn, x, mod, l, g_pre, g_post,
                    ws_gate.astype(BF16), ws_up.astype(BF16), ws_down.astype(BF16),
                    ybuf.reshape(TOP_K * T * ROWS_PER_TOKEN, LANES), split_streams)


def _rope_tables():
    n = DEC_SEQ
    rows = n // GRID_W
    row = jnp.repeat(jnp.arange(rows), GRID_W).astype(F32)
    col = jnp.tile(jnp.arange(GRID_W), rows).astype(F32)
    half = DQK_B // 2
    inv = ROPE_BASE ** (-jnp.arange(0, half, 2, dtype=F32) / half)
    ang_r = row[:, None] * inv
    ang_c = col[:, None] * inv
    ang = jnp.concatenate([ang_r, ang_r, ang_c, ang_c], axis=-1)
    reps = QK_B // DQK_B
    return jnp.tile(jnp.cos(ang), (1, reps)), jnp.tile(jnp.sin(ang), (1, reps))


def _pad_in_proj(w):
    s = [0, Q_A, 2 * Q_A, 2 * Q_A + V_A, 2 * Q_A + 2 * V_A]
    s += [s[-1] + GATE_RANK, s[-1] + 2 * GATE_RANK]
    s += [s[-1] + QK_B, s[-1] + 2 * QK_B, s[-1] + 2 * QK_B + V_B]
    gates = jnp.pad(w[:, s[4]:s[6]], ((0, 0), (0, GL_PAD - 2 * GATE_RANK)))
    return jnp.concatenate([w[:, s[0]:s[4]], gates, w[:, s[6]:s[9]]], axis=1).astype(BF16)


def kernel(x_prompt, x_sample, c, c_ctx, state_gla, cache_k, cache_v, ada_w, ada_b, norm_pre_mix, norm_post_mix, norm_pre_ffn, norm_post_ffn, ab_w_in, gla_w_g2, gla_b_g2, gla_norm_g, diff_lambda, diff_norm_g, ab_w_out, sgu_w_in, sgu_b_in, sgu_norm_g, sgu_w_s, sgu_b_s, sgu_w_out, moe_w_router, moe_e_bias, moe_w_gate, moe_w_up, moe_w_down, moe_ws_gate, moe_ws_up, moe_ws_down):
    depth = ada_w.shape[0]
    assert depth == 2, "layer 0 reads the two input streams, the last layer writes them back"
    xp, xs = x_prompt.reshape(TP, D), x_sample.reshape(TS, D)
    x = None
    cond = jnp.concatenate([c_ctx[None, :], c, jnp.zeros((8 - 1 - DEC_BATCH, D), F32)], axis=0)
    mod = _modulation(cond, ada_w, ada_b)
    cos, sin = _rope_tables()
    vec = lambda a: a.reshape(1, -1)
    new_s = new_k = new_v = None
    for l in range(depth):
        if l % 2 == 0:
            e = l // 2
            lam_init = 0.8 - 0.6 * math.exp(-0.3 * l)
            a, r_a, gl, q_b, k_b, v_b = _in_proj(xp, xs, mod, l, vec(norm_pre_mix[l]), _pad_in_proj(ab_w_in[e]), cos, sin)
            s0_t = jnp.swapaxes(state_gla[:, e], -1, -2)
            o_f, o_bw, s_fin_t = _gla(a, gl, gla_w_g2[e], gla_b_g2[e].reshape(2, 1, Q_A), s0_t)
            o_att = _diff_attention(q_b, k_b, v_b, cache_k, cache_v, diff_lambda[e], lam_init)
            x = _mix_out(lam_init, o_f, o_bw, r_a, o_att, xp, xs, mod, l, vec(gla_norm_g[e]), vec(diff_norm_g[e]),
                         ab_w_out[e].astype(BF16), vec(norm_post_mix[l]))
            new_s = jnp.swapaxes(s_fin_t, -1, -2)
            new_k = k_b[:TP].reshape(BATCH, SEQ, H_B, 2, DQK_B).transpose(0, 2, 3, 1, 4)
            new_v = v_b[:TP].reshape(BATCH, SEQ, H_B, DV_B).transpose(0, 2, 1, 3)
        else:
            o = l // 2
            x = _sgu(x, mod, l, vec(norm_pre_mix[l]), sgu_w_in[o].astype(BF16), vec(sgu_b_in[o]),
                     vec(sgu_norm_g[o]), sgu_w_s[o], sgu_b_s[o].T, sgu_w_out[o].astype(BF16),
                     vec(norm_post_mix[l]))
        x = _moe_layer(x, mod, l, vec(norm_pre_ffn[l]), vec(norm_post_ffn[l]), moe_w_router[l], moe_e_bias[l],
                       moe_w_gate, moe_w_up, moe_w_down, moe_ws_gate[l], moe_ws_up[l], moe_ws_down[l],
                       split_streams=(l == depth - 1))
    y_prompt = x[0].reshape(BATCH, SEQ, D)
    y_sample = x[1].reshape(DEC_BATCH, DEC_SEQ, D)
    return (y_prompt, y_sample, new_s[:, None], new_k[:, None], new_v[:, None])
```

```python
import functools
import math

import jax
import jax.numpy as jnp
from jax import lax
from jax.experimental import pallas as pl
from jax.experimental.pallas import tpu as pltpu
from jax.experimental.pallas import tpu_sc as plsc

F32 = jnp.float32
BF16 = jnp.bfloat16
I32 = jnp.int32

D = 1024
BATCH, SEQ = 32, 256
DEC_BATCH, DEC_SEQ = 4, 2048
PAST_LEN = 256
GRID_W = 64
EPS = 1e-6
TP = BATCH * SEQ
TS = DEC_BATCH * DEC_SEQ
T = TP + TS
H_A, DK_A, DV_A = 4, 64, 128
Q_A, V_A = H_A * DK_A, H_A * DV_A
GATE_RANK, GATE_TAU, GLA_CHUNK = 16, 16.0, 64
H_B, DQK_B, DV_B = 4, 64, 128
QK_B, V_B = H_B * 2 * DQK_B, H_B * DV_B
ROPE_BASE = 10000.0
SGU_DIM, SGU_GROUPS, SGU_CHUNK = 1024, 4, 128
N_EXPERTS, TOP_K, N_GROUPS, TOPK_GROUPS = 64, 8, 8, 4
GROUP_SIZE = N_EXPERTS // N_GROUPS
D_EXPERT, D_SHARED = 256, 256
ROUTED_SCALE = 2.5

TM = 512
NPT = TP // TM
TILES_PER_DEC = DEC_SEQ // TM
SEG = 256
NSEG = T // SEG
NSEG_P = TP // SEG
SEG_PER_DEC = DEC_SEQ // SEG
TR = 512
TD = 256
GM = 512
GM_SUB = 256
NT_MAX = T * TOP_K // GM + N_EXPERTS
SP = NT_MAX * GM
GL_PAD = 128
VMEM_LIMIT = 56 * 1024 * 1024
NEG_INF = float("-inf")


def _bdot(a, b):
    return jnp.dot(a.astype(BF16), b.astype(BF16), preferred_element_type=F32)


def _bdot_nt(a, b):
    return lax.dot_general(a.astype(BF16), b.astype(BF16), (((1,), (1,)), ((), ())),
                           preferred_element_type=F32)


def _bdot_tn(a, b):
    return lax.dot_general(a.astype(BF16), b.astype(BF16), (((0,), (0,)), ((), ())),
                           preferred_element_type=F32)


def _split3(x):
    x1 = x.astype(BF16)
    r1 = x - x1.astype(F32)
    x2 = r1.astype(BF16)
    x3 = (r1 - x2.astype(F32)).astype(BF16)
    return x1, x2, x3


def _rms(x, g):
    return x * lax.rsqrt(jnp.mean(x * x, axis=-1, keepdims=True) + EPS) * g


def _silu(x):
    return x * jax.nn.sigmoid(x)


def _mod_row(i):
    return jnp.where(i < NPT, 0, 1 + (i - NPT) // TILES_PER_DEC)


def _params(sem, limit=VMEM_LIMIT):
    return pltpu.CompilerParams(dimension_semantics=sem, vmem_limit_bytes=limit)


def _mod_kernel(c_ref, w_ref, b_ref, o_ref):
    o_ref[0] = _bdot(_silu(c_ref[...]), w_ref[0]) + b_ref[0]


def _modulation(cond, ada_w, ada_b):
    depth = ada_w.shape[0]
    nj = 6
    out = pl.pallas_call(
        _mod_kernel,
        grid=(depth, nj),
        in_specs=[
            pl.BlockSpec((8, D), lambda l, j: (0, 0)),
            pl.BlockSpec((1, D, D), lambda l, j: (l, 0, j)),
            pl.BlockSpec((1, 1, D), lambda l, j: (l, 0, j)),
        ],
        out_specs=pl.BlockSpec((1, 8, D), lambda l, j: (l, 0, j)),
        out_shape=jax.ShapeDtypeStruct((depth, 8, 6 * D), F32),
        compiler_params=_params(("arbitrary", "arbitrary")),
        name="adaln_modulation",
    )(cond, ada_w, ada_b.reshape(depth, 1, 6 * D))
    return out.reshape(depth, 8, 6, D)


_C_A, _C_R, _C_GL, _C_Q, _C_K, _C_V, _C_END = 0, 1024, 1536, 1664, 2176, 2688, 3200


def _rope(x, cos, sin):
    lane = lax.broadcasted_iota(I32, x.shape, 1)
    first = (lane % 32) < 16
    n = x.shape[1]
    xr = jnp.where(first, -pltpu.roll(x, n - 16, 1), pltpu.roll(x, 16, 1))
    return x * cos + xr * sin


def _stream_specs():
    return [pl.BlockSpec((TM, D), lambda i: (jnp.minimum(i, NPT - 1), 0)),
            pl.BlockSpec((TM, D), lambda i: (jnp.maximum(i - NPT, 0), 0))]


def _stream_tile(xp_ref, xs_ref):
    return jnp.where(pl.program_id(0) < NPT, xp_ref[...], xs_ref[...])


def _in_kernel(xp_ref, xs_ref, mod_ref, g_ref, w_ref, cos_ref, sin_ref,
               a_ref, r_ref, gl_ref, q_ref, k_ref, v_ref):
    i = pl.program_id(0)
    m = mod_ref[0, 0]
    h = _rms(_stream_tile(xp_ref, xs_ref), g_ref[...]) * (1.0 + m[1:2]) + m[0:1]
    hb = h.astype(BF16)

    def proj(c0, c1):
        return jnp.dot(hb, w_ref[:, c0:c1], preferred_element_type=F32)

    a_ref[...] = proj(_C_A, _C_R)
    r_ref[...] = proj(_C_R, _C_GL)
    gl_ref[...] = proj(_C_GL, _C_Q)
    v_ref[...] = proj(_C_V, _C_END)
    q = proj(_C_Q, _C_K)
    k = proj(_C_K, _C_V)

    @pl.when(i < NPT)
    def _():
        q_ref[...] = q
        k_ref[...] = k

    @pl.when(i >= NPT)
    def _():
        cos = cos_ref[...]
        sin = sin_ref[...]
        q_ref[...] = _rope(q, cos, sin)
        k_ref[...] = _rope(k, cos, sin)


def _in_proj(xp, xs, mod, l, g, w_pad, cos, sin):
    tok = lambda width: pl.BlockSpec((TM, width), lambda i: (i, 0))
    rope_spec = pl.BlockSpec((TM, QK_B), lambda i: (jnp.maximum(i - NPT, 0) % TILES_PER_DEC, 0))
    widths = (1024, 512, GL_PAD, 512, 512, 512)
    return pl.pallas_call(
        _in_kernel,
        grid=(T // TM,),
        in_specs=_stream_specs() + [
            pl.BlockSpec((1, 1, 6, D), lambda i: (l, _mod_row(i), 0, 0)),
            pl.BlockSpec((1, D), lambda i: (0, 0)),
            pl.BlockSpec((D, _C_END), lambda i: (0, 0)),
            rope_spec, rope_spec,
        ],
        out_specs=[tok(w) for w in widths],
        out_shape=[jax.ShapeDtypeStruct((T, w), F32) for w in widths],
        compiler_params=_params(("arbitrary",)),
        name="mixer_ab_in_proj",
    )(xp, xs, mod, g, w_pad, cos, sin)


def _log_sigmoid(x):
    return jnp.minimum(x, 0.0) - jnp.log1p(jnp.exp(-jnp.abs(x)))


def _gla_kernel(af_ref, ab_ref, glf_ref, glb_ref, wg_ref, bg_ref, s0_ref,
                of_ref, ob_ref, sfin_ref, st_ref):
    i = pl.program_id(0)

    @pl.when(i < NSEG_P)
    def _():
        st_ref[...] = jnp.zeros_like(st_ref)

    @pl.when(jnp.logical_and(i >= NSEG_P, (i - NSEG_P) % SEG_PER_DEC == 0))
    def _():
        st_ref[...] = s0_ref[0]

    r = lax.broadcasted_iota(I32, (SEG, SEG), 0)
    c = lax.broadcasted_iota(I32, (SEG, SEG), 1)
    same = (r // GLA_CHUNK) == (c // GLA_CHUNK)
    rc = lax.broadcasted_iota(I32, (GLA_CHUNK, GLA_CHUNK), 0)
    cc = lax.broadcasted_iota(I32, (GLA_CHUNK, GLA_CHUNK), 1)
    nchunk = SEG // GLA_CHUNK

    for d, (a_ref, gl_ref, o_ref) in enumerate(((af_ref, glf_ref, of_ref), (ab_ref, glb_ref, ob_ref))):
        fwd = d == 0
        gcol = gl_ref[:, d * GATE_RANK:(d + 1) * GATE_RANK]
        la = _log_sigmoid(_bdot(gcol, wg_ref[d]) + bg_ref[d]) / GATE_TAU
        tri = jnp.where(jnp.logical_and(same, (c <= r) if fwd else (c >= r)), 1.0, 0.0).astype(BF16)
        l1, l2, l3 = _split3(la)
        b_all = (jnp.dot(tri, l1, preferred_element_type=F32)
                 + jnp.dot(tri, l2, preferred_element_type=F32)
                 + jnp.dot(tri, l3, preferred_element_type=F32))
        keep = (cc <= rc) if fwd else (cc >= rc)
        order = range(nchunk) if fwd else range(nchunk - 1, -1, -1)
        for ch in order:
            r0 = ch * GLA_CHUNK
            rows = slice(r0, r0 + GLA_CHUNK)
            for h in range(H_A):
                kc = slice(h * DK_A, (h + 1) * DK_A)
                q = a_ref[rows, h * DK_A:(h + 1) * DK_A] * (DK_A ** -0.5)
                k = a_ref[rows, Q_A + h * DK_A:Q_A + (h + 1) * DK_A]
                v = a_ref[rows, 2 * Q_A + h * DV_A:2 * Q_A + (h + 1) * DV_A]
                b = b_all[rows, kc]
                b_end = b[GLA_CHUNK - 1:GLA_CHUNK, :] if fwd else b[0:1, :]
                q_in = q * jnp.exp(b)
                attn = jnp.where(keep, _bdot_nt(q_in, k * jnp.exp(-b)), 0.0)
                s_t = st_ref[d, h]
                o_ref[rows, h * DV_A:(h + 1) * DV_A] = _bdot(attn, v) + _bdot_nt(q_in, s_t)
                st_ref[d, h] = s_t * jnp.exp(b_end) + _bdot_tn(v, k * jnp.exp(b_end - b))

    @pl.when(i < NSEG_P)
    def _():
        sfin_ref[0] = st_ref[...]


def _seg_bwd(i):
    j = i - NSEG_P
    return jnp.where(i < NSEG_P, i, NSEG_P + (j // SEG_PER_DEC) * SEG_PER_DEC + (SEG_PER_DEC - 1 - j % SEG_PER_DEC))


def _gla(a, gl, wg, bg, s0_t):
    seg = lambda width, f: pl.BlockSpec((SEG, width), lambda i: (f(i), 0))
    ident = lambda i: i
    st_block = (1, 2, H_A, DV_A, DK_A)
    return pl.pallas_call(
        _gla_kernel,
        grid=(NSEG,),
        in_specs=[
            seg(D, ident), seg(D, _seg_bwd), seg(GL_PAD, ident), seg(GL_PAD, _seg_bwd),
            pl.BlockSpec((2, GATE_RANK, Q_A), lambda i: (0, 0, 0)),
            pl.BlockSpec((2, 1, Q_A), lambda i: (0, 0, 0)),
            pl.BlockSpec(st_block, lambda i: (jnp.maximum(i - NSEG_P, 0) // SEG_PER_DEC, 0, 0, 0, 0)),
        ],
        out_specs=[
            seg(V_A, ident), seg(V_A, _seg_bwd),
            pl.BlockSpec(st_block, lambda i: (jnp.minimum(i, NSEG_P - 1), 0, 0, 0, 0)),
        ],
        out_shape=[
            jax.ShapeDtypeStruct((T, V_A), F32),
            jax.ShapeDtypeStruct((T, V_A), F32),
            jax.ShapeDtypeStruct((BATCH, 2, H_A, DV_A, DK_A), F32),
        ],
        scratch_shapes=[pltpu.VMEM((2, H_A, DV_A, DK_A), F32)],
        compiler_params=_params(("arbitrary",)),
        name="gla_bidir",
    )(a, a, gl, gl, wg, bg, s0_t)


def _diff_lambda(lam_ref, lam_init):
    lp = lam_ref[...]
    s01 = jnp.sum(lp[0:1] * lp[1:2], axis=1, keepdims=True)
    s23 = jnp.sum(lp[2:3] * lp[3:4], axis=1, keepdims=True)
    return jnp.exp(s01) - jnp.exp(s23) + lam_init


def _attn_prompt_kernel(lam_init, q_ref, k_ref, v_ref, lam_ref, o_ref):
    lam = _diff_lambda(lam_ref, lam_init)
    scale = DQK_B ** -0.5
    for h in range(H_B):
        ps = []
        for m in range(2):
            cols = slice((2 * h + m) * DQK_B, (2 * h + m + 1) * DQK_B)
            s = _bdot_nt(q_ref[:, cols], k_ref[:, cols]) * scale
            e = jnp.exp(s - jnp.max(s, axis=1, keepdims=True))
            ps.append(e / jnp.sum(e, axis=1, keepdims=True))
        w = ps[0] - lam * ps[1]
        o_ref[:, h * DV_B:(h + 1) * DV_B] = _bdot(w, v_ref[:, h * DV_B:(h + 1) * DV_B])


def _attn_sample_kernel(lam_init, q_ref, k_ref, v_ref, ck_ref, cv_ref, lam_ref, o_ref):
    lam = _diff_lambda(lam_ref, lam_init)
    scale = DQK_B ** -0.5
    for h in range(H_B):
        parts = []
        for m in range(2):
            cols = slice((2 * h + m) * DQK_B, (2 * h + m + 1) * DQK_B)
            q = q_ref[:, cols]
            sc = _bdot_nt(q, ck_ref[0, 0, h, m]) * scale
            sn = _bdot_nt(q, k_ref[:, cols]) * scale
            mx = jnp.maximum(jnp.max(sc, axis=1, keepdims=True), jnp.max(sn, axis=1, keepdims=True))
            ec = jnp.exp(sc - mx)
            en = jnp.exp(sn - mx)
            den = jnp.sum(ec, axis=1, keepdims=True) + jnp.sum(en, axis=1, keepdims=True)
            parts.append((ec / den, en / den))
        wc = parts[0][0] - lam * parts[1][0]
        wn = parts[0][1] - lam * parts[1][1]
        o_ref[:, h * DV_B:(h + 1) * DV_B] = (_bdot(wc, cv_ref[0, 0, h])
                                             + _bdot(wn, v_ref[:, h * DV_B:(h + 1) * DV_B]))


QB = SEQ
NQB_DEC = DEC_SEQ // QB


def _attn_kernel(lam_init, q_ref, kp_ref, vp_ref, ks_ref, vs_ref, ck_ref, cv_ref, lam_ref, o_ref):
    i = pl.program_id(0)

    @pl.when(i < BATCH)
    def _():
        _attn_prompt_kernel(lam_init, q_ref, kp_ref, vp_ref, lam_ref, o_ref)

    @pl.when(i >= BATCH)
    def _():
        _attn_sample_kernel(lam_init, q_ref, ks_ref, vs_ref, ck_ref, cv_ref, lam_ref, o_ref)


def _diff_attention(q, k, v, cache_k, cache_v, lam_p, lam_init):
    blk = lambda rows, f: pl.BlockSpec((rows, 512), f)
    dec_b = lambda i: jnp.maximum(i - BATCH, 0) // NQB_DEC
    own = lambda i: (i, 0)
    prompt_kv = lambda i: (jnp.minimum(i, BATCH - 1), 0)
    dec_kv = lambda i: (TP // DEC_SEQ + dec_b(i), 0)
    return pl.pallas_call(
        functools.partial(_attn_kernel, lam_init),
        grid=(BATCH + DEC_BATCH * NQB_DEC,),
        in_specs=[
            blk(QB, own), blk(SEQ, prompt_kv), blk(SEQ, prompt_kv), blk(DEC_SEQ, dec_kv), blk(DEC_SEQ, dec_kv),
            pl.BlockSpec((1, 1, H_B, 2, PAST_LEN, DQK_B), lambda i: (dec_b(i), 0, 0, 0, 0, 0)),
            pl.BlockSpec((1, 1, H_B, PAST_LEN, DV_B), lambda i: (dec_b(i), 0, 0, 0, 0)),
            pl.BlockSpec((4, DQK_B), lambda i: (0, 0)),
        ],
        out_specs=blk(QB, own),
        out_shape=jax.ShapeDtypeStruct((T, V_B), F32),
        compiler_params=_params(("arbitrary",)),
        name="diff_attention",
    )(q, k, v, k, v, cache_k, cache_v, lam_p)


def _head_rms(x, g, nheads, width):
    return jnp.concatenate([_rms(x[:, h * width:(h + 1) * width], g) for h in range(nheads)], axis=1)


def _mix_out_kernel(lam_init, of_ref, ob_ref, r_ref, oatt_ref, xp_ref, xs_ref, mod_ref,
                    gg_ref, dg_ref, wo_ref, gp_ref, o_ref):
    m = mod_ref[0, 0]
    o_a = _head_rms(of_ref[...] + ob_ref[...], gg_ref[...], H_A, DV_A) * _silu(r_ref[...])
    o_b = _head_rms(oatt_ref[...], dg_ref[...], H_B, DV_B) * (1.0 - lam_init)
    out = _bdot(o_a, wo_ref[0:V_A, :]) + _bdot(o_b, wo_ref[V_A:V_A + V_B, :])
    o_ref[...] = _stream_tile(xp_ref, xs_ref) + m[2:3] * _rms(out, gp_ref[...])


def _mix_out(lam_init, o_f, o_b, r_a, o_att, xp, xs, mod, l, gla_g, diff_g, w_o, g_post):
    tok = lambda width: pl.BlockSpec((TM, width), lambda i: (i, 0))
    vec = lambda width: pl.BlockSpec((1, width), lambda i: (0, 0))
    return pl.pallas_call(
        functools.partial(_mix_out_kernel, lam_init),
        grid=(T // TM,),
        in_specs=[
            tok(512), tok(512), tok(512), tok(512), *_stream_specs(),
            pl.BlockSpec((1, 1, 6, D), lambda i: (l, _mod_row(i), 0, 0)),
            vec(DV_A), vec(DV_B),
            pl.BlockSpec((V_A + V_B, D), lambda i: (0, 0)),
            vec(D),
        ],
        out_specs=tok(D),
        out_shape=jax.ShapeDtypeStruct((T, D), F32),
        compiler_params=_params(("arbitrary",)),
        name="mixer_ab_out",
    )(o_f, o_b, r_a, o_att, xp, xs, mod, gla_g, diff_g, w_o, g_post)


def _gelu_tanh(x):
    return 0.5 * x * (1.0 + jnp.tanh(math.sqrt(2.0 / math.pi) * (x + 0.044715 * (x * x * x))))


def _sgu_kernel(x_ref, mod_ref, gpre_ref, win_ref, bin_ref, vg_ref, ws_ref, bs_ref,
                wout_ref, gpost_ref, o_ref, t_ref):
    m = mod_ref[0, 0]
    x = x_ref[...]
    h = _rms(x, gpre_ref[...]) * (1.0 + m[1:2]) + m[0:1]
    z = _gelu_tanh(_bdot(h, win_ref[...]) + bin_ref[...])
    v = _rms(z[:, SGU_DIM:], vg_ref[...])
    gw = SGU_DIM // SGU_GROUPS
    for ch in range(TM // SGU_CHUNK):
        rows = slice(ch * SGU_CHUNK, (ch + 1) * SGU_CHUNK)
        for g in range(SGU_GROUPS):
            cols = slice(g * gw, (g + 1) * gw)
            vs = _bdot(ws_ref[g], v[rows, cols]) + bs_ref[:, g:g + 1]
            t_ref[rows, cols] = (z[rows, cols] * vs).astype(BF16)
    out = jnp.dot(t_ref[...], wout_ref[...], preferred_element_type=F32)
    o_ref[...] = x + m[2:3] * _rms(out, gpost_ref[...])


def _sgu(x, mod, l, g_pre, w_in, b_in, v_g, w_s, b_s_t, w_out, g_post):
    tok = pl.BlockSpec((TM, D), lambda i: (i, 0))
    full = lambda *shape: pl.BlockSpec(shape, lambda i: (0,) * len(shape))
    return pl.pallas_call(
        _sgu_kernel,
        grid=(T // TM,),
        in_specs=[
            tok,
            pl.BlockSpec((1, 1, 6, D), lambda i: (l, _mod_row(i), 0, 0)),
            full(1, D), full(D, 2 * SGU_DIM), full(1, 2 * SGU_DIM), full(1, SGU_DIM),
            full(SGU_GROUPS, SGU_CHUNK, SGU_CHUNK), full(SGU_CHUNK, SGU_GROUPS),
            full(SGU_DIM, D), full(1, D),
        ],
        out_specs=tok,
        out_shape=jax.ShapeDtypeStruct((T, D), F32),
        scratch_shapes=[pltpu.VMEM((TM, SGU_DIM), BF16)],
        compiler_params=_params(("arbitrary",)),
        name="sgu_mixer",
    )(x, mod, g_pre, w_in, b_in, v_g, w_s, b_s_t, w_out, g_post)


LANES = 128
U32 = jnp.uint32
PACKED = D // 2
ROWS_PER_TOKEN = PACKED // LANES
HIGH_HALF = 0xFFFF0000


def _pack_rows(x):
    bits = lax.bitcast_convert_type(x.astype(BF16).astype(F32), U32)
    return bits[:, :PACKED] | (bits[:, PACKED:] >> 16)


def _unpack_rows(u):
    return (lax.bitcast_convert_type(u & U32(HIGH_HALF), F32), lax.bitcast_convert_type(u << 16, F32))


def _store_token_tiles(ref, u, first=0):
    n = u.shape[0]
    for c in range(ROWS_PER_TOKEN):
        ref[pl.ds(first * ROWS_PER_TOKEN + c, n, stride=ROWS_PER_TOKEN), :] = u[:, c * LANES:(c + 1) * LANES]


def _load_token_tiles(ref, n, first=0):
    return jnp.concatenate([ref[pl.ds(first * ROWS_PER_TOKEN + c, n, stride=ROWS_PER_TOKEN), :]
                            for c in range(ROWS_PER_TOKEN)], axis=1)


def _router_kernel(x_ref, mod_ref, g_ref, wr_ref, eb_ref,
                   h_ref, te_ref, wn_ref, rk_ref, cnt_ref, carry_ref):
    i = pl.program_id(0)

    @pl.when(i == 0)
    def _():
        carry_ref[...] = jnp.zeros_like(carry_ref)

    m = mod_ref[0, 0]
    h = _rms(x_ref[...], g_ref[...]) * (1.0 + m[4:5]) + m[3:4]
    _store_token_tiles(h_ref, _pack_rows(h))
    h1, h2, _ = _split3(h)
    w1, w2, _ = _split3(wr_ref[...])
    nt = lambda a, b: lax.dot_general(a, b, (((1,), (1,)), ((), ())), preferred_element_type=F32)
    logits = nt(w1, h1) + nt(w1, h2) + nt(w2, h1)
    scores = jax.nn.sigmoid(logits)
    sel = scores + eb_ref[...]

    row8 = lax.broadcasted_iota(I32, (GROUP_SIZE, TR), 0)
    gscore = []
    for g in range(N_GROUPS):
        xg = sel[g * GROUP_SIZE:(g + 1) * GROUP_SIZE]
        m1 = jnp.max(xg, axis=0, keepdims=True)
        i1 = jnp.min(jnp.where(xg == m1, row8, GROUP_SIZE), axis=0, keepdims=True)
        m2 = jnp.max(jnp.where(row8 == i1, NEG_INF, xg), axis=0, keepdims=True)
        gscore.append(m1 + m2)
    pieces = []
    for g in range(N_GROUPS):
        rank = jnp.zeros((1, TR), I32)
        for g2 in range(N_GROUPS):
            if g2 == g:
                continue
            beats = (gscore[g2] >= gscore[g]) if g2 < g else (gscore[g2] > gscore[g])
            rank = rank + beats.astype(I32)
        pieces.append(jnp.where(rank < TOPK_GROUPS, sel[g * GROUP_SIZE:(g + 1) * GROUP_SIZE], NEG_INF))
    cur = jnp.concatenate(pieces, axis=0)

    row = lax.broadcasted_iota(I32, (N_EXPERTS, TR), 0)
    idxs, ws = [], []
    for _ in range(TOP_K):
        mx = jnp.max(cur, axis=0, keepdims=True)
        idx = jnp.min(jnp.where(cur == mx, row, N_EXPERTS), axis=0, keepdims=True)
        hit = row == idx
        ws.append(jnp.sum(jnp.where(hit, scores, 0.0), axis=0, keepdims=True))
        cur = jnp.where(hit, NEG_INF, cur)
        idxs.append(idx)
    mask = jnp.zeros((N_EXPERTS, TR), F32)
    for idx in idxs:
        mask = mask + (row == idx).astype(F32)
    wsum = ws[0]
    for wk in ws[1:]:
        wsum = wsum + wk

    tj = lax.broadcasted_iota(I32, (TR, TR), 0)
    ti = lax.broadcasted_iota(I32, (TR, TR), 1)
    upper = jnp.where(tj < ti, 1.0, 0.0).astype(BF16)
    pos = carry_ref[...] + jnp.dot(mask.astype(BF16), upper, preferred_element_type=F32)
    for k in range(TOP_K):
        hit = row == idxs[k]
        te_ref[k:k + 1, :] = idxs[k]
        wn_ref[k:k + 1, :] = ws[k] / wsum * ROUTED_SCALE
        rk_ref[k:k + 1, :] = jnp.sum(jnp.where(hit, pos, 0.0), axis=0, keepdims=True).astype(I32)
    carry_ref[...] = carry_ref[...] + jnp.sum(mask, axis=1, keepdims=True)
    cnt_ref[...] = carry_ref[...]


def _router(x, mod, l, g, wr_t, e_bias):
    kt = lambda dtype: jax.ShapeDtypeStruct((TOP_K, T), dtype)
    kt_spec = pl.BlockSpec((TOP_K, TR), lambda i: (0, i))
    tiles_per_dec = DEC_SEQ // TR
    mod_row = lambda i: jnp.where(i < TP // TR, 0, 1 + (i - TP // TR) // tiles_per_dec)
    return pl.pallas_call(
        _router_kernel,
        grid=(T // TR,),
        in_specs=[
            pl.BlockSpec((TR, D), lambda i: (i, 0)),
            pl.BlockSpec((1, 1, 6, D), lambda i: (l, mod_row(i), 0, 0)),
            pl.BlockSpec((1, D), lambda i: (0, 0)),
            pl.BlockSpec((N_EXPERTS, D), lambda i: (0, 0)),
            pl.BlockSpec((N_EXPERTS, 1), lambda i: (0, 0)),
        ],
        out_specs=[
            pl.BlockSpec((TR * ROWS_PER_TOKEN, LANES), lambda i: (i, 0)),
            kt_spec, kt_spec, kt_spec,
            pl.BlockSpec((N_EXPERTS, 1), lambda i: (0, 0)),
        ],
        out_shape=[
            jax.ShapeDtypeStruct((T * ROWS_PER_TOKEN, LANES), U32), kt(I32), kt(F32), kt(I32),
            jax.ShapeDtypeStruct((N_EXPERTS, 1), F32),
        ],
        scratch_shapes=[pltpu.VMEM((N_EXPERTS, 1), F32)],
        compiler_params=_params(("arbitrary",)),
        name="moe_router",
    )(x, mod, g, wr_t, e_bias)


_PAD_BITS = tuple(1 << b for b in range(GM.bit_length() - 1))


def _pad_fill_kernel(pad_start_ref, pad_len_ref, xg_in_ref, xg_ref, zero_ref, sem):
    del xg_in_ref
    zero_ref[...] = jnp.zeros_like(zero_ref)

    def pad_copies(e):
        start = pad_start_ref[e]
        n = pad_len_ref[e]
        copies = []
        for bit in _PAD_BITS:
            first = start + (n & ~(2 * bit - 1))
            copies.append(((n & bit) != 0, pltpu.make_async_copy(
                zero_ref.at[pl.ds(0, bit)], xg_ref.at[pl.ds(first, bit)], sem)))
        return copies

    def start_e(e, carry):
        for on, cp in pad_copies(e):
            @pl.when(on)
            def _():
                cp.start()
        return carry

    def wait_e(e, carry):
        for on, cp in pad_copies(e):
            @pl.when(on)
            def _():
                cp.wait()
        return carry

    lax.fori_loop(0, N_EXPERTS, start_e, 0)
    lax.fori_loop(0, N_EXPERTS, wait_e, 0)


def _pad_fill(pad_start, pad_len, xg):
    grid_spec = pltpu.PrefetchScalarGridSpec(
        num_scalar_prefetch=2,
        grid=(1,),
        in_specs=[pl.BlockSpec(memory_space=pl.ANY)],
        out_specs=pl.BlockSpec(memory_space=pl.ANY),
        scratch_shapes=[pltpu.VMEM((GM // 2, ROWS_PER_TOKEN, LANES), xg.dtype), pltpu.SemaphoreType.DMA],
    )
    return pl.pallas_call(
        _pad_fill_kernel,
        grid_spec=grid_spec,
        out_shape=jax.ShapeDtypeStruct(xg.shape, xg.dtype),
        input_output_aliases={2: 0},
        compiler_params=_params(("arbitrary",)),
        name="moe_pad_fill",
    )(pad_start, pad_len, xg)


SC_CORES, SC_SUBCORES = 2, 16
SC_WORKERS = SC_CORES * SC_SUBCORES
SC_W = 64


def _sc_worker_id():
    return lax.axis_index("s") * SC_CORES + lax.axis_index("c")


def _sc_dispatch(h3, slot3):
    nchunk = T // SC_WORKERS // SC_W
    mesh = plsc.VectorSubcoreMesh(core_axis_name="c", subcore_axis_name="s")
    tile = (SC_W, ROWS_PER_TOKEN, LANES)

    @functools.partial(
        pl.kernel, mesh=mesh,
        out_type=jax.ShapeDtypeStruct((SP, ROWS_PER_TOKEN, LANES), h3.dtype),
        scratch_types=[pltpu.VMEM((TOP_K, SC_W), I32), pltpu.VMEM((TOP_K, SC_W), I32),
                       pltpu.VMEM(tile, h3.dtype), pltpu.VMEM(tile, h3.dtype),
                       pltpu.SemaphoreType.DMA((2,)), pltpu.SemaphoreType.DMA((2,))],
    )
    def k(h_hbm, slot_hbm, xg_hbm, idx0, idx1, rows0, rows1, lsem, ssem):
        first = _sc_worker_id() * nchunk
        idx = (idx0, idx1)
        rows = (rows0, rows1)

        def loads(j, b):
            blk = first + j
            tok = pl.multiple_of(blk * SC_W, SC_W)
            return (pltpu.make_async_copy(slot_hbm.at[blk], idx[b], lsem.at[b]),
                    pltpu.make_async_copy(h_hbm.at[pl.ds(tok, SC_W)], rows[b], lsem.at[b]))

        def scatters(b):
            return [pltpu.make_async_copy(rows[b], xg_hbm.at[idx[b].at[kk]], ssem.at[b]) for kk in range(TOP_K)]

        for cp in loads(0, 0):
            cp.start()

        @pl.loop(0, nchunk, step=2)
        def _(j):
            for b in (0, 1):
                jj = j + b
                for cp in loads(jj, b):
                    cp.wait()
                for cp in scatters(b):
                    cp.start()

                @pl.when(jj + 1 < nchunk)
                def _():
                    @pl.when(jj >= 1)
                    def _():
                        for cp in scatters(1 - b):
                            cp.wait()
                    for cp in loads(jj + 1, 1 - b):
                        cp.start()

        for b in (0, 1):
            for cp in scatters(b):
                cp.wait()

    return k(h3, slot3)


def _gmm_kernel(tile_e_ref, tile_blk_ref, nvalid_ref, x_ref, wg_ref, wu_ref, wd_ref,
                y_ref, wgu_scr, wd_scr):
    j = pl.program_id(0)

    @pl.when(j < nvalid_ref[0])
    def _():
        prev = tile_e_ref[jnp.maximum(j - 1, 0)]

        @pl.when(jnp.logical_or(j == 0, tile_e_ref[j] != prev))
        def _():
            wgu_scr[:, 0:D_EXPERT] = wg_ref[0, 0].astype(BF16)
            wgu_scr[:, D_EXPERT:2 * D_EXPERT] = wu_ref[0, 0].astype(BF16)
            wd_scr[...] = wd_ref[0, 0].astype(BF16)

        for s in range(GM // GM_SUB):
            x_hi, x_lo = _unpack_rows(_load_token_tiles(x_ref, GM_SUB, s * GM_SUB))
            gu = (jnp.dot(x_hi.astype(BF16), wgu_scr[0:PACKED, :], preferred_element_type=F32)
                  + jnp.dot(x_lo.astype(BF16), wgu_scr[PACKED:D, :], preferred_element_type=F32))
            hid = _silu(gu[:, 0:D_EXPERT]) * gu[:, D_EXPERT:2 * D_EXPERT]
            y = jnp.dot(hid.astype(BF16), wd_scr[...], preferred_element_type=F32)
            _store_token_tiles(y_ref, _pack_rows(y), s * GM_SUB)


def _gmm(tile_e, tile_blk, nvalid, xg, l, w_gate, w_up, w_down):
    row_tile = pl.BlockSpec((GM * ROWS_PER_TOKEN, LANES), lambda j, te, tb, nv: (tb[j], 0))
    grid_spec = pltpu.PrefetchScalarGridSpec(
        num_scalar_prefetch=3,
        grid=(NT_MAX,),
        in_specs=[
            row_tile,
            pl.BlockSpec((1, 1, D, D_EXPERT), lambda j, te, tb, nv: (l, te[j], 0, 0)),
            pl.BlockSpec((1, 1, D, D_EXPERT), lambda j, te, tb, nv: (l, te[j], 0, 0)),
            pl.BlockSpec((1, 1, D_EXPERT, D), lambda j, te, tb, nv: (l, te[j], 0, 0)),
        ],
        out_specs=row_tile,
        scratch_shapes=[pltpu.VMEM((D, 2 * D_EXPERT), BF16), pltpu.VMEM((D_EXPERT, D), BF16)],
    )
    return pl.pallas_call(
        _gmm_kernel,
        grid_spec=grid_spec,
        out_shape=jax.ShapeDtypeStruct((SP * ROWS_PER_TOKEN, LANES), U32),
        compiler_params=_params(("arbitrary",)),
        name="moe_grouped_matmul",
    )(tile_e, tile_blk, nvalid, xg, w_gate, w_up, w_down)


def _sc_gather(table3, idx):
    n_idx = idx.shape[0]
    per_w = n_idx // SC_WORKERS
    nchunk = per_w // SC_W
    mesh = plsc.VectorSubcoreMesh(core_axis_name="c", subcore_axis_name="s")
    tile = (SC_W, ROWS_PER_TOKEN, LANES)

    @functools.partial(
        pl.kernel, mesh=mesh,
        out_type=jax.ShapeDtypeStruct((n_idx, ROWS_PER_TOKEN, LANES), table3.dtype),
        scratch_types=[pltpu.VMEM((per_w,), I32), pltpu.VMEM(tile, table3.dtype), pltpu.VMEM(tile, table3.dtype),
                       pltpu.SemaphoreType.DMA((2,)), pltpu.SemaphoreType.DMA((2,))],
    )
    def k(table_hbm, idx_hbm, out_hbm, idx_v, rows0, rows1, gsem, wsem):
        base = pl.multiple_of(_sc_worker_id() * per_w, per_w)
        rows = (rows0, rows1)
        pltpu.sync_copy(idx_hbm.at[pl.ds(base, per_w)], idx_v)

        def gather(j, b):
            ids = idx_v.at[pl.ds(pl.multiple_of(j * SC_W, SC_W), SC_W)]
            return pltpu.make_async_copy(table_hbm.at[ids], rows[b], gsem.at[b])

        def write(j, b):
            dst = out_hbm.at[pl.ds(pl.multiple_of(base + j * SC_W, SC_W), SC_W)]
            return pltpu.make_async_copy(rows[b], dst, wsem.at[b])

        gather(0, 0).start()

        @pl.loop(0, nchunk, step=2)
        def _(j):
            for b in (0, 1):
                jj = j + b
                gather(jj, b).wait()
                write(jj, b).start()

                @pl.when(jj + 1 < nchunk)
                def _():
                    @pl.when(jj >= 1)
                    def _():
                        write(jj - 1, 1 - b).wait()
                    gather(jj + 1, 1 - b).start()

        write(nchunk - 2, 0).wait()
        write(nchunk - 1, 1).wait()

    return k(table3, idx)


def _combine_kernel(wn_ref, x_ref, mod_ref, gpre_ref, gp_ref, wsg_ref, wsu_ref, wsd_ref, *rest):
    y_refs, o_refs = rest[:TOP_K], rest[TOP_K:]
    m = mod_ref[0, 0]
    x = x_ref[...]
    hb = (_rms(x, gpre_ref[...]) * (1.0 + m[4:5]) + m[3:4]).astype(BF16)
    hid = (_silu(jnp.dot(hb, wsg_ref[...], preferred_element_type=F32))
           * jnp.dot(hb, wsu_ref[...], preferred_element_type=F32))
    acc = jnp.dot(hid.astype(BF16), wsd_ref[...], preferred_element_type=F32)

    r = lax.broadcasted_iota(I32, (TD, TD), 0)
    c = lax.broadcasted_iota(I32, (TD, TD), 1)
    eye = jnp.where(r == c, 1.0, 0.0).astype(BF16)
    nt = lambda a, b: lax.dot_general(a, b, (((1,), (1,)), ((), ())), preferred_element_type=F32)
    w1, w2, w3 = _split3(wn_ref[...])
    w_t = nt(eye, w1) + nt(eye, w2) + nt(eye, w3)

    acc_hi = acc[:, :PACKED]
    acc_lo = acc[:, PACKED:]
    for k in range(TOP_K):
        y_hi, y_lo = _unpack_rows(_load_token_tiles(y_refs[k], TD))
        acc_hi = acc_hi + y_hi * w_t[:, k:k + 1]
        acc_lo = acc_lo + y_lo * w_t[:, k:k + 1]
    acc = jnp.concatenate([acc_hi, acc_lo], axis=1)
    out = x + m[5:6] * _rms(acc, gp_ref[...])
    if len(o_refs) == 1:
        o_refs[0][...] = out
    else:
        i = pl.program_id(0)

        @pl.when(i < TP // TD)
        def _():
            o_refs[0][...] = out

        @pl.when(i >= TP // TD)
        def _():
            o_refs[1][...] = out


def _combine(wn, x, mod, l, g_pre, g_post, ws_gate, ws_up, ws_down, ybuf, split_streams):
    tok = pl.BlockSpec((TD, D), lambda i: (i, 0))
    tiles_per_dec = DEC_SEQ // TD
    npd = TP // TD
    mod_row = lambda i: jnp.where(i < npd, 0, 1 + (i - npd) // tiles_per_dec)
    if split_streams:
        out_specs = [pl.BlockSpec((TD, D), lambda i: (jnp.minimum(i, npd - 1), 0)),
                     pl.BlockSpec((TD, D), lambda i: (jnp.maximum(i - npd, 0), 0))]
        out_shape = [jax.ShapeDtypeStruct((TP, D), F32), jax.ShapeDtypeStruct((TS, D), F32)]
    else:
        out_specs, out_shape = tok, jax.ShapeDtypeStruct((T, D), F32)
    full = lambda *shape: pl.BlockSpec(shape, lambda i: (0,) * len(shape))
    y_spec = lambda k: pl.BlockSpec((TD * ROWS_PER_TOKEN, LANES), lambda i: (k * (T // TD) + i, 0))
    return pl.pallas_call(
        _combine_kernel,
        grid=(T // TD,),
        in_specs=[
            pl.BlockSpec((TOP_K, TD), lambda i: (0, i)),
            tok,
            pl.BlockSpec((1, 1, 6, D), lambda i: (l, mod_row(i), 0, 0)),
            full(1, D), full(1, D), full(D, D_SHARED), full(D, D_SHARED), full(D_SHARED, D),
        ] + [y_spec(k) for k in range(TOP_K)],
        out_specs=out_specs,
        out_shape=out_shape,
        compiler_params=_params(("arbitrary",)),
        name="moe_combine",
    )(wn, x, mod, g_pre, g_post, ws_gate, ws_up, ws_down, *([ybuf] * TOP_K))


def _moe_layer(x, mod, l, g_pre, g_post, w_router, e_bias, w_gate, w_up, w_down,
               ws_gate, ws_up, ws_down, split_streams):
    h, top_e, wn, rk, cnt = _router(x, mod, l, g_pre, w_router.T, e_bias.reshape(N_EXPERTS, 1))
    cnt = cnt.reshape(N_EXPERTS).astype(I32)
    padded = (cnt + GM - 1) // GM * GM
    ends = jnp.cumsum(padded)
    offs = ends - padded
    eid = jnp.arange(N_EXPERTS, dtype=I32)[:, None, None]
    slot = rk + jnp.sum(jnp.where(top_e[None] == eid, offs[:, None, None], 0), axis=0)
    nvalid = ends[-1] // GM
    tile_start = jnp.arange(NT_MAX, dtype=I32) * GM
    tile_raw = jnp.sum((tile_start[:, None] >= ends[None, :]).astype(I32), axis=1)
    last = jnp.maximum(nvalid - 1, 0)
    tile_blk = jnp.minimum(jnp.arange(NT_MAX, dtype=I32), last)
    tile_e = jnp.minimum(tile_raw, N_EXPERTS - 1)
    tile_e = jnp.where(jnp.arange(NT_MAX) <= last, tile_e, tile_e[last])
    slot3 = slot.reshape(TOP_K, T // SC_W, SC_W).transpose(1, 0, 2)
    xg = _sc_dispatch(h.reshape(T, ROWS_PER_TOKEN, LANES), slot3)
    xg = _pad_fill(offs + cnt, padded - cnt, xg).reshape(SP * ROWS_PER_TOKEN, LANES)
    yg = _gmm(tile_e, tile_blk, nvalid.reshape(1), xg, l, w_gate, w_up, w_down)
    ybuf = _sc_gather(yg.reshape(SP, ROWS_PER_TOKEN, LANES), slot.reshape(TOP_K * T))
    return _combine(wn, x, mod, l, g_pre, g_post,
                    ws_gate.astype(BF16), ws_up.astype(BF16), ws_down.astype(BF16),
                    ybuf.reshape(TOP_K * T * ROWS_PER_TOKEN, LANES), split_streams)


def _rope_tables():
    n = DEC_SEQ
    rows = n // GRID_W
    row = jnp.repeat(jnp.arange(rows), GRID_W).astype(F32)
    col = jnp.tile(jnp.arange(GRID_W), rows).astype(F32)
    half = DQK_B // 2
    inv = ROPE_BASE ** (-jnp.arange(0, half, 2, dtype=F32) / half)
    ang_r = row[:, None] * inv
    ang_c = col[:, None] * inv
    ang = jnp.concatenate([ang_r, ang_r, ang_c, ang_c], axis=-1)
    reps = QK_B // DQK_B
    return jnp.tile(jnp.cos(ang), (1, reps)), jnp.tile(jnp.sin(ang), (1, reps))


def _pad_in_proj(w):
    s = [0, Q_A, 2 * Q_A, 2 * Q_A + V_A, 2 * Q_A + 2 * V_A]
    s += [s[-1] + GATE_RANK, s[-1] + 2 * GATE_RANK]
    s += [s[-1] + QK_B, s[-1] + 2 * QK_B, s[-1] + 2 * QK_B + V_B]
    gates = jnp.pad(w[:, s[4]:s[6]], ((0, 0), (0, GL_PAD - 2 * GATE_RANK)))
    return jnp.concatenate([w[:, s[0]:s[4]], gates, w[:, s[6]:s[9]]], axis=1).astype(BF16)


def kernel(x_prompt, x_sample, c, c_ctx, state_gla, cache_k, cache_v, ada_w, ada_b, norm_pre_mix, norm_post_mix, norm_pre_ffn, norm_post_ffn, ab_w_in, gla_w_g2, gla_b_g2, gla_norm_g, diff_lambda, diff_norm_g, ab_w_out, sgu_w_in, sgu_b_in, sgu_norm_g, sgu_w_s, sgu_b_s, sgu_w_out, moe_w_router, moe_e_bias, moe_w_gate, moe_w_up, moe_w_down, moe_ws_gate, moe_ws_up, moe_ws_down):
    depth = ada_w.shape[0]
    assert depth == 2, "layer 0 reads the two input streams, the last layer writes them back"
    xp, xs = x_prompt.reshape(TP, D), x_sample.reshape(TS, D)
    x = None
    cond = jnp.concatenate([c_ctx[None, :], c, jnp.zeros((8 - 1 - DEC_BATCH, D), F32)], axis=0)
    mod = _modulation(cond, ada_w, ada_b)
    cos, sin = _rope_tables()
    vec = lambda a: a.reshape(1, -1)
    new_s = new_k = new_v = None
    for l in range(depth):
        if l % 2 == 0:
            e = l // 2
            lam_init = 0.8 - 0.6 * math.exp(-0.3 * l)
            a, r_a, gl, q_b, k_b, v_b = _in_proj(xp, xs, mod, l, vec(norm_pre_mix[l]), _pad_in_proj(ab_w_in[e]), cos, sin)
            s0_t = jnp.swapaxes(state_gla[:, e], -1, -2)
            o_f, o_bw, s_fin_t = _gla(a, gl, gla_w_g2[e], gla_b_g2[e].reshape(2, 1, Q_A), s0_t)
            o_att = _diff_attention(q_b, k_b, v_b, cache_k, cache_v, diff_lambda[e], lam_init)
            x = _mix_out(lam_init, o_f, o_bw, r_a, o_att, xp, xs, mod, l, vec(gla_norm_g[e]), vec(diff_norm_g[e]),
                         ab_w_out[e].astype(BF16), vec(norm_post_mix[l]))
            new_s = jnp.swapaxes(s_fin_t, -1, -2)
            new_k = k_b[:TP].reshape(BATCH, SEQ, H_B, 2, DQK_B).transpose(0, 2, 3, 1, 4)
            new_v = v_b[:TP].reshape(BATCH, SEQ, H_B, DV_B).transpose(0, 2, 1, 3)
        else:
            o = l // 2
            x = _sgu(x, mod, l, vec(norm_pre_mix[l]), sgu_w_in[o].astype(BF16), vec(sgu_b_in[o]),
                     vec(sgu_norm_g[o]), sgu_w_s[o], sgu_b_s[o].T, sgu_w_out[o].astype(BF16),
                     vec(norm_post_mix[l]))
        x = _moe_layer(x, mod, l, vec(norm_pre_ffn[l]), vec(norm_post_ffn[l]), moe_w_router[l], moe_e_bias[l],
                       moe_w_gate, moe_w_up, moe_w_down, moe_ws_gate[l], moe_ws_up[l], moe_ws_down[l],
                       split_streams=(l == depth - 1))
    y_prompt = x[0].reshape(BATCH, SEQ, D)
    y_sample = x[1].reshape(DEC_BATCH, DEC_SEQ, D)
    return (y_prompt, y_sample, new_s[:, None], new_k[:, None], new_v[:, None])
```

```python
import functools
import math

import jax
import jax.numpy as jnp
from jax import lax
from jax.experimental import pallas as pl
from jax.experimental.pallas import tpu as pltpu
from jax.experimental.pallas import tpu_sc as plsc

F32 = jnp.float32
BF16 = jnp.bfloat16
I32 = jnp.int32

D = 1024
BATCH, SEQ = 32, 256
DEC_BATCH, DEC_SEQ = 4, 2048
PAST_LEN = 256
GRID_W = 64
EPS = 1e-6
TP = BATCH * SEQ
TS = DEC_BATCH * DEC_SEQ
T = TP + TS
H_A, DK_A, DV_A = 4, 64, 128
Q_A, V_A = H_A * DK_A, H_A * DV_A
GATE_RANK, GATE_TAU, GLA_CHUNK = 16, 16.0, 64
H_B, DQK_B, DV_B = 4, 64, 128
QK_B, V_B = H_B * 2 * DQK_B, H_B * DV_B
ROPE_BASE = 10000.0
SGU_DIM, SGU_GROUPS, SGU_CHUNK = 1024, 4, 128
N_EXPERTS, TOP_K, N_GROUPS, TOPK_GROUPS = 64, 8, 8, 4
GROUP_SIZE = N_EXPERTS // N_GROUPS
D_EXPERT, D_SHARED = 256, 256
ROUTED_SCALE = 2.5

TM = 512
NPT = TP // TM
TILES_PER_DEC = DEC_SEQ // TM
SEG = 256
NSEG = T // SEG
NSEG_P = TP // SEG
SEG_PER_DEC = DEC_SEQ // SEG
TR = 512
TD = 256
GM = 512
GM_SUB = 256
NT_MAX = T * TOP_K // GM + N_EXPERTS
SP = NT_MAX * GM
GL_PAD = 128
VMEM_LIMIT = 56 * 1024 * 1024
NEG_INF = float("-inf")


def _bdot(a, b):
    return jnp.dot(a.astype(BF16), b.astype(BF16), preferred_element_type=F32)


def _bdot_nt(a, b):
    return lax.dot_general(a.astype(BF16), b.astype(BF16), (((1,), (1,)), ((), ())),
                           preferred_element_type=F32)


def _bdot_tn(a, b):
    return lax.dot_general(a.astype(BF16), b.astype(BF16), (((0,), (0,)), ((), ())),
                           preferred_element_type=F32)


def _split3(x):
    x1 = x.astype(BF16)
    r1 = x - x1.astype(F32)
    x2 = r1.astype(BF16)
    x3 = (r1 - x2.astype(F32)).astype(BF16)
    return x1, x2, x3


def _rms(x, g):
    return x * lax.rsqrt(jnp.mean(x * x, axis=-1, keepdims=True) + EPS) * g


def _silu(x):
    return x * jax.nn.sigmoid(x)


def _mod_row(i):
    return jnp.where(i < NPT, 0, 1 + (i - NPT) // TILES_PER_DEC)


def _params(sem, limit=VMEM_LIMIT):
    return pltpu.CompilerParams(dimension_semantics=sem, vmem_limit_bytes=limit)


def _mod_kernel(c_ref, w_ref, b_ref, o_ref):
    o_ref[0] = _bdot(_silu(c_ref[...]), w_ref[0]) + b_ref[0]


def _modulation(cond, ada_w, ada_b):
    depth = ada_w.shape[0]
    nj = 6
    out = pl.pallas_call(
        _mod_kernel,
        grid=(depth, nj),
        in_specs=[
            pl.BlockSpec((8, D), lambda l, j: (0, 0)),
            pl.BlockSpec((1, D, D), lambda l, j: (l, 0, j)),
            pl.BlockSpec((1, 1, D), lambda l, j: (l, 0, j)),
        ],
        out_specs=pl.BlockSpec((1, 8, D), lambda l, j: (l, 0, j)),
        out_shape=jax.ShapeDtypeStruct((depth, 8, 6 * D), F32),
        compiler_params=_params(("arbitrary", "arbitrary")),
        name="adaln_modulation",
    )(cond, ada_w, ada_b.reshape(depth, 1, 6 * D))
    return out.reshape(depth, 8, 6, D)


_C_A, _C_R, _C_GL, _C_Q, _C_K, _C_V, _C_END = 0, 1024, 1536, 1664, 2176, 2688, 3200


def _rope(x, cos, sin):
    lane = lax.broadcasted_iota(I32, x.shape, 1)
    first = (lane % 32) < 16
    n = x.shape[1]
    xr = jnp.where(first, -pltpu.roll(x, n - 16, 1), pltpu.roll(x, 16, 1))
    return x * cos + xr * sin


def _stream_specs():
    return [pl.BlockSpec((TM, D), lambda i: (jnp.minimum(i, NPT - 1), 0)),
            pl.BlockSpec((TM, D), lambda i: (jnp.maximum(i - NPT, 0), 0))]


def _stream_tile(xp_ref, xs_ref):
    return jnp.where(pl.program_id(0) < NPT, xp_ref[...], xs_ref[...])


def _in_kernel(xp_ref, xs_ref, mod_ref, g_ref, w_ref, cos_ref, sin_ref,
               a_ref, r_ref, gl_ref, q_ref, k_ref, v_ref):
    i = pl.program_id(0)
    m = mod_ref[0, 0]
    h = _rms(_stream_tile(xp_ref, xs_ref), g_ref[...]) * (1.0 + m[1:2]) + m[0:1]
    hb = h.astype(BF16)

    def proj(c0, c1):
        return jnp.dot(hb, w_ref[:, c0:c1], preferred_element_type=F32)

    a_ref[...] = proj(_C_A, _C_R)
    r_ref[...] = proj(_C_R, _C_GL)
    gl_ref[...] = proj(_C_GL, _C_Q)
    v_ref[...] = proj(_C_V, _C_END)
    q = proj(_C_Q, _C_K)
    k = proj(_C_K, _C_V)

    @pl.when(i < NPT)
    def _():
        q_ref[...] = q
        k_ref[...] = k

    @pl.when(i >= NPT)
    def _():
        cos = cos_ref[...]
        sin = sin_ref[...]
        q_ref[...] = _rope(q, cos, sin)
        k_ref[...] = _rope(k, cos, sin)


def _in_proj(xp, xs, mod, l, g, w_pad, cos, sin):
    tok = lambda width: pl.BlockSpec((TM, width), lambda i: (i, 0))
    rope_spec = pl.BlockSpec((TM, QK_B), lambda i: (jnp.maximum(i - NPT, 0) % TILES_PER_DEC, 0))
    widths = (1024, 512, GL_PAD, 512, 512, 512)
    return pl.pallas_call(
        _in_kernel,
        grid=(T // TM,),
        in_specs=_stream_specs() + [
            pl.BlockSpec((1, 1, 6, D), lambda i: (l, _mod_row(i), 0, 0)),
            pl.BlockSpec((1, D), lambda i: (0, 0)),
            pl.BlockSpec((D, _C_END), lambda i: (0, 0)),
            rope_spec, rope_spec,
        ],
        out_specs=[tok(w) for w in widths],
        out_shape=[jax.ShapeDtypeStruct((T, w), F32) for w in widths],
        compiler_params=_params(("arbitrary",)),
        name="mixer_ab_in_proj",
    )(xp, xs, mod, g, w_pad, cos, sin)


def _log_sigmoid(x):
    return jnp.minimum(x, 0.0) - jnp.log(1.0 + jnp.exp(-jnp.abs(x)))


def _gla_kernel(af_ref, ab_ref, glf_ref, glb_ref, wg_ref, bg_ref, s0_ref,
                of_ref, ob_ref, sfin_ref, st_ref):
    i = pl.program_id(0)

    @pl.when(i < NSEG_P)
    def _():
        st_ref[...] = jnp.zeros_like(st_ref)

    @pl.when(jnp.logical_and(i >= NSEG_P, (i - NSEG_P) % SEG_PER_DEC == 0))
    def _():
        st_ref[...] = s0_ref[0]

    r = lax.broadcasted_iota(I32, (SEG, SEG), 0)
    c = lax.broadcasted_iota(I32, (SEG, SEG), 1)
    same = (r // GLA_CHUNK) == (c // GLA_CHUNK)
    nchunk = SEG // GLA_CHUNK
    own_head = (lax.broadcasted_iota(I32, (V_A, Q_A), 0) // DV_A) == (lax.broadcasted_iota(I32, (V_A, Q_A), 1) // DK_A)

    for d, (a_ref, gl_ref, o_ref) in enumerate(((af_ref, glf_ref, of_ref), (ab_ref, glb_ref, ob_ref))):
        fwd = d == 0
        gcol = gl_ref[:, d * GATE_RANK:(d + 1) * GATE_RANK]
        la = _log_sigmoid(_bdot(gcol, wg_ref[d]) + bg_ref[d]) / GATE_TAU
        causal = jnp.logical_and(same, (c <= r) if fwd else (c >= r))
        tri = jnp.where(causal, 1.0, 0.0).astype(BF16)
        l1, l2, l3 = _split3(la)
        b_all = (jnp.dot(tri, l1, preferred_element_type=F32)
                 + jnp.dot(tri, l2, preferred_element_type=F32)
                 + jnp.dot(tri, l3, preferred_element_type=F32))
        q_in_all = a_ref[:, 0:Q_A] * (DK_A ** -0.5) * jnp.exp(b_all)
        kd_all = a_ref[:, Q_A:2 * Q_A] * jnp.exp(-b_all)
        intra = []
        for h in range(H_A):
            kc = slice(h * DK_A, (h + 1) * DK_A)
            attn = jnp.where(causal, _bdot_nt(q_in_all[:, kc], kd_all[:, kc]), 0.0)
            intra.append(_bdot(attn, a_ref[:, 2 * Q_A + h * DV_A:2 * Q_A + (h + 1) * DV_A]))
        intra = jnp.concatenate(intra, axis=1)
        state = st_ref[d]
        order = range(nchunk) if fwd else range(nchunk - 1, -1, -1)
        for ch in order:
            r0 = ch * GLA_CHUNK
            rows = slice(r0, r0 + GLA_CHUNK)
            end = r0 + GLA_CHUNK - 1 if fwd else r0
            b_end = b_all[end:end + 1, :]
            kw = a_ref[rows, Q_A:2 * Q_A] * jnp.exp(b_end - b_all[rows, :])
            o_ref[rows, :] = intra[rows, :] + _bdot_nt(q_in_all[rows, :], state)
            kv_t = _bdot_tn(a_ref[rows, 2 * Q_A:2 * Q_A + V_A], kw)
            state = state * jnp.exp(b_end) + jnp.where(own_head, kv_t, 0.0)
        st_ref[d] = state

    @pl.when(i < NSEG_P)
    def _():
        for d in range(2):
            for h in range(H_A):
                sfin_ref[0, d, h] = st_ref[d, h * DV_A:(h + 1) * DV_A, h * DK_A:(h + 1) * DK_A]


def _seg_bwd(i):
    j = i - NSEG_P
    return jnp.where(i < NSEG_P, i, NSEG_P + (j // SEG_PER_DEC) * SEG_PER_DEC + (SEG_PER_DEC - 1 - j % SEG_PER_DEC))


def _gla(a, gl, wg, bg, s0_t):
    seg = lambda width, f: pl.BlockSpec((SEG, width), lambda i: (f(i), 0))
    ident = lambda i: i
    st_block = (1, 2, H_A, DV_A, DK_A)
    return pl.pallas_call(
        _gla_kernel,
        grid=(NSEG,),
        in_specs=[
            seg(D, ident), seg(D, _seg_bwd), seg(GL_PAD, ident), seg(GL_PAD, _seg_bwd),
            pl.BlockSpec((2, GATE_RANK, Q_A), lambda i: (0, 0, 0)),
            pl.BlockSpec((2, 1, Q_A), lambda i: (0, 0, 0)),
            pl.BlockSpec((1, 2, V_A, Q_A), lambda i: (jnp.maximum(i - NSEG_P, 0) // SEG_PER_DEC, 0, 0, 0)),
        ],
        out_specs=[
            seg(V_A, ident), seg(V_A, _seg_bwd),
            pl.BlockSpec(st_block, lambda i: (jnp.minimum(i, NSEG_P - 1), 0, 0, 0, 0)),
        ],
        out_shape=[
            jax.ShapeDtypeStruct((T, V_A), F32),
            jax.ShapeDtypeStruct((T, V_A), F32),
            jax.ShapeDtypeStruct((BATCH, 2, H_A, DV_A, DK_A), F32),
        ],
        scratch_shapes=[pltpu.VMEM((2, V_A, Q_A), F32)],
        compiler_params=_params(("arbitrary",)),
        name="gla_bidir",
    )(a, a, gl, gl, wg, bg, s0_t)


def _diff_lambda(lam_ref, lam_init):
    lp = lam_ref[...]
    s01 = jnp.sum(lp[0:1] * lp[1:2], axis=1, keepdims=True)
    s23 = jnp.sum(lp[2:3] * lp[3:4], axis=1, keepdims=True)
    return jnp.exp(s01) - jnp.exp(s23) + lam_init


def _attn_prompt_kernel(lam_init, q_ref, k_ref, v_ref, lam_ref, o_ref):
    lam = _diff_lambda(lam_ref, lam_init)
    for h in range(H_B):
        ps = []
        for m in range(2):
            cols = slice((2 * h + m) * DQK_B, (2 * h + m + 1) * DQK_B)
            s = _bdot_nt(q_ref[:, cols] * (DQK_B ** -0.5), k_ref[:, cols])
            e = jnp.exp(s - jnp.max(s, axis=1, keepdims=True))
            ps.append(e * (1.0 / jnp.sum(e, axis=1, keepdims=True)))
        w = ps[0] - lam * ps[1]
        o_ref[:, h * DV_B:(h + 1) * DV_B] = _bdot(w, v_ref[:, h * DV_B:(h + 1) * DV_B])


def _attn_sample_kernel(lam_init, q_ref, k_ref, v_ref, ck_ref, cv_ref, lam_ref, o_ref):
    lam = _diff_lambda(lam_ref, lam_init)
    for h in range(H_B):
        parts = []
        for m in range(2):
            cols = slice((2 * h + m) * DQK_B, (2 * h + m + 1) * DQK_B)
            q = q_ref[:, cols] * (DQK_B ** -0.5)
            sc = _bdot_nt(q, ck_ref[0, 0, h, m])
            sn = _bdot_nt(q, k_ref[:, cols])
            mx = jnp.maximum(jnp.max(sc, axis=1, keepdims=True), jnp.max(sn, axis=1, keepdims=True))
            ec = jnp.exp(sc - mx)
            en = jnp.exp(sn - mx)
            inv = (1.0 if m == 0 else -lam) / (jnp.sum(ec, axis=1, keepdims=True) + jnp.sum(en, axis=1, keepdims=True))
            parts.append((ec * inv, en * inv))
        wc = parts[0][0] + parts[1][0]
        wn = parts[0][1] + parts[1][1]
        o_ref[:, h * DV_B:(h + 1) * DV_B] = (_bdot(wc, cv_ref[0, 0, h])
                                             + _bdot(wn, v_ref[:, h * DV_B:(h + 1) * DV_B]))


QB = SEQ
NQB_DEC = DEC_SEQ // QB


def _attn_kernel(lam_init, q_ref, kp_ref, vp_ref, ks_ref, vs_ref, ck_ref, cv_ref, lam_ref, o_ref):
    i = pl.program_id(0)

    @pl.when(i < BATCH)
    def _():
        _attn_prompt_kernel(lam_init, q_ref, kp_ref, vp_ref, lam_ref, o_ref)

    @pl.when(i >= BATCH)
    def _():
        _attn_sample_kernel(lam_init, q_ref, ks_ref, vs_ref, ck_ref, cv_ref, lam_ref, o_ref)


def _diff_attention(q, k, v, cache_k, cache_v, lam_p, lam_init):
    blk = lambda rows, f: pl.BlockSpec((rows, 512), f)
    dec_b = lambda i: jnp.maximum(i - BATCH, 0) // NQB_DEC
    own = lambda i: (i, 0)
    prompt_kv = lambda i: (jnp.minimum(i, BATCH - 1), 0)
    dec_kv = lambda i: (TP // DEC_SEQ + dec_b(i), 0)
    return pl.pallas_call(
        functools.partial(_attn_kernel, lam_init),
        grid=(BATCH + DEC_BATCH * NQB_DEC,),
        in_specs=[
            blk(QB, own), blk(SEQ, prompt_kv), blk(SEQ, prompt_kv), blk(DEC_SEQ, dec_kv), blk(DEC_SEQ, dec_kv),
            pl.BlockSpec((1, 1, H_B, 2, PAST_LEN, DQK_B), lambda i: (dec_b(i), 0, 0, 0, 0, 0)),
            pl.BlockSpec((1, 1, H_B, PAST_LEN, DV_B), lambda i: (dec_b(i), 0, 0, 0, 0)),
            pl.BlockSpec((4, DQK_B), lambda i: (0, 0)),
        ],
        out_specs=blk(QB, own),
        out_shape=jax.ShapeDtypeStruct((T, V_B), F32),
        compiler_params=_params(("arbitrary",)),
        name="diff_attention",
    )(q, k, v, k, v, cache_k, cache_v, lam_p)


def _head_rms(x, g, nheads, width):
    return jnp.concatenate([_rms(x[:, h * width:(h + 1) * width], g) for h in range(nheads)], axis=1)


def _mix_out_kernel(lam_init, of_ref, ob_ref, r_ref, oatt_ref, xp_ref, xs_ref, mod_ref,
                    gg_ref, dg_ref, wo_ref, gp_ref, o_ref):
    m = mod_ref[0, 0]
    o_a = _head_rms(of_ref[...] + ob_ref[...], gg_ref[...], H_A, DV_A) * _silu(r_ref[...])
    o_b = _head_rms(oatt_ref[...], dg_ref[...], H_B, DV_B) * (1.0 - lam_init)
    out = _bdot(o_a, wo_ref[0:V_A, :]) + _bdot(o_b, wo_ref[V_A:V_A + V_B, :])
    o_ref[...] = _stream_tile(xp_ref, xs_ref) + m[2:3] * _rms(out, gp_ref[...])


def _mix_out(lam_init, o_f, o_b, r_a, o_att, xp, xs, mod, l, gla_g, diff_g, w_o, g_post):
    tok = lambda width: pl.BlockSpec((TM, width), lambda i: (i, 0))
    vec = lambda width: pl.BlockSpec((1, width), lambda i: (0, 0))
    return pl.pallas_call(
        functools.partial(_mix_out_kernel, lam_init),
        grid=(T // TM,),
        in_specs=[
            tok(512), tok(512), tok(512), tok(512), *_stream_specs(),
            pl.BlockSpec((1, 1, 6, D), lambda i: (l, _mod_row(i), 0, 0)),
            vec(DV_A), vec(DV_B),
            pl.BlockSpec((V_A + V_B, D), lambda i: (0, 0)),
            vec(D),
        ],
        out_specs=tok(D),
        out_shape=jax.ShapeDtypeStruct((T, D), F32),
        compiler_params=_params(("arbitrary",)),
        name="mixer_ab_out",
    )(o_f, o_b, r_a, o_att, xp, xs, mod, gla_g, diff_g, w_o, g_post)


def _gelu_tanh(x):
    return 0.5 * x * (1.0 + jnp.tanh(math.sqrt(2.0 / math.pi) * (x + 0.044715 * (x * x * x))))


def _sgu_kernel(x_ref, mod_ref, gpre_ref, win_ref, bin_ref, vg_ref, ws_ref, bs_ref,
                wout_ref, gpost_ref, o_ref, t_ref):
    m = mod_ref[0, 0]
    x = x_ref[...]
    h = _rms(x, gpre_ref[...]) * (1.0 + m[1:2]) + m[0:1]
    z = _gelu_tanh(_bdot(h, win_ref[...]) + bin_ref[...])
    v = _rms(z[:, SGU_DIM:], vg_ref[...])
    gw = SGU_DIM // SGU_GROUPS
    for ch in range(TM // SGU_CHUNK):
        rows = slice(ch * SGU_CHUNK, (ch + 1) * SGU_CHUNK)
        for g in range(SGU_GROUPS):
            cols = slice(g * gw, (g + 1) * gw)
            vs = _bdot(ws_ref[g], v[rows, cols]) + bs_ref[:, g:g + 1]
            t_ref[rows, cols] = (z[rows, cols] * vs).astype(BF16)
    out = jnp.dot(t_ref[...], wout_ref[...], preferred_element_type=F32)
    o_ref[...] = x + m[2:3] * _rms(out, gpost_ref[...])


def _sgu(x, mod, l, g_pre, w_in, b_in, v_g, w_s, b_s_t, w_out, g_post):
    tok = pl.BlockSpec((TM, D), lambda i: (i, 0))
    full = lambda *shape: pl.BlockSpec(shape, lambda i: (0,) * len(shape))
    return pl.pallas_call(
        _sgu_kernel,
        grid=(T // TM,),
        in_specs=[
            tok,
            pl.BlockSpec((1, 1, 6, D), lambda i: (l, _mod_row(i), 0, 0)),
            full(1, D), full(D, 2 * SGU_DIM), full(1, 2 * SGU_DIM), full(1, SGU_DIM),
            full(SGU_GROUPS, SGU_CHUNK, SGU_CHUNK), full(SGU_CHUNK, SGU_GROUPS),
            full(SGU_DIM, D), full(1, D),
        ],
        out_specs=tok,
        out_shape=jax.ShapeDtypeStruct((T, D), F32),
        scratch_shapes=[pltpu.VMEM((TM, SGU_DIM), BF16)],
        compiler_params=_params(("arbitrary",)),
        name="sgu_mixer",
    )(x, mod, g_pre, w_in, b_in, v_g, w_s, b_s_t, w_out, g_post)


LANES = 128
U32 = jnp.uint32
PACKED = D // 2
ROWS_PER_TOKEN = PACKED // LANES
HIGH_HALF = 0xFFFF0000


def _pack_rows(x):
    bits = lax.bitcast_convert_type(x.astype(BF16).astype(F32), U32)
    return bits[:, :PACKED] | (bits[:, PACKED:] >> 16)


def _unpack_rows(u):
    return (lax.bitcast_convert_type(u & U32(HIGH_HALF), F32), lax.bitcast_convert_type(u << 16, F32))


def _store_token_tiles(ref, u, first=0):
    n = u.shape[0]
    for c in range(ROWS_PER_TOKEN):
        ref[pl.ds(first * ROWS_PER_TOKEN + c, n, stride=ROWS_PER_TOKEN), :] = u[:, c * LANES:(c + 1) * LANES]


def _load_token_tiles(ref, n, first=0):
    return jnp.concatenate([ref[pl.ds(first * ROWS_PER_TOKEN + c, n, stride=ROWS_PER_TOKEN), :]
                            for c in range(ROWS_PER_TOKEN)], axis=1)


def _router_kernel(x_ref, mod_ref, g_ref, wr_ref, eb_ref,
                   h_ref, te_ref, wn_ref, rk_ref, cnt_ref, carry_ref):
    i = pl.program_id(0)

    @pl.when(i == 0)
    def _():
        carry_ref[...] = jnp.zeros_like(carry_ref)

    m = mod_ref[0, 0]
    h = _rms(x_ref[...], g_ref[...]) * (1.0 + m[4:5]) + m[3:4]
    _store_token_tiles(h_ref, _pack_rows(h))
    h1, h2, _ = _split3(h)
    w1, w2, _ = _split3(wr_ref[...])
    nt = lambda a, b: lax.dot_general(a, b, (((1,), (1,)), ((), ())), preferred_element_type=F32)
    logits = nt(w1, h1) + nt(w1, h2) + nt(w2, h1)
    scores = jax.nn.sigmoid(logits)
    sel = scores + eb_ref[...]

    row8 = lax.broadcasted_iota(I32, (GROUP_SIZE, TR), 0)
    gscore = []
    for g in range(N_GROUPS):
        xg = sel[g * GROUP_SIZE:(g + 1) * GROUP_SIZE]
        m1 = jnp.max(xg, axis=0, keepdims=True)
        i1 = jnp.min(jnp.where(xg == m1, row8, GROUP_SIZE), axis=0, keepdims=True)
        m2 = jnp.max(jnp.where(row8 == i1, NEG_INF, xg), axis=0, keepdims=True)
        gscore.append(m1 + m2)
    pieces = []
    for g in range(N_GROUPS):
        rank = jnp.zeros((1, TR), I32)
        for g2 in range(N_GROUPS):
            if g2 == g:
                continue
            beats = (gscore[g2] >= gscore[g]) if g2 < g else (gscore[g2] > gscore[g])
            rank = rank + beats.astype(I32)
        pieces.append(jnp.where(rank < TOPK_GROUPS, sel[g * GROUP_SIZE:(g + 1) * GROUP_SIZE], NEG_INF))
    cur = jnp.concatenate(pieces, axis=0)

    row = lax.broadcasted_iota(I32, (N_EXPERTS, TR), 0)
    idxs, ws = [], []
    for _ in range(TOP_K):
        mx = jnp.max(cur, axis=0, keepdims=True)
        idx = jnp.min(jnp.where(cur == mx, row, N_EXPERTS), axis=0, keepdims=True)
        hit = row == idx
        ws.append(jnp.sum(jnp.where(hit, scores, 0.0), axis=0, keepdims=True))
        cur = jnp.where(hit, NEG_INF, cur)
        idxs.append(idx)
    mask = jnp.zeros((N_EXPERTS, TR), F32)
    for idx in idxs:
        mask = mask + (row == idx).astype(F32)
    wsum = ws[0]
    for wk in ws[1:]:
        wsum = wsum + wk

    tj = lax.broadcasted_iota(I32, (TR, TR), 0)
    ti = lax.broadcasted_iota(I32, (TR, TR), 1)
    upper = jnp.where(tj < ti, 1.0, 0.0).astype(BF16)
    pos = carry_ref[...] + jnp.dot(mask.astype(BF16), upper, preferred_element_type=F32)
    for k in range(TOP_K):
        hit = row == idxs[k]
        te_ref[k:k + 1, :] = idxs[k]
        wn_ref[k:k + 1, :] = ws[k] / wsum * ROUTED_SCALE
        rk_ref[k:k + 1, :] = jnp.sum(jnp.where(hit, pos, 0.0), axis=0, keepdims=True).astype(I32)
    carry_ref[...] = carry_ref[...] + jnp.sum(mask, axis=1, keepdims=True)
    cnt_ref[...] = carry_ref[...]


def _router(x, mod, l, g, wr_t, e_bias):
    kt = lambda dtype: jax.ShapeDtypeStruct((TOP_K, T), dtype)
    kt_spec = pl.BlockSpec((TOP_K, TR), lambda i: (0, i))
    tiles_per_dec = DEC_SEQ // TR
    mod_row = lambda i: jnp.where(i < TP // TR, 0, 1 + (i - TP // TR) // tiles_per_dec)
    return pl.pallas_call(
        _router_kernel,
        grid=(T // TR,),
        in_specs=[
            pl.BlockSpec((TR, D), lambda i: (i, 0)),
            pl.BlockSpec((1, 1, 6, D), lambda i: (l, mod_row(i), 0, 0)),
            pl.BlockSpec((1, D), lambda i: (0, 0)),
            pl.BlockSpec((N_EXPERTS, D), lambda i: (0, 0)),
            pl.BlockSpec((N_EXPERTS, 1), lambda i: (0, 0)),
        ],
        out_specs=[
            pl.BlockSpec((TR * ROWS_PER_TOKEN, LANES), lambda i: (i, 0)),
            kt_spec, kt_spec, kt_spec,
            pl.BlockSpec((N_EXPERTS, 1), lambda i: (0, 0)),
        ],
        out_shape=[
            jax.ShapeDtypeStruct((T * ROWS_PER_TOKEN, LANES), U32), kt(I32), kt(F32), kt(I32),
            jax.ShapeDtypeStruct((N_EXPERTS, 1), F32),
        ],
        scratch_shapes=[pltpu.VMEM((N_EXPERTS, 1), F32)],
        compiler_params=_params(("arbitrary",)),
        name="moe_router",
    )(x, mod, g, wr_t, e_bias)


_PAD_BITS = tuple(1 << b for b in range(GM.bit_length() - 1))


def _pad_fill_kernel(pad_start_ref, pad_len_ref, xg_in_ref, xg_ref, zero_ref, sem):
    del xg_in_ref
    zero_ref[...] = jnp.zeros_like(zero_ref)

    def pad_copies(e):
        start = pad_start_ref[e]
        n = pad_len_ref[e]
        copies = []
        for bit in _PAD_BITS:
            first = start + (n & ~(2 * bit - 1))
            copies.append(((n & bit) != 0, pltpu.make_async_copy(
                zero_ref.at[pl.ds(0, bit)], xg_ref.at[pl.ds(first, bit)], sem)))
        return copies

    def start_e(e, carry):
        for on, cp in pad_copies(e):
            @pl.when(on)
            def _():
                cp.start()
        return carry

    def wait_e(e, carry):
        for on, cp in pad_copies(e):
            @pl.when(on)
            def _():
                cp.wait()
        return carry

    lax.fori_loop(0, N_EXPERTS, start_e, 0)
    lax.fori_loop(0, N_EXPERTS, wait_e, 0)


def _pad_fill(pad_start, pad_len, xg):
    grid_spec = pltpu.PrefetchScalarGridSpec(
        num_scalar_prefetch=2,
        grid=(1,),
        in_specs=[pl.BlockSpec(memory_space=pl.ANY)],
        out_specs=pl.BlockSpec(memory_space=pl.ANY),
        scratch_shapes=[pltpu.VMEM((GM // 2, ROWS_PER_TOKEN, LANES), xg.dtype), pltpu.SemaphoreType.DMA],
    )
    return pl.pallas_call(
        _pad_fill_kernel,
        grid_spec=grid_spec,
        out_shape=jax.ShapeDtypeStruct(xg.shape, xg.dtype),
        input_output_aliases={2: 0},
        compiler_params=_params(("arbitrary",)),
        name="moe_pad_fill",
    )(pad_start, pad_len, xg)


SC_CORES, SC_SUBCORES = 2, 16
SC_WORKERS = SC_CORES * SC_SUBCORES
SC_W = 64


def _sc_worker_id():
    return lax.axis_index("s") * SC_CORES + lax.axis_index("c")


def _sc_dispatch(h3, slot3):
    nchunk = T // SC_WORKERS // SC_W
    mesh = plsc.VectorSubcoreMesh(core_axis_name="c", subcore_axis_name="s")
    tile = (SC_W, ROWS_PER_TOKEN, LANES)

    @functools.partial(
        pl.kernel, mesh=mesh,
        out_type=jax.ShapeDtypeStruct((SP, ROWS_PER_TOKEN, LANES), h3.dtype),
        scratch_types=[pltpu.VMEM((TOP_K, SC_W), I32), pltpu.VMEM((TOP_K, SC_W), I32),
                       pltpu.VMEM(tile, h3.dtype), pltpu.VMEM(tile, h3.dtype),
                       pltpu.SemaphoreType.DMA((2,)), pltpu.SemaphoreType.DMA((2,))],
    )
    def k(h_hbm, slot_hbm, xg_hbm, idx0, idx1, rows0, rows1, lsem, ssem):
        first = _sc_worker_id() * nchunk
        idx = (idx0, idx1)
        rows = (rows0, rows1)

        def loads(j, b):
            blk = first + j
            tok = pl.multiple_of(blk * SC_W, SC_W)
            return (pltpu.make_async_copy(slot_hbm.at[blk], idx[b], lsem.at[b]),
                    pltpu.make_async_copy(h_hbm.at[pl.ds(tok, SC_W)], rows[b], lsem.at[b]))

        def scatters(b):
            return [pltpu.make_async_copy(rows[b], xg_hbm.at[idx[b].at[kk]], ssem.at[b]) for kk in range(TOP_K)]

        for cp in loads(0, 0):
            cp.start()

        @pl.loop(0, nchunk, step=2)
        def _(j):
            for b in (0, 1):
                jj = j + b
                for cp in loads(jj, b):
                    cp.wait()
                for cp in scatters(b):
                    cp.start()

                @pl.when(jj + 1 < nchunk)
                def _():
                    @pl.when(jj >= 1)
                    def _():
                        for cp in scatters(1 - b):
                            cp.wait()
                    for cp in loads(jj + 1, 1 - b):
                        cp.start()

        for b in (0, 1):
            for cp in scatters(b):
                cp.wait()

    return k(h3, slot3)


def _gmm_kernel(tile_e_ref, tile_blk_ref, nvalid_ref, x_ref, wg_ref, wu_ref, wd_ref,
                y_ref, wgu_scr, wd_scr):
    j = pl.program_id(0)

    @pl.when(j < nvalid_ref[0])
    def _():
        prev = tile_e_ref[jnp.maximum(j - 1, 0)]

        @pl.when(jnp.logical_or(j == 0, tile_e_ref[j] != prev))
        def _():
            wgu_scr[:, 0:D_EXPERT] = wg_ref[0, 0].astype(BF16)
            wgu_scr[:, D_EXPERT:2 * D_EXPERT] = wu_ref[0, 0].astype(BF16)
            wd_scr[...] = wd_ref[0, 0].astype(BF16)

        for s in range(GM // GM_SUB):
            x_hi, x_lo = _unpack_rows(_load_token_tiles(x_ref, GM_SUB, s * GM_SUB))
            gu = (jnp.dot(x_hi.astype(BF16), wgu_scr[0:PACKED, :], preferred_element_type=F32)
                  + jnp.dot(x_lo.astype(BF16), wgu_scr[PACKED:D, :], preferred_element_type=F32))
            hid = _silu(gu[:, 0:D_EXPERT]) * gu[:, D_EXPERT:2 * D_EXPERT]
            y = jnp.dot(hid.astype(BF16), wd_scr[...], preferred_element_type=F32)
            _store_token_tiles(y_ref, _pack_rows(y), s * GM_SUB)


def _gmm(tile_e, tile_blk, nvalid, xg, l, w_gate, w_up, w_down):
    row_tile = pl.BlockSpec((GM * ROWS_PER_TOKEN, LANES), lambda j, te, tb, nv: (tb[j], 0))
    grid_spec = pltpu.PrefetchScalarGridSpec(
        num_scalar_prefetch=3,
        grid=(NT_MAX,),
        in_specs=[
            row_tile,
            pl.BlockSpec((1, 1, D, D_EXPERT), lambda j, te, tb, nv: (l, te[j], 0, 0)),
            pl.BlockSpec((1, 1, D, D_EXPERT), lambda j, te, tb, nv: (l, te[j], 0, 0)),
            pl.BlockSpec((1, 1, D_EXPERT, D), lambda j, te, tb, nv: (l, te[j], 0, 0)),
        ],
        out_specs=row_tile,
        scratch_shapes=[pltpu.VMEM((D, 2 * D_EXPERT), BF16), pltpu.VMEM((D_EXPERT, D), BF16)],
    )
    return pl.pallas_call(
        _gmm_kernel,
        grid_spec=grid_spec,
        out_shape=jax.ShapeDtypeStruct((SP * ROWS_PER_TOKEN, LANES), U32),
        compiler_params=_params(("arbitrary",)),
        name="moe_grouped_matmul",
    )(tile_e, tile_blk, nvalid, xg, w_gate, w_up, w_down)


def _sc_gather(table3, idx):
    n_idx = idx.shape[0]
    per_w = n_idx // SC_WORKERS
    nchunk = per_w // SC_W
    mesh = plsc.VectorSubcoreMesh(core_axis_name="c", subcore_axis_name="s")
    tile = (SC_W, ROWS_PER_TOKEN, LANES)

    @functools.partial(
        pl.kernel, mesh=mesh,
        out_type=jax.ShapeDtypeStruct((n_idx, ROWS_PER_TOKEN, LANES), table3.dtype),
        scratch_types=[pltpu.VMEM((per_w,), I32), pltpu.VMEM(tile, table3.dtype), pltpu.VMEM(tile, table3.dtype),
                       pltpu.SemaphoreType.DMA((2,)), pltpu.SemaphoreType.DMA((2,))],
    )
    def k(table_hbm, idx_hbm, out_hbm, idx_v, rows0, rows1, gsem, wsem):
        base = pl.multiple_of(_sc_worker_id() * per_w, per_w)
        rows = (rows0, rows1)
        pltpu.sync_copy(idx_hbm.at[pl.ds(base, per_w)], idx_v)

        def gather(j, b):
            ids = idx_v.at[pl.ds(pl.multiple_of(j * SC_W, SC_W), SC_W)]
            return pltpu.make_async_copy(table_hbm.at[ids], rows[b], gsem.at[b])

        def write(j, b):
            dst = out_hbm.at[pl.ds(pl.multiple_of(base + j * SC_W, SC_W), SC_W)]
            return pltpu.make_async_copy(rows[b], dst, wsem.at[b])

        gather(0, 0).start()

        @pl.loop(0, nchunk, step=2)
        def _(j):
            for b in (0, 1):
                jj = j + b
                gather(jj, b).wait()
                write(jj, b).start()

                @pl.when(jj + 1 < nchunk)
                def _():
                    @pl.when(jj >= 1)
                    def _():
                        write(jj - 1, 1 - b).wait()
                    gather(jj + 1, 1 - b).start()

        write(nchunk - 2, 0).wait()
        write(nchunk - 1, 1).wait()

    return k(table3, idx)


def _combine_kernel(wn_ref, x_ref, mod_ref, gpre_ref, gp_ref, wsg_ref, wsu_ref, wsd_ref, *rest):
    y_refs, o_refs = rest[:TOP_K], rest[TOP_K:]
    m = mod_ref[0, 0]
    x = x_ref[...]
    hb = (_rms(x, gpre_ref[...]) * (1.0 + m[4:5]) + m[3:4]).astype(BF16)
    hid = (_silu(jnp.dot(hb, wsg_ref[...], preferred_element_type=F32))
           * jnp.dot(hb, wsu_ref[...], preferred_element_type=F32))
    acc = jnp.dot(hid.astype(BF16), wsd_ref[...], preferred_element_type=F32)

    r = lax.broadcasted_iota(I32, (TD, TD), 0)
    c = lax.broadcasted_iota(I32, (TD, TD), 1)
    eye = jnp.where(r == c, 1.0, 0.0).astype(BF16)
    nt = lambda a, b: lax.dot_general(a, b, (((1,), (1,)), ((), ())), preferred_element_type=F32)
    w1, w2, w3 = _split3(wn_ref[...])
    w_t = nt(eye, w1) + nt(eye, w2) + nt(eye, w3)

    acc_hi = acc[:, :PACKED]
    acc_lo = acc[:, PACKED:]
    for k in range(TOP_K):
        y_hi, y_lo = _unpack_rows(_load_token_tiles(y_refs[k], TD))
        acc_hi = acc_hi + y_hi * w_t[:, k:k + 1]
        acc_lo = acc_lo + y_lo * w_t[:, k:k + 1]
    acc = jnp.concatenate([acc_hi, acc_lo], axis=1)
    out = x + m[5:6] * _rms(acc, gp_ref[...])
    if len(o_refs) == 1:
        o_refs[0][...] = out
    else:
        i = pl.program_id(0)

        @pl.when(i < TP // TD)
        def _():
            o_refs[0][...] = out

        @pl.when(i >= TP // TD)
        def _():
            o_refs[1][...] = out


def _combine(wn, x, mod, l, g_pre, g_post, ws_gate, ws_up, ws_down, ybuf, split_streams):
    tok = pl.BlockSpec((TD, D), lambda i: (i, 0))
    tiles_per_dec = DEC_SEQ // TD
    npd = TP // TD
    mod_row = lambda i: jnp.where(i < npd, 0, 1 + (i - npd) // tiles_per_dec)
    if split_streams:
        out_specs = [pl.BlockSpec((TD, D), lambda i: (jnp.minimum(i, npd - 1), 0)),
                     pl.BlockSpec((TD, D), lambda i: (jnp.maximum(i - npd, 0), 0))]
        out_shape = [jax.ShapeDtypeStruct((TP, D), F32), jax.ShapeDtypeStruct((TS, D), F32)]
    else:
        out_specs, out_shape = tok, jax.ShapeDtypeStruct((T, D), F32)
    full = lambda *shape: pl.BlockSpec(shape, lambda i: (0,) * len(shape))
    y_spec = lambda k: pl.BlockSpec((TD * ROWS_PER_TOKEN, LANES), lambda i: (k * (T // TD) + i, 0))
    return pl.pallas_call(
        _combine_kernel,
        grid=(T // TD,),
        in_specs=[
            pl.BlockSpec((TOP_K, TD), lambda i: (0, i)),
            tok,
            pl.BlockSpec((1, 1, 6, D), lambda i: (l, mod_row(i), 0, 0)),
            full(1, D), full(1, D), full(D, D_SHARED), full(D, D_SHARED), full(D_SHARED, D),
        ] + [y_spec(k) for k in range(TOP_K)],
        out_specs=out_specs,
        out_shape=out_shape,
        compiler_params=_params(("arbitrary",)),
        name="moe_combine",
    )(wn, x, mod, g_pre, g_post, ws_gate, ws_up, ws_down, *([ybuf] * TOP_K))


def _moe_layer(x, mod, l, g_pre, g_post, w_router, e_bias, w_gate, w_up, w_down,
               ws_gate, ws_up, ws_down, split_streams):
    h, top_e, wn, rk, cnt = _router(x, mod, l, g_pre, w_router.T, e_bias.reshape(N_EXPERTS, 1))
    cnt = cnt.reshape(N_EXPERTS).astype(I32)
    padded = (cnt + GM - 1) // GM * GM
    ends = jnp.cumsum(padded)
    offs = ends - padded
    eid = jnp.arange(N_EXPERTS, dtype=I32)[:, None, None]
    slot = rk + jnp.sum(jnp.where(top_e[None] == eid, offs[:, None, None], 0), axis=0)
    nvalid = ends[-1] // GM
    tile_start = jnp.arange(NT_MAX, dtype=I32) * GM
    tile_raw = jnp.sum((tile_start[:, None] >= ends[None, :]).astype(I32), axis=1)
    last = jnp.maximum(nvalid - 1, 0)
    tile_blk = jnp.minimum(jnp.arange(NT_MAX, dtype=I32), last)
    tile_e = jnp.minimum(tile_raw, N_EXPERTS - 1)
    tile_e = jnp.where(jnp.arange(NT_MAX) <= last, tile_e, tile_e[last])
    slot3 = slot.reshape(TOP_K, T // SC_W, SC_W).transpose(1, 0, 2)
    xg = _sc_dispatch(h.reshape(T, ROWS_PER_TOKEN, LANES), slot3)
    xg = _pad_fill(offs + cnt, padded - cnt, xg).reshape(SP * ROWS_PER_TOKEN, LANES)
    yg = _gmm(tile_e, tile_blk, nvalid.reshape(1), xg, l, w_gate, w_up, w_down)
    ybuf = _sc_gather(yg.reshape(SP, ROWS_PER_TOKEN, LANES), slot.reshape(TOP_K * T))
    return _combine(wn, x, mod, l, g_pre, g_post,
                    ws_gate.astype(BF16), ws_up.astype(BF16), ws_down.astype(BF16),
                    ybuf.reshape(TOP_K * T * ROWS_PER_TOKEN, LANES), split_streams)


def _rope_tables():
    n = DEC_SEQ
    rows = n // GRID_W
    row = jnp.repeat(jnp.arange(rows), GRID_W).astype(F32)
    col = jnp.tile(jnp.arange(GRID_W), rows).astype(F32)
    half = DQK_B // 2
    inv = ROPE_BASE ** (-jnp.arange(0, half, 2, dtype=F32) / half)
    ang_r = row[:, None] * inv
    ang_c = col[:, None] * inv
    ang = jnp.concatenate([ang_r, ang_r, ang_c, ang_c], axis=-1)
    reps = QK_B // DQK_B
    return jnp.tile(jnp.cos(ang), (1, reps)), jnp.tile(jnp.sin(ang), (1, reps))


def _pad_in_proj(w):
    s = [0, Q_A, 2 * Q_A, 2 * Q_A + V_A, 2 * Q_A + 2 * V_A]
    s += [s[-1] + GATE_RANK, s[-1] + 2 * GATE_RANK]
    s += [s[-1] + QK_B, s[-1] + 2 * QK_B, s[-1] + 2 * QK_B + V_B]
    gates = jnp.pad(w[:, s[4]:s[6]], ((0, 0), (0, GL_PAD - 2 * GATE_RANK)))
    return jnp.concatenate([w[:, s[0]:s[4]], gates, w[:, s[6]:s[9]]], axis=1).astype(BF16)


def kernel(x_prompt, x_sample, c, c_ctx, state_gla, cache_k, cache_v, ada_w, ada_b, norm_pre_mix, norm_post_mix, norm_pre_ffn, norm_post_ffn, ab_w_in, gla_w_g2, gla_b_g2, gla_norm_g, diff_lambda, diff_norm_g, ab_w_out, sgu_w_in, sgu_b_in, sgu_norm_g, sgu_w_s, sgu_b_s, sgu_w_out, moe_w_router, moe_e_bias, moe_w_gate, moe_w_up, moe_w_down, moe_ws_gate, moe_ws_up, moe_ws_down):
    depth = ada_w.shape[0]
    assert depth == 2, "layer 0 reads the two input streams, the last layer writes them back"
    xp, xs = x_prompt.reshape(TP, D), x_sample.reshape(TS, D)
    x = None
    cond = jnp.concatenate([c_ctx[None, :], c, jnp.zeros((8 - 1 - DEC_BATCH, D), F32)], axis=0)
    mod = _modulation(cond, ada_w, ada_b)
    cos, sin = _rope_tables()
    vec = lambda a: a.reshape(1, -1)
    new_s = new_k = new_v = None
    for l in range(depth):
        if l % 2 == 0:
            e = l // 2
            lam_init = 0.8 - 0.6 * math.exp(-0.3 * l)
            a, r_a, gl, q_b, k_b, v_b = _in_proj(xp, xs, mod, l, vec(norm_pre_mix[l]), _pad_in_proj(ab_w_in[e]), cos, sin)
            s0_t = jnp.swapaxes(state_gla[:, e], -1, -2)
            same_head = jnp.eye(H_A, dtype=bool)[None, None, :, None, :, None]
            s0_t = jnp.where(same_head, s0_t[:, :, :, :, None, :], 0.0).reshape(DEC_BATCH, 2, V_A, Q_A)
            o_f, o_bw, s_fin_t = _gla(a, gl, gla_w_g2[e], gla_b_g2[e].reshape(2, 1, Q_A), s0_t)
            o_att = _diff_attention(q_b, k_b, v_b, cache_k, cache_v, diff_lambda[e], lam_init)
            x = _mix_out(lam_init, o_f, o_bw, r_a, o_att, xp, xs, mod, l, vec(gla_norm_g[e]), vec(diff_norm_g[e]),
                         ab_w_out[e].astype(BF16), vec(norm_post_mix[l]))
            new_s = jnp.swapaxes(s_fin_t, -1, -2)
            new_k = k_b[:TP].reshape(BATCH, SEQ, H_B, 2, DQK_B).transpose(0, 2, 3, 1, 4)
            new_v = v_b[:TP].reshape(BATCH, SEQ, H_B, DV_B).transpose(0, 2, 1, 3)
        else:
            o = l // 2
            x = _sgu(x, mod, l, vec(norm_pre_mix[l]), sgu_w_in[o].astype(BF16), vec(sgu_b_in[o]),
                     vec(sgu_norm_g[o]), sgu_w_s[o], sgu_b_s[o].T, sgu_w_out[o].astype(BF16),
                     vec(norm_post_mix[l]))
        x = _moe_layer(x, mod, l, vec(norm_pre_ffn[l]), vec(norm_post_ffn[l]), moe_w_router[l], moe_e_bias[l],
                       moe_w_gate, moe_w_up, moe_w_down, moe_ws_gate[l], moe_ws_up[l], moe_ws_down[l],
                       split_streams=(l == depth - 1))
    y_prompt = x[0].reshape(BATCH, SEQ, D)
    y_sample = x[1].reshape(DEC_BATCH, DEC_SEQ, D)
    return (y_prompt, y_sample, new_s[:, None], new_k[:, None], new_v[:, None])
```

```python
import functools
import math

import jax
import jax.numpy as jnp
from jax import lax
from jax.experimental import pallas as pl
from jax.experimental.pallas import tpu as pltpu
from jax.experimental.pallas import tpu_sc as plsc

F32 = jnp.float32
BF16 = jnp.bfloat16
I32 = jnp.int32

D = 1024
BATCH, SEQ = 32, 256
DEC_BATCH, DEC_SEQ = 4, 2048
PAST_LEN = 256
GRID_W = 64
EPS = 1e-6
TP = BATCH * SEQ
TS = DEC_BATCH * DEC_SEQ
T = TP + TS
H_A, DK_A, DV_A = 4, 64, 128
Q_A, V_A = H_A * DK_A, H_A * DV_A
GATE_RANK, GATE_TAU, GLA_CHUNK = 16, 16.0, 64
H_B, DQK_B, DV_B = 4, 64, 128
QK_B, V_B = H_B * 2 * DQK_B, H_B * DV_B
ROPE_BASE = 10000.0
SGU_DIM, SGU_GROUPS, SGU_CHUNK = 1024, 4, 128
N_EXPERTS, TOP_K, N_GROUPS, TOPK_GROUPS = 64, 8, 8, 4
GROUP_SIZE = N_EXPERTS // N_GROUPS
D_EXPERT, D_SHARED = 256, 256
ROUTED_SCALE = 2.5

TM = 512
NPT = TP // TM
TILES_PER_DEC = DEC_SEQ // TM
SEG = 256
NSEG = T // SEG
NSEG_P = TP // SEG
SEG_PER_DEC = DEC_SEQ // SEG
TR = 512
TD = 256
GM = 512
GM_SUB = 256
NT_MAX = T * TOP_K // GM + N_EXPERTS
SP = NT_MAX * GM
GL_PAD = 128
VMEM_LIMIT = 56 * 1024 * 1024
NEG_INF = float("-inf")


def _bdot(a, b):
    return jnp.dot(a.astype(BF16), b.astype(BF16), preferred_element_type=F32)


def _bdot_nt(a, b):
    return lax.dot_general(a.astype(BF16), b.astype(BF16), (((1,), (1,)), ((), ())),
                           preferred_element_type=F32)


def _bdot_tn(a, b):
    return lax.dot_general(a.astype(BF16), b.astype(BF16), (((0,), (0,)), ((), ())),
                           preferred_element_type=F32)


def _split3(x):
    x1 = x.astype(BF16)
    r1 = x - x1.astype(F32)
    x2 = r1.astype(BF16)
    x3 = (r1 - x2.astype(F32)).astype(BF16)
    return x1, x2, x3


def _rms(x, g):
    return x * lax.rsqrt(jnp.mean(x * x, axis=-1, keepdims=True) + EPS) * g


def _silu(x):
    return x * jax.nn.sigmoid(x)


def _mod_row(i):
    return jnp.where(i < NPT, 0, 1 + (i - NPT) // TILES_PER_DEC)


def _params(sem, limit=VMEM_LIMIT):
    return pltpu.CompilerParams(dimension_semantics=sem, vmem_limit_bytes=limit)


def _mod_kernel(c_ref, w_ref, b_ref, o_ref):
    o_ref[0] = _bdot(_silu(c_ref[...]), w_ref[0]) + b_ref[0]


def _modulation(cond, ada_w, ada_b):
    depth = ada_w.shape[0]
    nj = 6
    out = pl.pallas_call(
        _mod_kernel,
        grid=(depth, nj),
        in_specs=[
            pl.BlockSpec((8, D), lambda l, j: (0, 0)),
            pl.BlockSpec((1, D, D), lambda l, j: (l, 0, j)),
            pl.BlockSpec((1, 1, D), lambda l, j: (l, 0, j)),
        ],
        out_specs=pl.BlockSpec((1, 8, D), lambda l, j: (l, 0, j)),
        out_shape=jax.ShapeDtypeStruct((depth, 8, 6 * D), F32),
        compiler_params=_params(("arbitrary", "arbitrary")),
        name="adaln_modulation",
    )(cond, ada_w, ada_b.reshape(depth, 1, 6 * D))
    return out.reshape(depth, 8, 6, D)


_C_A, _C_R, _C_GL, _C_Q, _C_K, _C_V, _C_END = 0, 1024, 1536, 1664, 2176, 2688, 3200


def _rope(x, cos, sin):
    lane = lax.broadcasted_iota(I32, x.shape, 1)
    first = (lane % 32) < 16
    n = x.shape[1]
    xr = jnp.where(first, -pltpu.roll(x, n - 16, 1), pltpu.roll(x, 16, 1))
    return x * cos + xr * sin


def _stream_specs():
    return [pl.BlockSpec((TM, D), lambda i: (jnp.minimum(i, NPT - 1), 0)),
            pl.BlockSpec((TM, D), lambda i: (jnp.maximum(i - NPT, 0), 0))]


def _stream_tile(xp_ref, xs_ref):
    return jnp.where(pl.program_id(0) < NPT, xp_ref[...], xs_ref[...])


def _in_kernel(xp_ref, xs_ref, mod_ref, g_ref, w_ref, cos_ref, sin_ref,
               a_ref, r_ref, gl_ref, q_ref, k_ref, v_ref):
    i = pl.program_id(0)
    m = mod_ref[0, 0]
    h = _rms(_stream_tile(xp_ref, xs_ref), g_ref[...]) * (1.0 + m[1:2]) + m[0:1]
    hb = h.astype(BF16)

    def proj(c0, c1):
        return jnp.dot(hb, w_ref[:, c0:c1], preferred_element_type=F32)

    a_ref[...] = proj(_C_A, _C_R)
    r_ref[...] = proj(_C_R, _C_GL)
    gl_ref[...] = proj(_C_GL, _C_Q)
    v_ref[...] = proj(_C_V, _C_END)
    q = proj(_C_Q, _C_K)
    k = proj(_C_K, _C_V)

    @pl.when(i < NPT)
    def _():
        q_ref[...] = q
        k_ref[...] = k

    @pl.when(i >= NPT)
    def _():
        cos = cos_ref[...]
        sin = sin_ref[...]
        q_ref[...] = _rope(q, cos, sin)
        k_ref[...] = _rope(k, cos, sin)


def _in_proj(xp, xs, mod, l, g, w_pad, cos, sin):
    tok = lambda width: pl.BlockSpec((TM, width), lambda i: (i, 0))
    rope_spec = pl.BlockSpec((TM, QK_B), lambda i: (jnp.maximum(i - NPT, 0) % TILES_PER_DEC, 0))
    widths = (1024, 512, GL_PAD, 512, 512, 512)
    return pl.pallas_call(
        _in_kernel,
        grid=(T // TM,),
        in_specs=_stream_specs() + [
            pl.BlockSpec((1, 1, 6, D), lambda i: (l, _mod_row(i), 0, 0)),
            pl.BlockSpec((1, D), lambda i: (0, 0)),
            pl.BlockSpec((D, _C_END), lambda i: (0, 0)),
            rope_spec, rope_spec,
        ],
        out_specs=[tok(w) for w in widths],
        out_shape=[jax.ShapeDtypeStruct((T, w), F32) for w in widths],
        compiler_params=_params(("arbitrary",)),
        name="mixer_ab_in_proj",
    )(xp, xs, mod, g, w_pad, cos, sin)


def _log_sigmoid(x):
    return jnp.minimum(x, 0.0) - jnp.log(1.0 + jnp.exp(-jnp.abs(x)))


def _gla_kernel(af_ref, ab_ref, glf_ref, glb_ref, wg_ref, bg_ref, s0_ref,
                of_ref, ob_ref, sfin_ref, st_ref):
    i = pl.program_id(0)

    @pl.when(i < NSEG_P)
    def _():
        st_ref[...] = jnp.zeros_like(st_ref)

    @pl.when(jnp.logical_and(i >= NSEG_P, (i - NSEG_P) % SEG_PER_DEC == 0))
    def _():
        st_ref[...] = s0_ref[0]

    r = lax.broadcasted_iota(I32, (SEG, SEG), 0)
    c = lax.broadcasted_iota(I32, (SEG, SEG), 1)
    same = (r // GLA_CHUNK) == (c // GLA_CHUNK)
    nchunk = SEG // GLA_CHUNK
    own_head = (lax.broadcasted_iota(I32, (V_A, Q_A), 0) // DV_A) == (lax.broadcasted_iota(I32, (V_A, Q_A), 1) // DK_A)

    for d, (a_ref, gl_ref, o_ref) in enumerate(((af_ref, glf_ref, of_ref), (ab_ref, glb_ref, ob_ref))):
        fwd = d == 0
        gcol = gl_ref[:, d * GATE_RANK:(d + 1) * GATE_RANK]
        la = _log_sigmoid(_bdot(gcol, wg_ref[d]) + bg_ref[d]) / GATE_TAU
        causal = jnp.logical_and(same, (c <= r) if fwd else (c >= r))
        tri = jnp.where(causal, 1.0, 0.0).astype(BF16)
        l1, l2, l3 = _split3(la)
        b_all = (jnp.dot(tri, l1, preferred_element_type=F32)
                 + jnp.dot(tri, l2, preferred_element_type=F32)
                 + jnp.dot(tri, l3, preferred_element_type=F32))
        q_in_all = a_ref[:, 0:Q_A] * (DK_A ** -0.5) * jnp.exp(b_all)
        kd_all = a_ref[:, Q_A:2 * Q_A] * jnp.exp(-b_all)
        intra = []
        for h in range(H_A):
            kc = slice(h * DK_A, (h + 1) * DK_A)
            attn = jnp.where(causal, _bdot_nt(q_in_all[:, kc], kd_all[:, kc]), 0.0)
            intra.append(_bdot(attn, a_ref[:, 2 * Q_A + h * DV_A:2 * Q_A + (h + 1) * DV_A]))
        intra = jnp.concatenate(intra, axis=1)
        state = st_ref[d]
        order = range(nchunk) if fwd else range(nchunk - 1, -1, -1)
        for ch in order:
            r0 = ch * GLA_CHUNK
            rows = slice(r0, r0 + GLA_CHUNK)
            end = r0 + GLA_CHUNK - 1 if fwd else r0
            b_end = b_all[end:end + 1, :]
            kw = a_ref[rows, Q_A:2 * Q_A] * jnp.exp(b_end - b_all[rows, :])
            o_ref[rows, :] = intra[rows, :] + _bdot_nt(q_in_all[rows, :], state)
            kv_t = _bdot_tn(a_ref[rows, 2 * Q_A:2 * Q_A + V_A], kw)
            state = state * jnp.exp(b_end) + jnp.where(own_head, kv_t, 0.0)
        st_ref[d] = state

    @pl.when(i < NSEG_P)
    def _():
        for d in range(2):
            for h in range(H_A):
                sfin_ref[0, d, h] = st_ref[d, h * DV_A:(h + 1) * DV_A, h * DK_A:(h + 1) * DK_A]


def _seg_bwd(i):
    j = i - NSEG_P
    return jnp.where(i < NSEG_P, i, NSEG_P + (j // SEG_PER_DEC) * SEG_PER_DEC + (SEG_PER_DEC - 1 - j % SEG_PER_DEC))


def _gla(a, gl, wg, bg, s0_t):
    seg = lambda width, f: pl.BlockSpec((SEG, width), lambda i: (f(i), 0))
    ident = lambda i: i
    st_block = (1, 2, H_A, DV_A, DK_A)
    return pl.pallas_call(
        _gla_kernel,
        grid=(NSEG,),
        in_specs=[
            seg(D, ident), seg(D, _seg_bwd), seg(GL_PAD, ident), seg(GL_PAD, _seg_bwd),
            pl.BlockSpec((2, GATE_RANK, Q_A), lambda i: (0, 0, 0)),
            pl.BlockSpec((2, 1, Q_A), lambda i: (0, 0, 0)),
            pl.BlockSpec((1, 2, V_A, Q_A), lambda i: (jnp.maximum(i - NSEG_P, 0) // SEG_PER_DEC, 0, 0, 0)),
        ],
        out_specs=[
            seg(V_A, ident), seg(V_A, _seg_bwd),
            pl.BlockSpec(st_block, lambda i: (jnp.minimum(i, NSEG_P - 1), 0, 0, 0, 0)),
        ],
        out_shape=[
            jax.ShapeDtypeStruct((T, V_A), F32),
            jax.ShapeDtypeStruct((T, V_A), F32),
            jax.ShapeDtypeStruct((BATCH, 2, H_A, DV_A, DK_A), F32),
        ],
        scratch_shapes=[pltpu.VMEM((2, V_A, Q_A), F32)],
        compiler_params=_params(("arbitrary",)),
        name="gla_bidir",
    )(a, a, gl, gl, wg, bg, s0_t)


def _diff_lambda(lam_ref, lam_init):
    lp = lam_ref[...]
    s01 = jnp.sum(lp[0:1] * lp[1:2], axis=1, keepdims=True)
    s23 = jnp.sum(lp[2:3] * lp[3:4], axis=1, keepdims=True)
    return jnp.exp(s01) - jnp.exp(s23) + lam_init


def _attn_prompt_kernel(lam_init, q_ref, k_ref, v_ref, lam_ref, o_ref):
    lam = _diff_lambda(lam_ref, lam_init)
    for h in range(H_B):
        ps = []
        for m in range(2):
            cols = slice((2 * h + m) * DQK_B, (2 * h + m + 1) * DQK_B)
            s = _bdot_nt(q_ref[:, cols] * (DQK_B ** -0.5), k_ref[:, cols])
            e = jnp.exp(s - jnp.max(s, axis=1, keepdims=True))
            ps.append(e * (1.0 / jnp.sum(e, axis=1, keepdims=True)))
        w = ps[0] - lam * ps[1]
        o_ref[:, h * DV_B:(h + 1) * DV_B] = _bdot(w, v_ref[:, h * DV_B:(h + 1) * DV_B])


def _attn_sample_kernel(lam_init, q_ref, k_ref, v_ref, ck_ref, cv_ref, lam_ref, o_ref):
    lam = _diff_lambda(lam_ref, lam_init)
    for h in range(H_B):
        parts = []
        for m in range(2):
            cols = slice((2 * h + m) * DQK_B, (2 * h + m + 1) * DQK_B)
            q = q_ref[:, cols] * (DQK_B ** -0.5)
            sc = _bdot_nt(q, ck_ref[0, 0, h, m])
            sn = _bdot_nt(q, k_ref[:, cols])
            mx = jnp.maximum(jnp.max(sc, axis=1, keepdims=True), jnp.max(sn, axis=1, keepdims=True))
            ec = jnp.exp(sc - mx)
            en = jnp.exp(sn - mx)
            inv = (1.0 if m == 0 else -lam) / (jnp.sum(ec, axis=1, keepdims=True) + jnp.sum(en, axis=1, keepdims=True))
            parts.append((ec * inv, en * inv))
        wc = parts[0][0] + parts[1][0]
        wn = parts[0][1] + parts[1][1]
        o_ref[:, h * DV_B:(h + 1) * DV_B] = (_bdot(wc, cv_ref[0, 0, h])
                                             + _bdot(wn, v_ref[:, h * DV_B:(h + 1) * DV_B]))


QB = SEQ
NQB_DEC = DEC_SEQ // QB


def _attn_kernel(lam_init, q_ref, kp_ref, vp_ref, ks_ref, vs_ref, ck_ref, cv_ref, lam_ref, o_ref):
    i = pl.program_id(0)

    @pl.when(i < BATCH)
    def _():
        _attn_prompt_kernel(lam_init, q_ref, kp_ref, vp_ref, lam_ref, o_ref)

    @pl.when(i >= BATCH)
    def _():
        _attn_sample_kernel(lam_init, q_ref, ks_ref, vs_ref, ck_ref, cv_ref, lam_ref, o_ref)


def _diff_attention(q, k, v, cache_k, cache_v, lam_p, lam_init):
    blk = lambda rows, f: pl.BlockSpec((rows, 512), f)
    dec_b = lambda i: jnp.maximum(i - BATCH, 0) // NQB_DEC
    own = lambda i: (i, 0)
    prompt_kv = lambda i: (jnp.minimum(i, BATCH - 1), 0)
    dec_kv = lambda i: (TP // DEC_SEQ + dec_b(i), 0)
    return pl.pallas_call(
        functools.partial(_attn_kernel, lam_init),
        grid=(BATCH + DEC_BATCH * NQB_DEC,),
        in_specs=[
            blk(QB, own), blk(SEQ, prompt_kv), blk(SEQ, prompt_kv), blk(DEC_SEQ, dec_kv), blk(DEC_SEQ, dec_kv),
            pl.BlockSpec((1, 1, H_B, 2, PAST_LEN, DQK_B), lambda i: (dec_b(i), 0, 0, 0, 0, 0)),
            pl.BlockSpec((1, 1, H_B, PAST_LEN, DV_B), lambda i: (dec_b(i), 0, 0, 0, 0)),
            pl.BlockSpec((4, DQK_B), lambda i: (0, 0)),
        ],
        out_specs=blk(QB, own),
        out_shape=jax.ShapeDtypeStruct((T, V_B), F32),
        compiler_params=_params(("arbitrary",)),
        name="diff_attention",
    )(q, k, v, k, v, cache_k, cache_v, lam_p)


def _head_rms(x, g, nheads, width):
    return jnp.concatenate([_rms(x[:, h * width:(h + 1) * width], g) for h in range(nheads)], axis=1)


def _mix_out_kernel(lam_init, of_ref, ob_ref, r_ref, oatt_ref, xp_ref, xs_ref, mod_ref,
                    gg_ref, dg_ref, wo_ref, gp_ref, o_ref):
    m = mod_ref[0, 0]
    o_a = _head_rms(of_ref[...] + ob_ref[...], gg_ref[...], H_A, DV_A) * _silu(r_ref[...])
    o_b = _head_rms(oatt_ref[...], dg_ref[...], H_B, DV_B) * (1.0 - lam_init)
    out = _bdot(o_a, wo_ref[0:V_A, :]) + _bdot(o_b, wo_ref[V_A:V_A + V_B, :])
    o_ref[...] = _stream_tile(xp_ref, xs_ref) + m[2:3] * _rms(out, gp_ref[...])


def _mix_out(lam_init, o_f, o_b, r_a, o_att, xp, xs, mod, l, gla_g, diff_g, w_o, g_post):
    tok = lambda width: pl.BlockSpec((TM, width), lambda i: (i, 0))
    vec = lambda width: pl.BlockSpec((1, width), lambda i: (0, 0))
    return pl.pallas_call(
        functools.partial(_mix_out_kernel, lam_init),
        grid=(T // TM,),
        in_specs=[
            tok(512), tok(512), tok(512), tok(512), *_stream_specs(),
            pl.BlockSpec((1, 1, 6, D), lambda i: (l, _mod_row(i), 0, 0)),
            vec(DV_A), vec(DV_B),
            pl.BlockSpec((V_A + V_B, D), lambda i: (0, 0)),
            vec(D),
        ],
        out_specs=tok(D),
        out_shape=jax.ShapeDtypeStruct((T, D), F32),
        compiler_params=_params(("arbitrary",)),
        name="mixer_ab_out",
    )(o_f, o_b, r_a, o_att, xp, xs, mod, gla_g, diff_g, w_o, g_post)


def _gelu_tanh(x):
    return 0.5 * x * (1.0 + jnp.tanh(math.sqrt(2.0 / math.pi) * (x + 0.044715 * (x * x * x))))


def _sgu_kernel(xp_ref, xs_ref, mod_ref, gpre_ref, win_ref, bin_ref, vg_ref, ws_ref, bs_ref,
                wout_ref, gpost_ref, o_ref, t_ref):
    m = mod_ref[0, 0]
    x = _stream_tile(xp_ref, xs_ref)
    h = _rms(x, gpre_ref[...]) * (1.0 + m[1:2]) + m[0:1]
    z = _gelu_tanh(_bdot(h, win_ref[...]) + bin_ref[...])
    v = _rms(z[:, SGU_DIM:], vg_ref[...])
    gw = SGU_DIM // SGU_GROUPS
    for ch in range(TM // SGU_CHUNK):
        rows = slice(ch * SGU_CHUNK, (ch + 1) * SGU_CHUNK)
        for g in range(SGU_GROUPS):
            cols = slice(g * gw, (g + 1) * gw)
            vs = _bdot(ws_ref[g], v[rows, cols]) + bs_ref[:, g:g + 1]
            t_ref[rows, cols] = (z[rows, cols] * vs).astype(BF16)
    out = jnp.dot(t_ref[...], wout_ref[...], preferred_element_type=F32)
    o_ref[...] = x + m[2:3] * _rms(out, gpost_ref[...])


def _sgu(xp, xs, mod, l, g_pre, w_in, b_in, v_g, w_s, b_s_t, w_out, g_post):
    tok = pl.BlockSpec((TM, D), lambda i: (i, 0))
    full = lambda *shape: pl.BlockSpec(shape, lambda i: (0,) * len(shape))
    return pl.pallas_call(
        _sgu_kernel,
        grid=(T // TM,),
        in_specs=_stream_specs() + [
            pl.BlockSpec((1, 1, 6, D), lambda i: (l, _mod_row(i), 0, 0)),
            full(1, D), full(D, 2 * SGU_DIM), full(1, 2 * SGU_DIM), full(1, SGU_DIM),
            full(SGU_GROUPS, SGU_CHUNK, SGU_CHUNK), full(SGU_CHUNK, SGU_GROUPS),
            full(SGU_DIM, D), full(1, D),
        ],
        out_specs=tok,
        out_shape=jax.ShapeDtypeStruct((T, D), F32),
        scratch_shapes=[pltpu.VMEM((TM, SGU_DIM), BF16)],
        compiler_params=_params(("arbitrary",)),
        name="sgu_mixer",
    )(xp, xs, mod, g_pre, w_in, b_in, v_g, w_s, b_s_t, w_out, g_post)


LANES = 128
U32 = jnp.uint32
PACKED = D // 2
ROWS_PER_TOKEN = PACKED // LANES
HIGH_HALF = 0xFFFF0000


def _pack_rows(x):
    bits = lax.bitcast_convert_type(x.astype(BF16).astype(F32), U32)
    return bits[:, :PACKED] | (bits[:, PACKED:] >> 16)


def _unpack_rows(u):
    return (lax.bitcast_convert_type(u & U32(HIGH_HALF), F32), lax.bitcast_convert_type(u << 16, F32))


def _store_token_tiles(ref, u, first=0):
    n = u.shape[0]
    for c in range(ROWS_PER_TOKEN):
        ref[pl.ds(first * ROWS_PER_TOKEN + c, n, stride=ROWS_PER_TOKEN), :] = u[:, c * LANES:(c + 1) * LANES]


def _load_token_tiles(ref, n, first=0):
    return jnp.concatenate([ref[pl.ds(first * ROWS_PER_TOKEN + c, n, stride=ROWS_PER_TOKEN), :]
                            for c in range(ROWS_PER_TOKEN)], axis=1)


def _router_kernel(x_ref, mod_ref, g_ref, wr_ref, eb_ref,
                   h_ref, te_ref, wn_ref, rk_ref, cnt_ref, carry_ref):
    i = pl.program_id(0)

    @pl.when(i == 0)
    def _():
        carry_ref[...] = jnp.zeros_like(carry_ref)

    m = mod_ref[0, 0]
    h = _rms(x_ref[...], g_ref[...]) * (1.0 + m[4:5]) + m[3:4]
    _store_token_tiles(h_ref, _pack_rows(h))
    h1, h2, _ = _split3(h)
    w1, w2, _ = _split3(wr_ref[...])
    nt = lambda a, b: lax.dot_general(a, b, (((1,), (1,)), ((), ())), preferred_element_type=F32)
    logits = nt(w1, h1) + nt(w1, h2) + nt(w2, h1)
    scores = jax.nn.sigmoid(logits)
    sel = scores + eb_ref[...]

    row8 = lax.broadcasted_iota(I32, (GROUP_SIZE, TR), 0)
    gscore = []
    for g in range(N_GROUPS):
        xg = sel[g * GROUP_SIZE:(g + 1) * GROUP_SIZE]
        m1 = jnp.max(xg, axis=0, keepdims=True)
        i1 = jnp.min(jnp.where(xg == m1, row8, GROUP_SIZE), axis=0, keepdims=True)
        m2 = jnp.max(jnp.where(row8 == i1, NEG_INF, xg), axis=0, keepdims=True)
        gscore.append(m1 + m2)
    pieces = []
    for g in range(N_GROUPS):
        rank = jnp.zeros((1, TR), I32)
        for g2 in range(N_GROUPS):
            if g2 == g:
                continue
            beats = (gscore[g2] >= gscore[g]) if g2 < g else (gscore[g2] > gscore[g])
            rank = rank + beats.astype(I32)
        pieces.append(jnp.where(rank < TOPK_GROUPS, sel[g * GROUP_SIZE:(g + 1) * GROUP_SIZE], NEG_INF))
    cur = jnp.concatenate(pieces, axis=0)

    row = lax.broadcasted_iota(I32, (N_EXPERTS, TR), 0)
    idxs, ws = [], []
    for _ in range(TOP_K):
        mx = jnp.max(cur, axis=0, keepdims=True)
        idx = jnp.min(jnp.where(cur == mx, row, N_EXPERTS), axis=0, keepdims=True)
        hit = row == idx
        ws.append(jnp.sum(jnp.where(hit, scores, 0.0), axis=0, keepdims=True))
        cur = jnp.where(hit, NEG_INF, cur)
        idxs.append(idx)
    mask = jnp.zeros((N_EXPERTS, TR), F32)
    for idx in idxs:
        mask = mask + (row == idx).astype(F32)
    wsum = ws[0]
    for wk in ws[1:]:
        wsum = wsum + wk

    tj = lax.broadcasted_iota(I32, (TR, TR), 0)
    ti = lax.broadcasted_iota(I32, (TR, TR), 1)
    upper = jnp.where(tj < ti, 1.0, 0.0).astype(BF16)
    pos = carry_ref[...] + jnp.dot(mask.astype(BF16), upper, preferred_element_type=F32)
    for k in range(TOP_K):
        hit = row == idxs[k]
        te_ref[k:k + 1, :] = idxs[k]
        wn_ref[k:k + 1, :] = ws[k] / wsum * ROUTED_SCALE
        rk_ref[k:k + 1, :] = jnp.sum(jnp.where(hit, pos, 0.0), axis=0, keepdims=True).astype(I32)
    carry_ref[...] = carry_ref[...] + jnp.sum(mask, axis=1, keepdims=True)
    cnt_ref[...] = carry_ref[...]


def _router(x, mod, l, g, wr_t, e_bias):
    kt = lambda dtype: jax.ShapeDtypeStruct((TOP_K, T), dtype)
    kt_spec = pl.BlockSpec((TOP_K, TR), lambda i: (0, i))
    tiles_per_dec = DEC_SEQ // TR
    mod_row = lambda i: jnp.where(i < TP // TR, 0, 1 + (i - TP // TR) // tiles_per_dec)
    return pl.pallas_call(
        _router_kernel,
        grid=(T // TR,),
        in_specs=[
            pl.BlockSpec((TR, D), lambda i: (i, 0)),
            pl.BlockSpec((1, 1, 6, D), lambda i: (l, mod_row(i), 0, 0)),
            pl.BlockSpec((1, D), lambda i: (0, 0)),
            pl.BlockSpec((N_EXPERTS, D), lambda i: (0, 0)),
            pl.BlockSpec((N_EXPERTS, 1), lambda i: (0, 0)),
        ],
        out_specs=[
            pl.BlockSpec((TR * ROWS_PER_TOKEN, LANES), lambda i: (i, 0)),
            kt_spec, kt_spec, kt_spec,
            pl.BlockSpec((N_EXPERTS, 1), lambda i: (0, 0)),
        ],
        out_shape=[
            jax.ShapeDtypeStruct((T * ROWS_PER_TOKEN, LANES), U32), kt(I32), kt(F32), kt(I32),
            jax.ShapeDtypeStruct((N_EXPERTS, 1), F32),
        ],
        scratch_shapes=[pltpu.VMEM((N_EXPERTS, 1), F32)],
        compiler_params=_params(("arbitrary",)),
        name="moe_router",
    )(x, mod, g, wr_t, e_bias)


_PAD_BITS = tuple(1 << b for b in range(GM.bit_length() - 1))


def _pad_fill_kernel(pad_start_ref, pad_len_ref, xg_in_ref, xg_ref, zero_ref, sem):
    del xg_in_ref
    zero_ref[...] = jnp.zeros_like(zero_ref)

    def pad_copies(e):
        start = pad_start_ref[e]
        n = pad_len_ref[e]
        copies = []
        for bit in _PAD_BITS:
            first = start + (n & ~(2 * bit - 1))
            copies.append(((n & bit) != 0, pltpu.make_async_copy(
                zero_ref.at[pl.ds(0, bit)], xg_ref.at[pl.ds(first, bit)], sem)))
        return copies

    def start_e(e, carry):
        for on, cp in pad_copies(e):
            @pl.when(on)
            def _():
                cp.start()
        return carry

    def wait_e(e, carry):
        for on, cp in pad_copies(e):
            @pl.when(on)
            def _():
                cp.wait()
        return carry

    lax.fori_loop(0, N_EXPERTS, start_e, 0)
    lax.fori_loop(0, N_EXPERTS, wait_e, 0)


def _pad_fill(pad_start, pad_len, xg):
    grid_spec = pltpu.PrefetchScalarGridSpec(
        num_scalar_prefetch=2,
        grid=(1,),
        in_specs=[pl.BlockSpec(memory_space=pl.ANY)],
        out_specs=pl.BlockSpec(memory_space=pl.ANY),
        scratch_shapes=[pltpu.VMEM((GM // 2, ROWS_PER_TOKEN, LANES), xg.dtype), pltpu.SemaphoreType.DMA],
    )
    return pl.pallas_call(
        _pad_fill_kernel,
        grid_spec=grid_spec,
        out_shape=jax.ShapeDtypeStruct(xg.shape, xg.dtype),
        input_output_aliases={2: 0},
        compiler_params=_params(("arbitrary",)),
        name="moe_pad_fill",
    )(pad_start, pad_len, xg)


SC_CORES, SC_SUBCORES = 2, 16
SC_WORKERS = SC_CORES * SC_SUBCORES
SC_W = 64


def _sc_worker_id():
    return lax.axis_index("s") * SC_CORES + lax.axis_index("c")


def _sc_dispatch(h3, slot3):
    nchunk = T // SC_WORKERS // SC_W
    mesh = plsc.VectorSubcoreMesh(core_axis_name="c", subcore_axis_name="s")
    tile = (SC_W, ROWS_PER_TOKEN, LANES)

    @functools.partial(
        pl.kernel, mesh=mesh,
        out_type=jax.ShapeDtypeStruct((SP, ROWS_PER_TOKEN, LANES), h3.dtype),
        scratch_types=[pltpu.VMEM((TOP_K, SC_W), I32), pltpu.VMEM((TOP_K, SC_W), I32),
                       pltpu.VMEM(tile, h3.dtype), pltpu.VMEM(tile, h3.dtype),
                       pltpu.SemaphoreType.DMA((2,)), pltpu.SemaphoreType.DMA((2,))],
    )
    def k(h_hbm, slot_hbm, xg_hbm, idx0, idx1, rows0, rows1, lsem, ssem):
        first = _sc_worker_id() * nchunk
        idx = (idx0, idx1)
        rows = (rows0, rows1)

        def loads(j, b):
            blk = first + j
            tok = pl.multiple_of(blk * SC_W, SC_W)
            return (pltpu.make_async_copy(slot_hbm.at[blk], idx[b], lsem.at[b]),
                    pltpu.make_async_copy(h_hbm.at[pl.ds(tok, SC_W)], rows[b], lsem.at[b]))

        def scatters(b):
            return [pltpu.make_async_copy(rows[b], xg_hbm.at[idx[b].at[kk]], ssem.at[b]) for kk in range(TOP_K)]

        for cp in loads(0, 0):
            cp.start()

        @pl.loop(0, nchunk, step=2)
        def _(j):
            for b in (0, 1):
                jj = j + b
                for cp in loads(jj, b):
                    cp.wait()
                for cp in scatters(b):
                    cp.start()

                @pl.when(jj + 1 < nchunk)
                def _():
                    @pl.when(jj >= 1)
                    def _():
                        for cp in scatters(1 - b):
                            cp.wait()
                    for cp in loads(jj + 1, 1 - b):
                        cp.start()

        for b in (0, 1):
            for cp in scatters(b):
                cp.wait()

    return k(h3, slot3)


def _gmm_kernel(tile_e_ref, tile_blk_ref, nvalid_ref, x_ref, wg_ref, wu_ref, wd_ref,
                y_ref, wgu_scr, wd_scr):
    j = pl.program_id(0)

    @pl.when(j < nvalid_ref[0])
    def _():
        prev = tile_e_ref[jnp.maximum(j - 1, 0)]

        @pl.when(jnp.logical_or(j == 0, tile_e_ref[j] != prev))
        def _():
            wgu_scr[:, 0:D_EXPERT] = wg_ref[0, 0].astype(BF16)
            wgu_scr[:, D_EXPERT:2 * D_EXPERT] = wu_ref[0, 0].astype(BF16)
            wd_scr[...] = wd_ref[0, 0].astype(BF16)

        for s in range(GM // GM_SUB):
            x_hi, x_lo = _unpack_rows(_load_token_tiles(x_ref, GM_SUB, s * GM_SUB))
            gu = (jnp.dot(x_hi.astype(BF16), wgu_scr[0:PACKED, :], preferred_element_type=F32)
                  + jnp.dot(x_lo.astype(BF16), wgu_scr[PACKED:D, :], preferred_element_type=F32))
            hid = _silu(gu[:, 0:D_EXPERT]) * gu[:, D_EXPERT:2 * D_EXPERT]
            y = jnp.dot(hid.astype(BF16), wd_scr[...], preferred_element_type=F32)
            _store_token_tiles(y_ref, _pack_rows(y), s * GM_SUB)


def _gmm(tile_e, tile_blk, nvalid, xg, l, w_gate, w_up, w_down):
    row_tile = pl.BlockSpec((GM * ROWS_PER_TOKEN, LANES), lambda j, te, tb, nv: (tb[j], 0))
    grid_spec = pltpu.PrefetchScalarGridSpec(
        num_scalar_prefetch=3,
        grid=(NT_MAX,),
        in_specs=[
            row_tile,
            pl.BlockSpec((1, 1, D, D_EXPERT), lambda j, te, tb, nv: (l, te[j], 0, 0)),
            pl.BlockSpec((1, 1, D, D_EXPERT), lambda j, te, tb, nv: (l, te[j], 0, 0)),
            pl.BlockSpec((1, 1, D_EXPERT, D), lambda j, te, tb, nv: (l, te[j], 0, 0)),
        ],
        out_specs=row_tile,
        scratch_shapes=[pltpu.VMEM((D, 2 * D_EXPERT), BF16), pltpu.VMEM((D_EXPERT, D), BF16)],
    )
    return pl.pallas_call(
        _gmm_kernel,
        grid_spec=grid_spec,
        out_shape=jax.ShapeDtypeStruct((SP * ROWS_PER_TOKEN, LANES), U32),
        compiler_params=_params(("arbitrary",)),
        name="moe_grouped_matmul",
    )(tile_e, tile_blk, nvalid, xg, w_gate, w_up, w_down)


def _sc_gather(table3, idx):
    n_idx = idx.shape[0]
    per_w = n_idx // SC_WORKERS
    nchunk = per_w // SC_W
    mesh = plsc.VectorSubcoreMesh(core_axis_name="c", subcore_axis_name="s")
    tile = (SC_W, ROWS_PER_TOKEN, LANES)

    @functools.partial(
        pl.kernel, mesh=mesh,
        out_type=jax.ShapeDtypeStruct((n_idx, ROWS_PER_TOKEN, LANES), table3.dtype),
        scratch_types=[pltpu.VMEM((per_w,), I32), pltpu.VMEM(tile, table3.dtype), pltpu.VMEM(tile, table3.dtype),
                       pltpu.SemaphoreType.DMA((2,)), pltpu.SemaphoreType.DMA((2,))],
    )
    def k(table_hbm, idx_hbm, out_hbm, idx_v, rows0, rows1, gsem, wsem):
        base = pl.multiple_of(_sc_worker_id() * per_w, per_w)
        rows = (rows0, rows1)
        pltpu.sync_copy(idx_hbm.at[pl.ds(base, per_w)], idx_v)

        def gather(j, b):
            ids = idx_v.at[pl.ds(pl.multiple_of(j * SC_W, SC_W), SC_W)]
            return pltpu.make_async_copy(table_hbm.at[ids], rows[b], gsem.at[b])

        def write(j, b):
            dst = out_hbm.at[pl.ds(pl.multiple_of(base + j * SC_W, SC_W), SC_W)]
            return pltpu.make_async_copy(rows[b], dst, wsem.at[b])

        gather(0, 0).start()

        @pl.loop(0, nchunk, step=2)
        def _(j):
            for b in (0, 1):
                jj = j + b
                gather(jj, b).wait()
                write(jj, b).start()

                @pl.when(jj + 1 < nchunk)
                def _():
                    @pl.when(jj >= 1)
                    def _():
                        write(jj - 1, 1 - b).wait()
                    gather(jj + 1, 1 - b).start()

        write(nchunk - 2, 0).wait()
        write(nchunk - 1, 1).wait()

    return k(table3, idx)


def _combine_kernel(wn_ref, x_ref, mod_ref, gpre_ref, gp_ref, wsg_ref, wsu_ref, wsd_ref, *rest):
    y_refs, o_ref = rest[:TOP_K], rest[TOP_K]
    m = mod_ref[0, 0]
    x = x_ref[...]
    hb = (_rms(x, gpre_ref[...]) * (1.0 + m[4:5]) + m[3:4]).astype(BF16)
    hid = (_silu(jnp.dot(hb, wsg_ref[...], preferred_element_type=F32))
           * jnp.dot(hb, wsu_ref[...], preferred_element_type=F32))
    acc = jnp.dot(hid.astype(BF16), wsd_ref[...], preferred_element_type=F32)

    r = lax.broadcasted_iota(I32, (TD, TD), 0)
    c = lax.broadcasted_iota(I32, (TD, TD), 1)
    eye = jnp.where(r == c, 1.0, 0.0).astype(BF16)
    nt = lambda a, b: lax.dot_general(a, b, (((1,), (1,)), ((), ())), preferred_element_type=F32)
    w1, w2, w3 = _split3(wn_ref[...])
    w_t = nt(eye, w1) + nt(eye, w2) + nt(eye, w3)

    acc_hi = acc[:, :PACKED]
    acc_lo = acc[:, PACKED:]
    for k in range(TOP_K):
        y_hi, y_lo = _unpack_rows(_load_token_tiles(y_refs[k], TD))
        acc_hi = acc_hi + y_hi * w_t[:, k:k + 1]
        acc_lo = acc_lo + y_lo * w_t[:, k:k + 1]
    acc = jnp.concatenate([acc_hi, acc_lo], axis=1)
    o_ref[...] = x + m[5:6] * _rms(acc, gp_ref[...])


def _combine(wn, x, mod, l, g_pre, g_post, ws_gate, ws_up, ws_down, ybuf, first_tok, n_tok):
    off = first_tok // TD
    tiles_per_dec = DEC_SEQ // TD
    npd = TP // TD
    mod_row = lambda i: jnp.where(i + off < npd, 0, 1 + (i + off - npd) // tiles_per_dec)
    full = lambda *shape: pl.BlockSpec(shape, lambda i: (0,) * len(shape))
    y_spec = lambda k: pl.BlockSpec((TD * ROWS_PER_TOKEN, LANES), lambda i: (k * (n_tok // TD) + i, 0))
    return pl.pallas_call(
        _combine_kernel,
        grid=(n_tok // TD,),
        in_specs=[
            pl.BlockSpec((TOP_K, TD), lambda i: (0, i + off)),
            pl.BlockSpec((TD, D), lambda i: (i + off, 0)),
            pl.BlockSpec((1, 1, 6, D), lambda i: (l, mod_row(i), 0, 0)),
            full(1, D), full(1, D), full(D, D_SHARED), full(D, D_SHARED), full(D_SHARED, D),
        ] + [y_spec(k) for k in range(TOP_K)],
        out_specs=pl.BlockSpec((TD, D), lambda i: (i, 0)),
        out_shape=jax.ShapeDtypeStruct((n_tok, D), F32),
        compiler_params=_params(("arbitrary",)),
        name="moe_combine",
    )(wn, x, mod, g_pre, g_post, ws_gate, ws_up, ws_down, *([ybuf] * TOP_K))


def _moe_layer(x, mod, l, g_pre, g_post, w_router, e_bias, w_gate, w_up, w_down,
               ws_gate, ws_up, ws_down):
    h, top_e, wn, rk, cnt = _router(x, mod, l, g_pre, w_router.T, e_bias.reshape(N_EXPERTS, 1))
    cnt = cnt.reshape(N_EXPERTS).astype(I32)
    padded = (cnt + GM - 1) // GM * GM
    ends = jnp.cumsum(padded)
    offs = ends - padded
    eid = jnp.arange(N_EXPERTS, dtype=I32)[:, None, None]
    slot = rk + jnp.sum(jnp.where(top_e[None] == eid, offs[:, None, None], 0), axis=0)
    nvalid = ends[-1] // GM
    tile_start = jnp.arange(NT_MAX, dtype=I32) * GM
    tile_raw = jnp.sum((tile_start[:, None] >= ends[None, :]).astype(I32), axis=1)
    last = jnp.maximum(nvalid - 1, 0)
    tile_blk = jnp.minimum(jnp.arange(NT_MAX, dtype=I32), last)
    tile_e = jnp.minimum(tile_raw, N_EXPERTS - 1)
    tile_e = jnp.where(jnp.arange(NT_MAX) <= last, tile_e, tile_e[last])
    slot3 = slot.reshape(TOP_K, T // SC_W, SC_W).transpose(1, 0, 2)
    xg = _sc_dispatch(h.reshape(T, ROWS_PER_TOKEN, LANES), slot3)
    xg = _pad_fill(offs + cnt, padded - cnt, xg).reshape(SP * ROWS_PER_TOKEN, LANES)
    yg = _gmm(tile_e, tile_blk, nvalid.reshape(1), xg, l, w_gate, w_up, w_down)
    yg3 = yg.reshape(SP, ROWS_PER_TOKEN, LANES)
    ws = (ws_gate.astype(BF16), ws_up.astype(BF16), ws_down.astype(BF16))
    outs = []
    for first_tok, n_tok in ((0, TP), (TP, TS)):
        ybuf = _sc_gather(yg3, slot[:, first_tok:first_tok + n_tok].reshape(TOP_K * n_tok))
        outs.append(_combine(wn, x, mod, l, g_pre, g_post, *ws,
                             ybuf.reshape(TOP_K * n_tok * ROWS_PER_TOKEN, LANES), first_tok, n_tok))
    return outs


def _rope_tables():
    n = DEC_SEQ
    rows = n // GRID_W
    row = jnp.repeat(jnp.arange(rows), GRID_W).astype(F32)
    col = jnp.tile(jnp.arange(GRID_W), rows).astype(F32)
    half = DQK_B // 2
    inv = ROPE_BASE ** (-jnp.arange(0, half, 2, dtype=F32) / half)
    ang_r = row[:, None] * inv
    ang_c = col[:, None] * inv
    ang = jnp.concatenate([ang_r, ang_r, ang_c, ang_c], axis=-1)
    reps = QK_B // DQK_B
    return jnp.tile(jnp.cos(ang), (1, reps)), jnp.tile(jnp.sin(ang), (1, reps))


def _pad_in_proj(w):
    s = [0, Q_A, 2 * Q_A, 2 * Q_A + V_A, 2 * Q_A + 2 * V_A]
    s += [s[-1] + GATE_RANK, s[-1] + 2 * GATE_RANK]
    s += [s[-1] + QK_B, s[-1] + 2 * QK_B, s[-1] + 2 * QK_B + V_B]
    gates = jnp.pad(w[:, s[4]:s[6]], ((0, 0), (0, GL_PAD - 2 * GATE_RANK)))
    return jnp.concatenate([w[:, s[0]:s[4]], gates, w[:, s[6]:s[9]]], axis=1).astype(BF16)


def kernel(x_prompt, x_sample, c, c_ctx, state_gla, cache_k, cache_v, ada_w, ada_b, norm_pre_mix, norm_post_mix, norm_pre_ffn, norm_post_ffn, ab_w_in, gla_w_g2, gla_b_g2, gla_norm_g, diff_lambda, diff_norm_g, ab_w_out, sgu_w_in, sgu_b_in, sgu_norm_g, sgu_w_s, sgu_b_s, sgu_w_out, moe_w_router, moe_e_bias, moe_w_gate, moe_w_up, moe_w_down, moe_ws_gate, moe_ws_up, moe_ws_down):
    depth = ada_w.shape[0]
    xp, xs = x_prompt.reshape(TP, D), x_sample.reshape(TS, D)
    cond = jnp.concatenate([c_ctx[None, :], c, jnp.zeros((8 - 1 - DEC_BATCH, D), F32)], axis=0)
    mod = _modulation(cond, ada_w, ada_b)
    cos, sin = _rope_tables()
    vec = lambda a: a.reshape(1, -1)
    new_s = new_k = new_v = None
    for l in range(depth):
        if l % 2 == 0:
            e = l // 2
            lam_init = 0.8 - 0.6 * math.exp(-0.3 * l)
            a, r_a, gl, q_b, k_b, v_b = _in_proj(xp, xs, mod, l, vec(norm_pre_mix[l]), _pad_in_proj(ab_w_in[e]), cos, sin)
            s0_t = jnp.swapaxes(state_gla[:, e], -1, -2)
            same_head = jnp.eye(H_A, dtype=bool)[None, None, :, None, :, None]
            s0_t = jnp.where(same_head, s0_t[:, :, :, :, None, :], 0.0).reshape(DEC_BATCH, 2, V_A, Q_A)
            o_f, o_bw, s_fin_t = _gla(a, gl, gla_w_g2[e], gla_b_g2[e].reshape(2, 1, Q_A), s0_t)
            o_att = _diff_attention(q_b, k_b, v_b, cache_k, cache_v, diff_lambda[e], lam_init)
            x = _mix_out(lam_init, o_f, o_bw, r_a, o_att, xp, xs, mod, l, vec(gla_norm_g[e]), vec(diff_norm_g[e]),
                         ab_w_out[e].astype(BF16), vec(norm_post_mix[l]))
            new_s = jnp.swapaxes(s_fin_t, -1, -2)
            new_k = k_b[:TP].reshape(BATCH, SEQ, H_B, 2, DQK_B).transpose(0, 2, 3, 1, 4)
            new_v = v_b[:TP].reshape(BATCH, SEQ, H_B, DV_B).transpose(0, 2, 1, 3)
        else:
            o = l // 2
            x = _sgu(xp, xs, mod, l, vec(norm_pre_mix[l]), sgu_w_in[o].astype(BF16), vec(sgu_b_in[o]),
                     vec(sgu_norm_g[o]), sgu_w_s[o], sgu_b_s[o].T, sgu_w_out[o].astype(BF16),
                     vec(norm_post_mix[l]))
        xp, xs = _moe_layer(x, mod, l, vec(norm_pre_ffn[l]), vec(norm_post_ffn[l]), moe_w_router[l], moe_e_bias[l],
                            moe_w_gate, moe_w_up, moe_w_down, moe_ws_gate[l], moe_ws_up[l], moe_ws_down[l])
    y_prompt = xp.reshape(BATCH, SEQ, D)
    y_sample = xs.reshape(DEC_BATCH, DEC_SEQ, D)
    return (y_prompt, y_sample, new_s[:, None], new_k[:, None], new_v[:, None])
```

```python
import functools
import math

import jax
import jax.numpy as jnp
from jax import lax
from jax.experimental import pallas as pl
from jax.experimental.pallas import tpu as pltpu
from jax.experimental.pallas import tpu_sc as plsc

F32 = jnp.float32
BF16 = jnp.bfloat16
I32 = jnp.int32

D = 1024
BATCH, SEQ = 32, 256
DEC_BATCH, DEC_SEQ = 4, 2048
PAST_LEN = 256
GRID_W = 64
EPS = 1e-6
TP = BATCH * SEQ
TS = DEC_BATCH * DEC_SEQ
T = TP + TS
H_A, DK_A, DV_A = 4, 64, 128
Q_A, V_A = H_A * DK_A, H_A * DV_A
GATE_RANK, GATE_TAU, GLA_CHUNK = 16, 16.0, 64
H_B, DQK_B, DV_B = 4, 64, 128
QK_B, V_B = H_B * 2 * DQK_B, H_B * DV_B
ROPE_BASE = 10000.0
SGU_DIM, SGU_GROUPS, SGU_CHUNK = 1024, 4, 128
N_EXPERTS, TOP_K, N_GROUPS, TOPK_GROUPS = 64, 8, 8, 4
GROUP_SIZE = N_EXPERTS // N_GROUPS
D_EXPERT, D_SHARED = 256, 256
ROUTED_SCALE = 2.5

TM = 512
NPT = TP // TM
TILES_PER_DEC = DEC_SEQ // TM
SEG = 256
NSEG = T // SEG
NSEG_P = TP // SEG
SEG_PER_DEC = DEC_SEQ // SEG
TR = 512
TD = 512
GM = 512
GM_SUB = 256
NT_MAX = T * TOP_K // GM + N_EXPERTS
SP = NT_MAX * GM
GL_PAD = 128
VMEM_LIMIT = 56 * 1024 * 1024
NEG_INF = float("-inf")


def _bdot(a, b):
    return jnp.dot(a.astype(BF16), b.astype(BF16), preferred_element_type=F32)


def _bdot_nt(a, b):
    return lax.dot_general(a.astype(BF16), b.astype(BF16), (((1,), (1,)), ((), ())),
                           preferred_element_type=F32)


def _bdot_tn(a, b):
    return lax.dot_general(a.astype(BF16), b.astype(BF16), (((0,), (0,)), ((), ())),
                           preferred_element_type=F32)


def _split3(x):
    x1 = x.astype(BF16)
    r1 = x - x1.astype(F32)
    x2 = r1.astype(BF16)
    x3 = (r1 - x2.astype(F32)).astype(BF16)
    return x1, x2, x3


def _rms(x, g):
    return x * lax.rsqrt(jnp.mean(x * x, axis=-1, keepdims=True) + EPS) * g


def _silu(x):
    return x * jax.nn.sigmoid(x)


def _mod_row(i):
    return jnp.where(i < NPT, 0, 1 + (i - NPT) // TILES_PER_DEC)


def _params(sem, limit=VMEM_LIMIT):
    return pltpu.CompilerParams(dimension_semantics=sem, vmem_limit_bytes=limit)


def _mod_kernel(c_ref, w_ref, b_ref, o_ref):
    o_ref[0] = _bdot(_silu(c_ref[...]), w_ref[0]) + b_ref[0]


def _modulation(cond, ada_w, ada_b):
    depth = ada_w.shape[0]
    nj = 6
    out = pl.pallas_call(
        _mod_kernel,
        grid=(depth, nj),
        in_specs=[
            pl.BlockSpec((8, D), lambda l, j: (0, 0)),
            pl.BlockSpec((1, D, D), lambda l, j: (l, 0, j)),
            pl.BlockSpec((1, 1, D), lambda l, j: (l, 0, j)),
        ],
        out_specs=pl.BlockSpec((1, 8, D), lambda l, j: (l, 0, j)),
        out_shape=jax.ShapeDtypeStruct((depth, 8, 6 * D), F32),
        compiler_params=_params(("arbitrary", "arbitrary")),
        name="adaln_modulation",
    )(cond, ada_w, ada_b.reshape(depth, 1, 6 * D))
    return out.reshape(depth, 8, 6, D)


_C_A, _C_R, _C_GL, _C_Q, _C_K, _C_V, _C_END = 0, 1024, 1536, 1664, 2176, 2688, 3200


def _rope(x, cos, sin):
    lane = lax.broadcasted_iota(I32, x.shape, 1)
    first = (lane % 32) < 16
    n = x.shape[1]
    xr = jnp.where(first, -pltpu.roll(x, n - 16, 1), pltpu.roll(x, 16, 1))
    return x * cos + xr * sin


def _stream_specs():
    return [pl.BlockSpec((TM, D), lambda i: (jnp.minimum(i, NPT - 1), 0)),
            pl.BlockSpec((TM, D), lambda i: (jnp.maximum(i - NPT, 0), 0))]


def _stream_tile(xp_ref, xs_ref):
    return jnp.where(pl.program_id(0) < NPT, xp_ref[...], xs_ref[...])


def _in_kernel(xp_ref, xs_ref, mod_ref, g_ref, w_ref, cos_ref, sin_ref,
               a_ref, r_ref, gl_ref, q_ref, k_ref, v_ref):
    i = pl.program_id(0)
    m = mod_ref[0, 0]
    h = _rms(_stream_tile(xp_ref, xs_ref), g_ref[...]) * (1.0 + m[1:2]) + m[0:1]
    hb = h.astype(BF16)

    def proj(c0, c1):
        return jnp.dot(hb, w_ref[:, c0:c1], preferred_element_type=F32)

    a_ref[...] = proj(_C_A, _C_R)
    r_ref[...] = proj(_C_R, _C_GL)
    gl_ref[...] = proj(_C_GL, _C_Q)
    v_ref[...] = proj(_C_V, _C_END)
    q = proj(_C_Q, _C_K)
    k = proj(_C_K, _C_V)

    @pl.when(i < NPT)
    def _():
        q_ref[...] = q
        k_ref[...] = k

    @pl.when(i >= NPT)
    def _():
        cos = cos_ref[...]
        sin = sin_ref[...]
        q_ref[...] = _rope(q, cos, sin)
        k_ref[...] = _rope(k, cos, sin)


def _in_proj(xp, xs, mod, l, g, w_pad, cos, sin):
    tok = lambda width: pl.BlockSpec((TM, width), lambda i: (i, 0))
    rope_spec = pl.BlockSpec((TM, QK_B), lambda i: (jnp.maximum(i - NPT, 0) % TILES_PER_DEC, 0))
    widths = (1024, 512, GL_PAD, 512, 512, 512)
    return pl.pallas_call(
        _in_kernel,
        grid=(T // TM,),
        in_specs=_stream_specs() + [
            pl.BlockSpec((1, 1, 6, D), lambda i: (l, _mod_row(i), 0, 0)),
            pl.BlockSpec((1, D), lambda i: (0, 0)),
            pl.BlockSpec((D, _C_END), lambda i: (0, 0)),
            rope_spec, rope_spec,
        ],
        out_specs=[tok(w) for w in widths],
        out_shape=[jax.ShapeDtypeStruct((T, w), F32) for w in widths],
        compiler_params=_params(("arbitrary",)),
        name="mixer_ab_in_proj",
    )(xp, xs, mod, g, w_pad, cos, sin)


def _log_sigmoid(x):
    return jnp.minimum(x, 0.0) - jnp.log(1.0 + jnp.exp(-jnp.abs(x)))


def _gla_kernel(af_ref, ab_ref, glf_ref, glb_ref, wg_ref, bg_ref, s0_ref,
                of_ref, ob_ref, sfin_ref, st_ref):
    i = pl.program_id(0)

    @pl.when(i < NSEG_P)
    def _():
        st_ref[...] = jnp.zeros_like(st_ref)

    @pl.when(jnp.logical_and(i >= NSEG_P, (i - NSEG_P) % SEG_PER_DEC == 0))
    def _():
        st_ref[...] = s0_ref[0]

    r = lax.broadcasted_iota(I32, (SEG, SEG), 0)
    c = lax.broadcasted_iota(I32, (SEG, SEG), 1)
    same = (r // GLA_CHUNK) == (c // GLA_CHUNK)
    nchunk = SEG // GLA_CHUNK
    own_head = (lax.broadcasted_iota(I32, (V_A, Q_A), 0) // DV_A) == (lax.broadcasted_iota(I32, (V_A, Q_A), 1) // DK_A)

    for d, (a_ref, gl_ref, o_ref) in enumerate(((af_ref, glf_ref, of_ref), (ab_ref, glb_ref, ob_ref))):
        fwd = d == 0
        gcol = gl_ref[:, d * GATE_RANK:(d + 1) * GATE_RANK]
        la = _log_sigmoid(_bdot(gcol, wg_ref[d]) + bg_ref[d]) / GATE_TAU
        causal = jnp.logical_and(same, (c <= r) if fwd else (c >= r))
        tri = jnp.where(causal, 1.0, 0.0).astype(BF16)
        l1, l2, l3 = _split3(la)
        b_all = (jnp.dot(tri, l1, preferred_element_type=F32)
                 + jnp.dot(tri, l2, preferred_element_type=F32)
                 + jnp.dot(tri, l3, preferred_element_type=F32))
        q_in_all = a_ref[:, 0:Q_A] * (DK_A ** -0.5) * jnp.exp(b_all)
        kd_all = a_ref[:, Q_A:2 * Q_A] * jnp.exp(-b_all)
        intra = []
        for h in range(H_A):
            kc = slice(h * DK_A, (h + 1) * DK_A)
            attn = jnp.where(causal, _bdot_nt(q_in_all[:, kc], kd_all[:, kc]), 0.0)
            intra.append(_bdot(attn, a_ref[:, 2 * Q_A + h * DV_A:2 * Q_A + (h + 1) * DV_A]))
        intra = jnp.concatenate(intra, axis=1)
        state = st_ref[d]
        order = range(nchunk) if fwd else range(nchunk - 1, -1, -1)
        for ch in order:
            r0 = ch * GLA_CHUNK
            rows = slice(r0, r0 + GLA_CHUNK)
            end = r0 + GLA_CHUNK - 1 if fwd else r0
            b_end = b_all[end:end + 1, :]
            kw = a_ref[rows, Q_A:2 * Q_A] * jnp.exp(b_end - b_all[rows, :])
            o_ref[rows, :] = intra[rows, :] + _bdot_nt(q_in_all[rows, :], state)
            kv_t = _bdot_tn(a_ref[rows, 2 * Q_A:2 * Q_A + V_A], kw)
            state = state * jnp.exp(b_end) + jnp.where(own_head, kv_t, 0.0)
        st_ref[d] = state

    @pl.when(i < NSEG_P)
    def _():
        for d in range(2):
            for h in range(H_A):
                sfin_ref[0, d, h] = st_ref[d, h * DV_A:(h + 1) * DV_A, h * DK_A:(h + 1) * DK_A]


def _seg_bwd(i):
    j = i - NSEG_P
    return jnp.where(i < NSEG_P, i, NSEG_P + (j // SEG_PER_DEC) * SEG_PER_DEC + (SEG_PER_DEC - 1 - j % SEG_PER_DEC))


def _gla(a, gl, wg, bg, s0_t):
    seg = lambda width, f: pl.BlockSpec((SEG, width), lambda i: (f(i), 0))
    ident = lambda i: i
    st_block = (1, 2, H_A, DV_A, DK_A)
    return pl.pallas_call(
        _gla_kernel,
        grid=(NSEG,),
        in_specs=[
            seg(D, ident), seg(D, _seg_bwd), seg(GL_PAD, ident), seg(GL_PAD, _seg_bwd),
            pl.BlockSpec((2, GATE_RANK, Q_A), lambda i: (0, 0, 0)),
            pl.BlockSpec((2, 1, Q_A), lambda i: (0, 0, 0)),
            pl.BlockSpec((1, 2, V_A, Q_A), lambda i: (jnp.maximum(i - NSEG_P, 0) // SEG_PER_DEC, 0, 0, 0)),
        ],
        out_specs=[
            seg(V_A, ident), seg(V_A, _seg_bwd),
            pl.BlockSpec(st_block, lambda i: (jnp.minimum(i, NSEG_P - 1), 0, 0, 0, 0)),
        ],
        out_shape=[
            jax.ShapeDtypeStruct((T, V_A), F32),
            jax.ShapeDtypeStruct((T, V_A), F32),
            jax.ShapeDtypeStruct((BATCH, 2, H_A, DV_A, DK_A), F32),
        ],
        scratch_shapes=[pltpu.VMEM((2, V_A, Q_A), F32)],
        compiler_params=_params(("arbitrary",)),
        name="gla_bidir",
    )(a, a, gl, gl, wg, bg, s0_t)


def _diff_lambda(lam_ref, lam_init):
    lp = lam_ref[...]
    s01 = jnp.sum(lp[0:1] * lp[1:2], axis=1, keepdims=True)
    s23 = jnp.sum(lp[2:3] * lp[3:4], axis=1, keepdims=True)
    return jnp.exp(s01) - jnp.exp(s23) + lam_init


def _attn_prompt_kernel(lam_init, q_ref, k_ref, v_ref, lam_ref, o_ref):
    lam = _diff_lambda(lam_ref, lam_init)
    for h in range(H_B):
        ps = []
        for m in range(2):
            cols = slice((2 * h + m) * DQK_B, (2 * h + m + 1) * DQK_B)
            s = _bdot_nt(q_ref[:, cols] * (DQK_B ** -0.5), k_ref[:, cols])
            e = jnp.exp(s - jnp.max(s, axis=1, keepdims=True))
            ps.append(e * (1.0 / jnp.sum(e, axis=1, keepdims=True)))
        w = ps[0] - lam * ps[1]
        o_ref[:, h * DV_B:(h + 1) * DV_B] = _bdot(w, v_ref[:, h * DV_B:(h + 1) * DV_B])


def _attn_sample_kernel(lam_init, q_ref, k_ref, v_ref, ck_ref, cv_ref, lam_ref, o_ref):
    lam = _diff_lambda(lam_ref, lam_init)
    for h in range(H_B):
        parts = []
        for m in range(2):
            cols = slice((2 * h + m) * DQK_B, (2 * h + m + 1) * DQK_B)
            q = q_ref[:, cols] * (DQK_B ** -0.5)
            sc = _bdot_nt(q, ck_ref[0, 0, h, m])
            sn = _bdot_nt(q, k_ref[:, cols])
            mx = jnp.maximum(jnp.max(sc, axis=1, keepdims=True), jnp.max(sn, axis=1, keepdims=True))
            ec = jnp.exp(sc - mx)
            en = jnp.exp(sn - mx)
            inv = (1.0 if m == 0 else -lam) / (jnp.sum(ec, axis=1, keepdims=True) + jnp.sum(en, axis=1, keepdims=True))
            parts.append((ec * inv, en * inv))
        wc = parts[0][0] + parts[1][0]
        wn = parts[0][1] + parts[1][1]
        o_ref[:, h * DV_B:(h + 1) * DV_B] = (_bdot(wc, cv_ref[0, 0, h])
                                             + _bdot(wn, v_ref[:, h * DV_B:(h + 1) * DV_B]))


QB = SEQ
NQB_DEC = DEC_SEQ // QB


def _attn_kernel(lam_init, q_ref, kp_ref, vp_ref, ks_ref, vs_ref, ck_ref, cv_ref, lam_ref, o_ref):
    i = pl.program_id(0)

    @pl.when(i < BATCH)
    def _():
        _attn_prompt_kernel(lam_init, q_ref, kp_ref, vp_ref, lam_ref, o_ref)

    @pl.when(i >= BATCH)
    def _():
        _attn_sample_kernel(lam_init, q_ref, ks_ref, vs_ref, ck_ref, cv_ref, lam_ref, o_ref)


def _diff_attention(q, k, v, cache_k, cache_v, lam_p, lam_init):
    blk = lambda rows, f: pl.BlockSpec((rows, 512), f)
    dec_b = lambda i: jnp.maximum(i - BATCH, 0) // NQB_DEC
    own = lambda i: (i, 0)
    prompt_kv = lambda i: (jnp.minimum(i, BATCH - 1), 0)
    dec_kv = lambda i: (TP // DEC_SEQ + dec_b(i), 0)
    return pl.pallas_call(
        functools.partial(_attn_kernel, lam_init),
        grid=(BATCH + DEC_BATCH * NQB_DEC,),
        in_specs=[
            blk(QB, own), blk(SEQ, prompt_kv), blk(SEQ, prompt_kv), blk(DEC_SEQ, dec_kv), blk(DEC_SEQ, dec_kv),
            pl.BlockSpec((1, 1, H_B, 2, PAST_LEN, DQK_B), lambda i: (dec_b(i), 0, 0, 0, 0, 0)),
            pl.BlockSpec((1, 1, H_B, PAST_LEN, DV_B), lambda i: (dec_b(i), 0, 0, 0, 0)),
            pl.BlockSpec((4, DQK_B), lambda i: (0, 0)),
        ],
        out_specs=blk(QB, own),
        out_shape=jax.ShapeDtypeStruct((T, V_B), F32),
        compiler_params=_params(("arbitrary",)),
        name="diff_attention",
    )(q, k, v, k, v, cache_k, cache_v, lam_p)


def _head_rms(x, g, nheads, width):
    return jnp.concatenate([_rms(x[:, h * width:(h + 1) * width], g) for h in range(nheads)], axis=1)


def _mix_out_kernel(lam_init, of_ref, ob_ref, r_ref, oatt_ref, xp_ref, xs_ref, mod_ref,
                    gg_ref, dg_ref, wo_ref, gp_ref, o_ref):
    m = mod_ref[0, 0]
    o_a = _head_rms(of_ref[...] + ob_ref[...], gg_ref[...], H_A, DV_A) * _silu(r_ref[...])
    o_b = _head_rms(oatt_ref[...], dg_ref[...], H_B, DV_B) * (1.0 - lam_init)
    out = _bdot(o_a, wo_ref[0:V_A, :]) + _bdot(o_b, wo_ref[V_A:V_A + V_B, :])
    o_ref[...] = _stream_tile(xp_ref, xs_ref) + m[2:3] * _rms(out, gp_ref[...])


def _mix_out(lam_init, o_f, o_b, r_a, o_att, xp, xs, mod, l, gla_g, diff_g, w_o, g_post):
    tok = lambda width: pl.BlockSpec((TM, width), lambda i: (i, 0))
    vec = lambda width: pl.BlockSpec((1, width), lambda i: (0, 0))
    return pl.pallas_call(
        functools.partial(_mix_out_kernel, lam_init),
        grid=(T // TM,),
        in_specs=[
            tok(512), tok(512), tok(512), tok(512), *_stream_specs(),
            pl.BlockSpec((1, 1, 6, D), lambda i: (l, _mod_row(i), 0, 0)),
            vec(DV_A), vec(DV_B),
            pl.BlockSpec((V_A + V_B, D), lambda i: (0, 0)),
            vec(D),
        ],
        out_specs=tok(D),
        out_shape=jax.ShapeDtypeStruct((T, D), F32),
        compiler_params=_params(("arbitrary",)),
        name="mixer_ab_out",
    )(o_f, o_b, r_a, o_att, xp, xs, mod, gla_g, diff_g, w_o, g_post)


def _gelu_tanh(x):
    return 0.5 * x * (1.0 + jnp.tanh(math.sqrt(2.0 / math.pi) * (x + 0.044715 * (x * x * x))))


def _sgu_kernel(xp_ref, xs_ref, mod_ref, gpre_ref, win_ref, bin_ref, vg_ref, ws_ref, bs_ref,
                wout_ref, gpost_ref, o_ref, t_ref):
    m = mod_ref[0, 0]
    x = _stream_tile(xp_ref, xs_ref)
    h = _rms(x, gpre_ref[...]) * (1.0 + m[1:2]) + m[0:1]
    z = _gelu_tanh(_bdot(h, win_ref[...]) + bin_ref[...])
    v = _rms(z[:, SGU_DIM:], vg_ref[...])
    gw = SGU_DIM // SGU_GROUPS
    for ch in range(TM // SGU_CHUNK):
        rows = slice(ch * SGU_CHUNK, (ch + 1) * SGU_CHUNK)
        for g in range(SGU_GROUPS):
            cols = slice(g * gw, (g + 1) * gw)
            vs = _bdot(ws_ref[g], v[rows, cols]) + bs_ref[:, g:g + 1]
            t_ref[rows, cols] = (z[rows, cols] * vs).astype(BF16)
    out = jnp.dot(t_ref[...], wout_ref[...], preferred_element_type=F32)
    o_ref[...] = x + m[2:3] * _rms(out, gpost_ref[...])


def _sgu(xp, xs, mod, l, g_pre, w_in, b_in, v_g, w_s, b_s_t, w_out, g_post):
    tok = pl.BlockSpec((TM, D), lambda i: (i, 0))
    full = lambda *shape: pl.BlockSpec(shape, lambda i: (0,) * len(shape))
    return pl.pallas_call(
        _sgu_kernel,
        grid=(T // TM,),
        in_specs=_stream_specs() + [
            pl.BlockSpec((1, 1, 6, D), lambda i: (l, _mod_row(i), 0, 0)),
            full(1, D), full(D, 2 * SGU_DIM), full(1, 2 * SGU_DIM), full(1, SGU_DIM),
            full(SGU_GROUPS, SGU_CHUNK, SGU_CHUNK), full(SGU_CHUNK, SGU_GROUPS),
            full(SGU_DIM, D), full(1, D),
        ],
        out_specs=tok,
        out_shape=jax.ShapeDtypeStruct((T, D), F32),
        scratch_shapes=[pltpu.VMEM((TM, SGU_DIM), BF16)],
        compiler_params=_params(("arbitrary",)),
        name="sgu_mixer",
    )(xp, xs, mod, g_pre, w_in, b_in, v_g, w_s, b_s_t, w_out, g_post)


LANES = 128
U32 = jnp.uint32
PACKED = D // 2
ROWS_PER_TOKEN = PACKED // LANES
HIGH_HALF = 0xFFFF0000


def _pack_rows(x):
    bits = lax.bitcast_convert_type(x.astype(BF16).astype(F32), U32)
    return bits[:, :PACKED] | (bits[:, PACKED:] >> 16)


def _unpack_rows(u):
    return (lax.bitcast_convert_type(u & U32(HIGH_HALF), F32), lax.bitcast_convert_type(u << 16, F32))


def _store_token_tiles(ref, u, first=0):
    n = u.shape[0]
    for c in range(ROWS_PER_TOKEN):
        ref[pl.ds(first * ROWS_PER_TOKEN + c, n, stride=ROWS_PER_TOKEN), :] = u[:, c * LANES:(c + 1) * LANES]


def _load_token_tiles(ref, n, first=0):
    return jnp.concatenate([ref[pl.ds(first * ROWS_PER_TOKEN + c, n, stride=ROWS_PER_TOKEN), :]
                            for c in range(ROWS_PER_TOKEN)], axis=1)


def _router_kernel(x_ref, mod_ref, g_ref, wr_ref, eb_ref,
                   h_ref, te_ref, wn_ref, rk_ref, cnt_ref, carry_ref):
    i = pl.program_id(0)

    @pl.when(i == 0)
    def _():
        carry_ref[...] = jnp.zeros_like(carry_ref)

    m = mod_ref[0, 0]
    h = _rms(x_ref[...], g_ref[...]) * (1.0 + m[4:5]) + m[3:4]
    _store_token_tiles(h_ref, _pack_rows(h))
    h1, h2, _ = _split3(h)
    w1, w2, _ = _split3(wr_ref[...])
    nt = lambda a, b: lax.dot_general(a, b, (((1,), (1,)), ((), ())), preferred_element_type=F32)
    logits = nt(w1, h1) + nt(w1, h2) + nt(w2, h1)
    scores = jax.nn.sigmoid(logits)
    sel = scores + eb_ref[...]

    row8 = lax.broadcasted_iota(I32, (GROUP_SIZE, TR), 0)
    gscore = []
    for g in range(N_GROUPS):
        xg = sel[g * GROUP_SIZE:(g + 1) * GROUP_SIZE]
        m1 = jnp.max(xg, axis=0, keepdims=True)
        i1 = jnp.min(jnp.where(xg == m1, row8, GROUP_SIZE), axis=0, keepdims=True)
        m2 = jnp.max(jnp.where(row8 == i1, NEG_INF, xg), axis=0, keepdims=True)
        gscore.append(m1 + m2)
    pieces = []
    for g in range(N_GROUPS):
        rank = jnp.zeros((1, TR), I32)
        for g2 in range(N_GROUPS):
            if g2 == g:
                continue
            beats = (gscore[g2] >= gscore[g]) if g2 < g else (gscore[g2] > gscore[g])
            rank = rank + beats.astype(I32)
        pieces.append(jnp.where(rank < TOPK_GROUPS, sel[g * GROUP_SIZE:(g + 1) * GROUP_SIZE], NEG_INF))
    cur = jnp.concatenate(pieces, axis=0)

    row = lax.broadcasted_iota(I32, (N_EXPERTS, TR), 0)
    idxs, ws = [], []
    for _ in range(TOP_K):
        mx = jnp.max(cur, axis=0, keepdims=True)
        idx = jnp.min(jnp.where(cur == mx, row, N_EXPERTS), axis=0, keepdims=True)
        hit = row == idx
        ws.append(jnp.sum(jnp.where(hit, scores, 0.0), axis=0, keepdims=True))
        cur = jnp.where(hit, NEG_INF, cur)
        idxs.append(idx)
    mask = jnp.zeros((N_EXPERTS, TR), F32)
    for idx in idxs:
        mask = mask + (row == idx).astype(F32)
    wsum = ws[0]
    for wk in ws[1:]:
        wsum = wsum + wk

    tj = lax.broadcasted_iota(I32, (TR, TR), 0)
    ti = lax.broadcasted_iota(I32, (TR, TR), 1)
    upper = jnp.where(tj < ti, 1.0, 0.0).astype(BF16)
    pos = carry_ref[...] + jnp.dot(mask.astype(BF16), upper, preferred_element_type=F32)
    for k in range(TOP_K):
        hit = row == idxs[k]
        te_ref[k:k + 1, :] = idxs[k]
        wn_ref[k:k + 1, :] = ws[k] / wsum * ROUTED_SCALE
        rk_ref[k:k + 1, :] = jnp.sum(jnp.where(hit, pos, 0.0), axis=0, keepdims=True).astype(I32)
    carry_ref[...] = carry_ref[...] + jnp.sum(mask, axis=1, keepdims=True)
    cnt_ref[...] = carry_ref[...]


def _router(x, mod, l, g, wr_t, e_bias):
    kt = lambda dtype: jax.ShapeDtypeStruct((TOP_K, T), dtype)
    kt_spec = pl.BlockSpec((TOP_K, TR), lambda i: (0, i))
    tiles_per_dec = DEC_SEQ // TR
    mod_row = lambda i: jnp.where(i < TP // TR, 0, 1 + (i - TP // TR) // tiles_per_dec)
    return pl.pallas_call(
        _router_kernel,
        grid=(T // TR,),
        in_specs=[
            pl.BlockSpec((TR, D), lambda i: (i, 0)),
            pl.BlockSpec((1, 1, 6, D), lambda i: (l, mod_row(i), 0, 0)),
            pl.BlockSpec((1, D), lambda i: (0, 0)),
            pl.BlockSpec((N_EXPERTS, D), lambda i: (0, 0)),
            pl.BlockSpec((N_EXPERTS, 1), lambda i: (0, 0)),
        ],
        out_specs=[
            pl.BlockSpec((TR * ROWS_PER_TOKEN, LANES), lambda i: (i, 0)),
            kt_spec, kt_spec, kt_spec,
            pl.BlockSpec((N_EXPERTS, 1), lambda i: (0, 0)),
        ],
        out_shape=[
            jax.ShapeDtypeStruct((T * ROWS_PER_TOKEN, LANES), U32), kt(I32), kt(F32), kt(I32),
            jax.ShapeDtypeStruct((N_EXPERTS, 1), F32),
        ],
        scratch_shapes=[pltpu.VMEM((N_EXPERTS, 1), F32)],
        compiler_params=_params(("arbitrary",)),
        name="moe_router",
    )(x, mod, g, wr_t, e_bias)


_PAD_BITS = tuple(1 << b for b in range(GM.bit_length() - 1))


def _pad_fill_kernel(pad_start_ref, pad_len_ref, xg_in_ref, xg_ref, zero_ref, sem):
    del xg_in_ref
    zero_ref[...] = jnp.zeros_like(zero_ref)

    def pad_copies(e):
        start = pad_start_ref[e]
        n = pad_len_ref[e]
        copies = []
        for bit in _PAD_BITS:
            first = start + (n & ~(2 * bit - 1))
            copies.append(((n & bit) != 0, pltpu.make_async_copy(
                zero_ref.at[pl.ds(0, bit)], xg_ref.at[pl.ds(first, bit)], sem)))
        return copies

    def start_e(e, carry):
        for on, cp in pad_copies(e):
            @pl.when(on)
            def _():
                cp.start()
        return carry

    def wait_e(e, carry):
        for on, cp in pad_copies(e):
            @pl.when(on)
            def _():
                cp.wait()
        return carry

    lax.fori_loop(0, N_EXPERTS, start_e, 0)
    lax.fori_loop(0, N_EXPERTS, wait_e, 0)


def _pad_fill(pad_start, pad_len, xg):
    grid_spec = pltpu.PrefetchScalarGridSpec(
        num_scalar_prefetch=2,
        grid=(1,),
        in_specs=[pl.BlockSpec(memory_space=pl.ANY)],
        out_specs=pl.BlockSpec(memory_space=pl.ANY),
        scratch_shapes=[pltpu.VMEM((GM // 2, ROWS_PER_TOKEN, LANES), xg.dtype), pltpu.SemaphoreType.DMA],
    )
    return pl.pallas_call(
        _pad_fill_kernel,
        grid_spec=grid_spec,
        out_shape=jax.ShapeDtypeStruct(xg.shape, xg.dtype),
        input_output_aliases={2: 0},
        compiler_params=_params(("arbitrary",)),
        name="moe_pad_fill",
    )(pad_start, pad_len, xg)


SC_CORES, SC_SUBCORES = 2, 16
SC_WORKERS = SC_CORES * SC_SUBCORES
SC_W = 64


def _sc_worker_id():
    return lax.axis_index("s") * SC_CORES + lax.axis_index("c")


def _sc_dispatch(h3, slot3):
    nchunk = T // SC_WORKERS // SC_W
    mesh = plsc.VectorSubcoreMesh(core_axis_name="c", subcore_axis_name="s")
    tile = (SC_W, ROWS_PER_TOKEN, LANES)

    @functools.partial(
        pl.kernel, mesh=mesh,
        out_type=jax.ShapeDtypeStruct((SP, ROWS_PER_TOKEN, LANES), h3.dtype),
        scratch_types=[pltpu.VMEM((TOP_K, SC_W), I32), pltpu.VMEM((TOP_K, SC_W), I32),
                       pltpu.VMEM(tile, h3.dtype), pltpu.VMEM(tile, h3.dtype),
                       pltpu.SemaphoreType.DMA((2,)), pltpu.SemaphoreType.DMA((2,))],
    )
    def k(h_hbm, slot_hbm, xg_hbm, idx0, idx1, rows0, rows1, lsem, ssem):
        first = _sc_worker_id() * nchunk
        idx = (idx0, idx1)
        rows = (rows0, rows1)

        def loads(j, b):
            blk = first + j
            tok = pl.multiple_of(blk * SC_W, SC_W)
            return (pltpu.make_async_copy(slot_hbm.at[blk], idx[b], lsem.at[b]),
                    pltpu.make_async_copy(h_hbm.at[pl.ds(tok, SC_W)], rows[b], lsem.at[b]))

        def scatters(b):
            return [pltpu.make_async_copy(rows[b], xg_hbm.at[idx[b].at[kk]], ssem.at[b]) for kk in range(TOP_K)]

        for cp in loads(0, 0):
            cp.start()

        @pl.loop(0, nchunk, step=2)
        def _(j):
            for b in (0, 1):
                jj = j + b
                for cp in loads(jj, b):
                    cp.wait()
                for cp in scatters(b):
                    cp.start()

                @pl.when(jj + 1 < nchunk)
                def _():
                    @pl.when(jj >= 1)
                    def _():
                        for cp in scatters(1 - b):
                            cp.wait()
                    for cp in loads(jj + 1, 1 - b):
                        cp.start()

        for b in (0, 1):
            for cp in scatters(b):
                cp.wait()

    return k(h3, slot3)


def _gmm_kernel(tile_e_ref, tile_blk_ref, nvalid_ref, x_ref, wg_ref, wu_ref, wd_ref,
                y_ref, wgu_scr, wd_scr):
    j = pl.program_id(0)

    @pl.when(j < nvalid_ref[0])
    def _():
        prev = tile_e_ref[jnp.maximum(j - 1, 0)]

        @pl.when(jnp.logical_or(j == 0, tile_e_ref[j] != prev))
        def _():
            wgu_scr[:, 0:D_EXPERT] = wg_ref[0, 0].astype(BF16)
            wgu_scr[:, D_EXPERT:2 * D_EXPERT] = wu_ref[0, 0].astype(BF16)
            wd_scr[...] = wd_ref[0, 0].astype(BF16)

        for s in range(GM // GM_SUB):
            x_hi, x_lo = _unpack_rows(_load_token_tiles(x_ref, GM_SUB, s * GM_SUB))
            gu = (jnp.dot(x_hi.astype(BF16), wgu_scr[0:PACKED, :], preferred_element_type=F32)
                  + jnp.dot(x_lo.astype(BF16), wgu_scr[PACKED:D, :], preferred_element_type=F32))
            hid = _silu(gu[:, 0:D_EXPERT]) * gu[:, D_EXPERT:2 * D_EXPERT]
            y = jnp.dot(hid.astype(BF16), wd_scr[...], preferred_element_type=F32)
            _store_token_tiles(y_ref, _pack_rows(y), s * GM_SUB)


def _gmm(tile_e, tile_blk, nvalid, xg, l, w_gate, w_up, w_down):
    row_tile = pl.BlockSpec((GM * ROWS_PER_TOKEN, LANES), lambda j, te, tb, nv: (tb[j], 0))
    grid_spec = pltpu.PrefetchScalarGridSpec(
        num_scalar_prefetch=3,
        grid=(NT_MAX,),
        in_specs=[
            row_tile,
            pl.BlockSpec((1, 1, D, D_EXPERT), lambda j, te, tb, nv: (l, te[j], 0, 0)),
            pl.BlockSpec((1, 1, D, D_EXPERT), lambda j, te, tb, nv: (l, te[j], 0, 0)),
            pl.BlockSpec((1, 1, D_EXPERT, D), lambda j, te, tb, nv: (l, te[j], 0, 0)),
        ],
        out_specs=row_tile,
        scratch_shapes=[pltpu.VMEM((D, 2 * D_EXPERT), BF16), pltpu.VMEM((D_EXPERT, D), BF16)],
    )
    return pl.pallas_call(
        _gmm_kernel,
        grid_spec=grid_spec,
        out_shape=jax.ShapeDtypeStruct((SP * ROWS_PER_TOKEN, LANES), U32),
        compiler_params=_params(("arbitrary",)),
        name="moe_grouped_matmul",
    )(tile_e, tile_blk, nvalid, xg, w_gate, w_up, w_down)


def _sc_gather(table3, idx):
    n_idx = idx.shape[0]
    per_w = n_idx // SC_WORKERS
    nchunk = per_w // SC_W
    mesh = plsc.VectorSubcoreMesh(core_axis_name="c", subcore_axis_name="s")
    tile = (SC_W, ROWS_PER_TOKEN, LANES)

    @functools.partial(
        pl.kernel, mesh=mesh,
        out_type=jax.ShapeDtypeStruct((n_idx, ROWS_PER_TOKEN, LANES), table3.dtype),
        scratch_types=[pltpu.VMEM((per_w,), I32), pltpu.VMEM(tile, table3.dtype), pltpu.VMEM(tile, table3.dtype),
                       pltpu.SemaphoreType.DMA((2,)), pltpu.SemaphoreType.DMA((2,))],
    )
    def k(table_hbm, idx_hbm, out_hbm, idx_v, rows0, rows1, gsem, wsem):
        base = pl.multiple_of(_sc_worker_id() * per_w, per_w)
        rows = (rows0, rows1)
        pltpu.sync_copy(idx_hbm.at[pl.ds(base, per_w)], idx_v)

        def gather(j, b):
            ids = idx_v.at[pl.ds(pl.multiple_of(j * SC_W, SC_W), SC_W)]
            return pltpu.make_async_copy(table_hbm.at[ids], rows[b], gsem.at[b])

        def write(j, b):
            dst = out_hbm.at[pl.ds(pl.multiple_of(base + j * SC_W, SC_W), SC_W)]
            return pltpu.make_async_copy(rows[b], dst, wsem.at[b])

        gather(0, 0).start()

        @pl.loop(0, nchunk, step=2)
        def _(j):
            for b in (0, 1):
                jj = j + b
                gather(jj, b).wait()
                write(jj, b).start()

                @pl.when(jj + 1 < nchunk)
                def _():
                    @pl.when(jj >= 1)
                    def _():
                        write(jj - 1, 1 - b).wait()
                    gather(jj + 1, 1 - b).start()

        write(nchunk - 2, 0).wait()
        write(nchunk - 1, 1).wait()

    return k(table3, idx)


def _combine_kernel(wn_ref, x_ref, mod_ref, gpre_ref, gp_ref, wsg_ref, wsu_ref, wsd_ref, y_ref, o_ref):
    m = mod_ref[0, 0]
    x = x_ref[...]
    hb = (_rms(x, gpre_ref[...]) * (1.0 + m[4:5]) + m[3:4]).astype(BF16)
    hid = (_silu(jnp.dot(hb, wsg_ref[...], preferred_element_type=F32))
           * jnp.dot(hb, wsu_ref[...], preferred_element_type=F32))
    acc = jnp.dot(hid.astype(BF16), wsd_ref[...], preferred_element_type=F32)

    r = lax.broadcasted_iota(I32, (TD, TD), 0)
    c = lax.broadcasted_iota(I32, (TD, TD), 1)
    eye = jnp.where(r == c, 1.0, 0.0).astype(BF16)
    nt = lambda a, b: lax.dot_general(a, b, (((1,), (1,)), ((), ())), preferred_element_type=F32)
    w1, w2, w3 = _split3(wn_ref[...])
    w_t = nt(eye, w1) + nt(eye, w2) + nt(eye, w3)

    acc_hi = acc[:, :PACKED]
    acc_lo = acc[:, PACKED:]
    for k in range(TOP_K):
        y_hi, y_lo = _unpack_rows(_load_token_tiles(y_ref, TD, k * TD))
        acc_hi = acc_hi + y_hi * w_t[:, k:k + 1]
        acc_lo = acc_lo + y_lo * w_t[:, k:k + 1]
    acc = jnp.concatenate([acc_hi, acc_lo], axis=1)
    o_ref[...] = x + m[5:6] * _rms(acc, gp_ref[...])


def _combine(wn, x, mod, l, g_pre, g_post, ws_gate, ws_up, ws_down, ybuf, first_tok, n_tok):
    off = first_tok // TD
    tiles_per_dec = DEC_SEQ // TD
    npd = TP // TD
    mod_row = lambda i: jnp.where(i + off < npd, 0, 1 + (i + off - npd) // tiles_per_dec)
    full = lambda *shape: pl.BlockSpec(shape, lambda i: (0,) * len(shape))
    y_spec = pl.BlockSpec((TOP_K * TD * ROWS_PER_TOKEN, LANES), lambda i: (i, 0))
    return pl.pallas_call(
        _combine_kernel,
        grid=(n_tok // TD,),
        in_specs=[
            pl.BlockSpec((TOP_K, TD), lambda i: (0, i + off)),
            pl.BlockSpec((TD, D), lambda i: (i + off, 0)),
            pl.BlockSpec((1, 1, 6, D), lambda i: (l, mod_row(i), 0, 0)),
            full(1, D), full(1, D), full(D, D_SHARED), full(D, D_SHARED), full(D_SHARED, D),
            y_spec,
        ],
        out_specs=pl.BlockSpec((TD, D), lambda i: (i, 0)),
        out_shape=jax.ShapeDtypeStruct((n_tok, D), F32),
        compiler_params=_params(("arbitrary",)),
        name="moe_combine",
    )(wn, x, mod, g_pre, g_post, ws_gate, ws_up, ws_down, ybuf)


def _moe_layer(x, mod, l, g_pre, g_post, w_router, e_bias, w_gate, w_up, w_down,
               ws_gate, ws_up, ws_down):
    h, top_e, wn, rk, cnt = _router(x, mod, l, g_pre, w_router.T, e_bias.reshape(N_EXPERTS, 1))
    cnt = cnt.reshape(N_EXPERTS).astype(I32)
    padded = (cnt + GM - 1) // GM * GM
    ends = jnp.cumsum(padded)
    offs = ends - padded
    eid = jnp.arange(N_EXPERTS, dtype=I32)[:, None, None]
    slot = rk + jnp.sum(jnp.where(top_e[None] == eid, offs[:, None, None], 0), axis=0)
    nvalid = ends[-1] // GM
    tile_start = jnp.arange(NT_MAX, dtype=I32) * GM
    tile_raw = jnp.sum((tile_start[:, None] >= ends[None, :]).astype(I32), axis=1)
    last = jnp.maximum(nvalid - 1, 0)
    tile_blk = jnp.minimum(jnp.arange(NT_MAX, dtype=I32), last)
    tile_e = jnp.minimum(tile_raw, N_EXPERTS - 1)
    tile_e = jnp.where(jnp.arange(NT_MAX) <= last, tile_e, tile_e[last])
    slot3 = slot.reshape(TOP_K, T // SC_W, SC_W).transpose(1, 0, 2)
    xg = _sc_dispatch(h.reshape(T, ROWS_PER_TOKEN, LANES), slot3)
    xg = _pad_fill(offs + cnt, padded - cnt, xg).reshape(SP * ROWS_PER_TOKEN, LANES)
    yg = _gmm(tile_e, tile_blk, nvalid.reshape(1), xg, l, w_gate, w_up, w_down)
    yg3 = yg.reshape(SP, ROWS_PER_TOKEN, LANES)
    ws = (ws_gate.astype(BF16), ws_up.astype(BF16), ws_down.astype(BF16))
    outs = []
    for first_tok, n_tok in ((0, TP), (TP, TS)):
        ids = slot[:, first_tok:first_tok + n_tok].reshape(TOP_K, n_tok // TD, TD).transpose(1, 0, 2)
        ybuf = _sc_gather(yg3, ids.reshape(TOP_K * n_tok))
        outs.append(_combine(wn, x, mod, l, g_pre, g_post, *ws,
                             ybuf.reshape(TOP_K * n_tok * ROWS_PER_TOKEN, LANES), first_tok, n_tok))
    return outs


def _rope_tables():
    n = DEC_SEQ
    rows = n // GRID_W
    row = jnp.repeat(jnp.arange(rows), GRID_W).astype(F32)
    col = jnp.tile(jnp.arange(GRID_W), rows).astype(F32)
    half = DQK_B // 2
    inv = ROPE_BASE ** (-jnp.arange(0, half, 2, dtype=F32) / half)
    ang_r = row[:, None] * inv
    ang_c = col[:, None] * inv
    ang = jnp.concatenate([ang_r, ang_r, ang_c, ang_c], axis=-1)
    reps = QK_B // DQK_B
    return jnp.tile(jnp.cos(ang), (1, reps)), jnp.tile(jnp.sin(ang), (1, reps))


def _pad_in_proj(w):
    s = [0, Q_A, 2 * Q_A, 2 * Q_A + V_A, 2 * Q_A + 2 * V_A]
    s += [s[-1] + GATE_RANK, s[-1] + 2 * GATE_RANK]
    s += [s[-1] + QK_B, s[-1] + 2 * QK_B, s[-1] + 2 * QK_B + V_B]
    gates = jnp.pad(w[:, s[4]:s[6]], ((0, 0), (0, GL_PAD - 2 * GATE_RANK)))
    return jnp.concatenate([w[:, s[0]:s[4]], gates, w[:, s[6]:s[9]]], axis=1).astype(BF16)


def kernel(x_prompt, x_sample, c, c_ctx, state_gla, cache_k, cache_v, ada_w, ada_b, norm_pre_mix, norm_post_mix, norm_pre_ffn, norm_post_ffn, ab_w_in, gla_w_g2, gla_b_g2, gla_norm_g, diff_lambda, diff_norm_g, ab_w_out, sgu_w_in, sgu_b_in, sgu_norm_g, sgu_w_s, sgu_b_s, sgu_w_out, moe_w_router, moe_e_bias, moe_w_gate, moe_w_up, moe_w_down, moe_ws_gate, moe_ws_up, moe_ws_down):
    depth = ada_w.shape[0]
    xp, xs = x_prompt.reshape(TP, D), x_sample.reshape(TS, D)
    cond = jnp.concatenate([c_ctx[None, :], c, jnp.zeros((8 - 1 - DEC_BATCH, D), F32)], axis=0)
    mod = _modulation(cond, ada_w, ada_b)
    cos, sin = _rope_tables()
    vec = lambda a: a.reshape(1, -1)
    new_s = new_k = new_v = None
    for l in range(depth):
        if l % 2 == 0:
            e = l // 2
            lam_init = 0.8 - 0.6 * math.exp(-0.3 * l)
            a, r_a, gl, q_b, k_b, v_b = _in_proj(xp, xs, mod, l, vec(norm_pre_mix[l]), _pad_in_proj(ab_w_in[e]), cos, sin)
            s0_t = jnp.swapaxes(state_gla[:, e], -1, -2)
            same_head = jnp.eye(H_A, dtype=bool)[None, None, :, None, :, None]
            s0_t = jnp.where(same_head, s0_t[:, :, :, :, None, :], 0.0).reshape(DEC_BATCH, 2, V_A, Q_A)
            o_f, o_bw, s_fin_t = _gla(a, gl, gla_w_g2[e], gla_b_g2[e].reshape(2, 1, Q_A), s0_t)
            o_att = _diff_attention(q_b, k_b, v_b, cache_k, cache_v, diff_lambda[e], lam_init)
            x = _mix_out(lam_init, o_f, o_bw, r_a, o_att, xp, xs, mod, l, vec(gla_norm_g[e]), vec(diff_norm_g[e]),
                         ab_w_out[e].astype(BF16), vec(norm_post_mix[l]))
            new_s = jnp.swapaxes(s_fin_t, -1, -2)
            new_k = k_b[:TP].reshape(BATCH, SEQ, H_B, 2, DQK_B).transpose(0, 2, 3, 1, 4)
            new_v = v_b[:TP].reshape(BATCH, SEQ, H_B, DV_B).transpose(0, 2, 1, 3)
        else:
            o = l // 2
            x = _sgu(xp, xs, mod, l, vec(norm_pre_mix[l]), sgu_w_in[o].astype(BF16), vec(sgu_b_in[o]),
                     vec(sgu_norm_g[o]), sgu_w_s[o], sgu_b_s[o].T, sgu_w_out[o].astype(BF16),
                     vec(norm_post_mix[l]))
        xp, xs = _moe_layer(x, mod, l, vec(norm_pre_ffn[l]), vec(norm_post_ffn[l]), moe_w_router[l], moe_e_bias[l],
                            moe_w_gate, moe_w_up, moe_w_down, moe_ws_gate[l], moe_ws_up[l], moe_ws_down[l])
    y_prompt = xp.reshape(BATCH, SEQ, D)
    y_sample = xs.reshape(DEC_BATCH, DEC_SEQ, D)
    return (y_prompt, y_sample, new_s[:, None], new_k[:, None], new_v[:, None])
```

```python
import functools
import math

import jax
import jax.numpy as jnp
from jax import lax
from jax.experimental import pallas as pl
from jax.experimental.pallas import tpu as pltpu
from jax.experimental.pallas import tpu_sc as plsc

F32 = jnp.float32
BF16 = jnp.bfloat16
I32 = jnp.int32

D = 1024
BATCH, SEQ = 32, 256
DEC_BATCH, DEC_SEQ = 4, 2048
PAST_LEN = 256
GRID_W = 64
EPS = 1e-6
TP = BATCH * SEQ
TS = DEC_BATCH * DEC_SEQ
T = TP + TS
H_A, DK_A, DV_A = 4, 64, 128
Q_A, V_A = H_A * DK_A, H_A * DV_A
GATE_RANK, GATE_TAU, GLA_CHUNK = 16, 16.0, 64
H_B, DQK_B, DV_B = 4, 64, 128
QK_B, V_B = H_B * 2 * DQK_B, H_B * DV_B
ROPE_BASE = 10000.0
SGU_DIM, SGU_GROUPS, SGU_CHUNK = 1024, 4, 128
N_EXPERTS, TOP_K, N_GROUPS, TOPK_GROUPS = 64, 8, 8, 4
GROUP_SIZE = N_EXPERTS // N_GROUPS
D_EXPERT, D_SHARED = 256, 256
ROUTED_SCALE = 2.5

TM = 512
TM_SUB = 256
NPT = TP // TM
TILES_PER_DEC = DEC_SEQ // TM
SEG = 256
NSEG = T // SEG
NSEG_P = TP // SEG
SEG_PER_DEC = DEC_SEQ // SEG
TR = 512
TD = 512
GM = 512
GM_SUB = 256
NT_MAX = T * TOP_K // GM + N_EXPERTS
SP = NT_MAX * GM
GL_PAD = 128
VMEM_LIMIT = 56 * 1024 * 1024
NEG_INF = float("-inf")


def _bdot(a, b):
    return jnp.dot(a.astype(BF16), b.astype(BF16), preferred_element_type=F32)


def _bdot_nt(a, b):
    return lax.dot_general(a.astype(BF16), b.astype(BF16), (((1,), (1,)), ((), ())),
                           preferred_element_type=F32)


def _bdot_tn(a, b):
    return lax.dot_general(a.astype(BF16), b.astype(BF16), (((0,), (0,)), ((), ())),
                           preferred_element_type=F32)


def _split3(x):
    x1 = x.astype(BF16)
    r1 = x - x1.astype(F32)
    x2 = r1.astype(BF16)
    x3 = (r1 - x2.astype(F32)).astype(BF16)
    return x1, x2, x3


def _rms(x, g):
    return x * lax.rsqrt(jnp.mean(x * x, axis=-1, keepdims=True) + EPS) * g


def _silu(x):
    return x * jax.nn.sigmoid(x)


def _mod_row(i):
    return jnp.where(i < NPT, 0, 1 + (i - NPT) // TILES_PER_DEC)


def _params(sem, limit=VMEM_LIMIT):
    return pltpu.CompilerParams(dimension_semantics=sem, vmem_limit_bytes=limit)


def _mod_kernel(c_ref, w_ref, b_ref, o_ref):
    o_ref[0] = _bdot(_silu(c_ref[...]), w_ref[0]) + b_ref[0]


def _modulation(cond, ada_w, ada_b):
    depth = ada_w.shape[0]
    nj = 6
    out = pl.pallas_call(
        _mod_kernel,
        grid=(depth, nj),
        in_specs=[
            pl.BlockSpec((8, D), lambda l, j: (0, 0)),
            pl.BlockSpec((1, D, D), lambda l, j: (l, 0, j)),
            pl.BlockSpec((1, 1, D), lambda l, j: (l, 0, j)),
        ],
        out_specs=pl.BlockSpec((1, 8, D), lambda l, j: (l, 0, j)),
        out_shape=jax.ShapeDtypeStruct((depth, 8, 6 * D), F32),
        compiler_params=_params(("arbitrary", "arbitrary")),
        name="adaln_modulation",
    )(cond, ada_w, ada_b.reshape(depth, 1, 6 * D))
    return out.reshape(depth, 8, 6, D)


_C_A, _C_R, _C_GL, _C_Q, _C_K, _C_V, _C_END = 0, 1024, 1536, 1664, 2176, 2688, 3200


def _rope(x, cos, sin):
    lane = lax.broadcasted_iota(I32, x.shape, 1)
    first = (lane % 32) < 16
    n = x.shape[1]
    xr = jnp.where(first, -pltpu.roll(x, n - 16, 1), pltpu.roll(x, 16, 1))
    return x * cos + xr * sin


def _stream_specs():
    return [pl.BlockSpec((TM, D), lambda i: (jnp.minimum(i, NPT - 1), 0)),
            pl.BlockSpec((TM, D), lambda i: (jnp.maximum(i - NPT, 0), 0))]


def _stream_tile(xp_ref, xs_ref):
    return jnp.where(pl.program_id(0) < NPT, xp_ref[...], xs_ref[...])


def _in_kernel(xp_ref, xs_ref, mod_ref, g_ref, w_ref, cos_ref, sin_ref,
               a_ref, r_ref, gl_ref, q_ref, k_ref, v_ref):
    latent = pl.program_id(0) >= NPT
    m = mod_ref[0, 0]
    x = _stream_tile(xp_ref, xs_ref)
    for s in range(TM // TM_SUB):
        rows = slice(s * TM_SUB, (s + 1) * TM_SUB)
        h = _rms(x[rows], g_ref[...]) * (1.0 + m[1:2]) + m[0:1]
        hb = h.astype(BF16)

        def proj(c0, c1):
            return jnp.dot(hb, w_ref[:, c0:c1], preferred_element_type=F32)

        a_ref[rows, :] = proj(_C_A, _C_R)
        r_ref[rows, :] = proj(_C_R, _C_GL)
        gl_ref[rows, :] = proj(_C_GL, _C_Q)
        v_ref[rows, :] = proj(_C_V, _C_END)
        q = proj(_C_Q, _C_K)
        k = proj(_C_K, _C_V)
        cos = cos_ref[rows, :]
        sin = sin_ref[rows, :]
        q_ref[rows, :] = jnp.where(latent, _rope(q, cos, sin), q)
        k_ref[rows, :] = jnp.where(latent, _rope(k, cos, sin), k)


def _in_proj(xp, xs, mod, l, g, w_pad, cos, sin):
    tok = lambda width: pl.BlockSpec((TM, width), lambda i: (i, 0))
    rope_spec = pl.BlockSpec((TM, QK_B), lambda i: (jnp.maximum(i - NPT, 0) % TILES_PER_DEC, 0))
    widths = (1024, 512, GL_PAD, 512, 512, 512)
    return pl.pallas_call(
        _in_kernel,
        grid=(T // TM,),
        in_specs=_stream_specs() + [
            pl.BlockSpec((1, 1, 6, D), lambda i: (l, _mod_row(i), 0, 0)),
            pl.BlockSpec((1, D), lambda i: (0, 0)),
            pl.BlockSpec((D, _C_END), lambda i: (0, 0)),
            rope_spec, rope_spec,
        ],
        out_specs=[tok(w) for w in widths],
        out_shape=[jax.ShapeDtypeStruct((T, w), F32) for w in widths],
        compiler_params=_params(("arbitrary",)),
        name="mixer_ab_in_proj",
    )(xp, xs, mod, g, w_pad, cos, sin)


def _log_sigmoid(x):
    return jnp.minimum(x, 0.0) - jnp.log(1.0 + jnp.exp(-jnp.abs(x)))


def _gla_kernel(af_ref, ab_ref, glf_ref, glb_ref, wg_ref, bg_ref, s0_ref,
                of_ref, ob_ref, sfin_ref, st_ref):
    i = pl.program_id(0)

    @pl.when(i < NSEG_P)
    def _():
        st_ref[...] = jnp.zeros_like(st_ref)

    @pl.when(jnp.logical_and(i >= NSEG_P, (i - NSEG_P) % SEG_PER_DEC == 0))
    def _():
        st_ref[...] = s0_ref[0]

    r = lax.broadcasted_iota(I32, (SEG, SEG), 0)
    c = lax.broadcasted_iota(I32, (SEG, SEG), 1)
    same = (r // GLA_CHUNK) == (c // GLA_CHUNK)
    nchunk = SEG // GLA_CHUNK
    own_head = (lax.broadcasted_iota(I32, (V_A, Q_A), 0) // DV_A) == (lax.broadcasted_iota(I32, (V_A, Q_A), 1) // DK_A)

    for d, (a_ref, gl_ref, o_ref) in enumerate(((af_ref, glf_ref, of_ref), (ab_ref, glb_ref, ob_ref))):
        fwd = d == 0
        gcol = gl_ref[:, d * GATE_RANK:(d + 1) * GATE_RANK]
        la = _log_sigmoid(_bdot(gcol, wg_ref[d]) + bg_ref[d]) / GATE_TAU
        causal = jnp.logical_and(same, (c <= r) if fwd else (c >= r))
        tri = jnp.where(causal, 1.0, 0.0).astype(BF16)
        l1, l2, l3 = _split3(la)
        b_all = (jnp.dot(tri, l1, preferred_element_type=F32)
                 + jnp.dot(tri, l2, preferred_element_type=F32)
                 + jnp.dot(tri, l3, preferred_element_type=F32))
        q_in_all = a_ref[:, 0:Q_A] * (DK_A ** -0.5) * jnp.exp(b_all)
        kd_all = a_ref[:, Q_A:2 * Q_A] * jnp.exp(-b_all)
        intra = []
        for h in range(H_A):
            kc = slice(h * DK_A, (h + 1) * DK_A)
            attn = jnp.where(causal, _bdot_nt(q_in_all[:, kc], kd_all[:, kc]), 0.0)
            intra.append(_bdot(attn, a_ref[:, 2 * Q_A + h * DV_A:2 * Q_A + (h + 1) * DV_A]))
        intra = jnp.concatenate(intra, axis=1)
        state = st_ref[d]
        order = range(nchunk) if fwd else range(nchunk - 1, -1, -1)
        for ch in order:
            r0 = ch * GLA_CHUNK
            rows = slice(r0, r0 + GLA_CHUNK)
            end = r0 + GLA_CHUNK - 1 if fwd else r0
            b_end = b_all[end:end + 1, :]
            kw = a_ref[rows, Q_A:2 * Q_A] * jnp.exp(b_end - b_all[rows, :])
            o_ref[rows, :] = intra[rows, :] + _bdot_nt(q_in_all[rows, :], state)
            kv_t = _bdot_tn(a_ref[rows, 2 * Q_A:2 * Q_A + V_A], kw)
            state = state * jnp.exp(b_end) + jnp.where(own_head, kv_t, 0.0)
        st_ref[d] = state

    @pl.when(i < NSEG_P)
    def _():
        for d in range(2):
            for h in range(H_A):
                sfin_ref[0, d, h] = st_ref[d, h * DV_A:(h + 1) * DV_A, h * DK_A:(h + 1) * DK_A]


def _seg_bwd(i):
    j = i - NSEG_P
    return jnp.where(i < NSEG_P, i, NSEG_P + (j // SEG_PER_DEC) * SEG_PER_DEC + (SEG_PER_DEC - 1 - j % SEG_PER_DEC))


def _gla(a, gl, wg, bg, s0_t):
    seg = lambda width, f: pl.BlockSpec((SEG, width), lambda i: (f(i), 0))
    ident = lambda i: i
    st_block = (1, 2, H_A, DV_A, DK_A)
    return pl.pallas_call(
        _gla_kernel,
        grid=(NSEG,),
        in_specs=[
            seg(D, ident), seg(D, _seg_bwd), seg(GL_PAD, ident), seg(GL_PAD, _seg_bwd),
            pl.BlockSpec((2, GATE_RANK, Q_A), lambda i: (0, 0, 0)),
            pl.BlockSpec((2, 1, Q_A), lambda i: (0, 0, 0)),
            pl.BlockSpec((1, 2, V_A, Q_A), lambda i: (jnp.maximum(i - NSEG_P, 0) // SEG_PER_DEC, 0, 0, 0)),
        ],
        out_specs=[
            seg(V_A, ident), seg(V_A, _seg_bwd),
            pl.BlockSpec(st_block, lambda i: (jnp.minimum(i, NSEG_P - 1), 0, 0, 0, 0)),
        ],
        out_shape=[
            jax.ShapeDtypeStruct((T, V_A), F32),
            jax.ShapeDtypeStruct((T, V_A), F32),
            jax.ShapeDtypeStruct((BATCH, 2, H_A, DV_A, DK_A), F32),
        ],
        scratch_shapes=[pltpu.VMEM((2, V_A, Q_A), F32)],
        compiler_params=_params(("arbitrary",)),
        name="gla_bidir",
    )(a, a, gl, gl, wg, bg, s0_t)


def _diff_lambda(lam_ref, lam_init):
    lp = lam_ref[...]
    s01 = jnp.sum(lp[0:1] * lp[1:2], axis=1, keepdims=True)
    s23 = jnp.sum(lp[2:3] * lp[3:4], axis=1, keepdims=True)
    return jnp.exp(s01) - jnp.exp(s23) + lam_init


def _attn_prompt_kernel(lam_init, q_ref, k_ref, v_ref, lam_ref, o_ref):
    lam = _diff_lambda(lam_ref, lam_init)
    for h in range(H_B):
        ps = []
        for m in range(2):
            cols = slice((2 * h + m) * DQK_B, (2 * h + m + 1) * DQK_B)
            s = _bdot_nt(q_ref[:, cols] * (DQK_B ** -0.5), k_ref[:, cols])
            e = jnp.exp(s - jnp.max(s, axis=1, keepdims=True))
            ps.append(e * (1.0 / jnp.sum(e, axis=1, keepdims=True)))
        w = ps[0] - lam * ps[1]
        o_ref[:, h * DV_B:(h + 1) * DV_B] = _bdot(w, v_ref[:, h * DV_B:(h + 1) * DV_B])


def _attn_sample_kernel(lam_init, q_ref, k_ref, v_ref, ck_ref, cv_ref, lam_ref, o_ref):
    lam = _diff_lambda(lam_ref, lam_init)
    for h in range(H_B):
        parts = []
        for m in range(2):
            cols = slice((2 * h + m) * DQK_B, (2 * h + m + 1) * DQK_B)
            q = q_ref[:, cols] * (DQK_B ** -0.5)
            sc = _bdot_nt(q, ck_ref[0, 0, h, m])
            sn = _bdot_nt(q, k_ref[:, cols])
            mx = jnp.maximum(jnp.max(sc, axis=1, keepdims=True), jnp.max(sn, axis=1, keepdims=True))
            ec = jnp.exp(sc - mx)
            en = jnp.exp(sn - mx)
            inv = (1.0 if m == 0 else -lam) / (jnp.sum(ec, axis=1, keepdims=True) + jnp.sum(en, axis=1, keepdims=True))
            parts.append((ec * inv, en * inv))
        wc = parts[0][0] + parts[1][0]
        wn = parts[0][1] + parts[1][1]
        o_ref[:, h * DV_B:(h + 1) * DV_B] = (_bdot(wc, cv_ref[0, 0, h])
                                             + _bdot(wn, v_ref[:, h * DV_B:(h + 1) * DV_B]))


QB = SEQ
NQB_DEC = DEC_SEQ // QB


def _attn_kernel(lam_init, q_ref, kp_ref, vp_ref, ks_ref, vs_ref, ck_ref, cv_ref, lam_ref, o_ref):
    i = pl.program_id(0)

    @pl.when(i < BATCH)
    def _():
        _attn_prompt_kernel(lam_init, q_ref, kp_ref, vp_ref, lam_ref, o_ref)

    @pl.when(i >= BATCH)
    def _():
        _attn_sample_kernel(lam_init, q_ref, ks_ref, vs_ref, ck_ref, cv_ref, lam_ref, o_ref)


def _diff_attention(q, k, v, cache_k, cache_v, lam_p, lam_init):
    blk = lambda rows, f: pl.BlockSpec((rows, 512), f)
    dec_b = lambda i: jnp.maximum(i - BATCH, 0) // NQB_DEC
    own = lambda i: (i, 0)
    prompt_kv = lambda i: (jnp.minimum(i, BATCH - 1), 0)
    dec_kv = lambda i: (TP // DEC_SEQ + dec_b(i), 0)
    return pl.pallas_call(
        functools.partial(_attn_kernel, lam_init),
        grid=(BATCH + DEC_BATCH * NQB_DEC,),
        in_specs=[
            blk(QB, own), blk(SEQ, prompt_kv), blk(SEQ, prompt_kv), blk(DEC_SEQ, dec_kv), blk(DEC_SEQ, dec_kv),
            pl.BlockSpec((1, 1, H_B, 2, PAST_LEN, DQK_B), lambda i: (dec_b(i), 0, 0, 0, 0, 0)),
            pl.BlockSpec((1, 1, H_B, PAST_LEN, DV_B), lambda i: (dec_b(i), 0, 0, 0, 0)),
            pl.BlockSpec((4, DQK_B), lambda i: (0, 0)),
        ],
        out_specs=blk(QB, own),
        out_shape=jax.ShapeDtypeStruct((T, V_B), F32),
        compiler_params=_params(("arbitrary",)),
        name="diff_attention",
    )(q, k, v, k, v, cache_k, cache_v, lam_p)


def _head_rms(x, g, nheads, width):
    return jnp.concatenate([_rms(x[:, h * width:(h + 1) * width], g) for h in range(nheads)], axis=1)


def _mix_out_kernel(lam_init, of_ref, ob_ref, r_ref, oatt_ref, xp_ref, xs_ref, mod_ref,
                    gg_ref, dg_ref, wo_ref, gp_ref, o_ref):
    m = mod_ref[0, 0]
    o_a = _head_rms(of_ref[...] + ob_ref[...], gg_ref[...], H_A, DV_A) * _silu(r_ref[...])
    o_b = _head_rms(oatt_ref[...], dg_ref[...], H_B, DV_B) * (1.0 - lam_init)
    out = _bdot(o_a, wo_ref[0:V_A, :]) + _bdot(o_b, wo_ref[V_A:V_A + V_B, :])
    o_ref[...] = _stream_tile(xp_ref, xs_ref) + m[2:3] * _rms(out, gp_ref[...])


def _mix_out(lam_init, o_f, o_b, r_a, o_att, xp, xs, mod, l, gla_g, diff_g, w_o, g_post):
    tok = lambda width: pl.BlockSpec((TM, width), lambda i: (i, 0))
    vec = lambda width: pl.BlockSpec((1, width), lambda i: (0, 0))
    return pl.pallas_call(
        functools.partial(_mix_out_kernel, lam_init),
        grid=(T // TM,),
        in_specs=[
            tok(512), tok(512), tok(512), tok(512), *_stream_specs(),
            pl.BlockSpec((1, 1, 6, D), lambda i: (l, _mod_row(i), 0, 0)),
            vec(DV_A), vec(DV_B),
            pl.BlockSpec((V_A + V_B, D), lambda i: (0, 0)),
            vec(D),
        ],
        out_specs=tok(D),
        out_shape=jax.ShapeDtypeStruct((T, D), F32),
        compiler_params=_params(("arbitrary",)),
        name="mixer_ab_out",
    )(o_f, o_b, r_a, o_att, xp, xs, mod, gla_g, diff_g, w_o, g_post)


def _gelu_tanh(x):
    return 0.5 * x * (1.0 + jnp.tanh(math.sqrt(2.0 / math.pi) * (x + 0.044715 * (x * x * x))))


def _sgu_kernel(xp_ref, xs_ref, mod_ref, gpre_ref, win_ref, bin_ref, vg_ref, ws_ref, bs_ref,
                wout_ref, gpost_ref, o_ref, t_ref):
    m = mod_ref[0, 0]
    x = _stream_tile(xp_ref, xs_ref)
    h = _rms(x, gpre_ref[...]) * (1.0 + m[1:2]) + m[0:1]
    z = _gelu_tanh(_bdot(h, win_ref[...]) + bin_ref[...])
    v = _rms(z[:, SGU_DIM:], vg_ref[...])
    gw = SGU_DIM // SGU_GROUPS
    for ch in range(TM // SGU_CHUNK):
        rows = slice(ch * SGU_CHUNK, (ch + 1) * SGU_CHUNK)
        for g in range(SGU_GROUPS):
            cols = slice(g * gw, (g + 1) * gw)
            vs = _bdot(ws_ref[g], v[rows, cols]) + bs_ref[:, g:g + 1]
            t_ref[rows, cols] = (z[rows, cols] * vs).astype(BF16)
    out = jnp.dot(t_ref[...], wout_ref[...], preferred_element_type=F32)
    o_ref[...] = x + m[2:3] * _rms(out, gpost_ref[...])


def _sgu(xp, xs, mod, l, g_pre, w_in, b_in, v_g, w_s, b_s_t, w_out, g_post):
    tok = pl.BlockSpec((TM, D), lambda i: (i, 0))
    full = lambda *shape: pl.BlockSpec(shape, lambda i: (0,) * len(shape))
    return pl.pallas_call(
        _sgu_kernel,
        grid=(T // TM,),
        in_specs=_stream_specs() + [
            pl.BlockSpec((1, 1, 6, D), lambda i: (l, _mod_row(i), 0, 0)),
            full(1, D), full(D, 2 * SGU_DIM), full(1, 2 * SGU_DIM), full(1, SGU_DIM),
            full(SGU_GROUPS, SGU_CHUNK, SGU_CHUNK), full(SGU_CHUNK, SGU_GROUPS),
            full(SGU_DIM, D), full(1, D),
        ],
        out_specs=tok,
        out_shape=jax.ShapeDtypeStruct((T, D), F32),
        scratch_shapes=[pltpu.VMEM((TM, SGU_DIM), BF16)],
        compiler_params=_params(("arbitrary",)),
        name="sgu_mixer",
    )(xp, xs, mod, g_pre, w_in, b_in, v_g, w_s, b_s_t, w_out, g_post)


LANES = 128
U32 = jnp.uint32
PACKED = D // 2
ROWS_PER_TOKEN = PACKED // LANES
HIGH_HALF = 0xFFFF0000


def _pack_rows(x):
    bits = lax.bitcast_convert_type(x.astype(BF16).astype(F32), U32)
    return bits[:, :PACKED] | (bits[:, PACKED:] >> 16)


def _unpack_rows(u):
    return (lax.bitcast_convert_type(u & U32(HIGH_HALF), F32), lax.bitcast_convert_type(u << 16, F32))


def _store_token_tiles(ref, u, first=0):
    n = u.shape[0]
    for c in range(ROWS_PER_TOKEN):
        ref[pl.ds(first * ROWS_PER_TOKEN + c, n, stride=ROWS_PER_TOKEN), :] = u[:, c * LANES:(c + 1) * LANES]


def _load_token_tiles(ref, n, first=0):
    return jnp.concatenate([ref[pl.ds(first * ROWS_PER_TOKEN + c, n, stride=ROWS_PER_TOKEN), :]
                            for c in range(ROWS_PER_TOKEN)], axis=1)


def _router_kernel(x_ref, mod_ref, g_ref, wr_ref, eb_ref,
                   h_ref, te_ref, wn_ref, rk_ref, cnt_ref, carry_ref):
    i = pl.program_id(0)

    @pl.when(i == 0)
    def _():
        carry_ref[...] = jnp.zeros_like(carry_ref)

    m = mod_ref[0, 0]
    h = _rms(x_ref[...], g_ref[...]) * (1.0 + m[4:5]) + m[3:4]
    _store_token_tiles(h_ref, _pack_rows(h))
    h1, h2, _ = _split3(h)
    w1, w2, _ = _split3(wr_ref[...])
    nt = lambda a, b: lax.dot_general(a, b, (((1,), (1,)), ((), ())), preferred_element_type=F32)
    logits = nt(w1, h1) + nt(w1, h2) + nt(w2, h1)
    scores = jax.nn.sigmoid(logits)
    sel = scores + eb_ref[...]

    row8 = lax.broadcasted_iota(I32, (GROUP_SIZE, TR), 0)
    gscore = []
    for g in range(N_GROUPS):
        xg = sel[g * GROUP_SIZE:(g + 1) * GROUP_SIZE]
        m1 = jnp.max(xg, axis=0, keepdims=True)
        i1 = jnp.min(jnp.where(xg == m1, row8, GROUP_SIZE), axis=0, keepdims=True)
        m2 = jnp.max(jnp.where(row8 == i1, NEG_INF, xg), axis=0, keepdims=True)
        gscore.append(m1 + m2)
    pieces = []
    for g in range(N_GROUPS):
        rank = jnp.zeros((1, TR), I32)
        for g2 in range(N_GROUPS):
            if g2 == g:
                continue
            beats = (gscore[g2] >= gscore[g]) if g2 < g else (gscore[g2] > gscore[g])
            rank = rank + beats.astype(I32)
        pieces.append(jnp.where(rank < TOPK_GROUPS, sel[g * GROUP_SIZE:(g + 1) * GROUP_SIZE], NEG_INF))
    cur = jnp.concatenate(pieces, axis=0)

    row = lax.broadcasted_iota(I32, (N_EXPERTS, TR), 0)
    idxs, ws = [], []
    for _ in range(TOP_K):
        mx = jnp.max(cur, axis=0, keepdims=True)
        idx = jnp.min(jnp.where(cur == mx, row, N_EXPERTS), axis=0, keepdims=True)
        hit = row == idx
        ws.append(jnp.sum(jnp.where(hit, scores, 0.0), axis=0, keepdims=True))
        cur = jnp.where(hit, NEG_INF, cur)
        idxs.append(idx)
    mask = jnp.zeros((N_EXPERTS, TR), F32)
    for idx in idxs:
        mask = mask + (row == idx).astype(F32)
    wsum = ws[0]
    for wk in ws[1:]:
        wsum = wsum + wk

    tj = lax.broadcasted_iota(I32, (TR, TR), 0)
    ti = lax.broadcasted_iota(I32, (TR, TR), 1)
    upper = jnp.where(tj < ti, 1.0, 0.0).astype(BF16)
    pos = carry_ref[...] + jnp.dot(mask.astype(BF16), upper, preferred_element_type=F32)
    for k in range(TOP_K):
        hit = row == idxs[k]
        te_ref[k:k + 1, :] = idxs[k]
        wn_ref[k:k + 1, :] = ws[k] / wsum * ROUTED_SCALE
        rk_ref[k:k + 1, :] = jnp.sum(jnp.where(hit, pos, 0.0), axis=0, keepdims=True).astype(I32)
    carry_ref[...] = carry_ref[...] + jnp.sum(mask, axis=1, keepdims=True)
    cnt_ref[...] = carry_ref[...]


def _router(x, mod, l, g, wr_t, e_bias):
    kt = lambda dtype: jax.ShapeDtypeStruct((TOP_K, T), dtype)
    kt_spec = pl.BlockSpec((TOP_K, TR), lambda i: (0, i))
    tiles_per_dec = DEC_SEQ // TR
    mod_row = lambda i: jnp.where(i < TP // TR, 0, 1 + (i - TP // TR) // tiles_per_dec)
    return pl.pallas_call(
        _router_kernel,
        grid=(T // TR,),
        in_specs=[
            pl.BlockSpec((TR, D), lambda i: (i, 0)),
            pl.BlockSpec((1, 1, 6, D), lambda i: (l, mod_row(i), 0, 0)),
            pl.BlockSpec((1, D), lambda i: (0, 0)),
            pl.BlockSpec((N_EXPERTS, D), lambda i: (0, 0)),
            pl.BlockSpec((N_EXPERTS, 1), lambda i: (0, 0)),
        ],
        out_specs=[
            pl.BlockSpec((TR * ROWS_PER_TOKEN, LANES), lambda i: (i, 0)),
            kt_spec, kt_spec, kt_spec,
            pl.BlockSpec((N_EXPERTS, 1), lambda i: (0, 0)),
        ],
        out_shape=[
            jax.ShapeDtypeStruct((T * ROWS_PER_TOKEN, LANES), U32), kt(I32), kt(F32), kt(I32),
            jax.ShapeDtypeStruct((N_EXPERTS, 1), F32),
        ],
        scratch_shapes=[pltpu.VMEM((N_EXPERTS, 1), F32)],
        compiler_params=_params(("arbitrary",)),
        name="moe_router",
    )(x, mod, g, wr_t, e_bias)


_PAD_BITS = tuple(1 << b for b in range(GM.bit_length() - 1))


def _pad_fill_kernel(pad_start_ref, pad_len_ref, xg_in_ref, xg_ref, zero_ref, sem):
    del xg_in_ref
    zero_ref[...] = jnp.zeros_like(zero_ref)

    def pad_copies(e):
        start = pad_start_ref[e]
        n = pad_len_ref[e]
        copies = []
        for bit in _PAD_BITS:
            first = start + (n & ~(2 * bit - 1))
            copies.append(((n & bit) != 0, pltpu.make_async_copy(
                zero_ref.at[pl.ds(0, bit)], xg_ref.at[pl.ds(first, bit)], sem)))
        return copies

    def start_e(e, carry):
        for on, cp in pad_copies(e):
            @pl.when(on)
            def _():
                cp.start()
        return carry

    def wait_e(e, carry):
        for on, cp in pad_copies(e):
            @pl.when(on)
            def _():
                cp.wait()
        return carry

    lax.fori_loop(0, N_EXPERTS, start_e, 0)
    lax.fori_loop(0, N_EXPERTS, wait_e, 0)


def _pad_fill(pad_start, pad_len, xg):
    grid_spec = pltpu.PrefetchScalarGridSpec(
        num_scalar_prefetch=2,
        grid=(1,),
        in_specs=[pl.BlockSpec(memory_space=pl.ANY)],
        out_specs=pl.BlockSpec(memory_space=pl.ANY),
        scratch_shapes=[pltpu.VMEM((GM // 2, ROWS_PER_TOKEN, LANES), xg.dtype), pltpu.SemaphoreType.DMA],
    )
    return pl.pallas_call(
        _pad_fill_kernel,
        grid_spec=grid_spec,
        out_shape=jax.ShapeDtypeStruct(xg.shape, xg.dtype),
        input_output_aliases={2: 0},
        compiler_params=_params(("arbitrary",)),
        name="moe_pad_fill",
    )(pad_start, pad_len, xg)


SC_CORES, SC_SUBCORES = 2, 16
SC_WORKERS = SC_CORES * SC_SUBCORES
SC_W = 64


def _sc_worker_id():
    return lax.axis_index("s") * SC_CORES + lax.axis_index("c")


def _sc_dispatch(h3, slot3):
    nchunk = T // SC_WORKERS // SC_W
    mesh = plsc.VectorSubcoreMesh(core_axis_name="c", subcore_axis_name="s")
    tile = (SC_W, ROWS_PER_TOKEN, LANES)

    @functools.partial(
        pl.kernel, mesh=mesh,
        out_type=jax.ShapeDtypeStruct((SP, ROWS_PER_TOKEN, LANES), h3.dtype),
        scratch_types=[pltpu.VMEM((TOP_K, SC_W), I32), pltpu.VMEM((TOP_K, SC_W), I32),
                       pltpu.VMEM(tile, h3.dtype), pltpu.VMEM(tile, h3.dtype),
                       pltpu.SemaphoreType.DMA((2,)), pltpu.SemaphoreType.DMA((2,))],
    )
    def k(h_hbm, slot_hbm, xg_hbm, idx0, idx1, rows0, rows1, lsem, ssem):
        first = _sc_worker_id() * nchunk
        idx = (idx0, idx1)
        rows = (rows0, rows1)

        def loads(j, b):
            blk = first + j
            tok = pl.multiple_of(blk * SC_W, SC_W)
            return (pltpu.make_async_copy(slot_hbm.at[blk], idx[b], lsem.at[b]),
                    pltpu.make_async_copy(h_hbm.at[pl.ds(tok, SC_W)], rows[b], lsem.at[b]))

        def scatters(b):
            return [pltpu.make_async_copy(rows[b], xg_hbm.at[idx[b].at[kk]], ssem.at[b]) for kk in range(TOP_K)]

        for cp in loads(0, 0):
            cp.start()

        @pl.loop(0, nchunk, step=2)
        def _(j):
            for b in (0, 1):
                jj = j + b
                for cp in loads(jj, b):
                    cp.wait()
                for cp in scatters(b):
                    cp.start()

                @pl.when(jj + 1 < nchunk)
                def _():
                    @pl.when(jj >= 1)
                    def _():
                        for cp in scatters(1 - b):
                            cp.wait()
                    for cp in loads(jj + 1, 1 - b):
                        cp.start()

        for b in (0, 1):
            for cp in scatters(b):
                cp.wait()

    return k(h3, slot3)


def _gmm_kernel(tile_e_ref, tile_blk_ref, nvalid_ref, x_ref, wg_ref, wu_ref, wd_ref,
                y_ref, wgu_scr, wd_scr):
    j = pl.program_id(0)

    @pl.when(j < nvalid_ref[0])
    def _():
        prev = tile_e_ref[jnp.maximum(j - 1, 0)]

        @pl.when(jnp.logical_or(j == 0, tile_e_ref[j] != prev))
        def _():
            wgu_scr[:, 0:D_EXPERT] = wg_ref[0, 0].astype(BF16)
            wgu_scr[:, D_EXPERT:2 * D_EXPERT] = wu_ref[0, 0].astype(BF16)
            wd_scr[...] = wd_ref[0, 0].astype(BF16)

        for s in range(GM // GM_SUB):
            x_hi, x_lo = _unpack_rows(_load_token_tiles(x_ref, GM_SUB, s * GM_SUB))
            gu = (jnp.dot(x_hi.astype(BF16), wgu_scr[0:PACKED, :], preferred_element_type=F32)
                  + jnp.dot(x_lo.astype(BF16), wgu_scr[PACKED:D, :], preferred_element_type=F32))
            hid = _silu(gu[:, 0:D_EXPERT]) * gu[:, D_EXPERT:2 * D_EXPERT]
            y = jnp.dot(hid.astype(BF16), wd_scr[...], preferred_element_type=F32)
            _store_token_tiles(y_ref, _pack_rows(y), s * GM_SUB)


def _gmm(tile_e, tile_blk, nvalid, xg, l, w_gate, w_up, w_down):
    row_tile = pl.BlockSpec((GM * ROWS_PER_TOKEN, LANES), lambda j, te, tb, nv: (tb[j], 0))
    grid_spec = pltpu.PrefetchScalarGridSpec(
        num_scalar_prefetch=3,
        grid=(NT_MAX,),
        in_specs=[
            row_tile,
            pl.BlockSpec((1, 1, D, D_EXPERT), lambda j, te, tb, nv: (l, te[j], 0, 0)),
            pl.BlockSpec((1, 1, D, D_EXPERT), lambda j, te, tb, nv: (l, te[j], 0, 0)),
            pl.BlockSpec((1, 1, D_EXPERT, D), lambda j, te, tb, nv: (l, te[j], 0, 0)),
        ],
        out_specs=row_tile,
        scratch_shapes=[pltpu.VMEM((D, 2 * D_EXPERT), BF16), pltpu.VMEM((D_EXPERT, D), BF16)],
    )
    return pl.pallas_call(
        _gmm_kernel,
        grid_spec=grid_spec,
        out_shape=jax.ShapeDtypeStruct((SP * ROWS_PER_TOKEN, LANES), U32),
        compiler_params=_params(("arbitrary",)),
        name="moe_grouped_matmul",
    )(tile_e, tile_blk, nvalid, xg, w_gate, w_up, w_down)


def _sc_gather(table3, idx):
    n_idx = idx.shape[0]
    per_w = n_idx // SC_WORKERS
    nchunk = per_w // SC_W
    mesh = plsc.VectorSubcoreMesh(core_axis_name="c", subcore_axis_name="s")
    tile = (SC_W, ROWS_PER_TOKEN, LANES)

    @functools.partial(
        pl.kernel, mesh=mesh,
        out_type=jax.ShapeDtypeStruct((n_idx, ROWS_PER_TOKEN, LANES), table3.dtype),
        scratch_types=[pltpu.VMEM((per_w,), I32), pltpu.VMEM(tile, table3.dtype), pltpu.VMEM(tile, table3.dtype),
                       pltpu.SemaphoreType.DMA((2,)), pltpu.SemaphoreType.DMA((2,))],
    )
    def k(table_hbm, idx_hbm, out_hbm, idx_v, rows0, rows1, gsem, wsem):
        base = pl.multiple_of(_sc_worker_id() * per_w, per_w)
        rows = (rows0, rows1)
        pltpu.sync_copy(idx_hbm.at[pl.ds(base, per_w)], idx_v)

        def gather(j, b):
            ids = idx_v.at[pl.ds(pl.multiple_of(j * SC_W, SC_W), SC_W)]
            return pltpu.make_async_copy(table_hbm.at[ids], rows[b], gsem.at[b])

        def write(j, b):
            dst = out_hbm.at[pl.ds(pl.multiple_of(base + j * SC_W, SC_W), SC_W)]
            return pltpu.make_async_copy(rows[b], dst, wsem.at[b])

        gather(0, 0).start()

        @pl.loop(0, nchunk, step=2)
        def _(j):
            for b in (0, 1):
                jj = j + b
                gather(jj, b).wait()
                write(jj, b).start()

                @pl.when(jj + 1 < nchunk)
                def _():
                    @pl.when(jj >= 1)
                    def _():
                        write(jj - 1, 1 - b).wait()
                    gather(jj + 1, 1 - b).start()

        write(nchunk - 2, 0).wait()
        write(nchunk - 1, 1).wait()

    return k(table3, idx)


def _combine_kernel(wn_ref, x_ref, mod_ref, gpre_ref, gp_ref, wsg_ref, wsu_ref, wsd_ref, y_ref, o_ref):
    m = mod_ref[0, 0]
    x = x_ref[...]
    hb = (_rms(x, gpre_ref[...]) * (1.0 + m[4:5]) + m[3:4]).astype(BF16)
    hid = (_silu(jnp.dot(hb, wsg_ref[...], preferred_element_type=F32))
           * jnp.dot(hb, wsu_ref[...], preferred_element_type=F32))
    acc = jnp.dot(hid.astype(BF16), wsd_ref[...], preferred_element_type=F32)

    r = lax.broadcasted_iota(I32, (TD, TD), 0)
    c = lax.broadcasted_iota(I32, (TD, TD), 1)
    eye = jnp.where(r == c, 1.0, 0.0).astype(BF16)
    nt = lambda a, b: lax.dot_general(a, b, (((1,), (1,)), ((), ())), preferred_element_type=F32)
    w1, w2, w3 = _split3(wn_ref[...])
    w_t = nt(eye, w1) + nt(eye, w2) + nt(eye, w3)

    acc_hi = acc[:, :PACKED]
    acc_lo = acc[:, PACKED:]
    for k in range(TOP_K):
        y_hi, y_lo = _unpack_rows(_load_token_tiles(y_ref, TD, k * TD))
        acc_hi = acc_hi + y_hi * w_t[:, k:k + 1]
        acc_lo = acc_lo + y_lo * w_t[:, k:k + 1]
    acc = jnp.concatenate([acc_hi, acc_lo], axis=1)
    o_ref[...] = x + m[5:6] * _rms(acc, gp_ref[...])


def _combine(wn, x, mod, l, g_pre, g_post, ws_gate, ws_up, ws_down, ybuf, first_tok, n_tok):
    off = first_tok // TD
    tiles_per_dec = DEC_SEQ // TD
    npd = TP // TD
    mod_row = lambda i: jnp.where(i + off < npd, 0, 1 + (i + off - npd) // tiles_per_dec)
    full = lambda *shape: pl.BlockSpec(shape, lambda i: (0,) * len(shape))
    y_spec = pl.BlockSpec((TOP_K * TD * ROWS_PER_TOKEN, LANES), lambda i: (i, 0))
    return pl.pallas_call(
        _combine_kernel,
        grid=(n_tok // TD,),
        in_specs=[
            pl.BlockSpec((TOP_K, TD), lambda i: (0, i + off)),
            pl.BlockSpec((TD, D), lambda i: (i + off, 0)),
            pl.BlockSpec((1, 1, 6, D), lambda i: (l, mod_row(i), 0, 0)),
            full(1, D), full(1, D), full(D, D_SHARED), full(D, D_SHARED), full(D_SHARED, D),
            y_spec,
        ],
        out_specs=pl.BlockSpec((TD, D), lambda i: (i, 0)),
        out_shape=jax.ShapeDtypeStruct((n_tok, D), F32),
        compiler_params=_params(("arbitrary",)),
        name="moe_combine",
    )(wn, x, mod, g_pre, g_post, ws_gate, ws_up, ws_down, ybuf)


def _moe_layer(x, mod, l, g_pre, g_post, w_router, e_bias, w_gate, w_up, w_down,
               ws_gate, ws_up, ws_down):
    h, top_e, wn, rk, cnt = _router(x, mod, l, g_pre, w_router.T, e_bias.reshape(N_EXPERTS, 1))
    cnt = cnt.reshape(N_EXPERTS).astype(I32)
    padded = (cnt + GM - 1) // GM * GM
    ends = jnp.cumsum(padded)
    offs = ends - padded
    eid = jnp.arange(N_EXPERTS, dtype=I32)[:, None, None]
    slot = rk + jnp.sum(jnp.where(top_e[None] == eid, offs[:, None, None], 0), axis=0)
    nvalid = ends[-1] // GM
    tile_start = jnp.arange(NT_MAX, dtype=I32) * GM
    tile_raw = jnp.sum((tile_start[:, None] >= ends[None, :]).astype(I32), axis=1)
    last = jnp.maximum(nvalid - 1, 0)
    tile_blk = jnp.minimum(jnp.arange(NT_MAX, dtype=I32), last)
    tile_e = jnp.minimum(tile_raw, N_EXPERTS - 1)
    tile_e = jnp.where(jnp.arange(NT_MAX) <= last, tile_e, tile_e[last])
    slot3 = slot.reshape(TOP_K, T // SC_W, SC_W).transpose(1, 0, 2)
    xg = _sc_dispatch(h.reshape(T, ROWS_PER_TOKEN, LANES), slot3)
    xg = _pad_fill(offs + cnt, padded - cnt, xg).reshape(SP * ROWS_PER_TOKEN, LANES)
    yg = _gmm(tile_e, tile_blk, nvalid.reshape(1), xg, l, w_gate, w_up, w_down)
    yg3 = yg.reshape(SP, ROWS_PER_TOKEN, LANES)
    ws = (ws_gate.astype(BF16), ws_up.astype(BF16), ws_down.astype(BF16))
    outs = []
    for first_tok, n_tok in ((0, TP), (TP, TS)):
        ids = slot[:, first_tok:first_tok + n_tok].reshape(TOP_K, n_tok // TD, TD).transpose(1, 0, 2)
        ybuf = _sc_gather(yg3, ids.reshape(TOP_K * n_tok))
        outs.append(_combine(wn, x, mod, l, g_pre, g_post, *ws,
                             ybuf.reshape(TOP_K * n_tok * ROWS_PER_TOKEN, LANES), first_tok, n_tok))
    return outs


def _rope_tables():
    n = DEC_SEQ
    rows = n // GRID_W
    row = jnp.repeat(jnp.arange(rows), GRID_W).astype(F32)
    col = jnp.tile(jnp.arange(GRID_W), rows).astype(F32)
    half = DQK_B // 2
    inv = ROPE_BASE ** (-jnp.arange(0, half, 2, dtype=F32) / half)
    ang_r = row[:, None] * inv
    ang_c = col[:, None] * inv
    ang = jnp.concatenate([ang_r, ang_r, ang_c, ang_c], axis=-1)
    reps = QK_B // DQK_B
    return jnp.tile(jnp.cos(ang), (1, reps)), jnp.tile(jnp.sin(ang), (1, reps))


def _pad_in_proj(w):
    s = [0, Q_A, 2 * Q_A, 2 * Q_A + V_A, 2 * Q_A + 2 * V_A]
    s += [s[-1] + GATE_RANK, s[-1] + 2 * GATE_RANK]
    s += [s[-1] + QK_B, s[-1] + 2 * QK_B, s[-1] + 2 * QK_B + V_B]
    gates = jnp.pad(w[:, s[4]:s[6]], ((0, 0), (0, GL_PAD - 2 * GATE_RANK)))
    return jnp.concatenate([w[:, s[0]:s[4]], gates, w[:, s[6]:s[9]]], axis=1).astype(BF16)


def kernel(x_prompt, x_sample, c, c_ctx, state_gla, cache_k, cache_v, ada_w, ada_b, norm_pre_mix, norm_post_mix, norm_pre_ffn, norm_post_ffn, ab_w_in, gla_w_g2, gla_b_g2, gla_norm_g, diff_lambda, diff_norm_g, ab_w_out, sgu_w_in, sgu_b_in, sgu_norm_g, sgu_w_s, sgu_b_s, sgu_w_out, moe_w_router, moe_e_bias, moe_w_gate, moe_w_up, moe_w_down, moe_ws_gate, moe_ws_up, moe_ws_down):
    depth = ada_w.shape[0]
    xp, xs = x_prompt.reshape(TP, D), x_sample.reshape(TS, D)
    cond = jnp.concatenate([c_ctx[None, :], c, jnp.zeros((8 - 1 - DEC_BATCH, D), F32)], axis=0)
    mod = _modulation(cond, ada_w, ada_b)
    cos, sin = _rope_tables()
    vec = lambda a: a.reshape(1, -1)
    new_s = new_k = new_v = None
    for l in range(depth):
        if l % 2 == 0:
            e = l // 2
            lam_init = 0.8 - 0.6 * math.exp(-0.3 * l)
            a, r_a, gl, q_b, k_b, v_b = _in_proj(xp, xs, mod, l, vec(norm_pre_mix[l]), _pad_in_proj(ab_w_in[e]), cos, sin)
            s0_t = jnp.swapaxes(state_gla[:, e], -1, -2)
            same_head = jnp.eye(H_A, dtype=bool)[None, None, :, None, :, None]
            s0_t = jnp.where(same_head, s0_t[:, :, :, :, None, :], 0.0).reshape(DEC_BATCH, 2, V_A, Q_A)
            o_f, o_bw, s_fin_t = _gla(a, gl, gla_w_g2[e], gla_b_g2[e].reshape(2, 1, Q_A), s0_t)
            o_att = _diff_attention(q_b, k_b, v_b, cache_k, cache_v, diff_lambda[e], lam_init)
            x = _mix_out(lam_init, o_f, o_bw, r_a, o_att, xp, xs, mod, l, vec(gla_norm_g[e]), vec(diff_norm_g[e]),
                         ab_w_out[e].astype(BF16), vec(norm_post_mix[l]))
            new_s = jnp.swapaxes(s_fin_t, -1, -2)
            new_k = k_b[:TP].reshape(BATCH, SEQ, H_B, 2, DQK_B).transpose(0, 2, 3, 1, 4)
            new_v = v_b[:TP].reshape(BATCH, SEQ, H_B, DV_B).transpose(0, 2, 1, 3)
        else:
            o = l // 2
            x = _sgu(xp, xs, mod, l, vec(norm_pre_mix[l]), sgu_w_in[o].astype(BF16), vec(sgu_b_in[o]),
                     vec(sgu_norm_g[o]), sgu_w_s[o], sgu_b_s[o].T, sgu_w_out[o].astype(BF16),
                     vec(norm_post_mix[l]))
        xp, xs = _moe_layer(x, mod, l, vec(norm_pre_ffn[l]), vec(norm_post_ffn[l]), moe_w_router[l], moe_e_bias[l],
                            moe_w_gate, moe_w_up, moe_w_down, moe_ws_gate[l], moe_ws_up[l], moe_ws_down[l])
    y_prompt = xp.reshape(BATCH, SEQ, D)
    y_sample = xs.reshape(DEC_BATCH, DEC_SEQ, D)
    return (y_prompt, y_sample, new_s[:, None], new_k[:, None], new_v[:, None])
```

```python
import functools
import math

import jax
import jax.numpy as jnp
from jax import lax
from jax.experimental import pallas as pl
from jax.experimental.pallas import tpu as pltpu
from jax.experimental.pallas import tpu_sc as plsc

F32 = jnp.float32
BF16 = jnp.bfloat16
I32 = jnp.int32

D = 1024
BATCH, SEQ = 32, 256
DEC_BATCH, DEC_SEQ = 4, 2048
PAST_LEN = 256
GRID_W = 64
EPS = 1e-6
TP = BATCH * SEQ
TS = DEC_BATCH * DEC_SEQ
T = TP + TS
H_A, DK_A, DV_A = 4, 64, 128
Q_A, V_A = H_A * DK_A, H_A * DV_A
GATE_RANK, GATE_TAU, GLA_CHUNK = 16, 16.0, 64
H_B, DQK_B, DV_B = 4, 64, 128
QK_B, V_B = H_B * 2 * DQK_B, H_B * DV_B
ROPE_BASE = 10000.0
SGU_DIM, SGU_GROUPS, SGU_CHUNK = 1024, 4, 128
N_EXPERTS, TOP_K, N_GROUPS, TOPK_GROUPS = 64, 8, 8, 4
GROUP_SIZE = N_EXPERTS // N_GROUPS
D_EXPERT, D_SHARED = 256, 256
ROUTED_SCALE = 2.5

TM = 512
TM_SUB = 256
NPT = TP // TM
TILES_PER_DEC = DEC_SEQ // TM
SEG = 256
NSEG = T // SEG
NSEG_P = TP // SEG
SEG_PER_DEC = DEC_SEQ // SEG
TR = 512
TD = 512
GM = 1024
GM_SUB = 256
NT_MAX = T * TOP_K // GM + N_EXPERTS
SP = NT_MAX * GM
GL_PAD = 128
VMEM_LIMIT = 56 * 1024 * 1024
NEG_INF = float("-inf")


def _bdot(a, b):
    return jnp.dot(a.astype(BF16), b.astype(BF16), preferred_element_type=F32)


def _bdot_nt(a, b):
    return lax.dot_general(a.astype(BF16), b.astype(BF16), (((1,), (1,)), ((), ())),
                           preferred_element_type=F32)


def _bdot_tn(a, b):
    return lax.dot_general(a.astype(BF16), b.astype(BF16), (((0,), (0,)), ((), ())),
                           preferred_element_type=F32)


def _split3(x):
    x1 = x.astype(BF16)
    r1 = x - x1.astype(F32)
    x2 = r1.astype(BF16)
    x3 = (r1 - x2.astype(F32)).astype(BF16)
    return x1, x2, x3


def _rms(x, g):
    return x * lax.rsqrt(jnp.mean(x * x, axis=-1, keepdims=True) + EPS) * g


def _silu(x):
    return x * jax.nn.sigmoid(x)


def _mod_row(i):
    return jnp.where(i < NPT, 0, 1 + (i - NPT) // TILES_PER_DEC)


def _params(sem, limit=VMEM_LIMIT):
    return pltpu.CompilerParams(dimension_semantics=sem, vmem_limit_bytes=limit)


def _mod_kernel(c_ref, w_ref, b_ref, o_ref):
    o_ref[0] = _bdot(_silu(c_ref[...]), w_ref[0]) + b_ref[0]


def _modulation(cond, ada_w, ada_b):
    depth = ada_w.shape[0]
    nj = 6
    out = pl.pallas_call(
        _mod_kernel,
        grid=(depth, nj),
        in_specs=[
            pl.BlockSpec((8, D), lambda l, j: (0, 0)),
            pl.BlockSpec((1, D, D), lambda l, j: (l, 0, j)),
            pl.BlockSpec((1, 1, D), lambda l, j: (l, 0, j)),
        ],
        out_specs=pl.BlockSpec((1, 8, D), lambda l, j: (l, 0, j)),
        out_shape=jax.ShapeDtypeStruct((depth, 8, 6 * D), F32),
        compiler_params=_params(("arbitrary", "arbitrary")),
        name="adaln_modulation",
    )(cond, ada_w, ada_b.reshape(depth, 1, 6 * D))
    return out.reshape(depth, 8, 6, D)


_C_A, _C_R, _C_GL, _C_Q, _C_K, _C_V, _C_END = 0, 1024, 1536, 1664, 2176, 2688, 3200


def _rope(x, cos, sin):
    lane = lax.broadcasted_iota(I32, x.shape, 1)
    first = (lane % 32) < 16
    n = x.shape[1]
    xr = jnp.where(first, -pltpu.roll(x, n - 16, 1), pltpu.roll(x, 16, 1))
    return x * cos + xr * sin


def _stream_specs():
    return [pl.BlockSpec((TM, D), lambda i: (jnp.minimum(i, NPT - 1), 0)),
            pl.BlockSpec((TM, D), lambda i: (jnp.maximum(i - NPT, 0), 0))]


def _stream_tile(xp_ref, xs_ref):
    return jnp.where(pl.program_id(0) < NPT, xp_ref[...], xs_ref[...])


def _in_kernel(xp_ref, xs_ref, mod_ref, g_ref, w_ref, cos_ref, sin_ref,
               a_ref, r_ref, gl_ref, q_ref, k_ref, v_ref):
    latent = pl.program_id(0) >= NPT
    m = mod_ref[0, 0]
    x = _stream_tile(xp_ref, xs_ref)
    for s in range(TM // TM_SUB):
        rows = slice(s * TM_SUB, (s + 1) * TM_SUB)
        h = _rms(x[rows], g_ref[...]) * (1.0 + m[1:2]) + m[0:1]
        hb = h.astype(BF16)

        def proj(c0, c1):
            return jnp.dot(hb, w_ref[:, c0:c1], preferred_element_type=F32)

        a_ref[rows, :] = proj(_C_A, _C_R)
        r_ref[rows, :] = proj(_C_R, _C_GL)
        gl_ref[rows, :] = proj(_C_GL, _C_Q)
        v_ref[rows, :] = proj(_C_V, _C_END)
        q = proj(_C_Q, _C_K)
        k = proj(_C_K, _C_V)
        cos = cos_ref[rows, :]
        sin = sin_ref[rows, :]
        q_ref[rows, :] = jnp.where(latent, _rope(q, cos, sin), q)
        k_ref[rows, :] = jnp.where(latent, _rope(k, cos, sin), k)


def _in_proj(xp, xs, mod, l, g, w_pad, cos, sin):
    tok = lambda width: pl.BlockSpec((TM, width), lambda i: (i, 0))
    rope_spec = pl.BlockSpec((TM, QK_B), lambda i: (jnp.maximum(i - NPT, 0) % TILES_PER_DEC, 0))
    widths = (1024, 512, GL_PAD, 512, 512, 512)
    return pl.pallas_call(
        _in_kernel,
        grid=(T // TM,),
        in_specs=_stream_specs() + [
            pl.BlockSpec((1, 1, 6, D), lambda i: (l, _mod_row(i), 0, 0)),
            pl.BlockSpec((1, D), lambda i: (0, 0)),
            pl.BlockSpec((D, _C_END), lambda i: (0, 0)),
            rope_spec, rope_spec,
        ],
        out_specs=[tok(w) for w in widths],
        out_shape=[jax.ShapeDtypeStruct((T, w), F32) for w in widths],
        compiler_params=_params(("arbitrary",)),
        name="mixer_ab_in_proj",
    )(xp, xs, mod, g, w_pad, cos, sin)


def _log_sigmoid(x):
    return jnp.minimum(x, 0.0) - jnp.log(1.0 + jnp.exp(-jnp.abs(x)))


def _gla_kernel(af_ref, ab_ref, glf_ref, glb_ref, wg_ref, bg_ref, s0_ref,
                of_ref, ob_ref, sfin_ref, st_ref):
    i = pl.program_id(0)

    @pl.when(i < NSEG_P)
    def _():
        st_ref[...] = jnp.zeros_like(st_ref)

    @pl.when(jnp.logical_and(i >= NSEG_P, (i - NSEG_P) % SEG_PER_DEC == 0))
    def _():
        st_ref[...] = s0_ref[0]

    r = lax.broadcasted_iota(I32, (SEG, SEG), 0)
    c = lax.broadcasted_iota(I32, (SEG, SEG), 1)
    same = (r // GLA_CHUNK) == (c // GLA_CHUNK)
    nchunk = SEG // GLA_CHUNK
    own_head = (lax.broadcasted_iota(I32, (V_A, Q_A), 0) // DV_A) == (lax.broadcasted_iota(I32, (V_A, Q_A), 1) // DK_A)

    for d, (a_ref, gl_ref, o_ref) in enumerate(((af_ref, glf_ref, of_ref), (ab_ref, glb_ref, ob_ref))):
        fwd = d == 0
        gcol = gl_ref[:, d * GATE_RANK:(d + 1) * GATE_RANK]
        la = _log_sigmoid(_bdot(gcol, wg_ref[d]) + bg_ref[d]) / GATE_TAU
        causal = jnp.logical_and(same, (c <= r) if fwd else (c >= r))
        tri = jnp.where(causal, 1.0, 0.0).astype(BF16)
        l1, l2, l3 = _split3(la)
        b_all = (jnp.dot(tri, l1, preferred_element_type=F32)
                 + jnp.dot(tri, l2, preferred_element_type=F32)
                 + jnp.dot(tri, l3, preferred_element_type=F32))
        q_in_all = a_ref[:, 0:Q_A] * (DK_A ** -0.5) * jnp.exp(b_all)
        kd_all = a_ref[:, Q_A:2 * Q_A] * jnp.exp(-b_all)
        intra = []
        for h in range(H_A):
            kc = slice(h * DK_A, (h + 1) * DK_A)
            attn = jnp.where(causal, _bdot_nt(q_in_all[:, kc], kd_all[:, kc]), 0.0)
            intra.append(_bdot(attn, a_ref[:, 2 * Q_A + h * DV_A:2 * Q_A + (h + 1) * DV_A]))
        intra = jnp.concatenate(intra, axis=1)
        state = st_ref[d]
        order = range(nchunk) if fwd else range(nchunk - 1, -1, -1)
        for ch in order:
            r0 = ch * GLA_CHUNK
            rows = slice(r0, r0 + GLA_CHUNK)
            end = r0 + GLA_CHUNK - 1 if fwd else r0
            b_end = b_all[end:end + 1, :]
            kw = a_ref[rows, Q_A:2 * Q_A] * jnp.exp(b_end - b_all[rows, :])
            o_ref[rows, :] = intra[rows, :] + _bdot_nt(q_in_all[rows, :], state)
            kv_t = _bdot_tn(a_ref[rows, 2 * Q_A:2 * Q_A + V_A], kw)
            state = state * jnp.exp(b_end) + jnp.where(own_head, kv_t, 0.0)
        st_ref[d] = state

    @pl.when(i < NSEG_P)
    def _():
        for d in range(2):
            for h in range(H_A):
                sfin_ref[0, d, h] = st_ref[d, h * DV_A:(h + 1) * DV_A, h * DK_A:(h + 1) * DK_A]


def _seg_bwd(i):
    j = i - NSEG_P
    return jnp.where(i < NSEG_P, i, NSEG_P + (j // SEG_PER_DEC) * SEG_PER_DEC + (SEG_PER_DEC - 1 - j % SEG_PER_DEC))


def _gla(a, gl, wg, bg, s0_t):
    seg = lambda width, f: pl.BlockSpec((SEG, width), lambda i: (f(i), 0))
    ident = lambda i: i
    st_block = (1, 2, H_A, DV_A, DK_A)
    return pl.pallas_call(
        _gla_kernel,
        grid=(NSEG,),
        in_specs=[
            seg(D, ident), seg(D, _seg_bwd), seg(GL_PAD, ident), seg(GL_PAD, _seg_bwd),
            pl.BlockSpec((2, GATE_RANK, Q_A), lambda i: (0, 0, 0)),
            pl.BlockSpec((2, 1, Q_A), lambda i: (0, 0, 0)),
            pl.BlockSpec((1, 2, V_A, Q_A), lambda i: (jnp.maximum(i - NSEG_P, 0) // SEG_PER_DEC, 0, 0, 0)),
        ],
        out_specs=[
            seg(V_A, ident), seg(V_A, _seg_bwd),
            pl.BlockSpec(st_block, lambda i: (jnp.minimum(i, NSEG_P - 1), 0, 0, 0, 0)),
        ],
        out_shape=[
            jax.ShapeDtypeStruct((T, V_A), F32),
            jax.ShapeDtypeStruct((T, V_A), F32),
            jax.ShapeDtypeStruct((BATCH, 2, H_A, DV_A, DK_A), F32),
        ],
        scratch_shapes=[pltpu.VMEM((2, V_A, Q_A), F32)],
        compiler_params=_params(("arbitrary",)),
        name="gla_bidir",
    )(a, a, gl, gl, wg, bg, s0_t)


def _diff_lambda(lam_ref, lam_init):
    lp = lam_ref[...]
    s01 = jnp.sum(lp[0:1] * lp[1:2], axis=1, keepdims=True)
    s23 = jnp.sum(lp[2:3] * lp[3:4], axis=1, keepdims=True)
    return jnp.exp(s01) - jnp.exp(s23) + lam_init


def _attn_prompt_kernel(lam_init, q_ref, k_ref, v_ref, lam_ref, o_ref):
    lam = _diff_lambda(lam_ref, lam_init)
    for h in range(H_B):
        ps = []
        for m in range(2):
            cols = slice((2 * h + m) * DQK_B, (2 * h + m + 1) * DQK_B)
            s = _bdot_nt(q_ref[:, cols] * (DQK_B ** -0.5), k_ref[:, cols])
            e = jnp.exp(s - jnp.max(s, axis=1, keepdims=True))
            ps.append(e * (1.0 / jnp.sum(e, axis=1, keepdims=True)))
        w = ps[0] - lam * ps[1]
        o_ref[:, h * DV_B:(h + 1) * DV_B] = _bdot(w, v_ref[:, h * DV_B:(h + 1) * DV_B])


def _attn_sample_kernel(lam_init, q_ref, k_ref, v_ref, ck_ref, cv_ref, lam_ref, o_ref):
    lam = _diff_lambda(lam_ref, lam_init)
    for h in range(H_B):
        parts = []
        for m in range(2):
            cols = slice((2 * h + m) * DQK_B, (2 * h + m + 1) * DQK_B)
            q = q_ref[:, cols] * (DQK_B ** -0.5)
            sc = _bdot_nt(q, ck_ref[0, 0, h, m])
            sn = _bdot_nt(q, k_ref[:, cols])
            mx = jnp.maximum(jnp.max(sc, axis=1, keepdims=True), jnp.max(sn, axis=1, keepdims=True))
            ec = jnp.exp(sc - mx)
            en = jnp.exp(sn - mx)
            inv = (1.0 if m == 0 else -lam) / (jnp.sum(ec, axis=1, keepdims=True) + jnp.sum(en, axis=1, keepdims=True))
            parts.append((ec * inv, en * inv))
        wc = parts[0][0] + parts[1][0]
        wn = parts[0][1] + parts[1][1]
        o_ref[:, h * DV_B:(h + 1) * DV_B] = (_bdot(wc, cv_ref[0, 0, h])
                                             + _bdot(wn, v_ref[:, h * DV_B:(h + 1) * DV_B]))


QB = SEQ
NQB_DEC = DEC_SEQ // QB


def _attn_kernel(lam_init, q_ref, kp_ref, vp_ref, ks_ref, vs_ref, ck_ref, cv_ref, lam_ref, o_ref):
    i = pl.program_id(0)

    @pl.when(i < BATCH)
    def _():
        _attn_prompt_kernel(lam_init, q_ref, kp_ref, vp_ref, lam_ref, o_ref)

    @pl.when(i >= BATCH)
    def _():
        _attn_sample_kernel(lam_init, q_ref, ks_ref, vs_ref, ck_ref, cv_ref, lam_ref, o_ref)


def _diff_attention(q, k, v, cache_k, cache_v, lam_p, lam_init):
    blk = lambda rows, f: pl.BlockSpec((rows, 512), f)
    dec_b = lambda i: jnp.maximum(i - BATCH, 0) // NQB_DEC
    own = lambda i: (i, 0)
    prompt_kv = lambda i: (jnp.minimum(i, BATCH - 1), 0)
    dec_kv = lambda i: (TP // DEC_SEQ + dec_b(i), 0)
    return pl.pallas_call(
        functools.partial(_attn_kernel, lam_init),
        grid=(BATCH + DEC_BATCH * NQB_DEC,),
        in_specs=[
            blk(QB, own), blk(SEQ, prompt_kv), blk(SEQ, prompt_kv), blk(DEC_SEQ, dec_kv), blk(DEC_SEQ, dec_kv),
            pl.BlockSpec((1, 1, H_B, 2, PAST_LEN, DQK_B), lambda i: (dec_b(i), 0, 0, 0, 0, 0)),
            pl.BlockSpec((1, 1, H_B, PAST_LEN, DV_B), lambda i: (dec_b(i), 0, 0, 0, 0)),
            pl.BlockSpec((4, DQK_B), lambda i: (0, 0)),
        ],
        out_specs=blk(QB, own),
        out_shape=jax.ShapeDtypeStruct((T, V_B), F32),
        compiler_params=_params(("arbitrary",)),
        name="diff_attention",
    )(q, k, v, k, v, cache_k, cache_v, lam_p)


def _head_rms(x, g, nheads, width):
    return jnp.concatenate([_rms(x[:, h * width:(h + 1) * width], g) for h in range(nheads)], axis=1)


def _mix_out_kernel(lam_init, of_ref, ob_ref, r_ref, oatt_ref, xp_ref, xs_ref, mod_ref,
                    gg_ref, dg_ref, wo_ref, gp_ref, o_ref):
    m = mod_ref[0, 0]
    o_a = _head_rms(of_ref[...] + ob_ref[...], gg_ref[...], H_A, DV_A) * _silu(r_ref[...])
    o_b = _head_rms(oatt_ref[...], dg_ref[...], H_B, DV_B) * (1.0 - lam_init)
    out = _bdot(o_a, wo_ref[0:V_A, :]) + _bdot(o_b, wo_ref[V_A:V_A + V_B, :])
    o_ref[...] = _stream_tile(xp_ref, xs_ref) + m[2:3] * _rms(out, gp_ref[...])


def _mix_out(lam_init, o_f, o_b, r_a, o_att, xp, xs, mod, l, gla_g, diff_g, w_o, g_post):
    tok = lambda width: pl.BlockSpec((TM, width), lambda i: (i, 0))
    vec = lambda width: pl.BlockSpec((1, width), lambda i: (0, 0))
    return pl.pallas_call(
        functools.partial(_mix_out_kernel, lam_init),
        grid=(T // TM,),
        in_specs=[
            tok(512), tok(512), tok(512), tok(512), *_stream_specs(),
            pl.BlockSpec((1, 1, 6, D), lambda i: (l, _mod_row(i), 0, 0)),
            vec(DV_A), vec(DV_B),
            pl.BlockSpec((V_A + V_B, D), lambda i: (0, 0)),
            vec(D),
        ],
        out_specs=tok(D),
        out_shape=jax.ShapeDtypeStruct((T, D), F32),
        compiler_params=_params(("arbitrary",)),
        name="mixer_ab_out",
    )(o_f, o_b, r_a, o_att, xp, xs, mod, gla_g, diff_g, w_o, g_post)


def _gelu_tanh(x):
    return 0.5 * x * (1.0 + jnp.tanh(math.sqrt(2.0 / math.pi) * (x + 0.044715 * (x * x * x))))


def _sgu_kernel(xp_ref, xs_ref, mod_ref, gpre_ref, win_ref, bin_ref, vg_ref, ws_ref, bs_ref,
                wout_ref, gpost_ref, o_ref, t_ref):
    m = mod_ref[0, 0]
    x = _stream_tile(xp_ref, xs_ref)
    h = _rms(x, gpre_ref[...]) * (1.0 + m[1:2]) + m[0:1]
    z = _gelu_tanh(_bdot(h, win_ref[...]) + bin_ref[...])
    v = _rms(z[:, SGU_DIM:], vg_ref[...])
    gw = SGU_DIM // SGU_GROUPS
    for ch in range(TM // SGU_CHUNK):
        rows = slice(ch * SGU_CHUNK, (ch + 1) * SGU_CHUNK)
        for g in range(SGU_GROUPS):
            cols = slice(g * gw, (g + 1) * gw)
            vs = _bdot(ws_ref[g], v[rows, cols]) + bs_ref[:, g:g + 1]
            t_ref[rows, cols] = (z[rows, cols] * vs).astype(BF16)
    out = jnp.dot(t_ref[...], wout_ref[...], preferred_element_type=F32)
    o_ref[...] = x + m[2:3] * _rms(out, gpost_ref[...])


def _sgu(xp, xs, mod, l, g_pre, w_in, b_in, v_g, w_s, b_s_t, w_out, g_post):
    tok = pl.BlockSpec((TM, D), lambda i: (i, 0))
    full = lambda *shape: pl.BlockSpec(shape, lambda i: (0,) * len(shape))
    return pl.pallas_call(
        _sgu_kernel,
        grid=(T // TM,),
        in_specs=_stream_specs() + [
            pl.BlockSpec((1, 1, 6, D), lambda i: (l, _mod_row(i), 0, 0)),
            full(1, D), full(D, 2 * SGU_DIM), full(1, 2 * SGU_DIM), full(1, SGU_DIM),
            full(SGU_GROUPS, SGU_CHUNK, SGU_CHUNK), full(SGU_CHUNK, SGU_GROUPS),
            full(SGU_DIM, D), full(1, D),
        ],
        out_specs=tok,
        out_shape=jax.ShapeDtypeStruct((T, D), F32),
        scratch_shapes=[pltpu.VMEM((TM, SGU_DIM), BF16)],
        compiler_params=_params(("arbitrary",)),
        name="sgu_mixer",
    )(xp, xs, mod, g_pre, w_in, b_in, v_g, w_s, b_s_t, w_out, g_post)


LANES = 128
U32 = jnp.uint32
PACKED = D // 2
ROWS_PER_TOKEN = PACKED // LANES
HIGH_HALF = 0xFFFF0000


def _pack_rows(x):
    bits = lax.bitcast_convert_type(x.astype(BF16).astype(F32), U32)
    return bits[:, :PACKED] | (bits[:, PACKED:] >> 16)


def _unpack_rows(u):
    return (lax.bitcast_convert_type(u & U32(HIGH_HALF), F32), lax.bitcast_convert_type(u << 16, F32))


def _store_token_tiles(ref, u, first=0):
    n = u.shape[0]
    for c in range(ROWS_PER_TOKEN):
        ref[pl.ds(first * ROWS_PER_TOKEN + c, n, stride=ROWS_PER_TOKEN), :] = u[:, c * LANES:(c + 1) * LANES]


def _load_token_tiles(ref, n, first=0):
    return jnp.concatenate([ref[pl.ds(first * ROWS_PER_TOKEN + c, n, stride=ROWS_PER_TOKEN), :]
                            for c in range(ROWS_PER_TOKEN)], axis=1)


def _router_kernel(x_ref, mod_ref, g_ref, wr_ref, eb_ref,
                   h_ref, te_ref, wn_ref, rk_ref, cnt_ref, carry_ref):
    i = pl.program_id(0)

    @pl.when(i == 0)
    def _():
        carry_ref[...] = jnp.zeros_like(carry_ref)

    m = mod_ref[0, 0]
    h = _rms(x_ref[...], g_ref[...]) * (1.0 + m[4:5]) + m[3:4]
    _store_token_tiles(h_ref, _pack_rows(h))
    h1, h2, _ = _split3(h)
    w1, w2, _ = _split3(wr_ref[...])
    nt = lambda a, b: lax.dot_general(a, b, (((1,), (1,)), ((), ())), preferred_element_type=F32)
    logits = nt(w1, h1) + nt(w1, h2) + nt(w2, h1)
    scores = jax.nn.sigmoid(logits)
    sel = scores + eb_ref[...]

    row8 = lax.broadcasted_iota(I32, (GROUP_SIZE, TR), 0)
    gscore = []
    for g in range(N_GROUPS):
        xg = sel[g * GROUP_SIZE:(g + 1) * GROUP_SIZE]
        m1 = jnp.max(xg, axis=0, keepdims=True)
        i1 = jnp.min(jnp.where(xg == m1, row8, GROUP_SIZE), axis=0, keepdims=True)
        m2 = jnp.max(jnp.where(row8 == i1, NEG_INF, xg), axis=0, keepdims=True)
        gscore.append(m1 + m2)
    pieces = []
    for g in range(N_GROUPS):
        rank = jnp.zeros((1, TR), I32)
        for g2 in range(N_GROUPS):
            if g2 == g:
                continue
            beats = (gscore[g2] >= gscore[g]) if g2 < g else (gscore[g2] > gscore[g])
            rank = rank + beats.astype(I32)
        pieces.append(jnp.where(rank < TOPK_GROUPS, sel[g * GROUP_SIZE:(g + 1) * GROUP_SIZE], NEG_INF))
    cur = jnp.concatenate(pieces, axis=0)

    row = lax.broadcasted_iota(I32, (N_EXPERTS, TR), 0)
    idxs, ws = [], []
    for _ in range(TOP_K):
        mx = jnp.max(cur, axis=0, keepdims=True)
        idx = jnp.min(jnp.where(cur == mx, row, N_EXPERTS), axis=0, keepdims=True)
        hit = row == idx
        ws.append(jnp.sum(jnp.where(hit, scores, 0.0), axis=0, keepdims=True))
        cur = jnp.where(hit, NEG_INF, cur)
        idxs.append(idx)
    mask = jnp.zeros((N_EXPERTS, TR), F32)
    for idx in idxs:
        mask = mask + (row == idx).astype(F32)
    wsum = ws[0]
    for wk in ws[1:]:
        wsum = wsum + wk

    tj = lax.broadcasted_iota(I32, (TR, TR), 0)
    ti = lax.broadcasted_iota(I32, (TR, TR), 1)
    upper = jnp.where(tj < ti, 1.0, 0.0).astype(BF16)
    pos = carry_ref[...] + jnp.dot(mask.astype(BF16), upper, preferred_element_type=F32)
    for k in range(TOP_K):
        hit = row == idxs[k]
        te_ref[k:k + 1, :] = idxs[k]
        wn_ref[k:k + 1, :] = ws[k] / wsum * ROUTED_SCALE
        rk_ref[k:k + 1, :] = jnp.sum(jnp.where(hit, pos, 0.0), axis=0, keepdims=True).astype(I32)
    carry_ref[...] = carry_ref[...] + jnp.sum(mask, axis=1, keepdims=True)
    cnt_ref[...] = carry_ref[...]


def _router(x, mod, l, g, wr_t, e_bias):
    kt = lambda dtype: jax.ShapeDtypeStruct((TOP_K, T), dtype)
    kt_spec = pl.BlockSpec((TOP_K, TR), lambda i: (0, i))
    tiles_per_dec = DEC_SEQ // TR
    mod_row = lambda i: jnp.where(i < TP // TR, 0, 1 + (i - TP // TR) // tiles_per_dec)
    return pl.pallas_call(
        _router_kernel,
        grid=(T // TR,),
        in_specs=[
            pl.BlockSpec((TR, D), lambda i: (i, 0)),
            pl.BlockSpec((1, 1, 6, D), lambda i: (l, mod_row(i), 0, 0)),
            pl.BlockSpec((1, D), lambda i: (0, 0)),
            pl.BlockSpec((N_EXPERTS, D), lambda i: (0, 0)),
            pl.BlockSpec((N_EXPERTS, 1), lambda i: (0, 0)),
        ],
        out_specs=[
            pl.BlockSpec((TR * ROWS_PER_TOKEN, LANES), lambda i: (i, 0)),
            kt_spec, kt_spec, kt_spec,
            pl.BlockSpec((N_EXPERTS, 1), lambda i: (0, 0)),
        ],
        out_shape=[
            jax.ShapeDtypeStruct((T * ROWS_PER_TOKEN, LANES), U32), kt(I32), kt(F32), kt(I32),
            jax.ShapeDtypeStruct((N_EXPERTS, 1), F32),
        ],
        scratch_shapes=[pltpu.VMEM((N_EXPERTS, 1), F32)],
        compiler_params=_params(("arbitrary",)),
        name="moe_router",
    )(x, mod, g, wr_t, e_bias)


_PAD_BITS = tuple(1 << b for b in range(GM.bit_length() - 1))


def _pad_fill_kernel(pad_start_ref, pad_len_ref, xg_in_ref, xg_ref, zero_ref, sem):
    del xg_in_ref
    zero_ref[...] = jnp.zeros_like(zero_ref)

    def pad_copies(e):
        start = pad_start_ref[e]
        n = pad_len_ref[e]
        copies = []
        for bit in _PAD_BITS:
            first = start + (n & ~(2 * bit - 1))
            copies.append(((n & bit) != 0, pltpu.make_async_copy(
                zero_ref.at[pl.ds(0, bit)], xg_ref.at[pl.ds(first, bit)], sem)))
        return copies

    def start_e(e, carry):
        for on, cp in pad_copies(e):
            @pl.when(on)
            def _():
                cp.start()
        return carry

    def wait_e(e, carry):
        for on, cp in pad_copies(e):
            @pl.when(on)
            def _():
                cp.wait()
        return carry

    lax.fori_loop(0, N_EXPERTS, start_e, 0)
    lax.fori_loop(0, N_EXPERTS, wait_e, 0)


def _pad_fill(pad_start, pad_len, xg):
    grid_spec = pltpu.PrefetchScalarGridSpec(
        num_scalar_prefetch=2,
        grid=(1,),
        in_specs=[pl.BlockSpec(memory_space=pl.ANY)],
        out_specs=pl.BlockSpec(memory_space=pl.ANY),
        scratch_shapes=[pltpu.VMEM((GM // 2, ROWS_PER_TOKEN, LANES), xg.dtype), pltpu.SemaphoreType.DMA],
    )
    return pl.pallas_call(
        _pad_fill_kernel,
        grid_spec=grid_spec,
        out_shape=jax.ShapeDtypeStruct(xg.shape, xg.dtype),
        input_output_aliases={2: 0},
        compiler_params=_params(("arbitrary",)),
        name="moe_pad_fill",
    )(pad_start, pad_len, xg)


SC_CORES, SC_SUBCORES = 2, 16
SC_WORKERS = SC_CORES * SC_SUBCORES
SC_W = 64


def _sc_worker_id():
    return lax.axis_index("s") * SC_CORES + lax.axis_index("c")


def _sc_dispatch(h3, slot3):
    nchunk = T // SC_WORKERS // SC_W
    mesh = plsc.VectorSubcoreMesh(core_axis_name="c", subcore_axis_name="s")
    tile = (SC_W, ROWS_PER_TOKEN, LANES)

    @functools.partial(
        pl.kernel, mesh=mesh,
        out_type=jax.ShapeDtypeStruct((SP, ROWS_PER_TOKEN, LANES), h3.dtype),
        scratch_types=[pltpu.VMEM((TOP_K, SC_W), I32), pltpu.VMEM((TOP_K, SC_W), I32),
                       pltpu.VMEM(tile, h3.dtype), pltpu.VMEM(tile, h3.dtype),
                       pltpu.SemaphoreType.DMA((2,)), pltpu.SemaphoreType.DMA((2,))],
    )
    def k(h_hbm, slot_hbm, xg_hbm, idx0, idx1, rows0, rows1, lsem, ssem):
        first = _sc_worker_id() * nchunk
        idx = (idx0, idx1)
        rows = (rows0, rows1)

        def loads(j, b):
            blk = first + j
            tok = pl.multiple_of(blk * SC_W, SC_W)
            return (pltpu.make_async_copy(slot_hbm.at[blk], idx[b], lsem.at[b]),
                    pltpu.make_async_copy(h_hbm.at[pl.ds(tok, SC_W)], rows[b], lsem.at[b]))

        def scatters(b):
            return [pltpu.make_async_copy(rows[b], xg_hbm.at[idx[b].at[kk]], ssem.at[b]) for kk in range(TOP_K)]

        for cp in loads(0, 0):
            cp.start()

        @pl.loop(0, nchunk, step=2)
        def _(j):
            for b in (0, 1):
                jj = j + b
                for cp in loads(jj, b):
                    cp.wait()
                for cp in scatters(b):
                    cp.start()

                @pl.when(jj + 1 < nchunk)
                def _():
                    @pl.when(jj >= 1)
                    def _():
                        for cp in scatters(1 - b):
                            cp.wait()
                    for cp in loads(jj + 1, 1 - b):
                        cp.start()

        for b in (0, 1):
            for cp in scatters(b):
                cp.wait()

    return k(h3, slot3)


def _gmm_kernel(tile_e_ref, tile_blk_ref, tile_nsub_ref, x_ref, wg_ref, wu_ref, wd_ref,
                y_ref, wgu_scr, wd_scr):
    j = pl.program_id(0)
    nsub = tile_nsub_ref[j]

    @pl.when(nsub > 0)
    def _():
        prev = tile_e_ref[jnp.maximum(j - 1, 0)]

        @pl.when(jnp.logical_or(j == 0, tile_e_ref[j] != prev))
        def _():
            wgu_scr[:, 0:D_EXPERT] = wg_ref[0, 0].astype(BF16)
            wgu_scr[:, D_EXPERT:2 * D_EXPERT] = wu_ref[0, 0].astype(BF16)
            wd_scr[...] = wd_ref[0, 0].astype(BF16)

    def expert_mlp(s):
        x_hi, x_lo = _unpack_rows(_load_token_tiles(x_ref, GM_SUB, s * GM_SUB))
        gu = (jnp.dot(x_hi.astype(BF16), wgu_scr[0:PACKED, :], preferred_element_type=F32)
              + jnp.dot(x_lo.astype(BF16), wgu_scr[PACKED:D, :], preferred_element_type=F32))
        hid = _silu(gu[:, 0:D_EXPERT]) * gu[:, D_EXPERT:2 * D_EXPERT]
        y = jnp.dot(hid.astype(BF16), wd_scr[...], preferred_element_type=F32)
        _store_token_tiles(y_ref, _pack_rows(y), s * GM_SUB)

    for n in range(1, GM // GM_SUB + 1):
        @pl.when(nsub == n)
        def _():
            for s in range(n):
                expert_mlp(s)
            for s in range(n, GM // GM_SUB):
                _store_token_tiles(y_ref, jnp.zeros((GM_SUB, PACKED), U32), s * GM_SUB)


def _gmm(tile_e, tile_blk, tile_nsub, xg, l, w_gate, w_up, w_down):
    row_tile = pl.BlockSpec((GM * ROWS_PER_TOKEN, LANES), lambda j, te, tb, nv: (tb[j], 0))
    grid_spec = pltpu.PrefetchScalarGridSpec(
        num_scalar_prefetch=3,
        grid=(NT_MAX,),
        in_specs=[
            row_tile,
            pl.BlockSpec((1, 1, D, D_EXPERT), lambda j, te, tb, nv: (l, te[j], 0, 0)),
            pl.BlockSpec((1, 1, D, D_EXPERT), lambda j, te, tb, nv: (l, te[j], 0, 0)),
            pl.BlockSpec((1, 1, D_EXPERT, D), lambda j, te, tb, nv: (l, te[j], 0, 0)),
        ],
        out_specs=row_tile,
        scratch_shapes=[pltpu.VMEM((D, 2 * D_EXPERT), BF16), pltpu.VMEM((D_EXPERT, D), BF16)],
    )
    return pl.pallas_call(
        _gmm_kernel,
        grid_spec=grid_spec,
        out_shape=jax.ShapeDtypeStruct((SP * ROWS_PER_TOKEN, LANES), U32),
        compiler_params=_params(("arbitrary",)),
        name="moe_grouped_matmul",
    )(tile_e, tile_blk, tile_nsub, xg, w_gate, w_up, w_down)


def _sc_gather(table3, idx):
    n_idx = idx.shape[0]
    per_w = n_idx // SC_WORKERS
    nchunk = per_w // SC_W
    mesh = plsc.VectorSubcoreMesh(core_axis_name="c", subcore_axis_name="s")
    tile = (SC_W, ROWS_PER_TOKEN, LANES)

    @functools.partial(
        pl.kernel, mesh=mesh,
        out_type=jax.ShapeDtypeStruct((n_idx, ROWS_PER_TOKEN, LANES), table3.dtype),
        scratch_types=[pltpu.VMEM((per_w,), I32), pltpu.VMEM(tile, table3.dtype), pltpu.VMEM(tile, table3.dtype),
                       pltpu.SemaphoreType.DMA((2,)), pltpu.SemaphoreType.DMA((2,))],
    )
    def k(table_hbm, idx_hbm, out_hbm, idx_v, rows0, rows1, gsem, wsem):
        base = pl.multiple_of(_sc_worker_id() * per_w, per_w)
        rows = (rows0, rows1)
        pltpu.sync_copy(idx_hbm.at[pl.ds(base, per_w)], idx_v)

        def gather(j, b):
            ids = idx_v.at[pl.ds(pl.multiple_of(j * SC_W, SC_W), SC_W)]
            return pltpu.make_async_copy(table_hbm.at[ids], rows[b], gsem.at[b])

        def write(j, b):
            dst = out_hbm.at[pl.ds(pl.multiple_of(base + j * SC_W, SC_W), SC_W)]
            return pltpu.make_async_copy(rows[b], dst, wsem.at[b])

        gather(0, 0).start()

        @pl.loop(0, nchunk, step=2)
        def _(j):
            for b in (0, 1):
                jj = j + b
                gather(jj, b).wait()
                write(jj, b).start()

                @pl.when(jj + 1 < nchunk)
                def _():
                    @pl.when(jj >= 1)
                    def _():
                        write(jj - 1, 1 - b).wait()
                    gather(jj + 1, 1 - b).start()

        write(nchunk - 2, 0).wait()
        write(nchunk - 1, 1).wait()

    return k(table3, idx)


def _combine_kernel(wn_ref, x_ref, mod_ref, gpre_ref, gp_ref, wsg_ref, wsu_ref, wsd_ref, y_ref, o_ref):
    m = mod_ref[0, 0]
    x = x_ref[...]
    hb = (_rms(x, gpre_ref[...]) * (1.0 + m[4:5]) + m[3:4]).astype(BF16)
    hid = (_silu(jnp.dot(hb, wsg_ref[...], preferred_element_type=F32))
           * jnp.dot(hb, wsu_ref[...], preferred_element_type=F32))
    acc = jnp.dot(hid.astype(BF16), wsd_ref[...], preferred_element_type=F32)

    r = lax.broadcasted_iota(I32, (TD, TD), 0)
    c = lax.broadcasted_iota(I32, (TD, TD), 1)
    eye = jnp.where(r == c, 1.0, 0.0).astype(BF16)
    nt = lambda a, b: lax.dot_general(a, b, (((1,), (1,)), ((), ())), preferred_element_type=F32)
    w1, w2, w3 = _split3(wn_ref[...])
    w_t = nt(eye, w1) + nt(eye, w2) + nt(eye, w3)

    acc_hi = acc[:, :PACKED]
    acc_lo = acc[:, PACKED:]
    for k in range(TOP_K):
        y_hi, y_lo = _unpack_rows(_load_token_tiles(y_ref, TD, k * TD))
        acc_hi = acc_hi + y_hi * w_t[:, k:k + 1]
        acc_lo = acc_lo + y_lo * w_t[:, k:k + 1]
    acc = jnp.concatenate([acc_hi, acc_lo], axis=1)
    o_ref[...] = x + m[5:6] * _rms(acc, gp_ref[...])


def _combine(wn, x, mod, l, g_pre, g_post, ws_gate, ws_up, ws_down, ybuf, first_tok, n_tok):
    off = first_tok // TD
    tiles_per_dec = DEC_SEQ // TD
    npd = TP // TD
    mod_row = lambda i: jnp.where(i + off < npd, 0, 1 + (i + off - npd) // tiles_per_dec)
    full = lambda *shape: pl.BlockSpec(shape, lambda i: (0,) * len(shape))
    y_spec = pl.BlockSpec((TOP_K * TD * ROWS_PER_TOKEN, LANES), lambda i: (i, 0))
    return pl.pallas_call(
        _combine_kernel,
        grid=(n_tok // TD,),
        in_specs=[
            pl.BlockSpec((TOP_K, TD), lambda i: (0, i + off)),
            pl.BlockSpec((TD, D), lambda i: (i + off, 0)),
            pl.BlockSpec((1, 1, 6, D), lambda i: (l, mod_row(i), 0, 0)),
            full(1, D), full(1, D), full(D, D_SHARED), full(D, D_SHARED), full(D_SHARED, D),
            y_spec,
        ],
        out_specs=pl.BlockSpec((TD, D), lambda i: (i, 0)),
        out_shape=jax.ShapeDtypeStruct((n_tok, D), F32),
        compiler_params=_params(("arbitrary",)),
        name="moe_combine",
    )(wn, x, mod, g_pre, g_post, ws_gate, ws_up, ws_down, ybuf)


def _moe_layer(x, mod, l, g_pre, g_post, w_router, e_bias, w_gate, w_up, w_down,
               ws_gate, ws_up, ws_down):
    h, top_e, wn, rk, cnt = _router(x, mod, l, g_pre, w_router.T, e_bias.reshape(N_EXPERTS, 1))
    cnt = cnt.reshape(N_EXPERTS).astype(I32)
    padded = (cnt + GM - 1) // GM * GM
    ends = jnp.cumsum(padded)
    offs = ends - padded
    eid = jnp.arange(N_EXPERTS, dtype=I32)[:, None, None]
    slot = rk + jnp.sum(jnp.where(top_e[None] == eid, offs[:, None, None], 0), axis=0)
    nvalid = ends[-1] // GM
    tile_start = jnp.arange(NT_MAX, dtype=I32) * GM
    tile_raw = jnp.sum((tile_start[:, None] >= ends[None, :]).astype(I32), axis=1)
    last = jnp.maximum(nvalid - 1, 0)
    tile_blk = jnp.minimum(jnp.arange(NT_MAX, dtype=I32), last)
    tile_e = jnp.minimum(tile_raw, N_EXPERTS - 1)
    tile_e = jnp.where(jnp.arange(NT_MAX) <= last, tile_e, tile_e[last])
    own = tile_e[:, None] == jnp.arange(N_EXPERTS, dtype=I32)[None, :]
    tile_rows = jnp.sum(jnp.where(own, (offs + cnt)[None, :], 0), axis=1) - tile_start
    tile_nsub = jnp.where(jnp.arange(NT_MAX) <= last, (jnp.clip(tile_rows, 0, GM) + GM_SUB - 1) // GM_SUB, 0)
    slot3 = slot.reshape(TOP_K, T // SC_W, SC_W).transpose(1, 0, 2)
    xg = _sc_dispatch(h.reshape(T, ROWS_PER_TOKEN, LANES), slot3)
    xg = _pad_fill(offs + cnt, padded - cnt, xg).reshape(SP * ROWS_PER_TOKEN, LANES)
    yg = _gmm(tile_e, tile_blk, tile_nsub.astype(I32), xg, l, w_gate, w_up, w_down)
    yg3 = yg.reshape(SP, ROWS_PER_TOKEN, LANES)
    ws = (ws_gate.astype(BF16), ws_up.astype(BF16), ws_down.astype(BF16))
    outs = []
    for first_tok, n_tok in ((0, TP), (TP, TS)):
        ids = slot[:, first_tok:first_tok + n_tok].reshape(TOP_K, n_tok // TD, TD).transpose(1, 0, 2)
        ybuf = _sc_gather(yg3, ids.reshape(TOP_K * n_tok))
        outs.append(_combine(wn, x, mod, l, g_pre, g_post, *ws,
                             ybuf.reshape(TOP_K * n_tok * ROWS_PER_TOKEN, LANES), first_tok, n_tok))
    return outs


def _rope_tables():
    n = DEC_SEQ
    rows = n // GRID_W
    row = jnp.repeat(jnp.arange(rows), GRID_W).astype(F32)
    col = jnp.tile(jnp.arange(GRID_W), rows).astype(F32)
    half = DQK_B // 2
    inv = ROPE_BASE ** (-jnp.arange(0, half, 2, dtype=F32) / half)
    ang_r = row[:, None] * inv
    ang_c = col[:, None] * inv
    ang = jnp.concatenate([ang_r, ang_r, ang_c, ang_c], axis=-1)
    reps = QK_B // DQK_B
    return jnp.tile(jnp.cos(ang), (1, reps)), jnp.tile(jnp.sin(ang), (1, reps))


def _pad_in_proj(w):
    s = [0, Q_A, 2 * Q_A, 2 * Q_A + V_A, 2 * Q_A + 2 * V_A]
    s += [s[-1] + GATE_RANK, s[-1] + 2 * GATE_RANK]
    s += [s[-1] + QK_B, s[-1] + 2 * QK_B, s[-1] + 2 * QK_B + V_B]
    gates = jnp.pad(w[:, s[4]:s[6]], ((0, 0), (0, GL_PAD - 2 * GATE_RANK)))
    return jnp.concatenate([w[:, s[0]:s[4]], gates, w[:, s[6]:s[9]]], axis=1).astype(BF16)


def kernel(x_prompt, x_sample, c, c_ctx, state_gla, cache_k, cache_v, ada_w, ada_b, norm_pre_mix, norm_post_mix, norm_pre_ffn, norm_post_ffn, ab_w_in, gla_w_g2, gla_b_g2, gla_norm_g, diff_lambda, diff_norm_g, ab_w_out, sgu_w_in, sgu_b_in, sgu_norm_g, sgu_w_s, sgu_b_s, sgu_w_out, moe_w_router, moe_e_bias, moe_w_gate, moe_w_up, moe_w_down, moe_ws_gate, moe_ws_up, moe_ws_down):
    depth = ada_w.shape[0]
    xp, xs = x_prompt.reshape(TP, D), x_sample.reshape(TS, D)
    cond = jnp.concatenate([c_ctx[None, :], c, jnp.zeros((8 - 1 - DEC_BATCH, D), F32)], axis=0)
    mod = _modulation(cond, ada_w, ada_b)
    cos, sin = _rope_tables()
    vec = lambda a: a.reshape(1, -1)
    new_s = new_k = new_v = None
    for l in range(depth):
        if l % 2 == 0:
            e = l // 2
            lam_init = 0.8 - 0.6 * math.exp(-0.3 * l)
            a, r_a, gl, q_b, k_b, v_b = _in_proj(xp, xs, mod, l, vec(norm_pre_mix[l]), _pad_in_proj(ab_w_in[e]), cos, sin)
            s0_t = jnp.swapaxes(state_gla[:, e], -1, -2)
            same_head = jnp.eye(H_A, dtype=bool)[None, None, :, None, :, None]
            s0_t = jnp.where(same_head, s0_t[:, :, :, :, None, :], 0.0).reshape(DEC_BATCH, 2, V_A, Q_A)
            o_f, o_bw, s_fin_t = _gla(a, gl, gla_w_g2[e], gla_b_g2[e].reshape(2, 1, Q_A), s0_t)
            o_att = _diff_attention(q_b, k_b, v_b, cache_k, cache_v, diff_lambda[e], lam_init)
            x = _mix_out(lam_init, o_f, o_bw, r_a, o_att, xp, xs, mod, l, vec(gla_norm_g[e]), vec(diff_norm_g[e]),
                         ab_w_out[e].astype(BF16), vec(norm_post_mix[l]))
            new_s = jnp.swapaxes(s_fin_t, -1, -2)
            new_k = k_b[:TP].reshape(BATCH, SEQ, H_B, 2, DQK_B).transpose(0, 2, 3, 1, 4)
            new_v = v_b[:TP].reshape(BATCH, SEQ, H_B, DV_B).transpose(0, 2, 1, 3)
        else:
            o = l // 2
            x = _sgu(xp, xs, mod, l, vec(norm_pre_mix[l]), sgu_w_in[o].astype(BF16), vec(sgu_b_in[o]),
                     vec(sgu_norm_g[o]), sgu_w_s[o], sgu_b_s[o].T, sgu_w_out[o].astype(BF16),
                     vec(norm_post_mix[l]))
        xp, xs = _moe_layer(x, mod, l, vec(norm_pre_ffn[l]), vec(norm_post_ffn[l]), moe_w_router[l], moe_e_bias[l],
                            moe_w_gate, moe_w_up, moe_w_down, moe_ws_gate[l], moe_ws_up[l], moe_ws_down[l])
    y_prompt = xp.reshape(BATCH, SEQ, D)
    y_sample = xs.reshape(DEC_BATCH, DEC_SEQ, D)
    return (y_prompt, y_sample, new_s[:, None], new_k[:, None], new_v[:, None])
```

```python
import functools
import math

import jax
import jax.numpy as jnp
from jax import lax
from jax.experimental import pallas as pl
from jax.experimental.pallas import tpu as pltpu
from jax.experimental.pallas import tpu_sc as plsc

F32 = jnp.float32
BF16 = jnp.bfloat16
I32 = jnp.int32

D = 1024
BATCH, SEQ = 32, 256
DEC_BATCH, DEC_SEQ = 4, 2048
PAST_LEN = 256
GRID_W = 64
EPS = 1e-6
TP = BATCH * SEQ
TS = DEC_BATCH * DEC_SEQ
T = TP + TS
H_A, DK_A, DV_A = 4, 64, 128
Q_A, V_A = H_A * DK_A, H_A * DV_A
GATE_RANK, GATE_TAU, GLA_CHUNK = 16, 16.0, 64
H_B, DQK_B, DV_B = 4, 64, 128
QK_B, V_B = H_B * 2 * DQK_B, H_B * DV_B
ROPE_BASE = 10000.0
SGU_DIM, SGU_GROUPS, SGU_CHUNK = 1024, 4, 128
N_EXPERTS, TOP_K, N_GROUPS, TOPK_GROUPS = 64, 8, 8, 4
GROUP_SIZE = N_EXPERTS // N_GROUPS
D_EXPERT, D_SHARED = 256, 256
ROUTED_SCALE = 2.5

TM = 512
TM_SUB = 256
NPT = TP // TM
TILES_PER_DEC = DEC_SEQ // TM
SEG = 256
NSEG = T // SEG
NSEG_P = TP // SEG
SEG_PER_DEC = DEC_SEQ // SEG
TR = 512
TD = 512
GM = 1024
GM_SUB = 256
NT_MAX = T * TOP_K // GM + N_EXPERTS
SP = NT_MAX * GM
GL_PAD = 128
VMEM_LIMIT = 56 * 1024 * 1024
NEG_INF = float("-inf")


def _bdot(a, b):
    return jnp.dot(a.astype(BF16), b.astype(BF16), preferred_element_type=F32)


def _bdot_nt(a, b):
    return lax.dot_general(a.astype(BF16), b.astype(BF16), (((1,), (1,)), ((), ())),
                           preferred_element_type=F32)


def _bdot_tn(a, b):
    return lax.dot_general(a.astype(BF16), b.astype(BF16), (((0,), (0,)), ((), ())),
                           preferred_element_type=F32)


def _split3(x):
    x1 = x.astype(BF16)
    r1 = x - x1.astype(F32)
    x2 = r1.astype(BF16)
    x3 = (r1 - x2.astype(F32)).astype(BF16)
    return x1, x2, x3


def _rms(x, g):
    return x * lax.rsqrt(jnp.mean(x * x, axis=-1, keepdims=True) + EPS) * g


def _silu(x):
    return x * jax.nn.sigmoid(x)


def _mod_row(i):
    return jnp.where(i < NPT, 0, 1 + (i - NPT) // TILES_PER_DEC)


def _params(sem, limit=VMEM_LIMIT):
    return pltpu.CompilerParams(dimension_semantics=sem, vmem_limit_bytes=limit)


def _mod_kernel(c_ref, w_ref, b_ref, o_ref):
    o_ref[0] = _bdot(_silu(c_ref[...]), w_ref[0]) + b_ref[0]


def _modulation(cond, ada_w, ada_b):
    depth = ada_w.shape[0]
    nj = 6
    out = pl.pallas_call(
        _mod_kernel,
        grid=(depth, nj),
        in_specs=[
            pl.BlockSpec((8, D), lambda l, j: (0, 0)),
            pl.BlockSpec((1, D, D), lambda l, j: (l, 0, j)),
            pl.BlockSpec((1, 1, D), lambda l, j: (l, 0, j)),
        ],
        out_specs=pl.BlockSpec((1, 8, D), lambda l, j: (l, 0, j)),
        out_shape=jax.ShapeDtypeStruct((depth, 8, 6 * D), F32),
        compiler_params=_params(("arbitrary", "arbitrary")),
        name="adaln_modulation",
    )(cond, ada_w, ada_b.reshape(depth, 1, 6 * D))
    return out.reshape(depth, 8, 6, D)


_C_A, _C_R, _C_GL, _C_Q, _C_K, _C_V, _C_END = 0, 1024, 1536, 1664, 2176, 2688, 3200


def _rope(x, cos, sin):
    lane = lax.broadcasted_iota(I32, x.shape, 1)
    first = (lane % 32) < 16
    n = x.shape[1]
    xr = jnp.where(first, -pltpu.roll(x, n - 16, 1), pltpu.roll(x, 16, 1))
    return x * cos + xr * sin


def _stream_specs():
    return [pl.BlockSpec((TM, D), lambda i: (jnp.minimum(i, NPT - 1), 0)),
            pl.BlockSpec((TM, D), lambda i: (jnp.maximum(i - NPT, 0), 0))]


def _stream_tile(xp_ref, xs_ref):
    return jnp.where(pl.program_id(0) < NPT, xp_ref[...], xs_ref[...])


def _in_kernel(xp_ref, xs_ref, mod_ref, g_ref, w_ref, cos_ref, sin_ref,
               a_ref, r_ref, gl_ref, q_ref, k_ref, v_ref):
    latent = pl.program_id(0) >= NPT
    m = mod_ref[0, 0]
    x = _stream_tile(xp_ref, xs_ref)
    for s in range(TM // TM_SUB):
        rows = slice(s * TM_SUB, (s + 1) * TM_SUB)
        h = _rms(x[rows], g_ref[...]) * (1.0 + m[1:2]) + m[0:1]
        hb = h.astype(BF16)

        def proj(c0, c1):
            return jnp.dot(hb, w_ref[:, c0:c1], preferred_element_type=F32)

        a_ref[rows, :] = proj(_C_A, _C_R)
        r_ref[rows, :] = proj(_C_R, _C_GL)
        gl_ref[rows, :] = proj(_C_GL, _C_Q)
        v_ref[rows, :] = proj(_C_V, _C_END)
        q = proj(_C_Q, _C_K)
        k = proj(_C_K, _C_V)
        cos = cos_ref[rows, :]
        sin = sin_ref[rows, :]
        q_ref[rows, :] = jnp.where(latent, _rope(q, cos, sin), q)
        k_ref[rows, :] = jnp.where(latent, _rope(k, cos, sin), k)


def _in_proj(xp, xs, mod, l, g, w_pad, cos, sin):
    tok = lambda width: pl.BlockSpec((TM, width), lambda i: (i, 0))
    rope_spec = pl.BlockSpec((TM, QK_B), lambda i: (jnp.maximum(i - NPT, 0) % TILES_PER_DEC, 0))
    widths = (1024, 512, GL_PAD, 512, 512, 512)
    return pl.pallas_call(
        _in_kernel,
        grid=(T // TM,),
        in_specs=_stream_specs() + [
            pl.BlockSpec((1, 1, 6, D), lambda i: (l, _mod_row(i), 0, 0)),
            pl.BlockSpec((1, D), lambda i: (0, 0)),
            pl.BlockSpec((D, _C_END), lambda i: (0, 0)),
            rope_spec, rope_spec,
        ],
        out_specs=[tok(w) for w in widths],
        out_shape=[jax.ShapeDtypeStruct((T, w), F32) for w in widths],
        compiler_params=_params(("arbitrary",)),
        name="mixer_ab_in_proj",
    )(xp, xs, mod, g, w_pad, cos, sin)


def _log_sigmoid(x):
    return jnp.minimum(x, 0.0) - jnp.log(1.0 + jnp.exp(-jnp.abs(x)))


def _gla_kernel(af_ref, ab_ref, glf_ref, glb_ref, wg_ref, bg_ref, s0_ref,
                of_ref, ob_ref, sfin_ref, st_ref):
    i = pl.program_id(0)

    @pl.when(i < NSEG_P)
    def _():
        st_ref[...] = jnp.zeros_like(st_ref)

    @pl.when(jnp.logical_and(i >= NSEG_P, (i - NSEG_P) % SEG_PER_DEC == 0))
    def _():
        st_ref[...] = s0_ref[0]

    r = lax.broadcasted_iota(I32, (SEG, SEG), 0)
    c = lax.broadcasted_iota(I32, (SEG, SEG), 1)
    same = (r // GLA_CHUNK) == (c // GLA_CHUNK)
    nchunk = SEG // GLA_CHUNK
    own_head = (lax.broadcasted_iota(I32, (V_A, Q_A), 0) // DV_A) == (lax.broadcasted_iota(I32, (V_A, Q_A), 1) // DK_A)

    for d, (a_ref, gl_ref, o_ref) in enumerate(((af_ref, glf_ref, of_ref), (ab_ref, glb_ref, ob_ref))):
        fwd = d == 0
        gcol = gl_ref[:, d * GATE_RANK:(d + 1) * GATE_RANK]
        la = _log_sigmoid(_bdot(gcol, wg_ref[d]) + bg_ref[d]) / GATE_TAU
        causal = jnp.logical_and(same, (c <= r) if fwd else (c >= r))
        tri = jnp.where(causal, 1.0, 0.0).astype(BF16)
        l1, l2, l3 = _split3(la)
        b_all = (jnp.dot(tri, l1, preferred_element_type=F32)
                 + jnp.dot(tri, l2, preferred_element_type=F32)
                 + jnp.dot(tri, l3, preferred_element_type=F32))
        q_in_all = a_ref[:, 0:Q_A] * (DK_A ** -0.5) * jnp.exp(b_all)
        kd_all = a_ref[:, Q_A:2 * Q_A] * jnp.exp(-b_all)
        intra = []
        for h in range(H_A):
            kc = slice(h * DK_A, (h + 1) * DK_A)
            attn = jnp.where(causal, _bdot_nt(q_in_all[:, kc], kd_all[:, kc]), 0.0)
            intra.append(_bdot(attn, a_ref[:, 2 * Q_A + h * DV_A:2 * Q_A + (h + 1) * DV_A]))
        intra = jnp.concatenate(intra, axis=1)
        state = st_ref[d]
        order = range(nchunk) if fwd else range(nchunk - 1, -1, -1)
        for ch in order:
            r0 = ch * GLA_CHUNK
            rows = slice(r0, r0 + GLA_CHUNK)
            end = r0 + GLA_CHUNK - 1 if fwd else r0
            b_end = b_all[end:end + 1, :]
            kw = a_ref[rows, Q_A:2 * Q_A] * jnp.exp(b_end - b_all[rows, :])
            o_ref[rows, :] = intra[rows, :] + _bdot_nt(q_in_all[rows, :], state)
            kv_t = _bdot_tn(a_ref[rows, 2 * Q_A:2 * Q_A + V_A], kw)
            state = state * jnp.exp(b_end) + jnp.where(own_head, kv_t, 0.0)
        st_ref[d] = state

    @pl.when(i < NSEG_P)
    def _():
        for d in range(2):
            for h in range(H_A):
                sfin_ref[0, d, h] = st_ref[d, h * DV_A:(h + 1) * DV_A, h * DK_A:(h + 1) * DK_A]


def _seg_bwd(i):
    j = i - NSEG_P
    return jnp.where(i < NSEG_P, i, NSEG_P + (j // SEG_PER_DEC) * SEG_PER_DEC + (SEG_PER_DEC - 1 - j % SEG_PER_DEC))


def _gla(a, gl, wg, bg, s0_t):
    seg = lambda width, f: pl.BlockSpec((SEG, width), lambda i: (f(i), 0))
    ident = lambda i: i
    st_block = (1, 2, H_A, DV_A, DK_A)
    return pl.pallas_call(
        _gla_kernel,
        grid=(NSEG,),
        in_specs=[
            seg(D, ident), seg(D, _seg_bwd), seg(GL_PAD, ident), seg(GL_PAD, _seg_bwd),
            pl.BlockSpec((2, GATE_RANK, Q_A), lambda i: (0, 0, 0)),
            pl.BlockSpec((2, 1, Q_A), lambda i: (0, 0, 0)),
            pl.BlockSpec((1, 2, V_A, Q_A), lambda i: (jnp.maximum(i - NSEG_P, 0) // SEG_PER_DEC, 0, 0, 0)),
        ],
        out_specs=[
            seg(V_A, ident), seg(V_A, _seg_bwd),
            pl.BlockSpec(st_block, lambda i: (jnp.minimum(i, NSEG_P - 1), 0, 0, 0, 0)),
        ],
        out_shape=[
            jax.ShapeDtypeStruct((T, V_A), F32),
            jax.ShapeDtypeStruct((T, V_A), F32),
            jax.ShapeDtypeStruct((BATCH, 2, H_A, DV_A, DK_A), F32),
        ],
        scratch_shapes=[pltpu.VMEM((2, V_A, Q_A), F32)],
        compiler_params=_params(("arbitrary",)),
        name="gla_bidir",
    )(a, a, gl, gl, wg, bg, s0_t)


def _diff_lambda(lam_ref, lam_init):
    lp = lam_ref[...]
    s01 = jnp.sum(lp[0:1] * lp[1:2], axis=1, keepdims=True)
    s23 = jnp.sum(lp[2:3] * lp[3:4], axis=1, keepdims=True)
    return jnp.exp(s01) - jnp.exp(s23) + lam_init


def _attn_prompt_kernel(lam_init, q_ref, k_ref, v_ref, lam_ref, o_ref):
    lam = _diff_lambda(lam_ref, lam_init)
    for h in range(H_B):
        ps = []
        for m in range(2):
            cols = slice((2 * h + m) * DQK_B, (2 * h + m + 1) * DQK_B)
            s = _bdot_nt(q_ref[:, cols] * (DQK_B ** -0.5), k_ref[:, cols])
            e = jnp.exp(s - jnp.max(s, axis=1, keepdims=True))
            ps.append(e * (1.0 / jnp.sum(e, axis=1, keepdims=True)))
        w = ps[0] - lam * ps[1]
        o_ref[:, h * DV_B:(h + 1) * DV_B] = _bdot(w, v_ref[:, h * DV_B:(h + 1) * DV_B])


def _attn_sample_kernel(lam_init, q_ref, k_ref, v_ref, ck_ref, cv_ref, lam_ref, o_ref):
    lam = _diff_lambda(lam_ref, lam_init)
    for h in range(H_B):
        parts = []
        for m in range(2):
            cols = slice((2 * h + m) * DQK_B, (2 * h + m + 1) * DQK_B)
            q = q_ref[:, cols] * (DQK_B ** -0.5)
            sc = _bdot_nt(q, ck_ref[0, 0, h, m])
            sn = _bdot_nt(q, k_ref[:, cols])
            mx = jnp.maximum(jnp.max(sc, axis=1, keepdims=True), jnp.max(sn, axis=1, keepdims=True))
            ec = jnp.exp(sc - mx)
            en = jnp.exp(sn - mx)
            inv = (1.0 if m == 0 else -lam) / (jnp.sum(ec, axis=1, keepdims=True) + jnp.sum(en, axis=1, keepdims=True))
            parts.append((ec * inv, en * inv))
        wc = parts[0][0] + parts[1][0]
        wn = parts[0][1] + parts[1][1]
        o_ref[:, h * DV_B:(h + 1) * DV_B] = (_bdot(wc, cv_ref[0, 0, h])
                                             + _bdot(wn, v_ref[:, h * DV_B:(h + 1) * DV_B]))


QB = SEQ
NQB_DEC = DEC_SEQ // QB


def _attn_kernel(lam_init, q_ref, kp_ref, vp_ref, ks_ref, vs_ref, ck_ref, cv_ref, lam_ref, o_ref):
    i = pl.program_id(0)

    @pl.when(i < BATCH)
    def _():
        _attn_prompt_kernel(lam_init, q_ref, kp_ref, vp_ref, lam_ref, o_ref)

    @pl.when(i >= BATCH)
    def _():
        _attn_sample_kernel(lam_init, q_ref, ks_ref, vs_ref, ck_ref, cv_ref, lam_ref, o_ref)


def _diff_attention(q, k, v, cache_k, cache_v, lam_p, lam_init):
    blk = lambda rows, f: pl.BlockSpec((rows, 512), f)
    dec_b = lambda i: jnp.maximum(i - BATCH, 0) // NQB_DEC
    own = lambda i: (i, 0)
    prompt_kv = lambda i: (jnp.minimum(i, BATCH - 1), 0)
    dec_kv = lambda i: (TP // DEC_SEQ + dec_b(i), 0)
    return pl.pallas_call(
        functools.partial(_attn_kernel, lam_init),
        grid=(BATCH + DEC_BATCH * NQB_DEC,),
        in_specs=[
            blk(QB, own), blk(SEQ, prompt_kv), blk(SEQ, prompt_kv), blk(DEC_SEQ, dec_kv), blk(DEC_SEQ, dec_kv),
            pl.BlockSpec((1, 1, H_B, 2, PAST_LEN, DQK_B), lambda i: (dec_b(i), 0, 0, 0, 0, 0)),
            pl.BlockSpec((1, 1, H_B, PAST_LEN, DV_B), lambda i: (dec_b(i), 0, 0, 0, 0)),
            pl.BlockSpec((4, DQK_B), lambda i: (0, 0)),
        ],
        out_specs=blk(QB, own),
        out_shape=jax.ShapeDtypeStruct((T, V_B), F32),
        compiler_params=_params(("arbitrary",)),
        name="diff_attention",
    )(q, k, v, k, v, cache_k, cache_v, lam_p)


def _head_rms(x, g, nheads, width):
    return jnp.concatenate([_rms(x[:, h * width:(h + 1) * width], g) for h in range(nheads)], axis=1)


def _mix_out_kernel(lam_init, of_ref, ob_ref, r_ref, oatt_ref, xp_ref, xs_ref, mod_ref,
                    gg_ref, dg_ref, wo_ref, gp_ref, gffn_ref, o_ref, hp_ref):
    m = mod_ref[0, 0]
    o_a = _head_rms(of_ref[...] + ob_ref[...], gg_ref[...], H_A, DV_A) * _silu(r_ref[...])
    o_b = _head_rms(oatt_ref[...], dg_ref[...], H_B, DV_B) * (1.0 - lam_init)
    out = _bdot(o_a, wo_ref[0:V_A, :]) + _bdot(o_b, wo_ref[V_A:V_A + V_B, :])
    x1 = _stream_tile(xp_ref, xs_ref) + m[2:3] * _rms(out, gp_ref[...])
    o_ref[...] = x1
    _store_token_tiles(hp_ref, _ffn_input_rows(x1, m, gffn_ref[...]))


def _mix_out(lam_init, o_f, o_b, r_a, o_att, xp, xs, mod, l, gla_g, diff_g, w_o, g_post, g_ffn):
    tok = lambda width: pl.BlockSpec((TM, width), lambda i: (i, 0))
    vec = lambda width: pl.BlockSpec((1, width), lambda i: (0, 0))
    return pl.pallas_call(
        functools.partial(_mix_out_kernel, lam_init),
        grid=(T // TM,),
        in_specs=[
            tok(512), tok(512), tok(512), tok(512), *_stream_specs(),
            pl.BlockSpec((1, 1, 6, D), lambda i: (l, _mod_row(i), 0, 0)),
            vec(DV_A), vec(DV_B),
            pl.BlockSpec((V_A + V_B, D), lambda i: (0, 0)),
            vec(D), vec(D),
        ],
        out_specs=[tok(D), pl.BlockSpec((TM * ROWS_PER_TOKEN, LANES), lambda i: (i, 0))],
        out_shape=[jax.ShapeDtypeStruct((T, D), F32), jax.ShapeDtypeStruct((T * ROWS_PER_TOKEN, LANES), U32)],
        compiler_params=_params(("arbitrary",)),
        name="mixer_ab_out",
    )(o_f, o_b, r_a, o_att, xp, xs, mod, gla_g, diff_g, w_o, g_post, g_ffn)


def _gelu_tanh(x):
    return 0.5 * x * (1.0 + jnp.tanh(math.sqrt(2.0 / math.pi) * (x + 0.044715 * (x * x * x))))


def _sgu_kernel(xp_ref, xs_ref, mod_ref, gpre_ref, win_ref, bin_ref, vg_ref, ws_ref, bs_ref,
                wout_ref, gpost_ref, gffn_ref, o_ref, hp_ref, t_ref):
    m = mod_ref[0, 0]
    x = _stream_tile(xp_ref, xs_ref)
    h = _rms(x, gpre_ref[...]) * (1.0 + m[1:2]) + m[0:1]
    z = _gelu_tanh(_bdot(h, win_ref[...]) + bin_ref[...])
    v = _rms(z[:, SGU_DIM:], vg_ref[...])
    gw = SGU_DIM // SGU_GROUPS
    for ch in range(TM // SGU_CHUNK):
        rows = slice(ch * SGU_CHUNK, (ch + 1) * SGU_CHUNK)
        for g in range(SGU_GROUPS):
            cols = slice(g * gw, (g + 1) * gw)
            vs = _bdot(ws_ref[g], v[rows, cols]) + bs_ref[:, g:g + 1]
            t_ref[rows, cols] = (z[rows, cols] * vs).astype(BF16)
    out = jnp.dot(t_ref[...], wout_ref[...], preferred_element_type=F32)
    x1 = x + m[2:3] * _rms(out, gpost_ref[...])
    o_ref[...] = x1
    _store_token_tiles(hp_ref, _ffn_input_rows(x1, m, gffn_ref[...]))


def _sgu(xp, xs, mod, l, g_pre, w_in, b_in, v_g, w_s, b_s_t, w_out, g_post, g_ffn):
    tok = pl.BlockSpec((TM, D), lambda i: (i, 0))
    full = lambda *shape: pl.BlockSpec(shape, lambda i: (0,) * len(shape))
    return pl.pallas_call(
        _sgu_kernel,
        grid=(T // TM,),
        in_specs=_stream_specs() + [
            pl.BlockSpec((1, 1, 6, D), lambda i: (l, _mod_row(i), 0, 0)),
            full(1, D), full(D, 2 * SGU_DIM), full(1, 2 * SGU_DIM), full(1, SGU_DIM),
            full(SGU_GROUPS, SGU_CHUNK, SGU_CHUNK), full(SGU_CHUNK, SGU_GROUPS),
            full(SGU_DIM, D), full(1, D), full(1, D),
        ],
        out_specs=[tok, pl.BlockSpec((TM * ROWS_PER_TOKEN, LANES), lambda i: (i, 0))],
        out_shape=[jax.ShapeDtypeStruct((T, D), F32), jax.ShapeDtypeStruct((T * ROWS_PER_TOKEN, LANES), U32)],
        scratch_shapes=[pltpu.VMEM((TM, SGU_DIM), BF16)],
        compiler_params=_params(("arbitrary",)),
        name="sgu_mixer",
    )(xp, xs, mod, g_pre, w_in, b_in, v_g, w_s, b_s_t, w_out, g_post, g_ffn)


LANES = 128
U32 = jnp.uint32
PACKED = D // 2
ROWS_PER_TOKEN = PACKED // LANES
HIGH_HALF = 0xFFFF0000


def _pack_rows(x):
    bits = lax.bitcast_convert_type(x.astype(BF16).astype(F32), U32)
    return bits[:, :PACKED] | (bits[:, PACKED:] >> 16)


def _unpack_rows(u):
    return (lax.bitcast_convert_type(u & U32(HIGH_HALF), F32), lax.bitcast_convert_type(u << 16, F32))


def _store_token_tiles(ref, u, first=0):
    n = u.shape[0]
    for c in range(ROWS_PER_TOKEN):
        ref[pl.ds(first * ROWS_PER_TOKEN + c, n, stride=ROWS_PER_TOKEN), :] = u[:, c * LANES:(c + 1) * LANES]


def _load_token_tiles(ref, n, first=0):
    return jnp.concatenate([ref[pl.ds(first * ROWS_PER_TOKEN + c, n, stride=ROWS_PER_TOKEN), :]
                            for c in range(ROWS_PER_TOKEN)], axis=1)


def _ffn_input_rows(x, m, g):
    return _pack_rows(_rms(x, g) * (1.0 + m[4:5]) + m[3:4])


def _router_kernel(hp_ref, wr_ref, eb_ref, te_ref, wn_ref, rk_ref, cnt_ref, carry_ref):
    i = pl.program_id(0)

    @pl.when(i == 0)
    def _():
        carry_ref[...] = jnp.zeros_like(carry_ref)

    h_hi, h_lo = (t.astype(BF16) for t in _unpack_rows(_load_token_tiles(hp_ref, TR)))
    w1, w2, _ = _split3(wr_ref[...])
    nt = lambda a, b: lax.dot_general(a, b, (((1,), (1,)), ((), ())), preferred_element_type=F32)
    logits = (nt(w1[:, :PACKED], h_hi) + nt(w1[:, PACKED:], h_lo)
              + nt(w2[:, :PACKED], h_hi) + nt(w2[:, PACKED:], h_lo))
    scores = jax.nn.sigmoid(logits)
    sel = scores + eb_ref[...]

    row8 = lax.broadcasted_iota(I32, (GROUP_SIZE, TR), 0)
    gscore = []
    for g in range(N_GROUPS):
        xg = sel[g * GROUP_SIZE:(g + 1) * GROUP_SIZE]
        m1 = jnp.max(xg, axis=0, keepdims=True)
        i1 = jnp.min(jnp.where(xg == m1, row8, GROUP_SIZE), axis=0, keepdims=True)
        m2 = jnp.max(jnp.where(row8 == i1, NEG_INF, xg), axis=0, keepdims=True)
        gscore.append(m1 + m2)
    pieces = []
    for g in range(N_GROUPS):
        rank = jnp.zeros((1, TR), I32)
        for g2 in range(N_GROUPS):
            if g2 == g:
                continue
            beats = (gscore[g2] >= gscore[g]) if g2 < g else (gscore[g2] > gscore[g])
            rank = rank + beats.astype(I32)
        pieces.append(jnp.where(rank < TOPK_GROUPS, sel[g * GROUP_SIZE:(g + 1) * GROUP_SIZE], NEG_INF))
    cur = jnp.concatenate(pieces, axis=0)

    row = lax.broadcasted_iota(I32, (N_EXPERTS, TR), 0)
    idxs, ws = [], []
    for _ in range(TOP_K):
        mx = jnp.max(cur, axis=0, keepdims=True)
        idx = jnp.min(jnp.where(cur == mx, row, N_EXPERTS), axis=0, keepdims=True)
        hit = row == idx
        ws.append(jnp.sum(jnp.where(hit, scores, 0.0), axis=0, keepdims=True))
        cur = jnp.where(hit, NEG_INF, cur)
        idxs.append(idx)
    mask = jnp.zeros((N_EXPERTS, TR), F32)
    for idx in idxs:
        mask = mask + (row == idx).astype(F32)
    wsum = ws[0]
    for wk in ws[1:]:
        wsum = wsum + wk

    tj = lax.broadcasted_iota(I32, (TR, TR), 0)
    ti = lax.broadcasted_iota(I32, (TR, TR), 1)
    upper = jnp.where(tj < ti, 1.0, 0.0).astype(BF16)
    pos = carry_ref[...] + jnp.dot(mask.astype(BF16), upper, preferred_element_type=F32)
    for k in range(TOP_K):
        hit = row == idxs[k]
        te_ref[k:k + 1, :] = idxs[k]
        wn_ref[k:k + 1, :] = ws[k] / wsum * ROUTED_SCALE
        rk_ref[k:k + 1, :] = jnp.sum(jnp.where(hit, pos, 0.0), axis=0, keepdims=True).astype(I32)
    carry_ref[...] = carry_ref[...] + jnp.sum(mask, axis=1, keepdims=True)
    cnt_ref[...] = carry_ref[...]


def _router(hp, wr_t, e_bias):
    kt = lambda dtype: jax.ShapeDtypeStruct((TOP_K, T), dtype)
    kt_spec = pl.BlockSpec((TOP_K, TR), lambda i: (0, i))
    return pl.pallas_call(
        _router_kernel,
        grid=(T // TR,),
        in_specs=[
            pl.BlockSpec((TR * ROWS_PER_TOKEN, LANES), lambda i: (i, 0)),
            pl.BlockSpec((N_EXPERTS, D), lambda i: (0, 0)),
            pl.BlockSpec((N_EXPERTS, 1), lambda i: (0, 0)),
        ],
        out_specs=[
            kt_spec, kt_spec, kt_spec,
            pl.BlockSpec((N_EXPERTS, 1), lambda i: (0, 0)),
        ],
        out_shape=[
            kt(I32), kt(F32), kt(I32),
            jax.ShapeDtypeStruct((N_EXPERTS, 1), F32),
        ],
        scratch_shapes=[pltpu.VMEM((N_EXPERTS, 1), F32)],
        compiler_params=_params(("arbitrary",)),
        name="moe_router",
    )(hp, wr_t, e_bias)


_PAD_BITS = tuple(1 << b for b in range(GM.bit_length() - 1))


def _pad_fill_kernel(pad_start_ref, pad_len_ref, xg_in_ref, xg_ref, zero_ref, sem):
    del xg_in_ref
    zero_ref[...] = jnp.zeros_like(zero_ref)

    def pad_copies(e):
        start = pad_start_ref[e]
        n = pad_len_ref[e]
        copies = []
        for bit in _PAD_BITS:
            first = start + (n & ~(2 * bit - 1))
            copies.append(((n & bit) != 0, pltpu.make_async_copy(
                zero_ref.at[pl.ds(0, bit)], xg_ref.at[pl.ds(first, bit)], sem)))
        return copies

    def start_e(e, carry):
        for on, cp in pad_copies(e):
            @pl.when(on)
            def _():
                cp.start()
        return carry

    def wait_e(e, carry):
        for on, cp in pad_copies(e):
            @pl.when(on)
            def _():
                cp.wait()
        return carry

    lax.fori_loop(0, N_EXPERTS, start_e, 0)
    lax.fori_loop(0, N_EXPERTS, wait_e, 0)


def _pad_fill(pad_start, pad_len, xg):
    grid_spec = pltpu.PrefetchScalarGridSpec(
        num_scalar_prefetch=2,
        grid=(1,),
        in_specs=[pl.BlockSpec(memory_space=pl.ANY)],
        out_specs=pl.BlockSpec(memory_space=pl.ANY),
        scratch_shapes=[pltpu.VMEM((GM // 2, ROWS_PER_TOKEN, LANES), xg.dtype), pltpu.SemaphoreType.DMA],
    )
    return pl.pallas_call(
        _pad_fill_kernel,
        grid_spec=grid_spec,
        out_shape=jax.ShapeDtypeStruct(xg.shape, xg.dtype),
        input_output_aliases={2: 0},
        compiler_params=_params(("arbitrary",)),
        name="moe_pad_fill",
    )(pad_start, pad_len, xg)


SC_CORES, SC_SUBCORES = 2, 16
SC_WORKERS = SC_CORES * SC_SUBCORES
SC_W = 64


def _sc_worker_id():
    return lax.axis_index("s") * SC_CORES + lax.axis_index("c")


def _sc_dispatch(h3, slot3):
    nchunk = T // SC_WORKERS // SC_W
    mesh = plsc.VectorSubcoreMesh(core_axis_name="c", subcore_axis_name="s")
    tile = (SC_W, ROWS_PER_TOKEN, LANES)

    @functools.partial(
        pl.kernel, mesh=mesh,
        out_type=jax.ShapeDtypeStruct((SP, ROWS_PER_TOKEN, LANES), h3.dtype),
        scratch_types=[pltpu.VMEM((TOP_K, SC_W), I32), pltpu.VMEM((TOP_K, SC_W), I32),
                       pltpu.VMEM(tile, h3.dtype), pltpu.VMEM(tile, h3.dtype),
                       pltpu.SemaphoreType.DMA((2,)), pltpu.SemaphoreType.DMA((2,))],
    )
    def k(h_hbm, slot_hbm, xg_hbm, idx0, idx1, rows0, rows1, lsem, ssem):
        first = _sc_worker_id() * nchunk
        idx = (idx0, idx1)
        rows = (rows0, rows1)

        def loads(j, b):
            blk = first + j
            tok = pl.multiple_of(blk * SC_W, SC_W)
            return (pltpu.make_async_copy(slot_hbm.at[blk], idx[b], lsem.at[b]),
                    pltpu.make_async_copy(h_hbm.at[pl.ds(tok, SC_W)], rows[b], lsem.at[b]))

        def scatters(b):
            return [pltpu.make_async_copy(rows[b], xg_hbm.at[idx[b].at[kk]], ssem.at[b]) for kk in range(TOP_K)]

        for cp in loads(0, 0):
            cp.start()

        @pl.loop(0, nchunk, step=2)
        def _(j):
            for b in (0, 1):
                jj = j + b
                for cp in loads(jj, b):
                    cp.wait()
                for cp in scatters(b):
                    cp.start()

                @pl.when(jj + 1 < nchunk)
                def _():
                    @pl.when(jj >= 1)
                    def _():
                        for cp in scatters(1 - b):
                            cp.wait()
                    for cp in loads(jj + 1, 1 - b):
                        cp.start()

        for b in (0, 1):
            for cp in scatters(b):
                cp.wait()

    return k(h3, slot3)


def _gmm_kernel(tile_e_ref, tile_blk_ref, tile_nsub_ref, x_ref, wg_ref, wu_ref, wd_ref,
                y_ref, wgu_scr, wd_scr):
    j = pl.program_id(0)
    nsub = tile_nsub_ref[j]

    @pl.when(nsub > 0)
    def _():
        prev = tile_e_ref[jnp.maximum(j - 1, 0)]

        @pl.when(jnp.logical_or(j == 0, tile_e_ref[j] != prev))
        def _():
            wgu_scr[:, 0:D_EXPERT] = wg_ref[0, 0].astype(BF16)
            wgu_scr[:, D_EXPERT:2 * D_EXPERT] = wu_ref[0, 0].astype(BF16)
            wd_scr[...] = wd_ref[0, 0].astype(BF16)

    def expert_mlp(s):
        x_hi, x_lo = _unpack_rows(_load_token_tiles(x_ref, GM_SUB, s * GM_SUB))
        gu = (jnp.dot(x_hi.astype(BF16), wgu_scr[0:PACKED, :], preferred_element_type=F32)
              + jnp.dot(x_lo.astype(BF16), wgu_scr[PACKED:D, :], preferred_element_type=F32))
        hid = _silu(gu[:, 0:D_EXPERT]) * gu[:, D_EXPERT:2 * D_EXPERT]
        y = jnp.dot(hid.astype(BF16), wd_scr[...], preferred_element_type=F32)
        _store_token_tiles(y_ref, _pack_rows(y), s * GM_SUB)

    for n in range(1, GM // GM_SUB + 1):
        @pl.when(nsub == n)
        def _():
            for s in range(n):
                expert_mlp(s)
            for s in range(n, GM // GM_SUB):
                _store_token_tiles(y_ref, jnp.zeros((GM_SUB, PACKED), U32), s * GM_SUB)


def _gmm(tile_e, tile_blk, tile_nsub, xg, l, w_gate, w_up, w_down):
    row_tile = pl.BlockSpec((GM * ROWS_PER_TOKEN, LANES), lambda j, te, tb, nv: (tb[j], 0))
    grid_spec = pltpu.PrefetchScalarGridSpec(
        num_scalar_prefetch=3,
        grid=(NT_MAX,),
        in_specs=[
            row_tile,
            pl.BlockSpec((1, 1, D, D_EXPERT), lambda j, te, tb, nv: (l, te[j], 0, 0)),
            pl.BlockSpec((1, 1, D, D_EXPERT), lambda j, te, tb, nv: (l, te[j], 0, 0)),
            pl.BlockSpec((1, 1, D_EXPERT, D), lambda j, te, tb, nv: (l, te[j], 0, 0)),
        ],
        out_specs=row_tile,
        scratch_shapes=[pltpu.VMEM((D, 2 * D_EXPERT), BF16), pltpu.VMEM((D_EXPERT, D), BF16)],
    )
    return pl.pallas_call(
        _gmm_kernel,
        grid_spec=grid_spec,
        out_shape=jax.ShapeDtypeStruct((SP * ROWS_PER_TOKEN, LANES), U32),
        compiler_params=_params(("arbitrary",)),
        name="moe_grouped_matmul",
    )(tile_e, tile_blk, tile_nsub, xg, w_gate, w_up, w_down)


def _sc_gather(table3, idx):
    n_idx = idx.shape[0]
    per_w = n_idx // SC_WORKERS
    nchunk = per_w // SC_W
    mesh = plsc.VectorSubcoreMesh(core_axis_name="c", subcore_axis_name="s")
    tile = (SC_W, ROWS_PER_TOKEN, LANES)

    @functools.partial(
        pl.kernel, mesh=mesh,
        out_type=jax.ShapeDtypeStruct((n_idx, ROWS_PER_TOKEN, LANES), table3.dtype),
        scratch_types=[pltpu.VMEM((per_w,), I32), pltpu.VMEM(tile, table3.dtype), pltpu.VMEM(tile, table3.dtype),
                       pltpu.SemaphoreType.DMA((2,)), pltpu.SemaphoreType.DMA((2,))],
    )
    def k(table_hbm, idx_hbm, out_hbm, idx_v, rows0, rows1, gsem, wsem):
        base = pl.multiple_of(_sc_worker_id() * per_w, per_w)
        rows = (rows0, rows1)
        pltpu.sync_copy(idx_hbm.at[pl.ds(base, per_w)], idx_v)

        def gather(j, b):
            ids = idx_v.at[pl.ds(pl.multiple_of(j * SC_W, SC_W), SC_W)]
            return pltpu.make_async_copy(table_hbm.at[ids], rows[b], gsem.at[b])

        def write(j, b):
            dst = out_hbm.at[pl.ds(pl.multiple_of(base + j * SC_W, SC_W), SC_W)]
            return pltpu.make_async_copy(rows[b], dst, wsem.at[b])

        gather(0, 0).start()

        @pl.loop(0, nchunk, step=2)
        def _(j):
            for b in (0, 1):
                jj = j + b
                gather(jj, b).wait()
                write(jj, b).start()

                @pl.when(jj + 1 < nchunk)
                def _():
                    @pl.when(jj >= 1)
                    def _():
                        write(jj - 1, 1 - b).wait()
                    gather(jj + 1, 1 - b).start()

        write(nchunk - 2, 0).wait()
        write(nchunk - 1, 1).wait()

    return k(table3, idx)


def _combine_kernel(wn_ref, x_ref, mod_ref, gpre_ref, gp_ref, wsg_ref, wsu_ref, wsd_ref, y_ref, o_ref):
    m = mod_ref[0, 0]
    x = x_ref[...]
    hb = (_rms(x, gpre_ref[...]) * (1.0 + m[4:5]) + m[3:4]).astype(BF16)
    hid = (_silu(jnp.dot(hb, wsg_ref[...], preferred_element_type=F32))
           * jnp.dot(hb, wsu_ref[...], preferred_element_type=F32))
    acc = jnp.dot(hid.astype(BF16), wsd_ref[...], preferred_element_type=F32)

    r = lax.broadcasted_iota(I32, (TD, TD), 0)
    c = lax.broadcasted_iota(I32, (TD, TD), 1)
    eye = jnp.where(r == c, 1.0, 0.0).astype(BF16)
    nt = lambda a, b: lax.dot_general(a, b, (((1,), (1,)), ((), ())), preferred_element_type=F32)
    w1, w2, w3 = _split3(wn_ref[...])
    w_t = nt(eye, w1) + nt(eye, w2) + nt(eye, w3)

    acc_hi = acc[:, :PACKED]
    acc_lo = acc[:, PACKED:]
    for k in range(TOP_K):
        y_hi, y_lo = _unpack_rows(_load_token_tiles(y_ref, TD, k * TD))
        acc_hi = acc_hi + y_hi * w_t[:, k:k + 1]
        acc_lo = acc_lo + y_lo * w_t[:, k:k + 1]
    acc = jnp.concatenate([acc_hi, acc_lo], axis=1)
    o_ref[...] = x + m[5:6] * _rms(acc, gp_ref[...])


def _combine(wn, x, mod, l, g_pre, g_post, ws_gate, ws_up, ws_down, ybuf, first_tok, n_tok):
    off = first_tok // TD
    tiles_per_dec = DEC_SEQ // TD
    npd = TP // TD
    mod_row = lambda i: jnp.where(i + off < npd, 0, 1 + (i + off - npd) // tiles_per_dec)
    full = lambda *shape: pl.BlockSpec(shape, lambda i: (0,) * len(shape))
    y_spec = pl.BlockSpec((TOP_K * TD * ROWS_PER_TOKEN, LANES), lambda i: (i, 0))
    return pl.pallas_call(
        _combine_kernel,
        grid=(n_tok // TD,),
        in_specs=[
            pl.BlockSpec((TOP_K, TD), lambda i: (0, i + off)),
            pl.BlockSpec((TD, D), lambda i: (i + off, 0)),
            pl.BlockSpec((1, 1, 6, D), lambda i: (l, mod_row(i), 0, 0)),
            full(1, D), full(1, D), full(D, D_SHARED), full(D, D_SHARED), full(D_SHARED, D),
            y_spec,
        ],
        out_specs=pl.BlockSpec((TD, D), lambda i: (i, 0)),
        out_shape=jax.ShapeDtypeStruct((n_tok, D), F32),
        compiler_params=_params(("arbitrary",)),
        name="moe_combine",
    )(wn, x, mod, g_pre, g_post, ws_gate, ws_up, ws_down, ybuf)


def _moe_layer(x, h, mod, l, g_pre, g_post, w_router, e_bias, w_gate, w_up, w_down,
               ws_gate, ws_up, ws_down):
    top_e, wn, rk, cnt = _router(h, w_router.T, e_bias.reshape(N_EXPERTS, 1))
    cnt = cnt.reshape(N_EXPERTS).astype(I32)
    padded = (cnt + GM - 1) // GM * GM
    ends = jnp.cumsum(padded)
    offs = ends - padded
    eid = jnp.arange(N_EXPERTS, dtype=I32)[:, None, None]
    slot = rk + jnp.sum(jnp.where(top_e[None] == eid, offs[:, None, None], 0), axis=0)
    nvalid = ends[-1] // GM
    tile_start = jnp.arange(NT_MAX, dtype=I32) * GM
    tile_raw = jnp.sum((tile_start[:, None] >= ends[None, :]).astype(I32), axis=1)
    last = jnp.maximum(nvalid - 1, 0)
    tile_blk = jnp.minimum(jnp.arange(NT_MAX, dtype=I32), last)
    tile_e = jnp.minimum(tile_raw, N_EXPERTS - 1)
    tile_e = jnp.where(jnp.arange(NT_MAX) <= last, tile_e, tile_e[last])
    own = tile_e[:, None] == jnp.arange(N_EXPERTS, dtype=I32)[None, :]
    tile_rows = jnp.sum(jnp.where(own, (offs + cnt)[None, :], 0), axis=1) - tile_start
    tile_nsub = jnp.where(jnp.arange(NT_MAX) <= last, (jnp.clip(tile_rows, 0, GM) + GM_SUB - 1) // GM_SUB, 0)
    slot3 = slot.reshape(TOP_K, T // SC_W, SC_W).transpose(1, 0, 2)
    xg = _sc_dispatch(h.reshape(T, ROWS_PER_TOKEN, LANES), slot3)
    xg = _pad_fill(offs + cnt, padded - cnt, xg).reshape(SP * ROWS_PER_TOKEN, LANES)
    yg = _gmm(tile_e, tile_blk, tile_nsub.astype(I32), xg, l, w_gate, w_up, w_down)
    yg3 = yg.reshape(SP, ROWS_PER_TOKEN, LANES)
    ws = (ws_gate.astype(BF16), ws_up.astype(BF16), ws_down.astype(BF16))
    outs = []
    for first_tok, n_tok in ((0, TP), (TP, TS)):
        ids = slot[:, first_tok:first_tok + n_tok].reshape(TOP_K, n_tok // TD, TD).transpose(1, 0, 2)
        ybuf = _sc_gather(yg3, ids.reshape(TOP_K * n_tok))
        outs.append(_combine(wn, x, mod, l, g_pre, g_post, *ws,
                             ybuf.reshape(TOP_K * n_tok * ROWS_PER_TOKEN, LANES), first_tok, n_tok))
    return outs


def _rope_tables():
    n = DEC_SEQ
    rows = n // GRID_W
    row = jnp.repeat(jnp.arange(rows), GRID_W).astype(F32)
    col = jnp.tile(jnp.arange(GRID_W), rows).astype(F32)
    half = DQK_B // 2
    inv = ROPE_BASE ** (-jnp.arange(0, half, 2, dtype=F32) / half)
    ang_r = row[:, None] * inv
    ang_c = col[:, None] * inv
    ang = jnp.concatenate([ang_r, ang_r, ang_c, ang_c], axis=-1)
    reps = QK_B // DQK_B
    return jnp.tile(jnp.cos(ang), (1, reps)), jnp.tile(jnp.sin(ang), (1, reps))


def _pad_in_proj(w):
    s = [0, Q_A, 2 * Q_A, 2 * Q_A + V_A, 2 * Q_A + 2 * V_A]
    s += [s[-1] + GATE_RANK, s[-1] + 2 * GATE_RANK]
    s += [s[-1] + QK_B, s[-1] + 2 * QK_B, s[-1] + 2 * QK_B + V_B]
    gates = jnp.pad(w[:, s[4]:s[6]], ((0, 0), (0, GL_PAD - 2 * GATE_RANK)))
    return jnp.concatenate([w[:, s[0]:s[4]], gates, w[:, s[6]:s[9]]], axis=1).astype(BF16)


def kernel(x_prompt, x_sample, c, c_ctx, state_gla, cache_k, cache_v, ada_w, ada_b, norm_pre_mix, norm_post_mix, norm_pre_ffn, norm_post_ffn, ab_w_in, gla_w_g2, gla_b_g2, gla_norm_g, diff_lambda, diff_norm_g, ab_w_out, sgu_w_in, sgu_b_in, sgu_norm_g, sgu_w_s, sgu_b_s, sgu_w_out, moe_w_router, moe_e_bias, moe_w_gate, moe_w_up, moe_w_down, moe_ws_gate, moe_ws_up, moe_ws_down):
    depth = ada_w.shape[0]
    xp, xs = x_prompt.reshape(TP, D), x_sample.reshape(TS, D)
    cond = jnp.concatenate([c_ctx[None, :], c, jnp.zeros((8 - 1 - DEC_BATCH, D), F32)], axis=0)
    mod = _modulation(cond, ada_w, ada_b)
    cos, sin = _rope_tables()
    vec = lambda a: a.reshape(1, -1)
    new_s = new_k = new_v = None
    for l in range(depth):
        if l % 2 == 0:
            e = l // 2
            lam_init = 0.8 - 0.6 * math.exp(-0.3 * l)
            a, r_a, gl, q_b, k_b, v_b = _in_proj(xp, xs, mod, l, vec(norm_pre_mix[l]), _pad_in_proj(ab_w_in[e]), cos, sin)
            s0_t = jnp.swapaxes(state_gla[:, e], -1, -2)
            same_head = jnp.eye(H_A, dtype=bool)[None, None, :, None, :, None]
            s0_t = jnp.where(same_head, s0_t[:, :, :, :, None, :], 0.0).reshape(DEC_BATCH, 2, V_A, Q_A)
            o_f, o_bw, s_fin_t = _gla(a, gl, gla_w_g2[e], gla_b_g2[e].reshape(2, 1, Q_A), s0_t)
            o_att = _diff_attention(q_b, k_b, v_b, cache_k, cache_v, diff_lambda[e], lam_init)
            x, h = _mix_out(lam_init, o_f, o_bw, r_a, o_att, xp, xs, mod, l, vec(gla_norm_g[e]), vec(diff_norm_g[e]),
                            ab_w_out[e].astype(BF16), vec(norm_post_mix[l]), vec(norm_pre_ffn[l]))
            new_s = jnp.swapaxes(s_fin_t, -1, -2)
            new_k = k_b[:TP].reshape(BATCH, SEQ, H_B, 2, DQK_B).transpose(0, 2, 3, 1, 4)
            new_v = v_b[:TP].reshape(BATCH, SEQ, H_B, DV_B).transpose(0, 2, 1, 3)
        else:
            o = l // 2
            x, h = _sgu(xp, xs, mod, l, vec(norm_pre_mix[l]), sgu_w_in[o].astype(BF16), vec(sgu_b_in[o]),
                        vec(sgu_norm_g[o]), sgu_w_s[o], sgu_b_s[o].T, sgu_w_out[o].astype(BF16),
                        vec(norm_post_mix[l]), vec(norm_pre_ffn[l]))
        xp, xs = _moe_layer(x, h, mod, l, vec(norm_pre_ffn[l]), vec(norm_post_ffn[l]), moe_w_router[l], moe_e_bias[l],
                            moe_w_gate, moe_w_up, moe_w_down, moe_ws_gate[l], moe_ws_up[l], moe_ws_down[l])
    y_prompt = xp.reshape(BATCH, SEQ, D)
    y_sample = xs.reshape(DEC_BATCH, DEC_SEQ, D)
    return (y_prompt, y_sample, new_s[:, None], new_k[:, None], new_v[:, None])
```

```python
import functools
import math

import jax
import jax.numpy as jnp
from jax import lax
from jax.experimental import pallas as pl
from jax.experimental.pallas import tpu as pltpu
from jax.experimental.pallas import tpu_sc as plsc

F32 = jnp.float32
BF16 = jnp.bfloat16
I32 = jnp.int32

D = 1024
BATCH, SEQ = 32, 256
DEC_BATCH, DEC_SEQ = 4, 2048
PAST_LEN = 256
GRID_W = 64
EPS = 1e-6
TP = BATCH * SEQ
TS = DEC_BATCH * DEC_SEQ
T = TP + TS
H_A, DK_A, DV_A = 4, 64, 128
Q_A, V_A = H_A * DK_A, H_A * DV_A
GATE_RANK, GATE_TAU, GLA_CHUNK = 16, 16.0, 64
H_B, DQK_B, DV_B = 4, 64, 128
QK_B, V_B = H_B * 2 * DQK_B, H_B * DV_B
ROPE_BASE = 10000.0
SGU_DIM, SGU_GROUPS, SGU_CHUNK = 1024, 4, 128
N_EXPERTS, TOP_K, N_GROUPS, TOPK_GROUPS = 64, 8, 8, 4
GROUP_SIZE = N_EXPERTS // N_GROUPS
D_EXPERT, D_SHARED = 256, 256
ROUTED_SCALE = 2.5

TM = 512
TM_SUB = 256
NPT = TP // TM
TILES_PER_DEC = DEC_SEQ // TM
SEG = 256
NSEG = T // SEG
NSEG_P = TP // SEG
SEG_PER_DEC = DEC_SEQ // SEG
TR = 512
TD = 512
GM = 1024
GM_SUB = 256
NT_MAX = T * TOP_K // GM + N_EXPERTS
SP = NT_MAX * GM
GL_PAD = 128
VMEM_LIMIT = 56 * 1024 * 1024
NEG_INF = float("-inf")


def _bdot(a, b):
    return jnp.dot(a.astype(BF16), b.astype(BF16), preferred_element_type=F32)


def _bdot_nt(a, b):
    return lax.dot_general(a.astype(BF16), b.astype(BF16), (((1,), (1,)), ((), ())),
                           preferred_element_type=F32)


def _bdot_tn(a, b):
    return lax.dot_general(a.astype(BF16), b.astype(BF16), (((0,), (0,)), ((), ())),
                           preferred_element_type=F32)


def _split3(x):
    x1 = x.astype(BF16)
    r1 = x - x1.astype(F32)
    x2 = r1.astype(BF16)
    x3 = (r1 - x2.astype(F32)).astype(BF16)
    return x1, x2, x3


def _rms(x, g):
    return x * lax.rsqrt(jnp.mean(x * x, axis=-1, keepdims=True) + EPS) * g


def _silu(x):
    return x * jax.nn.sigmoid(x)


def _mod_row(i):
    return jnp.where(i < NPT, 0, 1 + (i - NPT) // TILES_PER_DEC)


def _params(sem, limit=VMEM_LIMIT):
    return pltpu.CompilerParams(dimension_semantics=sem, vmem_limit_bytes=limit)


def _mod_kernel(c_ref, w_ref, b_ref, o_ref):
    o_ref[0] = _bdot(_silu(c_ref[...]), w_ref[0]) + b_ref[0]


def _modulation(cond, ada_w, ada_b):
    depth = ada_w.shape[0]
    nj = 6
    out = pl.pallas_call(
        _mod_kernel,
        grid=(depth, nj),
        in_specs=[
            pl.BlockSpec((8, D), lambda l, j: (0, 0)),
            pl.BlockSpec((1, D, D), lambda l, j: (l, 0, j)),
            pl.BlockSpec((1, 1, D), lambda l, j: (l, 0, j)),
        ],
        out_specs=pl.BlockSpec((1, 8, D), lambda l, j: (l, 0, j)),
        out_shape=jax.ShapeDtypeStruct((depth, 8, 6 * D), F32),
        compiler_params=_params(("arbitrary", "arbitrary")),
        name="adaln_modulation",
    )(cond, ada_w, ada_b.reshape(depth, 1, 6 * D))
    return out.reshape(depth, 8, 6, D)


_C_A, _C_R, _C_GL, _C_Q, _C_K, _C_V, _C_END = 0, 1024, 1536, 1664, 2176, 2688, 3200


def _rope(x, cos, sin):
    lane = lax.broadcasted_iota(I32, x.shape, 1)
    first = (lane % 32) < 16
    n = x.shape[1]
    xr = jnp.where(first, -pltpu.roll(x, n - 16, 1), pltpu.roll(x, 16, 1))
    return x * cos + xr * sin


def _stream_specs():
    return [pl.BlockSpec((TM, D), lambda i: (jnp.minimum(i, NPT - 1), 0)),
            pl.BlockSpec((TM, D), lambda i: (jnp.maximum(i - NPT, 0), 0))]


def _stream_tile(xp_ref, xs_ref):
    return jnp.where(pl.program_id(0) < NPT, xp_ref[...], xs_ref[...])


def _in_kernel(xp_ref, xs_ref, mod_ref, g_ref, w_ref, cos_ref, sin_ref,
               a_ref, r_ref, gl_ref, q_ref, k_ref, v_ref, ck_ref, cv_ref):
    latent = pl.program_id(0) >= NPT
    m = mod_ref[0, 0]
    x = _stream_tile(xp_ref, xs_ref)
    for s in range(TM // TM_SUB):
        rows = slice(s * TM_SUB, (s + 1) * TM_SUB)
        h = _rms(x[rows], g_ref[...]) * (1.0 + m[1:2]) + m[0:1]
        hb = h.astype(BF16)

        def proj(c0, c1):
            return jnp.dot(hb, w_ref[:, c0:c1], preferred_element_type=F32)

        a_ref[rows, :] = proj(_C_A, _C_R)
        r_ref[rows, :] = proj(_C_R, _C_GL)
        gl_ref[rows, :] = proj(_C_GL, _C_Q)
        v_ref[rows, :] = proj(_C_V, _C_END)
        q = proj(_C_Q, _C_K)
        k = proj(_C_K, _C_V)
        cos = cos_ref[rows, :]
        sin = sin_ref[rows, :]
        q_ref[rows, :] = jnp.where(latent, _rope(q, cos, sin), q)
        k_ref[rows, :] = jnp.where(latent, _rope(k, cos, sin), k)

    @pl.when(jnp.logical_not(latent))
    def _():
        for s in range(TM // SEQ):
            rows = slice(s * SEQ, (s + 1) * SEQ)
            for h in range(H_B):
                cv_ref[s, 0, h] = v_ref[rows, h * DV_B:(h + 1) * DV_B]
                for mp in range(2):
                    ck_ref[s, 0, h, mp] = k_ref[rows, (2 * h + mp) * DQK_B:(2 * h + mp + 1) * DQK_B]


def _in_proj(xp, xs, mod, l, g, w_pad, cos, sin):
    tok = lambda width: pl.BlockSpec((TM, width), lambda i: (i, 0))
    rope_spec = pl.BlockSpec((TM, QK_B), lambda i: (jnp.maximum(i - NPT, 0) % TILES_PER_DEC, 0))
    widths = (1024, 512, GL_PAD, 512, 512, 512)
    seqs = TM // SEQ
    prompt_tile = lambda i: jnp.minimum(i, NPT - 1)
    cache_specs = [pl.BlockSpec((seqs, 1, H_B, 2, SEQ, DQK_B), lambda i: (prompt_tile(i), 0, 0, 0, 0, 0)),
                   pl.BlockSpec((seqs, 1, H_B, SEQ, DV_B), lambda i: (prompt_tile(i), 0, 0, 0, 0))]
    cache_shapes = [jax.ShapeDtypeStruct((BATCH, 1, H_B, 2, SEQ, DQK_B), F32),
                    jax.ShapeDtypeStruct((BATCH, 1, H_B, SEQ, DV_B), F32)]
    return pl.pallas_call(
        _in_kernel,
        grid=(T // TM,),
        in_specs=_stream_specs() + [
            pl.BlockSpec((1, 1, 6, D), lambda i: (l, _mod_row(i), 0, 0)),
            pl.BlockSpec((1, D), lambda i: (0, 0)),
            pl.BlockSpec((D, _C_END), lambda i: (0, 0)),
            rope_spec, rope_spec,
        ],
        out_specs=[tok(w) for w in widths] + cache_specs,
        out_shape=[jax.ShapeDtypeStruct((T, w), F32) for w in widths] + cache_shapes,
        compiler_params=_params(("arbitrary",)),
        name="mixer_ab_in_proj",
    )(xp, xs, mod, g, w_pad, cos, sin)


def _log_sigmoid(x):
    return jnp.minimum(x, 0.0) - jnp.log(1.0 + jnp.exp(-jnp.abs(x)))


def _gla_kernel(af_ref, ab_ref, glf_ref, glb_ref, wg_ref, bg_ref, s0_ref,
                of_ref, ob_ref, sfin_ref, st_ref):
    i = pl.program_id(0)

    @pl.when(i < NSEG_P)
    def _():
        st_ref[...] = jnp.zeros_like(st_ref)

    @pl.when(jnp.logical_and(i >= NSEG_P, (i - NSEG_P) % SEG_PER_DEC == 0))
    def _():
        st_ref[...] = s0_ref[0]

    r = lax.broadcasted_iota(I32, (SEG, SEG), 0)
    c = lax.broadcasted_iota(I32, (SEG, SEG), 1)
    same = (r // GLA_CHUNK) == (c // GLA_CHUNK)
    nchunk = SEG // GLA_CHUNK
    own_head = (lax.broadcasted_iota(I32, (V_A, Q_A), 0) // DV_A) == (lax.broadcasted_iota(I32, (V_A, Q_A), 1) // DK_A)

    for d, (a_ref, gl_ref, o_ref) in enumerate(((af_ref, glf_ref, of_ref), (ab_ref, glb_ref, ob_ref))):
        fwd = d == 0
        gcol = gl_ref[:, d * GATE_RANK:(d + 1) * GATE_RANK]
        la = _log_sigmoid(_bdot(gcol, wg_ref[d]) + bg_ref[d]) / GATE_TAU
        causal = jnp.logical_and(same, (c <= r) if fwd else (c >= r))
        tri = jnp.where(causal, 1.0, 0.0).astype(BF16)
        l1, l2, l3 = _split3(la)
        b_all = (jnp.dot(tri, l1, preferred_element_type=F32)
                 + jnp.dot(tri, l2, preferred_element_type=F32)
                 + jnp.dot(tri, l3, preferred_element_type=F32))
        q_in_all = a_ref[:, 0:Q_A] * (DK_A ** -0.5) * jnp.exp(b_all)
        kd_all = a_ref[:, Q_A:2 * Q_A] * jnp.exp(-b_all)
        intra = []
        for h in range(H_A):
            kc = slice(h * DK_A, (h + 1) * DK_A)
            attn = jnp.where(causal, _bdot_nt(q_in_all[:, kc], kd_all[:, kc]), 0.0)
            intra.append(_bdot(attn, a_ref[:, 2 * Q_A + h * DV_A:2 * Q_A + (h + 1) * DV_A]))
        intra = jnp.concatenate(intra, axis=1)
        state = st_ref[d]
        order = range(nchunk) if fwd else range(nchunk - 1, -1, -1)
        for ch in order:
            r0 = ch * GLA_CHUNK
            rows = slice(r0, r0 + GLA_CHUNK)
            end = r0 + GLA_CHUNK - 1 if fwd else r0
            b_end = b_all[end:end + 1, :]
            kw = a_ref[rows, Q_A:2 * Q_A] * jnp.exp(b_end - b_all[rows, :])
            o_ref[rows, :] = intra[rows, :] + _bdot_nt(q_in_all[rows, :], state)
            kv_t = _bdot_tn(a_ref[rows, 2 * Q_A:2 * Q_A + V_A], kw)
            state = state * jnp.exp(b_end) + jnp.where(own_head, kv_t, 0.0)
        st_ref[d] = state

    @pl.when(i < NSEG_P)
    def _():
        for d in range(2):
            for h in range(H_A):
                sfin_ref[0, d, h] = st_ref[d, h * DV_A:(h + 1) * DV_A, h * DK_A:(h + 1) * DK_A]


def _seg_bwd(i):
    j = i - NSEG_P
    return jnp.where(i < NSEG_P, i, NSEG_P + (j // SEG_PER_DEC) * SEG_PER_DEC + (SEG_PER_DEC - 1 - j % SEG_PER_DEC))


def _gla(a, gl, wg, bg, s0_t):
    seg = lambda width, f: pl.BlockSpec((SEG, width), lambda i: (f(i), 0))
    ident = lambda i: i
    st_block = (1, 2, H_A, DV_A, DK_A)
    return pl.pallas_call(
        _gla_kernel,
        grid=(NSEG,),
        in_specs=[
            seg(D, ident), seg(D, _seg_bwd), seg(GL_PAD, ident), seg(GL_PAD, _seg_bwd),
            pl.BlockSpec((2, GATE_RANK, Q_A), lambda i: (0, 0, 0)),
            pl.BlockSpec((2, 1, Q_A), lambda i: (0, 0, 0)),
            pl.BlockSpec((1, 2, V_A, Q_A), lambda i: (jnp.maximum(i - NSEG_P, 0) // SEG_PER_DEC, 0, 0, 0)),
        ],
        out_specs=[
            seg(V_A, ident), seg(V_A, _seg_bwd),
            pl.BlockSpec(st_block, lambda i: (jnp.minimum(i, NSEG_P - 1), 0, 0, 0, 0)),
        ],
        out_shape=[
            jax.ShapeDtypeStruct((T, V_A), F32),
            jax.ShapeDtypeStruct((T, V_A), F32),
            jax.ShapeDtypeStruct((BATCH, 2, H_A, DV_A, DK_A), F32),
        ],
        scratch_shapes=[pltpu.VMEM((2, V_A, Q_A), F32)],
        compiler_params=_params(("arbitrary",)),
        name="gla_bidir",
    )(a, a, gl, gl, wg, bg, s0_t)


def _diff_lambda(lam_ref, lam_init):
    lp = lam_ref[...]
    s01 = jnp.sum(lp[0:1] * lp[1:2], axis=1, keepdims=True)
    s23 = jnp.sum(lp[2:3] * lp[3:4], axis=1, keepdims=True)
    return jnp.exp(s01) - jnp.exp(s23) + lam_init


def _attn_prompt_kernel(lam_init, q_ref, k_ref, v_ref, lam_ref, o_ref):
    lam = _diff_lambda(lam_ref, lam_init)
    for h in range(H_B):
        ps = []
        for m in range(2):
            cols = slice((2 * h + m) * DQK_B, (2 * h + m + 1) * DQK_B)
            s = _bdot_nt(q_ref[:, cols] * (DQK_B ** -0.5), k_ref[:, cols])
            e = jnp.exp(s - jnp.max(s, axis=1, keepdims=True))
            ps.append(e * (1.0 / jnp.sum(e, axis=1, keepdims=True)))
        w = ps[0] - lam * ps[1]
        o_ref[:, h * DV_B:(h + 1) * DV_B] = _bdot(w, v_ref[:, h * DV_B:(h + 1) * DV_B])


def _attn_sample_kernel(lam_init, q_ref, k_ref, v_ref, ck_ref, cv_ref, lam_ref, o_ref):
    lam = _diff_lambda(lam_ref, lam_init)
    for h in range(H_B):
        parts = []
        for m in range(2):
            cols = slice((2 * h + m) * DQK_B, (2 * h + m + 1) * DQK_B)
            q = q_ref[:, cols] * (DQK_B ** -0.5)
            sc = _bdot_nt(q, ck_ref[0, 0, h, m])
            sn = _bdot_nt(q, k_ref[:, cols])
            mx = jnp.maximum(jnp.max(sc, axis=1, keepdims=True), jnp.max(sn, axis=1, keepdims=True))
            ec = jnp.exp(sc - mx)
            en = jnp.exp(sn - mx)
            inv = (1.0 if m == 0 else -lam) / (jnp.sum(ec, axis=1, keepdims=True) + jnp.sum(en, axis=1, keepdims=True))
            parts.append((ec * inv, en * inv))
        wc = parts[0][0] + parts[1][0]
        wn = parts[0][1] + parts[1][1]
        o_ref[:, h * DV_B:(h + 1) * DV_B] = (_bdot(wc, cv_ref[0, 0, h])
                                             + _bdot(wn, v_ref[:, h * DV_B:(h + 1) * DV_B]))


QB = SEQ
NQB_DEC = DEC_SEQ // QB


def _attn_kernel(lam_init, q_ref, kp_ref, vp_ref, ks_ref, vs_ref, ck_ref, cv_ref, lam_ref, o_ref):
    i = pl.program_id(0)

    @pl.when(i < BATCH)
    def _():
        _attn_prompt_kernel(lam_init, q_ref, kp_ref, vp_ref, lam_ref, o_ref)

    @pl.when(i >= BATCH)
    def _():
        _attn_sample_kernel(lam_init, q_ref, ks_ref, vs_ref, ck_ref, cv_ref, lam_ref, o_ref)


def _diff_attention(q, k, v, cache_k, cache_v, lam_p, lam_init):
    blk = lambda rows, f: pl.BlockSpec((rows, 512), f)
    dec_b = lambda i: jnp.maximum(i - BATCH, 0) // NQB_DEC
    own = lambda i: (i, 0)
    prompt_kv = lambda i: (jnp.minimum(i, BATCH - 1), 0)
    dec_kv = lambda i: (TP // DEC_SEQ + dec_b(i), 0)
    return pl.pallas_call(
        functools.partial(_attn_kernel, lam_init),
        grid=(BATCH + DEC_BATCH * NQB_DEC,),
        in_specs=[
            blk(QB, own), blk(SEQ, prompt_kv), blk(SEQ, prompt_kv), blk(DEC_SEQ, dec_kv), blk(DEC_SEQ, dec_kv),
            pl.BlockSpec((1, 1, H_B, 2, PAST_LEN, DQK_B), lambda i: (dec_b(i), 0, 0, 0, 0, 0)),
            pl.BlockSpec((1, 1, H_B, PAST_LEN, DV_B), lambda i: (dec_b(i), 0, 0, 0, 0)),
            pl.BlockSpec((4, DQK_B), lambda i: (0, 0)),
        ],
        out_specs=blk(QB, own),
        out_shape=jax.ShapeDtypeStruct((T, V_B), F32),
        compiler_params=_params(("arbitrary",)),
        name="diff_attention",
    )(q, k, v, k, v, cache_k, cache_v, lam_p)


def _head_rms(x, g, nheads, width):
    return jnp.concatenate([_rms(x[:, h * width:(h + 1) * width], g) for h in range(nheads)], axis=1)


def _mix_out_kernel(lam_init, of_ref, ob_ref, r_ref, oatt_ref, xp_ref, xs_ref, mod_ref,
                    gg_ref, dg_ref, wo_ref, gp_ref, gffn_ref, o_ref, hp_ref):
    m = mod_ref[0, 0]
    o_a = _head_rms(of_ref[...] + ob_ref[...], gg_ref[...], H_A, DV_A) * _silu(r_ref[...])
    o_b = _head_rms(oatt_ref[...], dg_ref[...], H_B, DV_B) * (1.0 - lam_init)
    out = _bdot(o_a, wo_ref[0:V_A, :]) + _bdot(o_b, wo_ref[V_A:V_A + V_B, :])
    x1 = _stream_tile(xp_ref, xs_ref) + m[2:3] * _rms(out, gp_ref[...])
    o_ref[...] = x1
    _store_token_tiles(hp_ref, _ffn_input_rows(x1, m, gffn_ref[...]))


def _mix_out(lam_init, o_f, o_b, r_a, o_att, xp, xs, mod, l, gla_g, diff_g, w_o, g_post, g_ffn):
    tok = lambda width: pl.BlockSpec((TM, width), lambda i: (i, 0))
    vec = lambda width: pl.BlockSpec((1, width), lambda i: (0, 0))
    return pl.pallas_call(
        functools.partial(_mix_out_kernel, lam_init),
        grid=(T // TM,),
        in_specs=[
            tok(512), tok(512), tok(512), tok(512), *_stream_specs(),
            pl.BlockSpec((1, 1, 6, D), lambda i: (l, _mod_row(i), 0, 0)),
            vec(DV_A), vec(DV_B),
            pl.BlockSpec((V_A + V_B, D), lambda i: (0, 0)),
            vec(D), vec(D),
        ],
        out_specs=[tok(D), pl.BlockSpec((TM * ROWS_PER_TOKEN, LANES), lambda i: (i, 0))],
        out_shape=[jax.ShapeDtypeStruct((T, D), F32), jax.ShapeDtypeStruct((T * ROWS_PER_TOKEN, LANES), U32)],
        compiler_params=_params(("arbitrary",)),
        name="mixer_ab_out",
    )(o_f, o_b, r_a, o_att, xp, xs, mod, gla_g, diff_g, w_o, g_post, g_ffn)


def _gelu_tanh(x):
    return 0.5 * x * (1.0 + jnp.tanh(math.sqrt(2.0 / math.pi) * (x + 0.044715 * (x * x * x))))


def _sgu_kernel(xp_ref, xs_ref, mod_ref, gpre_ref, win_ref, bin_ref, vg_ref, ws_ref, bs_ref,
                wout_ref, gpost_ref, gffn_ref, o_ref, hp_ref, t_ref):
    m = mod_ref[0, 0]
    x = _stream_tile(xp_ref, xs_ref)
    h = _rms(x, gpre_ref[...]) * (1.0 + m[1:2]) + m[0:1]
    z = _gelu_tanh(_bdot(h, win_ref[...]) + bin_ref[...])
    v = _rms(z[:, SGU_DIM:], vg_ref[...])
    gw = SGU_DIM // SGU_GROUPS
    for ch in range(TM // SGU_CHUNK):
        rows = slice(ch * SGU_CHUNK, (ch + 1) * SGU_CHUNK)
        for g in range(SGU_GROUPS):
            cols = slice(g * gw, (g + 1) * gw)
            vs = _bdot(ws_ref[g], v[rows, cols]) + bs_ref[:, g:g + 1]
            t_ref[rows, cols] = (z[rows, cols] * vs).astype(BF16)
    out = jnp.dot(t_ref[...], wout_ref[...], preferred_element_type=F32)
    x1 = x + m[2:3] * _rms(out, gpost_ref[...])
    o_ref[...] = x1
    _store_token_tiles(hp_ref, _ffn_input_rows(x1, m, gffn_ref[...]))


def _sgu(xp, xs, mod, l, g_pre, w_in, b_in, v_g, w_s, b_s_t, w_out, g_post, g_ffn):
    tok = pl.BlockSpec((TM, D), lambda i: (i, 0))
    full = lambda *shape: pl.BlockSpec(shape, lambda i: (0,) * len(shape))
    return pl.pallas_call(
        _sgu_kernel,
        grid=(T // TM,),
        in_specs=_stream_specs() + [
            pl.BlockSpec((1, 1, 6, D), lambda i: (l, _mod_row(i), 0, 0)),
            full(1, D), full(D, 2 * SGU_DIM), full(1, 2 * SGU_DIM), full(1, SGU_DIM),
            full(SGU_GROUPS, SGU_CHUNK, SGU_CHUNK), full(SGU_CHUNK, SGU_GROUPS),
            full(SGU_DIM, D), full(1, D), full(1, D),
        ],
        out_specs=[tok, pl.BlockSpec((TM * ROWS_PER_TOKEN, LANES), lambda i: (i, 0))],
        out_shape=[jax.ShapeDtypeStruct((T, D), F32), jax.ShapeDtypeStruct((T * ROWS_PER_TOKEN, LANES), U32)],
        scratch_shapes=[pltpu.VMEM((TM, SGU_DIM), BF16)],
        compiler_params=_params(("arbitrary",)),
        name="sgu_mixer",
    )(xp, xs, mod, g_pre, w_in, b_in, v_g, w_s, b_s_t, w_out, g_post, g_ffn)


LANES = 128
U32 = jnp.uint32
PACKED = D // 2
ROWS_PER_TOKEN = PACKED // LANES
HIGH_HALF = 0xFFFF0000


def _pack_rows(x):
    bits = lax.bitcast_convert_type(x.astype(BF16).astype(F32), U32)
    return bits[:, :PACKED] | (bits[:, PACKED:] >> 16)


def _unpack_rows(u):
    return (lax.bitcast_convert_type(u & U32(HIGH_HALF), F32), lax.bitcast_convert_type(u << 16, F32))


def _store_token_tiles(ref, u, first=0):
    n = u.shape[0]
    for c in range(ROWS_PER_TOKEN):
        ref[pl.ds(first * ROWS_PER_TOKEN + c, n, stride=ROWS_PER_TOKEN), :] = u[:, c * LANES:(c + 1) * LANES]


def _load_token_tiles(ref, n, first=0):
    return jnp.concatenate([ref[pl.ds(first * ROWS_PER_TOKEN + c, n, stride=ROWS_PER_TOKEN), :]
                            for c in range(ROWS_PER_TOKEN)], axis=1)


def _ffn_input_rows(x, m, g):
    return _pack_rows(_rms(x, g) * (1.0 + m[4:5]) + m[3:4])


def _router_kernel(hp_ref, wr_ref, eb_ref, te_ref, wn_ref, rk_ref, cnt_ref, carry_ref):
    i = pl.program_id(0)

    @pl.when(i == 0)
    def _():
        carry_ref[...] = jnp.zeros_like(carry_ref)

    h_hi, h_lo = (t.astype(BF16) for t in _unpack_rows(_load_token_tiles(hp_ref, TR)))
    w1, w2, _ = _split3(wr_ref[...])
    nt = lambda a, b: lax.dot_general(a, b, (((1,), (1,)), ((), ())), preferred_element_type=F32)
    logits = (nt(w1[:, :PACKED], h_hi) + nt(w1[:, PACKED:], h_lo)
              + nt(w2[:, :PACKED], h_hi) + nt(w2[:, PACKED:], h_lo))
    scores = jax.nn.sigmoid(logits)
    sel = scores + eb_ref[...]

    row8 = lax.broadcasted_iota(I32, (GROUP_SIZE, TR), 0)
    gscore = []
    for g in range(N_GROUPS):
        xg = sel[g * GROUP_SIZE:(g + 1) * GROUP_SIZE]
        m1 = jnp.max(xg, axis=0, keepdims=True)
        i1 = jnp.min(jnp.where(xg == m1, row8, GROUP_SIZE), axis=0, keepdims=True)
        m2 = jnp.max(jnp.where(row8 == i1, NEG_INF, xg), axis=0, keepdims=True)
        gscore.append(m1 + m2)
    pieces = []
    for g in range(N_GROUPS):
        rank = jnp.zeros((1, TR), I32)
        for g2 in range(N_GROUPS):
            if g2 == g:
                continue
            beats = (gscore[g2] >= gscore[g]) if g2 < g else (gscore[g2] > gscore[g])
            rank = rank + beats.astype(I32)
        pieces.append(jnp.where(rank < TOPK_GROUPS, sel[g * GROUP_SIZE:(g + 1) * GROUP_SIZE], NEG_INF))
    cur = jnp.concatenate(pieces, axis=0)

    row = lax.broadcasted_iota(I32, (N_EXPERTS, TR), 0)
    idxs, ws = [], []
    for _ in range(TOP_K):
        mx = jnp.max(cur, axis=0, keepdims=True)
        idx = jnp.min(jnp.where(cur == mx, row, N_EXPERTS), axis=0, keepdims=True)
        hit = row == idx
        ws.append(jnp.sum(jnp.where(hit, scores, 0.0), axis=0, keepdims=True))
        cur = jnp.where(hit, NEG_INF, cur)
        idxs.append(idx)
    mask = jnp.zeros((N_EXPERTS, TR), F32)
    for idx in idxs:
        mask = mask + (row == idx).astype(F32)
    wsum = ws[0]
    for wk in ws[1:]:
        wsum = wsum + wk

    tj = lax.broadcasted_iota(I32, (TR, TR), 0)
    ti = lax.broadcasted_iota(I32, (TR, TR), 1)
    upper = jnp.where(tj < ti, 1.0, 0.0).astype(BF16)
    pos = carry_ref[...] + jnp.dot(mask.astype(BF16), upper, preferred_element_type=F32)
    for k in range(TOP_K):
        hit = row == idxs[k]
        te_ref[k:k + 1, :] = idxs[k]
        wn_ref[k:k + 1, :] = ws[k] / wsum * ROUTED_SCALE
        rk_ref[k:k + 1, :] = jnp.sum(jnp.where(hit, pos, 0.0), axis=0, keepdims=True).astype(I32)
    carry_ref[...] = carry_ref[...] + jnp.sum(mask, axis=1, keepdims=True)
    cnt_ref[...] = carry_ref[...]


def _router(hp, wr_t, e_bias):
    kt = lambda dtype: jax.ShapeDtypeStruct((TOP_K, T), dtype)
    kt_spec = pl.BlockSpec((TOP_K, TR), lambda i: (0, i))
    return pl.pallas_call(
        _router_kernel,
        grid=(T // TR,),
        in_specs=[
            pl.BlockSpec((TR * ROWS_PER_TOKEN, LANES), lambda i: (i, 0)),
            pl.BlockSpec((N_EXPERTS, D), lambda i: (0, 0)),
            pl.BlockSpec((N_EXPERTS, 1), lambda i: (0, 0)),
        ],
        out_specs=[
            kt_spec, kt_spec, kt_spec,
            pl.BlockSpec((N_EXPERTS, 1), lambda i: (0, 0)),
        ],
        out_shape=[
            kt(I32), kt(F32), kt(I32),
            jax.ShapeDtypeStruct((N_EXPERTS, 1), F32),
        ],
        scratch_shapes=[pltpu.VMEM((N_EXPERTS, 1), F32)],
        compiler_params=_params(("arbitrary",)),
        name="moe_router",
    )(hp, wr_t, e_bias)


_PAD_BITS = tuple(1 << b for b in range(GM.bit_length() - 1))


def _pad_fill_kernel(pad_start_ref, pad_len_ref, xg_in_ref, xg_ref, zero_ref, sem):
    del xg_in_ref
    zero_ref[...] = jnp.zeros_like(zero_ref)

    def pad_copies(e):
        start = pad_start_ref[e]
        n = pad_len_ref[e]
        copies = []
        for bit in _PAD_BITS:
            first = start + (n & ~(2 * bit - 1))
            copies.append(((n & bit) != 0, pltpu.make_async_copy(
                zero_ref.at[pl.ds(0, bit)], xg_ref.at[pl.ds(first, bit)], sem)))
        return copies

    def start_e(e, carry):
        for on, cp in pad_copies(e):
            @pl.when(on)
            def _():
                cp.start()
        return carry

    def wait_e(e, carry):
        for on, cp in pad_copies(e):
            @pl.when(on)
            def _():
                cp.wait()
        return carry

    lax.fori_loop(0, N_EXPERTS, start_e, 0)
    lax.fori_loop(0, N_EXPERTS, wait_e, 0)


def _pad_fill(pad_start, pad_len, xg):
    grid_spec = pltpu.PrefetchScalarGridSpec(
        num_scalar_prefetch=2,
        grid=(1,),
        in_specs=[pl.BlockSpec(memory_space=pl.ANY)],
        out_specs=pl.BlockSpec(memory_space=pl.ANY),
        scratch_shapes=[pltpu.VMEM((GM // 2, ROWS_PER_TOKEN, LANES), xg.dtype), pltpu.SemaphoreType.DMA],
    )
    return pl.pallas_call(
        _pad_fill_kernel,
        grid_spec=grid_spec,
        out_shape=jax.ShapeDtypeStruct(xg.shape, xg.dtype),
        input_output_aliases={2: 0},
        compiler_params=_params(("arbitrary",)),
        name="moe_pad_fill",
    )(pad_start, pad_len, xg)


SC_CORES, SC_SUBCORES = 2, 16
SC_WORKERS = SC_CORES * SC_SUBCORES
SC_W = 64


def _sc_worker_id():
    return lax.axis_index("s") * SC_CORES + lax.axis_index("c")


def _sc_dispatch(h3, slot3):
    nchunk = T // SC_WORKERS // SC_W
    mesh = plsc.VectorSubcoreMesh(core_axis_name="c", subcore_axis_name="s")
    tile = (SC_W, ROWS_PER_TOKEN, LANES)

    @functools.partial(
        pl.kernel, mesh=mesh,
        out_type=jax.ShapeDtypeStruct((SP, ROWS_PER_TOKEN, LANES), h3.dtype),
        scratch_types=[pltpu.VMEM((TOP_K, SC_W), I32), pltpu.VMEM((TOP_K, SC_W), I32),
                       pltpu.VMEM(tile, h3.dtype), pltpu.VMEM(tile, h3.dtype),
                       pltpu.SemaphoreType.DMA((2,)), pltpu.SemaphoreType.DMA((2,))],
    )
    def k(h_hbm, slot_hbm, xg_hbm, idx0, idx1, rows0, rows1, lsem, ssem):
        first = _sc_worker_id() * nchunk
        idx = (idx0, idx1)
        rows = (rows0, rows1)

        def loads(j, b):
            blk = first + j
            tok = pl.multiple_of(blk * SC_W, SC_W)
            return (pltpu.make_async_copy(slot_hbm.at[blk], idx[b], lsem.at[b]),
                    pltpu.make_async_copy(h_hbm.at[pl.ds(tok, SC_W)], rows[b], lsem.at[b]))

        def scatters(b):
            return [pltpu.make_async_copy(rows[b], xg_hbm.at[idx[b].at[kk]], ssem.at[b]) for kk in range(TOP_K)]

        for cp in loads(0, 0):
            cp.start()

        @pl.loop(0, nchunk, step=2)
        def _(j):
            for b in (0, 1):
                jj = j + b
                for cp in loads(jj, b):
                    cp.wait()
                for cp in scatters(b):
                    cp.start()

                @pl.when(jj + 1 < nchunk)
                def _():
                    @pl.when(jj >= 1)
                    def _():
                        for cp in scatters(1 - b):
                            cp.wait()
                    for cp in loads(jj + 1, 1 - b):
                        cp.start()

        for b in (0, 1):
            for cp in scatters(b):
                cp.wait()

    return k(h3, slot3)


def _gmm_kernel(tile_e_ref, tile_blk_ref, tile_nsub_ref, x_ref, wg_ref, wu_ref, wd_ref,
                y_ref, wgu_scr, wd_scr):
    j = pl.program_id(0)
    nsub = tile_nsub_ref[j]

    @pl.when(nsub > 0)
    def _():
        prev = tile_e_ref[jnp.maximum(j - 1, 0)]

        @pl.when(jnp.logical_or(j == 0, tile_e_ref[j] != prev))
        def _():
            wgu_scr[:, 0:D_EXPERT] = wg_ref[0, 0].astype(BF16)
            wgu_scr[:, D_EXPERT:2 * D_EXPERT] = wu_ref[0, 0].astype(BF16)
            wd_scr[...] = wd_ref[0, 0].astype(BF16)

    def expert_mlp(s):
        x_hi, x_lo = _unpack_rows(_load_token_tiles(x_ref, GM_SUB, s * GM_SUB))
        gu = (jnp.dot(x_hi.astype(BF16), wgu_scr[0:PACKED, :], preferred_element_type=F32)
              + jnp.dot(x_lo.astype(BF16), wgu_scr[PACKED:D, :], preferred_element_type=F32))
        hid = _silu(gu[:, 0:D_EXPERT]) * gu[:, D_EXPERT:2 * D_EXPERT]
        y = jnp.dot(hid.astype(BF16), wd_scr[...], preferred_element_type=F32)
        _store_token_tiles(y_ref, _pack_rows(y), s * GM_SUB)

    for n in range(1, GM // GM_SUB + 1):
        @pl.when(nsub == n)
        def _():
            for s in range(n):
                expert_mlp(s)
            for s in range(n, GM // GM_SUB):
                _store_token_tiles(y_ref, jnp.zeros((GM_SUB, PACKED), U32), s * GM_SUB)


def _gmm(tile_e, tile_blk, tile_nsub, xg, l, w_gate, w_up, w_down):
    row_tile = pl.BlockSpec((GM * ROWS_PER_TOKEN, LANES), lambda j, te, tb, nv: (tb[j], 0))
    grid_spec = pltpu.PrefetchScalarGridSpec(
        num_scalar_prefetch=3,
        grid=(NT_MAX,),
        in_specs=[
            row_tile,
            pl.BlockSpec((1, 1, D, D_EXPERT), lambda j, te, tb, nv: (l, te[j], 0, 0)),
            pl.BlockSpec((1, 1, D, D_EXPERT), lambda j, te, tb, nv: (l, te[j], 0, 0)),
            pl.BlockSpec((1, 1, D_EXPERT, D), lambda j, te, tb, nv: (l, te[j], 0, 0)),
        ],
        out_specs=row_tile,
        scratch_shapes=[pltpu.VMEM((D, 2 * D_EXPERT), BF16), pltpu.VMEM((D_EXPERT, D), BF16)],
    )
    return pl.pallas_call(
        _gmm_kernel,
        grid_spec=grid_spec,
        out_shape=jax.ShapeDtypeStruct((SP * ROWS_PER_TOKEN, LANES), U32),
        compiler_params=_params(("arbitrary",)),
        name="moe_grouped_matmul",
    )(tile_e, tile_blk, tile_nsub, xg, w_gate, w_up, w_down)


def _sc_gather(table3, idx):
    n_idx = idx.shape[0]
    per_w = n_idx // SC_WORKERS
    nchunk = per_w // SC_W
    mesh = plsc.VectorSubcoreMesh(core_axis_name="c", subcore_axis_name="s")
    tile = (SC_W, ROWS_PER_TOKEN, LANES)

    @functools.partial(
        pl.kernel, mesh=mesh,
        out_type=jax.ShapeDtypeStruct((n_idx, ROWS_PER_TOKEN, LANES), table3.dtype),
        scratch_types=[pltpu.VMEM((per_w,), I32), pltpu.VMEM(tile, table3.dtype), pltpu.VMEM(tile, table3.dtype),
                       pltpu.SemaphoreType.DMA((2,)), pltpu.SemaphoreType.DMA((2,))],
    )
    def k(table_hbm, idx_hbm, out_hbm, idx_v, rows0, rows1, gsem, wsem):
        base = pl.multiple_of(_sc_worker_id() * per_w, per_w)
        rows = (rows0, rows1)
        pltpu.sync_copy(idx_hbm.at[pl.ds(base, per_w)], idx_v)

        def gather(j, b):
            ids = idx_v.at[pl.ds(pl.multiple_of(j * SC_W, SC_W), SC_W)]
            return pltpu.make_async_copy(table_hbm.at[ids], rows[b], gsem.at[b])

        def write(j, b):
            dst = out_hbm.at[pl.ds(pl.multiple_of(base + j * SC_W, SC_W), SC_W)]
            return pltpu.make_async_copy(rows[b], dst, wsem.at[b])

        gather(0, 0).start()

        @pl.loop(0, nchunk, step=2)
        def _(j):
            for b in (0, 1):
                jj = j + b
                gather(jj, b).wait()
                write(jj, b).start()

                @pl.when(jj + 1 < nchunk)
                def _():
                    @pl.when(jj >= 1)
                    def _():
                        write(jj - 1, 1 - b).wait()
                    gather(jj + 1, 1 - b).start()

        write(nchunk - 2, 0).wait()
        write(nchunk - 1, 1).wait()

    return k(table3, idx)


def _combine_kernel(wn_ref, x_ref, mod_ref, gpre_ref, gp_ref, wsg_ref, wsu_ref, wsd_ref, y_ref, o_ref):
    m = mod_ref[0, 0]
    x = x_ref[...]
    hb = (_rms(x, gpre_ref[...]) * (1.0 + m[4:5]) + m[3:4]).astype(BF16)
    hid = (_silu(jnp.dot(hb, wsg_ref[...], preferred_element_type=F32))
           * jnp.dot(hb, wsu_ref[...], preferred_element_type=F32))
    acc = jnp.dot(hid.astype(BF16), wsd_ref[...], preferred_element_type=F32)

    r = lax.broadcasted_iota(I32, (TD, TD), 0)
    c = lax.broadcasted_iota(I32, (TD, TD), 1)
    eye = jnp.where(r == c, 1.0, 0.0).astype(BF16)
    nt = lambda a, b: lax.dot_general(a, b, (((1,), (1,)), ((), ())), preferred_element_type=F32)
    w1, w2, w3 = _split3(wn_ref[...])
    w_t = nt(eye, w1) + nt(eye, w2) + nt(eye, w3)

    acc_hi = acc[:, :PACKED]
    acc_lo = acc[:, PACKED:]
    for k in range(TOP_K):
        y_hi, y_lo = _unpack_rows(_load_token_tiles(y_ref, TD, k * TD))
        acc_hi = acc_hi + y_hi * w_t[:, k:k + 1]
        acc_lo = acc_lo + y_lo * w_t[:, k:k + 1]
    acc = jnp.concatenate([acc_hi, acc_lo], axis=1)
    o_ref[...] = x + m[5:6] * _rms(acc, gp_ref[...])


def _combine(wn, x, mod, l, g_pre, g_post, ws_gate, ws_up, ws_down, ybuf, first_tok, n_tok):
    off = first_tok // TD
    tiles_per_dec = DEC_SEQ // TD
    npd = TP // TD
    mod_row = lambda i: jnp.where(i + off < npd, 0, 1 + (i + off - npd) // tiles_per_dec)
    full = lambda *shape: pl.BlockSpec(shape, lambda i: (0,) * len(shape))
    y_spec = pl.BlockSpec((TOP_K * TD * ROWS_PER_TOKEN, LANES), lambda i: (i, 0))
    return pl.pallas_call(
        _combine_kernel,
        grid=(n_tok // TD,),
        in_specs=[
            pl.BlockSpec((TOP_K, TD), lambda i: (0, i + off)),
            pl.BlockSpec((TD, D), lambda i: (i + off, 0)),
            pl.BlockSpec((1, 1, 6, D), lambda i: (l, mod_row(i), 0, 0)),
            full(1, D), full(1, D), full(D, D_SHARED), full(D, D_SHARED), full(D_SHARED, D),
            y_spec,
        ],
        out_specs=pl.BlockSpec((TD, D), lambda i: (i, 0)),
        out_shape=jax.ShapeDtypeStruct((n_tok, D), F32),
        compiler_params=_params(("arbitrary",)),
        name="moe_combine",
    )(wn, x, mod, g_pre, g_post, ws_gate, ws_up, ws_down, ybuf)


def _moe_layer(x, h, mod, l, g_pre, g_post, w_router, e_bias, w_gate, w_up, w_down,
               ws_gate, ws_up, ws_down):
    top_e, wn, rk, cnt = _router(h, w_router.T, e_bias.reshape(N_EXPERTS, 1))
    cnt = cnt.reshape(N_EXPERTS).astype(I32)
    padded = (cnt + GM - 1) // GM * GM
    ends = jnp.cumsum(padded)
    offs = ends - padded
    eid = jnp.arange(N_EXPERTS, dtype=I32)[:, None, None]
    slot = rk + jnp.sum(jnp.where(top_e[None] == eid, offs[:, None, None], 0), axis=0)
    nvalid = ends[-1] // GM
    tile_start = jnp.arange(NT_MAX, dtype=I32) * GM
    tile_raw = jnp.sum((tile_start[:, None] >= ends[None, :]).astype(I32), axis=1)
    last = jnp.maximum(nvalid - 1, 0)
    tile_blk = jnp.minimum(jnp.arange(NT_MAX, dtype=I32), last)
    tile_e = jnp.minimum(tile_raw, N_EXPERTS - 1)
    tile_e = jnp.where(jnp.arange(NT_MAX) <= last, tile_e, tile_e[last])
    own = tile_e[:, None] == jnp.arange(N_EXPERTS, dtype=I32)[None, :]
    tile_rows = jnp.sum(jnp.where(own, (offs + cnt)[None, :], 0), axis=1) - tile_start
    tile_nsub = jnp.where(jnp.arange(NT_MAX) <= last, (jnp.clip(tile_rows, 0, GM) + GM_SUB - 1) // GM_SUB, 0)
    slot3 = slot.reshape(TOP_K, T // SC_W, SC_W).transpose(1, 0, 2)
    xg = _sc_dispatch(h.reshape(T, ROWS_PER_TOKEN, LANES), slot3)
    xg = _pad_fill(offs + cnt, padded - cnt, xg).reshape(SP * ROWS_PER_TOKEN, LANES)
    yg = _gmm(tile_e, tile_blk, tile_nsub.astype(I32), xg, l, w_gate, w_up, w_down)
    yg3 = yg.reshape(SP, ROWS_PER_TOKEN, LANES)
    ws = (ws_gate.astype(BF16), ws_up.astype(BF16), ws_down.astype(BF16))
    outs = []
    for first_tok, n_tok in ((0, TP), (TP, TS)):
        ids = slot[:, first_tok:first_tok + n_tok].reshape(TOP_K, n_tok // TD, TD).transpose(1, 0, 2)
        ybuf = _sc_gather(yg3, ids.reshape(TOP_K * n_tok))
        outs.append(_combine(wn, x, mod, l, g_pre, g_post, *ws,
                             ybuf.reshape(TOP_K * n_tok * ROWS_PER_TOKEN, LANES), first_tok, n_tok))
    return outs


def _rope_tables():
    n = DEC_SEQ
    rows = n // GRID_W
    row = jnp.repeat(jnp.arange(rows), GRID_W).astype(F32)
    col = jnp.tile(jnp.arange(GRID_W), rows).astype(F32)
    half = DQK_B // 2
    inv = ROPE_BASE ** (-jnp.arange(0, half, 2, dtype=F32) / half)
    ang_r = row[:, None] * inv
    ang_c = col[:, None] * inv
    ang = jnp.concatenate([ang_r, ang_r, ang_c, ang_c], axis=-1)
    reps = QK_B // DQK_B
    return jnp.tile(jnp.cos(ang), (1, reps)), jnp.tile(jnp.sin(ang), (1, reps))


def _pad_in_proj(w):
    s = [0, Q_A, 2 * Q_A, 2 * Q_A + V_A, 2 * Q_A + 2 * V_A]
    s += [s[-1] + GATE_RANK, s[-1] + 2 * GATE_RANK]
    s += [s[-1] + QK_B, s[-1] + 2 * QK_B, s[-1] + 2 * QK_B + V_B]
    gates = jnp.pad(w[:, s[4]:s[6]], ((0, 0), (0, GL_PAD - 2 * GATE_RANK)))
    return jnp.concatenate([w[:, s[0]:s[4]], gates, w[:, s[6]:s[9]]], axis=1).astype(BF16)


def kernel(x_prompt, x_sample, c, c_ctx, state_gla, cache_k, cache_v, ada_w, ada_b, norm_pre_mix, norm_post_mix, norm_pre_ffn, norm_post_ffn, ab_w_in, gla_w_g2, gla_b_g2, gla_norm_g, diff_lambda, diff_norm_g, ab_w_out, sgu_w_in, sgu_b_in, sgu_norm_g, sgu_w_s, sgu_b_s, sgu_w_out, moe_w_router, moe_e_bias, moe_w_gate, moe_w_up, moe_w_down, moe_ws_gate, moe_ws_up, moe_ws_down):
    depth = ada_w.shape[0]
    xp, xs = x_prompt.reshape(TP, D), x_sample.reshape(TS, D)
    cond = jnp.concatenate([c_ctx[None, :], c, jnp.zeros((8 - 1 - DEC_BATCH, D), F32)], axis=0)
    mod = _modulation(cond, ada_w, ada_b)
    cos, sin = _rope_tables()
    vec = lambda a: a.reshape(1, -1)
    new_s = new_k = new_v = None
    for l in range(depth):
        if l % 2 == 0:
            e = l // 2
            lam_init = 0.8 - 0.6 * math.exp(-0.3 * l)
            a, r_a, gl, q_b, k_b, v_b, new_k, new_v = _in_proj(xp, xs, mod, l, vec(norm_pre_mix[l]),
                                                               _pad_in_proj(ab_w_in[e]), cos, sin)
            s0_t = jnp.swapaxes(state_gla[:, e], -1, -2)
            same_head = jnp.eye(H_A, dtype=bool)[None, None, :, None, :, None]
            s0_t = jnp.where(same_head, s0_t[:, :, :, :, None, :], 0.0).reshape(DEC_BATCH, 2, V_A, Q_A)
            o_f, o_bw, s_fin_t = _gla(a, gl, gla_w_g2[e], gla_b_g2[e].reshape(2, 1, Q_A), s0_t)
            o_att = _diff_attention(q_b, k_b, v_b, cache_k, cache_v, diff_lambda[e], lam_init)
            x, h = _mix_out(lam_init, o_f, o_bw, r_a, o_att, xp, xs, mod, l, vec(gla_norm_g[e]), vec(diff_norm_g[e]),
                            ab_w_out[e].astype(BF16), vec(norm_post_mix[l]), vec(norm_pre_ffn[l]))
            new_s = jnp.swapaxes(s_fin_t, -1, -2)[:, None]
        else:
            o = l // 2
            x, h = _sgu(xp, xs, mod, l, vec(norm_pre_mix[l]), sgu_w_in[o].astype(BF16), vec(sgu_b_in[o]),
                        vec(sgu_norm_g[o]), sgu_w_s[o], sgu_b_s[o].T, sgu_w_out[o].astype(BF16),
                        vec(norm_post_mix[l]), vec(norm_pre_ffn[l]))
        xp, xs = _moe_layer(x, h, mod, l, vec(norm_pre_ffn[l]), vec(norm_post_ffn[l]), moe_w_router[l], moe_e_bias[l],
                            moe_w_gate, moe_w_up, moe_w_down, moe_ws_gate[l], moe_ws_up[l], moe_ws_down[l])
    y_prompt = xp.reshape(BATCH, SEQ, D)
    y_sample = xs.reshape(DEC_BATCH, DEC_SEQ, D)
    return (y_prompt, y_sample, new_s, new_k, new_v)
```

```python
import functools
import math

import jax
import jax.numpy as jnp
from jax import lax
from jax.experimental import pallas as pl
from jax.experimental.pallas import tpu as pltpu
from jax.experimental.pallas import tpu_sc as plsc

F32 = jnp.float32
BF16 = jnp.bfloat16
I32 = jnp.int32

D = 1024
BATCH, SEQ = 32, 256
DEC_BATCH, DEC_SEQ = 4, 2048
PAST_LEN = 256
GRID_W = 64
EPS = 1e-6
TP = BATCH * SEQ
TS = DEC_BATCH * DEC_SEQ
T = TP + TS
H_A, DK_A, DV_A = 4, 64, 128
Q_A, V_A = H_A * DK_A, H_A * DV_A
GATE_RANK, GATE_TAU, GLA_CHUNK = 16, 16.0, 64
H_B, DQK_B, DV_B = 4, 64, 128
QK_B, V_B = H_B * 2 * DQK_B, H_B * DV_B
ROPE_BASE = 10000.0
SGU_DIM, SGU_GROUPS, SGU_CHUNK = 1024, 4, 128
N_EXPERTS, TOP_K, N_GROUPS, TOPK_GROUPS = 64, 8, 8, 4
GROUP_SIZE = N_EXPERTS // N_GROUPS
D_EXPERT, D_SHARED = 256, 256
ROUTED_SCALE = 2.5

TM = 512
TM_SUB = 256
NPT = TP // TM
TILES_PER_DEC = DEC_SEQ // TM
SEG = 256
NSEG = T // SEG
NSEG_P = TP // SEG
SEG_PER_DEC = DEC_SEQ // SEG
TR = 512
TD = 512
GM = 1024
GM_SUB = 256
NT_MAX = T * TOP_K // GM + N_EXPERTS
SP = NT_MAX * GM
GL_PAD = 128
VMEM_LIMIT = 56 * 1024 * 1024
NEG_INF = float("-inf")


def _bdot(a, b):
    return jnp.dot(a.astype(BF16), b.astype(BF16), preferred_element_type=F32)


def _bdot_nt(a, b):
    return lax.dot_general(a.astype(BF16), b.astype(BF16), (((1,), (1,)), ((), ())),
                           preferred_element_type=F32)


def _bdot_tn(a, b):
    return lax.dot_general(a.astype(BF16), b.astype(BF16), (((0,), (0,)), ((), ())),
                           preferred_element_type=F32)


def _split3(x):
    x1 = x.astype(BF16)
    r1 = x - x1.astype(F32)
    x2 = r1.astype(BF16)
    x3 = (r1 - x2.astype(F32)).astype(BF16)
    return x1, x2, x3


def _rms(x, g):
    return x * lax.rsqrt(jnp.mean(x * x, axis=-1, keepdims=True) + EPS) * g


def _silu(x):
    return x * jax.nn.sigmoid(x)


def _mod_row(i):
    return jnp.where(i < NPT, 0, 1 + (i - NPT) // TILES_PER_DEC)


def _params(sem, limit=VMEM_LIMIT):
    return pltpu.CompilerParams(dimension_semantics=sem, vmem_limit_bytes=limit)


def _mod_kernel(c_ref, w_ref, b_ref, o_ref):
    o_ref[0] = _bdot(_silu(c_ref[...]), w_ref[0]) + b_ref[0]


def _modulation(cond, ada_w, ada_b):
    depth = ada_w.shape[0]
    nj = 6
    out = pl.pallas_call(
        _mod_kernel,
        grid=(depth, nj),
        in_specs=[
            pl.BlockSpec((8, D), lambda l, j: (0, 0)),
            pl.BlockSpec((1, D, D), lambda l, j: (l, 0, j)),
            pl.BlockSpec((1, 1, D), lambda l, j: (l, 0, j)),
        ],
        out_specs=pl.BlockSpec((1, 8, D), lambda l, j: (l, 0, j)),
        out_shape=jax.ShapeDtypeStruct((depth, 8, 6 * D), F32),
        compiler_params=_params(("arbitrary", "arbitrary")),
        name="adaln_modulation",
    )(cond, ada_w, ada_b.reshape(depth, 1, 6 * D))
    return out.reshape(depth, 8, 6, D)


_C_A, _C_R, _C_GL, _C_Q, _C_K, _C_V, _C_END = 0, 1024, 1536, 1664, 2176, 2688, 3200


def _rope(x, cos, sin):
    lane = lax.broadcasted_iota(I32, x.shape, 1)
    first = (lane % 32) < 16
    n = x.shape[1]
    xr = jnp.where(first, -pltpu.roll(x, n - 16, 1), pltpu.roll(x, 16, 1))
    return x * cos + xr * sin


def _stream_specs():
    return [pl.BlockSpec((TM, D), lambda i: (jnp.minimum(i, NPT - 1), 0)),
            pl.BlockSpec((TM, D), lambda i: (jnp.maximum(i - NPT, 0), 0))]


def _stream_tile(xp_ref, xs_ref):
    return jnp.where(pl.program_id(0) < NPT, xp_ref[...], xs_ref[...])


def _in_kernel(xp_ref, xs_ref, mod_ref, g_ref, w_ref, cos_ref, sin_ref,
               a_ref, r_ref, gl_ref, q_ref, k_ref, v_ref, ck_ref, cv_ref):
    latent = pl.program_id(0) >= NPT
    m = mod_ref[0, 0]
    x = _stream_tile(xp_ref, xs_ref)
    for s in range(TM // TM_SUB):
        rows = slice(s * TM_SUB, (s + 1) * TM_SUB)
        h = _rms(x[rows], g_ref[...]) * (1.0 + m[1:2]) + m[0:1]
        hb = h.astype(BF16)

        def proj(c0, c1):
            return jnp.dot(hb, w_ref[:, c0:c1], preferred_element_type=F32)

        a_ref[rows, :] = proj(_C_A, _C_R)
        r_ref[rows, :] = proj(_C_R, _C_GL)
        gl_ref[rows, :] = proj(_C_GL, _C_Q)
        v_ref[rows, :] = proj(_C_V, _C_END)
        q = proj(_C_Q, _C_K)
        k = proj(_C_K, _C_V)
        cos = cos_ref[rows, :]
        sin = sin_ref[rows, :]
        q_ref[rows, :] = jnp.where(latent, _rope(q, cos, sin), q)
        k_ref[rows, :] = jnp.where(latent, _rope(k, cos, sin), k)

    @pl.when(jnp.logical_not(latent))
    def _():
        for s in range(TM // SEQ):
            rows = slice(s * SEQ, (s + 1) * SEQ)
            for h in range(H_B):
                cv_ref[s, 0, h] = v_ref[rows, h * DV_B:(h + 1) * DV_B]
                for mp in range(2):
                    ck_ref[s, 0, h, mp] = k_ref[rows, (2 * h + mp) * DQK_B:(2 * h + mp + 1) * DQK_B]


def _in_proj(xp, xs, mod, l, g, w_pad, cos, sin):
    tok = lambda width: pl.BlockSpec((TM, width), lambda i: (i, 0))
    rope_spec = pl.BlockSpec((TM, QK_B), lambda i: (jnp.maximum(i - NPT, 0) % TILES_PER_DEC, 0))
    widths = (1024, 512, GL_PAD, 512, 512, 512)
    seqs = TM // SEQ
    prompt_tile = lambda i: jnp.minimum(i, NPT - 1)
    cache_specs = [pl.BlockSpec((seqs, 1, H_B, 2, SEQ, DQK_B), lambda i: (prompt_tile(i), 0, 0, 0, 0, 0)),
                   pl.BlockSpec((seqs, 1, H_B, SEQ, DV_B), lambda i: (prompt_tile(i), 0, 0, 0, 0))]
    cache_shapes = [jax.ShapeDtypeStruct((BATCH, 1, H_B, 2, SEQ, DQK_B), F32),
                    jax.ShapeDtypeStruct((BATCH, 1, H_B, SEQ, DV_B), F32)]
    return pl.pallas_call(
        _in_kernel,
        grid=(T // TM,),
        in_specs=_stream_specs() + [
            pl.BlockSpec((1, 1, 6, D), lambda i: (l, _mod_row(i), 0, 0)),
            pl.BlockSpec((1, D), lambda i: (0, 0)),
            pl.BlockSpec((D, _C_END), lambda i: (0, 0)),
            rope_spec, rope_spec,
        ],
        out_specs=[tok(w) for w in widths] + cache_specs,
        out_shape=[jax.ShapeDtypeStruct((T, w), F32) for w in widths] + cache_shapes,
        compiler_params=_params(("arbitrary",)),
        name="mixer_ab_in_proj",
    )(xp, xs, mod, g, w_pad, cos, sin)


def _log_sigmoid(x):
    return jnp.minimum(x, 0.0) - jnp.log(1.0 + jnp.exp(-jnp.abs(x)))


def _gla_kernel(af_ref, ab_ref, glf_ref, glb_ref, wg_ref, bg_ref, s0_ref,
                of_ref, ob_ref, sfin_ref, st_ref):
    i = pl.program_id(0)

    @pl.when(i < NSEG_P)
    def _():
        st_ref[...] = jnp.zeros_like(st_ref)

    @pl.when(jnp.logical_and(i >= NSEG_P, (i - NSEG_P) % SEG_PER_DEC == 0))
    def _():
        st_ref[...] = s0_ref[0]

    r = lax.broadcasted_iota(I32, (SEG, SEG), 0)
    c = lax.broadcasted_iota(I32, (SEG, SEG), 1)
    same = (r // GLA_CHUNK) == (c // GLA_CHUNK)
    nchunk = SEG // GLA_CHUNK
    own_head = (lax.broadcasted_iota(I32, (V_A, Q_A), 0) // DV_A) == (lax.broadcasted_iota(I32, (V_A, Q_A), 1) // DK_A)

    for d, (a_ref, gl_ref, o_ref) in enumerate(((af_ref, glf_ref, of_ref), (ab_ref, glb_ref, ob_ref))):
        fwd = d == 0
        gcol = gl_ref[:, d * GATE_RANK:(d + 1) * GATE_RANK]
        la = _log_sigmoid(_bdot(gcol, wg_ref[d]) + bg_ref[d]) / GATE_TAU
        causal = jnp.logical_and(same, (c <= r) if fwd else (c >= r))
        tri = jnp.where(causal, 1.0, 0.0).astype(BF16)
        l1, l2, l3 = _split3(la)
        b_all = (jnp.dot(tri, l1, preferred_element_type=F32)
                 + jnp.dot(tri, l2, preferred_element_type=F32)
                 + jnp.dot(tri, l3, preferred_element_type=F32))
        q_in_all = a_ref[:, 0:Q_A] * (DK_A ** -0.5) * jnp.exp(b_all)
        kd_all = a_ref[:, Q_A:2 * Q_A] * jnp.exp(-b_all)
        intra = []
        for h in range(H_A):
            kc = slice(h * DK_A, (h + 1) * DK_A)
            attn = jnp.where(causal, _bdot_nt(q_in_all[:, kc], kd_all[:, kc]), 0.0)
            intra.append(_bdot(attn, a_ref[:, 2 * Q_A + h * DV_A:2 * Q_A + (h + 1) * DV_A]))
        intra = jnp.concatenate(intra, axis=1)
        state = st_ref[d]
        order = range(nchunk) if fwd else range(nchunk - 1, -1, -1)
        for ch in order:
            r0 = ch * GLA_CHUNK
            rows = slice(r0, r0 + GLA_CHUNK)
            end = r0 + GLA_CHUNK - 1 if fwd else r0
            b_end = b_all[end:end + 1, :]
            kw = a_ref[rows, Q_A:2 * Q_A] * jnp.exp(b_end - b_all[rows, :])
            o_ref[rows, :] = intra[rows, :] + _bdot_nt(q_in_all[rows, :], state)
            kv_t = _bdot_tn(a_ref[rows, 2 * Q_A:2 * Q_A + V_A], kw)
            state = state * jnp.exp(b_end) + jnp.where(own_head, kv_t, 0.0)
        st_ref[d] = state

    @pl.when(i < NSEG_P)
    def _():
        for d in range(2):
            for h in range(H_A):
                sfin_ref[0, d, h] = st_ref[d, h * DV_A:(h + 1) * DV_A, h * DK_A:(h + 1) * DK_A]


def _seg_bwd(i):
    j = i - NSEG_P
    return jnp.where(i < NSEG_P, i, NSEG_P + (j // SEG_PER_DEC) * SEG_PER_DEC + (SEG_PER_DEC - 1 - j % SEG_PER_DEC))


def _gla(a, gl, wg, bg, s0_t):
    seg = lambda width, f: pl.BlockSpec((SEG, width), lambda i: (f(i), 0))
    ident = lambda i: i
    st_block = (1, 2, H_A, DV_A, DK_A)
    return pl.pallas_call(
        _gla_kernel,
        grid=(NSEG,),
        in_specs=[
            seg(D, ident), seg(D, _seg_bwd), seg(GL_PAD, ident), seg(GL_PAD, _seg_bwd),
            pl.BlockSpec((2, GATE_RANK, Q_A), lambda i: (0, 0, 0)),
            pl.BlockSpec((2, 1, Q_A), lambda i: (0, 0, 0)),
            pl.BlockSpec((1, 2, V_A, Q_A), lambda i: (jnp.maximum(i - NSEG_P, 0) // SEG_PER_DEC, 0, 0, 0)),
        ],
        out_specs=[
            seg(V_A, ident), seg(V_A, _seg_bwd),
            pl.BlockSpec(st_block, lambda i: (jnp.minimum(i, NSEG_P - 1), 0, 0, 0, 0)),
        ],
        out_shape=[
            jax.ShapeDtypeStruct((T, V_A), F32),
            jax.ShapeDtypeStruct((T, V_A), F32),
            jax.ShapeDtypeStruct((BATCH, 2, H_A, DV_A, DK_A), F32),
        ],
        scratch_shapes=[pltpu.VMEM((2, V_A, Q_A), F32)],
        compiler_params=_params(("arbitrary",)),
        name="gla_bidir",
    )(a, a, gl, gl, wg, bg, s0_t)


def _diff_lambda(lam_ref, lam_init):
    lp = lam_ref[...]
    s01 = jnp.sum(lp[0:1] * lp[1:2], axis=1, keepdims=True)
    s23 = jnp.sum(lp[2:3] * lp[3:4], axis=1, keepdims=True)
    return jnp.exp(s01) - jnp.exp(s23) + lam_init


def _attn_prompt_kernel(lam_init, q_ref, k_ref, v_ref, lam_ref, o_ref):
    lam = _diff_lambda(lam_ref, lam_init)
    for h in range(H_B):
        ps = []
        for m in range(2):
            cols = slice((2 * h + m) * DQK_B, (2 * h + m + 1) * DQK_B)
            s = _bdot_nt(q_ref[:, cols] * (DQK_B ** -0.5), k_ref[:, cols])
            e = jnp.exp(s - jnp.max(s, axis=1, keepdims=True))
            ps.append(e * (1.0 / jnp.sum(e, axis=1, keepdims=True)))
        w = ps[0] - lam * ps[1]
        o_ref[:, h * DV_B:(h + 1) * DV_B] = _bdot(w, v_ref[:, h * DV_B:(h + 1) * DV_B])


def _attn_sample_kernel(lam_init, q_ref, k_ref, v_ref, ck_ref, cv_ref, lam_ref, o_ref):
    lam = _diff_lambda(lam_ref, lam_init)
    for h in range(H_B):
        parts = []
        for m in range(2):
            cols = slice((2 * h + m) * DQK_B, (2 * h + m + 1) * DQK_B)
            q = q_ref[:, cols] * (DQK_B ** -0.5)
            sc = _bdot_nt(q, ck_ref[0, 0, h, m])
            sn = _bdot_nt(q, k_ref[:, cols])
            mx = jnp.maximum(jnp.max(sc, axis=1, keepdims=True), jnp.max(sn, axis=1, keepdims=True))
            ec = jnp.exp(sc - mx)
            en = jnp.exp(sn - mx)
            inv = (1.0 if m == 0 else -lam) / (jnp.sum(ec, axis=1, keepdims=True) + jnp.sum(en, axis=1, keepdims=True))
            parts.append((ec * inv, en * inv))
        wc = parts[0][0] + parts[1][0]
        wn = parts[0][1] + parts[1][1]
        o_ref[:, h * DV_B:(h + 1) * DV_B] = (_bdot(wc, cv_ref[0, 0, h])
                                             + _bdot(wn, v_ref[:, h * DV_B:(h + 1) * DV_B]))


QB = SEQ
NQB_DEC = DEC_SEQ // QB


def _attn_kernel(lam_init, q_ref, kp_ref, vp_ref, ks_ref, vs_ref, ck_ref, cv_ref, lam_ref, o_ref):
    i = pl.program_id(0)

    @pl.when(i < BATCH)
    def _():
        _attn_prompt_kernel(lam_init, q_ref, kp_ref, vp_ref, lam_ref, o_ref)

    @pl.when(i >= BATCH)
    def _():
        _attn_sample_kernel(lam_init, q_ref, ks_ref, vs_ref, ck_ref, cv_ref, lam_ref, o_ref)


def _diff_attention(q, k, v, cache_k, cache_v, lam_p, lam_init):
    blk = lambda rows, f: pl.BlockSpec((rows, 512), f)
    dec_b = lambda i: jnp.maximum(i - BATCH, 0) // NQB_DEC
    own = lambda i: (i, 0)
    prompt_kv = lambda i: (jnp.minimum(i, BATCH - 1), 0)
    dec_kv = lambda i: (TP // DEC_SEQ + dec_b(i), 0)
    return pl.pallas_call(
        functools.partial(_attn_kernel, lam_init),
        grid=(BATCH + DEC_BATCH * NQB_DEC,),
        in_specs=[
            blk(QB, own), blk(SEQ, prompt_kv), blk(SEQ, prompt_kv), blk(DEC_SEQ, dec_kv), blk(DEC_SEQ, dec_kv),
            pl.BlockSpec((1, 1, H_B, 2, PAST_LEN, DQK_B), lambda i: (dec_b(i), 0, 0, 0, 0, 0)),
            pl.BlockSpec((1, 1, H_B, PAST_LEN, DV_B), lambda i: (dec_b(i), 0, 0, 0, 0)),
            pl.BlockSpec((4, DQK_B), lambda i: (0, 0)),
        ],
        out_specs=blk(QB, own),
        out_shape=jax.ShapeDtypeStruct((T, V_B), F32),
        compiler_params=_params(("arbitrary",)),
        name="diff_attention",
    )(q, k, v, k, v, cache_k, cache_v, lam_p)


def _head_rms(x, g, nheads, width):
    return jnp.concatenate([_rms(x[:, h * width:(h + 1) * width], g) for h in range(nheads)], axis=1)


def _mix_out_kernel(lam_init, of_ref, ob_ref, r_ref, oatt_ref, xp_ref, xs_ref, mod_ref,
                    gg_ref, dg_ref, wo_ref, gp_ref, gffn_ref, o_ref, hp_ref):
    m = mod_ref[0, 0]
    o_a = _head_rms(of_ref[...] + ob_ref[...], gg_ref[...], H_A, DV_A) * _silu(r_ref[...])
    o_b = _head_rms(oatt_ref[...], dg_ref[...], H_B, DV_B) * (1.0 - lam_init)
    out = _bdot(o_a, wo_ref[0:V_A, :]) + _bdot(o_b, wo_ref[V_A:V_A + V_B, :])
    x1 = _stream_tile(xp_ref, xs_ref) + m[2:3] * _rms(out, gp_ref[...])
    o_ref[...] = x1
    _store_token_tiles(hp_ref, _ffn_input_rows(x1, m, gffn_ref[...]))


def _mix_out(lam_init, o_f, o_b, r_a, o_att, xp, xs, mod, l, gla_g, diff_g, w_o, g_post, g_ffn):
    tok = lambda width: pl.BlockSpec((TM, width), lambda i: (i, 0))
    vec = lambda width: pl.BlockSpec((1, width), lambda i: (0, 0))
    return pl.pallas_call(
        functools.partial(_mix_out_kernel, lam_init),
        grid=(T // TM,),
        in_specs=[
            tok(512), tok(512), tok(512), tok(512), *_stream_specs(),
            pl.BlockSpec((1, 1, 6, D), lambda i: (l, _mod_row(i), 0, 0)),
            vec(DV_A), vec(DV_B),
            pl.BlockSpec((V_A + V_B, D), lambda i: (0, 0)),
            vec(D), vec(D),
        ],
        out_specs=[tok(D), pl.BlockSpec((TM * ROWS_PER_TOKEN, LANES), lambda i: (i, 0))],
        out_shape=[jax.ShapeDtypeStruct((T, D), F32), jax.ShapeDtypeStruct((T * ROWS_PER_TOKEN, LANES), U32)],
        compiler_params=_params(("arbitrary",)),
        name="mixer_ab_out",
    )(o_f, o_b, r_a, o_att, xp, xs, mod, gla_g, diff_g, w_o, g_post, g_ffn)


def _gelu_tanh(x):
    return 0.5 * x * (1.0 + jnp.tanh(math.sqrt(2.0 / math.pi) * (x + 0.044715 * (x * x * x))))


def _sgu_kernel(xp_ref, xs_ref, mod_ref, gpre_ref, win_ref, bin_ref, vg_ref, ws_ref, bs_ref,
                wout_ref, gpost_ref, gffn_ref, o_ref, hp_ref, t_ref):
    m = mod_ref[0, 0]
    x = _stream_tile(xp_ref, xs_ref)
    h = _rms(x, gpre_ref[...]) * (1.0 + m[1:2]) + m[0:1]
    z = _gelu_tanh(_bdot(h, win_ref[...]) + bin_ref[...])
    v = _rms(z[:, SGU_DIM:], vg_ref[...])
    gw = SGU_DIM // SGU_GROUPS
    for ch in range(TM // SGU_CHUNK):
        rows = slice(ch * SGU_CHUNK, (ch + 1) * SGU_CHUNK)
        for g in range(SGU_GROUPS):
            cols = slice(g * gw, (g + 1) * gw)
            vs = _bdot(ws_ref[g], v[rows, cols]) + bs_ref[:, g:g + 1]
            t_ref[rows, cols] = (z[rows, cols] * vs).astype(BF16)
    out = jnp.dot(t_ref[...], wout_ref[...], preferred_element_type=F32)
    x1 = x + m[2:3] * _rms(out, gpost_ref[...])
    o_ref[...] = x1
    _store_token_tiles(hp_ref, _ffn_input_rows(x1, m, gffn_ref[...]))


def _sgu(xp, xs, mod, l, g_pre, w_in, b_in, v_g, w_s, b_s_t, w_out, g_post, g_ffn):
    tok = pl.BlockSpec((TM, D), lambda i: (i, 0))
    full = lambda *shape: pl.BlockSpec(shape, lambda i: (0,) * len(shape))
    return pl.pallas_call(
        _sgu_kernel,
        grid=(T // TM,),
        in_specs=_stream_specs() + [
            pl.BlockSpec((1, 1, 6, D), lambda i: (l, _mod_row(i), 0, 0)),
            full(1, D), full(D, 2 * SGU_DIM), full(1, 2 * SGU_DIM), full(1, SGU_DIM),
            full(SGU_GROUPS, SGU_CHUNK, SGU_CHUNK), full(SGU_CHUNK, SGU_GROUPS),
            full(SGU_DIM, D), full(1, D), full(1, D),
        ],
        out_specs=[tok, pl.BlockSpec((TM * ROWS_PER_TOKEN, LANES), lambda i: (i, 0))],
        out_shape=[jax.ShapeDtypeStruct((T, D), F32), jax.ShapeDtypeStruct((T * ROWS_PER_TOKEN, LANES), U32)],
        scratch_shapes=[pltpu.VMEM((TM, SGU_DIM), BF16)],
        compiler_params=_params(("arbitrary",)),
        name="sgu_mixer",
    )(xp, xs, mod, g_pre, w_in, b_in, v_g, w_s, b_s_t, w_out, g_post, g_ffn)


LANES = 128
U32 = jnp.uint32
PACKED = D // 2
ROWS_PER_TOKEN = PACKED // LANES
HIGH_HALF = 0xFFFF0000


def _pack_rows(x):
    bits = lax.bitcast_convert_type(x.astype(BF16).astype(F32), U32)
    return bits[:, :PACKED] | (bits[:, PACKED:] >> 16)


def _unpack_rows(u):
    return (lax.bitcast_convert_type(u & U32(HIGH_HALF), F32), lax.bitcast_convert_type(u << 16, F32))


def _store_token_tiles(ref, u, first=0):
    n = u.shape[0]
    for c in range(ROWS_PER_TOKEN):
        ref[pl.ds(first * ROWS_PER_TOKEN + c, n, stride=ROWS_PER_TOKEN), :] = u[:, c * LANES:(c + 1) * LANES]


def _load_token_tiles(ref, n, first=0):
    return jnp.concatenate([ref[pl.ds(first * ROWS_PER_TOKEN + c, n, stride=ROWS_PER_TOKEN), :]
                            for c in range(ROWS_PER_TOKEN)], axis=1)


def _ffn_input_rows(x, m, g):
    return _pack_rows(_rms(x, g) * (1.0 + m[4:5]) + m[3:4])


def _router_kernel(hp_ref, wr_ref, eb_ref, te_ref, wn_ref, rk_ref, cnt_ref, carry_ref):
    i = pl.program_id(0)

    @pl.when(i == 0)
    def _():
        carry_ref[...] = jnp.zeros_like(carry_ref)

    h_hi, h_lo = (t.astype(BF16) for t in _unpack_rows(_load_token_tiles(hp_ref, TR)))
    w1, w2, _ = _split3(wr_ref[...])
    nt = lambda a, b: lax.dot_general(a, b, (((1,), (1,)), ((), ())), preferred_element_type=F32)
    logits = (nt(w1[:, :PACKED], h_hi) + nt(w1[:, PACKED:], h_lo)
              + nt(w2[:, :PACKED], h_hi) + nt(w2[:, PACKED:], h_lo))
    scores = jax.nn.sigmoid(logits)
    sel = scores + eb_ref[...]

    row8 = lax.broadcasted_iota(I32, (GROUP_SIZE, TR), 0)
    gscore = []
    for g in range(N_GROUPS):
        xg = sel[g * GROUP_SIZE:(g + 1) * GROUP_SIZE]
        m1 = jnp.max(xg, axis=0, keepdims=True)
        i1 = jnp.min(jnp.where(xg == m1, row8, GROUP_SIZE), axis=0, keepdims=True)
        m2 = jnp.max(jnp.where(row8 == i1, NEG_INF, xg), axis=0, keepdims=True)
        gscore.append(m1 + m2)
    pieces = []
    for g in range(N_GROUPS):
        rank = jnp.zeros((1, TR), I32)
        for g2 in range(N_GROUPS):
            if g2 == g:
                continue
            beats = (gscore[g2] >= gscore[g]) if g2 < g else (gscore[g2] > gscore[g])
            rank = rank + beats.astype(I32)
        pieces.append(jnp.where(rank < TOPK_GROUPS, sel[g * GROUP_SIZE:(g + 1) * GROUP_SIZE], NEG_INF))
    cur = jnp.concatenate(pieces, axis=0)

    row = lax.broadcasted_iota(I32, (N_EXPERTS, TR), 0)
    idxs, ws = [], []
    for _ in range(TOP_K):
        mx = jnp.max(cur, axis=0, keepdims=True)
        idx = jnp.min(jnp.where(cur == mx, row, N_EXPERTS), axis=0, keepdims=True)
        hit = row == idx
        ws.append(jnp.sum(jnp.where(hit, scores, 0.0), axis=0, keepdims=True))
        cur = jnp.where(hit, NEG_INF, cur)
        idxs.append(idx)
    mask = jnp.zeros((N_EXPERTS, TR), F32)
    for idx in idxs:
        mask = mask + (row == idx).astype(F32)
    wsum = ws[0]
    for wk in ws[1:]:
        wsum = wsum + wk

    tj = lax.broadcasted_iota(I32, (TR, TR), 0)
    ti = lax.broadcasted_iota(I32, (TR, TR), 1)
    upper = jnp.where(tj < ti, 1.0, 0.0).astype(BF16)
    pos = carry_ref[...] + jnp.dot(mask.astype(BF16), upper, preferred_element_type=F32)
    for k in range(TOP_K):
        hit = row == idxs[k]
        te_ref[k:k + 1, :] = idxs[k]
        wn_ref[k:k + 1, :] = ws[k] / wsum * ROUTED_SCALE
        rk_ref[k:k + 1, :] = jnp.sum(jnp.where(hit, pos, 0.0), axis=0, keepdims=True).astype(I32)
    carry_ref[...] = carry_ref[...] + jnp.sum(mask, axis=1, keepdims=True)
    cnt_ref[...] = carry_ref[...]


def _router(hp, wr_t, e_bias):
    kt = lambda dtype: jax.ShapeDtypeStruct((TOP_K, T), dtype)
    kt_spec = pl.BlockSpec((TOP_K, TR), lambda i: (0, i))
    return pl.pallas_call(
        _router_kernel,
        grid=(T // TR,),
        in_specs=[
            pl.BlockSpec((TR * ROWS_PER_TOKEN, LANES), lambda i: (i, 0)),
            pl.BlockSpec((N_EXPERTS, D), lambda i: (0, 0)),
            pl.BlockSpec((N_EXPERTS, 1), lambda i: (0, 0)),
        ],
        out_specs=[
            kt_spec, kt_spec, kt_spec,
            pl.BlockSpec((N_EXPERTS, 1), lambda i: (0, 0)),
        ],
        out_shape=[
            kt(I32), kt(F32), kt(I32),
            jax.ShapeDtypeStruct((N_EXPERTS, 1), F32),
        ],
        scratch_shapes=[pltpu.VMEM((N_EXPERTS, 1), F32)],
        compiler_params=_params(("arbitrary",)),
        name="moe_router",
    )(hp, wr_t, e_bias)


_PAD_BITS = tuple(1 << b for b in range(GM.bit_length() - 1))


def _pad_fill_kernel(pad_start_ref, pad_len_ref, xg_in_ref, xg_ref, zero_ref, sem):
    del xg_in_ref
    zero_ref[...] = jnp.zeros_like(zero_ref)

    def pad_copies(e):
        start = pad_start_ref[e]
        n = pad_len_ref[e]
        copies = []
        for bit in _PAD_BITS:
            first = start + (n & ~(2 * bit - 1))
            copies.append(((n & bit) != 0, pltpu.make_async_copy(
                zero_ref.at[pl.ds(0, bit)], xg_ref.at[pl.ds(first, bit)], sem)))
        return copies

    def start_e(e, carry):
        for on, cp in pad_copies(e):
            @pl.when(on)
            def _():
                cp.start()
        return carry

    def wait_e(e, carry):
        for on, cp in pad_copies(e):
            @pl.when(on)
            def _():
                cp.wait()
        return carry

    lax.fori_loop(0, N_EXPERTS, start_e, 0)
    lax.fori_loop(0, N_EXPERTS, wait_e, 0)


def _pad_fill(pad_start, pad_len, xg):
    grid_spec = pltpu.PrefetchScalarGridSpec(
        num_scalar_prefetch=2,
        grid=(1,),
        in_specs=[pl.BlockSpec(memory_space=pl.ANY)],
        out_specs=pl.BlockSpec(memory_space=pl.ANY),
        scratch_shapes=[pltpu.VMEM((GM // 2, ROWS_PER_TOKEN, LANES), xg.dtype), pltpu.SemaphoreType.DMA],
    )
    return pl.pallas_call(
        _pad_fill_kernel,
        grid_spec=grid_spec,
        out_shape=jax.ShapeDtypeStruct(xg.shape, xg.dtype),
        input_output_aliases={2: 0},
        compiler_params=_params(("arbitrary",)),
        name="moe_pad_fill",
    )(pad_start, pad_len, xg)


SC_CORES, SC_SUBCORES = 2, 16
SC_WORKERS = SC_CORES * SC_SUBCORES
SC_W = 64


def _sc_worker_id():
    return lax.axis_index("s") * SC_CORES + lax.axis_index("c")


def _sc_dispatch(h3, slot3):
    nchunk = T // SC_WORKERS // SC_W
    mesh = plsc.VectorSubcoreMesh(core_axis_name="c", subcore_axis_name="s")
    tile = (SC_W, ROWS_PER_TOKEN, LANES)

    @functools.partial(
        pl.kernel, mesh=mesh,
        out_type=jax.ShapeDtypeStruct((SP, ROWS_PER_TOKEN, LANES), h3.dtype),
        scratch_types=[pltpu.VMEM((TOP_K, SC_W), I32), pltpu.VMEM((TOP_K, SC_W), I32),
                       pltpu.VMEM(tile, h3.dtype), pltpu.VMEM(tile, h3.dtype),
                       pltpu.SemaphoreType.DMA((2,)), pltpu.SemaphoreType.DMA((2,))],
    )
    def k(h_hbm, slot_hbm, xg_hbm, idx0, idx1, rows0, rows1, lsem, ssem):
        first = _sc_worker_id() * nchunk
        idx = (idx0, idx1)
        rows = (rows0, rows1)

        def loads(j, b):
            blk = first + j
            tok = pl.multiple_of(blk * SC_W, SC_W)
            return (pltpu.make_async_copy(slot_hbm.at[blk], idx[b], lsem.at[b]),
                    pltpu.make_async_copy(h_hbm.at[pl.ds(tok, SC_W)], rows[b], lsem.at[b]))

        def scatters(b):
            return [pltpu.make_async_copy(rows[b], xg_hbm.at[idx[b].at[kk]], ssem.at[b]) for kk in range(TOP_K)]

        for cp in loads(0, 0):
            cp.start()

        @pl.loop(0, nchunk, step=2)
        def _(j):
            for b in (0, 1):
                jj = j + b
                for cp in loads(jj, b):
                    cp.wait()
                for cp in scatters(b):
                    cp.start()

                @pl.when(jj + 1 < nchunk)
                def _():
                    @pl.when(jj >= 1)
                    def _():
                        for cp in scatters(1 - b):
                            cp.wait()
                    for cp in loads(jj + 1, 1 - b):
                        cp.start()

        for b in (0, 1):
            for cp in scatters(b):
                cp.wait()

    return k(h3, slot3)


def _gmm_kernel(tile_e_ref, tile_blk_ref, tile_nsub_ref, x_ref, wg_ref, wu_ref, wd_ref,
                y_ref, wgu_scr, wd_scr):
    j = pl.program_id(0)
    nsub = tile_nsub_ref[j]

    @pl.when(nsub > 0)
    def _():
        prev = tile_e_ref[jnp.maximum(j - 1, 0)]

        @pl.when(jnp.logical_or(j == 0, tile_e_ref[j] != prev))
        def _():
            wgu_scr[:, 0:D_EXPERT] = wg_ref[0, 0].astype(BF16)
            wgu_scr[:, D_EXPERT:2 * D_EXPERT] = wu_ref[0, 0].astype(BF16)
            wd_scr[...] = wd_ref[0, 0].astype(BF16)

    def expert_mlp(s):
        x_hi, x_lo = _unpack_rows(_load_token_tiles(x_ref, GM_SUB, s * GM_SUB))
        gu = (jnp.dot(x_hi.astype(BF16), wgu_scr[0:PACKED, :], preferred_element_type=F32)
              + jnp.dot(x_lo.astype(BF16), wgu_scr[PACKED:D, :], preferred_element_type=F32))
        hid = _silu(gu[:, 0:D_EXPERT]) * gu[:, D_EXPERT:2 * D_EXPERT]
        y = jnp.dot(hid.astype(BF16), wd_scr[...], preferred_element_type=F32)
        _store_token_tiles(y_ref, _pack_rows(y), s * GM_SUB)

    for n in range(1, GM // GM_SUB + 1):
        @pl.when(nsub == n)
        def _():
            for s in range(n):
                expert_mlp(s)
            for s in range(n, GM // GM_SUB):
                _store_token_tiles(y_ref, jnp.zeros((GM_SUB, PACKED), U32), s * GM_SUB)


def _gmm(tile_e, tile_blk, tile_nsub, xg, l, w_gate, w_up, w_down):
    row_tile = pl.BlockSpec((GM * ROWS_PER_TOKEN, LANES), lambda j, te, tb, nv: (tb[j], 0))
    grid_spec = pltpu.PrefetchScalarGridSpec(
        num_scalar_prefetch=3,
        grid=(NT_MAX,),
        in_specs=[
            row_tile,
            pl.BlockSpec((1, 1, D, D_EXPERT), lambda j, te, tb, nv: (l, te[j], 0, 0)),
            pl.BlockSpec((1, 1, D, D_EXPERT), lambda j, te, tb, nv: (l, te[j], 0, 0)),
            pl.BlockSpec((1, 1, D_EXPERT, D), lambda j, te, tb, nv: (l, te[j], 0, 0)),
        ],
        out_specs=row_tile,
        scratch_shapes=[pltpu.VMEM((D, 2 * D_EXPERT), BF16), pltpu.VMEM((D_EXPERT, D), BF16)],
    )
    return pl.pallas_call(
        _gmm_kernel,
        grid_spec=grid_spec,
        out_shape=jax.ShapeDtypeStruct((SP * ROWS_PER_TOKEN, LANES), U32),
        compiler_params=_params(("arbitrary",)),
        name="moe_grouped_matmul",
    )(tile_e, tile_blk, tile_nsub, xg, w_gate, w_up, w_down)


def _sc_gather(table3, idx):
    n_idx = idx.shape[0]
    per_w = n_idx // SC_WORKERS
    nchunk = per_w // SC_W
    mesh = plsc.VectorSubcoreMesh(core_axis_name="c", subcore_axis_name="s")
    tile = (SC_W, ROWS_PER_TOKEN, LANES)

    @functools.partial(
        pl.kernel, mesh=mesh,
        out_type=jax.ShapeDtypeStruct((n_idx, ROWS_PER_TOKEN, LANES), table3.dtype),
        scratch_types=[pltpu.VMEM((per_w,), I32), pltpu.VMEM(tile, table3.dtype), pltpu.VMEM(tile, table3.dtype),
                       pltpu.SemaphoreType.DMA((2,)), pltpu.SemaphoreType.DMA((2,))],
    )
    def k(table_hbm, idx_hbm, out_hbm, idx_v, rows0, rows1, gsem, wsem):
        base = pl.multiple_of(_sc_worker_id() * per_w, per_w)
        rows = (rows0, rows1)
        pltpu.sync_copy(idx_hbm.at[pl.ds(base, per_w)], idx_v)

        def gather(j, b):
            ids = idx_v.at[pl.ds(pl.multiple_of(j * SC_W, SC_W), SC_W)]
            return pltpu.make_async_copy(table_hbm.at[ids], rows[b], gsem.at[b])

        def write(j, b):
            dst = out_hbm.at[pl.ds(pl.multiple_of(base + j * SC_W, SC_W), SC_W)]
            return pltpu.make_async_copy(rows[b], dst, wsem.at[b])

        gather(0, 0).start()

        @pl.loop(0, nchunk, step=2)
        def _(j):
            for b in (0, 1):
                jj = j + b
                gather(jj, b).wait()
                write(jj, b).start()

                @pl.when(jj + 1 < nchunk)
                def _():
                    @pl.when(jj >= 1)
                    def _():
                        write(jj - 1, 1 - b).wait()
                    gather(jj + 1, 1 - b).start()

        write(nchunk - 2, 0).wait()
        write(nchunk - 1, 1).wait()

    return k(table3, idx)


def _shared_kernel(hp_ref, wsg_ref, wsu_ref, wsd_ref, o_ref):
    h_hi, h_lo = (t.astype(BF16) for t in _unpack_rows(_load_token_tiles(hp_ref, TM)))

    def proj(w_ref):
        return (jnp.dot(h_hi, w_ref[0:PACKED, :], preferred_element_type=F32)
                + jnp.dot(h_lo, w_ref[PACKED:D, :], preferred_element_type=F32))

    hid = _silu(proj(wsg_ref)) * proj(wsu_ref)
    o_ref[...] = jnp.dot(hid.astype(BF16), wsd_ref[...], preferred_element_type=F32)


def _shared_expert(hp, ws_gate, ws_up, ws_down):
    full = lambda *shape: pl.BlockSpec(shape, lambda i: (0,) * len(shape))
    return pl.pallas_call(
        _shared_kernel,
        grid=(T // TM,),
        in_specs=[pl.BlockSpec((TM * ROWS_PER_TOKEN, LANES), lambda i: (i, 0)),
                  full(D, D_SHARED), full(D, D_SHARED), full(D_SHARED, D)],
        out_specs=pl.BlockSpec((TM, D), lambda i: (i, 0)),
        out_shape=jax.ShapeDtypeStruct((T, D), F32),
        compiler_params=_params(("arbitrary",)),
        name="moe_shared_expert",
    )(hp, ws_gate, ws_up, ws_down)


def _combine_kernel(wn_ref, x_ref, mod_ref, gp_ref, ysh_ref, y_ref, o_ref):
    m = mod_ref[0, 0]
    x = x_ref[...]
    acc = ysh_ref[...]

    r = lax.broadcasted_iota(I32, (TD, TD), 0)
    c = lax.broadcasted_iota(I32, (TD, TD), 1)
    eye = jnp.where(r == c, 1.0, 0.0).astype(BF16)
    nt = lambda a, b: lax.dot_general(a, b, (((1,), (1,)), ((), ())), preferred_element_type=F32)
    w1, w2, w3 = _split3(wn_ref[...])
    w_t = nt(eye, w1) + nt(eye, w2) + nt(eye, w3)

    acc_hi = acc[:, :PACKED]
    acc_lo = acc[:, PACKED:]
    for k in range(TOP_K):
        y_hi, y_lo = _unpack_rows(_load_token_tiles(y_ref, TD, k * TD))
        acc_hi = acc_hi + y_hi * w_t[:, k:k + 1]
        acc_lo = acc_lo + y_lo * w_t[:, k:k + 1]
    acc = jnp.concatenate([acc_hi, acc_lo], axis=1)
    o_ref[...] = x + m[5:6] * _rms(acc, gp_ref[...])


def _combine(wn, x, mod, l, g_post, y_shared, ybuf, first_tok, n_tok):
    off = first_tok // TD
    tiles_per_dec = DEC_SEQ // TD
    npd = TP // TD
    mod_row = lambda i: jnp.where(i + off < npd, 0, 1 + (i + off - npd) // tiles_per_dec)
    full = lambda *shape: pl.BlockSpec(shape, lambda i: (0,) * len(shape))
    y_spec = pl.BlockSpec((TOP_K * TD * ROWS_PER_TOKEN, LANES), lambda i: (i, 0))
    return pl.pallas_call(
        _combine_kernel,
        grid=(n_tok // TD,),
        in_specs=[
            pl.BlockSpec((TOP_K, TD), lambda i: (0, i + off)),
            pl.BlockSpec((TD, D), lambda i: (i + off, 0)),
            pl.BlockSpec((1, 1, 6, D), lambda i: (l, mod_row(i), 0, 0)),
            full(1, D),
            pl.BlockSpec((TD, D), lambda i: (i + off, 0)),
            y_spec,
        ],
        out_specs=pl.BlockSpec((TD, D), lambda i: (i, 0)),
        out_shape=jax.ShapeDtypeStruct((n_tok, D), F32),
        compiler_params=_params(("arbitrary",)),
        name="moe_combine",
    )(wn, x, mod, g_post, y_shared, ybuf)


def _moe_layer(x, h, mod, l, g_post, w_router, e_bias, w_gate, w_up, w_down,
               ws_gate, ws_up, ws_down):
    top_e, wn, rk, cnt = _router(h, w_router.T, e_bias.reshape(N_EXPERTS, 1))
    y_shared = _shared_expert(h, ws_gate.astype(BF16), ws_up.astype(BF16), ws_down.astype(BF16))
    cnt = cnt.reshape(N_EXPERTS).astype(I32)
    padded = (cnt + GM - 1) // GM * GM
    ends = jnp.cumsum(padded)
    offs = ends - padded
    eid = jnp.arange(N_EXPERTS, dtype=I32)[:, None, None]
    slot = rk + jnp.sum(jnp.where(top_e[None] == eid, offs[:, None, None], 0), axis=0)
    nvalid = ends[-1] // GM
    tile_start = jnp.arange(NT_MAX, dtype=I32) * GM
    tile_raw = jnp.sum((tile_start[:, None] >= ends[None, :]).astype(I32), axis=1)
    last = jnp.maximum(nvalid - 1, 0)
    tile_blk = jnp.minimum(jnp.arange(NT_MAX, dtype=I32), last)
    tile_e = jnp.minimum(tile_raw, N_EXPERTS - 1)
    tile_e = jnp.where(jnp.arange(NT_MAX) <= last, tile_e, tile_e[last])
    own = tile_e[:, None] == jnp.arange(N_EXPERTS, dtype=I32)[None, :]
    tile_rows = jnp.sum(jnp.where(own, (offs + cnt)[None, :], 0), axis=1) - tile_start
    tile_nsub = jnp.where(jnp.arange(NT_MAX) <= last, (jnp.clip(tile_rows, 0, GM) + GM_SUB - 1) // GM_SUB, 0)
    slot3 = slot.reshape(TOP_K, T // SC_W, SC_W).transpose(1, 0, 2)
    xg = _sc_dispatch(h.reshape(T, ROWS_PER_TOKEN, LANES), slot3)
    xg = _pad_fill(offs + cnt, padded - cnt, xg).reshape(SP * ROWS_PER_TOKEN, LANES)
    yg = _gmm(tile_e, tile_blk, tile_nsub.astype(I32), xg, l, w_gate, w_up, w_down)
    yg3 = yg.reshape(SP, ROWS_PER_TOKEN, LANES)
    outs = []
    for first_tok, n_tok in ((0, TP), (TP, TS)):
        ids = slot[:, first_tok:first_tok + n_tok].reshape(TOP_K, n_tok // TD, TD).transpose(1, 0, 2)
        ybuf = _sc_gather(yg3, ids.reshape(TOP_K * n_tok))
        outs.append(_combine(wn, x, mod, l, g_post, y_shared,
                             ybuf.reshape(TOP_K * n_tok * ROWS_PER_TOKEN, LANES), first_tok, n_tok))
    return outs


def _rope_tables():
    n = DEC_SEQ
    rows = n // GRID_W
    row = jnp.repeat(jnp.arange(rows), GRID_W).astype(F32)
    col = jnp.tile(jnp.arange(GRID_W), rows).astype(F32)
    half = DQK_B // 2
    inv = ROPE_BASE ** (-jnp.arange(0, half, 2, dtype=F32) / half)
    ang_r = row[:, None] * inv
    ang_c = col[:, None] * inv
    ang = jnp.concatenate([ang_r, ang_r, ang_c, ang_c], axis=-1)
    reps = QK_B // DQK_B
    return jnp.tile(jnp.cos(ang), (1, reps)), jnp.tile(jnp.sin(ang), (1, reps))


def _pad_in_proj(w):
    s = [0, Q_A, 2 * Q_A, 2 * Q_A + V_A, 2 * Q_A + 2 * V_A]
    s += [s[-1] + GATE_RANK, s[-1] + 2 * GATE_RANK]
    s += [s[-1] + QK_B, s[-1] + 2 * QK_B, s[-1] + 2 * QK_B + V_B]
    gates = jnp.pad(w[:, s[4]:s[6]], ((0, 0), (0, GL_PAD - 2 * GATE_RANK)))
    return jnp.concatenate([w[:, s[0]:s[4]], gates, w[:, s[6]:s[9]]], axis=1).astype(BF16)


def kernel(x_prompt, x_sample, c, c_ctx, state_gla, cache_k, cache_v, ada_w, ada_b, norm_pre_mix, norm_post_mix, norm_pre_ffn, norm_post_ffn, ab_w_in, gla_w_g2, gla_b_g2, gla_norm_g, diff_lambda, diff_norm_g, ab_w_out, sgu_w_in, sgu_b_in, sgu_norm_g, sgu_w_s, sgu_b_s, sgu_w_out, moe_w_router, moe_e_bias, moe_w_gate, moe_w_up, moe_w_down, moe_ws_gate, moe_ws_up, moe_ws_down):
    depth = ada_w.shape[0]
    xp, xs = x_prompt.reshape(TP, D), x_sample.reshape(TS, D)
    cond = jnp.concatenate([c_ctx[None, :], c, jnp.zeros((8 - 1 - DEC_BATCH, D), F32)], axis=0)
    mod = _modulation(cond, ada_w, ada_b)
    cos, sin = _rope_tables()
    vec = lambda a: a.reshape(1, -1)
    new_s = new_k = new_v = None
    for l in range(depth):
        if l % 2 == 0:
            e = l // 2
            lam_init = 0.8 - 0.6 * math.exp(-0.3 * l)
            a, r_a, gl, q_b, k_b, v_b, new_k, new_v = _in_proj(xp, xs, mod, l, vec(norm_pre_mix[l]),
                                                               _pad_in_proj(ab_w_in[e]), cos, sin)
            s0_t = jnp.swapaxes(state_gla[:, e], -1, -2)
            same_head = jnp.eye(H_A, dtype=bool)[None, None, :, None, :, None]
            s0_t = jnp.where(same_head, s0_t[:, :, :, :, None, :], 0.0).reshape(DEC_BATCH, 2, V_A, Q_A)
            o_f, o_bw, s_fin_t = _gla(a, gl, gla_w_g2[e], gla_b_g2[e].reshape(2, 1, Q_A), s0_t)
            o_att = _diff_attention(q_b, k_b, v_b, cache_k, cache_v, diff_lambda[e], lam_init)
            x, h = _mix_out(lam_init, o_f, o_bw, r_a, o_att, xp, xs, mod, l, vec(gla_norm_g[e]), vec(diff_norm_g[e]),
                            ab_w_out[e].astype(BF16), vec(norm_post_mix[l]), vec(norm_pre_ffn[l]))
            new_s = jnp.swapaxes(s_fin_t, -1, -2)[:, None]
        else:
            o = l // 2
            x, h = _sgu(xp, xs, mod, l, vec(norm_pre_mix[l]), sgu_w_in[o].astype(BF16), vec(sgu_b_in[o]),
                        vec(sgu_norm_g[o]), sgu_w_s[o], sgu_b_s[o].T, sgu_w_out[o].astype(BF16),
                        vec(norm_post_mix[l]), vec(norm_pre_ffn[l]))
        xp, xs = _moe_layer(x, h, mod, l, vec(norm_post_ffn[l]), moe_w_router[l], moe_e_bias[l],
                            moe_w_gate, moe_w_up, moe_w_down, moe_ws_gate[l], moe_ws_up[l], moe_ws_down[l])
    y_prompt = xp.reshape(BATCH, SEQ, D)
    y_sample = xs.reshape(DEC_BATCH, DEC_SEQ, D)
    return (y_prompt, y_sample, new_s, new_k, new_v)
```

```python
import functools
import math

import jax
import jax.numpy as jnp
from jax import lax
from jax.experimental import pallas as pl
from jax.experimental.pallas import tpu as pltpu
from jax.experimental.pallas import tpu_sc as plsc

F32 = jnp.float32
BF16 = jnp.bfloat16
I32 = jnp.int32

D = 1024
BATCH, SEQ = 32, 256
DEC_BATCH, DEC_SEQ = 4, 2048
PAST_LEN = 256
GRID_W = 64
EPS = 1e-6
TP = BATCH * SEQ
TS = DEC_BATCH * DEC_SEQ
T = TP + TS
H_A, DK_A, DV_A = 4, 64, 128
Q_A, V_A = H_A * DK_A, H_A * DV_A
GATE_RANK, GATE_TAU, GLA_CHUNK = 16, 16.0, 64
H_B, DQK_B, DV_B = 4, 64, 128
QK_B, V_B = H_B * 2 * DQK_B, H_B * DV_B
ROPE_BASE = 10000.0
SGU_DIM, SGU_GROUPS, SGU_CHUNK = 1024, 4, 128
N_EXPERTS, TOP_K, N_GROUPS, TOPK_GROUPS = 64, 8, 8, 4
GROUP_SIZE = N_EXPERTS // N_GROUPS
D_EXPERT, D_SHARED = 256, 256
ROUTED_SCALE = 2.5

TM = 512
TM_SUB = 256
NPT = TP // TM
TILES_PER_DEC = DEC_SEQ // TM
SEG = 256
NSEG = T // SEG
NSEG_P = TP // SEG
SEG_PER_DEC = DEC_SEQ // SEG
TR = 512
TD = 512
GM = 1024
GM_SUB = 256
NT_MAX = T * TOP_K // GM + N_EXPERTS
SP = NT_MAX * GM
GL_PAD = 128
VMEM_LIMIT = 56 * 1024 * 1024
NEG_INF = float("-inf")


def _bdot(a, b):
    return jnp.dot(a.astype(BF16), b.astype(BF16), preferred_element_type=F32)


def _bdot_nt(a, b):
    return lax.dot_general(a.astype(BF16), b.astype(BF16), (((1,), (1,)), ((), ())),
                           preferred_element_type=F32)


def _bdot_tn(a, b):
    return lax.dot_general(a.astype(BF16), b.astype(BF16), (((0,), (0,)), ((), ())),
                           preferred_element_type=F32)


def _split3(x):
    x1 = x.astype(BF16)
    r1 = x - x1.astype(F32)
    x2 = r1.astype(BF16)
    x3 = (r1 - x2.astype(F32)).astype(BF16)
    return x1, x2, x3


def _rms(x, g):
    return x * lax.rsqrt(jnp.mean(x * x, axis=-1, keepdims=True) + EPS) * g


def _silu(x):
    return x * jax.nn.sigmoid(x)


def _mod_row(i):
    return jnp.where(i < NPT, 0, 1 + (i - NPT) // TILES_PER_DEC)


def _params(sem, limit=VMEM_LIMIT):
    return pltpu.CompilerParams(dimension_semantics=sem, vmem_limit_bytes=limit)


def _mod_kernel(c_ref, w_ref, b_ref, o_ref):
    o_ref[0] = _bdot(_silu(c_ref[...]), w_ref[0]) + b_ref[0]


def _modulation(cond, ada_w, ada_b):
    depth = ada_w.shape[0]
    nj = 6
    out = pl.pallas_call(
        _mod_kernel,
        grid=(depth, nj),
        in_specs=[
            pl.BlockSpec((8, D), lambda l, j: (0, 0)),
            pl.BlockSpec((1, D, D), lambda l, j: (l, 0, j)),
            pl.BlockSpec((1, 1, D), lambda l, j: (l, 0, j)),
        ],
        out_specs=pl.BlockSpec((1, 8, D), lambda l, j: (l, 0, j)),
        out_shape=jax.ShapeDtypeStruct((depth, 8, 6 * D), F32),
        compiler_params=_params(("arbitrary", "arbitrary")),
        name="adaln_modulation",
    )(cond, ada_w, ada_b.reshape(depth, 1, 6 * D))
    return out.reshape(depth, 8, 6, D)


_C_A, _C_R, _C_GL, _C_Q, _C_K, _C_V, _C_END = 0, 1024, 1536, 1664, 2176, 2688, 3200


def _rope(x, cos, sin):
    lane = lax.broadcasted_iota(I32, x.shape, 1)
    first = (lane % 32) < 16
    n = x.shape[1]
    xr = jnp.where(first, -pltpu.roll(x, n - 16, 1), pltpu.roll(x, 16, 1))
    return x * cos + xr * sin


def _stream_specs():
    return [pl.BlockSpec((TM, D), lambda i: (jnp.minimum(i, NPT - 1), 0)),
            pl.BlockSpec((TM, D), lambda i: (jnp.maximum(i - NPT, 0), 0))]


def _stream_tile(xp_ref, xs_ref):
    return jnp.where(pl.program_id(0) < NPT, xp_ref[...], xs_ref[...])


def _in_kernel(xp_ref, xs_ref, mod_ref, g_ref, w_ref, cos_ref, sin_ref,
               a_ref, r_ref, gl_ref, q_ref, k_ref, v_ref, ck_ref, cv_ref):
    latent = pl.program_id(0) >= NPT
    m = mod_ref[0, 0]
    x = _stream_tile(xp_ref, xs_ref)
    for s in range(TM // TM_SUB):
        rows = slice(s * TM_SUB, (s + 1) * TM_SUB)
        h = _rms(x[rows], g_ref[...]) * (1.0 + m[1:2]) + m[0:1]
        hb = h.astype(BF16)

        def proj(c0, c1):
            return jnp.dot(hb, w_ref[:, c0:c1], preferred_element_type=F32)

        a_ref[rows, :] = proj(_C_A, _C_R)
        r_ref[rows, :] = proj(_C_R, _C_GL)
        gl_ref[rows, :] = proj(_C_GL, _C_Q)
        v_ref[rows, :] = proj(_C_V, _C_END)
        q = proj(_C_Q, _C_K)
        k = proj(_C_K, _C_V)
        cos = cos_ref[rows, :]
        sin = sin_ref[rows, :]
        q_ref[rows, :] = jnp.where(latent, _rope(q, cos, sin), q)
        k_ref[rows, :] = jnp.where(latent, _rope(k, cos, sin), k)

    @pl.when(jnp.logical_not(latent))
    def _():
        for s in range(TM // SEQ):
            rows = slice(s * SEQ, (s + 1) * SEQ)
            for h in range(H_B):
                cv_ref[s, 0, h] = v_ref[rows, h * DV_B:(h + 1) * DV_B]
                for mp in range(2):
                    ck_ref[s, 0, h, mp] = k_ref[rows, (2 * h + mp) * DQK_B:(2 * h + mp + 1) * DQK_B]


def _in_proj(xp, xs, mod, l, g, w_pad, cos, sin):
    tok = lambda width: pl.BlockSpec((TM, width), lambda i: (i, 0))
    rope_spec = pl.BlockSpec((TM, QK_B), lambda i: (jnp.maximum(i - NPT, 0) % TILES_PER_DEC, 0))
    widths = (1024, 512, GL_PAD, 512, 512, 512)
    seqs = TM // SEQ
    prompt_tile = lambda i: jnp.minimum(i, NPT - 1)
    cache_specs = [pl.BlockSpec((seqs, 1, H_B, 2, SEQ, DQK_B), lambda i: (prompt_tile(i), 0, 0, 0, 0, 0)),
                   pl.BlockSpec((seqs, 1, H_B, SEQ, DV_B), lambda i: (prompt_tile(i), 0, 0, 0, 0))]
    cache_shapes = [jax.ShapeDtypeStruct((BATCH, 1, H_B, 2, SEQ, DQK_B), F32),
                    jax.ShapeDtypeStruct((BATCH, 1, H_B, SEQ, DV_B), F32)]
    return pl.pallas_call(
        _in_kernel,
        grid=(T // TM,),
        in_specs=_stream_specs() + [
            pl.BlockSpec((1, 1, 6, D), lambda i: (l, _mod_row(i), 0, 0)),
            pl.BlockSpec((1, D), lambda i: (0, 0)),
            pl.BlockSpec((D, _C_END), lambda i: (0, 0)),
            rope_spec, rope_spec,
        ],
        out_specs=[tok(w) for w in widths] + cache_specs,
        out_shape=[jax.ShapeDtypeStruct((T, w), F32) for w in widths] + cache_shapes,
        compiler_params=_params(("arbitrary",)),
        name="mixer_ab_in_proj",
    )(xp, xs, mod, g, w_pad, cos, sin)


def _log_sigmoid(x):
    return jnp.minimum(x, 0.0) - jnp.log(1.0 + jnp.exp(-jnp.abs(x)))


def _gla_kernel(af_ref, ab_ref, glf_ref, glb_ref, wg_ref, bg_ref, s0_ref,
                of_ref, ob_ref, sfin_ref, st_ref):
    i = pl.program_id(0)

    @pl.when(i < NSEG_P)
    def _():
        st_ref[...] = jnp.zeros_like(st_ref)

    @pl.when(jnp.logical_and(i >= NSEG_P, (i - NSEG_P) % SEG_PER_DEC == 0))
    def _():
        st_ref[...] = s0_ref[0]

    r = lax.broadcasted_iota(I32, (SEG, SEG), 0)
    c = lax.broadcasted_iota(I32, (SEG, SEG), 1)
    same = (r // GLA_CHUNK) == (c // GLA_CHUNK)
    nchunk = SEG // GLA_CHUNK
    own_head = (lax.broadcasted_iota(I32, (V_A, Q_A), 0) // DV_A) == (lax.broadcasted_iota(I32, (V_A, Q_A), 1) // DK_A)

    for d, (a_ref, gl_ref, o_ref) in enumerate(((af_ref, glf_ref, of_ref), (ab_ref, glb_ref, ob_ref))):
        fwd = d == 0
        gcol = gl_ref[:, d * GATE_RANK:(d + 1) * GATE_RANK]
        la = _log_sigmoid(_bdot(gcol, wg_ref[d]) + bg_ref[d]) / GATE_TAU
        causal = jnp.logical_and(same, (c <= r) if fwd else (c >= r))
        tri = jnp.where(causal, 1.0, 0.0).astype(BF16)
        l1, l2, l3 = _split3(la)
        b_all = (jnp.dot(tri, l1, preferred_element_type=F32)
                 + jnp.dot(tri, l2, preferred_element_type=F32)
                 + jnp.dot(tri, l3, preferred_element_type=F32))
        q_in_all = a_ref[:, 0:Q_A] * (DK_A ** -0.5) * jnp.exp(b_all)
        kd_all = a_ref[:, Q_A:2 * Q_A] * jnp.exp(-b_all)
        intra = []
        for h in range(H_A):
            kc = slice(h * DK_A, (h + 1) * DK_A)
            attn = jnp.where(causal, _bdot_nt(q_in_all[:, kc], kd_all[:, kc]), 0.0)
            intra.append(_bdot(attn, a_ref[:, 2 * Q_A + h * DV_A:2 * Q_A + (h + 1) * DV_A]))
        intra = jnp.concatenate(intra, axis=1)
        state = st_ref[d]
        order = range(nchunk) if fwd else range(nchunk - 1, -1, -1)
        for ch in order:
            r0 = ch * GLA_CHUNK
            rows = slice(r0, r0 + GLA_CHUNK)
            end = r0 + GLA_CHUNK - 1 if fwd else r0
            b_end = b_all[end:end + 1, :]
            kw = a_ref[rows, Q_A:2 * Q_A] * jnp.exp(b_end - b_all[rows, :])
            o_ref[rows, :] = intra[rows, :] + _bdot_nt(q_in_all[rows, :], state)
            kv_t = _bdot_tn(a_ref[rows, 2 * Q_A:2 * Q_A + V_A], kw)
            state = state * jnp.exp(b_end) + jnp.where(own_head, kv_t, 0.0)
        st_ref[d] = state

    @pl.when(i < NSEG_P)
    def _():
        for d in range(2):
            for h in range(H_A):
                sfin_ref[0, d, h] = st_ref[d, h * DV_A:(h + 1) * DV_A, h * DK_A:(h + 1) * DK_A]


def _seg_bwd(i):
    j = i - NSEG_P
    return jnp.where(i < NSEG_P, i, NSEG_P + (j // SEG_PER_DEC) * SEG_PER_DEC + (SEG_PER_DEC - 1 - j % SEG_PER_DEC))


def _gla(a, gl, wg, bg, s0_t):
    seg = lambda width, f: pl.BlockSpec((SEG, width), lambda i: (f(i), 0))
    ident = lambda i: i
    st_block = (1, 2, H_A, DV_A, DK_A)
    return pl.pallas_call(
        _gla_kernel,
        grid=(NSEG,),
        in_specs=[
            seg(D, ident), seg(D, _seg_bwd), seg(GL_PAD, ident), seg(GL_PAD, _seg_bwd),
            pl.BlockSpec((2, GATE_RANK, Q_A), lambda i: (0, 0, 0)),
            pl.BlockSpec((2, 1, Q_A), lambda i: (0, 0, 0)),
            pl.BlockSpec((1, 2, V_A, Q_A), lambda i: (jnp.maximum(i - NSEG_P, 0) // SEG_PER_DEC, 0, 0, 0)),
        ],
        out_specs=[
            seg(V_A, ident), seg(V_A, _seg_bwd),
            pl.BlockSpec(st_block, lambda i: (jnp.minimum(i, NSEG_P - 1), 0, 0, 0, 0)),
        ],
        out_shape=[
            jax.ShapeDtypeStruct((T, V_A), F32),
            jax.ShapeDtypeStruct((T, V_A), F32),
            jax.ShapeDtypeStruct((BATCH, 2, H_A, DV_A, DK_A), F32),
        ],
        scratch_shapes=[pltpu.VMEM((2, V_A, Q_A), F32)],
        compiler_params=_params(("arbitrary",)),
        name="gla_bidir",
    )(a, a, gl, gl, wg, bg, s0_t)


def _diff_lambda(lam_ref, lam_init):
    lp = lam_ref[...]
    s01 = jnp.sum(lp[0:1] * lp[1:2], axis=1, keepdims=True)
    s23 = jnp.sum(lp[2:3] * lp[3:4], axis=1, keepdims=True)
    return jnp.exp(s01) - jnp.exp(s23) + lam_init


def _attn_prompt_kernel(lam_init, q_ref, k_ref, v_ref, lam_ref, o_ref):
    lam = _diff_lambda(lam_ref, lam_init)
    for h in range(H_B):
        ps = []
        for m in range(2):
            cols = slice((2 * h + m) * DQK_B, (2 * h + m + 1) * DQK_B)
            s = _bdot_nt(q_ref[:, cols] * (DQK_B ** -0.5), k_ref[:, cols])
            e = jnp.exp(s - jnp.max(s, axis=1, keepdims=True))
            ps.append(e * (1.0 / jnp.sum(e, axis=1, keepdims=True)))
        w = ps[0] - lam * ps[1]
        o_ref[:, h * DV_B:(h + 1) * DV_B] = _bdot(w, v_ref[:, h * DV_B:(h + 1) * DV_B])


def _attn_sample_kernel(lam_init, q_ref, k_ref, v_ref, ck_ref, cv_ref, lam_ref, o_ref):
    lam = _diff_lambda(lam_ref, lam_init)
    for h in range(H_B):
        parts = []
        for m in range(2):
            cols = slice((2 * h + m) * DQK_B, (2 * h + m + 1) * DQK_B)
            q = q_ref[:, cols] * (DQK_B ** -0.5)
            sc = _bdot_nt(q, ck_ref[0, 0, h, m])
            sn = _bdot_nt(q, k_ref[:, cols])
            mx = jnp.maximum(jnp.max(sc, axis=1, keepdims=True), jnp.max(sn, axis=1, keepdims=True))
            ec = jnp.exp(sc - mx)
            en = jnp.exp(sn - mx)
            inv = (1.0 if m == 0 else -lam) / (jnp.sum(ec, axis=1, keepdims=True) + jnp.sum(en, axis=1, keepdims=True))
            parts.append((ec * inv, en * inv))
        wc = parts[0][0] + parts[1][0]
        wn = parts[0][1] + parts[1][1]
        o_ref[:, h * DV_B:(h + 1) * DV_B] = (_bdot(wc, cv_ref[0, 0, h])
                                             + _bdot(wn, v_ref[:, h * DV_B:(h + 1) * DV_B]))


QB = SEQ
NQB_DEC = DEC_SEQ // QB


def _attn_kernel(lam_init, q_ref, kp_ref, vp_ref, ks_ref, vs_ref, ck_ref, cv_ref, lam_ref, o_ref):
    i = pl.program_id(0)

    @pl.when(i < BATCH)
    def _():
        _attn_prompt_kernel(lam_init, q_ref, kp_ref, vp_ref, lam_ref, o_ref)

    @pl.when(i >= BATCH)
    def _():
        _attn_sample_kernel(lam_init, q_ref, ks_ref, vs_ref, ck_ref, cv_ref, lam_ref, o_ref)


def _diff_attention(q, k, v, cache_k, cache_v, lam_p, lam_init):
    blk = lambda rows, f: pl.BlockSpec((rows, 512), f)
    dec_b = lambda i: jnp.maximum(i - BATCH, 0) // NQB_DEC
    own = lambda i: (i, 0)
    prompt_kv = lambda i: (jnp.minimum(i, BATCH - 1), 0)
    dec_kv = lambda i: (TP // DEC_SEQ + dec_b(i), 0)
    return pl.pallas_call(
        functools.partial(_attn_kernel, lam_init),
        grid=(BATCH + DEC_BATCH * NQB_DEC,),
        in_specs=[
            blk(QB, own), blk(SEQ, prompt_kv), blk(SEQ, prompt_kv), blk(DEC_SEQ, dec_kv), blk(DEC_SEQ, dec_kv),
            pl.BlockSpec((1, 1, H_B, 2, PAST_LEN, DQK_B), lambda i: (dec_b(i), 0, 0, 0, 0, 0)),
            pl.BlockSpec((1, 1, H_B, PAST_LEN, DV_B), lambda i: (dec_b(i), 0, 0, 0, 0)),
            pl.BlockSpec((4, DQK_B), lambda i: (0, 0)),
        ],
        out_specs=blk(QB, own),
        out_shape=jax.ShapeDtypeStruct((T, V_B), F32),
        compiler_params=_params(("arbitrary",)),
        name="diff_attention",
    )(q, k, v, k, v, cache_k, cache_v, lam_p)


def _head_rms(x, g, nheads, width):
    return jnp.concatenate([_rms(x[:, h * width:(h + 1) * width], g) for h in range(nheads)], axis=1)


def _mix_out_kernel(lam_init, of_ref, ob_ref, r_ref, oatt_ref, xp_ref, xs_ref, mod_ref,
                    gg_ref, dg_ref, wo_ref, gp_ref, gffn_ref, o_ref, hp_ref):
    m = mod_ref[0, 0]
    o_a = _head_rms(of_ref[...] + ob_ref[...], gg_ref[...], H_A, DV_A) * _silu(r_ref[...])
    o_b = _head_rms(oatt_ref[...], dg_ref[...], H_B, DV_B) * (1.0 - lam_init)
    out = _bdot(o_a, wo_ref[0:V_A, :]) + _bdot(o_b, wo_ref[V_A:V_A + V_B, :])
    x1 = _stream_tile(xp_ref, xs_ref) + m[2:3] * _rms(out, gp_ref[...])
    o_ref[...] = x1
    _store_token_tiles(hp_ref, _ffn_input_rows(x1, m, gffn_ref[...]))


def _mix_out(lam_init, o_f, o_b, r_a, o_att, xp, xs, mod, l, gla_g, diff_g, w_o, g_post, g_ffn):
    tok = lambda width: pl.BlockSpec((TM, width), lambda i: (i, 0))
    vec = lambda width: pl.BlockSpec((1, width), lambda i: (0, 0))
    return pl.pallas_call(
        functools.partial(_mix_out_kernel, lam_init),
        grid=(T // TM,),
        in_specs=[
            tok(512), tok(512), tok(512), tok(512), *_stream_specs(),
            pl.BlockSpec((1, 1, 6, D), lambda i: (l, _mod_row(i), 0, 0)),
            vec(DV_A), vec(DV_B),
            pl.BlockSpec((V_A + V_B, D), lambda i: (0, 0)),
            vec(D), vec(D),
        ],
        out_specs=[tok(D), pl.BlockSpec((TM * ROWS_PER_TOKEN, LANES), lambda i: (i, 0))],
        out_shape=[jax.ShapeDtypeStruct((T, D), F32), jax.ShapeDtypeStruct((T * ROWS_PER_TOKEN, LANES), U32)],
        compiler_params=_params(("arbitrary",)),
        name="mixer_ab_out",
    )(o_f, o_b, r_a, o_att, xp, xs, mod, gla_g, diff_g, w_o, g_post, g_ffn)


def _gelu_tanh(x):
    return 0.5 * x * (1.0 + jnp.tanh(math.sqrt(2.0 / math.pi) * (x + 0.044715 * (x * x * x))))


def _sgu_kernel(xp_ref, xs_ref, mod_ref, gpre_ref, win_ref, bin_ref, vg_ref, ws_ref, bs_ref,
                wout_ref, gpost_ref, gffn_ref, o_ref, hp_ref, t_ref):
    m = mod_ref[0, 0]
    x = _stream_tile(xp_ref, xs_ref)
    h = _rms(x, gpre_ref[...]) * (1.0 + m[1:2]) + m[0:1]
    z = _gelu_tanh(_bdot(h, win_ref[...]) + bin_ref[...])
    v = _rms(z[:, SGU_DIM:], vg_ref[...])
    gw = SGU_DIM // SGU_GROUPS
    for ch in range(TM // SGU_CHUNK):
        rows = slice(ch * SGU_CHUNK, (ch + 1) * SGU_CHUNK)
        for g in range(SGU_GROUPS):
            cols = slice(g * gw, (g + 1) * gw)
            vs = _bdot(ws_ref[g], v[rows, cols]) + bs_ref[:, g:g + 1]
            t_ref[rows, cols] = (z[rows, cols] * vs).astype(BF16)
    out = jnp.dot(t_ref[...], wout_ref[...], preferred_element_type=F32)
    x1 = x + m[2:3] * _rms(out, gpost_ref[...])
    o_ref[...] = x1
    _store_token_tiles(hp_ref, _ffn_input_rows(x1, m, gffn_ref[...]))


def _sgu(xp, xs, mod, l, g_pre, w_in, b_in, v_g, w_s, b_s_t, w_out, g_post, g_ffn):
    tok = pl.BlockSpec((TM, D), lambda i: (i, 0))
    full = lambda *shape: pl.BlockSpec(shape, lambda i: (0,) * len(shape))
    return pl.pallas_call(
        _sgu_kernel,
        grid=(T // TM,),
        in_specs=_stream_specs() + [
            pl.BlockSpec((1, 1, 6, D), lambda i: (l, _mod_row(i), 0, 0)),
            full(1, D), full(D, 2 * SGU_DIM), full(1, 2 * SGU_DIM), full(1, SGU_DIM),
            full(SGU_GROUPS, SGU_CHUNK, SGU_CHUNK), full(SGU_CHUNK, SGU_GROUPS),
            full(SGU_DIM, D), full(1, D), full(1, D),
        ],
        out_specs=[tok, pl.BlockSpec((TM * ROWS_PER_TOKEN, LANES), lambda i: (i, 0))],
        out_shape=[jax.ShapeDtypeStruct((T, D), F32), jax.ShapeDtypeStruct((T * ROWS_PER_TOKEN, LANES), U32)],
        scratch_shapes=[pltpu.VMEM((TM, SGU_DIM), BF16)],
        compiler_params=_params(("arbitrary",)),
        name="sgu_mixer",
    )(xp, xs, mod, g_pre, w_in, b_in, v_g, w_s, b_s_t, w_out, g_post, g_ffn)


LANES = 128
U32 = jnp.uint32
PACKED = D // 2
ROWS_PER_TOKEN = PACKED // LANES
HIGH_HALF = 0xFFFF0000


def _pack_rows(x):
    bits = lax.bitcast_convert_type(x.astype(BF16).astype(F32), U32)
    return bits[:, :PACKED] | (bits[:, PACKED:] >> 16)


def _unpack_rows(u):
    return (lax.bitcast_convert_type(u & U32(HIGH_HALF), F32), lax.bitcast_convert_type(u << 16, F32))


def _store_token_tiles(ref, u, first=0):
    n = u.shape[0]
    for c in range(ROWS_PER_TOKEN):
        ref[pl.ds(first * ROWS_PER_TOKEN + c, n, stride=ROWS_PER_TOKEN), :] = u[:, c * LANES:(c + 1) * LANES]


def _load_token_tiles(ref, n, first=0):
    return jnp.concatenate([ref[pl.ds(first * ROWS_PER_TOKEN + c, n, stride=ROWS_PER_TOKEN), :]
                            for c in range(ROWS_PER_TOKEN)], axis=1)


def _ffn_input_rows(x, m, g):
    return _pack_rows(_rms(x, g) * (1.0 + m[4:5]) + m[3:4])


def _router_kernel(hp_ref, wr_ref, eb_ref, te_ref, wn_ref, rk_ref, cnt_ref, carry_ref):
    i = pl.program_id(0)

    @pl.when(i == 0)
    def _():
        carry_ref[...] = jnp.zeros_like(carry_ref)

    h_hi, h_lo = (t.astype(BF16) for t in _unpack_rows(_load_token_tiles(hp_ref, TR)))
    w1, w2, _ = _split3(wr_ref[...])
    nt = lambda a, b: lax.dot_general(a, b, (((1,), (1,)), ((), ())), preferred_element_type=F32)
    logits = (nt(w1[:, :PACKED], h_hi) + nt(w1[:, PACKED:], h_lo)
              + nt(w2[:, :PACKED], h_hi) + nt(w2[:, PACKED:], h_lo))
    scores = jax.nn.sigmoid(logits)
    sel = scores + eb_ref[...]

    row8 = lax.broadcasted_iota(I32, (GROUP_SIZE, TR), 0)
    gscore = []
    for g in range(N_GROUPS):
        xg = sel[g * GROUP_SIZE:(g + 1) * GROUP_SIZE]
        m1 = jnp.max(xg, axis=0, keepdims=True)
        i1 = jnp.min(jnp.where(xg == m1, row8, GROUP_SIZE), axis=0, keepdims=True)
        m2 = jnp.max(jnp.where(row8 == i1, NEG_INF, xg), axis=0, keepdims=True)
        gscore.append(m1 + m2)
    pieces = []
    for g in range(N_GROUPS):
        rank = jnp.zeros((1, TR), I32)
        for g2 in range(N_GROUPS):
            if g2 == g:
                continue
            beats = (gscore[g2] >= gscore[g]) if g2 < g else (gscore[g2] > gscore[g])
            rank = rank + beats.astype(I32)
        pieces.append(jnp.where(rank < TOPK_GROUPS, sel[g * GROUP_SIZE:(g + 1) * GROUP_SIZE], NEG_INF))
    cur = jnp.concatenate(pieces, axis=0)

    row = lax.broadcasted_iota(I32, (N_EXPERTS, TR), 0)
    idxs, ws = [], []
    for _ in range(TOP_K):
        mx = jnp.max(cur, axis=0, keepdims=True)
        idx = jnp.min(jnp.where(cur == mx, row, N_EXPERTS), axis=0, keepdims=True)
        hit = row == idx
        ws.append(jnp.sum(jnp.where(hit, scores, 0.0), axis=0, keepdims=True))
        cur = jnp.where(hit, NEG_INF, cur)
        idxs.append(idx)
    mask = jnp.zeros((N_EXPERTS, TR), F32)
    for idx in idxs:
        mask = mask + (row == idx).astype(F32)
    wsum = ws[0]
    for wk in ws[1:]:
        wsum = wsum + wk

    tj = lax.broadcasted_iota(I32, (TR, TR), 0)
    ti = lax.broadcasted_iota(I32, (TR, TR), 1)
    upper = jnp.where(tj < ti, 1.0, 0.0).astype(BF16)
    pos = carry_ref[...] + jnp.dot(mask.astype(BF16), upper, preferred_element_type=F32)
    for k in range(TOP_K):
        hit = row == idxs[k]
        te_ref[k:k + 1, :] = idxs[k]
        wn_ref[k:k + 1, :] = ws[k] / wsum * ROUTED_SCALE
        rk_ref[k:k + 1, :] = jnp.sum(jnp.where(hit, pos, 0.0), axis=0, keepdims=True).astype(I32)
    carry_ref[...] = carry_ref[...] + jnp.sum(mask, axis=1, keepdims=True)
    cnt_ref[...] = carry_ref[...]


def _router(hp, wr_t, e_bias):
    kt = lambda dtype: jax.ShapeDtypeStruct((TOP_K, T), dtype)
    kt_spec = pl.BlockSpec((TOP_K, TR), lambda i: (0, i))
    return pl.pallas_call(
        _router_kernel,
        grid=(T // TR,),
        in_specs=[
            pl.BlockSpec((TR * ROWS_PER_TOKEN, LANES), lambda i: (i, 0)),
            pl.BlockSpec((N_EXPERTS, D), lambda i: (0, 0)),
            pl.BlockSpec((N_EXPERTS, 1), lambda i: (0, 0)),
        ],
        out_specs=[
            kt_spec, kt_spec, kt_spec,
            pl.BlockSpec((N_EXPERTS, 1), lambda i: (0, 0)),
        ],
        out_shape=[
            kt(I32), kt(F32), kt(I32),
            jax.ShapeDtypeStruct((N_EXPERTS, 1), F32),
        ],
        scratch_shapes=[pltpu.VMEM((N_EXPERTS, 1), F32)],
        compiler_params=_params(("arbitrary",)),
        name="moe_router",
    )(hp, wr_t, e_bias)


_PAD_BITS = tuple(1 << b for b in range(GM.bit_length() - 1))


def _pad_fill_kernel(pad_start_ref, pad_len_ref, xg_in_ref, xg_ref, zero_ref, sem):
    del xg_in_ref
    zero_ref[...] = jnp.zeros_like(zero_ref)

    def pad_copies(e):
        start = pad_start_ref[e]
        n = pad_len_ref[e]
        copies = []
        for bit in _PAD_BITS:
            first = start + (n & ~(2 * bit - 1))
            copies.append(((n & bit) != 0, pltpu.make_async_copy(
                zero_ref.at[pl.ds(0, bit)], xg_ref.at[pl.ds(first, bit)], sem)))
        return copies

    def start_e(e, carry):
        for on, cp in pad_copies(e):
            @pl.when(on)
            def _():
                cp.start()
        return carry

    def wait_e(e, carry):
        for on, cp in pad_copies(e):
            @pl.when(on)
            def _():
                cp.wait()
        return carry

    lax.fori_loop(0, N_EXPERTS, start_e, 0)
    lax.fori_loop(0, N_EXPERTS, wait_e, 0)


def _pad_fill(pad_start, pad_len, xg):
    grid_spec = pltpu.PrefetchScalarGridSpec(
        num_scalar_prefetch=2,
        grid=(1,),
        in_specs=[pl.BlockSpec(memory_space=pl.ANY)],
        out_specs=pl.BlockSpec(memory_space=pl.ANY),
        scratch_shapes=[pltpu.VMEM((GM // 2, ROWS_PER_TOKEN, LANES), xg.dtype), pltpu.SemaphoreType.DMA],
    )
    return pl.pallas_call(
        _pad_fill_kernel,
        grid_spec=grid_spec,
        out_shape=jax.ShapeDtypeStruct(xg.shape, xg.dtype),
        input_output_aliases={2: 0},
        compiler_params=_params(("arbitrary",)),
        name="moe_pad_fill",
    )(pad_start, pad_len, xg)


SC_CORES, SC_SUBCORES = 2, 16
SC_WORKERS = SC_CORES * SC_SUBCORES
SC_W = 64


def _sc_worker_id():
    return lax.axis_index("s") * SC_CORES + lax.axis_index("c")


def _sc_dispatch(h3, slot3):
    nchunk = T // SC_WORKERS // SC_W
    mesh = plsc.VectorSubcoreMesh(core_axis_name="c", subcore_axis_name="s")
    tile = (SC_W, ROWS_PER_TOKEN, LANES)

    @functools.partial(
        pl.kernel, mesh=mesh,
        out_type=jax.ShapeDtypeStruct((SP, ROWS_PER_TOKEN, LANES), h3.dtype),
        scratch_types=[pltpu.VMEM((TOP_K, SC_W), I32), pltpu.VMEM((TOP_K, SC_W), I32),
                       pltpu.VMEM(tile, h3.dtype), pltpu.VMEM(tile, h3.dtype),
                       pltpu.SemaphoreType.DMA((2,)), pltpu.SemaphoreType.DMA((2,))],
    )
    def k(h_hbm, slot_hbm, xg_hbm, idx0, idx1, rows0, rows1, lsem, ssem):
        first = _sc_worker_id() * nchunk
        idx = (idx0, idx1)
        rows = (rows0, rows1)

        def loads(j, b):
            blk = first + j
            tok = pl.multiple_of(blk * SC_W, SC_W)
            return (pltpu.make_async_copy(slot_hbm.at[blk], idx[b], lsem.at[b]),
                    pltpu.make_async_copy(h_hbm.at[pl.ds(tok, SC_W)], rows[b], lsem.at[b]))

        def scatters(b):
            return [pltpu.make_async_copy(rows[b], xg_hbm.at[idx[b].at[kk]], ssem.at[b]) for kk in range(TOP_K)]

        for cp in loads(0, 0):
            cp.start()

        @pl.loop(0, nchunk, step=2)
        def _(j):
            for b in (0, 1):
                jj = j + b
                for cp in loads(jj, b):
                    cp.wait()
                for cp in scatters(b):
                    cp.start()

                @pl.when(jj + 1 < nchunk)
                def _():
                    @pl.when(jj >= 1)
                    def _():
                        for cp in scatters(1 - b):
                            cp.wait()
                    for cp in loads(jj + 1, 1 - b):
                        cp.start()

        for b in (0, 1):
            for cp in scatters(b):
                cp.wait()

    return k(h3, slot3)


def _gmm_kernel(layer, tile_e_ref, tile_blk_ref, tile_nsub_ref, tile_run_ref, tile_next_ref,
                x_ref, wg_hbm, wu_hbm, wd_hbm, y_ref, wg_st, wu_st, wd_st, wgu_scr, wd_scr, sems):
    j = pl.program_id(0)
    nsub = tile_nsub_ref[j]
    run = tile_run_ref[j]

    def weight_copies(e, slot):
        return [pltpu.make_async_copy(src.at[layer, e], dst.at[slot], sems.at[slot])
                for src, dst in ((wg_hbm, wg_st), (wu_hbm, wu_st), (wd_hbm, wd_st))]

    @pl.when(run >= 0)
    def _():
        @pl.when(j == 0)
        def _():
            for cp in weight_copies(tile_e_ref[j], run):
                cp.start()

        for cp in weight_copies(tile_e_ref[j], run):
            cp.wait()

        nxt = tile_next_ref[j]

        @pl.when(nxt >= 0)
        def _():
            for cp in weight_copies(nxt, 1 - run):
                cp.start()

        wgu_scr[:, 0:D_EXPERT] = wg_st[run].astype(BF16)
        wgu_scr[:, D_EXPERT:2 * D_EXPERT] = wu_st[run].astype(BF16)
        wd_scr[...] = wd_st[run].astype(BF16)

    def expert_mlp(s):
        x_hi, x_lo = _unpack_rows(_load_token_tiles(x_ref, GM_SUB, s * GM_SUB))
        gu = (jnp.dot(x_hi.astype(BF16), wgu_scr[0:PACKED, :], preferred_element_type=F32)
              + jnp.dot(x_lo.astype(BF16), wgu_scr[PACKED:D, :], preferred_element_type=F32))
        hid = _silu(gu[:, 0:D_EXPERT]) * gu[:, D_EXPERT:2 * D_EXPERT]
        y = jnp.dot(hid.astype(BF16), wd_scr[...], preferred_element_type=F32)
        _store_token_tiles(y_ref, _pack_rows(y), s * GM_SUB)

    for n in range(1, GM // GM_SUB + 1):
        @pl.when(nsub == n)
        def _():
            for s in range(n):
                expert_mlp(s)
            for s in range(n, GM // GM_SUB):
                _store_token_tiles(y_ref, jnp.zeros((GM_SUB, PACKED), U32), s * GM_SUB)


def _gmm(tile_e, tile_blk, tile_nsub, tile_run, tile_next, xg, l, w_gate, w_up, w_down):
    row_tile = pl.BlockSpec((GM * ROWS_PER_TOKEN, LANES), lambda j, te, tb, *_: (tb[j], 0))
    hbm = pl.BlockSpec(memory_space=pl.ANY)
    grid_spec = pltpu.PrefetchScalarGridSpec(
        num_scalar_prefetch=5,
        grid=(NT_MAX,),
        in_specs=[row_tile, hbm, hbm, hbm],
        out_specs=row_tile,
        scratch_shapes=[pltpu.VMEM((2, D, D_EXPERT), F32), pltpu.VMEM((2, D, D_EXPERT), F32),
                        pltpu.VMEM((2, D_EXPERT, D), F32),
                        pltpu.VMEM((D, 2 * D_EXPERT), BF16), pltpu.VMEM((D_EXPERT, D), BF16),
                        pltpu.SemaphoreType.DMA((2,))],
    )
    return pl.pallas_call(
        functools.partial(_gmm_kernel, l),
        grid_spec=grid_spec,
        out_shape=jax.ShapeDtypeStruct((SP * ROWS_PER_TOKEN, LANES), U32),
        compiler_params=_params(("arbitrary",)),
        name="moe_grouped_matmul",
    )(tile_e, tile_blk, tile_nsub, tile_run, tile_next, xg, w_gate, w_up, w_down)


def _sc_gather(table3, idx):
    n_idx = idx.shape[0]
    per_w = n_idx // SC_WORKERS
    nchunk = per_w // SC_W
    mesh = plsc.VectorSubcoreMesh(core_axis_name="c", subcore_axis_name="s")
    tile = (SC_W, ROWS_PER_TOKEN, LANES)

    @functools.partial(
        pl.kernel, mesh=mesh,
        out_type=jax.ShapeDtypeStruct((n_idx, ROWS_PER_TOKEN, LANES), table3.dtype),
        scratch_types=[pltpu.VMEM((per_w,), I32), pltpu.VMEM(tile, table3.dtype), pltpu.VMEM(tile, table3.dtype),
                       pltpu.SemaphoreType.DMA((2,)), pltpu.SemaphoreType.DMA((2,))],
    )
    def k(table_hbm, idx_hbm, out_hbm, idx_v, rows0, rows1, gsem, wsem):
        base = pl.multiple_of(_sc_worker_id() * per_w, per_w)
        rows = (rows0, rows1)
        pltpu.sync_copy(idx_hbm.at[pl.ds(base, per_w)], idx_v)

        def gather(j, b):
            ids = idx_v.at[pl.ds(pl.multiple_of(j * SC_W, SC_W), SC_W)]
            return pltpu.make_async_copy(table_hbm.at[ids], rows[b], gsem.at[b])

        def write(j, b):
            dst = out_hbm.at[pl.ds(pl.multiple_of(base + j * SC_W, SC_W), SC_W)]
            return pltpu.make_async_copy(rows[b], dst, wsem.at[b])

        gather(0, 0).start()

        @pl.loop(0, nchunk, step=2)
        def _(j):
            for b in (0, 1):
                jj = j + b
                gather(jj, b).wait()
                write(jj, b).start()

                @pl.when(jj + 1 < nchunk)
                def _():
                    @pl.when(jj >= 1)
                    def _():
                        write(jj - 1, 1 - b).wait()
                    gather(jj + 1, 1 - b).start()

        write(nchunk - 2, 0).wait()
        write(nchunk - 1, 1).wait()

    return k(table3, idx)


def _combine_kernel(wn_ref, x_ref, mod_ref, gpre_ref, gp_ref, wsg_ref, wsu_ref, wsd_ref, y_ref, o_ref):
    m = mod_ref[0, 0]
    x = x_ref[...]
    hb = (_rms(x, gpre_ref[...]) * (1.0 + m[4:5]) + m[3:4]).astype(BF16)
    hid = (_silu(jnp.dot(hb, wsg_ref[...], preferred_element_type=F32))
           * jnp.dot(hb, wsu_ref[...], preferred_element_type=F32))
    acc = jnp.dot(hid.astype(BF16), wsd_ref[...], preferred_element_type=F32)

    r = lax.broadcasted_iota(I32, (TD, TD), 0)
    c = lax.broadcasted_iota(I32, (TD, TD), 1)
    eye = jnp.where(r == c, 1.0, 0.0).astype(BF16)
    nt = lambda a, b: lax.dot_general(a, b, (((1,), (1,)), ((), ())), preferred_element_type=F32)
    w1, w2, w3 = _split3(wn_ref[...])
    w_t = nt(eye, w1) + nt(eye, w2) + nt(eye, w3)

    acc_hi = acc[:, :PACKED]
    acc_lo = acc[:, PACKED:]
    for k in range(TOP_K):
        y_hi, y_lo = _unpack_rows(_load_token_tiles(y_ref, TD, k * TD))
        acc_hi = acc_hi + y_hi * w_t[:, k:k + 1]
        acc_lo = acc_lo + y_lo * w_t[:, k:k + 1]
    acc = jnp.concatenate([acc_hi, acc_lo], axis=1)
    o_ref[...] = x + m[5:6] * _rms(acc, gp_ref[...])


def _combine(wn, x, mod, l, g_pre, g_post, ws_gate, ws_up, ws_down, ybuf, first_tok, n_tok):
    off = first_tok // TD
    tiles_per_dec = DEC_SEQ // TD
    npd = TP // TD
    mod_row = lambda i: jnp.where(i + off < npd, 0, 1 + (i + off - npd) // tiles_per_dec)
    full = lambda *shape: pl.BlockSpec(shape, lambda i: (0,) * len(shape))
    y_spec = pl.BlockSpec((TOP_K * TD * ROWS_PER_TOKEN, LANES), lambda i: (i, 0))
    return pl.pallas_call(
        _combine_kernel,
        grid=(n_tok // TD,),
        in_specs=[
            pl.BlockSpec((TOP_K, TD), lambda i: (0, i + off)),
            pl.BlockSpec((TD, D), lambda i: (i + off, 0)),
            pl.BlockSpec((1, 1, 6, D), lambda i: (l, mod_row(i), 0, 0)),
            full(1, D), full(1, D), full(D, D_SHARED), full(D, D_SHARED), full(D_SHARED, D),
            y_spec,
        ],
        out_specs=pl.BlockSpec((TD, D), lambda i: (i, 0)),
        out_shape=jax.ShapeDtypeStruct((n_tok, D), F32),
        compiler_params=_params(("arbitrary",)),
        name="moe_combine",
    )(wn, x, mod, g_pre, g_post, ws_gate, ws_up, ws_down, ybuf)


def _moe_layer(x, h, mod, l, g_pre, g_post, w_router, e_bias, w_gate, w_up, w_down,
               ws_gate, ws_up, ws_down):
    top_e, wn, rk, cnt = _router(h, w_router.T, e_bias.reshape(N_EXPERTS, 1))
    cnt = cnt.reshape(N_EXPERTS).astype(I32)
    padded = (cnt + GM - 1) // GM * GM
    ends = jnp.cumsum(padded)
    offs = ends - padded
    eid = jnp.arange(N_EXPERTS, dtype=I32)[:, None, None]
    slot = rk + jnp.sum(jnp.where(top_e[None] == eid, offs[:, None, None], 0), axis=0)
    nvalid = ends[-1] // GM
    tile_start = jnp.arange(NT_MAX, dtype=I32) * GM
    tile_raw = jnp.sum((tile_start[:, None] >= ends[None, :]).astype(I32), axis=1)
    last = jnp.maximum(nvalid - 1, 0)
    tile_blk = jnp.minimum(jnp.arange(NT_MAX, dtype=I32), last)
    tile_e = jnp.minimum(tile_raw, N_EXPERTS - 1)
    tile_e = jnp.where(jnp.arange(NT_MAX) <= last, tile_e, tile_e[last])
    own = tile_e[:, None] == jnp.arange(N_EXPERTS, dtype=I32)[None, :]
    tile_rows = jnp.sum(jnp.where(own, (offs + cnt)[None, :], 0), axis=1) - tile_start
    tile_nsub = jnp.where(jnp.arange(NT_MAX) <= last, (jnp.clip(tile_rows, 0, GM) + GM_SUB - 1) // GM_SUB, 0)
    starts = jnp.logical_and(jnp.arange(NT_MAX) <= last,
                             jnp.concatenate([jnp.ones((1,), bool), tile_e[1:] != tile_e[:-1]]))
    tile_run = jnp.where(starts, (jnp.cumsum(starts.astype(I32)) - 1) % 2, -1)
    later = jnp.logical_and(jnp.arange(N_EXPERTS, dtype=I32)[None, :] > tile_e[:, None], (cnt > 0)[None, :])
    tile_next = jnp.min(jnp.where(later, jnp.arange(N_EXPERTS, dtype=I32)[None, :], N_EXPERTS), axis=1)
    tile_next = jnp.where(tile_next < N_EXPERTS, tile_next, -1)
    slot3 = slot.reshape(TOP_K, T // SC_W, SC_W).transpose(1, 0, 2)
    xg = _sc_dispatch(h.reshape(T, ROWS_PER_TOKEN, LANES), slot3)
    xg = _pad_fill(offs + cnt, padded - cnt, xg).reshape(SP * ROWS_PER_TOKEN, LANES)
    yg = _gmm(tile_e, tile_blk, tile_nsub.astype(I32), tile_run.astype(I32), tile_next.astype(I32),
              xg, l, w_gate, w_up, w_down)
    yg3 = yg.reshape(SP, ROWS_PER_TOKEN, LANES)
    ws = (ws_gate.astype(BF16), ws_up.astype(BF16), ws_down.astype(BF16))
    outs = []
    for first_tok, n_tok in ((0, TP), (TP, TS)):
        ids = slot[:, first_tok:first_tok + n_tok].reshape(TOP_K, n_tok // TD, TD).transpose(1, 0, 2)
        ybuf = _sc_gather(yg3, ids.reshape(TOP_K * n_tok))
        outs.append(_combine(wn, x, mod, l, g_pre, g_post, *ws,
                             ybuf.reshape(TOP_K * n_tok * ROWS_PER_TOKEN, LANES), first_tok, n_tok))
    return outs


def _rope_tables():
    n = DEC_SEQ
    rows = n // GRID_W
    row = jnp.repeat(jnp.arange(rows), GRID_W).astype(F32)
    col = jnp.tile(jnp.arange(GRID_W), rows).astype(F32)
    half = DQK_B // 2
    inv = ROPE_BASE ** (-jnp.arange(0, half, 2, dtype=F32) / half)
    ang_r = row[:, None] * inv
    ang_c = col[:, None] * inv
    ang = jnp.concatenate([ang_r, ang_r, ang_c, ang_c], axis=-1)
    reps = QK_B // DQK_B
    return jnp.tile(jnp.cos(ang), (1, reps)), jnp.tile(jnp.sin(ang), (1, reps))


def _pad_in_proj(w):
    s = [0, Q_A, 2 * Q_A, 2 * Q_A + V_A, 2 * Q_A + 2 * V_A]
    s += [s[-1] + GATE_RANK, s[-1] + 2 * GATE_RANK]
    s += [s[-1] + QK_B, s[-1] + 2 * QK_B, s[-1] + 2 * QK_B + V_B]
    gates = jnp.pad(w[:, s[4]:s[6]], ((0, 0), (0, GL_PAD - 2 * GATE_RANK)))
    return jnp.concatenate([w[:, s[0]:s[4]], gates, w[:, s[6]:s[9]]], axis=1).astype(BF16)


def kernel(x_prompt, x_sample, c, c_ctx, state_gla, cache_k, cache_v, ada_w, ada_b, norm_pre_mix, norm_post_mix, norm_pre_ffn, norm_post_ffn, ab_w_in, gla_w_g2, gla_b_g2, gla_norm_g, diff_lambda, diff_norm_g, ab_w_out, sgu_w_in, sgu_b_in, sgu_norm_g, sgu_w_s, sgu_b_s, sgu_w_out, moe_w_router, moe_e_bias, moe_w_gate, moe_w_up, moe_w_down, moe_ws_gate, moe_ws_up, moe_ws_down):
    depth = ada_w.shape[0]
    xp, xs = x_prompt.reshape(TP, D), x_sample.reshape(TS, D)
    cond = jnp.concatenate([c_ctx[None, :], c, jnp.zeros((8 - 1 - DEC_BATCH, D), F32)], axis=0)
    mod = _modulation(cond, ada_w, ada_b)
    cos, sin = _rope_tables()
    vec = lambda a: a.reshape(1, -1)
    new_s = new_k = new_v = None
    for l in range(depth):
        if l % 2 == 0:
            e = l // 2
            lam_init = 0.8 - 0.6 * math.exp(-0.3 * l)
            a, r_a, gl, q_b, k_b, v_b, new_k, new_v = _in_proj(xp, xs, mod, l, vec(norm_pre_mix[l]),
                                                               _pad_in_proj(ab_w_in[e]), cos, sin)
            s0_t = jnp.swapaxes(state_gla[:, e], -1, -2)
            same_head = jnp.eye(H_A, dtype=bool)[None, None, :, None, :, None]
            s0_t = jnp.where(same_head, s0_t[:, :, :, :, None, :], 0.0).reshape(DEC_BATCH, 2, V_A, Q_A)
            o_f, o_bw, s_fin_t = _gla(a, gl, gla_w_g2[e], gla_b_g2[e].reshape(2, 1, Q_A), s0_t)
            o_att = _diff_attention(q_b, k_b, v_b, cache_k, cache_v, diff_lambda[e], lam_init)
            x, h = _mix_out(lam_init, o_f, o_bw, r_a, o_att, xp, xs, mod, l, vec(gla_norm_g[e]), vec(diff_norm_g[e]),
                            ab_w_out[e].astype(BF16), vec(norm_post_mix[l]), vec(norm_pre_ffn[l]))
            new_s = jnp.swapaxes(s_fin_t, -1, -2)[:, None]
        else:
            o = l // 2
            x, h = _sgu(xp, xs, mod, l, vec(norm_pre_mix[l]), sgu_w_in[o].astype(BF16), vec(sgu_b_in[o]),
                        vec(sgu_norm_g[o]), sgu_w_s[o], sgu_b_s[o].T, sgu_w_out[o].astype(BF16),
                        vec(norm_post_mix[l]), vec(norm_pre_ffn[l]))
        xp, xs = _moe_layer(x, h, mod, l, vec(norm_pre_ffn[l]), vec(norm_post_ffn[l]), moe_w_router[l], moe_e_bias[l],
                            moe_w_gate, moe_w_up, moe_w_down, moe_ws_gate[l], moe_ws_up[l], moe_ws_down[l])
    y_prompt = xp.reshape(BATCH, SEQ, D)
    y_sample = xs.reshape(DEC_BATCH, DEC_SEQ, D)
    return (y_prompt, y_sample, new_s, new_k, new_v)
```

```python
import functools
import math

import jax
import jax.numpy as jnp
from jax import lax
from jax.experimental import pallas as pl
from jax.experimental.pallas import tpu as pltpu
from jax.experimental.pallas import tpu_sc as plsc

F32 = jnp.float32
BF16 = jnp.bfloat16
I32 = jnp.int32

D = 1024
BATCH, SEQ = 32, 256
DEC_BATCH, DEC_SEQ = 4, 2048
PAST_LEN = 256
GRID_W = 64
EPS = 1e-6
TP = BATCH * SEQ
TS = DEC_BATCH * DEC_SEQ
T = TP + TS
H_A, DK_A, DV_A = 4, 64, 128
Q_A, V_A = H_A * DK_A, H_A * DV_A
GATE_RANK, GATE_TAU, GLA_CHUNK = 16, 16.0, 64
H_B, DQK_B, DV_B = 4, 64, 128
QK_B, V_B = H_B * 2 * DQK_B, H_B * DV_B
ROPE_BASE = 10000.0
SGU_DIM, SGU_GROUPS, SGU_CHUNK = 1024, 4, 128
N_EXPERTS, TOP_K, N_GROUPS, TOPK_GROUPS = 64, 8, 8, 4
GROUP_SIZE = N_EXPERTS // N_GROUPS
D_EXPERT, D_SHARED = 256, 256
ROUTED_SCALE = 2.5

TM = 512
TM_SUB = 256
NPT = TP // TM
TILES_PER_DEC = DEC_SEQ // TM
SEG = 256
NSEG = T // SEG
NSEG_P = TP // SEG
SEG_PER_DEC = DEC_SEQ // SEG
TR = 512
TD = 512
GM = 1024
GM_SUB = 256
NT_MAX = T * TOP_K // GM + N_EXPERTS
SP = NT_MAX * GM
GL_PAD = 128
VMEM_LIMIT = 56 * 1024 * 1024
NEG_INF = float("-inf")


def _bdot(a, b):
    return jnp.dot(a.astype(BF16), b.astype(BF16), preferred_element_type=F32)


def _bdot_nt(a, b):
    return lax.dot_general(a.astype(BF16), b.astype(BF16), (((1,), (1,)), ((), ())),
                           preferred_element_type=F32)


def _bdot_tn(a, b):
    return lax.dot_general(a.astype(BF16), b.astype(BF16), (((0,), (0,)), ((), ())),
                           preferred_element_type=F32)


def _split3(x):
    x1 = x.astype(BF16)
    r1 = x - x1.astype(F32)
    x2 = r1.astype(BF16)
    x3 = (r1 - x2.astype(F32)).astype(BF16)
    return x1, x2, x3


def _rms(x, g):
    return x * lax.rsqrt(jnp.mean(x * x, axis=-1, keepdims=True) + EPS) * g


def _silu(x):
    return x * jax.nn.sigmoid(x)


def _mod_row(i):
    return jnp.where(i < NPT, 0, 1 + (i - NPT) // TILES_PER_DEC)


def _params(sem, limit=VMEM_LIMIT):
    return pltpu.CompilerParams(dimension_semantics=sem, vmem_limit_bytes=limit)


def _mod_kernel(c_ref, w_ref, b_ref, o_ref):
    o_ref[0] = _bdot(_silu(c_ref[...]), w_ref[0]) + b_ref[0]


def _modulation(cond, ada_w, ada_b):
    depth = ada_w.shape[0]
    nj = 6
    out = pl.pallas_call(
        _mod_kernel,
        grid=(depth, nj),
        in_specs=[
            pl.BlockSpec((8, D), lambda l, j: (0, 0)),
            pl.BlockSpec((1, D, D), lambda l, j: (l, 0, j)),
            pl.BlockSpec((1, 1, D), lambda l, j: (l, 0, j)),
        ],
        out_specs=pl.BlockSpec((1, 8, D), lambda l, j: (l, 0, j)),
        out_shape=jax.ShapeDtypeStruct((depth, 8, 6 * D), F32),
        compiler_params=_params(("arbitrary", "arbitrary")),
        name="adaln_modulation",
    )(cond, ada_w, ada_b.reshape(depth, 1, 6 * D))
    return out.reshape(depth, 8, 6, D)


_C_A = 0
_C_R = _C_A + 2 * Q_A + V_A
_C_GL = _C_R + V_A
_C_Q = _C_GL + GL_PAD
_C_K = _C_Q + QK_B
_C_V = _C_K + QK_B
_C_END = _C_V + V_B
ROT_PAIR = DQK_B // 4


def _rope(x, cos, sin):
    lane = lax.broadcasted_iota(I32, x.shape, 1)
    first = (lane % (2 * ROT_PAIR)) < ROT_PAIR
    n = x.shape[1]
    xr = jnp.where(first, -pltpu.roll(x, n - ROT_PAIR, 1), pltpu.roll(x, ROT_PAIR, 1))
    return x * cos + xr * sin


def _stream_specs():
    return [pl.BlockSpec((TM, D), lambda i: (jnp.minimum(i, NPT - 1), 0)),
            pl.BlockSpec((TM, D), lambda i: (jnp.maximum(i - NPT, 0), 0))]


def _stream_tile(xp_ref, xs_ref):
    return jnp.where(pl.program_id(0) < NPT, xp_ref[...], xs_ref[...])


def _in_kernel(xp_ref, xs_ref, mod_ref, g_ref, w_ref, cos_ref, sin_ref,
               a_ref, r_ref, gl_ref, q_ref, k_ref, v_ref, ck_ref, cv_ref):
    latent = pl.program_id(0) >= NPT
    m = mod_ref[0, 0]
    x = _stream_tile(xp_ref, xs_ref)
    for s in range(TM // TM_SUB):
        rows = slice(s * TM_SUB, (s + 1) * TM_SUB)
        h = _rms(x[rows], g_ref[...]) * (1.0 + m[1:2]) + m[0:1]
        hb = h.astype(BF16)

        def proj(c0, c1):
            return jnp.dot(hb, w_ref[:, c0:c1], preferred_element_type=F32)

        a_ref[rows, :] = proj(_C_A, _C_R)
        r_ref[rows, :] = proj(_C_R, _C_GL)
        gl_ref[rows, :] = proj(_C_GL, _C_Q)
        v_ref[rows, :] = proj(_C_V, _C_END)
        q = proj(_C_Q, _C_K)
        k = proj(_C_K, _C_V)
        cos = cos_ref[rows, :]
        sin = sin_ref[rows, :]
        q_ref[rows, :] = jnp.where(latent, _rope(q, cos, sin), q)
        k_ref[rows, :] = jnp.where(latent, _rope(k, cos, sin), k)

    @pl.when(jnp.logical_not(latent))
    def _():
        for s in range(TM // SEQ):
            rows = slice(s * SEQ, (s + 1) * SEQ)
            for h in range(H_B):
                cv_ref[s, 0, h] = v_ref[rows, h * DV_B:(h + 1) * DV_B]
                for mp in range(2):
                    ck_ref[s, 0, h, mp] = k_ref[rows, (2 * h + mp) * DQK_B:(2 * h + mp + 1) * DQK_B]


def _in_proj(xp, xs, mod, l, g, w_pad, cos, sin):
    tok = lambda width: pl.BlockSpec((TM, width), lambda i: (i, 0))
    rope_spec = pl.BlockSpec((TM, QK_B), lambda i: (jnp.maximum(i - NPT, 0) % TILES_PER_DEC, 0))
    widths = (2 * Q_A + V_A, V_A, GL_PAD, QK_B, QK_B, V_B)
    seqs = TM // SEQ
    prompt_tile = lambda i: jnp.minimum(i, NPT - 1)
    cache_specs = [pl.BlockSpec((seqs, 1, H_B, 2, SEQ, DQK_B), lambda i: (prompt_tile(i), 0, 0, 0, 0, 0)),
                   pl.BlockSpec((seqs, 1, H_B, SEQ, DV_B), lambda i: (prompt_tile(i), 0, 0, 0, 0))]
    cache_shapes = [jax.ShapeDtypeStruct((BATCH, 1, H_B, 2, SEQ, DQK_B), F32),
                    jax.ShapeDtypeStruct((BATCH, 1, H_B, SEQ, DV_B), F32)]
    return pl.pallas_call(
        _in_kernel,
        grid=(T // TM,),
        in_specs=_stream_specs() + [
            pl.BlockSpec((1, 1, 6, D), lambda i: (l, _mod_row(i), 0, 0)),
            pl.BlockSpec((1, D), lambda i: (0, 0)),
            pl.BlockSpec((D, _C_END), lambda i: (0, 0)),
            rope_spec, rope_spec,
        ],
        out_specs=[tok(w) for w in widths] + cache_specs,
        out_shape=[jax.ShapeDtypeStruct((T, w), F32) for w in widths] + cache_shapes,
        compiler_params=_params(("arbitrary",)),
        name="mixer_ab_in_proj",
    )(xp, xs, mod, g, w_pad, cos, sin)


def _log_sigmoid(x):
    return jnp.minimum(x, 0.0) - jnp.log(1.0 + jnp.exp(-jnp.abs(x)))


def _gla_kernel(af_ref, ab_ref, glf_ref, glb_ref, wg_ref, bg_ref, s0_ref,
                of_ref, ob_ref, sfin_ref, st_ref):
    i = pl.program_id(0)

    @pl.when(i < NSEG_P)
    def _():
        st_ref[...] = jnp.zeros_like(st_ref)

    @pl.when(jnp.logical_and(i >= NSEG_P, (i - NSEG_P) % SEG_PER_DEC == 0))
    def _():
        st_ref[...] = s0_ref[0]

    r = lax.broadcasted_iota(I32, (SEG, SEG), 0)
    c = lax.broadcasted_iota(I32, (SEG, SEG), 1)
    same = (r // GLA_CHUNK) == (c // GLA_CHUNK)
    nchunk = SEG // GLA_CHUNK
    own_head = (lax.broadcasted_iota(I32, (V_A, Q_A), 0) // DV_A) == (lax.broadcasted_iota(I32, (V_A, Q_A), 1) // DK_A)

    for d, (a_ref, gl_ref, o_ref) in enumerate(((af_ref, glf_ref, of_ref), (ab_ref, glb_ref, ob_ref))):
        fwd = d == 0
        gcol = gl_ref[:, d * GATE_RANK:(d + 1) * GATE_RANK]
        la = _log_sigmoid(_bdot(gcol, wg_ref[d]) + bg_ref[d]) / GATE_TAU
        causal = jnp.logical_and(same, (c <= r) if fwd else (c >= r))
        tri = jnp.where(causal, 1.0, 0.0).astype(BF16)
        l1, l2, l3 = _split3(la)
        b_all = (jnp.dot(tri, l1, preferred_element_type=F32)
                 + jnp.dot(tri, l2, preferred_element_type=F32)
                 + jnp.dot(tri, l3, preferred_element_type=F32))
        q_in_all = a_ref[:, 0:Q_A] * (DK_A ** -0.5) * jnp.exp(b_all)
        kd_all = a_ref[:, Q_A:2 * Q_A] * jnp.exp(-b_all)
        intra = []
        for h in range(H_A):
            kc = slice(h * DK_A, (h + 1) * DK_A)
            attn = jnp.where(causal, _bdot_nt(q_in_all[:, kc], kd_all[:, kc]), 0.0)
            intra.append(_bdot(attn, a_ref[:, 2 * Q_A + h * DV_A:2 * Q_A + (h + 1) * DV_A]))
        intra = jnp.concatenate(intra, axis=1)
        state = st_ref[d]
        order = range(nchunk) if fwd else range(nchunk - 1, -1, -1)
        for ch in order:
            r0 = ch * GLA_CHUNK
            rows = slice(r0, r0 + GLA_CHUNK)
            end = r0 + GLA_CHUNK - 1 if fwd else r0
            b_end = b_all[end:end + 1, :]
            kw = a_ref[rows, Q_A:2 * Q_A] * jnp.exp(b_end - b_all[rows, :])
            o_ref[rows, :] = intra[rows, :] + _bdot_nt(q_in_all[rows, :], state)
            kv_t = _bdot_tn(a_ref[rows, 2 * Q_A:2 * Q_A + V_A], kw)
            state = state * jnp.exp(b_end) + jnp.where(own_head, kv_t, 0.0)
        st_ref[d] = state

    @pl.when(i < NSEG_P)
    def _():
        for d in range(2):
            for h in range(H_A):
                sfin_ref[0, d, h] = st_ref[d, h * DV_A:(h + 1) * DV_A, h * DK_A:(h + 1) * DK_A]


def _seg_bwd(i):
    j = i - NSEG_P
    return jnp.where(i < NSEG_P, i, NSEG_P + (j // SEG_PER_DEC) * SEG_PER_DEC + (SEG_PER_DEC - 1 - j % SEG_PER_DEC))


def _gla(a, gl, wg, bg, s0_t):
    seg = lambda width, f: pl.BlockSpec((SEG, width), lambda i: (f(i), 0))
    ident = lambda i: i
    st_block = (1, 2, H_A, DV_A, DK_A)
    return pl.pallas_call(
        _gla_kernel,
        grid=(NSEG,),
        in_specs=[
            seg(D, ident), seg(D, _seg_bwd), seg(GL_PAD, ident), seg(GL_PAD, _seg_bwd),
            pl.BlockSpec((2, GATE_RANK, Q_A), lambda i: (0, 0, 0)),
            pl.BlockSpec((2, 1, Q_A), lambda i: (0, 0, 0)),
            pl.BlockSpec((1, 2, V_A, Q_A), lambda i: (jnp.maximum(i - NSEG_P, 0) // SEG_PER_DEC, 0, 0, 0)),
        ],
        out_specs=[
            seg(V_A, ident), seg(V_A, _seg_bwd),
            pl.BlockSpec(st_block, lambda i: (jnp.minimum(i, NSEG_P - 1), 0, 0, 0, 0)),
        ],
        out_shape=[
            jax.ShapeDtypeStruct((T, V_A), F32),
            jax.ShapeDtypeStruct((T, V_A), F32),
            jax.ShapeDtypeStruct((BATCH, 2, H_A, DV_A, DK_A), F32),
        ],
        scratch_shapes=[pltpu.VMEM((2, V_A, Q_A), F32)],
        compiler_params=_params(("arbitrary",)),
        name="gla_bidir",
    )(a, a, gl, gl, wg, bg, s0_t)


def _diff_lambda(lam_ref, lam_init):
    lp = lam_ref[...]
    s01 = jnp.sum(lp[0:1] * lp[1:2], axis=1, keepdims=True)
    s23 = jnp.sum(lp[2:3] * lp[3:4], axis=1, keepdims=True)
    return jnp.exp(s01) - jnp.exp(s23) + lam_init


def _attn_prompt_kernel(lam_init, q_ref, k_ref, v_ref, lam_ref, o_ref):
    lam = _diff_lambda(lam_ref, lam_init)
    for h in range(H_B):
        ps = []
        for m in range(2):
            cols = slice((2 * h + m) * DQK_B, (2 * h + m + 1) * DQK_B)
            s = _bdot_nt(q_ref[:, cols] * (DQK_B ** -0.5), k_ref[:, cols])
            e = jnp.exp(s - jnp.max(s, axis=1, keepdims=True))
            ps.append(e * (1.0 / jnp.sum(e, axis=1, keepdims=True)))
        w = ps[0] - lam * ps[1]
        o_ref[:, h * DV_B:(h + 1) * DV_B] = _bdot(w, v_ref[:, h * DV_B:(h + 1) * DV_B])


def _attn_sample_kernel(lam_init, q_ref, k_ref, v_ref, ck_ref, cv_ref, lam_ref, o_ref):
    lam = _diff_lambda(lam_ref, lam_init)
    for h in range(H_B):
        parts = []
        for m in range(2):
            cols = slice((2 * h + m) * DQK_B, (2 * h + m + 1) * DQK_B)
            q = q_ref[:, cols] * (DQK_B ** -0.5)
            sc = _bdot_nt(q, ck_ref[0, 0, h, m])
            sn = _bdot_nt(q, k_ref[:, cols])
            mx = jnp.maximum(jnp.max(sc, axis=1, keepdims=True), jnp.max(sn, axis=1, keepdims=True))
            ec = jnp.exp(sc - mx)
            en = jnp.exp(sn - mx)
            inv = (1.0 if m == 0 else -lam) / (jnp.sum(ec, axis=1, keepdims=True) + jnp.sum(en, axis=1, keepdims=True))
            parts.append((ec * inv, en * inv))
        wc = parts[0][0] + parts[1][0]
        wn = parts[0][1] + parts[1][1]
        o_ref[:, h * DV_B:(h + 1) * DV_B] = (_bdot(wc, cv_ref[0, 0, h])
                                             + _bdot(wn, v_ref[:, h * DV_B:(h + 1) * DV_B]))


QB = SEQ
NQB_DEC = DEC_SEQ // QB


def _attn_kernel(lam_init, q_ref, kp_ref, vp_ref, ks_ref, vs_ref, ck_ref, cv_ref, lam_ref, o_ref):
    i = pl.program_id(0)

    @pl.when(i < BATCH)
    def _():
        _attn_prompt_kernel(lam_init, q_ref, kp_ref, vp_ref, lam_ref, o_ref)

    @pl.when(i >= BATCH)
    def _():
        _attn_sample_kernel(lam_init, q_ref, ks_ref, vs_ref, ck_ref, cv_ref, lam_ref, o_ref)


def _diff_attention(q, k, v, cache_k, cache_v, lam_p, lam_init):
    blk = lambda rows, f: pl.BlockSpec((rows, QK_B), f)
    dec_b = lambda i: jnp.maximum(i - BATCH, 0) // NQB_DEC
    own = lambda i: (i, 0)
    prompt_kv = lambda i: (jnp.minimum(i, BATCH - 1), 0)
    dec_kv = lambda i: (TP // DEC_SEQ + dec_b(i), 0)
    return pl.pallas_call(
        functools.partial(_attn_kernel, lam_init),
        grid=(BATCH + DEC_BATCH * NQB_DEC,),
        in_specs=[
            blk(QB, own), blk(SEQ, prompt_kv), blk(SEQ, prompt_kv), blk(DEC_SEQ, dec_kv), blk(DEC_SEQ, dec_kv),
            pl.BlockSpec((1, 1, H_B, 2, PAST_LEN, DQK_B), lambda i: (dec_b(i), 0, 0, 0, 0, 0)),
            pl.BlockSpec((1, 1, H_B, PAST_LEN, DV_B), lambda i: (dec_b(i), 0, 0, 0, 0)),
            pl.BlockSpec((4, DQK_B), lambda i: (0, 0)),
        ],
        out_specs=blk(QB, own),
        out_shape=jax.ShapeDtypeStruct((T, V_B), F32),
        compiler_params=_params(("arbitrary",)),
        name="diff_attention",
    )(q, k, v, k, v, cache_k, cache_v, lam_p)


def _head_rms(x, g, nheads, width):
    return jnp.concatenate([_rms(x[:, h * width:(h + 1) * width], g) for h in range(nheads)], axis=1)


def _mix_out_kernel(lam_init, of_ref, ob_ref, r_ref, oatt_ref, xp_ref, xs_ref, mod_ref,
                    gg_ref, dg_ref, wo_ref, gp_ref, gffn_ref, o_ref, hp_ref):
    m = mod_ref[0, 0]
    o_a = _head_rms(of_ref[...] + ob_ref[...], gg_ref[...], H_A, DV_A) * _silu(r_ref[...])
    o_b = _head_rms(oatt_ref[...], dg_ref[...], H_B, DV_B) * (1.0 - lam_init)
    out = _bdot(o_a, wo_ref[0:V_A, :]) + _bdot(o_b, wo_ref[V_A:V_A + V_B, :])
    x1 = _stream_tile(xp_ref, xs_ref) + m[2:3] * _rms(out, gp_ref[...])
    o_ref[...] = x1
    _store_token_tiles(hp_ref, _ffn_input_rows(x1, m, gffn_ref[...]))


def _mix_out(lam_init, o_f, o_b, r_a, o_att, xp, xs, mod, l, gla_g, diff_g, w_o, g_post, g_ffn):
    tok = lambda width: pl.BlockSpec((TM, width), lambda i: (i, 0))
    vec = lambda width: pl.BlockSpec((1, width), lambda i: (0, 0))
    return pl.pallas_call(
        functools.partial(_mix_out_kernel, lam_init),
        grid=(T // TM,),
        in_specs=[
            tok(V_A), tok(V_A), tok(V_A), tok(V_B), *_stream_specs(),
            pl.BlockSpec((1, 1, 6, D), lambda i: (l, _mod_row(i), 0, 0)),
            vec(DV_A), vec(DV_B),
            pl.BlockSpec((V_A + V_B, D), lambda i: (0, 0)),
            vec(D), vec(D),
        ],
        out_specs=[tok(D), pl.BlockSpec((TM * ROWS_PER_TOKEN, LANES), lambda i: (i, 0))],
        out_shape=[jax.ShapeDtypeStruct((T, D), F32), jax.ShapeDtypeStruct((T * ROWS_PER_TOKEN, LANES), U32)],
        compiler_params=_params(("arbitrary",)),
        name="mixer_ab_out",
    )(o_f, o_b, r_a, o_att, xp, xs, mod, gla_g, diff_g, w_o, g_post, g_ffn)


def _gelu_tanh(x):
    return 0.5 * x * (1.0 + jnp.tanh(math.sqrt(2.0 / math.pi) * (x + 0.044715 * (x * x * x))))


def _sgu_kernel(xp_ref, xs_ref, mod_ref, gpre_ref, win_ref, bin_ref, vg_ref, ws_ref, bs_ref,
                wout_ref, gpost_ref, gffn_ref, o_ref, hp_ref, t_ref):
    m = mod_ref[0, 0]
    x = _stream_tile(xp_ref, xs_ref)
    h = _rms(x, gpre_ref[...]) * (1.0 + m[1:2]) + m[0:1]
    z = _gelu_tanh(_bdot(h, win_ref[...]) + bin_ref[...])
    v = _rms(z[:, SGU_DIM:], vg_ref[...])
    gw = SGU_DIM // SGU_GROUPS
    for ch in range(TM // SGU_CHUNK):
        rows = slice(ch * SGU_CHUNK, (ch + 1) * SGU_CHUNK)
        for g in range(SGU_GROUPS):
            cols = slice(g * gw, (g + 1) * gw)
            vs = _bdot(ws_ref[g], v[rows, cols]) + bs_ref[:, g:g + 1]
            t_ref[rows, cols] = (z[rows, cols] * vs).astype(BF16)
    out = jnp.dot(t_ref[...], wout_ref[...], preferred_element_type=F32)
    x1 = x + m[2:3] * _rms(out, gpost_ref[...])
    o_ref[...] = x1
    _store_token_tiles(hp_ref, _ffn_input_rows(x1, m, gffn_ref[...]))


def _sgu(xp, xs, mod, l, g_pre, w_in, b_in, v_g, w_s, b_s_t, w_out, g_post, g_ffn):
    tok = pl.BlockSpec((TM, D), lambda i: (i, 0))
    full = lambda *shape: pl.BlockSpec(shape, lambda i: (0,) * len(shape))
    return pl.pallas_call(
        _sgu_kernel,
        grid=(T // TM,),
        in_specs=_stream_specs() + [
            pl.BlockSpec((1, 1, 6, D), lambda i: (l, _mod_row(i), 0, 0)),
            full(1, D), full(D, 2 * SGU_DIM), full(1, 2 * SGU_DIM), full(1, SGU_DIM),
            full(SGU_GROUPS, SGU_CHUNK, SGU_CHUNK), full(SGU_CHUNK, SGU_GROUPS),
            full(SGU_DIM, D), full(1, D), full(1, D),
        ],
        out_specs=[tok, pl.BlockSpec((TM * ROWS_PER_TOKEN, LANES), lambda i: (i, 0))],
        out_shape=[jax.ShapeDtypeStruct((T, D), F32), jax.ShapeDtypeStruct((T * ROWS_PER_TOKEN, LANES), U32)],
        scratch_shapes=[pltpu.VMEM((TM, SGU_DIM), BF16)],
        compiler_params=_params(("arbitrary",)),
        name="sgu_mixer",
    )(xp, xs, mod, g_pre, w_in, b_in, v_g, w_s, b_s_t, w_out, g_post, g_ffn)


LANES = 128
U32 = jnp.uint32
PACKED = D // 2
ROWS_PER_TOKEN = PACKED // LANES
BF16_BITS = 16
HIGH_HALF = 0xFFFF0000


def _pack_rows(x):
    bits = lax.bitcast_convert_type(x.astype(BF16).astype(F32), U32)
    return bits[:, :PACKED] | (bits[:, PACKED:] >> BF16_BITS)


def _unpack_rows(u):
    return (lax.bitcast_convert_type(u & U32(HIGH_HALF), F32), lax.bitcast_convert_type(u << BF16_BITS, F32))


def _store_token_tiles(ref, u, first=0):
    n = u.shape[0]
    for c in range(ROWS_PER_TOKEN):
        ref[pl.ds(first * ROWS_PER_TOKEN + c, n, stride=ROWS_PER_TOKEN), :] = u[:, c * LANES:(c + 1) * LANES]


def _load_token_tiles(ref, n, first=0):
    return jnp.concatenate([ref[pl.ds(first * ROWS_PER_TOKEN + c, n, stride=ROWS_PER_TOKEN), :]
                            for c in range(ROWS_PER_TOKEN)], axis=1)


def _ffn_input_rows(x, m, g):
    return _pack_rows(_rms(x, g) * (1.0 + m[4:5]) + m[3:4])


def _router_kernel(hp_ref, wr_ref, eb_ref, te_ref, wn_ref, rk_ref, cnt_ref, carry_ref, upper_ref):
    i = pl.program_id(0)

    @pl.when(i == 0)
    def _():
        carry_ref[...] = jnp.zeros_like(carry_ref)
        tj = lax.broadcasted_iota(I32, (TR, TR), 0)
        ti = lax.broadcasted_iota(I32, (TR, TR), 1)
        upper_ref[...] = jnp.where(tj < ti, 1.0, 0.0).astype(BF16)

    h_hi, h_lo = (t.astype(BF16) for t in _unpack_rows(_load_token_tiles(hp_ref, TR)))
    w1, w2, _ = _split3(wr_ref[...])
    nt = lambda a, b: lax.dot_general(a, b, (((1,), (1,)), ((), ())), preferred_element_type=F32)
    logits = (nt(w1[:, :PACKED], h_hi) + nt(w1[:, PACKED:], h_lo)
              + nt(w2[:, :PACKED], h_hi) + nt(w2[:, PACKED:], h_lo))
    scores = jax.nn.sigmoid(logits)
    sel = scores + eb_ref[...]

    row8 = lax.broadcasted_iota(I32, (GROUP_SIZE, TR), 0)
    gscore = []
    for g in range(N_GROUPS):
        xg = sel[g * GROUP_SIZE:(g + 1) * GROUP_SIZE]
        m1 = jnp.max(xg, axis=0, keepdims=True)
        i1 = jnp.min(jnp.where(xg == m1, row8, GROUP_SIZE), axis=0, keepdims=True)
        m2 = jnp.max(jnp.where(row8 == i1, NEG_INF, xg), axis=0, keepdims=True)
        gscore.append(m1 + m2)
    pieces = []
    for g in range(N_GROUPS):
        rank = jnp.zeros((1, TR), I32)
        for g2 in range(N_GROUPS):
            if g2 == g:
                continue
            beats = (gscore[g2] >= gscore[g]) if g2 < g else (gscore[g2] > gscore[g])
            rank = rank + beats.astype(I32)
        pieces.append(jnp.where(rank < TOPK_GROUPS, sel[g * GROUP_SIZE:(g + 1) * GROUP_SIZE], NEG_INF))
    cur = jnp.concatenate(pieces, axis=0)

    row = lax.broadcasted_iota(I32, (N_EXPERTS, TR), 0)
    idxs, ws = [], []
    for _ in range(TOP_K):
        mx = jnp.max(cur, axis=0, keepdims=True)
        idx = jnp.min(jnp.where(cur == mx, row, N_EXPERTS), axis=0, keepdims=True)
        hit = row == idx
        ws.append(jnp.sum(jnp.where(hit, scores, 0.0), axis=0, keepdims=True))
        cur = jnp.where(hit, NEG_INF, cur)
        idxs.append(idx)
    mask = jnp.zeros((N_EXPERTS, TR), F32)
    for idx in idxs:
        mask = mask + (row == idx).astype(F32)
    wsum = ws[0]
    for wk in ws[1:]:
        wsum = wsum + wk

    pos = carry_ref[...] + jnp.dot(mask.astype(BF16), upper_ref[...], preferred_element_type=F32)
    for k in range(TOP_K):
        hit = row == idxs[k]
        te_ref[k:k + 1, :] = idxs[k]
        wn_ref[k:k + 1, :] = ws[k] / wsum * ROUTED_SCALE
        rk_ref[k:k + 1, :] = jnp.sum(jnp.where(hit, pos, 0.0), axis=0, keepdims=True).astype(I32)
    carry_ref[...] = carry_ref[...] + jnp.sum(mask, axis=1, keepdims=True)
    cnt_ref[...] = carry_ref[...]


def _router(hp, wr_t, e_bias):
    kt = lambda dtype: jax.ShapeDtypeStruct((TOP_K, T), dtype)
    kt_spec = pl.BlockSpec((TOP_K, TR), lambda i: (0, i))
    return pl.pallas_call(
        _router_kernel,
        grid=(T // TR,),
        in_specs=[
            pl.BlockSpec((TR * ROWS_PER_TOKEN, LANES), lambda i: (i, 0)),
            pl.BlockSpec((N_EXPERTS, D), lambda i: (0, 0)),
            pl.BlockSpec((N_EXPERTS, 1), lambda i: (0, 0)),
        ],
        out_specs=[
            kt_spec, kt_spec, kt_spec,
            pl.BlockSpec((N_EXPERTS, 1), lambda i: (0, 0)),
        ],
        out_shape=[
            kt(I32), kt(F32), kt(I32),
            jax.ShapeDtypeStruct((N_EXPERTS, 1), F32),
        ],
        scratch_shapes=[pltpu.VMEM((N_EXPERTS, 1), F32), pltpu.VMEM((TR, TR), BF16)],
        compiler_params=_params(("arbitrary",)),
        name="moe_router",
    )(hp, wr_t, e_bias)


_PAD_BITS = tuple(1 << b for b in range(GM.bit_length() - 1))


def _pad_fill_kernel(pad_start_ref, pad_len_ref, xg_in_ref, xg_ref, zero_ref, sem):
    del xg_in_ref
    zero_ref[...] = jnp.zeros_like(zero_ref)

    def pad_copies(e):
        start = pad_start_ref[e]
        n = pad_len_ref[e]
        copies = []
        for bit in _PAD_BITS:
            first = start + (n & ~(2 * bit - 1))
            copies.append(((n & bit) != 0, pltpu.make_async_copy(
                zero_ref.at[pl.ds(0, bit)], xg_ref.at[pl.ds(first, bit)], sem)))
        return copies

    def start_e(e, carry):
        for on, cp in pad_copies(e):
            @pl.when(on)
            def _():
                cp.start()
        return carry

    def wait_e(e, carry):
        for on, cp in pad_copies(e):
            @pl.when(on)
            def _():
                cp.wait()
        return carry

    lax.fori_loop(0, N_EXPERTS, start_e, 0)
    lax.fori_loop(0, N_EXPERTS, wait_e, 0)


def _pad_fill(pad_start, pad_len, xg):
    grid_spec = pltpu.PrefetchScalarGridSpec(
        num_scalar_prefetch=2,
        grid=(1,),
        in_specs=[pl.BlockSpec(memory_space=pl.ANY)],
        out_specs=pl.BlockSpec(memory_space=pl.ANY),
        scratch_shapes=[pltpu.VMEM((GM // 2, ROWS_PER_TOKEN, LANES), xg.dtype), pltpu.SemaphoreType.DMA],
    )
    return pl.pallas_call(
        _pad_fill_kernel,
        grid_spec=grid_spec,
        out_shape=jax.ShapeDtypeStruct(xg.shape, xg.dtype),
        input_output_aliases={2: 0},
        compiler_params=_params(("arbitrary",)),
        name="moe_pad_fill",
    )(pad_start, pad_len, xg)


SC_CORES, SC_SUBCORES = 2, 16
SC_WORKERS = SC_CORES * SC_SUBCORES
SC_W = 64


def _sc_worker_id():
    return lax.axis_index("s") * SC_CORES + lax.axis_index("c")


def _sc_dispatch(h3, slot3):
    nchunk = T // SC_WORKERS // SC_W
    mesh = plsc.VectorSubcoreMesh(core_axis_name="c", subcore_axis_name="s")
    tile = (SC_W, ROWS_PER_TOKEN, LANES)

    @functools.partial(
        pl.kernel, mesh=mesh,
        out_type=jax.ShapeDtypeStruct((SP, ROWS_PER_TOKEN, LANES), h3.dtype),
        scratch_types=[pltpu.VMEM((TOP_K, SC_W), I32), pltpu.VMEM((TOP_K, SC_W), I32),
                       pltpu.VMEM(tile, h3.dtype), pltpu.VMEM(tile, h3.dtype),
                       pltpu.SemaphoreType.DMA((2,)), pltpu.SemaphoreType.DMA((2,))],
    )
    def k(h_hbm, slot_hbm, xg_hbm, idx0, idx1, rows0, rows1, lsem, ssem):
        first = _sc_worker_id() * nchunk
        idx = (idx0, idx1)
        rows = (rows0, rows1)

        def loads(j, b):
            blk = first + j
            tok = pl.multiple_of(blk * SC_W, SC_W)
            return (pltpu.make_async_copy(slot_hbm.at[blk], idx[b], lsem.at[b]),
                    pltpu.make_async_copy(h_hbm.at[pl.ds(tok, SC_W)], rows[b], lsem.at[b]))

        def scatters(b):
            return [pltpu.make_async_copy(rows[b], xg_hbm.at[idx[b].at[kk]], ssem.at[b]) for kk in range(TOP_K)]

        for cp in loads(0, 0):
            cp.start()

        @pl.loop(0, nchunk, step=2)
        def _(j):
            for b in (0, 1):
                jj = j + b
                for cp in loads(jj, b):
                    cp.wait()
                for cp in scatters(b):
                    cp.start()

                @pl.when(jj + 1 < nchunk)
                def _():
                    @pl.when(jj >= 1)
                    def _():
                        for cp in scatters(1 - b):
                            cp.wait()
                    for cp in loads(jj + 1, 1 - b):
                        cp.start()

        for b in (0, 1):
            for cp in scatters(b):
                cp.wait()

    return k(h3, slot3)


def _gmm_kernel(layer, tile_e_ref, tile_blk_ref, tile_nsub_ref, tile_run_ref, tile_next_ref,
                x_ref, wg_hbm, wu_hbm, wd_hbm, y_ref, wg_st, wu_st, wd_st, wgu_scr, wd_scr, sems):
    j = pl.program_id(0)
    nsub = tile_nsub_ref[j]
    run = tile_run_ref[j]

    def weight_copies(e, slot):
        return [pltpu.make_async_copy(src.at[layer, e], dst.at[slot], sems.at[slot])
                for src, dst in ((wg_hbm, wg_st), (wu_hbm, wu_st), (wd_hbm, wd_st))]

    @pl.when(run >= 0)
    def _():
        @pl.when(j == 0)
        def _():
            for cp in weight_copies(tile_e_ref[j], run):
                cp.start()

        for cp in weight_copies(tile_e_ref[j], run):
            cp.wait()

        nxt = tile_next_ref[j]

        @pl.when(nxt >= 0)
        def _():
            for cp in weight_copies(nxt, 1 - run):
                cp.start()

        wgu_scr[:, 0:D_EXPERT] = wg_st[run].astype(BF16)
        wgu_scr[:, D_EXPERT:2 * D_EXPERT] = wu_st[run].astype(BF16)
        wd_scr[...] = wd_st[run].astype(BF16)

    def expert_mlp(s):
        x_hi, x_lo = _unpack_rows(_load_token_tiles(x_ref, GM_SUB, s * GM_SUB))
        gu = (jnp.dot(x_hi.astype(BF16), wgu_scr[0:PACKED, :], preferred_element_type=F32)
              + jnp.dot(x_lo.astype(BF16), wgu_scr[PACKED:D, :], preferred_element_type=F32))
        hid = _silu(gu[:, 0:D_EXPERT]) * gu[:, D_EXPERT:2 * D_EXPERT]
        y = jnp.dot(hid.astype(BF16), wd_scr[...], preferred_element_type=F32)
        _store_token_tiles(y_ref, _pack_rows(y), s * GM_SUB)

    for n in range(1, GM // GM_SUB + 1):
        @pl.when(nsub == n)
        def _():
            for s in range(n):
                expert_mlp(s)
            for s in range(n, GM // GM_SUB):
                _store_token_tiles(y_ref, jnp.zeros((GM_SUB, PACKED), U32), s * GM_SUB)


def _gmm(tile_e, tile_blk, tile_nsub, tile_run, tile_next, xg, l, w_gate, w_up, w_down):
    row_tile = pl.BlockSpec((GM * ROWS_PER_TOKEN, LANES), lambda j, te, tb, *_: (tb[j], 0))
    hbm = pl.BlockSpec(memory_space=pl.ANY)
    grid_spec = pltpu.PrefetchScalarGridSpec(
        num_scalar_prefetch=5,
        grid=(NT_MAX,),
        in_specs=[row_tile, hbm, hbm, hbm],
        out_specs=row_tile,
        scratch_shapes=[pltpu.VMEM((2, D, D_EXPERT), F32), pltpu.VMEM((2, D, D_EXPERT), F32),
                        pltpu.VMEM((2, D_EXPERT, D), F32),
                        pltpu.VMEM((D, 2 * D_EXPERT), BF16), pltpu.VMEM((D_EXPERT, D), BF16),
                        pltpu.SemaphoreType.DMA((2,))],
    )
    return pl.pallas_call(
        functools.partial(_gmm_kernel, l),
        grid_spec=grid_spec,
        out_shape=jax.ShapeDtypeStruct((SP * ROWS_PER_TOKEN, LANES), U32),
        compiler_params=_params(("arbitrary",)),
        name="moe_grouped_matmul",
    )(tile_e, tile_blk, tile_nsub, tile_run, tile_next, xg, w_gate, w_up, w_down)


def _sc_gather(table3, idx):
    n_idx = idx.shape[0]
    per_w = n_idx // SC_WORKERS
    nchunk = per_w // SC_W
    mesh = plsc.VectorSubcoreMesh(core_axis_name="c", subcore_axis_name="s")
    tile = (SC_W, ROWS_PER_TOKEN, LANES)

    @functools.partial(
        pl.kernel, mesh=mesh,
        out_type=jax.ShapeDtypeStruct((n_idx, ROWS_PER_TOKEN, LANES), table3.dtype),
        scratch_types=[pltpu.VMEM((per_w,), I32), pltpu.VMEM(tile, table3.dtype), pltpu.VMEM(tile, table3.dtype),
                       pltpu.SemaphoreType.DMA((2,)), pltpu.SemaphoreType.DMA((2,))],
    )
    def k(table_hbm, idx_hbm, out_hbm, idx_v, rows0, rows1, gsem, wsem):
        base = pl.multiple_of(_sc_worker_id() * per_w, per_w)
        rows = (rows0, rows1)
        pltpu.sync_copy(idx_hbm.at[pl.ds(base, per_w)], idx_v)

        def gather(j, b):
            ids = idx_v.at[pl.ds(pl.multiple_of(j * SC_W, SC_W), SC_W)]
            return pltpu.make_async_copy(table_hbm.at[ids], rows[b], gsem.at[b])

        def write(j, b):
            dst = out_hbm.at[pl.ds(pl.multiple_of(base + j * SC_W, SC_W), SC_W)]
            return pltpu.make_async_copy(rows[b], dst, wsem.at[b])

        gather(0, 0).start()

        @pl.loop(0, nchunk, step=2)
        def _(j):
            for b in (0, 1):
                jj = j + b
                gather(jj, b).wait()
                write(jj, b).start()

                @pl.when(jj + 1 < nchunk)
                def _():
                    @pl.when(jj >= 1)
                    def _():
                        write(jj - 1, 1 - b).wait()
                    gather(jj + 1, 1 - b).start()

        write(nchunk - 2, 0).wait()
        write(nchunk - 1, 1).wait()

    return k(table3, idx)


def _combine_kernel(wn_ref, x_ref, mod_ref, gpre_ref, gp_ref, wsg_ref, wsu_ref, wsd_ref, y_ref, o_ref, eye_ref):
    @pl.when(pl.program_id(0) == 0)
    def _():
        r = lax.broadcasted_iota(I32, (TD, TD), 0)
        c = lax.broadcasted_iota(I32, (TD, TD), 1)
        eye_ref[...] = jnp.where(r == c, 1.0, 0.0).astype(BF16)

    m = mod_ref[0, 0]
    x = x_ref[...]
    hb = (_rms(x, gpre_ref[...]) * (1.0 + m[4:5]) + m[3:4]).astype(BF16)
    hid = (_silu(jnp.dot(hb, wsg_ref[...], preferred_element_type=F32))
           * jnp.dot(hb, wsu_ref[...], preferred_element_type=F32))
    acc = jnp.dot(hid.astype(BF16), wsd_ref[...], preferred_element_type=F32)

    eye = eye_ref[...]
    nt = lambda a, b: lax.dot_general(a, b, (((1,), (1,)), ((), ())), preferred_element_type=F32)
    w1, w2, w3 = _split3(wn_ref[...])
    w_t = nt(eye, w1) + nt(eye, w2) + nt(eye, w3)

    acc_hi = acc[:, :PACKED]
    acc_lo = acc[:, PACKED:]
    for k in range(TOP_K):
        y_hi, y_lo = _unpack_rows(_load_token_tiles(y_ref, TD, k * TD))
        acc_hi = acc_hi + y_hi * w_t[:, k:k + 1]
        acc_lo = acc_lo + y_lo * w_t[:, k:k + 1]
    acc = jnp.concatenate([acc_hi, acc_lo], axis=1)
    o_ref[...] = x + m[5:6] * _rms(acc, gp_ref[...])


def _combine(wn, x, mod, l, g_pre, g_post, ws_gate, ws_up, ws_down, ybuf, first_tok, n_tok):
    off = first_tok // TD
    tiles_per_dec = DEC_SEQ // TD
    npd = TP // TD
    mod_row = lambda i: jnp.where(i + off < npd, 0, 1 + (i + off - npd) // tiles_per_dec)
    full = lambda *shape: pl.BlockSpec(shape, lambda i: (0,) * len(shape))
    y_spec = pl.BlockSpec((TOP_K * TD * ROWS_PER_TOKEN, LANES), lambda i: (i, 0))
    return pl.pallas_call(
        _combine_kernel,
        grid=(n_tok // TD,),
        in_specs=[
            pl.BlockSpec((TOP_K, TD), lambda i: (0, i + off)),
            pl.BlockSpec((TD, D), lambda i: (i + off, 0)),
            pl.BlockSpec((1, 1, 6, D), lambda i: (l, mod_row(i), 0, 0)),
            full(1, D), full(1, D), full(D, D_SHARED), full(D, D_SHARED), full(D_SHARED, D),
            y_spec,
        ],
        out_specs=pl.BlockSpec((TD, D), lambda i: (i, 0)),
        out_shape=jax.ShapeDtypeStruct((n_tok, D), F32),
        scratch_shapes=[pltpu.VMEM((TD, TD), BF16)],
        compiler_params=_params(("arbitrary",)),
        name="moe_combine",
    )(wn, x, mod, g_pre, g_post, ws_gate, ws_up, ws_down, ybuf)


def _moe_layer(x, h, mod, l, g_pre, g_post, w_router, e_bias, w_gate, w_up, w_down,
               ws_gate, ws_up, ws_down):
    top_e, wn, rk, cnt = _router(h, w_router.T, e_bias.reshape(N_EXPERTS, 1))
    cnt = cnt.reshape(N_EXPERTS).astype(I32)
    padded = (cnt + GM - 1) // GM * GM
    ends = jnp.cumsum(padded)
    offs = ends - padded
    eid = jnp.arange(N_EXPERTS, dtype=I32)[:, None, None]
    slot = rk + jnp.sum(jnp.where(top_e[None] == eid, offs[:, None, None], 0), axis=0)
    nvalid = ends[-1] // GM
    tile_start = jnp.arange(NT_MAX, dtype=I32) * GM
    tile_raw = jnp.sum((tile_start[:, None] >= ends[None, :]).astype(I32), axis=1)
    last = jnp.maximum(nvalid - 1, 0)
    tile_blk = jnp.minimum(jnp.arange(NT_MAX, dtype=I32), last)
    tile_e = jnp.minimum(tile_raw, N_EXPERTS - 1)
    tile_e = jnp.where(jnp.arange(NT_MAX) <= last, tile_e, tile_e[last])
    own = tile_e[:, None] == jnp.arange(N_EXPERTS, dtype=I32)[None, :]
    tile_rows = jnp.sum(jnp.where(own, (offs + cnt)[None, :], 0), axis=1) - tile_start
    tile_nsub = jnp.where(jnp.arange(NT_MAX) <= last, (jnp.clip(tile_rows, 0, GM) + GM_SUB - 1) // GM_SUB, 0)
    starts = jnp.logical_and(jnp.arange(NT_MAX) <= last,
                             jnp.concatenate([jnp.ones((1,), bool), tile_e[1:] != tile_e[:-1]]))
    tile_run = jnp.where(starts, (jnp.cumsum(starts.astype(I32)) - 1) % 2, -1)
    later = jnp.logical_and(jnp.arange(N_EXPERTS, dtype=I32)[None, :] > tile_e[:, None], (cnt > 0)[None, :])
    tile_next = jnp.min(jnp.where(later, jnp.arange(N_EXPERTS, dtype=I32)[None, :], N_EXPERTS), axis=1)
    tile_next = jnp.where(tile_next < N_EXPERTS, tile_next, -1)
    slot3 = slot.reshape(TOP_K, T // SC_W, SC_W).transpose(1, 0, 2)
    xg = _sc_dispatch(h.reshape(T, ROWS_PER_TOKEN, LANES), slot3)
    xg = _pad_fill(offs + cnt, padded - cnt, xg).reshape(SP * ROWS_PER_TOKEN, LANES)
    yg = _gmm(tile_e, tile_blk, tile_nsub.astype(I32), tile_run.astype(I32), tile_next.astype(I32),
              xg, l, w_gate, w_up, w_down)
    yg3 = yg.reshape(SP, ROWS_PER_TOKEN, LANES)
    ws = (ws_gate.astype(BF16), ws_up.astype(BF16), ws_down.astype(BF16))
    outs = []
    for first_tok, n_tok in ((0, TP), (TP, TS)):
        ids = slot[:, first_tok:first_tok + n_tok].reshape(TOP_K, n_tok // TD, TD).transpose(1, 0, 2)
        ybuf = _sc_gather(yg3, ids.reshape(TOP_K * n_tok))
        outs.append(_combine(wn, x, mod, l, g_pre, g_post, *ws,
                             ybuf.reshape(TOP_K * n_tok * ROWS_PER_TOKEN, LANES), first_tok, n_tok))
    return outs


def _rope_tables():
    n = DEC_SEQ
    rows = n // GRID_W
    row = jnp.repeat(jnp.arange(rows), GRID_W).astype(F32)
    col = jnp.tile(jnp.arange(GRID_W), rows).astype(F32)
    half = DQK_B // 2
    inv = ROPE_BASE ** (-jnp.arange(0, half, 2, dtype=F32) / half)
    ang_r = row[:, None] * inv
    ang_c = col[:, None] * inv
    ang = jnp.concatenate([ang_r, ang_r, ang_c, ang_c], axis=-1)
    reps = QK_B // DQK_B
    return jnp.tile(jnp.cos(ang), (1, reps)), jnp.tile(jnp.sin(ang), (1, reps))


def _pad_in_proj(w):
    s = [0, Q_A, 2 * Q_A, 2 * Q_A + V_A, 2 * Q_A + 2 * V_A]
    s += [s[-1] + GATE_RANK, s[-1] + 2 * GATE_RANK]
    s += [s[-1] + QK_B, s[-1] + 2 * QK_B, s[-1] + 2 * QK_B + V_B]
    gates = jnp.pad(w[:, s[4]:s[6]], ((0, 0), (0, GL_PAD - 2 * GATE_RANK)))
    return jnp.concatenate([w[:, s[0]:s[4]], gates, w[:, s[6]:s[9]]], axis=1).astype(BF16)


def kernel(x_prompt, x_sample, c, c_ctx, state_gla, cache_k, cache_v, ada_w, ada_b, norm_pre_mix, norm_post_mix, norm_pre_ffn, norm_post_ffn, ab_w_in, gla_w_g2, gla_b_g2, gla_norm_g, diff_lambda, diff_norm_g, ab_w_out, sgu_w_in, sgu_b_in, sgu_norm_g, sgu_w_s, sgu_b_s, sgu_w_out, moe_w_router, moe_e_bias, moe_w_gate, moe_w_up, moe_w_down, moe_ws_gate, moe_ws_up, moe_ws_down):
    depth = ada_w.shape[0]
    assert x_prompt.shape == (BATCH, SEQ, D) and x_sample.shape == (DEC_BATCH, DEC_SEQ, D)
    assert state_gla.shape == (DEC_BATCH, (depth + 1) // 2, 2, H_A, DK_A, DV_A)
    assert cache_k.shape == (DEC_BATCH, 1, H_B, 2, PAST_LEN, DQK_B) and cache_v.shape == (DEC_BATCH, 1, H_B, PAST_LEN, DV_B)
    assert depth == 2 and moe_w_gate.shape == (depth, N_EXPERTS, D, D_EXPERT)
    xp, xs = x_prompt.reshape(TP, D), x_sample.reshape(TS, D)
    cond = jnp.concatenate([c_ctx[None, :], c, jnp.zeros((8 - 1 - DEC_BATCH, D), F32)], axis=0)
    mod = _modulation(cond, ada_w, ada_b)
    cos, sin = _rope_tables()
    vec = lambda a: a.reshape(1, -1)
    new_s = new_k = new_v = None
    for l in range(depth):
        if l % 2 == 0:
            e = l // 2
            lam_init = 0.8 - 0.6 * math.exp(-0.3 * l)
            a, r_a, gl, q_b, k_b, v_b, new_k, new_v = _in_proj(xp, xs, mod, l, vec(norm_pre_mix[l]),
                                                               _pad_in_proj(ab_w_in[e]), cos, sin)
            s0_t = jnp.swapaxes(state_gla[:, e], -1, -2)
            same_head = jnp.eye(H_A, dtype=bool)[None, None, :, None, :, None]
            s0_t = jnp.where(same_head, s0_t[:, :, :, :, None, :], 0.0).reshape(DEC_BATCH, 2, V_A, Q_A)
            o_f, o_bw, s_fin_t = _gla(a, gl, gla_w_g2[e], gla_b_g2[e].reshape(2, 1, Q_A), s0_t)
            o_att = _diff_attention(q_b, k_b, v_b, cache_k, cache_v, diff_lambda[e], lam_init)
            x, h = _mix_out(lam_init, o_f, o_bw, r_a, o_att, xp, xs, mod, l, vec(gla_norm_g[e]), vec(diff_norm_g[e]),
                            ab_w_out[e].astype(BF16), vec(norm_post_mix[l]), vec(norm_pre_ffn[l]))
            new_s = jnp.swapaxes(s_fin_t, -1, -2)[:, None]
        else:
            o = l // 2
            x, h = _sgu(xp, xs, mod, l, vec(norm_pre_mix[l]), sgu_w_in[o].astype(BF16), vec(sgu_b_in[o]),
                        vec(sgu_norm_g[o]), sgu_w_s[o], sgu_b_s[o].T, sgu_w_out[o].astype(BF16),
                        vec(norm_post_mix[l]), vec(norm_pre_ffn[l]))
        xp, xs = _moe_layer(x, h, mod, l, vec(norm_pre_ffn[l]), vec(norm_post_ffn[l]), moe_w_router[l], moe_e_bias[l],
                            moe_w_gate, moe_w_up, moe_w_down, moe_ws_gate[l], moe_ws_up[l], moe_ws_down[l])
    y_prompt = xp.reshape(BATCH, SEQ, D)
    y_sample = xs.reshape(DEC_BATCH, DEC_SEQ, D)
    return (y_prompt, y_sample, new_s, new_k, new_v)
```

```python
import functools
import math

import jax
import jax.numpy as jnp
from jax import lax
from jax.experimental import pallas as pl
from jax.experimental.pallas import tpu as pltpu
from jax.experimental.pallas import tpu_sc as plsc

F32 = jnp.float32
BF16 = jnp.bfloat16
I32 = jnp.int32

D = 1024
BATCH, SEQ = 32, 256
DEC_BATCH, DEC_SEQ = 4, 2048
PAST_LEN = 256
GRID_W = 64
EPS = 1e-6
TP = BATCH * SEQ
TS = DEC_BATCH * DEC_SEQ
T = TP + TS
H_A, DK_A, DV_A = 4, 64, 128
Q_A, V_A = H_A * DK_A, H_A * DV_A
GATE_RANK, GATE_TAU, GLA_CHUNK = 16, 16.0, 64
H_B, DQK_B, DV_B = 4, 64, 128
QK_B, V_B = H_B * 2 * DQK_B, H_B * DV_B
ROPE_BASE = 10000.0
SGU_DIM, SGU_GROUPS, SGU_CHUNK = 1024, 4, 128
N_EXPERTS, TOP_K, N_GROUPS, TOPK_GROUPS = 64, 8, 8, 4
GROUP_SIZE = N_EXPERTS // N_GROUPS
D_EXPERT, D_SHARED = 256, 256
ROUTED_SCALE = 2.5

TM = 512
TM_SUB = 256
NPT = TP // TM
TILES_PER_DEC = DEC_SEQ // TM
SEG = 256
NSEG = T // SEG
NSEG_P = TP // SEG
SEG_PER_DEC = DEC_SEQ // SEG
TR = 512
TD = 512
COMBINE_PARTS = 2
GM = 1024
GM_SUB = 256
NT_MAX = T * TOP_K // GM + N_EXPERTS
SP = NT_MAX * GM
GL_PAD = 128
VMEM_LIMIT = 56 * 1024 * 1024
NEG_INF = float("-inf")


def _bdot(a, b):
    return jnp.dot(a.astype(BF16), b.astype(BF16), preferred_element_type=F32)


def _bdot_nt(a, b):
    return lax.dot_general(a.astype(BF16), b.astype(BF16), (((1,), (1,)), ((), ())),
                           preferred_element_type=F32)


def _bdot_tn(a, b):
    return lax.dot_general(a.astype(BF16), b.astype(BF16), (((0,), (0,)), ((), ())),
                           preferred_element_type=F32)


def _split3(x):
    x1 = x.astype(BF16)
    r1 = x - x1.astype(F32)
    x2 = r1.astype(BF16)
    x3 = (r1 - x2.astype(F32)).astype(BF16)
    return x1, x2, x3


def _rms(x, g):
    return x * lax.rsqrt(jnp.mean(x * x, axis=-1, keepdims=True) + EPS) * g


def _silu(x):
    return x * jax.nn.sigmoid(x)


def _mod_row(i):
    return jnp.where(i < NPT, 0, 1 + (i - NPT) // TILES_PER_DEC)


def _params(sem, limit=VMEM_LIMIT):
    return pltpu.CompilerParams(dimension_semantics=sem, vmem_limit_bytes=limit)


def _mod_kernel(c_ref, w_ref, b_ref, o_ref):
    o_ref[0] = _bdot(_silu(c_ref[...]), w_ref[0]) + b_ref[0]


def _modulation(cond, ada_w, ada_b):
    depth = ada_w.shape[0]
    nj = 6
    out = pl.pallas_call(
        _mod_kernel,
        grid=(depth, nj),
        in_specs=[
            pl.BlockSpec((8, D), lambda l, j: (0, 0)),
            pl.BlockSpec((1, D, D), lambda l, j: (l, 0, j)),
            pl.BlockSpec((1, 1, D), lambda l, j: (l, 0, j)),
        ],
        out_specs=pl.BlockSpec((1, 8, D), lambda l, j: (l, 0, j)),
        out_shape=jax.ShapeDtypeStruct((depth, 8, 6 * D), F32),
        compiler_params=_params(("arbitrary", "arbitrary")),
        name="adaln_modulation",
    )(cond, ada_w, ada_b.reshape(depth, 1, 6 * D))
    return out.reshape(depth, 8, 6, D)


_C_A = 0
_C_R = _C_A + 2 * Q_A + V_A
_C_GL = _C_R + V_A
_C_Q = _C_GL + GL_PAD
_C_K = _C_Q + QK_B
_C_V = _C_K + QK_B
_C_END = _C_V + V_B
ROT_PAIR = DQK_B // 4


def _rope(x, cos, sin):
    lane = lax.broadcasted_iota(I32, x.shape, 1)
    first = (lane % (2 * ROT_PAIR)) < ROT_PAIR
    n = x.shape[1]
    xr = jnp.where(first, -pltpu.roll(x, n - ROT_PAIR, 1), pltpu.roll(x, ROT_PAIR, 1))
    return x * cos + xr * sin


def _stream_specs():
    return [pl.BlockSpec((TM, D), lambda i: (jnp.minimum(i, NPT - 1), 0)),
            pl.BlockSpec((TM, D), lambda i: (jnp.maximum(i - NPT, 0), 0))]


def _stream_tile(xp_ref, xs_ref):
    return jnp.where(pl.program_id(0) < NPT, xp_ref[...], xs_ref[...])


def _in_kernel(xp_ref, xs_ref, mod_ref, g_ref, w_ref, cos_ref, sin_ref,
               a_ref, r_ref, gl_ref, q_ref, k_ref, v_ref, ck_ref, cv_ref):
    latent = pl.program_id(0) >= NPT
    m = mod_ref[0, 0]
    x = _stream_tile(xp_ref, xs_ref)
    for s in range(TM // TM_SUB):
        rows = slice(s * TM_SUB, (s + 1) * TM_SUB)
        h = _rms(x[rows], g_ref[...]) * (1.0 + m[1:2]) + m[0:1]
        hb = h.astype(BF16)

        def proj(c0, c1):
            return jnp.dot(hb, w_ref[:, c0:c1], preferred_element_type=F32)

        a_ref[rows, :] = proj(_C_A, _C_R)
        r_ref[rows, :] = proj(_C_R, _C_GL)
        gl_ref[rows, :] = proj(_C_GL, _C_Q)
        v_ref[rows, :] = proj(_C_V, _C_END)
        q = proj(_C_Q, _C_K)
        k = proj(_C_K, _C_V)
        cos = cos_ref[rows, :]
        sin = sin_ref[rows, :]
        q_ref[rows, :] = jnp.where(latent, _rope(q, cos, sin), q)
        k_ref[rows, :] = jnp.where(latent, _rope(k, cos, sin), k)

    @pl.when(jnp.logical_not(latent))
    def _():
        for s in range(TM // SEQ):
            rows = slice(s * SEQ, (s + 1) * SEQ)
            for h in range(H_B):
                cv_ref[s, 0, h] = v_ref[rows, h * DV_B:(h + 1) * DV_B]
                for mp in range(2):
                    ck_ref[s, 0, h, mp] = k_ref[rows, (2 * h + mp) * DQK_B:(2 * h + mp + 1) * DQK_B]


def _in_proj(xp, xs, mod, l, g, w_pad, cos, sin):
    tok = lambda width: pl.BlockSpec((TM, width), lambda i: (i, 0))
    rope_spec = pl.BlockSpec((TM, QK_B), lambda i: (jnp.maximum(i - NPT, 0) % TILES_PER_DEC, 0))
    widths = (2 * Q_A + V_A, V_A, GL_PAD, QK_B, QK_B, V_B)
    seqs = TM // SEQ
    prompt_tile = lambda i: jnp.minimum(i, NPT - 1)
    cache_specs = [pl.BlockSpec((seqs, 1, H_B, 2, SEQ, DQK_B), lambda i: (prompt_tile(i), 0, 0, 0, 0, 0)),
                   pl.BlockSpec((seqs, 1, H_B, SEQ, DV_B), lambda i: (prompt_tile(i), 0, 0, 0, 0))]
    cache_shapes = [jax.ShapeDtypeStruct((BATCH, 1, H_B, 2, SEQ, DQK_B), F32),
                    jax.ShapeDtypeStruct((BATCH, 1, H_B, SEQ, DV_B), F32)]
    return pl.pallas_call(
        _in_kernel,
        grid=(T // TM,),
        in_specs=_stream_specs() + [
            pl.BlockSpec((1, 1, 6, D), lambda i: (l, _mod_row(i), 0, 0)),
            pl.BlockSpec((1, D), lambda i: (0, 0)),
            pl.BlockSpec((D, _C_END), lambda i: (0, 0)),
            rope_spec, rope_spec,
        ],
        out_specs=[tok(w) for w in widths] + cache_specs,
        out_shape=[jax.ShapeDtypeStruct((T, w), F32) for w in widths] + cache_shapes,
        compiler_params=_params(("arbitrary",)),
        name="mixer_ab_in_proj",
    )(xp, xs, mod, g, w_pad, cos, sin)


def _log_sigmoid(x):
    return jnp.minimum(x, 0.0) - jnp.log(1.0 + jnp.exp(-jnp.abs(x)))


def _gla_kernel(af_ref, ab_ref, glf_ref, glb_ref, wg_ref, bg_ref, s0_ref,
                of_ref, ob_ref, sfin_ref, st_ref):
    i = pl.program_id(0)

    @pl.when(i < NSEG_P)
    def _():
        st_ref[...] = jnp.zeros_like(st_ref)

    @pl.when(jnp.logical_and(i >= NSEG_P, (i - NSEG_P) % SEG_PER_DEC == 0))
    def _():
        st_ref[...] = s0_ref[0]

    r = lax.broadcasted_iota(I32, (SEG, SEG), 0)
    c = lax.broadcasted_iota(I32, (SEG, SEG), 1)
    same = (r // GLA_CHUNK) == (c // GLA_CHUNK)
    nchunk = SEG // GLA_CHUNK
    own_head = (lax.broadcasted_iota(I32, (V_A, Q_A), 0) // DV_A) == (lax.broadcasted_iota(I32, (V_A, Q_A), 1) // DK_A)

    for d, (a_ref, gl_ref, o_ref) in enumerate(((af_ref, glf_ref, of_ref), (ab_ref, glb_ref, ob_ref))):
        fwd = d == 0
        gcol = gl_ref[:, d * GATE_RANK:(d + 1) * GATE_RANK]
        la = _log_sigmoid(_bdot(gcol, wg_ref[d]) + bg_ref[d]) / GATE_TAU
        causal = jnp.logical_and(same, (c <= r) if fwd else (c >= r))
        tri = jnp.where(causal, 1.0, 0.0).astype(BF16)
        l1, l2, l3 = _split3(la)
        b_all = (jnp.dot(tri, l1, preferred_element_type=F32)
                 + jnp.dot(tri, l2, preferred_element_type=F32)
                 + jnp.dot(tri, l3, preferred_element_type=F32))
        q_in_all = a_ref[:, 0:Q_A] * (DK_A ** -0.5) * jnp.exp(b_all)
        kd_all = a_ref[:, Q_A:2 * Q_A] * jnp.exp(-b_all)
        intra = []
        for h in range(H_A):
            kc = slice(h * DK_A, (h + 1) * DK_A)
            attn = jnp.where(causal, _bdot_nt(q_in_all[:, kc], kd_all[:, kc]), 0.0)
            intra.append(_bdot(attn, a_ref[:, 2 * Q_A + h * DV_A:2 * Q_A + (h + 1) * DV_A]))
        intra = jnp.concatenate(intra, axis=1)
        state = st_ref[d]
        order = range(nchunk) if fwd else range(nchunk - 1, -1, -1)
        for ch in order:
            r0 = ch * GLA_CHUNK
            rows = slice(r0, r0 + GLA_CHUNK)
            end = r0 + GLA_CHUNK - 1 if fwd else r0
            b_end = b_all[end:end + 1, :]
            kw = a_ref[rows, Q_A:2 * Q_A] * jnp.exp(b_end - b_all[rows, :])
            o_ref[rows, :] = intra[rows, :] + _bdot_nt(q_in_all[rows, :], state)
            kv_t = _bdot_tn(a_ref[rows, 2 * Q_A:2 * Q_A + V_A], kw)
            state = state * jnp.exp(b_end) + jnp.where(own_head, kv_t, 0.0)
        st_ref[d] = state

    @pl.when(i < NSEG_P)
    def _():
        for d in range(2):
            for h in range(H_A):
                sfin_ref[0, d, h] = st_ref[d, h * DV_A:(h + 1) * DV_A, h * DK_A:(h + 1) * DK_A]


def _seg_bwd(i):
    j = i - NSEG_P
    return jnp.where(i < NSEG_P, i, NSEG_P + (j // SEG_PER_DEC) * SEG_PER_DEC + (SEG_PER_DEC - 1 - j % SEG_PER_DEC))


def _gla(a, gl, wg, bg, s0_t):
    seg = lambda width, f: pl.BlockSpec((SEG, width), lambda i: (f(i), 0))
    ident = lambda i: i
    st_block = (1, 2, H_A, DV_A, DK_A)
    return pl.pallas_call(
        _gla_kernel,
        grid=(NSEG,),
        in_specs=[
            seg(D, ident), seg(D, _seg_bwd), seg(GL_PAD, ident), seg(GL_PAD, _seg_bwd),
            pl.BlockSpec((2, GATE_RANK, Q_A), lambda i: (0, 0, 0)),
            pl.BlockSpec((2, 1, Q_A), lambda i: (0, 0, 0)),
            pl.BlockSpec((1, 2, V_A, Q_A), lambda i: (jnp.maximum(i - NSEG_P, 0) // SEG_PER_DEC, 0, 0, 0)),
        ],
        out_specs=[
            seg(V_A, ident), seg(V_A, _seg_bwd),
            pl.BlockSpec(st_block, lambda i: (jnp.minimum(i, NSEG_P - 1), 0, 0, 0, 0)),
        ],
        out_shape=[
            jax.ShapeDtypeStruct((T, V_A), F32),
            jax.ShapeDtypeStruct((T, V_A), F32),
            jax.ShapeDtypeStruct((BATCH, 2, H_A, DV_A, DK_A), F32),
        ],
        scratch_shapes=[pltpu.VMEM((2, V_A, Q_A), F32)],
        compiler_params=_params(("arbitrary",)),
        name="gla_bidir",
    )(a, a, gl, gl, wg, bg, s0_t)


def _diff_lambda(lam_ref, lam_init):
    lp = lam_ref[...]
    s01 = jnp.sum(lp[0:1] * lp[1:2], axis=1, keepdims=True)
    s23 = jnp.sum(lp[2:3] * lp[3:4], axis=1, keepdims=True)
    return jnp.exp(s01) - jnp.exp(s23) + lam_init


def _attn_prompt_kernel(lam_init, q_ref, k_ref, v_ref, lam_ref, o_ref):
    lam = _diff_lambda(lam_ref, lam_init)
    for h in range(H_B):
        ps = []
        for m in range(2):
            cols = slice((2 * h + m) * DQK_B, (2 * h + m + 1) * DQK_B)
            s = _bdot_nt(q_ref[:, cols] * (DQK_B ** -0.5), k_ref[:, cols])
            e = jnp.exp(s - jnp.max(s, axis=1, keepdims=True))
            ps.append(e * (1.0 / jnp.sum(e, axis=1, keepdims=True)))
        w = ps[0] - lam * ps[1]
        o_ref[:, h * DV_B:(h + 1) * DV_B] = _bdot(w, v_ref[:, h * DV_B:(h + 1) * DV_B])


def _attn_sample_kernel(lam_init, q_ref, k_ref, v_ref, ck_ref, cv_ref, lam_ref, o_ref):
    lam = _diff_lambda(lam_ref, lam_init)
    for h in range(H_B):
        parts = []
        for m in range(2):
            cols = slice((2 * h + m) * DQK_B, (2 * h + m + 1) * DQK_B)
            q = q_ref[:, cols] * (DQK_B ** -0.5)
            sc = _bdot_nt(q, ck_ref[0, 0, h, m])
            sn = _bdot_nt(q, k_ref[:, cols])
            mx = jnp.maximum(jnp.max(sc, axis=1, keepdims=True), jnp.max(sn, axis=1, keepdims=True))
            ec = jnp.exp(sc - mx)
            en = jnp.exp(sn - mx)
            inv = (1.0 if m == 0 else -lam) / (jnp.sum(ec, axis=1, keepdims=True) + jnp.sum(en, axis=1, keepdims=True))
            parts.append((ec * inv, en * inv))
        wc = parts[0][0] + parts[1][0]
        wn = parts[0][1] + parts[1][1]
        o_ref[:, h * DV_B:(h + 1) * DV_B] = (_bdot(wc, cv_ref[0, 0, h])
                                             + _bdot(wn, v_ref[:, h * DV_B:(h + 1) * DV_B]))


QB = SEQ
NQB_DEC = DEC_SEQ // QB


def _attn_kernel(lam_init, q_ref, kp_ref, vp_ref, ks_ref, vs_ref, ck_ref, cv_ref, lam_ref, o_ref):
    i = pl.program_id(0)

    @pl.when(i < BATCH)
    def _():
        _attn_prompt_kernel(lam_init, q_ref, kp_ref, vp_ref, lam_ref, o_ref)

    @pl.when(i >= BATCH)
    def _():
        _attn_sample_kernel(lam_init, q_ref, ks_ref, vs_ref, ck_ref, cv_ref, lam_ref, o_ref)


def _diff_attention(q, k, v, cache_k, cache_v, lam_p, lam_init):
    blk = lambda rows, f: pl.BlockSpec((rows, QK_B), f)
    dec_b = lambda i: jnp.maximum(i - BATCH, 0) // NQB_DEC
    own = lambda i: (i, 0)
    prompt_kv = lambda i: (jnp.minimum(i, BATCH - 1), 0)
    dec_kv = lambda i: (TP // DEC_SEQ + dec_b(i), 0)
    return pl.pallas_call(
        functools.partial(_attn_kernel, lam_init),
        grid=(BATCH + DEC_BATCH * NQB_DEC,),
        in_specs=[
            blk(QB, own), blk(SEQ, prompt_kv), blk(SEQ, prompt_kv), blk(DEC_SEQ, dec_kv), blk(DEC_SEQ, dec_kv),
            pl.BlockSpec((1, 1, H_B, 2, PAST_LEN, DQK_B), lambda i: (dec_b(i), 0, 0, 0, 0, 0)),
            pl.BlockSpec((1, 1, H_B, PAST_LEN, DV_B), lambda i: (dec_b(i), 0, 0, 0, 0)),
            pl.BlockSpec((4, DQK_B), lambda i: (0, 0)),
        ],
        out_specs=blk(QB, own),
        out_shape=jax.ShapeDtypeStruct((T, V_B), F32),
        compiler_params=_params(("arbitrary",)),
        name="diff_attention",
    )(q, k, v, k, v, cache_k, cache_v, lam_p)


def _head_rms(x, g, nheads, width):
    return jnp.concatenate([_rms(x[:, h * width:(h + 1) * width], g) for h in range(nheads)], axis=1)


def _mix_out_kernel(lam_init, of_ref, ob_ref, r_ref, oatt_ref, xp_ref, xs_ref, mod_ref,
                    gg_ref, dg_ref, wo_ref, gp_ref, gffn_ref, o_ref, hp_ref):
    m = mod_ref[0, 0]
    o_a = _head_rms(of_ref[...] + ob_ref[...], gg_ref[...], H_A, DV_A) * _silu(r_ref[...])
    o_b = _head_rms(oatt_ref[...], dg_ref[...], H_B, DV_B) * (1.0 - lam_init)
    out = _bdot(o_a, wo_ref[0:V_A, :]) + _bdot(o_b, wo_ref[V_A:V_A + V_B, :])
    x1 = _stream_tile(xp_ref, xs_ref) + m[2:3] * _rms(out, gp_ref[...])
    o_ref[...] = x1
    _store_token_tiles(hp_ref, _ffn_input_rows(x1, m, gffn_ref[...]))


def _mix_out(lam_init, o_f, o_b, r_a, o_att, xp, xs, mod, l, gla_g, diff_g, w_o, g_post, g_ffn):
    tok = lambda width: pl.BlockSpec((TM, width), lambda i: (i, 0))
    vec = lambda width: pl.BlockSpec((1, width), lambda i: (0, 0))
    return pl.pallas_call(
        functools.partial(_mix_out_kernel, lam_init),
        grid=(T // TM,),
        in_specs=[
            tok(V_A), tok(V_A), tok(V_A), tok(V_B), *_stream_specs(),
            pl.BlockSpec((1, 1, 6, D), lambda i: (l, _mod_row(i), 0, 0)),
            vec(DV_A), vec(DV_B),
            pl.BlockSpec((V_A + V_B, D), lambda i: (0, 0)),
            vec(D), vec(D),
        ],
        out_specs=[tok(D), pl.BlockSpec((TM * ROWS_PER_TOKEN, LANES), lambda i: (i, 0))],
        out_shape=[jax.ShapeDtypeStruct((T, D), F32), jax.ShapeDtypeStruct((T * ROWS_PER_TOKEN, LANES), U32)],
        compiler_params=_params(("arbitrary",)),
        name="mixer_ab_out",
    )(o_f, o_b, r_a, o_att, xp, xs, mod, gla_g, diff_g, w_o, g_post, g_ffn)


def _gelu_tanh(x):
    return 0.5 * x * (1.0 + jnp.tanh(math.sqrt(2.0 / math.pi) * (x + 0.044715 * (x * x * x))))


def _sgu_kernel(xp_ref, xs_ref, mod_ref, gpre_ref, win_ref, bin_ref, vg_ref, ws_ref, bs_ref,
                wout_ref, gpost_ref, gffn_ref, o_ref, hp_ref, t_ref):
    m = mod_ref[0, 0]
    x = _stream_tile(xp_ref, xs_ref)
    h = _rms(x, gpre_ref[...]) * (1.0 + m[1:2]) + m[0:1]
    z = _gelu_tanh(_bdot(h, win_ref[...]) + bin_ref[...])
    v = _rms(z[:, SGU_DIM:], vg_ref[...])
    gw = SGU_DIM // SGU_GROUPS
    for ch in range(TM // SGU_CHUNK):
        rows = slice(ch * SGU_CHUNK, (ch + 1) * SGU_CHUNK)
        for g in range(SGU_GROUPS):
            cols = slice(g * gw, (g + 1) * gw)
            vs = _bdot(ws_ref[g], v[rows, cols]) + bs_ref[:, g:g + 1]
            t_ref[rows, cols] = (z[rows, cols] * vs).astype(BF16)
    out = jnp.dot(t_ref[...], wout_ref[...], preferred_element_type=F32)
    x1 = x + m[2:3] * _rms(out, gpost_ref[...])
    o_ref[...] = x1
    _store_token_tiles(hp_ref, _ffn_input_rows(x1, m, gffn_ref[...]))


def _sgu(xp, xs, mod, l, g_pre, w_in, b_in, v_g, w_s, b_s_t, w_out, g_post, g_ffn):
    tok = pl.BlockSpec((TM, D), lambda i: (i, 0))
    full = lambda *shape: pl.BlockSpec(shape, lambda i: (0,) * len(shape))
    return pl.pallas_call(
        _sgu_kernel,
        grid=(T // TM,),
        in_specs=_stream_specs() + [
            pl.BlockSpec((1, 1, 6, D), lambda i: (l, _mod_row(i), 0, 0)),
            full(1, D), full(D, 2 * SGU_DIM), full(1, 2 * SGU_DIM), full(1, SGU_DIM),
            full(SGU_GROUPS, SGU_CHUNK, SGU_CHUNK), full(SGU_CHUNK, SGU_GROUPS),
            full(SGU_DIM, D), full(1, D), full(1, D),
        ],
        out_specs=[tok, pl.BlockSpec((TM * ROWS_PER_TOKEN, LANES), lambda i: (i, 0))],
        out_shape=[jax.ShapeDtypeStruct((T, D), F32), jax.ShapeDtypeStruct((T * ROWS_PER_TOKEN, LANES), U32)],
        scratch_shapes=[pltpu.VMEM((TM, SGU_DIM), BF16)],
        compiler_params=_params(("arbitrary",)),
        name="sgu_mixer",
    )(xp, xs, mod, g_pre, w_in, b_in, v_g, w_s, b_s_t, w_out, g_post, g_ffn)


LANES = 128
U32 = jnp.uint32
PACKED = D // 2
ROWS_PER_TOKEN = PACKED // LANES
BF16_BITS = 16
HIGH_HALF = 0xFFFF0000


def _pack_rows(x):
    bits = lax.bitcast_convert_type(x.astype(BF16).astype(F32), U32)
    return bits[:, :PACKED] | (bits[:, PACKED:] >> BF16_BITS)


def _unpack_rows(u):
    return (lax.bitcast_convert_type(u & U32(HIGH_HALF), F32), lax.bitcast_convert_type(u << BF16_BITS, F32))


def _store_token_tiles(ref, u, first=0):
    n = u.shape[0]
    for c in range(ROWS_PER_TOKEN):
        ref[pl.ds(first * ROWS_PER_TOKEN + c, n, stride=ROWS_PER_TOKEN), :] = u[:, c * LANES:(c + 1) * LANES]


def _load_token_tiles(ref, n, first=0):
    return jnp.concatenate([ref[pl.ds(first * ROWS_PER_TOKEN + c, n, stride=ROWS_PER_TOKEN), :]
                            for c in range(ROWS_PER_TOKEN)], axis=1)


def _ffn_input_rows(x, m, g):
    return _pack_rows(_rms(x, g) * (1.0 + m[4:5]) + m[3:4])


def _router_kernel(hp_ref, wr_ref, eb_ref, te_ref, wn_ref, rk_ref, cnt_ref, carry_ref, upper_ref):
    i = pl.program_id(0)

    @pl.when(i == 0)
    def _():
        carry_ref[...] = jnp.zeros_like(carry_ref)
        tj = lax.broadcasted_iota(I32, (TR, TR), 0)
        ti = lax.broadcasted_iota(I32, (TR, TR), 1)
        upper_ref[...] = jnp.where(tj < ti, 1.0, 0.0).astype(BF16)

    h_hi, h_lo = (t.astype(BF16) for t in _unpack_rows(_load_token_tiles(hp_ref, TR)))
    w1, w2, _ = _split3(wr_ref[...])
    nt = lambda a, b: lax.dot_general(a, b, (((1,), (1,)), ((), ())), preferred_element_type=F32)
    logits = (nt(w1[:, :PACKED], h_hi) + nt(w1[:, PACKED:], h_lo)
              + nt(w2[:, :PACKED], h_hi) + nt(w2[:, PACKED:], h_lo))
    scores = jax.nn.sigmoid(logits)
    sel = scores + eb_ref[...]

    row8 = lax.broadcasted_iota(I32, (GROUP_SIZE, TR), 0)
    gscore = []
    for g in range(N_GROUPS):
        xg = sel[g * GROUP_SIZE:(g + 1) * GROUP_SIZE]
        m1 = jnp.max(xg, axis=0, keepdims=True)
        i1 = jnp.min(jnp.where(xg == m1, row8, GROUP_SIZE), axis=0, keepdims=True)
        m2 = jnp.max(jnp.where(row8 == i1, NEG_INF, xg), axis=0, keepdims=True)
        gscore.append(m1 + m2)
    pieces = []
    for g in range(N_GROUPS):
        rank = jnp.zeros((1, TR), I32)
        for g2 in range(N_GROUPS):
            if g2 == g:
                continue
            beats = (gscore[g2] >= gscore[g]) if g2 < g else (gscore[g2] > gscore[g])
            rank = rank + beats.astype(I32)
        pieces.append(jnp.where(rank < TOPK_GROUPS, sel[g * GROUP_SIZE:(g + 1) * GROUP_SIZE], NEG_INF))
    cur = jnp.concatenate(pieces, axis=0)

    row = lax.broadcasted_iota(I32, (N_EXPERTS, TR), 0)
    idxs, ws = [], []
    for _ in range(TOP_K):
        mx = jnp.max(cur, axis=0, keepdims=True)
        idx = jnp.min(jnp.where(cur == mx, row, N_EXPERTS), axis=0, keepdims=True)
        hit = row == idx
        ws.append(jnp.sum(jnp.where(hit, scores, 0.0), axis=0, keepdims=True))
        cur = jnp.where(hit, NEG_INF, cur)
        idxs.append(idx)
    mask = jnp.zeros((N_EXPERTS, TR), F32)
    for idx in idxs:
        mask = mask + (row == idx).astype(F32)
    wsum = ws[0]
    for wk in ws[1:]:
        wsum = wsum + wk

    pos = carry_ref[...] + jnp.dot(mask.astype(BF16), upper_ref[...], preferred_element_type=F32)
    for k in range(TOP_K):
        hit = row == idxs[k]
        te_ref[k:k + 1, :] = idxs[k]
        wn_ref[k:k + 1, :] = ws[k] / wsum * ROUTED_SCALE
        rk_ref[k:k + 1, :] = jnp.sum(jnp.where(hit, pos, 0.0), axis=0, keepdims=True).astype(I32)
    carry_ref[...] = carry_ref[...] + jnp.sum(mask, axis=1, keepdims=True)
    cnt_ref[...] = carry_ref[...]


def _router(hp, wr_t, e_bias):
    kt = lambda dtype: jax.ShapeDtypeStruct((TOP_K, T), dtype)
    kt_spec = pl.BlockSpec((TOP_K, TR), lambda i: (0, i))
    return pl.pallas_call(
        _router_kernel,
        grid=(T // TR,),
        in_specs=[
            pl.BlockSpec((TR * ROWS_PER_TOKEN, LANES), lambda i: (i, 0)),
            pl.BlockSpec((N_EXPERTS, D), lambda i: (0, 0)),
            pl.BlockSpec((N_EXPERTS, 1), lambda i: (0, 0)),
        ],
        out_specs=[
            kt_spec, kt_spec, kt_spec,
            pl.BlockSpec((N_EXPERTS, 1), lambda i: (0, 0)),
        ],
        out_shape=[
            kt(I32), kt(F32), kt(I32),
            jax.ShapeDtypeStruct((N_EXPERTS, 1), F32),
        ],
        scratch_shapes=[pltpu.VMEM((N_EXPERTS, 1), F32), pltpu.VMEM((TR, TR), BF16)],
        compiler_params=_params(("arbitrary",)),
        name="moe_router",
    )(hp, wr_t, e_bias)


_PAD_BITS = tuple(1 << b for b in range(GM.bit_length() - 1))


def _pad_fill_kernel(pad_start_ref, pad_len_ref, xg_in_ref, xg_ref, zero_ref, sem):
    del xg_in_ref
    zero_ref[...] = jnp.zeros_like(zero_ref)

    def pad_copies(e):
        start = pad_start_ref[e]
        n = pad_len_ref[e]
        copies = []
        for bit in _PAD_BITS:
            first = start + (n & ~(2 * bit - 1))
            copies.append(((n & bit) != 0, pltpu.make_async_copy(
                zero_ref.at[pl.ds(0, bit)], xg_ref.at[pl.ds(first, bit)], sem)))
        return copies

    def start_e(e, carry):
        for on, cp in pad_copies(e):
            @pl.when(on)
            def _():
                cp.start()
        return carry

    def wait_e(e, carry):
        for on, cp in pad_copies(e):
            @pl.when(on)
            def _():
                cp.wait()
        return carry

    lax.fori_loop(0, N_EXPERTS, start_e, 0)
    lax.fori_loop(0, N_EXPERTS, wait_e, 0)


def _pad_fill(pad_start, pad_len, xg):
    grid_spec = pltpu.PrefetchScalarGridSpec(
        num_scalar_prefetch=2,
        grid=(1,),
        in_specs=[pl.BlockSpec(memory_space=pl.ANY)],
        out_specs=pl.BlockSpec(memory_space=pl.ANY),
        scratch_shapes=[pltpu.VMEM((GM // 2, ROWS_PER_TOKEN, LANES), xg.dtype), pltpu.SemaphoreType.DMA],
    )
    return pl.pallas_call(
        _pad_fill_kernel,
        grid_spec=grid_spec,
        out_shape=jax.ShapeDtypeStruct(xg.shape, xg.dtype),
        input_output_aliases={2: 0},
        compiler_params=_params(("arbitrary",)),
        name="moe_pad_fill",
    )(pad_start, pad_len, xg)


SC_CORES, SC_SUBCORES = 2, 16
SC_WORKERS = SC_CORES * SC_SUBCORES
SC_W = 64


def _sc_worker_id():
    return lax.axis_index("s") * SC_CORES + lax.axis_index("c")


def _sc_dispatch(h3, slot3):
    nchunk = T // SC_WORKERS // SC_W
    mesh = plsc.VectorSubcoreMesh(core_axis_name="c", subcore_axis_name="s")
    tile = (SC_W, ROWS_PER_TOKEN, LANES)

    @functools.partial(
        pl.kernel, mesh=mesh,
        out_type=jax.ShapeDtypeStruct((SP, ROWS_PER_TOKEN, LANES), h3.dtype),
        scratch_types=[pltpu.VMEM((TOP_K, SC_W), I32), pltpu.VMEM((TOP_K, SC_W), I32),
                       pltpu.VMEM(tile, h3.dtype), pltpu.VMEM(tile, h3.dtype),
                       pltpu.SemaphoreType.DMA((2,)), pltpu.SemaphoreType.DMA((2,))],
    )
    def k(h_hbm, slot_hbm, xg_hbm, idx0, idx1, rows0, rows1, lsem, ssem):
        first = _sc_worker_id() * nchunk
        idx = (idx0, idx1)
        rows = (rows0, rows1)

        def loads(j, b):
            blk = first + j
            tok = pl.multiple_of(blk * SC_W, SC_W)
            return (pltpu.make_async_copy(slot_hbm.at[blk], idx[b], lsem.at[b]),
                    pltpu.make_async_copy(h_hbm.at[pl.ds(tok, SC_W)], rows[b], lsem.at[b]))

        def scatters(b):
            return [pltpu.make_async_copy(rows[b], xg_hbm.at[idx[b].at[kk]], ssem.at[b]) for kk in range(TOP_K)]

        for cp in loads(0, 0):
            cp.start()

        @pl.loop(0, nchunk, step=2)
        def _(j):
            for b in (0, 1):
                jj = j + b
                for cp in loads(jj, b):
                    cp.wait()
                for cp in scatters(b):
                    cp.start()

                @pl.when(jj + 1 < nchunk)
                def _():
                    @pl.when(jj >= 1)
                    def _():
                        for cp in scatters(1 - b):
                            cp.wait()
                    for cp in loads(jj + 1, 1 - b):
                        cp.start()

        for b in (0, 1):
            for cp in scatters(b):
                cp.wait()

    return k(h3, slot3)


def _gmm_kernel(layer, tile_e_ref, tile_blk_ref, tile_nsub_ref, tile_run_ref, tile_next_ref,
                x_ref, wg_hbm, wu_hbm, wd_hbm, y_ref, wg_st, wu_st, wd_st, wgu_scr, wd_scr, sems):
    j = pl.program_id(0)
    nsub = tile_nsub_ref[j]
    run = tile_run_ref[j]

    def weight_copies(e, slot):
        return [pltpu.make_async_copy(src.at[layer, e], dst.at[slot], sems.at[slot])
                for src, dst in ((wg_hbm, wg_st), (wu_hbm, wu_st), (wd_hbm, wd_st))]

    @pl.when(run >= 0)
    def _():
        @pl.when(j == 0)
        def _():
            for cp in weight_copies(tile_e_ref[j], run):
                cp.start()

        for cp in weight_copies(tile_e_ref[j], run):
            cp.wait()

        nxt = tile_next_ref[j]

        @pl.when(nxt >= 0)
        def _():
            for cp in weight_copies(nxt, 1 - run):
                cp.start()

        wgu_scr[:, 0:D_EXPERT] = wg_st[run].astype(BF16)
        wgu_scr[:, D_EXPERT:2 * D_EXPERT] = wu_st[run].astype(BF16)
        wd_scr[...] = wd_st[run].astype(BF16)

    def expert_mlp(s):
        x_hi, x_lo = _unpack_rows(_load_token_tiles(x_ref, GM_SUB, s * GM_SUB))
        gu = (jnp.dot(x_hi.astype(BF16), wgu_scr[0:PACKED, :], preferred_element_type=F32)
              + jnp.dot(x_lo.astype(BF16), wgu_scr[PACKED:D, :], preferred_element_type=F32))
        hid = _silu(gu[:, 0:D_EXPERT]) * gu[:, D_EXPERT:2 * D_EXPERT]
        y = jnp.dot(hid.astype(BF16), wd_scr[...], preferred_element_type=F32)
        _store_token_tiles(y_ref, _pack_rows(y), s * GM_SUB)

    for n in range(1, GM // GM_SUB + 1):
        @pl.when(nsub == n)
        def _():
            for s in range(n):
                expert_mlp(s)
            for s in range(n, GM // GM_SUB):
                _store_token_tiles(y_ref, jnp.zeros((GM_SUB, PACKED), U32), s * GM_SUB)


def _gmm(tile_e, tile_blk, tile_nsub, tile_run, tile_next, xg, l, w_gate, w_up, w_down):
    row_tile = pl.BlockSpec((GM * ROWS_PER_TOKEN, LANES), lambda j, te, tb, *_: (tb[j], 0))
    hbm = pl.BlockSpec(memory_space=pl.ANY)
    grid_spec = pltpu.PrefetchScalarGridSpec(
        num_scalar_prefetch=5,
        grid=(NT_MAX,),
        in_specs=[row_tile, hbm, hbm, hbm],
        out_specs=row_tile,
        scratch_shapes=[pltpu.VMEM((2, D, D_EXPERT), F32), pltpu.VMEM((2, D, D_EXPERT), F32),
                        pltpu.VMEM((2, D_EXPERT, D), F32),
                        pltpu.VMEM((D, 2 * D_EXPERT), BF16), pltpu.VMEM((D_EXPERT, D), BF16),
                        pltpu.SemaphoreType.DMA((2,))],
    )
    return pl.pallas_call(
        functools.partial(_gmm_kernel, l),
        grid_spec=grid_spec,
        out_shape=jax.ShapeDtypeStruct((SP * ROWS_PER_TOKEN, LANES), U32),
        compiler_params=_params(("arbitrary",)),
        name="moe_grouped_matmul",
    )(tile_e, tile_blk, tile_nsub, tile_run, tile_next, xg, w_gate, w_up, w_down)


def _sc_gather(table3, idx):
    n_idx = idx.shape[0]
    per_w = n_idx // SC_WORKERS
    nchunk = per_w // SC_W
    mesh = plsc.VectorSubcoreMesh(core_axis_name="c", subcore_axis_name="s")
    tile = (SC_W, ROWS_PER_TOKEN, LANES)

    @functools.partial(
        pl.kernel, mesh=mesh,
        out_type=jax.ShapeDtypeStruct((n_idx, ROWS_PER_TOKEN, LANES), table3.dtype),
        scratch_types=[pltpu.VMEM((per_w,), I32), pltpu.VMEM(tile, table3.dtype), pltpu.VMEM(tile, table3.dtype),
                       pltpu.SemaphoreType.DMA((2,)), pltpu.SemaphoreType.DMA((2,))],
    )
    def k(table_hbm, idx_hbm, out_hbm, idx_v, rows0, rows1, gsem, wsem):
        base = pl.multiple_of(_sc_worker_id() * per_w, per_w)
        rows = (rows0, rows1)
        pltpu.sync_copy(idx_hbm.at[pl.ds(base, per_w)], idx_v)

        def gather(j, b):
            ids = idx_v.at[pl.ds(pl.multiple_of(j * SC_W, SC_W), SC_W)]
            return pltpu.make_async_copy(table_hbm.at[ids], rows[b], gsem.at[b])

        def write(j, b):
            dst = out_hbm.at[pl.ds(pl.multiple_of(base + j * SC_W, SC_W), SC_W)]
            return pltpu.make_async_copy(rows[b], dst, wsem.at[b])

        gather(0, 0).start()

        @pl.loop(0, nchunk, step=2)
        def _(j):
            for b in (0, 1):
                jj = j + b
                gather(jj, b).wait()
                write(jj, b).start()

                @pl.when(jj + 1 < nchunk)
                def _():
                    @pl.when(jj >= 1)
                    def _():
                        write(jj - 1, 1 - b).wait()
                    gather(jj + 1, 1 - b).start()

        write(nchunk - 2, 0).wait()
        write(nchunk - 1, 1).wait()

    return k(table3, idx)


def _combine_kernel(wn_ref, x_ref, mod_ref, gpre_ref, gp_ref, wsg_ref, wsu_ref, wsd_ref, y_ref, o_ref, eye_ref):
    @pl.when(pl.program_id(0) == 0)
    def _():
        r = lax.broadcasted_iota(I32, (TD, TD), 0)
        c = lax.broadcasted_iota(I32, (TD, TD), 1)
        eye_ref[...] = jnp.where(r == c, 1.0, 0.0).astype(BF16)

    m = mod_ref[0, 0]
    x = x_ref[...]
    hb = (_rms(x, gpre_ref[...]) * (1.0 + m[4:5]) + m[3:4]).astype(BF16)
    hid = (_silu(jnp.dot(hb, wsg_ref[...], preferred_element_type=F32))
           * jnp.dot(hb, wsu_ref[...], preferred_element_type=F32))
    acc = jnp.dot(hid.astype(BF16), wsd_ref[...], preferred_element_type=F32)

    eye = eye_ref[...]
    nt = lambda a, b: lax.dot_general(a, b, (((1,), (1,)), ((), ())), preferred_element_type=F32)
    w1, w2, w3 = _split3(wn_ref[...])
    w_t = nt(eye, w1) + nt(eye, w2) + nt(eye, w3)

    acc_hi = acc[:, :PACKED]
    acc_lo = acc[:, PACKED:]
    for k in range(TOP_K):
        y_hi, y_lo = _unpack_rows(_load_token_tiles(y_ref, TD, k * TD))
        acc_hi = acc_hi + y_hi * w_t[:, k:k + 1]
        acc_lo = acc_lo + y_lo * w_t[:, k:k + 1]
    acc = jnp.concatenate([acc_hi, acc_lo], axis=1)
    o_ref[...] = x + m[5:6] * _rms(acc, gp_ref[...])


def _combine_kernel_in_place(wn_ref, x_ref, mod_ref, gpre_ref, gp_ref, wsg_ref, wsu_ref, wsd_ref, y_ref,
                             prev_ref, o_ref, eye_ref):
    del prev_ref
    _combine_kernel(wn_ref, x_ref, mod_ref, gpre_ref, gp_ref, wsg_ref, wsu_ref, wsd_ref, y_ref, o_ref, eye_ref)


def _combine(wn, x, mod, l, g_pre, g_post, ws_gate, ws_up, ws_down, ybuf, first_tok, n_tok, out, out_first):
    off = first_tok // TD
    out_off = out_first // TD
    fresh = isinstance(out, jax.ShapeDtypeStruct)
    kernel_fn = _combine_kernel if fresh else _combine_kernel_in_place
    tiles_per_dec = DEC_SEQ // TD
    npd = TP // TD
    mod_row = lambda i: jnp.where(i + off < npd, 0, 1 + (i + off - npd) // tiles_per_dec)
    full = lambda *shape: pl.BlockSpec(shape, lambda i: (0,) * len(shape))
    y_spec = pl.BlockSpec((TOP_K * TD * ROWS_PER_TOKEN, LANES), lambda i: (i, 0))
    operands = [wn, x, mod, g_pre, g_post, ws_gate, ws_up, ws_down, ybuf] + ([] if fresh else [out])
    return pl.pallas_call(
        kernel_fn,
        grid=(n_tok // TD,),
        in_specs=[
            pl.BlockSpec((TOP_K, TD), lambda i: (0, i + off)),
            pl.BlockSpec((TD, D), lambda i: (i + off, 0)),
            pl.BlockSpec((1, 1, 6, D), lambda i: (l, mod_row(i), 0, 0)),
            full(1, D), full(1, D), full(D, D_SHARED), full(D, D_SHARED), full(D_SHARED, D),
            y_spec,
        ] + ([] if fresh else [pl.BlockSpec(memory_space=pl.ANY)]),
        out_specs=pl.BlockSpec((TD, D), lambda i: (i + out_off, 0)),
        out_shape=jax.ShapeDtypeStruct(out.shape, out.dtype),
        input_output_aliases={} if fresh else {len(operands) - 1: 0},
        scratch_shapes=[pltpu.VMEM((TD, TD), BF16)],
        compiler_params=_params(("arbitrary",)),
        name="moe_combine",
    )(*operands)


def _moe_layer(x, h, mod, l, g_pre, g_post, w_router, e_bias, w_gate, w_up, w_down,
               ws_gate, ws_up, ws_down):
    top_e, wn, rk, cnt = _router(h, w_router.T, e_bias.reshape(N_EXPERTS, 1))
    cnt = cnt.reshape(N_EXPERTS).astype(I32)
    padded = (cnt + GM - 1) // GM * GM
    ends = jnp.cumsum(padded)
    offs = ends - padded
    eid = jnp.arange(N_EXPERTS, dtype=I32)[:, None, None]
    slot = rk + jnp.sum(jnp.where(top_e[None] == eid, offs[:, None, None], 0), axis=0)
    nvalid = ends[-1] // GM
    tile_start = jnp.arange(NT_MAX, dtype=I32) * GM
    tile_raw = jnp.sum((tile_start[:, None] >= ends[None, :]).astype(I32), axis=1)
    last = jnp.maximum(nvalid - 1, 0)
    tile_blk = jnp.minimum(jnp.arange(NT_MAX, dtype=I32), last)
    tile_e = jnp.minimum(tile_raw, N_EXPERTS - 1)
    tile_e = jnp.where(jnp.arange(NT_MAX) <= last, tile_e, tile_e[last])
    own = tile_e[:, None] == jnp.arange(N_EXPERTS, dtype=I32)[None, :]
    tile_rows = jnp.sum(jnp.where(own, (offs + cnt)[None, :], 0), axis=1) - tile_start
    tile_nsub = jnp.where(jnp.arange(NT_MAX) <= last, (jnp.clip(tile_rows, 0, GM) + GM_SUB - 1) // GM_SUB, 0)
    starts = jnp.logical_and(jnp.arange(NT_MAX) <= last,
                             jnp.concatenate([jnp.ones((1,), bool), tile_e[1:] != tile_e[:-1]]))
    tile_run = jnp.where(starts, (jnp.cumsum(starts.astype(I32)) - 1) % 2, -1)
    later = jnp.logical_and(jnp.arange(N_EXPERTS, dtype=I32)[None, :] > tile_e[:, None], (cnt > 0)[None, :])
    tile_next = jnp.min(jnp.where(later, jnp.arange(N_EXPERTS, dtype=I32)[None, :], N_EXPERTS), axis=1)
    tile_next = jnp.where(tile_next < N_EXPERTS, tile_next, -1)
    slot3 = slot.reshape(TOP_K, T // SC_W, SC_W).transpose(1, 0, 2)
    xg = _sc_dispatch(h.reshape(T, ROWS_PER_TOKEN, LANES), slot3)
    xg = _pad_fill(offs + cnt, padded - cnt, xg).reshape(SP * ROWS_PER_TOKEN, LANES)
    yg = _gmm(tile_e, tile_blk, tile_nsub.astype(I32), tile_run.astype(I32), tile_next.astype(I32),
              xg, l, w_gate, w_up, w_down)
    yg3 = yg.reshape(SP, ROWS_PER_TOKEN, LANES)
    ws = (ws_gate.astype(BF16), ws_up.astype(BF16), ws_down.astype(BF16))
    outs = []
    for stream_first, stream_tok in ((0, TP), (TP, TS)):
        out = jax.ShapeDtypeStruct((stream_tok, D), F32)
        n_tok = stream_tok // COMBINE_PARTS
        for part in range(COMBINE_PARTS):
            first_tok = stream_first + part * n_tok
            ids = slot[:, first_tok:first_tok + n_tok].reshape(TOP_K, n_tok // TD, TD).transpose(1, 0, 2)
            ybuf = _sc_gather(yg3, ids.reshape(TOP_K * n_tok))
            out = _combine(wn, x, mod, l, g_pre, g_post, *ws,
                           ybuf.reshape(TOP_K * n_tok * ROWS_PER_TOKEN, LANES), first_tok, n_tok,
                           out, part * n_tok)
        outs.append(out)
    return outs


def _rope_tables():
    n = DEC_SEQ
    rows = n // GRID_W
    row = jnp.repeat(jnp.arange(rows), GRID_W).astype(F32)
    col = jnp.tile(jnp.arange(GRID_W), rows).astype(F32)
    half = DQK_B // 2
    inv = ROPE_BASE ** (-jnp.arange(0, half, 2, dtype=F32) / half)
    ang_r = row[:, None] * inv
    ang_c = col[:, None] * inv
    ang = jnp.concatenate([ang_r, ang_r, ang_c, ang_c], axis=-1)
    reps = QK_B // DQK_B
    return jnp.tile(jnp.cos(ang), (1, reps)), jnp.tile(jnp.sin(ang), (1, reps))


def _pad_in_proj(w):
    s = [0, Q_A, 2 * Q_A, 2 * Q_A + V_A, 2 * Q_A + 2 * V_A]
    s += [s[-1] + GATE_RANK, s[-1] + 2 * GATE_RANK]
    s += [s[-1] + QK_B, s[-1] + 2 * QK_B, s[-1] + 2 * QK_B + V_B]
    gates = jnp.pad(w[:, s[4]:s[6]], ((0, 0), (0, GL_PAD - 2 * GATE_RANK)))
    return jnp.concatenate([w[:, s[0]:s[4]], gates, w[:, s[6]:s[9]]], axis=1).astype(BF16)


def kernel(x_prompt, x_sample, c, c_ctx, state_gla, cache_k, cache_v, ada_w, ada_b, norm_pre_mix, norm_post_mix, norm_pre_ffn, norm_post_ffn, ab_w_in, gla_w_g2, gla_b_g2, gla_norm_g, diff_lambda, diff_norm_g, ab_w_out, sgu_w_in, sgu_b_in, sgu_norm_g, sgu_w_s, sgu_b_s, sgu_w_out, moe_w_router, moe_e_bias, moe_w_gate, moe_w_up, moe_w_down, moe_ws_gate, moe_ws_up, moe_ws_down):
    depth = ada_w.shape[0]
    assert x_prompt.shape == (BATCH, SEQ, D) and x_sample.shape == (DEC_BATCH, DEC_SEQ, D)
    assert state_gla.shape == (DEC_BATCH, (depth + 1) // 2, 2, H_A, DK_A, DV_A)
    assert cache_k.shape == (DEC_BATCH, 1, H_B, 2, PAST_LEN, DQK_B) and cache_v.shape == (DEC_BATCH, 1, H_B, PAST_LEN, DV_B)
    assert depth == 2 and moe_w_gate.shape == (depth, N_EXPERTS, D, D_EXPERT)
    xp, xs = x_prompt.reshape(TP, D), x_sample.reshape(TS, D)
    cond = jnp.concatenate([c_ctx[None, :], c, jnp.zeros((8 - 1 - DEC_BATCH, D), F32)], axis=0)
    mod = _modulation(cond, ada_w, ada_b)
    cos, sin = _rope_tables()
    vec = lambda a: a.reshape(1, -1)
    new_s = new_k = new_v = None
    for l in range(depth):
        if l % 2 == 0:
            e = l // 2
            lam_init = 0.8 - 0.6 * math.exp(-0.3 * l)
            a, r_a, gl, q_b, k_b, v_b, new_k, new_v = _in_proj(xp, xs, mod, l, vec(norm_pre_mix[l]),
                                                               _pad_in_proj(ab_w_in[e]), cos, sin)
            s0_t = jnp.swapaxes(state_gla[:, e], -1, -2)
            same_head = jnp.eye(H_A, dtype=bool)[None, None, :, None, :, None]
            s0_t = jnp.where(same_head, s0_t[:, :, :, :, None, :], 0.0).reshape(DEC_BATCH, 2, V_A, Q_A)
            o_f, o_bw, s_fin_t = _gla(a, gl, gla_w_g2[e], gla_b_g2[e].reshape(2, 1, Q_A), s0_t)
            o_att = _diff_attention(q_b, k_b, v_b, cache_k, cache_v, diff_lambda[e], lam_init)
            x, h = _mix_out(lam_init, o_f, o_bw, r_a, o_att, xp, xs, mod, l, vec(gla_norm_g[e]), vec(diff_norm_g[e]),
                            ab_w_out[e].astype(BF16), vec(norm_post_mix[l]), vec(norm_pre_ffn[l]))
            new_s = jnp.swapaxes(s_fin_t, -1, -2)[:, None]
        else:
            o = l // 2
            x, h = _sgu(xp, xs, mod, l, vec(norm_pre_mix[l]), sgu_w_in[o].astype(BF16), vec(sgu_b_in[o]),
                        vec(sgu_norm_g[o]), sgu_w_s[o], sgu_b_s[o].T, sgu_w_out[o].astype(BF16),
                        vec(norm_post_mix[l]), vec(norm_pre_ffn[l]))
        xp, xs = _moe_layer(x, h, mod, l, vec(norm_pre_ffn[l]), vec(norm_post_ffn[l]), moe_w_router[l], moe_e_bias[l],
                            moe_w_gate, moe_w_up, moe_w_down, moe_ws_gate[l], moe_ws_up[l], moe_ws_down[l])
    y_prompt = xp.reshape(BATCH, SEQ, D)
    y_sample = xs.reshape(DEC_BATCH, DEC_SEQ, D)
    return (y_prompt, y_sample, new_s, new_k, new_v)
```

```python
import functools
import math

import jax
import jax.numpy as jnp
from jax import lax
from jax.experimental import pallas as pl
from jax.experimental.pallas import tpu as pltpu
from jax.experimental.pallas import tpu_sc as plsc

F32 = jnp.float32
BF16 = jnp.bfloat16
I32 = jnp.int32

D = 1024
BATCH, SEQ = 32, 256
DEC_BATCH, DEC_SEQ = 4, 2048
PAST_LEN = 256
GRID_W = 64
EPS = 1e-6
TP = BATCH * SEQ
TS = DEC_BATCH * DEC_SEQ
T = TP + TS
H_A, DK_A, DV_A = 4, 64, 128
Q_A, V_A = H_A * DK_A, H_A * DV_A
GATE_RANK, GATE_TAU, GLA_CHUNK = 16, 16.0, 64
H_B, DQK_B, DV_B = 4, 64, 128
QK_B, V_B = H_B * 2 * DQK_B, H_B * DV_B
ROPE_BASE = 10000.0
SGU_DIM, SGU_GROUPS, SGU_CHUNK = 1024, 4, 128
N_EXPERTS, TOP_K, N_GROUPS, TOPK_GROUPS = 64, 8, 8, 4
GROUP_SIZE = N_EXPERTS // N_GROUPS
D_EXPERT, D_SHARED = 256, 256
ROUTED_SCALE = 2.5

TM = 512
TM_SUB = 256
NPT = TP // TM
TILES_PER_DEC = DEC_SEQ // TM
SEG = 256
NSEG = T // SEG
NSEG_P = TP // SEG
SEG_PER_DEC = DEC_SEQ // SEG
TR = 512
TD = 512
GM = 1024
GM_SUB = 256
NT_MAX = T * TOP_K // GM + N_EXPERTS
SP = T * TOP_K + N_EXPERTS * GM_SUB
GL_PAD = 128
VMEM_LIMIT = 56 * 1024 * 1024
NEG_INF = float("-inf")


def _bdot(a, b):
    return jnp.dot(a.astype(BF16), b.astype(BF16), preferred_element_type=F32)


def _bdot_nt(a, b):
    return lax.dot_general(a.astype(BF16), b.astype(BF16), (((1,), (1,)), ((), ())),
                           preferred_element_type=F32)


def _bdot_tn(a, b):
    return lax.dot_general(a.astype(BF16), b.astype(BF16), (((0,), (0,)), ((), ())),
                           preferred_element_type=F32)


def _split3(x):
    x1 = x.astype(BF16)
    r1 = x - x1.astype(F32)
    x2 = r1.astype(BF16)
    x3 = (r1 - x2.astype(F32)).astype(BF16)
    return x1, x2, x3


def _rms(x, g):
    return x * lax.rsqrt(jnp.mean(x * x, axis=-1, keepdims=True) + EPS) * g


def _silu(x):
    return x * jax.nn.sigmoid(x)


def _mod_row(i):
    return jnp.where(i < NPT, 0, 1 + (i - NPT) // TILES_PER_DEC)


def _params(sem, limit=VMEM_LIMIT):
    return pltpu.CompilerParams(dimension_semantics=sem, vmem_limit_bytes=limit)


def _mod_kernel(c_ref, w_ref, b_ref, o_ref):
    o_ref[0] = _bdot(_silu(c_ref[...]), w_ref[0]) + b_ref[0]


def _modulation(cond, ada_w, ada_b):
    depth = ada_w.shape[0]
    nj = 6
    out = pl.pallas_call(
        _mod_kernel,
        grid=(depth, nj),
        in_specs=[
            pl.BlockSpec((8, D), lambda l, j: (0, 0)),
            pl.BlockSpec((1, D, D), lambda l, j: (l, 0, j)),
            pl.BlockSpec((1, 1, D), lambda l, j: (l, 0, j)),
        ],
        out_specs=pl.BlockSpec((1, 8, D), lambda l, j: (l, 0, j)),
        out_shape=jax.ShapeDtypeStruct((depth, 8, 6 * D), F32),
        compiler_params=_params(("arbitrary", "arbitrary")),
        name="adaln_modulation",
    )(cond, ada_w, ada_b.reshape(depth, 1, 6 * D))
    return out.reshape(depth, 8, 6, D)


_C_A = 0
_C_R = _C_A + 2 * Q_A + V_A
_C_GL = _C_R + V_A
_C_Q = _C_GL + GL_PAD
_C_K = _C_Q + QK_B
_C_V = _C_K + QK_B
_C_END = _C_V + V_B
ROT_PAIR = DQK_B // 4


def _rope(x, cos, sin):
    lane = lax.broadcasted_iota(I32, x.shape, 1)
    first = (lane % (2 * ROT_PAIR)) < ROT_PAIR
    n = x.shape[1]
    xr = jnp.where(first, -pltpu.roll(x, n - ROT_PAIR, 1), pltpu.roll(x, ROT_PAIR, 1))
    return x * cos + xr * sin


def _stream_specs():
    return [pl.BlockSpec((TM, D), lambda i: (jnp.minimum(i, NPT - 1), 0)),
            pl.BlockSpec((TM, D), lambda i: (jnp.maximum(i - NPT, 0), 0))]


def _stream_tile(xp_ref, xs_ref):
    return jnp.where(pl.program_id(0) < NPT, xp_ref[...], xs_ref[...])


def _in_kernel(xp_ref, xs_ref, mod_ref, g_ref, w_ref, cos_ref, sin_ref,
               a_ref, r_ref, gl_ref, q_ref, k_ref, v_ref, ck_ref, cv_ref):
    latent = pl.program_id(0) >= NPT
    m = mod_ref[0, 0]
    x = _stream_tile(xp_ref, xs_ref)
    for s in range(TM // TM_SUB):
        rows = slice(s * TM_SUB, (s + 1) * TM_SUB)
        h = _rms(x[rows], g_ref[...]) * (1.0 + m[1:2]) + m[0:1]
        hb = h.astype(BF16)

        def proj(c0, c1):
            return jnp.dot(hb, w_ref[:, c0:c1], preferred_element_type=F32)

        a_ref[rows, :] = proj(_C_A, _C_R)
        r_ref[rows, :] = proj(_C_R, _C_GL)
        gl_ref[rows, :] = proj(_C_GL, _C_Q)
        v_ref[rows, :] = proj(_C_V, _C_END)
        q = proj(_C_Q, _C_K)
        k = proj(_C_K, _C_V)
        cos = cos_ref[rows, :]
        sin = sin_ref[rows, :]
        q_ref[rows, :] = jnp.where(latent, _rope(q, cos, sin), q)
        k_ref[rows, :] = jnp.where(latent, _rope(k, cos, sin), k)

    @pl.when(jnp.logical_not(latent))
    def _():
        for s in range(TM // SEQ):
            rows = slice(s * SEQ, (s + 1) * SEQ)
            for h in range(H_B):
                cv_ref[s, 0, h] = v_ref[rows, h * DV_B:(h + 1) * DV_B]
                for mp in range(2):
                    ck_ref[s, 0, h, mp] = k_ref[rows, (2 * h + mp) * DQK_B:(2 * h + mp + 1) * DQK_B]


def _in_proj(xp, xs, mod, l, g, w_pad, cos, sin):
    tok = lambda width: pl.BlockSpec((TM, width), lambda i: (i, 0))
    rope_spec = pl.BlockSpec((TM, QK_B), lambda i: (jnp.maximum(i - NPT, 0) % TILES_PER_DEC, 0))
    widths = (2 * Q_A + V_A, V_A, GL_PAD, QK_B, QK_B, V_B)
    seqs = TM // SEQ
    prompt_tile = lambda i: jnp.minimum(i, NPT - 1)
    cache_specs = [pl.BlockSpec((seqs, 1, H_B, 2, SEQ, DQK_B), lambda i: (prompt_tile(i), 0, 0, 0, 0, 0)),
                   pl.BlockSpec((seqs, 1, H_B, SEQ, DV_B), lambda i: (prompt_tile(i), 0, 0, 0, 0))]
    cache_shapes = [jax.ShapeDtypeStruct((BATCH, 1, H_B, 2, SEQ, DQK_B), F32),
                    jax.ShapeDtypeStruct((BATCH, 1, H_B, SEQ, DV_B), F32)]
    return pl.pallas_call(
        _in_kernel,
        grid=(T // TM,),
        in_specs=_stream_specs() + [
            pl.BlockSpec((1, 1, 6, D), lambda i: (l, _mod_row(i), 0, 0)),
            pl.BlockSpec((1, D), lambda i: (0, 0)),
            pl.BlockSpec((D, _C_END), lambda i: (0, 0)),
            rope_spec, rope_spec,
        ],
        out_specs=[tok(w) for w in widths] + cache_specs,
        out_shape=[jax.ShapeDtypeStruct((T, w), F32) for w in widths] + cache_shapes,
        compiler_params=_params(("arbitrary",)),
        name="mixer_ab_in_proj",
    )(xp, xs, mod, g, w_pad, cos, sin)


def _log_sigmoid(x):
    return jnp.minimum(x, 0.0) - jnp.log(1.0 + jnp.exp(-jnp.abs(x)))


def _gla_kernel(af_ref, ab_ref, glf_ref, glb_ref, wg_ref, bg_ref, s0_ref,
                of_ref, ob_ref, sfin_ref, st_ref):
    i = pl.program_id(0)

    @pl.when(i < NSEG_P)
    def _():
        st_ref[...] = jnp.zeros_like(st_ref)

    @pl.when(jnp.logical_and(i >= NSEG_P, (i - NSEG_P) % SEG_PER_DEC == 0))
    def _():
        st_ref[...] = s0_ref[0]

    r = lax.broadcasted_iota(I32, (SEG, SEG), 0)
    c = lax.broadcasted_iota(I32, (SEG, SEG), 1)
    same = (r // GLA_CHUNK) == (c // GLA_CHUNK)
    nchunk = SEG // GLA_CHUNK
    own_head = (lax.broadcasted_iota(I32, (V_A, Q_A), 0) // DV_A) == (lax.broadcasted_iota(I32, (V_A, Q_A), 1) // DK_A)

    for d, (a_ref, gl_ref, o_ref) in enumerate(((af_ref, glf_ref, of_ref), (ab_ref, glb_ref, ob_ref))):
        fwd = d == 0
        gcol = gl_ref[:, d * GATE_RANK:(d + 1) * GATE_RANK]
        la = _log_sigmoid(_bdot(gcol, wg_ref[d]) + bg_ref[d]) / GATE_TAU
        causal = jnp.logical_and(same, (c <= r) if fwd else (c >= r))
        tri = jnp.where(causal, 1.0, 0.0).astype(BF16)
        l1, l2, l3 = _split3(la)
        b_all = (jnp.dot(tri, l1, preferred_element_type=F32)
                 + jnp.dot(tri, l2, preferred_element_type=F32)
                 + jnp.dot(tri, l3, preferred_element_type=F32))
        q_in_all = a_ref[:, 0:Q_A] * (DK_A ** -0.5) * jnp.exp(b_all)
        kd_all = a_ref[:, Q_A:2 * Q_A] * jnp.exp(-b_all)
        intra = []
        for h in range(H_A):
            kc = slice(h * DK_A, (h + 1) * DK_A)
            attn = jnp.where(causal, _bdot_nt(q_in_all[:, kc], kd_all[:, kc]), 0.0)
            intra.append(_bdot(attn, a_ref[:, 2 * Q_A + h * DV_A:2 * Q_A + (h + 1) * DV_A]))
        intra = jnp.concatenate(intra, axis=1)
        state = st_ref[d]
        order = range(nchunk) if fwd else range(nchunk - 1, -1, -1)
        for ch in order:
            r0 = ch * GLA_CHUNK
            rows = slice(r0, r0 + GLA_CHUNK)
            end = r0 + GLA_CHUNK - 1 if fwd else r0
            b_end = b_all[end:end + 1, :]
            kw = a_ref[rows, Q_A:2 * Q_A] * jnp.exp(b_end - b_all[rows, :])
            o_ref[rows, :] = intra[rows, :] + _bdot_nt(q_in_all[rows, :], state)
            kv_t = _bdot_tn(a_ref[rows, 2 * Q_A:2 * Q_A + V_A], kw)
            state = state * jnp.exp(b_end) + jnp.where(own_head, kv_t, 0.0)
        st_ref[d] = state

    @pl.when(i < NSEG_P)
    def _():
        for d in range(2):
            for h in range(H_A):
                sfin_ref[0, d, h] = st_ref[d, h * DV_A:(h + 1) * DV_A, h * DK_A:(h + 1) * DK_A]


def _seg_bwd(i):
    j = i - NSEG_P
    return jnp.where(i < NSEG_P, i, NSEG_P + (j // SEG_PER_DEC) * SEG_PER_DEC + (SEG_PER_DEC - 1 - j % SEG_PER_DEC))


def _gla(a, gl, wg, bg, s0_t):
    seg = lambda width, f: pl.BlockSpec((SEG, width), lambda i: (f(i), 0))
    ident = lambda i: i
    st_block = (1, 2, H_A, DV_A, DK_A)
    return pl.pallas_call(
        _gla_kernel,
        grid=(NSEG,),
        in_specs=[
            seg(D, ident), seg(D, _seg_bwd), seg(GL_PAD, ident), seg(GL_PAD, _seg_bwd),
            pl.BlockSpec((2, GATE_RANK, Q_A), lambda i: (0, 0, 0)),
            pl.BlockSpec((2, 1, Q_A), lambda i: (0, 0, 0)),
            pl.BlockSpec((1, 2, V_A, Q_A), lambda i: (jnp.maximum(i - NSEG_P, 0) // SEG_PER_DEC, 0, 0, 0)),
        ],
        out_specs=[
            seg(V_A, ident), seg(V_A, _seg_bwd),
            pl.BlockSpec(st_block, lambda i: (jnp.minimum(i, NSEG_P - 1), 0, 0, 0, 0)),
        ],
        out_shape=[
            jax.ShapeDtypeStruct((T, V_A), F32),
            jax.ShapeDtypeStruct((T, V_A), F32),
            jax.ShapeDtypeStruct((BATCH, 2, H_A, DV_A, DK_A), F32),
        ],
        scratch_shapes=[pltpu.VMEM((2, V_A, Q_A), F32)],
        compiler_params=_params(("arbitrary",)),
        name="gla_bidir",
    )(a, a, gl, gl, wg, bg, s0_t)


def _diff_lambda(lam_ref, lam_init):
    lp = lam_ref[...]
    s01 = jnp.sum(lp[0:1] * lp[1:2], axis=1, keepdims=True)
    s23 = jnp.sum(lp[2:3] * lp[3:4], axis=1, keepdims=True)
    return jnp.exp(s01) - jnp.exp(s23) + lam_init


def _attn_prompt_kernel(lam_init, q_ref, k_ref, v_ref, lam_ref, o_ref):
    lam = _diff_lambda(lam_ref, lam_init)
    for h in range(H_B):
        ps = []
        for m in range(2):
            cols = slice((2 * h + m) * DQK_B, (2 * h + m + 1) * DQK_B)
            s = _bdot_nt(q_ref[:, cols] * (DQK_B ** -0.5), k_ref[:, cols])
            e = jnp.exp(s - jnp.max(s, axis=1, keepdims=True))
            ps.append(e * (1.0 / jnp.sum(e, axis=1, keepdims=True)))
        w = ps[0] - lam * ps[1]
        o_ref[:, h * DV_B:(h + 1) * DV_B] = _bdot(w, v_ref[:, h * DV_B:(h + 1) * DV_B])


def _attn_sample_kernel(lam_init, q_ref, k_ref, v_ref, ck_ref, cv_ref, lam_ref, o_ref):
    lam = _diff_lambda(lam_ref, lam_init)
    for h in range(H_B):
        parts = []
        for m in range(2):
            cols = slice((2 * h + m) * DQK_B, (2 * h + m + 1) * DQK_B)
            q = q_ref[:, cols] * (DQK_B ** -0.5)
            sc = _bdot_nt(q, ck_ref[0, 0, h, m])
            sn = _bdot_nt(q, k_ref[:, cols])
            mx = jnp.maximum(jnp.max(sc, axis=1, keepdims=True), jnp.max(sn, axis=1, keepdims=True))
            ec = jnp.exp(sc - mx)
            en = jnp.exp(sn - mx)
            inv = (1.0 if m == 0 else -lam) / (jnp.sum(ec, axis=1, keepdims=True) + jnp.sum(en, axis=1, keepdims=True))
            parts.append((ec * inv, en * inv))
        wc = parts[0][0] + parts[1][0]
        wn = parts[0][1] + parts[1][1]
        o_ref[:, h * DV_B:(h + 1) * DV_B] = (_bdot(wc, cv_ref[0, 0, h])
                                             + _bdot(wn, v_ref[:, h * DV_B:(h + 1) * DV_B]))


QB = SEQ
NQB_DEC = DEC_SEQ // QB


def _attn_kernel(lam_init, q_ref, kp_ref, vp_ref, ks_ref, vs_ref, ck_ref, cv_ref, lam_ref, o_ref):
    i = pl.program_id(0)

    @pl.when(i < BATCH)
    def _():
        _attn_prompt_kernel(lam_init, q_ref, kp_ref, vp_ref, lam_ref, o_ref)

    @pl.when(i >= BATCH)
    def _():
        _attn_sample_kernel(lam_init, q_ref, ks_ref, vs_ref, ck_ref, cv_ref, lam_ref, o_ref)


def _diff_attention(q, k, v, cache_k, cache_v, lam_p, lam_init):
    blk = lambda rows, f: pl.BlockSpec((rows, QK_B), f)
    dec_b = lambda i: jnp.maximum(i - BATCH, 0) // NQB_DEC
    own = lambda i: (i, 0)
    prompt_kv = lambda i: (jnp.minimum(i, BATCH - 1), 0)
    dec_kv = lambda i: (TP // DEC_SEQ + dec_b(i), 0)
    return pl.pallas_call(
        functools.partial(_attn_kernel, lam_init),
        grid=(BATCH + DEC_BATCH * NQB_DEC,),
        in_specs=[
            blk(QB, own), blk(SEQ, prompt_kv), blk(SEQ, prompt_kv), blk(DEC_SEQ, dec_kv), blk(DEC_SEQ, dec_kv),
            pl.BlockSpec((1, 1, H_B, 2, PAST_LEN, DQK_B), lambda i: (dec_b(i), 0, 0, 0, 0, 0)),
            pl.BlockSpec((1, 1, H_B, PAST_LEN, DV_B), lambda i: (dec_b(i), 0, 0, 0, 0)),
            pl.BlockSpec((4, DQK_B), lambda i: (0, 0)),
        ],
        out_specs=blk(QB, own),
        out_shape=jax.ShapeDtypeStruct((T, V_B), F32),
        compiler_params=_params(("arbitrary",)),
        name="diff_attention",
    )(q, k, v, k, v, cache_k, cache_v, lam_p)


def _head_rms(x, g, nheads, width):
    return jnp.concatenate([_rms(x[:, h * width:(h + 1) * width], g) for h in range(nheads)], axis=1)


def _mix_out_kernel(lam_init, of_ref, ob_ref, r_ref, oatt_ref, xp_ref, xs_ref, mod_ref,
                    gg_ref, dg_ref, wo_ref, gp_ref, gffn_ref, o_ref, hp_ref):
    m = mod_ref[0, 0]
    o_a = _head_rms(of_ref[...] + ob_ref[...], gg_ref[...], H_A, DV_A) * _silu(r_ref[...])
    o_b = _head_rms(oatt_ref[...], dg_ref[...], H_B, DV_B) * (1.0 - lam_init)
    out = _bdot(o_a, wo_ref[0:V_A, :]) + _bdot(o_b, wo_ref[V_A:V_A + V_B, :])
    x1 = _stream_tile(xp_ref, xs_ref) + m[2:3] * _rms(out, gp_ref[...])
    o_ref[...] = x1
    _store_token_tiles(hp_ref, _ffn_input_rows(x1, m, gffn_ref[...]))


def _mix_out(lam_init, o_f, o_b, r_a, o_att, xp, xs, mod, l, gla_g, diff_g, w_o, g_post, g_ffn):
    tok = lambda width: pl.BlockSpec((TM, width), lambda i: (i, 0))
    vec = lambda width: pl.BlockSpec((1, width), lambda i: (0, 0))
    return pl.pallas_call(
        functools.partial(_mix_out_kernel, lam_init),
        grid=(T // TM,),
        in_specs=[
            tok(V_A), tok(V_A), tok(V_A), tok(V_B), *_stream_specs(),
            pl.BlockSpec((1, 1, 6, D), lambda i: (l, _mod_row(i), 0, 0)),
            vec(DV_A), vec(DV_B),
            pl.BlockSpec((V_A + V_B, D), lambda i: (0, 0)),
            vec(D), vec(D),
        ],
        out_specs=[tok(D), pl.BlockSpec((TM * ROWS_PER_TOKEN, LANES), lambda i: (i, 0))],
        out_shape=[jax.ShapeDtypeStruct((T, D), F32), jax.ShapeDtypeStruct((T * ROWS_PER_TOKEN, LANES), U32)],
        compiler_params=_params(("arbitrary",)),
        name="mixer_ab_out",
    )(o_f, o_b, r_a, o_att, xp, xs, mod, gla_g, diff_g, w_o, g_post, g_ffn)


def _gelu_tanh(x):
    return 0.5 * x * (1.0 + jnp.tanh(math.sqrt(2.0 / math.pi) * (x + 0.044715 * (x * x * x))))


def _sgu_kernel(xp_ref, xs_ref, mod_ref, gpre_ref, win_ref, bin_ref, vg_ref, ws_ref, bs_ref,
                wout_ref, gpost_ref, gffn_ref, o_ref, hp_ref, t_ref):
    m = mod_ref[0, 0]
    x = _stream_tile(xp_ref, xs_ref)
    h = _rms(x, gpre_ref[...]) * (1.0 + m[1:2]) + m[0:1]
    z = _gelu_tanh(_bdot(h, win_ref[...]) + bin_ref[...])
    v = _rms(z[:, SGU_DIM:], vg_ref[...])
    gw = SGU_DIM // SGU_GROUPS
    for ch in range(TM // SGU_CHUNK):
        rows = slice(ch * SGU_CHUNK, (ch + 1) * SGU_CHUNK)
        for g in range(SGU_GROUPS):
            cols = slice(g * gw, (g + 1) * gw)
            vs = _bdot(ws_ref[g], v[rows, cols]) + bs_ref[:, g:g + 1]
            t_ref[rows, cols] = (z[rows, cols] * vs).astype(BF16)
    out = jnp.dot(t_ref[...], wout_ref[...], preferred_element_type=F32)
    x1 = x + m[2:3] * _rms(out, gpost_ref[...])
    o_ref[...] = x1
    _store_token_tiles(hp_ref, _ffn_input_rows(x1, m, gffn_ref[...]))


def _sgu(xp, xs, mod, l, g_pre, w_in, b_in, v_g, w_s, b_s_t, w_out, g_post, g_ffn):
    tok = pl.BlockSpec((TM, D), lambda i: (i, 0))
    full = lambda *shape: pl.BlockSpec(shape, lambda i: (0,) * len(shape))
    return pl.pallas_call(
        _sgu_kernel,
        grid=(T // TM,),
        in_specs=_stream_specs() + [
            pl.BlockSpec((1, 1, 6, D), lambda i: (l, _mod_row(i), 0, 0)),
            full(1, D), full(D, 2 * SGU_DIM), full(1, 2 * SGU_DIM), full(1, SGU_DIM),
            full(SGU_GROUPS, SGU_CHUNK, SGU_CHUNK), full(SGU_CHUNK, SGU_GROUPS),
            full(SGU_DIM, D), full(1, D), full(1, D),
        ],
        out_specs=[tok, pl.BlockSpec((TM * ROWS_PER_TOKEN, LANES), lambda i: (i, 0))],
        out_shape=[jax.ShapeDtypeStruct((T, D), F32), jax.ShapeDtypeStruct((T * ROWS_PER_TOKEN, LANES), U32)],
        scratch_shapes=[pltpu.VMEM((TM, SGU_DIM), BF16)],
        compiler_params=_params(("arbitrary",)),
        name="sgu_mixer",
    )(xp, xs, mod, g_pre, w_in, b_in, v_g, w_s, b_s_t, w_out, g_post, g_ffn)


LANES = 128
U32 = jnp.uint32
PACKED = D // 2
ROWS_PER_TOKEN = PACKED // LANES
BF16_BITS = 16
HIGH_HALF = 0xFFFF0000


def _pack_rows(x):
    bits = lax.bitcast_convert_type(x.astype(BF16).astype(F32), U32)
    return bits[:, :PACKED] | (bits[:, PACKED:] >> BF16_BITS)


def _unpack_rows(u):
    return (lax.bitcast_convert_type(u & U32(HIGH_HALF), F32), lax.bitcast_convert_type(u << BF16_BITS, F32))


def _store_token_tiles(ref, u, first=0):
    n = u.shape[0]
    for c in range(ROWS_PER_TOKEN):
        ref[pl.ds(first * ROWS_PER_TOKEN + c, n, stride=ROWS_PER_TOKEN), :] = u[:, c * LANES:(c + 1) * LANES]


def _load_token_tiles(ref, n, first=0):
    return jnp.concatenate([ref[pl.ds(first * ROWS_PER_TOKEN + c, n, stride=ROWS_PER_TOKEN), :]
                            for c in range(ROWS_PER_TOKEN)], axis=1)


def _ffn_input_rows(x, m, g):
    return _pack_rows(_rms(x, g) * (1.0 + m[4:5]) + m[3:4])


def _router_kernel(hp_ref, wr_ref, eb_ref, te_ref, wn_ref, rk_ref, cnt_ref, carry_ref, upper_ref):
    i = pl.program_id(0)

    @pl.when(i == 0)
    def _():
        carry_ref[...] = jnp.zeros_like(carry_ref)
        tj = lax.broadcasted_iota(I32, (TR, TR), 0)
        ti = lax.broadcasted_iota(I32, (TR, TR), 1)
        upper_ref[...] = jnp.where(tj < ti, 1.0, 0.0).astype(BF16)

    h_hi, h_lo = (t.astype(BF16) for t in _unpack_rows(_load_token_tiles(hp_ref, TR)))
    w1, w2, _ = _split3(wr_ref[...])
    nt = lambda a, b: lax.dot_general(a, b, (((1,), (1,)), ((), ())), preferred_element_type=F32)
    logits = (nt(w1[:, :PACKED], h_hi) + nt(w1[:, PACKED:], h_lo)
              + nt(w2[:, :PACKED], h_hi) + nt(w2[:, PACKED:], h_lo))
    scores = jax.nn.sigmoid(logits)
    sel = scores + eb_ref[...]

    row8 = lax.broadcasted_iota(I32, (GROUP_SIZE, TR), 0)
    gscore = []
    for g in range(N_GROUPS):
        xg = sel[g * GROUP_SIZE:(g + 1) * GROUP_SIZE]
        m1 = jnp.max(xg, axis=0, keepdims=True)
        i1 = jnp.min(jnp.where(xg == m1, row8, GROUP_SIZE), axis=0, keepdims=True)
        m2 = jnp.max(jnp.where(row8 == i1, NEG_INF, xg), axis=0, keepdims=True)
        gscore.append(m1 + m2)
    pieces = []
    for g in range(N_GROUPS):
        rank = jnp.zeros((1, TR), I32)
        for g2 in range(N_GROUPS):
            if g2 == g:
                continue
            beats = (gscore[g2] >= gscore[g]) if g2 < g else (gscore[g2] > gscore[g])
            rank = rank + beats.astype(I32)
        pieces.append(jnp.where(rank < TOPK_GROUPS, sel[g * GROUP_SIZE:(g + 1) * GROUP_SIZE], NEG_INF))
    cur = jnp.concatenate(pieces, axis=0)

    row = lax.broadcasted_iota(I32, (N_EXPERTS, TR), 0)
    idxs, ws = [], []
    for _ in range(TOP_K):
        mx = jnp.max(cur, axis=0, keepdims=True)
        idx = jnp.min(jnp.where(cur == mx, row, N_EXPERTS), axis=0, keepdims=True)
        hit = row == idx
        ws.append(jnp.sum(jnp.where(hit, scores, 0.0), axis=0, keepdims=True))
        cur = jnp.where(hit, NEG_INF, cur)
        idxs.append(idx)
    mask = jnp.zeros((N_EXPERTS, TR), F32)
    for idx in idxs:
        mask = mask + (row == idx).astype(F32)
    wsum = ws[0]
    for wk in ws[1:]:
        wsum = wsum + wk

    pos = carry_ref[...] + jnp.dot(mask.astype(BF16), upper_ref[...], preferred_element_type=F32)
    for k in range(TOP_K):
        hit = row == idxs[k]
        te_ref[k:k + 1, :] = idxs[k]
        wn_ref[k:k + 1, :] = ws[k] / wsum * ROUTED_SCALE
        rk_ref[k:k + 1, :] = jnp.sum(jnp.where(hit, pos, 0.0), axis=0, keepdims=True).astype(I32)
    carry_ref[...] = carry_ref[...] + jnp.sum(mask, axis=1, keepdims=True)
    cnt_ref[...] = carry_ref[...]


def _router(hp, wr_t, e_bias):
    kt = lambda dtype: jax.ShapeDtypeStruct((TOP_K, T), dtype)
    kt_spec = pl.BlockSpec((TOP_K, TR), lambda i: (0, i))
    return pl.pallas_call(
        _router_kernel,
        grid=(T // TR,),
        in_specs=[
            pl.BlockSpec((TR * ROWS_PER_TOKEN, LANES), lambda i: (i, 0)),
            pl.BlockSpec((N_EXPERTS, D), lambda i: (0, 0)),
            pl.BlockSpec((N_EXPERTS, 1), lambda i: (0, 0)),
        ],
        out_specs=[
            kt_spec, kt_spec, kt_spec,
            pl.BlockSpec((N_EXPERTS, 1), lambda i: (0, 0)),
        ],
        out_shape=[
            kt(I32), kt(F32), kt(I32),
            jax.ShapeDtypeStruct((N_EXPERTS, 1), F32),
        ],
        scratch_shapes=[pltpu.VMEM((N_EXPERTS, 1), F32), pltpu.VMEM((TR, TR), BF16)],
        compiler_params=_params(("arbitrary",)),
        name="moe_router",
    )(hp, wr_t, e_bias)


_PAD_BITS = tuple(1 << b for b in range(GM_SUB.bit_length() - 1))


def _pad_fill_kernel(pad_start_ref, pad_len_ref, xg_in_ref, xg_ref, zero_ref, sem):
    del xg_in_ref
    zero_ref[...] = jnp.zeros_like(zero_ref)

    def pad_copies(e):
        start = pad_start_ref[e]
        n = pad_len_ref[e]
        copies = []
        for bit in _PAD_BITS:
            first = start + (n & ~(2 * bit - 1))
            copies.append(((n & bit) != 0, pltpu.make_async_copy(
                zero_ref.at[pl.ds(0, bit)], xg_ref.at[pl.ds(first, bit)], sem)))
        return copies

    def start_e(e, carry):
        for on, cp in pad_copies(e):
            @pl.when(on)
            def _():
                cp.start()
        return carry

    def wait_e(e, carry):
        for on, cp in pad_copies(e):
            @pl.when(on)
            def _():
                cp.wait()
        return carry

    lax.fori_loop(0, N_EXPERTS, start_e, 0)
    lax.fori_loop(0, N_EXPERTS, wait_e, 0)


def _pad_fill(pad_start, pad_len, xg):
    grid_spec = pltpu.PrefetchScalarGridSpec(
        num_scalar_prefetch=2,
        grid=(1,),
        in_specs=[pl.BlockSpec(memory_space=pl.ANY)],
        out_specs=pl.BlockSpec(memory_space=pl.ANY),
        scratch_shapes=[pltpu.VMEM((GM_SUB // 2, ROWS_PER_TOKEN, LANES), xg.dtype), pltpu.SemaphoreType.DMA],
    )
    return pl.pallas_call(
        _pad_fill_kernel,
        grid_spec=grid_spec,
        out_shape=jax.ShapeDtypeStruct(xg.shape, xg.dtype),
        input_output_aliases={2: 0},
        compiler_params=_params(("arbitrary",)),
        name="moe_pad_fill",
    )(pad_start, pad_len, xg)


SC_CORES, SC_SUBCORES = 2, 16
SC_WORKERS = SC_CORES * SC_SUBCORES
SC_W = 64


def _sc_worker_id():
    return lax.axis_index("s") * SC_CORES + lax.axis_index("c")


def _sc_dispatch(h3, slot3):
    nchunk = T // SC_WORKERS // SC_W
    mesh = plsc.VectorSubcoreMesh(core_axis_name="c", subcore_axis_name="s")
    tile = (SC_W, ROWS_PER_TOKEN, LANES)

    @functools.partial(
        pl.kernel, mesh=mesh,
        out_type=jax.ShapeDtypeStruct((SP, ROWS_PER_TOKEN, LANES), h3.dtype),
        scratch_types=[pltpu.VMEM((TOP_K, SC_W), I32), pltpu.VMEM((TOP_K, SC_W), I32),
                       pltpu.VMEM(tile, h3.dtype), pltpu.VMEM(tile, h3.dtype),
                       pltpu.SemaphoreType.DMA((2,)), pltpu.SemaphoreType.DMA((2,))],
    )
    def k(h_hbm, slot_hbm, xg_hbm, idx0, idx1, rows0, rows1, lsem, ssem):
        first = _sc_worker_id() * nchunk
        idx = (idx0, idx1)
        rows = (rows0, rows1)

        def loads(j, b):
            blk = first + j
            tok = pl.multiple_of(blk * SC_W, SC_W)
            return (pltpu.make_async_copy(slot_hbm.at[blk], idx[b], lsem.at[b]),
                    pltpu.make_async_copy(h_hbm.at[pl.ds(tok, SC_W)], rows[b], lsem.at[b]))

        def scatters(b):
            return [pltpu.make_async_copy(rows[b], xg_hbm.at[idx[b].at[kk]], ssem.at[b]) for kk in range(TOP_K)]

        for cp in loads(0, 0):
            cp.start()

        @pl.loop(0, nchunk, step=2)
        def _(j):
            for b in (0, 1):
                jj = j + b
                for cp in loads(jj, b):
                    cp.wait()
                for cp in scatters(b):
                    cp.start()

                @pl.when(jj + 1 < nchunk)
                def _():
                    @pl.when(jj >= 1)
                    def _():
                        for cp in scatters(1 - b):
                            cp.wait()
                    for cp in loads(jj + 1, 1 - b):
                        cp.start()

        for b in (0, 1):
            for cp in scatters(b):
                cp.wait()

    return k(h3, slot3)


NSUB = GM // GM_SUB
SUB_ROWS = GM_SUB * ROWS_PER_TOKEN


def _per_block_count(count, fn):
    for n in range(1, NSUB + 1):
        @pl.when(count == n)
        def _(n=n):
            fn(n)


def _gmm_kernel(layer, tile_e_ref, tile_blk_ref, tile_nsub_ref, tile_run_ref, tile_next_ref,
                xg_hbm, wg_hbm, wu_hbm, wd_hbm, yg_hbm,
                x_st, y_st, wg_st, wu_st, wd_st, wgu_scr, wd_scr, sems, xsems, ysems):
    j = pl.program_id(0)
    last_step = pl.num_programs(0) - 1
    nsub = tile_nsub_ref[j]
    run = tile_run_ref[j]
    slot = j % 2
    x_ref = x_st.at[slot]
    y_ref = y_st.at[slot]

    def tile_rows(t, n):
        return pl.ds(pl.multiple_of(tile_blk_ref[t] * SUB_ROWS, SUB_ROWS), n * SUB_ROWS)

    def x_copy(t, n):
        return pltpu.make_async_copy(xg_hbm.at[tile_rows(t, n)], x_st.at[t % 2, pl.ds(0, n * SUB_ROWS)], xsems.at[t % 2])

    def y_copy(t, n):
        return pltpu.make_async_copy(y_st.at[t % 2, pl.ds(0, n * SUB_ROWS)], yg_hbm.at[tile_rows(t, n)], ysems.at[t % 2])

    @pl.when(j == 0)
    def _():
        _per_block_count(nsub, lambda n: x_copy(j, n).start())

    _per_block_count(nsub, lambda n: x_copy(j, n).wait())

    @pl.when(j < last_step)
    def _():
        _per_block_count(tile_nsub_ref[j + 1], lambda n: x_copy(j + 1, n).start())

    @pl.when(j >= 2)
    def _():
        _per_block_count(tile_nsub_ref[j - 2], lambda n: y_copy(j - 2, n).wait())

    def weight_copies(e, slot):
        return [pltpu.make_async_copy(src.at[layer, e], dst.at[slot], sems.at[slot])
                for src, dst in ((wg_hbm, wg_st), (wu_hbm, wu_st), (wd_hbm, wd_st))]

    @pl.when(run >= 0)
    def _():
        @pl.when(j == 0)
        def _():
            for cp in weight_copies(tile_e_ref[j], run):
                cp.start()

        for cp in weight_copies(tile_e_ref[j], run):
            cp.wait()

        nxt = tile_next_ref[j]

        @pl.when(nxt >= 0)
        def _():
            for cp in weight_copies(nxt, 1 - run):
                cp.start()

        wgu_scr[:, 0:D_EXPERT] = wg_st[run].astype(BF16)
        wgu_scr[:, D_EXPERT:2 * D_EXPERT] = wu_st[run].astype(BF16)
        wd_scr[...] = wd_st[run].astype(BF16)

    def expert_mlp(s):
        x_hi, x_lo = _unpack_rows(_load_token_tiles(x_ref, GM_SUB, s * GM_SUB))
        gu = (jnp.dot(x_hi.astype(BF16), wgu_scr[0:PACKED, :], preferred_element_type=F32)
              + jnp.dot(x_lo.astype(BF16), wgu_scr[PACKED:D, :], preferred_element_type=F32))
        hid = _silu(gu[:, 0:D_EXPERT]) * gu[:, D_EXPERT:2 * D_EXPERT]
        y = jnp.dot(hid.astype(BF16), wd_scr[...], preferred_element_type=F32)
        _store_token_tiles(y_ref, _pack_rows(y), s * GM_SUB)

    def compute_and_send(n):
        for s in range(n):
            expert_mlp(s)
        y_copy(j, n).start()

    _per_block_count(nsub, compute_and_send)

    @pl.when(j == last_step)
    def _():
        _per_block_count(tile_nsub_ref[jnp.maximum(j - 1, 0)], lambda n: y_copy(j - 1, n).wait())
        _per_block_count(nsub, lambda n: y_copy(j, n).wait())


def _gmm(tile_e, tile_blk, tile_nsub, tile_run, tile_next, xg, l, w_gate, w_up, w_down):
    hbm = pl.BlockSpec(memory_space=pl.ANY)
    stage = pltpu.VMEM((2, GM * ROWS_PER_TOKEN, LANES), U32)
    grid_spec = pltpu.PrefetchScalarGridSpec(
        num_scalar_prefetch=5,
        grid=(NT_MAX,),
        in_specs=[hbm, hbm, hbm, hbm],
        out_specs=hbm,
        scratch_shapes=[stage, stage,
                        pltpu.VMEM((2, D, D_EXPERT), F32), pltpu.VMEM((2, D, D_EXPERT), F32),
                        pltpu.VMEM((2, D_EXPERT, D), F32),
                        pltpu.VMEM((D, 2 * D_EXPERT), BF16), pltpu.VMEM((D_EXPERT, D), BF16),
                        pltpu.SemaphoreType.DMA((2,)), pltpu.SemaphoreType.DMA((2,)), pltpu.SemaphoreType.DMA((2,))],
    )
    return pl.pallas_call(
        functools.partial(_gmm_kernel, l),
        grid_spec=grid_spec,
        out_shape=jax.ShapeDtypeStruct((SP * ROWS_PER_TOKEN, LANES), U32),
        compiler_params=_params(("arbitrary",)),
        name="moe_grouped_matmul",
    )(tile_e, tile_blk, tile_nsub, tile_run, tile_next, xg, w_gate, w_up, w_down)


def _sc_gather(table3, idx):
    n_idx = idx.shape[0]
    per_w = n_idx // SC_WORKERS
    nchunk = per_w // SC_W
    mesh = plsc.VectorSubcoreMesh(core_axis_name="c", subcore_axis_name="s")
    tile = (SC_W, ROWS_PER_TOKEN, LANES)

    @functools.partial(
        pl.kernel, mesh=mesh,
        out_type=jax.ShapeDtypeStruct((n_idx, ROWS_PER_TOKEN, LANES), table3.dtype),
        scratch_types=[pltpu.VMEM((per_w,), I32), pltpu.VMEM(tile, table3.dtype), pltpu.VMEM(tile, table3.dtype),
                       pltpu.SemaphoreType.DMA((2,)), pltpu.SemaphoreType.DMA((2,))],
    )
    def k(table_hbm, idx_hbm, out_hbm, idx_v, rows0, rows1, gsem, wsem):
        base = pl.multiple_of(_sc_worker_id() * per_w, per_w)
        rows = (rows0, rows1)
        pltpu.sync_copy(idx_hbm.at[pl.ds(base, per_w)], idx_v)

        def gather(j, b):
            ids = idx_v.at[pl.ds(pl.multiple_of(j * SC_W, SC_W), SC_W)]
            return pltpu.make_async_copy(table_hbm.at[ids], rows[b], gsem.at[b])

        def write(j, b):
            dst = out_hbm.at[pl.ds(pl.multiple_of(base + j * SC_W, SC_W), SC_W)]
            return pltpu.make_async_copy(rows[b], dst, wsem.at[b])

        gather(0, 0).start()

        @pl.loop(0, nchunk, step=2)
        def _(j):
            for b in (0, 1):
                jj = j + b
                gather(jj, b).wait()
                write(jj, b).start()

                @pl.when(jj + 1 < nchunk)
                def _():
                    @pl.when(jj >= 1)
                    def _():
                        write(jj - 1, 1 - b).wait()
                    gather(jj + 1, 1 - b).start()

        write(nchunk - 2, 0).wait()
        write(nchunk - 1, 1).wait()

    return k(table3, idx)


def _combine_kernel(wn_ref, x_ref, mod_ref, gpre_ref, gp_ref, wsg_ref, wsu_ref, wsd_ref, y_ref, o_ref, eye_ref):
    @pl.when(pl.program_id(0) == 0)
    def _():
        r = lax.broadcasted_iota(I32, (TD, TD), 0)
        c = lax.broadcasted_iota(I32, (TD, TD), 1)
        eye_ref[...] = jnp.where(r == c, 1.0, 0.0).astype(BF16)

    m = mod_ref[0, 0]
    x = x_ref[...]
    hb = (_rms(x, gpre_ref[...]) * (1.0 + m[4:5]) + m[3:4]).astype(BF16)
    hid = (_silu(jnp.dot(hb, wsg_ref[...], preferred_element_type=F32))
           * jnp.dot(hb, wsu_ref[...], preferred_element_type=F32))
    acc = jnp.dot(hid.astype(BF16), wsd_ref[...], preferred_element_type=F32)

    eye = eye_ref[...]
    nt = lambda a, b: lax.dot_general(a, b, (((1,), (1,)), ((), ())), preferred_element_type=F32)
    w1, w2, w3 = _split3(wn_ref[...])
    w_t = nt(eye, w1) + nt(eye, w2) + nt(eye, w3)

    acc_hi = acc[:, :PACKED]
    acc_lo = acc[:, PACKED:]
    for k in range(TOP_K):
        y_hi, y_lo = _unpack_rows(_load_token_tiles(y_ref, TD, k * TD))
        acc_hi = acc_hi + y_hi * w_t[:, k:k + 1]
        acc_lo = acc_lo + y_lo * w_t[:, k:k + 1]
    acc = jnp.concatenate([acc_hi, acc_lo], axis=1)
    o_ref[...] = x + m[5:6] * _rms(acc, gp_ref[...])


def _combine(wn, x, mod, l, g_pre, g_post, ws_gate, ws_up, ws_down, ybuf, first_tok, n_tok):
    off = first_tok // TD
    tiles_per_dec = DEC_SEQ // TD
    npd = TP // TD
    mod_row = lambda i: jnp.where(i + off < npd, 0, 1 + (i + off - npd) // tiles_per_dec)
    full = lambda *shape: pl.BlockSpec(shape, lambda i: (0,) * len(shape))
    y_spec = pl.BlockSpec((TOP_K * TD * ROWS_PER_TOKEN, LANES), lambda i: (i, 0))
    return pl.pallas_call(
        _combine_kernel,
        grid=(n_tok // TD,),
        in_specs=[
            pl.BlockSpec((TOP_K, TD), lambda i: (0, i + off)),
            pl.BlockSpec((TD, D), lambda i: (i + off, 0)),
            pl.BlockSpec((1, 1, 6, D), lambda i: (l, mod_row(i), 0, 0)),
            full(1, D), full(1, D), full(D, D_SHARED), full(D, D_SHARED), full(D_SHARED, D),
            y_spec,
        ],
        out_specs=pl.BlockSpec((TD, D), lambda i: (i, 0)),
        out_shape=jax.ShapeDtypeStruct((n_tok, D), F32),
        scratch_shapes=[pltpu.VMEM((TD, TD), BF16)],
        compiler_params=_params(("arbitrary",)),
        name="moe_combine",
    )(wn, x, mod, g_pre, g_post, ws_gate, ws_up, ws_down, ybuf)


def _moe_layer(x, h, mod, l, g_pre, g_post, w_router, e_bias, w_gate, w_up, w_down,
               ws_gate, ws_up, ws_down):
    top_e, wn, rk, cnt = _router(h, w_router.T, e_bias.reshape(N_EXPERTS, 1))
    cnt = cnt.reshape(N_EXPERTS).astype(I32)
    padded = (cnt + GM_SUB - 1) // GM_SUB * GM_SUB
    ends = jnp.cumsum(padded)
    offs = ends - padded
    eid = jnp.arange(N_EXPERTS, dtype=I32)[:, None, None]
    slot = rk + jnp.sum(jnp.where(top_e[None] == eid, offs[:, None, None], 0), axis=0)
    ntile = (cnt + GM - 1) // GM
    tile_ends = jnp.cumsum(ntile)
    tid = jnp.arange(NT_MAX, dtype=I32)
    last = jnp.maximum(tile_ends[-1] - 1, 0)
    tile_e = jnp.minimum(jnp.sum((tid[:, None] >= tile_ends[None, :]).astype(I32), axis=1), N_EXPERTS - 1)
    tile_e = jnp.where(tid <= last, tile_e, tile_e[last])
    own = tile_e[:, None] == jnp.arange(N_EXPERTS, dtype=I32)[None, :]
    pick = lambda per_expert: jnp.sum(jnp.where(own, per_expert[None, :], 0), axis=1)
    t_in = tid - pick(tile_ends - ntile)
    tile_blk = jnp.where(tid <= last, (pick(offs) + GM * t_in) // GM_SUB, 0)
    tile_rows = jnp.clip(pick(cnt) - GM * t_in, 0, GM)
    tile_nsub = jnp.where(tid <= last, (tile_rows + GM_SUB - 1) // GM_SUB, 0)
    starts = jnp.logical_and(jnp.arange(NT_MAX) <= last,
                             jnp.concatenate([jnp.ones((1,), bool), tile_e[1:] != tile_e[:-1]]))
    tile_run = jnp.where(starts, (jnp.cumsum(starts.astype(I32)) - 1) % 2, -1)
    later = jnp.logical_and(jnp.arange(N_EXPERTS, dtype=I32)[None, :] > tile_e[:, None], (cnt > 0)[None, :])
    tile_next = jnp.min(jnp.where(later, jnp.arange(N_EXPERTS, dtype=I32)[None, :], N_EXPERTS), axis=1)
    tile_next = jnp.where(tile_next < N_EXPERTS, tile_next, -1)
    slot3 = slot.reshape(TOP_K, T // SC_W, SC_W).transpose(1, 0, 2)
    xg = _sc_dispatch(h.reshape(T, ROWS_PER_TOKEN, LANES), slot3)
    xg = _pad_fill(offs + cnt, padded - cnt, xg).reshape(SP * ROWS_PER_TOKEN, LANES)
    yg = _gmm(tile_e, tile_blk, tile_nsub.astype(I32), tile_run.astype(I32), tile_next.astype(I32),
              xg, l, w_gate, w_up, w_down)
    yg3 = yg.reshape(SP, ROWS_PER_TOKEN, LANES)
    ws = (ws_gate.astype(BF16), ws_up.astype(BF16), ws_down.astype(BF16))
    outs = []
    for first_tok, n_tok in ((0, TP), (TP, TS)):
        ids = slot[:, first_tok:first_tok + n_tok].reshape(TOP_K, n_tok // TD, TD).transpose(1, 0, 2)
        ybuf = _sc_gather(yg3, ids.reshape(TOP_K * n_tok))
        outs.append(_combine(wn, x, mod, l, g_pre, g_post, *ws,
                             ybuf.reshape(TOP_K * n_tok * ROWS_PER_TOKEN, LANES), first_tok, n_tok))
    return outs


def _rope_tables():
    n = DEC_SEQ
    rows = n // GRID_W
    row = jnp.repeat(jnp.arange(rows), GRID_W).astype(F32)
    col = jnp.tile(jnp.arange(GRID_W), rows).astype(F32)
    half = DQK_B // 2
    inv = ROPE_BASE ** (-jnp.arange(0, half, 2, dtype=F32) / half)
    ang_r = row[:, None] * inv
    ang_c = col[:, None] * inv
    ang = jnp.concatenate([ang_r, ang_r, ang_c, ang_c], axis=-1)
    reps = QK_B // DQK_B
    return jnp.tile(jnp.cos(ang), (1, reps)), jnp.tile(jnp.sin(ang), (1, reps))


def _pad_in_proj(w):
    s = [0, Q_A, 2 * Q_A, 2 * Q_A + V_A, 2 * Q_A + 2 * V_A]
    s += [s[-1] + GATE_RANK, s[-1] + 2 * GATE_RANK]
    s += [s[-1] + QK_B, s[-1] + 2 * QK_B, s[-1] + 2 * QK_B + V_B]
    gates = jnp.pad(w[:, s[4]:s[6]], ((0, 0), (0, GL_PAD - 2 * GATE_RANK)))
    return jnp.concatenate([w[:, s[0]:s[4]], gates, w[:, s[6]:s[9]]], axis=1).astype(BF16)


def kernel(x_prompt, x_sample, c, c_ctx, state_gla, cache_k, cache_v, ada_w, ada_b, norm_pre_mix, norm_post_mix, norm_pre_ffn, norm_post_ffn, ab_w_in, gla_w_g2, gla_b_g2, gla_norm_g, diff_lambda, diff_norm_g, ab_w_out, sgu_w_in, sgu_b_in, sgu_norm_g, sgu_w_s, sgu_b_s, sgu_w_out, moe_w_router, moe_e_bias, moe_w_gate, moe_w_up, moe_w_down, moe_ws_gate, moe_ws_up, moe_ws_down):
    depth = ada_w.shape[0]
    assert x_prompt.shape == (BATCH, SEQ, D) and x_sample.shape == (DEC_BATCH, DEC_SEQ, D)
    assert state_gla.shape == (DEC_BATCH, (depth + 1) // 2, 2, H_A, DK_A, DV_A)
    assert cache_k.shape == (DEC_BATCH, 1, H_B, 2, PAST_LEN, DQK_B) and cache_v.shape == (DEC_BATCH, 1, H_B, PAST_LEN, DV_B)
    assert depth == 2 and moe_w_gate.shape == (depth, N_EXPERTS, D, D_EXPERT)
    xp, xs = x_prompt.reshape(TP, D), x_sample.reshape(TS, D)
    cond = jnp.concatenate([c_ctx[None, :], c, jnp.zeros((8 - 1 - DEC_BATCH, D), F32)], axis=0)
    mod = _modulation(cond, ada_w, ada_b)
    cos, sin = _rope_tables()
    vec = lambda a: a.reshape(1, -1)
    new_s = new_k = new_v = None
    for l in range(depth):
        if l % 2 == 0:
            e = l // 2
            lam_init = 0.8 - 0.6 * math.exp(-0.3 * l)
            a, r_a, gl, q_b, k_b, v_b, new_k, new_v = _in_proj(xp, xs, mod, l, vec(norm_pre_mix[l]),
                                                               _pad_in_proj(ab_w_in[e]), cos, sin)
            s0_t = jnp.swapaxes(state_gla[:, e], -1, -2)
            same_head = jnp.eye(H_A, dtype=bool)[None, None, :, None, :, None]
            s0_t = jnp.where(same_head, s0_t[:, :, :, :, None, :], 0.0).reshape(DEC_BATCH, 2, V_A, Q_A)
            o_f, o_bw, s_fin_t = _gla(a, gl, gla_w_g2[e], gla_b_g2[e].reshape(2, 1, Q_A), s0_t)
            o_att = _diff_attention(q_b, k_b, v_b, cache_k, cache_v, diff_lambda[e], lam_init)
            x, h = _mix_out(lam_init, o_f, o_bw, r_a, o_att, xp, xs, mod, l, vec(gla_norm_g[e]), vec(diff_norm_g[e]),
                            ab_w_out[e].astype(BF16), vec(norm_post_mix[l]), vec(norm_pre_ffn[l]))
            new_s = jnp.swapaxes(s_fin_t, -1, -2)[:, None]
        else:
            o = l // 2
            x, h = _sgu(xp, xs, mod, l, vec(norm_pre_mix[l]), sgu_w_in[o].astype(BF16), vec(sgu_b_in[o]),
                        vec(sgu_norm_g[o]), sgu_w_s[o], sgu_b_s[o].T, sgu_w_out[o].astype(BF16),
                        vec(norm_post_mix[l]), vec(norm_pre_ffn[l]))
        xp, xs = _moe_layer(x, h, mod, l, vec(norm_pre_ffn[l]), vec(norm_post_ffn[l]), moe_w_router[l], moe_e_bias[l],
                            moe_w_gate, moe_w_up, moe_w_down, moe_ws_gate[l], moe_ws_up[l], moe_ws_down[l])
    y_prompt = xp.reshape(BATCH, SEQ, D)
    y_sample = xs.reshape(DEC_BATCH, DEC_SEQ, D)
    return (y_prompt, y_sample, new_s, new_k, new_v)
```

```python
import functools
import math

import jax
import jax.numpy as jnp
from jax import lax
from jax.experimental import pallas as pl
from jax.experimental.pallas import tpu as pltpu
from jax.experimental.pallas import tpu_sc as plsc

F32 = jnp.float32
BF16 = jnp.bfloat16
I32 = jnp.int32

D = 1024
BATCH, SEQ = 32, 256
DEC_BATCH, DEC_SEQ = 4, 2048
PAST_LEN = 256
GRID_W = 64
EPS = 1e-6
TP = BATCH * SEQ
TS = DEC_BATCH * DEC_SEQ
T = TP + TS
H_A, DK_A, DV_A = 4, 64, 128
Q_A, V_A = H_A * DK_A, H_A * DV_A
GATE_RANK, GATE_TAU, GLA_CHUNK = 16, 16.0, 64
H_B, DQK_B, DV_B = 4, 64, 128
QK_B, V_B = H_B * 2 * DQK_B, H_B * DV_B
ROPE_BASE = 10000.0
SGU_DIM, SGU_GROUPS, SGU_CHUNK = 1024, 4, 128
N_EXPERTS, TOP_K, N_GROUPS, TOPK_GROUPS = 64, 8, 8, 4
GROUP_SIZE = N_EXPERTS // N_GROUPS
D_EXPERT, D_SHARED = 256, 256
ROUTED_SCALE = 2.5

TM = 512
TM_SUB = 256
NPT = TP // TM
TILES_PER_DEC = DEC_SEQ // TM
SEG = 256
NSEG = T // SEG
NSEG_P = TP // SEG
SEG_PER_DEC = DEC_SEQ // SEG
TR = 512
TD = 512
GM = 2048
GM_SUB = 256
NT_MAX = T * TOP_K // GM + N_EXPERTS
SP = T * TOP_K + N_EXPERTS * GM_SUB
GL_PAD = 128
VMEM_LIMIT = 56 * 1024 * 1024
NEG_INF = float("-inf")


def _bdot(a, b):
    return jnp.dot(a.astype(BF16), b.astype(BF16), preferred_element_type=F32)


def _bdot_nt(a, b):
    return lax.dot_general(a.astype(BF16), b.astype(BF16), (((1,), (1,)), ((), ())),
                           preferred_element_type=F32)


def _bdot_tn(a, b):
    return lax.dot_general(a.astype(BF16), b.astype(BF16), (((0,), (0,)), ((), ())),
                           preferred_element_type=F32)


def _split3(x):
    x1 = x.astype(BF16)
    r1 = x - x1.astype(F32)
    x2 = r1.astype(BF16)
    x3 = (r1 - x2.astype(F32)).astype(BF16)
    return x1, x2, x3


def _rms(x, g):
    return x * lax.rsqrt(jnp.mean(x * x, axis=-1, keepdims=True) + EPS) * g


def _silu(x):
    return x * jax.nn.sigmoid(x)


def _mod_row(i):
    return jnp.where(i < NPT, 0, 1 + (i - NPT) // TILES_PER_DEC)


def _params(sem, limit=VMEM_LIMIT):
    return pltpu.CompilerParams(dimension_semantics=sem, vmem_limit_bytes=limit)


def _mod_kernel(c_ref, w_ref, b_ref, o_ref):
    o_ref[0] = _bdot(_silu(c_ref[...]), w_ref[0]) + b_ref[0]


def _modulation(cond, ada_w, ada_b):
    depth = ada_w.shape[0]
    nj = 6
    out = pl.pallas_call(
        _mod_kernel,
        grid=(depth, nj),
        in_specs=[
            pl.BlockSpec((8, D), lambda l, j: (0, 0)),
            pl.BlockSpec((1, D, D), lambda l, j: (l, 0, j)),
            pl.BlockSpec((1, 1, D), lambda l, j: (l, 0, j)),
        ],
        out_specs=pl.BlockSpec((1, 8, D), lambda l, j: (l, 0, j)),
        out_shape=jax.ShapeDtypeStruct((depth, 8, 6 * D), F32),
        compiler_params=_params(("arbitrary", "arbitrary")),
        name="adaln_modulation",
    )(cond, ada_w, ada_b.reshape(depth, 1, 6 * D))
    return out.reshape(depth, 8, 6, D)


_C_A = 0
_C_R = _C_A + 2 * Q_A + V_A
_C_GL = _C_R + V_A
_C_Q = _C_GL + GL_PAD
_C_K = _C_Q + QK_B
_C_V = _C_K + QK_B
_C_END = _C_V + V_B
ROT_PAIR = DQK_B // 4


def _rope(x, cos, sin):
    lane = lax.broadcasted_iota(I32, x.shape, 1)
    first = (lane % (2 * ROT_PAIR)) < ROT_PAIR
    n = x.shape[1]
    xr = jnp.where(first, -pltpu.roll(x, n - ROT_PAIR, 1), pltpu.roll(x, ROT_PAIR, 1))
    return x * cos + xr * sin


def _stream_specs():
    return [pl.BlockSpec((TM, D), lambda i: (jnp.minimum(i, NPT - 1), 0)),
            pl.BlockSpec((TM, D), lambda i: (jnp.maximum(i - NPT, 0), 0))]


def _stream_tile(xp_ref, xs_ref):
    return jnp.where(pl.program_id(0) < NPT, xp_ref[...], xs_ref[...])


def _in_kernel(xp_ref, xs_ref, mod_ref, g_ref, w_ref, cos_ref, sin_ref,
               a_ref, r_ref, gl_ref, q_ref, k_ref, v_ref, ck_ref, cv_ref):
    latent = pl.program_id(0) >= NPT
    m = mod_ref[0, 0]
    x = _stream_tile(xp_ref, xs_ref)
    for s in range(TM // TM_SUB):
        rows = slice(s * TM_SUB, (s + 1) * TM_SUB)
        h = _rms(x[rows], g_ref[...]) * (1.0 + m[1:2]) + m[0:1]
        hb = h.astype(BF16)

        def proj(c0, c1):
            return jnp.dot(hb, w_ref[:, c0:c1], preferred_element_type=F32)

        a_ref[rows, :] = proj(_C_A, _C_R)
        r_ref[rows, :] = proj(_C_R, _C_GL)
        gl_ref[rows, :] = proj(_C_GL, _C_Q)
        v_ref[rows, :] = proj(_C_V, _C_END)
        q = proj(_C_Q, _C_K)
        k = proj(_C_K, _C_V)
        cos = cos_ref[rows, :]
        sin = sin_ref[rows, :]
        q_ref[rows, :] = jnp.where(latent, _rope(q, cos, sin), q)
        k_ref[rows, :] = jnp.where(latent, _rope(k, cos, sin), k)

    @pl.when(jnp.logical_not(latent))
    def _():
        for s in range(TM // SEQ):
            rows = slice(s * SEQ, (s + 1) * SEQ)
            for h in range(H_B):
                cv_ref[s, 0, h] = v_ref[rows, h * DV_B:(h + 1) * DV_B]
                for mp in range(2):
                    ck_ref[s, 0, h, mp] = k_ref[rows, (2 * h + mp) * DQK_B:(2 * h + mp + 1) * DQK_B]


def _in_proj(xp, xs, mod, l, g, w_pad, cos, sin):
    tok = lambda width: pl.BlockSpec((TM, width), lambda i: (i, 0))
    rope_spec = pl.BlockSpec((TM, QK_B), lambda i: (jnp.maximum(i - NPT, 0) % TILES_PER_DEC, 0))
    widths = (2 * Q_A + V_A, V_A, GL_PAD, QK_B, QK_B, V_B)
    seqs = TM // SEQ
    prompt_tile = lambda i: jnp.minimum(i, NPT - 1)
    cache_specs = [pl.BlockSpec((seqs, 1, H_B, 2, SEQ, DQK_B), lambda i: (prompt_tile(i), 0, 0, 0, 0, 0)),
                   pl.BlockSpec((seqs, 1, H_B, SEQ, DV_B), lambda i: (prompt_tile(i), 0, 0, 0, 0))]
    cache_shapes = [jax.ShapeDtypeStruct((BATCH, 1, H_B, 2, SEQ, DQK_B), F32),
                    jax.ShapeDtypeStruct((BATCH, 1, H_B, SEQ, DV_B), F32)]
    return pl.pallas_call(
        _in_kernel,
        grid=(T // TM,),
        in_specs=_stream_specs() + [
            pl.BlockSpec((1, 1, 6, D), lambda i: (l, _mod_row(i), 0, 0)),
            pl.BlockSpec((1, D), lambda i: (0, 0)),
            pl.BlockSpec((D, _C_END), lambda i: (0, 0)),
            rope_spec, rope_spec,
        ],
        out_specs=[tok(w) for w in widths] + cache_specs,
        out_shape=[jax.ShapeDtypeStruct((T, w), F32) for w in widths] + cache_shapes,
        compiler_params=_params(("arbitrary",)),
        name="mixer_ab_in_proj",
    )(xp, xs, mod, g, w_pad, cos, sin)


def _log_sigmoid(x):
    return jnp.minimum(x, 0.0) - jnp.log(1.0 + jnp.exp(-jnp.abs(x)))


def _gla_kernel(af_ref, ab_ref, glf_ref, glb_ref, wg_ref, bg_ref, s0_ref,
                of_ref, ob_ref, sfin_ref, st_ref):
    i = pl.program_id(0)

    @pl.when(i < NSEG_P)
    def _():
        st_ref[...] = jnp.zeros_like(st_ref)

    @pl.when(jnp.logical_and(i >= NSEG_P, (i - NSEG_P) % SEG_PER_DEC == 0))
    def _():
        st_ref[...] = s0_ref[0]

    r = lax.broadcasted_iota(I32, (SEG, SEG), 0)
    c = lax.broadcasted_iota(I32, (SEG, SEG), 1)
    same = (r // GLA_CHUNK) == (c // GLA_CHUNK)
    nchunk = SEG // GLA_CHUNK
    own_head = (lax.broadcasted_iota(I32, (V_A, Q_A), 0) // DV_A) == (lax.broadcasted_iota(I32, (V_A, Q_A), 1) // DK_A)

    for d, (a_ref, gl_ref, o_ref) in enumerate(((af_ref, glf_ref, of_ref), (ab_ref, glb_ref, ob_ref))):
        fwd = d == 0
        gcol = gl_ref[:, d * GATE_RANK:(d + 1) * GATE_RANK]
        la = _log_sigmoid(_bdot(gcol, wg_ref[d]) + bg_ref[d]) / GATE_TAU
        causal = jnp.logical_and(same, (c <= r) if fwd else (c >= r))
        tri = jnp.where(causal, 1.0, 0.0).astype(BF16)
        l1, l2, l3 = _split3(la)
        b_all = (jnp.dot(tri, l1, preferred_element_type=F32)
                 + jnp.dot(tri, l2, preferred_element_type=F32)
                 + jnp.dot(tri, l3, preferred_element_type=F32))
        q_in_all = a_ref[:, 0:Q_A] * (DK_A ** -0.5) * jnp.exp(b_all)
        kd_all = a_ref[:, Q_A:2 * Q_A] * jnp.exp(-b_all)
        intra = []
        for h in range(H_A):
            kc = slice(h * DK_A, (h + 1) * DK_A)
            attn = jnp.where(causal, _bdot_nt(q_in_all[:, kc], kd_all[:, kc]), 0.0)
            intra.append(_bdot(attn, a_ref[:, 2 * Q_A + h * DV_A:2 * Q_A + (h + 1) * DV_A]))
        intra = jnp.concatenate(intra, axis=1)
        state = st_ref[d]
        order = range(nchunk) if fwd else range(nchunk - 1, -1, -1)
        for ch in order:
            r0 = ch * GLA_CHUNK
            rows = slice(r0, r0 + GLA_CHUNK)
            end = r0 + GLA_CHUNK - 1 if fwd else r0
            b_end = b_all[end:end + 1, :]
            kw = a_ref[rows, Q_A:2 * Q_A] * jnp.exp(b_end - b_all[rows, :])
            o_ref[rows, :] = intra[rows, :] + _bdot_nt(q_in_all[rows, :], state)
            kv_t = _bdot_tn(a_ref[rows, 2 * Q_A:2 * Q_A + V_A], kw)
            state = state * jnp.exp(b_end) + jnp.where(own_head, kv_t, 0.0)
        st_ref[d] = state

    @pl.when(i < NSEG_P)
    def _():
        for d in range(2):
            for h in range(H_A):
                sfin_ref[0, d, h] = st_ref[d, h * DV_A:(h + 1) * DV_A, h * DK_A:(h + 1) * DK_A]


def _seg_bwd(i):
    j = i - NSEG_P
    return jnp.where(i < NSEG_P, i, NSEG_P + (j // SEG_PER_DEC) * SEG_PER_DEC + (SEG_PER_DEC - 1 - j % SEG_PER_DEC))


def _gla(a, gl, wg, bg, s0_t):
    seg = lambda width, f: pl.BlockSpec((SEG, width), lambda i: (f(i), 0))
    ident = lambda i: i
    st_block = (1, 2, H_A, DV_A, DK_A)
    return pl.pallas_call(
        _gla_kernel,
        grid=(NSEG,),
        in_specs=[
            seg(D, ident), seg(D, _seg_bwd), seg(GL_PAD, ident), seg(GL_PAD, _seg_bwd),
            pl.BlockSpec((2, GATE_RANK, Q_A), lambda i: (0, 0, 0)),
            pl.BlockSpec((2, 1, Q_A), lambda i: (0, 0, 0)),
            pl.BlockSpec((1, 2, V_A, Q_A), lambda i: (jnp.maximum(i - NSEG_P, 0) // SEG_PER_DEC, 0, 0, 0)),
        ],
        out_specs=[
            seg(V_A, ident), seg(V_A, _seg_bwd),
            pl.BlockSpec(st_block, lambda i: (jnp.minimum(i, NSEG_P - 1), 0, 0, 0, 0)),
        ],
        out_shape=[
            jax.ShapeDtypeStruct((T, V_A), F32),
            jax.ShapeDtypeStruct((T, V_A), F32),
            jax.ShapeDtypeStruct((BATCH, 2, H_A, DV_A, DK_A), F32),
        ],
        scratch_shapes=[pltpu.VMEM((2, V_A, Q_A), F32)],
        compiler_params=_params(("arbitrary",)),
        name="gla_bidir",
    )(a, a, gl, gl, wg, bg, s0_t)


def _diff_lambda(lam_ref, lam_init):
    lp = lam_ref[...]
    s01 = jnp.sum(lp[0:1] * lp[1:2], axis=1, keepdims=True)
    s23 = jnp.sum(lp[2:3] * lp[3:4], axis=1, keepdims=True)
    return jnp.exp(s01) - jnp.exp(s23) + lam_init


def _attn_prompt_kernel(lam_init, q_ref, k_ref, v_ref, lam_ref, o_ref):
    lam = _diff_lambda(lam_ref, lam_init)
    for h in range(H_B):
        ps = []
        for m in range(2):
            cols = slice((2 * h + m) * DQK_B, (2 * h + m + 1) * DQK_B)
            s = _bdot_nt(q_ref[:, cols] * (DQK_B ** -0.5), k_ref[:, cols])
            e = jnp.exp(s - jnp.max(s, axis=1, keepdims=True))
            ps.append(e * (1.0 / jnp.sum(e, axis=1, keepdims=True)))
        w = ps[0] - lam * ps[1]
        o_ref[:, h * DV_B:(h + 1) * DV_B] = _bdot(w, v_ref[:, h * DV_B:(h + 1) * DV_B])


def _attn_sample_kernel(lam_init, q_ref, k_ref, v_ref, ck_ref, cv_ref, lam_ref, o_ref):
    lam = _diff_lambda(lam_ref, lam_init)
    for h in range(H_B):
        parts = []
        for m in range(2):
            cols = slice((2 * h + m) * DQK_B, (2 * h + m + 1) * DQK_B)
            q = q_ref[:, cols] * (DQK_B ** -0.5)
            sc = _bdot_nt(q, ck_ref[0, 0, h, m])
            sn = _bdot_nt(q, k_ref[:, cols])
            mx = jnp.maximum(jnp.max(sc, axis=1, keepdims=True), jnp.max(sn, axis=1, keepdims=True))
            ec = jnp.exp(sc - mx)
            en = jnp.exp(sn - mx)
            inv = (1.0 if m == 0 else -lam) / (jnp.sum(ec, axis=1, keepdims=True) + jnp.sum(en, axis=1, keepdims=True))
            parts.append((ec * inv, en * inv))
        wc = parts[0][0] + parts[1][0]
        wn = parts[0][1] + parts[1][1]
        o_ref[:, h * DV_B:(h + 1) * DV_B] = (_bdot(wc, cv_ref[0, 0, h])
                                             + _bdot(wn, v_ref[:, h * DV_B:(h + 1) * DV_B]))


QB = SEQ
NQB_DEC = DEC_SEQ // QB


def _attn_kernel(lam_init, q_ref, kp_ref, vp_ref, ks_ref, vs_ref, ck_ref, cv_ref, lam_ref, o_ref):
    i = pl.program_id(0)

    @pl.when(i < BATCH)
    def _():
        _attn_prompt_kernel(lam_init, q_ref, kp_ref, vp_ref, lam_ref, o_ref)

    @pl.when(i >= BATCH)
    def _():
        _attn_sample_kernel(lam_init, q_ref, ks_ref, vs_ref, ck_ref, cv_ref, lam_ref, o_ref)


def _diff_attention(q, k, v, cache_k, cache_v, lam_p, lam_init):
    blk = lambda rows, f: pl.BlockSpec((rows, QK_B), f)
    dec_b = lambda i: jnp.maximum(i - BATCH, 0) // NQB_DEC
    own = lambda i: (i, 0)
    prompt_kv = lambda i: (jnp.minimum(i, BATCH - 1), 0)
    dec_kv = lambda i: (TP // DEC_SEQ + dec_b(i), 0)
    return pl.pallas_call(
        functools.partial(_attn_kernel, lam_init),
        grid=(BATCH + DEC_BATCH * NQB_DEC,),
        in_specs=[
            blk(QB, own), blk(SEQ, prompt_kv), blk(SEQ, prompt_kv), blk(DEC_SEQ, dec_kv), blk(DEC_SEQ, dec_kv),
            pl.BlockSpec((1, 1, H_B, 2, PAST_LEN, DQK_B), lambda i: (dec_b(i), 0, 0, 0, 0, 0)),
            pl.BlockSpec((1, 1, H_B, PAST_LEN, DV_B), lambda i: (dec_b(i), 0, 0, 0, 0)),
            pl.BlockSpec((4, DQK_B), lambda i: (0, 0)),
        ],
        out_specs=blk(QB, own),
        out_shape=jax.ShapeDtypeStruct((T, V_B), F32),
        compiler_params=_params(("arbitrary",)),
        name="diff_attention",
    )(q, k, v, k, v, cache_k, cache_v, lam_p)


def _head_rms(x, g, nheads, width):
    return jnp.concatenate([_rms(x[:, h * width:(h + 1) * width], g) for h in range(nheads)], axis=1)


def _mix_out_kernel(lam_init, of_ref, ob_ref, r_ref, oatt_ref, xp_ref, xs_ref, mod_ref,
                    gg_ref, dg_ref, wo_ref, gp_ref, gffn_ref, o_ref, hp_ref):
    m = mod_ref[0, 0]
    o_a = _head_rms(of_ref[...] + ob_ref[...], gg_ref[...], H_A, DV_A) * _silu(r_ref[...])
    o_b = _head_rms(oatt_ref[...], dg_ref[...], H_B, DV_B) * (1.0 - lam_init)
    out = _bdot(o_a, wo_ref[0:V_A, :]) + _bdot(o_b, wo_ref[V_A:V_A + V_B, :])
    x1 = _stream_tile(xp_ref, xs_ref) + m[2:3] * _rms(out, gp_ref[...])
    o_ref[...] = x1
    _store_token_tiles(hp_ref, _ffn_input_rows(x1, m, gffn_ref[...]))


def _mix_out(lam_init, o_f, o_b, r_a, o_att, xp, xs, mod, l, gla_g, diff_g, w_o, g_post, g_ffn):
    tok = lambda width: pl.BlockSpec((TM, width), lambda i: (i, 0))
    vec = lambda width: pl.BlockSpec((1, width), lambda i: (0, 0))
    return pl.pallas_call(
        functools.partial(_mix_out_kernel, lam_init),
        grid=(T // TM,),
        in_specs=[
            tok(V_A), tok(V_A), tok(V_A), tok(V_B), *_stream_specs(),
            pl.BlockSpec((1, 1, 6, D), lambda i: (l, _mod_row(i), 0, 0)),
            vec(DV_A), vec(DV_B),
            pl.BlockSpec((V_A + V_B, D), lambda i: (0, 0)),
            vec(D), vec(D),
        ],
        out_specs=[tok(D), pl.BlockSpec((TM * ROWS_PER_TOKEN, LANES), lambda i: (i, 0))],
        out_shape=[jax.ShapeDtypeStruct((T, D), F32), jax.ShapeDtypeStruct((T * ROWS_PER_TOKEN, LANES), U32)],
        compiler_params=_params(("arbitrary",)),
        name="mixer_ab_out",
    )(o_f, o_b, r_a, o_att, xp, xs, mod, gla_g, diff_g, w_o, g_post, g_ffn)


def _gelu_tanh(x):
    return 0.5 * x * (1.0 + jnp.tanh(math.sqrt(2.0 / math.pi) * (x + 0.044715 * (x * x * x))))


def _sgu_kernel(xp_ref, xs_ref, mod_ref, gpre_ref, win_ref, bin_ref, vg_ref, ws_ref, bs_ref,
                wout_ref, gpost_ref, gffn_ref, o_ref, hp_ref, t_ref):
    m = mod_ref[0, 0]
    x = _stream_tile(xp_ref, xs_ref)
    h = _rms(x, gpre_ref[...]) * (1.0 + m[1:2]) + m[0:1]
    z = _gelu_tanh(_bdot(h, win_ref[...]) + bin_ref[...])
    v = _rms(z[:, SGU_DIM:], vg_ref[...])
    gw = SGU_DIM // SGU_GROUPS
    for ch in range(TM // SGU_CHUNK):
        rows = slice(ch * SGU_CHUNK, (ch + 1) * SGU_CHUNK)
        for g in range(SGU_GROUPS):
            cols = slice(g * gw, (g + 1) * gw)
            vs = _bdot(ws_ref[g], v[rows, cols]) + bs_ref[:, g:g + 1]
            t_ref[rows, cols] = (z[rows, cols] * vs).astype(BF16)
    out = jnp.dot(t_ref[...], wout_ref[...], preferred_element_type=F32)
    x1 = x + m[2:3] * _rms(out, gpost_ref[...])
    o_ref[...] = x1
    _store_token_tiles(hp_ref, _ffn_input_rows(x1, m, gffn_ref[...]))


def _sgu(xp, xs, mod, l, g_pre, w_in, b_in, v_g, w_s, b_s_t, w_out, g_post, g_ffn):
    tok = pl.BlockSpec((TM, D), lambda i: (i, 0))
    full = lambda *shape: pl.BlockSpec(shape, lambda i: (0,) * len(shape))
    return pl.pallas_call(
        _sgu_kernel,
        grid=(T // TM,),
        in_specs=_stream_specs() + [
            pl.BlockSpec((1, 1, 6, D), lambda i: (l, _mod_row(i), 0, 0)),
            full(1, D), full(D, 2 * SGU_DIM), full(1, 2 * SGU_DIM), full(1, SGU_DIM),
            full(SGU_GROUPS, SGU_CHUNK, SGU_CHUNK), full(SGU_CHUNK, SGU_GROUPS),
            full(SGU_DIM, D), full(1, D), full(1, D),
        ],
        out_specs=[tok, pl.BlockSpec((TM * ROWS_PER_TOKEN, LANES), lambda i: (i, 0))],
        out_shape=[jax.ShapeDtypeStruct((T, D), F32), jax.ShapeDtypeStruct((T * ROWS_PER_TOKEN, LANES), U32)],
        scratch_shapes=[pltpu.VMEM((TM, SGU_DIM), BF16)],
        compiler_params=_params(("arbitrary",)),
        name="sgu_mixer",
    )(xp, xs, mod, g_pre, w_in, b_in, v_g, w_s, b_s_t, w_out, g_post, g_ffn)


LANES = 128
U32 = jnp.uint32
PACKED = D // 2
ROWS_PER_TOKEN = PACKED // LANES
BF16_BITS = 16
HIGH_HALF = 0xFFFF0000


def _pack_rows(x):
    bits = lax.bitcast_convert_type(x.astype(BF16).astype(F32), U32)
    return bits[:, :PACKED] | (bits[:, PACKED:] >> BF16_BITS)


def _unpack_rows(u):
    return (lax.bitcast_convert_type(u & U32(HIGH_HALF), F32), lax.bitcast_convert_type(u << BF16_BITS, F32))


def _store_token_tiles(ref, u, first=0):
    n = u.shape[0]
    for c in range(ROWS_PER_TOKEN):
        ref[pl.ds(first * ROWS_PER_TOKEN + c, n, stride=ROWS_PER_TOKEN), :] = u[:, c * LANES:(c + 1) * LANES]


def _load_token_tiles(ref, n, first=0):
    return jnp.concatenate([ref[pl.ds(first * ROWS_PER_TOKEN + c, n, stride=ROWS_PER_TOKEN), :]
                            for c in range(ROWS_PER_TOKEN)], axis=1)


def _ffn_input_rows(x, m, g):
    return _pack_rows(_rms(x, g) * (1.0 + m[4:5]) + m[3:4])


def _router_kernel(hp_ref, wr_ref, eb_ref, te_ref, wn_ref, rk_ref, cnt_ref, carry_ref, upper_ref):
    i = pl.program_id(0)

    @pl.when(i == 0)
    def _():
        carry_ref[...] = jnp.zeros_like(carry_ref)
        tj = lax.broadcasted_iota(I32, (TR, TR), 0)
        ti = lax.broadcasted_iota(I32, (TR, TR), 1)
        upper_ref[...] = jnp.where(tj < ti, 1.0, 0.0).astype(BF16)

    h_hi, h_lo = (t.astype(BF16) for t in _unpack_rows(_load_token_tiles(hp_ref, TR)))
    w1, w2, _ = _split3(wr_ref[...])
    nt = lambda a, b: lax.dot_general(a, b, (((1,), (1,)), ((), ())), preferred_element_type=F32)
    logits = (nt(w1[:, :PACKED], h_hi) + nt(w1[:, PACKED:], h_lo)
              + nt(w2[:, :PACKED], h_hi) + nt(w2[:, PACKED:], h_lo))
    scores = jax.nn.sigmoid(logits)
    sel = scores + eb_ref[...]

    row8 = lax.broadcasted_iota(I32, (GROUP_SIZE, TR), 0)
    gscore = []
    for g in range(N_GROUPS):
        xg = sel[g * GROUP_SIZE:(g + 1) * GROUP_SIZE]
        m1 = jnp.max(xg, axis=0, keepdims=True)
        i1 = jnp.min(jnp.where(xg == m1, row8, GROUP_SIZE), axis=0, keepdims=True)
        m2 = jnp.max(jnp.where(row8 == i1, NEG_INF, xg), axis=0, keepdims=True)
        gscore.append(m1 + m2)
    pieces = []
    for g in range(N_GROUPS):
        rank = jnp.zeros((1, TR), I32)
        for g2 in range(N_GROUPS):
            if g2 == g:
                continue
            beats = (gscore[g2] >= gscore[g]) if g2 < g else (gscore[g2] > gscore[g])
            rank = rank + beats.astype(I32)
        pieces.append(jnp.where(rank < TOPK_GROUPS, sel[g * GROUP_SIZE:(g + 1) * GROUP_SIZE], NEG_INF))
    cur = jnp.concatenate(pieces, axis=0)

    row = lax.broadcasted_iota(I32, (N_EXPERTS, TR), 0)
    idxs, ws = [], []
    for _ in range(TOP_K):
        mx = jnp.max(cur, axis=0, keepdims=True)
        idx = jnp.min(jnp.where(cur == mx, row, N_EXPERTS), axis=0, keepdims=True)
        hit = row == idx
        ws.append(jnp.sum(jnp.where(hit, scores, 0.0), axis=0, keepdims=True))
        cur = jnp.where(hit, NEG_INF, cur)
        idxs.append(idx)
    mask = jnp.zeros((N_EXPERTS, TR), F32)
    for idx in idxs:
        mask = mask + (row == idx).astype(F32)
    wsum = ws[0]
    for wk in ws[1:]:
        wsum = wsum + wk

    pos = carry_ref[...] + jnp.dot(mask.astype(BF16), upper_ref[...], preferred_element_type=F32)
    for k in range(TOP_K):
        hit = row == idxs[k]
        te_ref[k:k + 1, :] = idxs[k]
        wn_ref[k:k + 1, :] = ws[k] / wsum * ROUTED_SCALE
        rk_ref[k:k + 1, :] = jnp.sum(jnp.where(hit, pos, 0.0), axis=0, keepdims=True).astype(I32)
    carry_ref[...] = carry_ref[...] + jnp.sum(mask, axis=1, keepdims=True)
    cnt_ref[...] = carry_ref[...]


def _router(hp, wr_t, e_bias):
    kt = lambda dtype: jax.ShapeDtypeStruct((TOP_K, T), dtype)
    kt_spec = pl.BlockSpec((TOP_K, TR), lambda i: (0, i))
    return pl.pallas_call(
        _router_kernel,
        grid=(T // TR,),
        in_specs=[
            pl.BlockSpec((TR * ROWS_PER_TOKEN, LANES), lambda i: (i, 0)),
            pl.BlockSpec((N_EXPERTS, D), lambda i: (0, 0)),
            pl.BlockSpec((N_EXPERTS, 1), lambda i: (0, 0)),
        ],
        out_specs=[
            kt_spec, kt_spec, kt_spec,
            pl.BlockSpec((N_EXPERTS, 1), lambda i: (0, 0)),
        ],
        out_shape=[
            kt(I32), kt(F32), kt(I32),
            jax.ShapeDtypeStruct((N_EXPERTS, 1), F32),
        ],
        scratch_shapes=[pltpu.VMEM((N_EXPERTS, 1), F32), pltpu.VMEM((TR, TR), BF16)],
        compiler_params=_params(("arbitrary",)),
        name="moe_router",
    )(hp, wr_t, e_bias)


_PAD_BITS = tuple(1 << b for b in range(GM_SUB.bit_length() - 1))


def _pad_fill_kernel(pad_start_ref, pad_len_ref, xg_in_ref, xg_ref, zero_ref, sem):
    del xg_in_ref
    zero_ref[...] = jnp.zeros_like(zero_ref)

    def pad_copies(e):
        start = pad_start_ref[e]
        n = pad_len_ref[e]
        copies = []
        for bit in _PAD_BITS:
            first = start + (n & ~(2 * bit - 1))
            copies.append(((n & bit) != 0, pltpu.make_async_copy(
                zero_ref.at[pl.ds(0, bit)], xg_ref.at[pl.ds(first, bit)], sem)))
        return copies

    def start_e(e, carry):
        for on, cp in pad_copies(e):
            @pl.when(on)
            def _():
                cp.start()
        return carry

    def wait_e(e, carry):
        for on, cp in pad_copies(e):
            @pl.when(on)
            def _():
                cp.wait()
        return carry

    lax.fori_loop(0, N_EXPERTS, start_e, 0)
    lax.fori_loop(0, N_EXPERTS, wait_e, 0)


def _pad_fill(pad_start, pad_len, xg):
    grid_spec = pltpu.PrefetchScalarGridSpec(
        num_scalar_prefetch=2,
        grid=(1,),
        in_specs=[pl.BlockSpec(memory_space=pl.ANY)],
        out_specs=pl.BlockSpec(memory_space=pl.ANY),
        scratch_shapes=[pltpu.VMEM((GM_SUB // 2, ROWS_PER_TOKEN, LANES), xg.dtype), pltpu.SemaphoreType.DMA],
    )
    return pl.pallas_call(
        _pad_fill_kernel,
        grid_spec=grid_spec,
        out_shape=jax.ShapeDtypeStruct(xg.shape, xg.dtype),
        input_output_aliases={2: 0},
        compiler_params=_params(("arbitrary",)),
        name="moe_pad_fill",
    )(pad_start, pad_len, xg)


SC_CORES, SC_SUBCORES = 2, 16
SC_WORKERS = SC_CORES * SC_SUBCORES
SC_W = 64


def _sc_worker_id():
    return lax.axis_index("s") * SC_CORES + lax.axis_index("c")


def _sc_dispatch(h3, slot3):
    nchunk = T // SC_WORKERS // SC_W
    mesh = plsc.VectorSubcoreMesh(core_axis_name="c", subcore_axis_name="s")
    tile = (SC_W, ROWS_PER_TOKEN, LANES)

    @functools.partial(
        pl.kernel, mesh=mesh,
        out_type=jax.ShapeDtypeStruct((SP, ROWS_PER_TOKEN, LANES), h3.dtype),
        scratch_types=[pltpu.VMEM((TOP_K, SC_W), I32), pltpu.VMEM((TOP_K, SC_W), I32),
                       pltpu.VMEM(tile, h3.dtype), pltpu.VMEM(tile, h3.dtype),
                       pltpu.SemaphoreType.DMA((2,)), pltpu.SemaphoreType.DMA((2,))],
    )
    def k(h_hbm, slot_hbm, xg_hbm, idx0, idx1, rows0, rows1, lsem, ssem):
        first = _sc_worker_id() * nchunk
        idx = (idx0, idx1)
        rows = (rows0, rows1)

        def loads(j, b):
            blk = first + j
            tok = pl.multiple_of(blk * SC_W, SC_W)
            return (pltpu.make_async_copy(slot_hbm.at[blk], idx[b], lsem.at[b]),
                    pltpu.make_async_copy(h_hbm.at[pl.ds(tok, SC_W)], rows[b], lsem.at[b]))

        def scatters(b):
            return [pltpu.make_async_copy(rows[b], xg_hbm.at[idx[b].at[kk]], ssem.at[b]) for kk in range(TOP_K)]

        for cp in loads(0, 0):
            cp.start()

        @pl.loop(0, nchunk, step=2)
        def _(j):
            for b in (0, 1):
                jj = j + b
                for cp in loads(jj, b):
                    cp.wait()
                for cp in scatters(b):
                    cp.start()

                @pl.when(jj + 1 < nchunk)
                def _():
                    @pl.when(jj >= 1)
                    def _():
                        for cp in scatters(1 - b):
                            cp.wait()
                    for cp in loads(jj + 1, 1 - b):
                        cp.start()

        for b in (0, 1):
            for cp in scatters(b):
                cp.wait()

    return k(h3, slot3)


NSUB = GM // GM_SUB
SUB_ROWS = GM_SUB * ROWS_PER_TOKEN


def _per_block_count(count, fn):
    for n in range(1, NSUB + 1):
        @pl.when(count == n)
        def _(n=n):
            fn(n)


def _gmm_kernel(layer, tile_e_ref, tile_blk_ref, tile_nsub_ref, tile_run_ref, tile_next_ref,
                xg_hbm, wg_hbm, wu_hbm, wd_hbm, yg_hbm,
                x_st, y_st, wg_st, wu_st, wd_st, wgu_scr, wd_scr, sems, xsems, ysems):
    j = pl.program_id(0)
    last_step = pl.num_programs(0) - 1
    nsub = tile_nsub_ref[j]
    run = tile_run_ref[j]
    slot = j % 2
    x_ref = x_st.at[slot]
    y_ref = y_st.at[slot]

    def tile_rows(t, n):
        return pl.ds(pl.multiple_of(tile_blk_ref[t] * SUB_ROWS, SUB_ROWS), n * SUB_ROWS)

    def x_copy(t, n):
        return pltpu.make_async_copy(xg_hbm.at[tile_rows(t, n)], x_st.at[t % 2, pl.ds(0, n * SUB_ROWS)], xsems.at[t % 2])

    def y_copy(t, n):
        return pltpu.make_async_copy(y_st.at[t % 2, pl.ds(0, n * SUB_ROWS)], yg_hbm.at[tile_rows(t, n)], ysems.at[t % 2])

    @pl.when(j == 0)
    def _():
        _per_block_count(nsub, lambda n: x_copy(j, n).start())

    _per_block_count(nsub, lambda n: x_copy(j, n).wait())

    @pl.when(j < last_step)
    def _():
        _per_block_count(tile_nsub_ref[j + 1], lambda n: x_copy(j + 1, n).start())

    @pl.when(j >= 2)
    def _():
        _per_block_count(tile_nsub_ref[j - 2], lambda n: y_copy(j - 2, n).wait())

    def weight_copies(e, slot):
        return [pltpu.make_async_copy(src.at[layer, e], dst.at[slot], sems.at[slot])
                for src, dst in ((wg_hbm, wg_st), (wu_hbm, wu_st), (wd_hbm, wd_st))]

    @pl.when(run >= 0)
    def _():
        @pl.when(j == 0)
        def _():
            for cp in weight_copies(tile_e_ref[j], run):
                cp.start()

        for cp in weight_copies(tile_e_ref[j], run):
            cp.wait()

        nxt = tile_next_ref[j]

        @pl.when(nxt >= 0)
        def _():
            for cp in weight_copies(nxt, 1 - run):
                cp.start()

        wgu_scr[:, 0:D_EXPERT] = wg_st[run].astype(BF16)
        wgu_scr[:, D_EXPERT:2 * D_EXPERT] = wu_st[run].astype(BF16)
        wd_scr[...] = wd_st[run].astype(BF16)

    def expert_mlp(s):
        x_hi, x_lo = _unpack_rows(_load_token_tiles(x_ref, GM_SUB, s * GM_SUB))
        gu = (jnp.dot(x_hi.astype(BF16), wgu_scr[0:PACKED, :], preferred_element_type=F32)
              + jnp.dot(x_lo.astype(BF16), wgu_scr[PACKED:D, :], preferred_element_type=F32))
        hid = _silu(gu[:, 0:D_EXPERT]) * gu[:, D_EXPERT:2 * D_EXPERT]
        y = jnp.dot(hid.astype(BF16), wd_scr[...], preferred_element_type=F32)
        _store_token_tiles(y_ref, _pack_rows(y), s * GM_SUB)

    def compute_and_send(n):
        for s in range(n):
            expert_mlp(s)
        y_copy(j, n).start()

    _per_block_count(nsub, compute_and_send)

    @pl.when(j == last_step)
    def _():
        _per_block_count(tile_nsub_ref[jnp.maximum(j - 1, 0)], lambda n: y_copy(j - 1, n).wait())
        _per_block_count(nsub, lambda n: y_copy(j, n).wait())


def _gmm(tile_e, tile_blk, tile_nsub, tile_run, tile_next, xg, l, w_gate, w_up, w_down):
    hbm = pl.BlockSpec(memory_space=pl.ANY)
    stage = pltpu.VMEM((2, GM * ROWS_PER_TOKEN, LANES), U32)
    grid_spec = pltpu.PrefetchScalarGridSpec(
        num_scalar_prefetch=5,
        grid=(NT_MAX,),
        in_specs=[hbm, hbm, hbm, hbm],
        out_specs=hbm,
        scratch_shapes=[stage, stage,
                        pltpu.VMEM((2, D, D_EXPERT), F32), pltpu.VMEM((2, D, D_EXPERT), F32),
                        pltpu.VMEM((2, D_EXPERT, D), F32),
                        pltpu.VMEM((D, 2 * D_EXPERT), BF16), pltpu.VMEM((D_EXPERT, D), BF16),
                        pltpu.SemaphoreType.DMA((2,)), pltpu.SemaphoreType.DMA((2,)), pltpu.SemaphoreType.DMA((2,))],
    )
    return pl.pallas_call(
        functools.partial(_gmm_kernel, l),
        grid_spec=grid_spec,
        out_shape=jax.ShapeDtypeStruct((SP * ROWS_PER_TOKEN, LANES), U32),
        compiler_params=_params(("arbitrary",)),
        name="moe_grouped_matmul",
    )(tile_e, tile_blk, tile_nsub, tile_run, tile_next, xg, w_gate, w_up, w_down)


def _sc_gather(table3, idx):
    n_idx = idx.shape[0]
    per_w = n_idx // SC_WORKERS
    nchunk = per_w // SC_W
    mesh = plsc.VectorSubcoreMesh(core_axis_name="c", subcore_axis_name="s")
    tile = (SC_W, ROWS_PER_TOKEN, LANES)

    @functools.partial(
        pl.kernel, mesh=mesh,
        out_type=jax.ShapeDtypeStruct((n_idx, ROWS_PER_TOKEN, LANES), table3.dtype),
        scratch_types=[pltpu.VMEM((per_w,), I32), pltpu.VMEM(tile, table3.dtype), pltpu.VMEM(tile, table3.dtype),
                       pltpu.SemaphoreType.DMA((2,)), pltpu.SemaphoreType.DMA((2,))],
    )
    def k(table_hbm, idx_hbm, out_hbm, idx_v, rows0, rows1, gsem, wsem):
        base = pl.multiple_of(_sc_worker_id() * per_w, per_w)
        rows = (rows0, rows1)
        pltpu.sync_copy(idx_hbm.at[pl.ds(base, per_w)], idx_v)

        def gather(j, b):
            ids = idx_v.at[pl.ds(pl.multiple_of(j * SC_W, SC_W), SC_W)]
            return pltpu.make_async_copy(table_hbm.at[ids], rows[b], gsem.at[b])

        def write(j, b):
            dst = out_hbm.at[pl.ds(pl.multiple_of(base + j * SC_W, SC_W), SC_W)]
            return pltpu.make_async_copy(rows[b], dst, wsem.at[b])

        gather(0, 0).start()

        @pl.loop(0, nchunk, step=2)
        def _(j):
            for b in (0, 1):
                jj = j + b
                gather(jj, b).wait()
                write(jj, b).start()

                @pl.when(jj + 1 < nchunk)
                def _():
                    @pl.when(jj >= 1)
                    def _():
                        write(jj - 1, 1 - b).wait()
                    gather(jj + 1, 1 - b).start()

        write(nchunk - 2, 0).wait()
        write(nchunk - 1, 1).wait()

    return k(table3, idx)


def _combine_kernel(wn_ref, x_ref, mod_ref, gpre_ref, gp_ref, wsg_ref, wsu_ref, wsd_ref, y_ref, o_ref, eye_ref):
    @pl.when(pl.program_id(0) == 0)
    def _():
        r = lax.broadcasted_iota(I32, (TD, TD), 0)
        c = lax.broadcasted_iota(I32, (TD, TD), 1)
        eye_ref[...] = jnp.where(r == c, 1.0, 0.0).astype(BF16)

    m = mod_ref[0, 0]
    x = x_ref[...]
    hb = (_rms(x, gpre_ref[...]) * (1.0 + m[4:5]) + m[3:4]).astype(BF16)
    hid = (_silu(jnp.dot(hb, wsg_ref[...], preferred_element_type=F32))
           * jnp.dot(hb, wsu_ref[...], preferred_element_type=F32))
    acc = jnp.dot(hid.astype(BF16), wsd_ref[...], preferred_element_type=F32)

    eye = eye_ref[...]
    nt = lambda a, b: lax.dot_general(a, b, (((1,), (1,)), ((), ())), preferred_element_type=F32)
    w1, w2, w3 = _split3(wn_ref[...])
    w_t = nt(eye, w1) + nt(eye, w2) + nt(eye, w3)

    acc_hi = acc[:, :PACKED]
    acc_lo = acc[:, PACKED:]
    for k in range(TOP_K):
        y_hi, y_lo = _unpack_rows(_load_token_tiles(y_ref, TD, k * TD))
        acc_hi = acc_hi + y_hi * w_t[:, k:k + 1]
        acc_lo = acc_lo + y_lo * w_t[:, k:k + 1]
    acc = jnp.concatenate([acc_hi, acc_lo], axis=1)
    o_ref[...] = x + m[5:6] * _rms(acc, gp_ref[...])


def _combine(wn, x, mod, l, g_pre, g_post, ws_gate, ws_up, ws_down, ybuf, first_tok, n_tok):
    off = first_tok // TD
    tiles_per_dec = DEC_SEQ // TD
    npd = TP // TD
    mod_row = lambda i: jnp.where(i + off < npd, 0, 1 + (i + off - npd) // tiles_per_dec)
    full = lambda *shape: pl.BlockSpec(shape, lambda i: (0,) * len(shape))
    y_spec = pl.BlockSpec((TOP_K * TD * ROWS_PER_TOKEN, LANES), lambda i: (i, 0))
    return pl.pallas_call(
        _combine_kernel,
        grid=(n_tok // TD,),
        in_specs=[
            pl.BlockSpec((TOP_K, TD), lambda i: (0, i + off)),
            pl.BlockSpec((TD, D), lambda i: (i + off, 0)),
            pl.BlockSpec((1, 1, 6, D), lambda i: (l, mod_row(i), 0, 0)),
            full(1, D), full(1, D), full(D, D_SHARED), full(D, D_SHARED), full(D_SHARED, D),
            y_spec,
        ],
        out_specs=pl.BlockSpec((TD, D), lambda i: (i, 0)),
        out_shape=jax.ShapeDtypeStruct((n_tok, D), F32),
        scratch_shapes=[pltpu.VMEM((TD, TD), BF16)],
        compiler_params=_params(("arbitrary",)),
        name="moe_combine",
    )(wn, x, mod, g_pre, g_post, ws_gate, ws_up, ws_down, ybuf)


def _moe_layer(x, h, mod, l, g_pre, g_post, w_router, e_bias, w_gate, w_up, w_down,
               ws_gate, ws_up, ws_down):
    top_e, wn, rk, cnt = _router(h, w_router.T, e_bias.reshape(N_EXPERTS, 1))
    cnt = cnt.reshape(N_EXPERTS).astype(I32)
    padded = (cnt + GM_SUB - 1) // GM_SUB * GM_SUB
    ends = jnp.cumsum(padded)
    offs = ends - padded
    eid = jnp.arange(N_EXPERTS, dtype=I32)[:, None, None]
    slot = rk + jnp.sum(jnp.where(top_e[None] == eid, offs[:, None, None], 0), axis=0)
    ntile = (cnt + GM - 1) // GM
    tile_ends = jnp.cumsum(ntile)
    tid = jnp.arange(NT_MAX, dtype=I32)
    last = jnp.maximum(tile_ends[-1] - 1, 0)
    tile_e = jnp.minimum(jnp.sum((tid[:, None] >= tile_ends[None, :]).astype(I32), axis=1), N_EXPERTS - 1)
    tile_e = jnp.where(tid <= last, tile_e, tile_e[last])
    own = tile_e[:, None] == jnp.arange(N_EXPERTS, dtype=I32)[None, :]
    pick = lambda per_expert: jnp.sum(jnp.where(own, per_expert[None, :], 0), axis=1)
    t_in = tid - pick(tile_ends - ntile)
    tile_blk = jnp.where(tid <= last, (pick(offs) + GM * t_in) // GM_SUB, 0)
    tile_rows = jnp.clip(pick(cnt) - GM * t_in, 0, GM)
    tile_nsub = jnp.where(tid <= last, (tile_rows + GM_SUB - 1) // GM_SUB, 0)
    starts = jnp.logical_and(jnp.arange(NT_MAX) <= last,
                             jnp.concatenate([jnp.ones((1,), bool), tile_e[1:] != tile_e[:-1]]))
    tile_run = jnp.where(starts, (jnp.cumsum(starts.astype(I32)) - 1) % 2, -1)
    later = jnp.logical_and(jnp.arange(N_EXPERTS, dtype=I32)[None, :] > tile_e[:, None], (cnt > 0)[None, :])
    tile_next = jnp.min(jnp.where(later, jnp.arange(N_EXPERTS, dtype=I32)[None, :], N_EXPERTS), axis=1)
    tile_next = jnp.where(tile_next < N_EXPERTS, tile_next, -1)
    slot3 = slot.reshape(TOP_K, T // SC_W, SC_W).transpose(1, 0, 2)
    xg = _sc_dispatch(h.reshape(T, ROWS_PER_TOKEN, LANES), slot3)
    xg = _pad_fill(offs + cnt, padded - cnt, xg).reshape(SP * ROWS_PER_TOKEN, LANES)
    yg = _gmm(tile_e, tile_blk, tile_nsub.astype(I32), tile_run.astype(I32), tile_next.astype(I32),
              xg, l, w_gate, w_up, w_down)
    yg3 = yg.reshape(SP, ROWS_PER_TOKEN, LANES)
    ws = (ws_gate.astype(BF16), ws_up.astype(BF16), ws_down.astype(BF16))
    outs = []
    for first_tok, n_tok in ((0, TP), (TP, TS)):
        ids = slot[:, first_tok:first_tok + n_tok].reshape(TOP_K, n_tok // TD, TD).transpose(1, 0, 2)
        ybuf = _sc_gather(yg3, ids.reshape(TOP_K * n_tok))
        outs.append(_combine(wn, x, mod, l, g_pre, g_post, *ws,
                             ybuf.reshape(TOP_K * n_tok * ROWS_PER_TOKEN, LANES), first_tok, n_tok))
    return outs


def _rope_tables():
    n = DEC_SEQ
    rows = n // GRID_W
    row = jnp.repeat(jnp.arange(rows), GRID_W).astype(F32)
    col = jnp.tile(jnp.arange(GRID_W), rows).astype(F32)
    half = DQK_B // 2
    inv = ROPE_BASE ** (-jnp.arange(0, half, 2, dtype=F32) / half)
    ang_r = row[:, None] * inv
    ang_c = col[:, None] * inv
    ang = jnp.concatenate([ang_r, ang_r, ang_c, ang_c], axis=-1)
    reps = QK_B // DQK_B
    return jnp.tile(jnp.cos(ang), (1, reps)), jnp.tile(jnp.sin(ang), (1, reps))


def _pad_in_proj(w):
    s = [0, Q_A, 2 * Q_A, 2 * Q_A + V_A, 2 * Q_A + 2 * V_A]
    s += [s[-1] + GATE_RANK, s[-1] + 2 * GATE_RANK]
    s += [s[-1] + QK_B, s[-1] + 2 * QK_B, s[-1] + 2 * QK_B + V_B]
    gates = jnp.pad(w[:, s[4]:s[6]], ((0, 0), (0, GL_PAD - 2 * GATE_RANK)))
    return jnp.concatenate([w[:, s[0]:s[4]], gates, w[:, s[6]:s[9]]], axis=1).astype(BF16)


def kernel(x_prompt, x_sample, c, c_ctx, state_gla, cache_k, cache_v, ada_w, ada_b, norm_pre_mix, norm_post_mix, norm_pre_ffn, norm_post_ffn, ab_w_in, gla_w_g2, gla_b_g2, gla_norm_g, diff_lambda, diff_norm_g, ab_w_out, sgu_w_in, sgu_b_in, sgu_norm_g, sgu_w_s, sgu_b_s, sgu_w_out, moe_w_router, moe_e_bias, moe_w_gate, moe_w_up, moe_w_down, moe_ws_gate, moe_ws_up, moe_ws_down):
    depth = ada_w.shape[0]
    assert x_prompt.shape == (BATCH, SEQ, D) and x_sample.shape == (DEC_BATCH, DEC_SEQ, D)
    assert state_gla.shape == (DEC_BATCH, (depth + 1) // 2, 2, H_A, DK_A, DV_A)
    assert cache_k.shape == (DEC_BATCH, 1, H_B, 2, PAST_LEN, DQK_B) and cache_v.shape == (DEC_BATCH, 1, H_B, PAST_LEN, DV_B)
    assert depth == 2 and moe_w_gate.shape == (depth, N_EXPERTS, D, D_EXPERT)
    xp, xs = x_prompt.reshape(TP, D), x_sample.reshape(TS, D)
    cond = jnp.concatenate([c_ctx[None, :], c, jnp.zeros((8 - 1 - DEC_BATCH, D), F32)], axis=0)
    mod = _modulation(cond, ada_w, ada_b)
    cos, sin = _rope_tables()
    vec = lambda a: a.reshape(1, -1)
    new_s = new_k = new_v = None
    for l in range(depth):
        if l % 2 == 0:
            e = l // 2
            lam_init = 0.8 - 0.6 * math.exp(-0.3 * l)
            a, r_a, gl, q_b, k_b, v_b, new_k, new_v = _in_proj(xp, xs, mod, l, vec(norm_pre_mix[l]),
                                                               _pad_in_proj(ab_w_in[e]), cos, sin)
            s0_t = jnp.swapaxes(state_gla[:, e], -1, -2)
            same_head = jnp.eye(H_A, dtype=bool)[None, None, :, None, :, None]
            s0_t = jnp.where(same_head, s0_t[:, :, :, :, None, :], 0.0).reshape(DEC_BATCH, 2, V_A, Q_A)
            o_f, o_bw, s_fin_t = _gla(a, gl, gla_w_g2[e], gla_b_g2[e].reshape(2, 1, Q_A), s0_t)
            o_att = _diff_attention(q_b, k_b, v_b, cache_k, cache_v, diff_lambda[e], lam_init)
            x, h = _mix_out(lam_init, o_f, o_bw, r_a, o_att, xp, xs, mod, l, vec(gla_norm_g[e]), vec(diff_norm_g[e]),
                            ab_w_out[e].astype(BF16), vec(norm_post_mix[l]), vec(norm_pre_ffn[l]))
            new_s = jnp.swapaxes(s_fin_t, -1, -2)[:, None]
        else:
            o = l // 2
            x, h = _sgu(xp, xs, mod, l, vec(norm_pre_mix[l]), sgu_w_in[o].astype(BF16), vec(sgu_b_in[o]),
                        vec(sgu_norm_g[o]), sgu_w_s[o], sgu_b_s[o].T, sgu_w_out[o].astype(BF16),
                        vec(norm_post_mix[l]), vec(norm_pre_ffn[l]))
        xp, xs = _moe_layer(x, h, mod, l, vec(norm_pre_ffn[l]), vec(norm_post_ffn[l]), moe_w_router[l], moe_e_bias[l],
                            moe_w_gate, moe_w_up, moe_w_down, moe_ws_gate[l], moe_ws_up[l], moe_ws_down[l])
    y_prompt = xp.reshape(BATCH, SEQ, D)
    y_sample = xs.reshape(DEC_BATCH, DEC_SEQ, D)
    return (y_prompt, y_sample, new_s, new_k, new_v)
```

```python
import functools
import math

import jax
import jax.numpy as jnp
from jax import lax
from jax.experimental import pallas as pl
from jax.experimental.pallas import tpu as pltpu
from jax.experimental.pallas import tpu_sc as plsc

F32 = jnp.float32
BF16 = jnp.bfloat16
I32 = jnp.int32

D = 1024
BATCH, SEQ = 32, 256
DEC_BATCH, DEC_SEQ = 4, 2048
PAST_LEN = 256
GRID_W = 64
EPS = 1e-6
TP = BATCH * SEQ
TS = DEC_BATCH * DEC_SEQ
T = TP + TS
H_A, DK_A, DV_A = 4, 64, 128
Q_A, V_A = H_A * DK_A, H_A * DV_A
GATE_RANK, GATE_TAU, GLA_CHUNK = 16, 16.0, 64
H_B, DQK_B, DV_B = 4, 64, 128
QK_B, V_B = H_B * 2 * DQK_B, H_B * DV_B
ROPE_BASE = 10000.0
SGU_DIM, SGU_GROUPS, SGU_CHUNK = 1024, 4, 128
N_EXPERTS, TOP_K, N_GROUPS, TOPK_GROUPS = 64, 8, 8, 4
GROUP_SIZE = N_EXPERTS // N_GROUPS
D_EXPERT, D_SHARED = 256, 256
ROUTED_SCALE = 2.5

TM = 512
TM_SUB = 256
MIX_DTYPE = jnp.bfloat16
NPT = TP // TM
TILES_PER_DEC = DEC_SEQ // TM
SEG = 256
NSEG = T // SEG
NSEG_P = TP // SEG
SEG_PER_DEC = DEC_SEQ // SEG
TR = 512
TD = 512
GM = 2048
GM_SUB = 256
NT_MAX = T * TOP_K // GM + N_EXPERTS
SP = T * TOP_K + N_EXPERTS * GM_SUB
GL_PAD = 128
VMEM_LIMIT = 56 * 1024 * 1024
NEG_INF = float("-inf")


def _bdot(a, b):
    return jnp.dot(a.astype(BF16), b.astype(BF16), preferred_element_type=F32)


def _bdot_nt(a, b):
    return lax.dot_general(a.astype(BF16), b.astype(BF16), (((1,), (1,)), ((), ())),
                           preferred_element_type=F32)


def _bdot_tn(a, b):
    return lax.dot_general(a.astype(BF16), b.astype(BF16), (((0,), (0,)), ((), ())),
                           preferred_element_type=F32)


def _split3(x):
    x1 = x.astype(BF16)
    r1 = x - x1.astype(F32)
    x2 = r1.astype(BF16)
    x3 = (r1 - x2.astype(F32)).astype(BF16)
    return x1, x2, x3


def _rms(x, g):
    return x * lax.rsqrt(jnp.mean(x * x, axis=-1, keepdims=True) + EPS) * g


def _silu(x):
    return x * jax.nn.sigmoid(x)


def _mod_row(i):
    return jnp.where(i < NPT, 0, 1 + (i - NPT) // TILES_PER_DEC)


def _params(sem, limit=VMEM_LIMIT):
    return pltpu.CompilerParams(dimension_semantics=sem, vmem_limit_bytes=limit)


def _mod_kernel(c_ref, w_ref, b_ref, o_ref):
    o_ref[0] = _bdot(_silu(c_ref[...]), w_ref[0]) + b_ref[0]


def _modulation(cond, ada_w, ada_b):
    depth = ada_w.shape[0]
    nj = 6
    out = pl.pallas_call(
        _mod_kernel,
        grid=(depth, nj),
        in_specs=[
            pl.BlockSpec((8, D), lambda l, j: (0, 0)),
            pl.BlockSpec((1, D, D), lambda l, j: (l, 0, j)),
            pl.BlockSpec((1, 1, D), lambda l, j: (l, 0, j)),
        ],
        out_specs=pl.BlockSpec((1, 8, D), lambda l, j: (l, 0, j)),
        out_shape=jax.ShapeDtypeStruct((depth, 8, 6 * D), F32),
        compiler_params=_params(("arbitrary", "arbitrary")),
        name="adaln_modulation",
    )(cond, ada_w, ada_b.reshape(depth, 1, 6 * D))
    return out.reshape(depth, 8, 6, D)


_C_A = 0
_C_R = _C_A + 2 * Q_A + V_A
_C_GL = _C_R + V_A
_C_Q = _C_GL + GL_PAD
_C_K = _C_Q + QK_B
_C_V = _C_K + QK_B
_C_END = _C_V + V_B
ROT_PAIR = DQK_B // 4


def _rope(x, cos, sin):
    lane = lax.broadcasted_iota(I32, x.shape, 1)
    first = (lane % (2 * ROT_PAIR)) < ROT_PAIR
    n = x.shape[1]
    xr = jnp.where(first, -pltpu.roll(x, n - ROT_PAIR, 1), pltpu.roll(x, ROT_PAIR, 1))
    return x * cos + xr * sin


def _stream_specs():
    return [pl.BlockSpec((TM, D), lambda i: (jnp.minimum(i, NPT - 1), 0)),
            pl.BlockSpec((TM, D), lambda i: (jnp.maximum(i - NPT, 0), 0))]


def _stream_tile(xp_ref, xs_ref):
    return jnp.where(pl.program_id(0) < NPT, xp_ref[...], xs_ref[...])


def _in_kernel(xp_ref, xs_ref, mod_ref, g_ref, w_ref, cos_ref, sin_ref,
               a_ref, r_ref, gl_ref, q_ref, k_ref, v_ref, ck_ref, cv_ref):
    latent = pl.program_id(0) >= NPT
    m = mod_ref[0, 0]
    x = _stream_tile(xp_ref, xs_ref)
    for s in range(TM // TM_SUB):
        rows = slice(s * TM_SUB, (s + 1) * TM_SUB)
        h = _rms(x[rows], g_ref[...]) * (1.0 + m[1:2]) + m[0:1]
        hb = h.astype(BF16)

        def proj(c0, c1):
            return jnp.dot(hb, w_ref[:, c0:c1], preferred_element_type=F32)

        a_ref[rows, :] = proj(_C_A, _C_R)
        r_ref[rows, :] = proj(_C_R, _C_GL).astype(MIX_DTYPE)
        gl_ref[rows, :] = proj(_C_GL, _C_Q)
        v_ref[rows, :] = proj(_C_V, _C_END)
        q = proj(_C_Q, _C_K)
        k = proj(_C_K, _C_V)
        cos = cos_ref[rows, :]
        sin = sin_ref[rows, :]
        q_ref[rows, :] = jnp.where(latent, _rope(q, cos, sin), q)
        k_ref[rows, :] = jnp.where(latent, _rope(k, cos, sin), k)

    @pl.when(jnp.logical_not(latent))
    def _():
        for s in range(TM // SEQ):
            rows = slice(s * SEQ, (s + 1) * SEQ)
            for h in range(H_B):
                cv_ref[s, 0, h] = v_ref[rows, h * DV_B:(h + 1) * DV_B]
                for mp in range(2):
                    ck_ref[s, 0, h, mp] = k_ref[rows, (2 * h + mp) * DQK_B:(2 * h + mp + 1) * DQK_B]


def _in_proj(xp, xs, mod, l, g, w_pad, cos, sin):
    tok = lambda width: pl.BlockSpec((TM, width), lambda i: (i, 0))
    rope_spec = pl.BlockSpec((TM, QK_B), lambda i: (jnp.maximum(i - NPT, 0) % TILES_PER_DEC, 0))
    widths = (2 * Q_A + V_A, V_A, GL_PAD, QK_B, QK_B, V_B)
    dtypes = (F32, MIX_DTYPE, F32, F32, F32, F32)
    seqs = TM // SEQ
    prompt_tile = lambda i: jnp.minimum(i, NPT - 1)
    cache_specs = [pl.BlockSpec((seqs, 1, H_B, 2, SEQ, DQK_B), lambda i: (prompt_tile(i), 0, 0, 0, 0, 0)),
                   pl.BlockSpec((seqs, 1, H_B, SEQ, DV_B), lambda i: (prompt_tile(i), 0, 0, 0, 0))]
    cache_shapes = [jax.ShapeDtypeStruct((BATCH, 1, H_B, 2, SEQ, DQK_B), F32),
                    jax.ShapeDtypeStruct((BATCH, 1, H_B, SEQ, DV_B), F32)]
    return pl.pallas_call(
        _in_kernel,
        grid=(T // TM,),
        in_specs=_stream_specs() + [
            pl.BlockSpec((1, 1, 6, D), lambda i: (l, _mod_row(i), 0, 0)),
            pl.BlockSpec((1, D), lambda i: (0, 0)),
            pl.BlockSpec((D, _C_END), lambda i: (0, 0)),
            rope_spec, rope_spec,
        ],
        out_specs=[tok(w) for w in widths] + cache_specs,
        out_shape=[jax.ShapeDtypeStruct((T, w), dt) for w, dt in zip(widths, dtypes)] + cache_shapes,
        compiler_params=_params(("arbitrary",)),
        name="mixer_ab_in_proj",
    )(xp, xs, mod, g, w_pad, cos, sin)


def _log_sigmoid(x):
    return jnp.minimum(x, 0.0) - jnp.log(1.0 + jnp.exp(-jnp.abs(x)))


def _gla_kernel(af_ref, ab_ref, glf_ref, glb_ref, wg_ref, bg_ref, s0_ref,
                of_ref, ob_ref, sfin_ref, st_ref):
    i = pl.program_id(0)

    @pl.when(i < NSEG_P)
    def _():
        st_ref[...] = jnp.zeros_like(st_ref)

    @pl.when(jnp.logical_and(i >= NSEG_P, (i - NSEG_P) % SEG_PER_DEC == 0))
    def _():
        st_ref[...] = s0_ref[0]

    r = lax.broadcasted_iota(I32, (SEG, SEG), 0)
    c = lax.broadcasted_iota(I32, (SEG, SEG), 1)
    same = (r // GLA_CHUNK) == (c // GLA_CHUNK)
    nchunk = SEG // GLA_CHUNK
    own_head = (lax.broadcasted_iota(I32, (V_A, Q_A), 0) // DV_A) == (lax.broadcasted_iota(I32, (V_A, Q_A), 1) // DK_A)

    for d, (a_ref, gl_ref, o_ref) in enumerate(((af_ref, glf_ref, of_ref), (ab_ref, glb_ref, ob_ref))):
        fwd = d == 0
        gcol = gl_ref[:, d * GATE_RANK:(d + 1) * GATE_RANK]
        la = _log_sigmoid(_bdot(gcol, wg_ref[d]) + bg_ref[d]) / GATE_TAU
        causal = jnp.logical_and(same, (c <= r) if fwd else (c >= r))
        tri = jnp.where(causal, 1.0, 0.0).astype(BF16)
        l1, l2, l3 = _split3(la)
        b_all = (jnp.dot(tri, l1, preferred_element_type=F32)
                 + jnp.dot(tri, l2, preferred_element_type=F32)
                 + jnp.dot(tri, l3, preferred_element_type=F32))
        q_in_all = a_ref[:, 0:Q_A] * (DK_A ** -0.5) * jnp.exp(b_all)
        kd_all = a_ref[:, Q_A:2 * Q_A] * jnp.exp(-b_all)
        intra = []
        for h in range(H_A):
            kc = slice(h * DK_A, (h + 1) * DK_A)
            attn = jnp.where(causal, _bdot_nt(q_in_all[:, kc], kd_all[:, kc]), 0.0)
            intra.append(_bdot(attn, a_ref[:, 2 * Q_A + h * DV_A:2 * Q_A + (h + 1) * DV_A]))
        intra = jnp.concatenate(intra, axis=1)
        state = st_ref[d]
        order = range(nchunk) if fwd else range(nchunk - 1, -1, -1)
        for ch in order:
            r0 = ch * GLA_CHUNK
            rows = slice(r0, r0 + GLA_CHUNK)
            end = r0 + GLA_CHUNK - 1 if fwd else r0
            b_end = b_all[end:end + 1, :]
            kw = a_ref[rows, Q_A:2 * Q_A] * jnp.exp(b_end - b_all[rows, :])
            o_ref[rows, :] = (intra[rows, :] + _bdot_nt(q_in_all[rows, :], state)).astype(MIX_DTYPE)
            kv_t = _bdot_tn(a_ref[rows, 2 * Q_A:2 * Q_A + V_A], kw)
            state = state * jnp.exp(b_end) + jnp.where(own_head, kv_t, 0.0)
        st_ref[d] = state

    @pl.when(i < NSEG_P)
    def _():
        for d in range(2):
            for h in range(H_A):
                sfin_ref[0, d, h] = st_ref[d, h * DV_A:(h + 1) * DV_A, h * DK_A:(h + 1) * DK_A]


def _seg_bwd(i):
    j = i - NSEG_P
    return jnp.where(i < NSEG_P, i, NSEG_P + (j // SEG_PER_DEC) * SEG_PER_DEC + (SEG_PER_DEC - 1 - j % SEG_PER_DEC))


def _gla(a, gl, wg, bg, s0_t):
    seg = lambda width, f: pl.BlockSpec((SEG, width), lambda i: (f(i), 0))
    ident = lambda i: i
    st_block = (1, 2, H_A, DV_A, DK_A)
    return pl.pallas_call(
        _gla_kernel,
        grid=(NSEG,),
        in_specs=[
            seg(D, ident), seg(D, _seg_bwd), seg(GL_PAD, ident), seg(GL_PAD, _seg_bwd),
            pl.BlockSpec((2, GATE_RANK, Q_A), lambda i: (0, 0, 0)),
            pl.BlockSpec((2, 1, Q_A), lambda i: (0, 0, 0)),
            pl.BlockSpec((1, 2, V_A, Q_A), lambda i: (jnp.maximum(i - NSEG_P, 0) // SEG_PER_DEC, 0, 0, 0)),
        ],
        out_specs=[
            seg(V_A, ident), seg(V_A, _seg_bwd),
            pl.BlockSpec(st_block, lambda i: (jnp.minimum(i, NSEG_P - 1), 0, 0, 0, 0)),
        ],
        out_shape=[
            jax.ShapeDtypeStruct((T, V_A), MIX_DTYPE),
            jax.ShapeDtypeStruct((T, V_A), MIX_DTYPE),
            jax.ShapeDtypeStruct((BATCH, 2, H_A, DV_A, DK_A), F32),
        ],
        scratch_shapes=[pltpu.VMEM((2, V_A, Q_A), F32)],
        compiler_params=_params(("arbitrary",)),
        name="gla_bidir",
    )(a, a, gl, gl, wg, bg, s0_t)


def _diff_lambda(lam_ref, lam_init):
    lp = lam_ref[...]
    s01 = jnp.sum(lp[0:1] * lp[1:2], axis=1, keepdims=True)
    s23 = jnp.sum(lp[2:3] * lp[3:4], axis=1, keepdims=True)
    return jnp.exp(s01) - jnp.exp(s23) + lam_init


def _attn_prompt_kernel(lam_init, q_ref, k_ref, v_ref, lam_ref, o_ref):
    lam = _diff_lambda(lam_ref, lam_init)
    for h in range(H_B):
        ps = []
        for m in range(2):
            cols = slice((2 * h + m) * DQK_B, (2 * h + m + 1) * DQK_B)
            s = _bdot_nt(q_ref[:, cols] * (DQK_B ** -0.5), k_ref[:, cols])
            e = jnp.exp(s - jnp.max(s, axis=1, keepdims=True))
            ps.append(e * (1.0 / jnp.sum(e, axis=1, keepdims=True)))
        w = ps[0] - lam * ps[1]
        o_ref[:, h * DV_B:(h + 1) * DV_B] = _bdot(w, v_ref[:, h * DV_B:(h + 1) * DV_B]).astype(MIX_DTYPE)


def _attn_sample_kernel(lam_init, q_ref, k_ref, v_ref, ck_ref, cv_ref, lam_ref, o_ref):
    lam = _diff_lambda(lam_ref, lam_init)
    for h in range(H_B):
        parts = []
        for m in range(2):
            cols = slice((2 * h + m) * DQK_B, (2 * h + m + 1) * DQK_B)
            q = q_ref[:, cols] * (DQK_B ** -0.5)
            sc = _bdot_nt(q, ck_ref[0, 0, h, m])
            sn = _bdot_nt(q, k_ref[:, cols])
            mx = jnp.maximum(jnp.max(sc, axis=1, keepdims=True), jnp.max(sn, axis=1, keepdims=True))
            ec = jnp.exp(sc - mx)
            en = jnp.exp(sn - mx)
            inv = (1.0 if m == 0 else -lam) / (jnp.sum(ec, axis=1, keepdims=True) + jnp.sum(en, axis=1, keepdims=True))
            parts.append((ec * inv, en * inv))
        wc = parts[0][0] + parts[1][0]
        wn = parts[0][1] + parts[1][1]
        o_ref[:, h * DV_B:(h + 1) * DV_B] = (_bdot(wc, cv_ref[0, 0, h])
                                             + _bdot(wn, v_ref[:, h * DV_B:(h + 1) * DV_B])).astype(MIX_DTYPE)


QB = SEQ
NQB_DEC = DEC_SEQ // QB


def _attn_kernel(lam_init, q_ref, kp_ref, vp_ref, ks_ref, vs_ref, ck_ref, cv_ref, lam_ref, o_ref):
    i = pl.program_id(0)

    @pl.when(i < BATCH)
    def _():
        _attn_prompt_kernel(lam_init, q_ref, kp_ref, vp_ref, lam_ref, o_ref)

    @pl.when(i >= BATCH)
    def _():
        _attn_sample_kernel(lam_init, q_ref, ks_ref, vs_ref, ck_ref, cv_ref, lam_ref, o_ref)


def _diff_attention(q, k, v, cache_k, cache_v, lam_p, lam_init):
    blk = lambda rows, f: pl.BlockSpec((rows, QK_B), f)
    dec_b = lambda i: jnp.maximum(i - BATCH, 0) // NQB_DEC
    own = lambda i: (i, 0)
    prompt_kv = lambda i: (jnp.minimum(i, BATCH - 1), 0)
    dec_kv = lambda i: (TP // DEC_SEQ + dec_b(i), 0)
    return pl.pallas_call(
        functools.partial(_attn_kernel, lam_init),
        grid=(BATCH + DEC_BATCH * NQB_DEC,),
        in_specs=[
            blk(QB, own), blk(SEQ, prompt_kv), blk(SEQ, prompt_kv), blk(DEC_SEQ, dec_kv), blk(DEC_SEQ, dec_kv),
            pl.BlockSpec((1, 1, H_B, 2, PAST_LEN, DQK_B), lambda i: (dec_b(i), 0, 0, 0, 0, 0)),
            pl.BlockSpec((1, 1, H_B, PAST_LEN, DV_B), lambda i: (dec_b(i), 0, 0, 0, 0)),
            pl.BlockSpec((4, DQK_B), lambda i: (0, 0)),
        ],
        out_specs=blk(QB, own),
        out_shape=jax.ShapeDtypeStruct((T, V_B), MIX_DTYPE),
        compiler_params=_params(("arbitrary",)),
        name="diff_attention",
    )(q, k, v, k, v, cache_k, cache_v, lam_p)


def _head_rms(x, g, nheads, width):
    return jnp.concatenate([_rms(x[:, h * width:(h + 1) * width], g) for h in range(nheads)], axis=1)


def _mix_out_kernel(lam_init, of_ref, ob_ref, r_ref, oatt_ref, xp_ref, xs_ref, mod_ref,
                    gg_ref, dg_ref, wo_ref, gp_ref, gffn_ref, o_ref, hp_ref):
    m = mod_ref[0, 0]
    load = lambda ref: ref[...].astype(F32)
    o_a = _head_rms(load(of_ref) + load(ob_ref), gg_ref[...], H_A, DV_A) * _silu(load(r_ref))
    o_b = _head_rms(load(oatt_ref), dg_ref[...], H_B, DV_B) * (1.0 - lam_init)
    out = _bdot(o_a, wo_ref[0:V_A, :]) + _bdot(o_b, wo_ref[V_A:V_A + V_B, :])
    x1 = _stream_tile(xp_ref, xs_ref) + m[2:3] * _rms(out, gp_ref[...])
    o_ref[...] = x1
    _store_token_tiles(hp_ref, _ffn_input_rows(x1, m, gffn_ref[...]))


def _mix_out(lam_init, o_f, o_b, r_a, o_att, xp, xs, mod, l, gla_g, diff_g, w_o, g_post, g_ffn):
    tok = lambda width: pl.BlockSpec((TM, width), lambda i: (i, 0))
    vec = lambda width: pl.BlockSpec((1, width), lambda i: (0, 0))
    return pl.pallas_call(
        functools.partial(_mix_out_kernel, lam_init),
        grid=(T // TM,),
        in_specs=[
            tok(V_A), tok(V_A), tok(V_A), tok(V_B), *_stream_specs(),
            pl.BlockSpec((1, 1, 6, D), lambda i: (l, _mod_row(i), 0, 0)),
            vec(DV_A), vec(DV_B),
            pl.BlockSpec((V_A + V_B, D), lambda i: (0, 0)),
            vec(D), vec(D),
        ],
        out_specs=[tok(D), pl.BlockSpec((TM * ROWS_PER_TOKEN, LANES), lambda i: (i, 0))],
        out_shape=[jax.ShapeDtypeStruct((T, D), F32), jax.ShapeDtypeStruct((T * ROWS_PER_TOKEN, LANES), U32)],
        compiler_params=_params(("arbitrary",)),
        name="mixer_ab_out",
    )(o_f, o_b, r_a, o_att, xp, xs, mod, gla_g, diff_g, w_o, g_post, g_ffn)


def _gelu_tanh(x):
    return 0.5 * x * (1.0 + jnp.tanh(math.sqrt(2.0 / math.pi) * (x + 0.044715 * (x * x * x))))


def _sgu_kernel(xp_ref, xs_ref, mod_ref, gpre_ref, win_ref, bin_ref, vg_ref, ws_ref, bs_ref,
                wout_ref, gpost_ref, gffn_ref, o_ref, hp_ref, t_ref):
    m = mod_ref[0, 0]
    x = _stream_tile(xp_ref, xs_ref)
    h = _rms(x, gpre_ref[...]) * (1.0 + m[1:2]) + m[0:1]
    z = _gelu_tanh(_bdot(h, win_ref[...]) + bin_ref[...])
    v = _rms(z[:, SGU_DIM:], vg_ref[...])
    gw = SGU_DIM // SGU_GROUPS
    for ch in range(TM // SGU_CHUNK):
        rows = slice(ch * SGU_CHUNK, (ch + 1) * SGU_CHUNK)
        for g in range(SGU_GROUPS):
            cols = slice(g * gw, (g + 1) * gw)
            vs = _bdot(ws_ref[g], v[rows, cols]) + bs_ref[:, g:g + 1]
            t_ref[rows, cols] = (z[rows, cols] * vs).astype(BF16)
    out = jnp.dot(t_ref[...], wout_ref[...], preferred_element_type=F32)
    x1 = x + m[2:3] * _rms(out, gpost_ref[...])
    o_ref[...] = x1
    _store_token_tiles(hp_ref, _ffn_input_rows(x1, m, gffn_ref[...]))


def _sgu(xp, xs, mod, l, g_pre, w_in, b_in, v_g, w_s, b_s_t, w_out, g_post, g_ffn):
    tok = pl.BlockSpec((TM, D), lambda i: (i, 0))
    full = lambda *shape: pl.BlockSpec(shape, lambda i: (0,) * len(shape))
    return pl.pallas_call(
        _sgu_kernel,
        grid=(T // TM,),
        in_specs=_stream_specs() + [
            pl.BlockSpec((1, 1, 6, D), lambda i: (l, _mod_row(i), 0, 0)),
            full(1, D), full(D, 2 * SGU_DIM), full(1, 2 * SGU_DIM), full(1, SGU_DIM),
            full(SGU_GROUPS, SGU_CHUNK, SGU_CHUNK), full(SGU_CHUNK, SGU_GROUPS),
            full(SGU_DIM, D), full(1, D), full(1, D),
        ],
        out_specs=[tok, pl.BlockSpec((TM * ROWS_PER_TOKEN, LANES), lambda i: (i, 0))],
        out_shape=[jax.ShapeDtypeStruct((T, D), F32), jax.ShapeDtypeStruct((T * ROWS_PER_TOKEN, LANES), U32)],
        scratch_shapes=[pltpu.VMEM((TM, SGU_DIM), BF16)],
        compiler_params=_params(("arbitrary",)),
        name="sgu_mixer",
    )(xp, xs, mod, g_pre, w_in, b_in, v_g, w_s, b_s_t, w_out, g_post, g_ffn)


LANES = 128
U32 = jnp.uint32
PACKED = D // 2
ROWS_PER_TOKEN = PACKED // LANES
BF16_BITS = 16
HIGH_HALF = 0xFFFF0000


def _pack_rows(x):
    bits = lax.bitcast_convert_type(x.astype(BF16).astype(F32), U32)
    return bits[:, :PACKED] | (bits[:, PACKED:] >> BF16_BITS)


def _unpack_rows(u):
    return (lax.bitcast_convert_type(u & U32(HIGH_HALF), F32), lax.bitcast_convert_type(u << BF16_BITS, F32))


def _store_token_tiles(ref, u, first=0):
    n = u.shape[0]
    for c in range(ROWS_PER_TOKEN):
        ref[pl.ds(first * ROWS_PER_TOKEN + c, n, stride=ROWS_PER_TOKEN), :] = u[:, c * LANES:(c + 1) * LANES]


def _load_token_tiles(ref, n, first=0):
    return jnp.concatenate([ref[pl.ds(first * ROWS_PER_TOKEN + c, n, stride=ROWS_PER_TOKEN), :]
                            for c in range(ROWS_PER_TOKEN)], axis=1)


def _ffn_input_rows(x, m, g):
    return _pack_rows(_rms(x, g) * (1.0 + m[4:5]) + m[3:4])


def _router_kernel(hp_ref, wr_ref, eb_ref, te_ref, wn_ref, rk_ref, cnt_ref, carry_ref, upper_ref):
    i = pl.program_id(0)

    @pl.when(i == 0)
    def _():
        carry_ref[...] = jnp.zeros_like(carry_ref)
        tj = lax.broadcasted_iota(I32, (TR, TR), 0)
        ti = lax.broadcasted_iota(I32, (TR, TR), 1)
        upper_ref[...] = jnp.where(tj < ti, 1.0, 0.0).astype(BF16)

    h_hi, h_lo = (t.astype(BF16) for t in _unpack_rows(_load_token_tiles(hp_ref, TR)))
    w1, w2, _ = _split3(wr_ref[...])
    nt = lambda a, b: lax.dot_general(a, b, (((1,), (1,)), ((), ())), preferred_element_type=F32)
    logits = (nt(w1[:, :PACKED], h_hi) + nt(w1[:, PACKED:], h_lo)
              + nt(w2[:, :PACKED], h_hi) + nt(w2[:, PACKED:], h_lo))
    scores = jax.nn.sigmoid(logits)
    sel = scores + eb_ref[...]

    row8 = lax.broadcasted_iota(I32, (GROUP_SIZE, TR), 0)
    gscore = []
    for g in range(N_GROUPS):
        xg = sel[g * GROUP_SIZE:(g + 1) * GROUP_SIZE]
        m1 = jnp.max(xg, axis=0, keepdims=True)
        i1 = jnp.min(jnp.where(xg == m1, row8, GROUP_SIZE), axis=0, keepdims=True)
        m2 = jnp.max(jnp.where(row8 == i1, NEG_INF, xg), axis=0, keepdims=True)
        gscore.append(m1 + m2)
    pieces = []
    for g in range(N_GROUPS):
        rank = jnp.zeros((1, TR), I32)
        for g2 in range(N_GROUPS):
            if g2 == g:
                continue
            beats = (gscore[g2] >= gscore[g]) if g2 < g else (gscore[g2] > gscore[g])
            rank = rank + beats.astype(I32)
        pieces.append(jnp.where(rank < TOPK_GROUPS, sel[g * GROUP_SIZE:(g + 1) * GROUP_SIZE], NEG_INF))
    cur = jnp.concatenate(pieces, axis=0)

    row = lax.broadcasted_iota(I32, (N_EXPERTS, TR), 0)
    idxs, ws = [], []
    for _ in range(TOP_K):
        mx = jnp.max(cur, axis=0, keepdims=True)
        idx = jnp.min(jnp.where(cur == mx, row, N_EXPERTS), axis=0, keepdims=True)
        hit = row == idx
        ws.append(jnp.sum(jnp.where(hit, scores, 0.0), axis=0, keepdims=True))
        cur = jnp.where(hit, NEG_INF, cur)
        idxs.append(idx)
    mask = jnp.zeros((N_EXPERTS, TR), F32)
    for idx in idxs:
        mask = mask + (row == idx).astype(F32)
    wsum = ws[0]
    for wk in ws[1:]:
        wsum = wsum + wk

    pos = carry_ref[...] + jnp.dot(mask.astype(BF16), upper_ref[...], preferred_element_type=F32)
    for k in range(TOP_K):
        hit = row == idxs[k]
        te_ref[k:k + 1, :] = idxs[k]
        wn_ref[k:k + 1, :] = ws[k] / wsum * ROUTED_SCALE
        rk_ref[k:k + 1, :] = jnp.sum(jnp.where(hit, pos, 0.0), axis=0, keepdims=True).astype(I32)
    carry_ref[...] = carry_ref[...] + jnp.sum(mask, axis=1, keepdims=True)
    cnt_ref[...] = carry_ref[...]


def _router(hp, wr_t, e_bias):
    kt = lambda dtype: jax.ShapeDtypeStruct((TOP_K, T), dtype)
    kt_spec = pl.BlockSpec((TOP_K, TR), lambda i: (0, i))
    return pl.pallas_call(
        _router_kernel,
        grid=(T // TR,),
        in_specs=[
            pl.BlockSpec((TR * ROWS_PER_TOKEN, LANES), lambda i: (i, 0)),
            pl.BlockSpec((N_EXPERTS, D), lambda i: (0, 0)),
            pl.BlockSpec((N_EXPERTS, 1), lambda i: (0, 0)),
        ],
        out_specs=[
            kt_spec, kt_spec, kt_spec,
            pl.BlockSpec((N_EXPERTS, 1), lambda i: (0, 0)),
        ],
        out_shape=[
            kt(I32), kt(F32), kt(I32),
            jax.ShapeDtypeStruct((N_EXPERTS, 1), F32),
        ],
        scratch_shapes=[pltpu.VMEM((N_EXPERTS, 1), F32), pltpu.VMEM((TR, TR), BF16)],
        compiler_params=_params(("arbitrary",)),
        name="moe_router",
    )(hp, wr_t, e_bias)


_PAD_BITS = tuple(1 << b for b in range(GM_SUB.bit_length() - 1))


def _pad_fill_kernel(pad_start_ref, pad_len_ref, xg_in_ref, xg_ref, zero_ref, sem):
    del xg_in_ref
    zero_ref[...] = jnp.zeros_like(zero_ref)

    def pad_copies(e):
        start = pad_start_ref[e]
        n = pad_len_ref[e]
        copies = []
        for bit in _PAD_BITS:
            first = start + (n & ~(2 * bit - 1))
            copies.append(((n & bit) != 0, pltpu.make_async_copy(
                zero_ref.at[pl.ds(0, bit)], xg_ref.at[pl.ds(first, bit)], sem)))
        return copies

    def start_e(e, carry):
        for on, cp in pad_copies(e):
            @pl.when(on)
            def _():
                cp.start()
        return carry

    def wait_e(e, carry):
        for on, cp in pad_copies(e):
            @pl.when(on)
            def _():
                cp.wait()
        return carry

    lax.fori_loop(0, N_EXPERTS, start_e, 0)
    lax.fori_loop(0, N_EXPERTS, wait_e, 0)


def _pad_fill(pad_start, pad_len, xg):
    grid_spec = pltpu.PrefetchScalarGridSpec(
        num_scalar_prefetch=2,
        grid=(1,),
        in_specs=[pl.BlockSpec(memory_space=pl.ANY)],
        out_specs=pl.BlockSpec(memory_space=pl.ANY),
        scratch_shapes=[pltpu.VMEM((GM_SUB // 2, ROWS_PER_TOKEN, LANES), xg.dtype), pltpu.SemaphoreType.DMA],
    )
    return pl.pallas_call(
        _pad_fill_kernel,
        grid_spec=grid_spec,
        out_shape=jax.ShapeDtypeStruct(xg.shape, xg.dtype),
        input_output_aliases={2: 0},
        compiler_params=_params(("arbitrary",)),
        name="moe_pad_fill",
    )(pad_start, pad_len, xg)


SC_CORES, SC_SUBCORES = 2, 16
SC_WORKERS = SC_CORES * SC_SUBCORES
SC_W = 64


def _sc_worker_id():
    return lax.axis_index("s") * SC_CORES + lax.axis_index("c")


def _sc_dispatch(h3, slot3):
    nchunk = T // SC_WORKERS // SC_W
    mesh = plsc.VectorSubcoreMesh(core_axis_name="c", subcore_axis_name="s")
    tile = (SC_W, ROWS_PER_TOKEN, LANES)

    @functools.partial(
        pl.kernel, mesh=mesh,
        out_type=jax.ShapeDtypeStruct((SP, ROWS_PER_TOKEN, LANES), h3.dtype),
        scratch_types=[pltpu.VMEM((TOP_K, SC_W), I32), pltpu.VMEM((TOP_K, SC_W), I32),
                       pltpu.VMEM(tile, h3.dtype), pltpu.VMEM(tile, h3.dtype),
                       pltpu.SemaphoreType.DMA((2,)), pltpu.SemaphoreType.DMA((2,))],
    )
    def k(h_hbm, slot_hbm, xg_hbm, idx0, idx1, rows0, rows1, lsem, ssem):
        first = _sc_worker_id() * nchunk
        idx = (idx0, idx1)
        rows = (rows0, rows1)

        def loads(j, b):
            blk = first + j
            tok = pl.multiple_of(blk * SC_W, SC_W)
            return (pltpu.make_async_copy(slot_hbm.at[blk], idx[b], lsem.at[b]),
                    pltpu.make_async_copy(h_hbm.at[pl.ds(tok, SC_W)], rows[b], lsem.at[b]))

        def scatters(b):
            return [pltpu.make_async_copy(rows[b], xg_hbm.at[idx[b].at[kk]], ssem.at[b]) for kk in range(TOP_K)]

        for cp in loads(0, 0):
            cp.start()

        @pl.loop(0, nchunk, step=2)
        def _(j):
            for b in (0, 1):
                jj = j + b
                for cp in loads(jj, b):
                    cp.wait()
                for cp in scatters(b):
                    cp.start()

                @pl.when(jj + 1 < nchunk)
                def _():
                    @pl.when(jj >= 1)
                    def _():
                        for cp in scatters(1 - b):
                            cp.wait()
                    for cp in loads(jj + 1, 1 - b):
                        cp.start()

        for b in (0, 1):
            for cp in scatters(b):
                cp.wait()

    return k(h3, slot3)


NSUB = GM // GM_SUB
SUB_ROWS = GM_SUB * ROWS_PER_TOKEN


def _per_block_count(count, fn):
    for n in range(1, NSUB + 1):
        @pl.when(count == n)
        def _(n=n):
            fn(n)


def _gmm_kernel(layer, tile_e_ref, tile_blk_ref, tile_nsub_ref, tile_run_ref, tile_next_ref,
                xg_hbm, wg_hbm, wu_hbm, wd_hbm, yg_hbm,
                x_st, y_st, wg_st, wu_st, wd_st, wgu_scr, wd_scr, sems, xsems, ysems):
    j = pl.program_id(0)
    last_step = pl.num_programs(0) - 1
    nsub = tile_nsub_ref[j]
    run = tile_run_ref[j]
    slot = j % 2
    x_ref = x_st.at[slot]
    y_ref = y_st.at[slot]

    def tile_rows(t, n):
        return pl.ds(pl.multiple_of(tile_blk_ref[t] * SUB_ROWS, SUB_ROWS), n * SUB_ROWS)

    def x_copy(t, n):
        return pltpu.make_async_copy(xg_hbm.at[tile_rows(t, n)], x_st.at[t % 2, pl.ds(0, n * SUB_ROWS)], xsems.at[t % 2])

    def y_copy(t, n):
        return pltpu.make_async_copy(y_st.at[t % 2, pl.ds(0, n * SUB_ROWS)], yg_hbm.at[tile_rows(t, n)], ysems.at[t % 2])

    @pl.when(j == 0)
    def _():
        _per_block_count(nsub, lambda n: x_copy(j, n).start())

    _per_block_count(nsub, lambda n: x_copy(j, n).wait())

    @pl.when(j < last_step)
    def _():
        _per_block_count(tile_nsub_ref[j + 1], lambda n: x_copy(j + 1, n).start())

    @pl.when(j >= 2)
    def _():
        _per_block_count(tile_nsub_ref[j - 2], lambda n: y_copy(j - 2, n).wait())

    def weight_copies(e, slot):
        return [pltpu.make_async_copy(src.at[layer, e], dst.at[slot], sems.at[slot])
                for src, dst in ((wg_hbm, wg_st), (wu_hbm, wu_st), (wd_hbm, wd_st))]

    @pl.when(run >= 0)
    def _():
        @pl.when(j == 0)
        def _():
            for cp in weight_copies(tile_e_ref[j], run):
                cp.start()

        for cp in weight_copies(tile_e_ref[j], run):
            cp.wait()

        nxt = tile_next_ref[j]

        @pl.when(nxt >= 0)
        def _():
            for cp in weight_copies(nxt, 1 - run):
                cp.start()

        wgu_scr[:, 0:D_EXPERT] = wg_st[run].astype(BF16)
        wgu_scr[:, D_EXPERT:2 * D_EXPERT] = wu_st[run].astype(BF16)
        wd_scr[...] = wd_st[run].astype(BF16)

    def expert_mlp(s):
        x_hi, x_lo = _unpack_rows(_load_token_tiles(x_ref, GM_SUB, s * GM_SUB))
        gu = (jnp.dot(x_hi.astype(BF16), wgu_scr[0:PACKED, :], preferred_element_type=F32)
              + jnp.dot(x_lo.astype(BF16), wgu_scr[PACKED:D, :], preferred_element_type=F32))
        hid = _silu(gu[:, 0:D_EXPERT]) * gu[:, D_EXPERT:2 * D_EXPERT]
        y = jnp.dot(hid.astype(BF16), wd_scr[...], preferred_element_type=F32)
        _store_token_tiles(y_ref, _pack_rows(y), s * GM_SUB)

    def compute_and_send(n):
        for s in range(n):
            expert_mlp(s)
        y_copy(j, n).start()

    _per_block_count(nsub, compute_and_send)

    @pl.when(j == last_step)
    def _():
        _per_block_count(tile_nsub_ref[jnp.maximum(j - 1, 0)], lambda n: y_copy(j - 1, n).wait())
        _per_block_count(nsub, lambda n: y_copy(j, n).wait())


def _gmm(tile_e, tile_blk, tile_nsub, tile_run, tile_next, xg, l, w_gate, w_up, w_down):
    hbm = pl.BlockSpec(memory_space=pl.ANY)
    stage = pltpu.VMEM((2, GM * ROWS_PER_TOKEN, LANES), U32)
    grid_spec = pltpu.PrefetchScalarGridSpec(
        num_scalar_prefetch=5,
        grid=(NT_MAX,),
        in_specs=[hbm, hbm, hbm, hbm],
        out_specs=hbm,
        scratch_shapes=[stage, stage,
                        pltpu.VMEM((2, D, D_EXPERT), F32), pltpu.VMEM((2, D, D_EXPERT), F32),
                        pltpu.VMEM((2, D_EXPERT, D), F32),
                        pltpu.VMEM((D, 2 * D_EXPERT), BF16), pltpu.VMEM((D_EXPERT, D), BF16),
                        pltpu.SemaphoreType.DMA((2,)), pltpu.SemaphoreType.DMA((2,)), pltpu.SemaphoreType.DMA((2,))],
    )
    return pl.pallas_call(
        functools.partial(_gmm_kernel, l),
        grid_spec=grid_spec,
        out_shape=jax.ShapeDtypeStruct((SP * ROWS_PER_TOKEN, LANES), U32),
        compiler_params=_params(("arbitrary",)),
        name="moe_grouped_matmul",
    )(tile_e, tile_blk, tile_nsub, tile_run, tile_next, xg, w_gate, w_up, w_down)


def _sc_gather(table3, idx):
    n_idx = idx.shape[0]
    per_w = n_idx // SC_WORKERS
    nchunk = per_w // SC_W
    mesh = plsc.VectorSubcoreMesh(core_axis_name="c", subcore_axis_name="s")
    tile = (SC_W, ROWS_PER_TOKEN, LANES)

    @functools.partial(
        pl.kernel, mesh=mesh,
        out_type=jax.ShapeDtypeStruct((n_idx, ROWS_PER_TOKEN, LANES), table3.dtype),
        scratch_types=[pltpu.VMEM((per_w,), I32), pltpu.VMEM(tile, table3.dtype), pltpu.VMEM(tile, table3.dtype),
                       pltpu.SemaphoreType.DMA((2,)), pltpu.SemaphoreType.DMA((2,))],
    )
    def k(table_hbm, idx_hbm, out_hbm, idx_v, rows0, rows1, gsem, wsem):
        base = pl.multiple_of(_sc_worker_id() * per_w, per_w)
        rows = (rows0, rows1)
        pltpu.sync_copy(idx_hbm.at[pl.ds(base, per_w)], idx_v)

        def gather(j, b):
            ids = idx_v.at[pl.ds(pl.multiple_of(j * SC_W, SC_W), SC_W)]
            return pltpu.make_async_copy(table_hbm.at[ids], rows[b], gsem.at[b])

        def write(j, b):
            dst = out_hbm.at[pl.ds(pl.multiple_of(base + j * SC_W, SC_W), SC_W)]
            return pltpu.make_async_copy(rows[b], dst, wsem.at[b])

        gather(0, 0).start()

        @pl.loop(0, nchunk, step=2)
        def _(j):
            for b in (0, 1):
                jj = j + b
                gather(jj, b).wait()
                write(jj, b).start()

                @pl.when(jj + 1 < nchunk)
                def _():
                    @pl.when(jj >= 1)
                    def _():
                        write(jj - 1, 1 - b).wait()
                    gather(jj + 1, 1 - b).start()

        write(nchunk - 2, 0).wait()
        write(nchunk - 1, 1).wait()

    return k(table3, idx)


def _combine_kernel(wn_ref, x_ref, mod_ref, gpre_ref, gp_ref, wsg_ref, wsu_ref, wsd_ref, y_ref, o_ref, eye_ref):
    @pl.when(pl.program_id(0) == 0)
    def _():
        r = lax.broadcasted_iota(I32, (TD, TD), 0)
        c = lax.broadcasted_iota(I32, (TD, TD), 1)
        eye_ref[...] = jnp.where(r == c, 1.0, 0.0).astype(BF16)

    m = mod_ref[0, 0]
    x = x_ref[...]
    hb = (_rms(x, gpre_ref[...]) * (1.0 + m[4:5]) + m[3:4]).astype(BF16)
    hid = (_silu(jnp.dot(hb, wsg_ref[...], preferred_element_type=F32))
           * jnp.dot(hb, wsu_ref[...], preferred_element_type=F32))
    acc = jnp.dot(hid.astype(BF16), wsd_ref[...], preferred_element_type=F32)

    eye = eye_ref[...]
    nt = lambda a, b: lax.dot_general(a, b, (((1,), (1,)), ((), ())), preferred_element_type=F32)
    w1, w2, w3 = _split3(wn_ref[...])
    w_t = nt(eye, w1) + nt(eye, w2) + nt(eye, w3)

    acc_hi = acc[:, :PACKED]
    acc_lo = acc[:, PACKED:]
    for k in range(TOP_K):
        y_hi, y_lo = _unpack_rows(_load_token_tiles(y_ref, TD, k * TD))
        acc_hi = acc_hi + y_hi * w_t[:, k:k + 1]
        acc_lo = acc_lo + y_lo * w_t[:, k:k + 1]
    acc = jnp.concatenate([acc_hi, acc_lo], axis=1)
    o_ref[...] = x + m[5:6] * _rms(acc, gp_ref[...])


def _combine(wn, x, mod, l, g_pre, g_post, ws_gate, ws_up, ws_down, ybuf, first_tok, n_tok):
    off = first_tok // TD
    tiles_per_dec = DEC_SEQ // TD
    npd = TP // TD
    mod_row = lambda i: jnp.where(i + off < npd, 0, 1 + (i + off - npd) // tiles_per_dec)
    full = lambda *shape: pl.BlockSpec(shape, lambda i: (0,) * len(shape))
    y_spec = pl.BlockSpec((TOP_K * TD * ROWS_PER_TOKEN, LANES), lambda i: (i, 0))
    return pl.pallas_call(
        _combine_kernel,
        grid=(n_tok // TD,),
        in_specs=[
            pl.BlockSpec((TOP_K, TD), lambda i: (0, i + off)),
            pl.BlockSpec((TD, D), lambda i: (i + off, 0)),
            pl.BlockSpec((1, 1, 6, D), lambda i: (l, mod_row(i), 0, 0)),
            full(1, D), full(1, D), full(D, D_SHARED), full(D, D_SHARED), full(D_SHARED, D),
            y_spec,
        ],
        out_specs=pl.BlockSpec((TD, D), lambda i: (i, 0)),
        out_shape=jax.ShapeDtypeStruct((n_tok, D), F32),
        scratch_shapes=[pltpu.VMEM((TD, TD), BF16)],
        compiler_params=_params(("arbitrary",)),
        name="moe_combine",
    )(wn, x, mod, g_pre, g_post, ws_gate, ws_up, ws_down, ybuf)


def _moe_layer(x, h, mod, l, g_pre, g_post, w_router, e_bias, w_gate, w_up, w_down,
               ws_gate, ws_up, ws_down):
    top_e, wn, rk, cnt = _router(h, w_router.T, e_bias.reshape(N_EXPERTS, 1))
    cnt = cnt.reshape(N_EXPERTS).astype(I32)
    padded = (cnt + GM_SUB - 1) // GM_SUB * GM_SUB
    ends = jnp.cumsum(padded)
    offs = ends - padded
    eid = jnp.arange(N_EXPERTS, dtype=I32)[:, None, None]
    slot = rk + jnp.sum(jnp.where(top_e[None] == eid, offs[:, None, None], 0), axis=0)
    ntile = (cnt + GM - 1) // GM
    tile_ends = jnp.cumsum(ntile)
    tid = jnp.arange(NT_MAX, dtype=I32)
    last = jnp.maximum(tile_ends[-1] - 1, 0)
    tile_e = jnp.minimum(jnp.sum((tid[:, None] >= tile_ends[None, :]).astype(I32), axis=1), N_EXPERTS - 1)
    tile_e = jnp.where(tid <= last, tile_e, tile_e[last])
    own = tile_e[:, None] == jnp.arange(N_EXPERTS, dtype=I32)[None, :]
    pick = lambda per_expert: jnp.sum(jnp.where(own, per_expert[None, :], 0), axis=1)
    t_in = tid - pick(tile_ends - ntile)
    tile_blk = jnp.where(tid <= last, (pick(offs) + GM * t_in) // GM_SUB, 0)
    tile_rows = jnp.clip(pick(cnt) - GM * t_in, 0, GM)
    tile_nsub = jnp.where(tid <= last, (tile_rows + GM_SUB - 1) // GM_SUB, 0)
    starts = jnp.logical_and(jnp.arange(NT_MAX) <= last,
                             jnp.concatenate([jnp.ones((1,), bool), tile_e[1:] != tile_e[:-1]]))
    tile_run = jnp.where(starts, (jnp.cumsum(starts.astype(I32)) - 1) % 2, -1)
    later = jnp.logical_and(jnp.arange(N_EXPERTS, dtype=I32)[None, :] > tile_e[:, None], (cnt > 0)[None, :])
    tile_next = jnp.min(jnp.where(later, jnp.arange(N_EXPERTS, dtype=I32)[None, :], N_EXPERTS), axis=1)
    tile_next = jnp.where(tile_next < N_EXPERTS, tile_next, -1)
    slot3 = slot.reshape(TOP_K, T // SC_W, SC_W).transpose(1, 0, 2)
    xg = _sc_dispatch(h.reshape(T, ROWS_PER_TOKEN, LANES), slot3)
    xg = _pad_fill(offs + cnt, padded - cnt, xg).reshape(SP * ROWS_PER_TOKEN, LANES)
    yg = _gmm(tile_e, tile_blk, tile_nsub.astype(I32), tile_run.astype(I32), tile_next.astype(I32),
              xg, l, w_gate, w_up, w_down)
    yg3 = yg.reshape(SP, ROWS_PER_TOKEN, LANES)
    ws = (ws_gate.astype(BF16), ws_up.astype(BF16), ws_down.astype(BF16))
    outs = []
    for first_tok, n_tok in ((0, TP), (TP, TS)):
        ids = slot[:, first_tok:first_tok + n_tok].reshape(TOP_K, n_tok // TD, TD).transpose(1, 0, 2)
        ybuf = _sc_gather(yg3, ids.reshape(TOP_K * n_tok))
        outs.append(_combine(wn, x, mod, l, g_pre, g_post, *ws,
                             ybuf.reshape(TOP_K * n_tok * ROWS_PER_TOKEN, LANES), first_tok, n_tok))
    return outs


def _rope_tables():
    n = DEC_SEQ
    rows = n // GRID_W
    row = jnp.repeat(jnp.arange(rows), GRID_W).astype(F32)
    col = jnp.tile(jnp.arange(GRID_W), rows).astype(F32)
    half = DQK_B // 2
    inv = ROPE_BASE ** (-jnp.arange(0, half, 2, dtype=F32) / half)
    ang_r = row[:, None] * inv
    ang_c = col[:, None] * inv
    ang = jnp.concatenate([ang_r, ang_r, ang_c, ang_c], axis=-1)
    reps = QK_B // DQK_B
    return jnp.tile(jnp.cos(ang), (1, reps)), jnp.tile(jnp.sin(ang), (1, reps))


def _pad_in_proj(w):
    s = [0, Q_A, 2 * Q_A, 2 * Q_A + V_A, 2 * Q_A + 2 * V_A]
    s += [s[-1] + GATE_RANK, s[-1] + 2 * GATE_RANK]
    s += [s[-1] + QK_B, s[-1] + 2 * QK_B, s[-1] + 2 * QK_B + V_B]
    gates = jnp.pad(w[:, s[4]:s[6]], ((0, 0), (0, GL_PAD - 2 * GATE_RANK)))
    return jnp.concatenate([w[:, s[0]:s[4]], gates, w[:, s[6]:s[9]]], axis=1).astype(BF16)


def kernel(x_prompt, x_sample, c, c_ctx, state_gla, cache_k, cache_v, ada_w, ada_b, norm_pre_mix, norm_post_mix, norm_pre_ffn, norm_post_ffn, ab_w_in, gla_w_g2, gla_b_g2, gla_norm_g, diff_lambda, diff_norm_g, ab_w_out, sgu_w_in, sgu_b_in, sgu_norm_g, sgu_w_s, sgu_b_s, sgu_w_out, moe_w_router, moe_e_bias, moe_w_gate, moe_w_up, moe_w_down, moe_ws_gate, moe_ws_up, moe_ws_down):
    depth = ada_w.shape[0]
    assert x_prompt.shape == (BATCH, SEQ, D) and x_sample.shape == (DEC_BATCH, DEC_SEQ, D)
    assert state_gla.shape == (DEC_BATCH, (depth + 1) // 2, 2, H_A, DK_A, DV_A)
    assert cache_k.shape == (DEC_BATCH, 1, H_B, 2, PAST_LEN, DQK_B) and cache_v.shape == (DEC_BATCH, 1, H_B, PAST_LEN, DV_B)
    assert depth == 2 and moe_w_gate.shape == (depth, N_EXPERTS, D, D_EXPERT)
    xp, xs = x_prompt.reshape(TP, D), x_sample.reshape(TS, D)
    cond = jnp.concatenate([c_ctx[None, :], c, jnp.zeros((8 - 1 - DEC_BATCH, D), F32)], axis=0)
    mod = _modulation(cond, ada_w, ada_b)
    cos, sin = _rope_tables()
    vec = lambda a: a.reshape(1, -1)
    new_s = new_k = new_v = None
    for l in range(depth):
        if l % 2 == 0:
            e = l // 2
            lam_init = 0.8 - 0.6 * math.exp(-0.3 * l)
            a, r_a, gl, q_b, k_b, v_b, new_k, new_v = _in_proj(xp, xs, mod, l, vec(norm_pre_mix[l]),
                                                               _pad_in_proj(ab_w_in[e]), cos, sin)
            s0_t = jnp.swapaxes(state_gla[:, e], -1, -2)
            same_head = jnp.eye(H_A, dtype=bool)[None, None, :, None, :, None]
            s0_t = jnp.where(same_head, s0_t[:, :, :, :, None, :], 0.0).reshape(DEC_BATCH, 2, V_A, Q_A)
            o_f, o_bw, s_fin_t = _gla(a, gl, gla_w_g2[e], gla_b_g2[e].reshape(2, 1, Q_A), s0_t)
            o_att = _diff_attention(q_b, k_b, v_b, cache_k, cache_v, diff_lambda[e], lam_init)
            x, h = _mix_out(lam_init, o_f, o_bw, r_a, o_att, xp, xs, mod, l, vec(gla_norm_g[e]), vec(diff_norm_g[e]),
                            ab_w_out[e].astype(BF16), vec(norm_post_mix[l]), vec(norm_pre_ffn[l]))
            new_s = jnp.swapaxes(s_fin_t, -1, -2)[:, None]
        else:
            o = l // 2
            x, h = _sgu(xp, xs, mod, l, vec(norm_pre_mix[l]), sgu_w_in[o].astype(BF16), vec(sgu_b_in[o]),
                        vec(sgu_norm_g[o]), sgu_w_s[o], sgu_b_s[o].T, sgu_w_out[o].astype(BF16),
                        vec(norm_post_mix[l]), vec(norm_pre_ffn[l]))
        xp, xs = _moe_layer(x, h, mod, l, vec(norm_pre_ffn[l]), vec(norm_post_ffn[l]), moe_w_router[l], moe_e_bias[l],
                            moe_w_gate, moe_w_up, moe_w_down, moe_ws_gate[l], moe_ws_up[l], moe_ws_down[l])
    y_prompt = xp.reshape(BATCH, SEQ, D)
    y_sample = xs.reshape(DEC_BATCH, DEC_SEQ, D)
    return (y_prompt, y_sample, new_s, new_k, new_v)
```

```python
import functools
import math

import jax
import jax.numpy as jnp
from jax import lax
from jax.experimental import pallas as pl
from jax.experimental.pallas import tpu as pltpu
from jax.experimental.pallas import tpu_sc as plsc

F32 = jnp.float32
BF16 = jnp.bfloat16
I32 = jnp.int32

D = 1024
BATCH, SEQ = 32, 256
DEC_BATCH, DEC_SEQ = 4, 2048
PAST_LEN = 256
GRID_W = 64
EPS = 1e-6
TP = BATCH * SEQ
TS = DEC_BATCH * DEC_SEQ
T = TP + TS
H_A, DK_A, DV_A = 4, 64, 128
Q_A, V_A = H_A * DK_A, H_A * DV_A
GATE_RANK, GATE_TAU, GLA_CHUNK = 16, 16.0, 64
H_B, DQK_B, DV_B = 4, 64, 128
QK_B, V_B = H_B * 2 * DQK_B, H_B * DV_B
ROPE_BASE = 10000.0
SGU_DIM, SGU_GROUPS, SGU_CHUNK = 1024, 4, 128
N_EXPERTS, TOP_K, N_GROUPS, TOPK_GROUPS = 64, 8, 8, 4
GROUP_SIZE = N_EXPERTS // N_GROUPS
D_EXPERT, D_SHARED = 256, 256
ROUTED_SCALE = 2.5

TM = 512
TM_SUB = 256
MIX_DTYPE = jnp.bfloat16
NPT = TP // TM
TILES_PER_DEC = DEC_SEQ // TM
SEG = 256
NSEG = T // SEG
NSEG_P = TP // SEG
SEG_PER_DEC = DEC_SEQ // SEG
TR = 512
TD = 512
GM = 2048
GM_SUB = 256
NT_MAX = T * TOP_K // GM + N_EXPERTS
SP = T * TOP_K + N_EXPERTS * GM_SUB
GL_PAD = 128
VMEM_LIMIT = 56 * 1024 * 1024
NEG_INF = float("-inf")


def _bdot(a, b):
    return jnp.dot(a.astype(BF16), b.astype(BF16), preferred_element_type=F32)


def _bdot_nt(a, b):
    return lax.dot_general(a.astype(BF16), b.astype(BF16), (((1,), (1,)), ((), ())),
                           preferred_element_type=F32)


def _bdot_tn(a, b):
    return lax.dot_general(a.astype(BF16), b.astype(BF16), (((0,), (0,)), ((), ())),
                           preferred_element_type=F32)


def _split3(x):
    x1 = x.astype(BF16)
    r1 = x - x1.astype(F32)
    x2 = r1.astype(BF16)
    x3 = (r1 - x2.astype(F32)).astype(BF16)
    return x1, x2, x3


def _rms(x, g):
    return x * lax.rsqrt(jnp.mean(x * x, axis=-1, keepdims=True) + EPS) * g


def _silu(x):
    return x * jax.nn.sigmoid(x)


def _mod_row(i):
    return jnp.where(i < NPT, 0, 1 + (i - NPT) // TILES_PER_DEC)


def _params(sem, limit=VMEM_LIMIT):
    return pltpu.CompilerParams(dimension_semantics=sem, vmem_limit_bytes=limit)


def _mod_kernel(c_ref, w_ref, b_ref, o_ref):
    o_ref[0] = _bdot(_silu(c_ref[...]), w_ref[0]) + b_ref[0]


def _modulation(cond, ada_w, ada_b):
    depth = ada_w.shape[0]
    nj = 6
    out = pl.pallas_call(
        _mod_kernel,
        grid=(depth, nj),
        in_specs=[
            pl.BlockSpec((8, D), lambda l, j: (0, 0)),
            pl.BlockSpec((1, D, D), lambda l, j: (l, 0, j)),
            pl.BlockSpec((1, 1, D), lambda l, j: (l, 0, j)),
        ],
        out_specs=pl.BlockSpec((1, 8, D), lambda l, j: (l, 0, j)),
        out_shape=jax.ShapeDtypeStruct((depth, 8, 6 * D), F32),
        compiler_params=_params(("arbitrary", "arbitrary")),
        name="adaln_modulation",
    )(cond, ada_w, ada_b.reshape(depth, 1, 6 * D))
    return out.reshape(depth, 8, 6, D)


_C_A = 0
_C_R = _C_A + 2 * Q_A + V_A
_C_GL = _C_R + V_A
_C_Q = _C_GL + GL_PAD
_C_K = _C_Q + QK_B
_C_V = _C_K + QK_B
_C_END = _C_V + V_B
ROT_PAIR = DQK_B // 4


def _rope(x, cos, sin):
    lane = lax.broadcasted_iota(I32, x.shape, 1)
    first = (lane % (2 * ROT_PAIR)) < ROT_PAIR
    n = x.shape[1]
    xr = jnp.where(first, -pltpu.roll(x, n - ROT_PAIR, 1), pltpu.roll(x, ROT_PAIR, 1))
    return x * cos + xr * sin


def _stream_specs():
    return [pl.BlockSpec((TM, D), lambda i: (jnp.minimum(i, NPT - 1), 0)),
            pl.BlockSpec((TM, D), lambda i: (jnp.maximum(i - NPT, 0), 0))]


def _stream_tile(xp_ref, xs_ref):
    return jnp.where(pl.program_id(0) < NPT, xp_ref[...], xs_ref[...])


def _in_kernel(xp_ref, xs_ref, mod_ref, g_ref, w_ref, cos_ref, sin_ref,
               a_ref, r_ref, gl_ref, q_ref, k_ref, v_ref, ck_ref, cv_ref, k32_ref, v32_ref):
    latent = pl.program_id(0) >= NPT
    m = mod_ref[0, 0]
    x = _stream_tile(xp_ref, xs_ref)
    for s in range(TM // TM_SUB):
        rows = slice(s * TM_SUB, (s + 1) * TM_SUB)
        h = _rms(x[rows], g_ref[...]) * (1.0 + m[1:2]) + m[0:1]
        hb = h.astype(BF16)

        def proj(c0, c1):
            return jnp.dot(hb, w_ref[:, c0:c1], preferred_element_type=F32)

        a_ref[rows, :] = proj(_C_A, _C_R)
        r_ref[rows, :] = proj(_C_R, _C_GL).astype(MIX_DTYPE)
        gl_ref[rows, :] = proj(_C_GL, _C_Q)
        v = proj(_C_V, _C_END)
        q = proj(_C_Q, _C_K)
        k = proj(_C_K, _C_V)
        cos = cos_ref[rows, :]
        sin = sin_ref[rows, :]
        q_ref[rows, :] = jnp.where(latent, _rope(q, cos, sin), q).astype(BF16)
        k_ref[rows, :] = jnp.where(latent, _rope(k, cos, sin), k).astype(BF16)
        v_ref[rows, :] = v.astype(BF16)
        k32_ref[rows, :] = k
        v32_ref[rows, :] = v

    @pl.when(jnp.logical_not(latent))
    def _():
        for s in range(TM // SEQ):
            rows = slice(s * SEQ, (s + 1) * SEQ)
            for h in range(H_B):
                cv_ref[s, 0, h] = v32_ref[rows, h * DV_B:(h + 1) * DV_B]
                for mp in range(2):
                    ck_ref[s, 0, h, mp] = k32_ref[rows, (2 * h + mp) * DQK_B:(2 * h + mp + 1) * DQK_B]


def _in_proj(xp, xs, mod, l, g, w_pad, cos, sin):
    tok = lambda width: pl.BlockSpec((TM, width), lambda i: (i, 0))
    rope_spec = pl.BlockSpec((TM, QK_B), lambda i: (jnp.maximum(i - NPT, 0) % TILES_PER_DEC, 0))
    widths = (2 * Q_A + V_A, V_A, GL_PAD, QK_B, QK_B, V_B)
    dtypes = (F32, MIX_DTYPE, F32, BF16, BF16, BF16)
    seqs = TM // SEQ
    prompt_tile = lambda i: jnp.minimum(i, NPT - 1)
    cache_specs = [pl.BlockSpec((seqs, 1, H_B, 2, SEQ, DQK_B), lambda i: (prompt_tile(i), 0, 0, 0, 0, 0)),
                   pl.BlockSpec((seqs, 1, H_B, SEQ, DV_B), lambda i: (prompt_tile(i), 0, 0, 0, 0))]
    cache_shapes = [jax.ShapeDtypeStruct((BATCH, 1, H_B, 2, SEQ, DQK_B), F32),
                    jax.ShapeDtypeStruct((BATCH, 1, H_B, SEQ, DV_B), F32)]
    return pl.pallas_call(
        _in_kernel,
        grid=(T // TM,),
        in_specs=_stream_specs() + [
            pl.BlockSpec((1, 1, 6, D), lambda i: (l, _mod_row(i), 0, 0)),
            pl.BlockSpec((1, D), lambda i: (0, 0)),
            pl.BlockSpec((D, _C_END), lambda i: (0, 0)),
            rope_spec, rope_spec,
        ],
        out_specs=[tok(w) for w in widths] + cache_specs,
        out_shape=[jax.ShapeDtypeStruct((T, w), dt) for w, dt in zip(widths, dtypes)] + cache_shapes,
        scratch_shapes=[pltpu.VMEM((TM, QK_B), F32), pltpu.VMEM((TM, V_B), F32)],
        compiler_params=_params(("arbitrary",)),
        name="mixer_ab_in_proj",
    )(xp, xs, mod, g, w_pad, cos, sin)


def _log_sigmoid(x):
    return jnp.minimum(x, 0.0) - jnp.log(1.0 + jnp.exp(-jnp.abs(x)))


def _gla_kernel(af_ref, ab_ref, glf_ref, glb_ref, wg_ref, bg_ref, s0_ref,
                of_ref, ob_ref, sfin_ref, st_ref):
    i = pl.program_id(0)

    @pl.when(i < NSEG_P)
    def _():
        st_ref[...] = jnp.zeros_like(st_ref)

    @pl.when(jnp.logical_and(i >= NSEG_P, (i - NSEG_P) % SEG_PER_DEC == 0))
    def _():
        st_ref[...] = s0_ref[0]

    r = lax.broadcasted_iota(I32, (SEG, SEG), 0)
    c = lax.broadcasted_iota(I32, (SEG, SEG), 1)
    same = (r // GLA_CHUNK) == (c // GLA_CHUNK)
    nchunk = SEG // GLA_CHUNK
    own_head = (lax.broadcasted_iota(I32, (V_A, Q_A), 0) // DV_A) == (lax.broadcasted_iota(I32, (V_A, Q_A), 1) // DK_A)

    for d, (a_ref, gl_ref, o_ref) in enumerate(((af_ref, glf_ref, of_ref), (ab_ref, glb_ref, ob_ref))):
        fwd = d == 0
        gcol = gl_ref[:, d * GATE_RANK:(d + 1) * GATE_RANK]
        la = _log_sigmoid(_bdot(gcol, wg_ref[d]) + bg_ref[d]) / GATE_TAU
        causal = jnp.logical_and(same, (c <= r) if fwd else (c >= r))
        tri = jnp.where(causal, 1.0, 0.0).astype(BF16)
        l1, l2, l3 = _split3(la)
        b_all = (jnp.dot(tri, l1, preferred_element_type=F32)
                 + jnp.dot(tri, l2, preferred_element_type=F32)
                 + jnp.dot(tri, l3, preferred_element_type=F32))
        q_in_all = a_ref[:, 0:Q_A] * (DK_A ** -0.5) * jnp.exp(b_all)
        kd_all = a_ref[:, Q_A:2 * Q_A] * jnp.exp(-b_all)
        intra = []
        for h in range(H_A):
            kc = slice(h * DK_A, (h + 1) * DK_A)
            attn = jnp.where(causal, _bdot_nt(q_in_all[:, kc], kd_all[:, kc]), 0.0)
            intra.append(_bdot(attn, a_ref[:, 2 * Q_A + h * DV_A:2 * Q_A + (h + 1) * DV_A]))
        intra = jnp.concatenate(intra, axis=1)
        state = st_ref[d]
        order = range(nchunk) if fwd else range(nchunk - 1, -1, -1)
        for ch in order:
            r0 = ch * GLA_CHUNK
            rows = slice(r0, r0 + GLA_CHUNK)
            end = r0 + GLA_CHUNK - 1 if fwd else r0
            b_end = b_all[end:end + 1, :]
            kw = a_ref[rows, Q_A:2 * Q_A] * jnp.exp(b_end - b_all[rows, :])
            o_ref[rows, :] = (intra[rows, :] + _bdot_nt(q_in_all[rows, :], state)).astype(MIX_DTYPE)
            kv_t = _bdot_tn(a_ref[rows, 2 * Q_A:2 * Q_A + V_A], kw)
            state = state * jnp.exp(b_end) + jnp.where(own_head, kv_t, 0.0)
        st_ref[d] = state

    @pl.when(i < NSEG_P)
    def _():
        for d in range(2):
            for h in range(H_A):
                sfin_ref[0, d, h] = st_ref[d, h * DV_A:(h + 1) * DV_A, h * DK_A:(h + 1) * DK_A]


def _seg_bwd(i):
    j = i - NSEG_P
    return jnp.where(i < NSEG_P, i, NSEG_P + (j // SEG_PER_DEC) * SEG_PER_DEC + (SEG_PER_DEC - 1 - j % SEG_PER_DEC))


def _gla(a, gl, wg, bg, s0_t):
    seg = lambda width, f: pl.BlockSpec((SEG, width), lambda i: (f(i), 0))
    ident = lambda i: i
    st_block = (1, 2, H_A, DV_A, DK_A)
    return pl.pallas_call(
        _gla_kernel,
        grid=(NSEG,),
        in_specs=[
            seg(D, ident), seg(D, _seg_bwd), seg(GL_PAD, ident), seg(GL_PAD, _seg_bwd),
            pl.BlockSpec((2, GATE_RANK, Q_A), lambda i: (0, 0, 0)),
            pl.BlockSpec((2, 1, Q_A), lambda i: (0, 0, 0)),
            pl.BlockSpec((1, 2, V_A, Q_A), lambda i: (jnp.maximum(i - NSEG_P, 0) // SEG_PER_DEC, 0, 0, 0)),
        ],
        out_specs=[
            seg(V_A, ident), seg(V_A, _seg_bwd),
            pl.BlockSpec(st_block, lambda i: (jnp.minimum(i, NSEG_P - 1), 0, 0, 0, 0)),
        ],
        out_shape=[
            jax.ShapeDtypeStruct((T, V_A), MIX_DTYPE),
            jax.ShapeDtypeStruct((T, V_A), MIX_DTYPE),
            jax.ShapeDtypeStruct((BATCH, 2, H_A, DV_A, DK_A), F32),
        ],
        scratch_shapes=[pltpu.VMEM((2, V_A, Q_A), F32)],
        compiler_params=_params(("arbitrary",)),
        name="gla_bidir",
    )(a, a, gl, gl, wg, bg, s0_t)


def _diff_lambda(lam_ref, lam_init):
    lp = lam_ref[...]
    s01 = jnp.sum(lp[0:1] * lp[1:2], axis=1, keepdims=True)
    s23 = jnp.sum(lp[2:3] * lp[3:4], axis=1, keepdims=True)
    return jnp.exp(s01) - jnp.exp(s23) + lam_init


def _attn_prompt_kernel(lam_init, q_ref, k_ref, v_ref, lam_ref, o_ref):
    lam = _diff_lambda(lam_ref, lam_init)
    for h in range(H_B):
        ps = []
        for m in range(2):
            cols = slice((2 * h + m) * DQK_B, (2 * h + m + 1) * DQK_B)
            s = _bdot_nt(q_ref[:, cols] * (DQK_B ** -0.5), k_ref[:, cols])
            e = jnp.exp(s - jnp.max(s, axis=1, keepdims=True))
            ps.append(e * (1.0 / jnp.sum(e, axis=1, keepdims=True)))
        w = ps[0] - lam * ps[1]
        o_ref[:, h * DV_B:(h + 1) * DV_B] = _bdot(w, v_ref[:, h * DV_B:(h + 1) * DV_B]).astype(MIX_DTYPE)


def _attn_sample_kernel(lam_init, q_ref, k_ref, v_ref, ck_ref, cv_ref, lam_ref, o_ref):
    lam = _diff_lambda(lam_ref, lam_init)
    for h in range(H_B):
        parts = []
        for m in range(2):
            cols = slice((2 * h + m) * DQK_B, (2 * h + m + 1) * DQK_B)
            q = q_ref[:, cols] * (DQK_B ** -0.5)
            sc = _bdot_nt(q, ck_ref[0, 0, h, m])
            sn = _bdot_nt(q, k_ref[:, cols])
            mx = jnp.maximum(jnp.max(sc, axis=1, keepdims=True), jnp.max(sn, axis=1, keepdims=True))
            ec = jnp.exp(sc - mx)
            en = jnp.exp(sn - mx)
            inv = (1.0 if m == 0 else -lam) / (jnp.sum(ec, axis=1, keepdims=True) + jnp.sum(en, axis=1, keepdims=True))
            parts.append((ec * inv, en * inv))
        wc = parts[0][0] + parts[1][0]
        wn = parts[0][1] + parts[1][1]
        o_ref[:, h * DV_B:(h + 1) * DV_B] = (_bdot(wc, cv_ref[0, 0, h])
                                             + _bdot(wn, v_ref[:, h * DV_B:(h + 1) * DV_B])).astype(MIX_DTYPE)


QB = SEQ
NQB_DEC = DEC_SEQ // QB


def _attn_kernel(lam_init, q_ref, kp_ref, vp_ref, ks_ref, vs_ref, ck_ref, cv_ref, lam_ref, o_ref):
    i = pl.program_id(0)

    @pl.when(i < BATCH)
    def _():
        _attn_prompt_kernel(lam_init, q_ref, kp_ref, vp_ref, lam_ref, o_ref)

    @pl.when(i >= BATCH)
    def _():
        _attn_sample_kernel(lam_init, q_ref, ks_ref, vs_ref, ck_ref, cv_ref, lam_ref, o_ref)


def _diff_attention(q, k, v, cache_k, cache_v, lam_p, lam_init):
    blk = lambda rows, f: pl.BlockSpec((rows, QK_B), f)
    dec_b = lambda i: jnp.maximum(i - BATCH, 0) // NQB_DEC
    own = lambda i: (i, 0)
    prompt_kv = lambda i: (jnp.minimum(i, BATCH - 1), 0)
    dec_kv = lambda i: (TP // DEC_SEQ + dec_b(i), 0)
    return pl.pallas_call(
        functools.partial(_attn_kernel, lam_init),
        grid=(BATCH + DEC_BATCH * NQB_DEC,),
        in_specs=[
            blk(QB, own), blk(SEQ, prompt_kv), blk(SEQ, prompt_kv), blk(DEC_SEQ, dec_kv), blk(DEC_SEQ, dec_kv),
            pl.BlockSpec((1, 1, H_B, 2, PAST_LEN, DQK_B), lambda i: (dec_b(i), 0, 0, 0, 0, 0)),
            pl.BlockSpec((1, 1, H_B, PAST_LEN, DV_B), lambda i: (dec_b(i), 0, 0, 0, 0)),
            pl.BlockSpec((4, DQK_B), lambda i: (0, 0)),
        ],
        out_specs=blk(QB, own),
        out_shape=jax.ShapeDtypeStruct((T, V_B), MIX_DTYPE),
        compiler_params=_params(("arbitrary",)),
        name="diff_attention",
    )(q, k, v, k, v, cache_k, cache_v, lam_p)


def _head_rms(x, g, nheads, width):
    return jnp.concatenate([_rms(x[:, h * width:(h + 1) * width], g) for h in range(nheads)], axis=1)


def _mix_out_kernel(lam_init, of_ref, ob_ref, r_ref, oatt_ref, xp_ref, xs_ref, mod_ref,
                    gg_ref, dg_ref, wo_ref, gp_ref, gffn_ref, o_ref, hp_ref):
    m = mod_ref[0, 0]
    load = lambda ref: ref[...].astype(F32)
    o_a = _head_rms(load(of_ref) + load(ob_ref), gg_ref[...], H_A, DV_A) * _silu(load(r_ref))
    o_b = _head_rms(load(oatt_ref), dg_ref[...], H_B, DV_B) * (1.0 - lam_init)
    out = _bdot(o_a, wo_ref[0:V_A, :]) + _bdot(o_b, wo_ref[V_A:V_A + V_B, :])
    x1 = _stream_tile(xp_ref, xs_ref) + m[2:3] * _rms(out, gp_ref[...])
    o_ref[...] = x1
    _store_token_tiles(hp_ref, _ffn_input_rows(x1, m, gffn_ref[...]))


def _mix_out(lam_init, o_f, o_b, r_a, o_att, xp, xs, mod, l, gla_g, diff_g, w_o, g_post, g_ffn):
    tok = lambda width: pl.BlockSpec((TM, width), lambda i: (i, 0))
    vec = lambda width: pl.BlockSpec((1, width), lambda i: (0, 0))
    return pl.pallas_call(
        functools.partial(_mix_out_kernel, lam_init),
        grid=(T // TM,),
        in_specs=[
            tok(V_A), tok(V_A), tok(V_A), tok(V_B), *_stream_specs(),
            pl.BlockSpec((1, 1, 6, D), lambda i: (l, _mod_row(i), 0, 0)),
            vec(DV_A), vec(DV_B),
            pl.BlockSpec((V_A + V_B, D), lambda i: (0, 0)),
            vec(D), vec(D),
        ],
        out_specs=[tok(D), pl.BlockSpec((TM * ROWS_PER_TOKEN, LANES), lambda i: (i, 0))],
        out_shape=[jax.ShapeDtypeStruct((T, D), F32), jax.ShapeDtypeStruct((T * ROWS_PER_TOKEN, LANES), U32)],
        compiler_params=_params(("arbitrary",)),
        name="mixer_ab_out",
    )(o_f, o_b, r_a, o_att, xp, xs, mod, gla_g, diff_g, w_o, g_post, g_ffn)


def _gelu_tanh(x):
    return 0.5 * x * (1.0 + jnp.tanh(math.sqrt(2.0 / math.pi) * (x + 0.044715 * (x * x * x))))


def _sgu_kernel(xp_ref, xs_ref, mod_ref, gpre_ref, win_ref, bin_ref, vg_ref, ws_ref, bs_ref,
                wout_ref, gpost_ref, gffn_ref, o_ref, hp_ref, t_ref):
    m = mod_ref[0, 0]
    x = _stream_tile(xp_ref, xs_ref)
    h = _rms(x, gpre_ref[...]) * (1.0 + m[1:2]) + m[0:1]
    z = _gelu_tanh(_bdot(h, win_ref[...]) + bin_ref[...])
    v = _rms(z[:, SGU_DIM:], vg_ref[...])
    gw = SGU_DIM // SGU_GROUPS
    for ch in range(TM // SGU_CHUNK):
        rows = slice(ch * SGU_CHUNK, (ch + 1) * SGU_CHUNK)
        for g in range(SGU_GROUPS):
            cols = slice(g * gw, (g + 1) * gw)
            vs = _bdot(ws_ref[g], v[rows, cols]) + bs_ref[:, g:g + 1]
            t_ref[rows, cols] = (z[rows, cols] * vs).astype(BF16)
    out = jnp.dot(t_ref[...], wout_ref[...], preferred_element_type=F32)
    x1 = x + m[2:3] * _rms(out, gpost_ref[...])
    o_ref[...] = x1
    _store_token_tiles(hp_ref, _ffn_input_rows(x1, m, gffn_ref[...]))


def _sgu(xp, xs, mod, l, g_pre, w_in, b_in, v_g, w_s, b_s_t, w_out, g_post, g_ffn):
    tok = pl.BlockSpec((TM, D), lambda i: (i, 0))
    full = lambda *shape: pl.BlockSpec(shape, lambda i: (0,) * len(shape))
    return pl.pallas_call(
        _sgu_kernel,
        grid=(T // TM,),
        in_specs=_stream_specs() + [
            pl.BlockSpec((1, 1, 6, D), lambda i: (l, _mod_row(i), 0, 0)),
            full(1, D), full(D, 2 * SGU_DIM), full(1, 2 * SGU_DIM), full(1, SGU_DIM),
            full(SGU_GROUPS, SGU_CHUNK, SGU_CHUNK), full(SGU_CHUNK, SGU_GROUPS),
            full(SGU_DIM, D), full(1, D), full(1, D),
        ],
        out_specs=[tok, pl.BlockSpec((TM * ROWS_PER_TOKEN, LANES), lambda i: (i, 0))],
        out_shape=[jax.ShapeDtypeStruct((T, D), F32), jax.ShapeDtypeStruct((T * ROWS_PER_TOKEN, LANES), U32)],
        scratch_shapes=[pltpu.VMEM((TM, SGU_DIM), BF16)],
        compiler_params=_params(("arbitrary",)),
        name="sgu_mixer",
    )(xp, xs, mod, g_pre, w_in, b_in, v_g, w_s, b_s_t, w_out, g_post, g_ffn)


LANES = 128
U32 = jnp.uint32
PACKED = D // 2
ROWS_PER_TOKEN = PACKED // LANES
BF16_BITS = 16
HIGH_HALF = 0xFFFF0000


def _pack_rows(x):
    bits = lax.bitcast_convert_type(x.astype(BF16).astype(F32), U32)
    return bits[:, :PACKED] | (bits[:, PACKED:] >> BF16_BITS)


def _unpack_rows(u):
    return (lax.bitcast_convert_type(u & U32(HIGH_HALF), F32), lax.bitcast_convert_type(u << BF16_BITS, F32))


def _store_token_tiles(ref, u, first=0):
    n = u.shape[0]
    for c in range(ROWS_PER_TOKEN):
        ref[pl.ds(first * ROWS_PER_TOKEN + c, n, stride=ROWS_PER_TOKEN), :] = u[:, c * LANES:(c + 1) * LANES]


def _load_token_tiles(ref, n, first=0):
    return jnp.concatenate([ref[pl.ds(first * ROWS_PER_TOKEN + c, n, stride=ROWS_PER_TOKEN), :]
                            for c in range(ROWS_PER_TOKEN)], axis=1)


def _ffn_input_rows(x, m, g):
    return _pack_rows(_rms(x, g) * (1.0 + m[4:5]) + m[3:4])


def _router_kernel(hp_ref, wr_ref, eb_ref, te_ref, wn_ref, rk_ref, cnt_ref, carry_ref, upper_ref):
    i = pl.program_id(0)

    @pl.when(i == 0)
    def _():
        carry_ref[...] = jnp.zeros_like(carry_ref)
        tj = lax.broadcasted_iota(I32, (TR, TR), 0)
        ti = lax.broadcasted_iota(I32, (TR, TR), 1)
        upper_ref[...] = jnp.where(tj < ti, 1.0, 0.0).astype(BF16)

    h_hi, h_lo = (t.astype(BF16) for t in _unpack_rows(_load_token_tiles(hp_ref, TR)))
    w1, w2, _ = _split3(wr_ref[...])
    nt = lambda a, b: lax.dot_general(a, b, (((1,), (1,)), ((), ())), preferred_element_type=F32)
    logits = (nt(w1[:, :PACKED], h_hi) + nt(w1[:, PACKED:], h_lo)
              + nt(w2[:, :PACKED], h_hi) + nt(w2[:, PACKED:], h_lo))
    scores = jax.nn.sigmoid(logits)
    sel = scores + eb_ref[...]

    row8 = lax.broadcasted_iota(I32, (GROUP_SIZE, TR), 0)
    gscore = []
    for g in range(N_GROUPS):
        xg = sel[g * GROUP_SIZE:(g + 1) * GROUP_SIZE]
        m1 = jnp.max(xg, axis=0, keepdims=True)
        i1 = jnp.min(jnp.where(xg == m1, row8, GROUP_SIZE), axis=0, keepdims=True)
        m2 = jnp.max(jnp.where(row8 == i1, NEG_INF, xg), axis=0, keepdims=True)
        gscore.append(m1 + m2)
    pieces = []
    for g in range(N_GROUPS):
        rank = jnp.zeros((1, TR), I32)
        for g2 in range(N_GROUPS):
            if g2 == g:
                continue
            beats = (gscore[g2] >= gscore[g]) if g2 < g else (gscore[g2] > gscore[g])
            rank = rank + beats.astype(I32)
        pieces.append(jnp.where(rank < TOPK_GROUPS, sel[g * GROUP_SIZE:(g + 1) * GROUP_SIZE], NEG_INF))
    cur = jnp.concatenate(pieces, axis=0)

    row = lax.broadcasted_iota(I32, (N_EXPERTS, TR), 0)
    idxs, ws = [], []
    for _ in range(TOP_K):
        mx = jnp.max(cur, axis=0, keepdims=True)
        idx = jnp.min(jnp.where(cur == mx, row, N_EXPERTS), axis=0, keepdims=True)
        hit = row == idx
        ws.append(jnp.sum(jnp.where(hit, scores, 0.0), axis=0, keepdims=True))
        cur = jnp.where(hit, NEG_INF, cur)
        idxs.append(idx)
    mask = jnp.zeros((N_EXPERTS, TR), F32)
    for idx in idxs:
        mask = mask + (row == idx).astype(F32)
    wsum = ws[0]
    for wk in ws[1:]:
        wsum = wsum + wk

    pos = carry_ref[...] + jnp.dot(mask.astype(BF16), upper_ref[...], preferred_element_type=F32)
    for k in range(TOP_K):
        hit = row == idxs[k]
        te_ref[k:k + 1, :] = idxs[k]
        wn_ref[k:k + 1, :] = ws[k] / wsum * ROUTED_SCALE
        rk_ref[k:k + 1, :] = jnp.sum(jnp.where(hit, pos, 0.0), axis=0, keepdims=True).astype(I32)
    carry_ref[...] = carry_ref[...] + jnp.sum(mask, axis=1, keepdims=True)
    cnt_ref[...] = carry_ref[...]


def _router(hp, wr_t, e_bias):
    kt = lambda dtype: jax.ShapeDtypeStruct((TOP_K, T), dtype)
    kt_spec = pl.BlockSpec((TOP_K, TR), lambda i: (0, i))
    return pl.pallas_call(
        _router_kernel,
        grid=(T // TR,),
        in_specs=[
            pl.BlockSpec((TR * ROWS_PER_TOKEN, LANES), lambda i: (i, 0)),
            pl.BlockSpec((N_EXPERTS, D), lambda i: (0, 0)),
            pl.BlockSpec((N_EXPERTS, 1), lambda i: (0, 0)),
        ],
        out_specs=[
            kt_spec, kt_spec, kt_spec,
            pl.BlockSpec((N_EXPERTS, 1), lambda i: (0, 0)),
        ],
        out_shape=[
            kt(I32), kt(F32), kt(I32),
            jax.ShapeDtypeStruct((N_EXPERTS, 1), F32),
        ],
        scratch_shapes=[pltpu.VMEM((N_EXPERTS, 1), F32), pltpu.VMEM((TR, TR), BF16)],
        compiler_params=_params(("arbitrary",)),
        name="moe_router",
    )(hp, wr_t, e_bias)


_PAD_BITS = tuple(1 << b for b in range(GM_SUB.bit_length() - 1))


def _pad_fill_kernel(pad_start_ref, pad_len_ref, xg_in_ref, xg_ref, zero_ref, sem):
    del xg_in_ref
    zero_ref[...] = jnp.zeros_like(zero_ref)

    def pad_copies(e):
        start = pad_start_ref[e]
        n = pad_len_ref[e]
        copies = []
        for bit in _PAD_BITS:
            first = start + (n & ~(2 * bit - 1))
            copies.append(((n & bit) != 0, pltpu.make_async_copy(
                zero_ref.at[pl.ds(0, bit)], xg_ref.at[pl.ds(first, bit)], sem)))
        return copies

    def start_e(e, carry):
        for on, cp in pad_copies(e):
            @pl.when(on)
            def _():
                cp.start()
        return carry

    def wait_e(e, carry):
        for on, cp in pad_copies(e):
            @pl.when(on)
            def _():
                cp.wait()
        return carry

    lax.fori_loop(0, N_EXPERTS, start_e, 0)
    lax.fori_loop(0, N_EXPERTS, wait_e, 0)


def _pad_fill(pad_start, pad_len, xg):
    grid_spec = pltpu.PrefetchScalarGridSpec(
        num_scalar_prefetch=2,
        grid=(1,),
        in_specs=[pl.BlockSpec(memory_space=pl.ANY)],
        out_specs=pl.BlockSpec(memory_space=pl.ANY),
        scratch_shapes=[pltpu.VMEM((GM_SUB // 2, ROWS_PER_TOKEN, LANES), xg.dtype), pltpu.SemaphoreType.DMA],
    )
    return pl.pallas_call(
        _pad_fill_kernel,
        grid_spec=grid_spec,
        out_shape=jax.ShapeDtypeStruct(xg.shape, xg.dtype),
        input_output_aliases={2: 0},
        compiler_params=_params(("arbitrary",)),
        name="moe_pad_fill",
    )(pad_start, pad_len, xg)


SC_CORES, SC_SUBCORES = 2, 16
SC_WORKERS = SC_CORES * SC_SUBCORES
SC_W = 64


def _sc_worker_id():
    return lax.axis_index("s") * SC_CORES + lax.axis_index("c")


def _sc_dispatch(h3, slot3):
    nchunk = T // SC_WORKERS // SC_W
    mesh = plsc.VectorSubcoreMesh(core_axis_name="c", subcore_axis_name="s")
    tile = (SC_W, ROWS_PER_TOKEN, LANES)

    @functools.partial(
        pl.kernel, mesh=mesh,
        out_type=jax.ShapeDtypeStruct((SP, ROWS_PER_TOKEN, LANES), h3.dtype),
        scratch_types=[pltpu.VMEM((TOP_K, SC_W), I32), pltpu.VMEM((TOP_K, SC_W), I32),
                       pltpu.VMEM(tile, h3.dtype), pltpu.VMEM(tile, h3.dtype),
                       pltpu.SemaphoreType.DMA((2,)), pltpu.SemaphoreType.DMA((2,))],
    )
    def k(h_hbm, slot_hbm, xg_hbm, idx0, idx1, rows0, rows1, lsem, ssem):
        first = _sc_worker_id() * nchunk
        idx = (idx0, idx1)
        rows = (rows0, rows1)

        def loads(j, b):
            blk = first + j
            tok = pl.multiple_of(blk * SC_W, SC_W)
            return (pltpu.make_async_copy(slot_hbm.at[blk], idx[b], lsem.at[b]),
                    pltpu.make_async_copy(h_hbm.at[pl.ds(tok, SC_W)], rows[b], lsem.at[b]))

        def scatters(b):
            return [pltpu.make_async_copy(rows[b], xg_hbm.at[idx[b].at[kk]], ssem.at[b]) for kk in range(TOP_K)]

        for cp in loads(0, 0):
            cp.start()

        @pl.loop(0, nchunk, step=2)
        def _(j):
            for b in (0, 1):
                jj = j + b
                for cp in loads(jj, b):
                    cp.wait()
                for cp in scatters(b):
                    cp.start()

                @pl.when(jj + 1 < nchunk)
                def _():
                    @pl.when(jj >= 1)
                    def _():
                        for cp in scatters(1 - b):
                            cp.wait()
                    for cp in loads(jj + 1, 1 - b):
                        cp.start()

        for b in (0, 1):
            for cp in scatters(b):
                cp.wait()

    return k(h3, slot3)


NSUB = GM // GM_SUB
SUB_ROWS = GM_SUB * ROWS_PER_TOKEN


def _per_block_count(count, fn):
    for n in range(1, NSUB + 1):
        @pl.when(count == n)
        def _(n=n):
            fn(n)


def _gmm_kernel(layer, tile_e_ref, tile_blk_ref, tile_nsub_ref, tile_run_ref, tile_next_ref,
                xg_hbm, wg_hbm, wu_hbm, wd_hbm, yg_hbm,
                x_st, y_st, wg_st, wu_st, wd_st, wgu_scr, wd_scr, sems, xsems, ysems):
    j = pl.program_id(0)
    last_step = pl.num_programs(0) - 1
    nsub = tile_nsub_ref[j]
    run = tile_run_ref[j]
    slot = j % 2
    x_ref = x_st.at[slot]
    y_ref = y_st.at[slot]

    def tile_rows(t, n):
        return pl.ds(pl.multiple_of(tile_blk_ref[t] * SUB_ROWS, SUB_ROWS), n * SUB_ROWS)

    def x_copy(t, n):
        return pltpu.make_async_copy(xg_hbm.at[tile_rows(t, n)], x_st.at[t % 2, pl.ds(0, n * SUB_ROWS)], xsems.at[t % 2])

    def y_copy(t, n):
        return pltpu.make_async_copy(y_st.at[t % 2, pl.ds(0, n * SUB_ROWS)], yg_hbm.at[tile_rows(t, n)], ysems.at[t % 2])

    @pl.when(j == 0)
    def _():
        _per_block_count(nsub, lambda n: x_copy(j, n).start())

    _per_block_count(nsub, lambda n: x_copy(j, n).wait())

    @pl.when(j < last_step)
    def _():
        _per_block_count(tile_nsub_ref[j + 1], lambda n: x_copy(j + 1, n).start())

    @pl.when(j >= 2)
    def _():
        _per_block_count(tile_nsub_ref[j - 2], lambda n: y_copy(j - 2, n).wait())

    def weight_copies(e, slot):
        return [pltpu.make_async_copy(src.at[layer, e], dst.at[slot], sems.at[slot])
                for src, dst in ((wg_hbm, wg_st), (wu_hbm, wu_st), (wd_hbm, wd_st))]

    @pl.when(run >= 0)
    def _():
        @pl.when(j == 0)
        def _():
            for cp in weight_copies(tile_e_ref[j], run):
                cp.start()

        for cp in weight_copies(tile_e_ref[j], run):
            cp.wait()

        nxt = tile_next_ref[j]

        @pl.when(nxt >= 0)
        def _():
            for cp in weight_copies(nxt, 1 - run):
                cp.start()

        wgu_scr[:, 0:D_EXPERT] = wg_st[run].astype(BF16)
        wgu_scr[:, D_EXPERT:2 * D_EXPERT] = wu_st[run].astype(BF16)
        wd_scr[...] = wd_st[run].astype(BF16)

    def expert_mlp(s):
        x_hi, x_lo = _unpack_rows(_load_token_tiles(x_ref, GM_SUB, s * GM_SUB))
        gu = (jnp.dot(x_hi.astype(BF16), wgu_scr[0:PACKED, :], preferred_element_type=F32)
              + jnp.dot(x_lo.astype(BF16), wgu_scr[PACKED:D, :], preferred_element_type=F32))
        hid = _silu(gu[:, 0:D_EXPERT]) * gu[:, D_EXPERT:2 * D_EXPERT]
        y = jnp.dot(hid.astype(BF16), wd_scr[...], preferred_element_type=F32)
        _store_token_tiles(y_ref, _pack_rows(y), s * GM_SUB)

    def compute_and_send(n):
        for s in range(n):
            expert_mlp(s)
        y_copy(j, n).start()

    _per_block_count(nsub, compute_and_send)

    @pl.when(j == last_step)
    def _():
        _per_block_count(tile_nsub_ref[jnp.maximum(j - 1, 0)], lambda n: y_copy(j - 1, n).wait())
        _per_block_count(nsub, lambda n: y_copy(j, n).wait())


def _gmm(tile_e, tile_blk, tile_nsub, tile_run, tile_next, xg, l, w_gate, w_up, w_down):
    hbm = pl.BlockSpec(memory_space=pl.ANY)
    stage = pltpu.VMEM((2, GM * ROWS_PER_TOKEN, LANES), U32)
    grid_spec = pltpu.PrefetchScalarGridSpec(
        num_scalar_prefetch=5,
        grid=(NT_MAX,),
        in_specs=[hbm, hbm, hbm, hbm],
        out_specs=hbm,
        scratch_shapes=[stage, stage,
                        pltpu.VMEM((2, D, D_EXPERT), F32), pltpu.VMEM((2, D, D_EXPERT), F32),
                        pltpu.VMEM((2, D_EXPERT, D), F32),
                        pltpu.VMEM((D, 2 * D_EXPERT), BF16), pltpu.VMEM((D_EXPERT, D), BF16),
                        pltpu.SemaphoreType.DMA((2,)), pltpu.SemaphoreType.DMA((2,)), pltpu.SemaphoreType.DMA((2,))],
    )
    return pl.pallas_call(
        functools.partial(_gmm_kernel, l),
        grid_spec=grid_spec,
        out_shape=jax.ShapeDtypeStruct((SP * ROWS_PER_TOKEN, LANES), U32),
        compiler_params=_params(("arbitrary",)),
        name="moe_grouped_matmul",
    )(tile_e, tile_blk, tile_nsub, tile_run, tile_next, xg, w_gate, w_up, w_down)


def _sc_gather(table3, idx):
    n_idx = idx.shape[0]
    per_w = n_idx // SC_WORKERS
    nchunk = per_w // SC_W
    mesh = plsc.VectorSubcoreMesh(core_axis_name="c", subcore_axis_name="s")
    tile = (SC_W, ROWS_PER_TOKEN, LANES)

    @functools.partial(
        pl.kernel, mesh=mesh,
        out_type=jax.ShapeDtypeStruct((n_idx, ROWS_PER_TOKEN, LANES), table3.dtype),
        scratch_types=[pltpu.VMEM((per_w,), I32), pltpu.VMEM(tile, table3.dtype), pltpu.VMEM(tile, table3.dtype),
                       pltpu.SemaphoreType.DMA((2,)), pltpu.SemaphoreType.DMA((2,))],
    )
    def k(table_hbm, idx_hbm, out_hbm, idx_v, rows0, rows1, gsem, wsem):
        base = pl.multiple_of(_sc_worker_id() * per_w, per_w)
        rows = (rows0, rows1)
        pltpu.sync_copy(idx_hbm.at[pl.ds(base, per_w)], idx_v)

        def gather(j, b):
            ids = idx_v.at[pl.ds(pl.multiple_of(j * SC_W, SC_W), SC_W)]
            return pltpu.make_async_copy(table_hbm.at[ids], rows[b], gsem.at[b])

        def write(j, b):
            dst = out_hbm.at[pl.ds(pl.multiple_of(base + j * SC_W, SC_W), SC_W)]
            return pltpu.make_async_copy(rows[b], dst, wsem.at[b])

        gather(0, 0).start()

        @pl.loop(0, nchunk, step=2)
        def _(j):
            for b in (0, 1):
                jj = j + b
                gather(jj, b).wait()
                write(jj, b).start()

                @pl.when(jj + 1 < nchunk)
                def _():
                    @pl.when(jj >= 1)
                    def _():
                        write(jj - 1, 1 - b).wait()
                    gather(jj + 1, 1 - b).start()

        write(nchunk - 2, 0).wait()
        write(nchunk - 1, 1).wait()

    return k(table3, idx)


def _combine_kernel(wn_ref, x_ref, mod_ref, gpre_ref, gp_ref, wsg_ref, wsu_ref, wsd_ref, y_ref, o_ref, eye_ref):
    @pl.when(pl.program_id(0) == 0)
    def _():
        r = lax.broadcasted_iota(I32, (TD, TD), 0)
        c = lax.broadcasted_iota(I32, (TD, TD), 1)
        eye_ref[...] = jnp.where(r == c, 1.0, 0.0).astype(BF16)

    m = mod_ref[0, 0]
    x = x_ref[...]
    hb = (_rms(x, gpre_ref[...]) * (1.0 + m[4:5]) + m[3:4]).astype(BF16)
    hid = (_silu(jnp.dot(hb, wsg_ref[...], preferred_element_type=F32))
           * jnp.dot(hb, wsu_ref[...], preferred_element_type=F32))
    acc = jnp.dot(hid.astype(BF16), wsd_ref[...], preferred_element_type=F32)

    eye = eye_ref[...]
    nt = lambda a, b: lax.dot_general(a, b, (((1,), (1,)), ((), ())), preferred_element_type=F32)
    w1, w2, w3 = _split3(wn_ref[...])
    w_t = nt(eye, w1) + nt(eye, w2) + nt(eye, w3)

    acc_hi = acc[:, :PACKED]
    acc_lo = acc[:, PACKED:]
    for k in range(TOP_K):
        y_hi, y_lo = _unpack_rows(_load_token_tiles(y_ref, TD, k * TD))
        acc_hi = acc_hi + y_hi * w_t[:, k:k + 1]
        acc_lo = acc_lo + y_lo * w_t[:, k:k + 1]
    acc = jnp.concatenate([acc_hi, acc_lo], axis=1)
    o_ref[...] = x + m[5:6] * _rms(acc, gp_ref[...])


def _combine(wn, x, mod, l, g_pre, g_post, ws_gate, ws_up, ws_down, ybuf, first_tok, n_tok):
    off = first_tok // TD
    tiles_per_dec = DEC_SEQ // TD
    npd = TP // TD
    mod_row = lambda i: jnp.where(i + off < npd, 0, 1 + (i + off - npd) // tiles_per_dec)
    full = lambda *shape: pl.BlockSpec(shape, lambda i: (0,) * len(shape))
    y_spec = pl.BlockSpec((TOP_K * TD * ROWS_PER_TOKEN, LANES), lambda i: (i, 0))
    return pl.pallas_call(
        _combine_kernel,
        grid=(n_tok // TD,),
        in_specs=[
            pl.BlockSpec((TOP_K, TD), lambda i: (0, i + off)),
            pl.BlockSpec((TD, D), lambda i: (i + off, 0)),
            pl.BlockSpec((1, 1, 6, D), lambda i: (l, mod_row(i), 0, 0)),
            full(1, D), full(1, D), full(D, D_SHARED), full(D, D_SHARED), full(D_SHARED, D),
            y_spec,
        ],
        out_specs=pl.BlockSpec((TD, D), lambda i: (i, 0)),
        out_shape=jax.ShapeDtypeStruct((n_tok, D), F32),
        scratch_shapes=[pltpu.VMEM((TD, TD), BF16)],
        compiler_params=_params(("arbitrary",)),
        name="moe_combine",
    )(wn, x, mod, g_pre, g_post, ws_gate, ws_up, ws_down, ybuf)


def _moe_layer(x, h, mod, l, g_pre, g_post, w_router, e_bias, w_gate, w_up, w_down,
               ws_gate, ws_up, ws_down):
    top_e, wn, rk, cnt = _router(h, w_router.T, e_bias.reshape(N_EXPERTS, 1))
    cnt = cnt.reshape(N_EXPERTS).astype(I32)
    padded = (cnt + GM_SUB - 1) // GM_SUB * GM_SUB
    ends = jnp.cumsum(padded)
    offs = ends - padded
    eid = jnp.arange(N_EXPERTS, dtype=I32)[:, None, None]
    slot = rk + jnp.sum(jnp.where(top_e[None] == eid, offs[:, None, None], 0), axis=0)
    ntile = (cnt + GM - 1) // GM
    tile_ends = jnp.cumsum(ntile)
    tid = jnp.arange(NT_MAX, dtype=I32)
    last = jnp.maximum(tile_ends[-1] - 1, 0)
    tile_e = jnp.minimum(jnp.sum((tid[:, None] >= tile_ends[None, :]).astype(I32), axis=1), N_EXPERTS - 1)
    tile_e = jnp.where(tid <= last, tile_e, tile_e[last])
    own = tile_e[:, None] == jnp.arange(N_EXPERTS, dtype=I32)[None, :]
    pick = lambda per_expert: jnp.sum(jnp.where(own, per_expert[None, :], 0), axis=1)
    t_in = tid - pick(tile_ends - ntile)
    tile_blk = jnp.where(tid <= last, (pick(offs) + GM * t_in) // GM_SUB, 0)
    tile_rows = jnp.clip(pick(cnt) - GM * t_in, 0, GM)
    tile_nsub = jnp.where(tid <= last, (tile_rows + GM_SUB - 1) // GM_SUB, 0)
    starts = jnp.logical_and(jnp.arange(NT_MAX) <= last,
                             jnp.concatenate([jnp.ones((1,), bool), tile_e[1:] != tile_e[:-1]]))
    tile_run = jnp.where(starts, (jnp.cumsum(starts.astype(I32)) - 1) % 2, -1)
    later = jnp.logical_and(jnp.arange(N_EXPERTS, dtype=I32)[None, :] > tile_e[:, None], (cnt > 0)[None, :])
    tile_next = jnp.min(jnp.where(later, jnp.arange(N_EXPERTS, dtype=I32)[None, :], N_EXPERTS), axis=1)
    tile_next = jnp.where(tile_next < N_EXPERTS, tile_next, -1)
    slot3 = slot.reshape(TOP_K, T // SC_W, SC_W).transpose(1, 0, 2)
    xg = _sc_dispatch(h.reshape(T, ROWS_PER_TOKEN, LANES), slot3)
    xg = _pad_fill(offs + cnt, padded - cnt, xg).reshape(SP * ROWS_PER_TOKEN, LANES)
    yg = _gmm(tile_e, tile_blk, tile_nsub.astype(I32), tile_run.astype(I32), tile_next.astype(I32),
              xg, l, w_gate, w_up, w_down)
    yg3 = yg.reshape(SP, ROWS_PER_TOKEN, LANES)
    ws = (ws_gate.astype(BF16), ws_up.astype(BF16), ws_down.astype(BF16))
    outs = []
    for first_tok, n_tok in ((0, TP), (TP, TS)):
        ids = slot[:, first_tok:first_tok + n_tok].reshape(TOP_K, n_tok // TD, TD).transpose(1, 0, 2)
        ybuf = _sc_gather(yg3, ids.reshape(TOP_K * n_tok))
        outs.append(_combine(wn, x, mod, l, g_pre, g_post, *ws,
                             ybuf.reshape(TOP_K * n_tok * ROWS_PER_TOKEN, LANES), first_tok, n_tok))
    return outs


def _rope_tables():
    n = DEC_SEQ
    rows = n // GRID_W
    row = jnp.repeat(jnp.arange(rows), GRID_W).astype(F32)
    col = jnp.tile(jnp.arange(GRID_W), rows).astype(F32)
    half = DQK_B // 2
    inv = ROPE_BASE ** (-jnp.arange(0, half, 2, dtype=F32) / half)
    ang_r = row[:, None] * inv
    ang_c = col[:, None] * inv
    ang = jnp.concatenate([ang_r, ang_r, ang_c, ang_c], axis=-1)
    reps = QK_B // DQK_B
    return jnp.tile(jnp.cos(ang), (1, reps)), jnp.tile(jnp.sin(ang), (1, reps))


def _pad_in_proj(w):
    s = [0, Q_A, 2 * Q_A, 2 * Q_A + V_A, 2 * Q_A + 2 * V_A]
    s += [s[-1] + GATE_RANK, s[-1] + 2 * GATE_RANK]
    s += [s[-1] + QK_B, s[-1] + 2 * QK_B, s[-1] + 2 * QK_B + V_B]
    gates = jnp.pad(w[:, s[4]:s[6]], ((0, 0), (0, GL_PAD - 2 * GATE_RANK)))
    return jnp.concatenate([w[:, s[0]:s[4]], gates, w[:, s[6]:s[9]]], axis=1).astype(BF16)


def kernel(x_prompt, x_sample, c, c_ctx, state_gla, cache_k, cache_v, ada_w, ada_b, norm_pre_mix, norm_post_mix, norm_pre_ffn, norm_post_ffn, ab_w_in, gla_w_g2, gla_b_g2, gla_norm_g, diff_lambda, diff_norm_g, ab_w_out, sgu_w_in, sgu_b_in, sgu_norm_g, sgu_w_s, sgu_b_s, sgu_w_out, moe_w_router, moe_e_bias, moe_w_gate, moe_w_up, moe_w_down, moe_ws_gate, moe_ws_up, moe_ws_down):
    depth = ada_w.shape[0]
    assert x_prompt.shape == (BATCH, SEQ, D) and x_sample.shape == (DEC_BATCH, DEC_SEQ, D)
    assert state_gla.shape == (DEC_BATCH, (depth + 1) // 2, 2, H_A, DK_A, DV_A)
    assert cache_k.shape == (DEC_BATCH, 1, H_B, 2, PAST_LEN, DQK_B) and cache_v.shape == (DEC_BATCH, 1, H_B, PAST_LEN, DV_B)
    assert depth == 2 and moe_w_gate.shape == (depth, N_EXPERTS, D, D_EXPERT)
    xp, xs = x_prompt.reshape(TP, D), x_sample.reshape(TS, D)
    cond = jnp.concatenate([c_ctx[None, :], c, jnp.zeros((8 - 1 - DEC_BATCH, D), F32)], axis=0)
    mod = _modulation(cond, ada_w, ada_b)
    cos, sin = _rope_tables()
    vec = lambda a: a.reshape(1, -1)
    new_s = new_k = new_v = None
    for l in range(depth):
        if l % 2 == 0:
            e = l // 2
            lam_init = 0.8 - 0.6 * math.exp(-0.3 * l)
            a, r_a, gl, q_b, k_b, v_b, new_k, new_v = _in_proj(xp, xs, mod, l, vec(norm_pre_mix[l]),
                                                               _pad_in_proj(ab_w_in[e]), cos, sin)
            s0_t = jnp.swapaxes(state_gla[:, e], -1, -2)
            same_head = jnp.eye(H_A, dtype=bool)[None, None, :, None, :, None]
            s0_t = jnp.where(same_head, s0_t[:, :, :, :, None, :], 0.0).reshape(DEC_BATCH, 2, V_A, Q_A)
            o_f, o_bw, s_fin_t = _gla(a, gl, gla_w_g2[e], gla_b_g2[e].reshape(2, 1, Q_A), s0_t)
            o_att = _diff_attention(q_b, k_b, v_b, cache_k, cache_v, diff_lambda[e], lam_init)
            x, h = _mix_out(lam_init, o_f, o_bw, r_a, o_att, xp, xs, mod, l, vec(gla_norm_g[e]), vec(diff_norm_g[e]),
                            ab_w_out[e].astype(BF16), vec(norm_post_mix[l]), vec(norm_pre_ffn[l]))
            new_s = jnp.swapaxes(s_fin_t, -1, -2)[:, None]
        else:
            o = l // 2
            x, h = _sgu(xp, xs, mod, l, vec(norm_pre_mix[l]), sgu_w_in[o].astype(BF16), vec(sgu_b_in[o]),
                        vec(sgu_norm_g[o]), sgu_w_s[o], sgu_b_s[o].T, sgu_w_out[o].astype(BF16),
                        vec(norm_post_mix[l]), vec(norm_pre_ffn[l]))
        xp, xs = _moe_layer(x, h, mod, l, vec(norm_pre_ffn[l]), vec(norm_post_ffn[l]), moe_w_router[l], moe_e_bias[l],
                            moe_w_gate, moe_w_up, moe_w_down, moe_ws_gate[l], moe_ws_up[l], moe_ws_down[l])
    y_prompt = xp.reshape(BATCH, SEQ, D)
    y_sample = xs.reshape(DEC_BATCH, DEC_SEQ, D)
    return (y_prompt, y_sample, new_s, new_k, new_v)
```

```python
import functools
import math

import jax
import jax.numpy as jnp
from jax import lax
from jax.experimental import pallas as pl
from jax.experimental.pallas import tpu as pltpu
from jax.experimental.pallas import tpu_sc as plsc

F32 = jnp.float32
BF16 = jnp.bfloat16
I32 = jnp.int32

D = 1024
BATCH, SEQ = 32, 256
DEC_BATCH, DEC_SEQ = 4, 2048
PAST_LEN = 256
GRID_W = 64
EPS = 1e-6
TP = BATCH * SEQ
TS = DEC_BATCH * DEC_SEQ
T = TP + TS
H_A, DK_A, DV_A = 4, 64, 128
Q_A, V_A = H_A * DK_A, H_A * DV_A
GATE_RANK, GATE_TAU, GLA_CHUNK = 16, 16.0, 64
H_B, DQK_B, DV_B = 4, 64, 128
QK_B, V_B = H_B * 2 * DQK_B, H_B * DV_B
ROPE_BASE = 10000.0
SGU_DIM, SGU_GROUPS, SGU_CHUNK = 1024, 4, 128
N_EXPERTS, TOP_K, N_GROUPS, TOPK_GROUPS = 64, 8, 8, 4
GROUP_SIZE = N_EXPERTS // N_GROUPS
D_EXPERT, D_SHARED = 256, 256
ROUTED_SCALE = 2.5

TM = 512
TM_SUB = 256
MIX_DTYPE = jnp.bfloat16
NPT = TP // TM
TILES_PER_DEC = DEC_SEQ // TM
SEG = 256
NSEG = T // SEG
NSEG_P = TP // SEG
SEG_PER_DEC = DEC_SEQ // SEG
TR = 512
TD = 512
GM = 2048
GM_SUB = 256
NT_MAX = T * TOP_K // GM + N_EXPERTS
SP = T * TOP_K + N_EXPERTS * GM_SUB
GL_PAD = 128
VMEM_LIMIT = 56 * 1024 * 1024
NEG_INF = float("-inf")


def _bdot(a, b):
    return jnp.dot(a.astype(BF16), b.astype(BF16), preferred_element_type=F32)


def _bdot_nt(a, b):
    return lax.dot_general(a.astype(BF16), b.astype(BF16), (((1,), (1,)), ((), ())),
                           preferred_element_type=F32)


def _bdot_tn(a, b):
    return lax.dot_general(a.astype(BF16), b.astype(BF16), (((0,), (0,)), ((), ())),
                           preferred_element_type=F32)


def _split3(x):
    x1 = x.astype(BF16)
    r1 = x - x1.astype(F32)
    x2 = r1.astype(BF16)
    x3 = (r1 - x2.astype(F32)).astype(BF16)
    return x1, x2, x3


def _rms(x, g):
    return x * lax.rsqrt(jnp.mean(x * x, axis=-1, keepdims=True) + EPS) * g


def _silu(x):
    return x * jax.nn.sigmoid(x)


def _mod_row(i):
    return jnp.where(i < NPT, 0, 1 + (i - NPT) // TILES_PER_DEC)


def _params(sem, limit=VMEM_LIMIT):
    return pltpu.CompilerParams(dimension_semantics=sem, vmem_limit_bytes=limit)


def _mod_kernel(c_ref, w_ref, b_ref, o_ref):
    o_ref[0] = _bdot(_silu(c_ref[...]), w_ref[0]) + b_ref[0]


def _modulation(cond, ada_w, ada_b):
    depth = ada_w.shape[0]
    nj = 6
    out = pl.pallas_call(
        _mod_kernel,
        grid=(depth, nj),
        in_specs=[
            pl.BlockSpec((8, D), lambda l, j: (0, 0)),
            pl.BlockSpec((1, D, D), lambda l, j: (l, 0, j)),
            pl.BlockSpec((1, 1, D), lambda l, j: (l, 0, j)),
        ],
        out_specs=pl.BlockSpec((1, 8, D), lambda l, j: (l, 0, j)),
        out_shape=jax.ShapeDtypeStruct((depth, 8, 6 * D), F32),
        compiler_params=_params(("arbitrary", "arbitrary")),
        name="adaln_modulation",
    )(cond, ada_w, ada_b.reshape(depth, 1, 6 * D))
    return out.reshape(depth, 8, 6, D)


_C_A = 0
_C_R = _C_A + 2 * Q_A + V_A
_C_GL = _C_R + V_A
_C_Q = _C_GL + GL_PAD
_C_K = _C_Q + QK_B
_C_V = _C_K + QK_B
_C_END = _C_V + V_B
ROT_PAIR = DQK_B // 4


def _rope(x, cos, sin):
    lane = lax.broadcasted_iota(I32, x.shape, 1)
    first = (lane % (2 * ROT_PAIR)) < ROT_PAIR
    n = x.shape[1]
    xr = jnp.where(first, -pltpu.roll(x, n - ROT_PAIR, 1), pltpu.roll(x, ROT_PAIR, 1))
    return x * cos + xr * sin


def _stream_specs():
    return [pl.BlockSpec((TM, D), lambda i: (jnp.minimum(i, NPT - 1), 0)),
            pl.BlockSpec((TM, D), lambda i: (jnp.maximum(i - NPT, 0), 0))]


def _stream_tile(xp_ref, xs_ref):
    return jnp.where(pl.program_id(0) < NPT, xp_ref[...], xs_ref[...])


def _in_kernel(xp_ref, xs_ref, mod_ref, g_ref, w_ref, cos_ref, sin_ref,
               a_ref, r_ref, gl_ref, q_ref, k_ref, v_ref, ck_ref, cv_ref, k32_ref, v32_ref):
    latent = pl.program_id(0) >= NPT
    m = mod_ref[0, 0]
    x = _stream_tile(xp_ref, xs_ref)
    for s in range(TM // TM_SUB):
        rows = slice(s * TM_SUB, (s + 1) * TM_SUB)
        h = _rms(x[rows], g_ref[...]) * (1.0 + m[1:2]) + m[0:1]
        hb = h.astype(BF16)

        def proj(c0, c1):
            return jnp.dot(hb, w_ref[:, c0:c1], preferred_element_type=F32)

        a_ref[rows, :] = proj(_C_A, _C_R)
        r_ref[rows, :] = proj(_C_R, _C_GL).astype(MIX_DTYPE)
        gl_ref[rows, :] = proj(_C_GL, _C_Q)
        v = proj(_C_V, _C_END)
        q = proj(_C_Q, _C_K)
        k = proj(_C_K, _C_V)
        cos = cos_ref[rows, :]
        sin = sin_ref[rows, :]
        q_ref[rows, :] = jnp.where(latent, _rope(q, cos, sin), q).astype(BF16)
        k_ref[rows, :] = jnp.where(latent, _rope(k, cos, sin), k).astype(BF16)
        v_ref[rows, :] = v.astype(BF16)
        k32_ref[rows, :] = k
        v32_ref[rows, :] = v

    @pl.when(jnp.logical_not(latent))
    def _():
        for s in range(TM // SEQ):
            rows = slice(s * SEQ, (s + 1) * SEQ)
            for h in range(H_B):
                cv_ref[s, 0, h] = v32_ref[rows, h * DV_B:(h + 1) * DV_B]
                for mp in range(2):
                    ck_ref[s, 0, h, mp] = k32_ref[rows, (2 * h + mp) * DQK_B:(2 * h + mp + 1) * DQK_B]


def _in_proj(xp, xs, mod, l, g, w_pad, cos, sin):
    tok = lambda width: pl.BlockSpec((TM, width), lambda i: (i, 0))
    rope_spec = pl.BlockSpec((TM, QK_B), lambda i: (jnp.maximum(i - NPT, 0) % TILES_PER_DEC, 0))
    widths = (2 * Q_A + V_A, V_A, GL_PAD, QK_B, QK_B, V_B)
    dtypes = (F32, MIX_DTYPE, F32, BF16, BF16, BF16)
    seqs = TM // SEQ
    prompt_tile = lambda i: jnp.minimum(i, NPT - 1)
    cache_specs = [pl.BlockSpec((seqs, 1, H_B, 2, SEQ, DQK_B), lambda i: (prompt_tile(i), 0, 0, 0, 0, 0)),
                   pl.BlockSpec((seqs, 1, H_B, SEQ, DV_B), lambda i: (prompt_tile(i), 0, 0, 0, 0))]
    cache_shapes = [jax.ShapeDtypeStruct((BATCH, 1, H_B, 2, SEQ, DQK_B), F32),
                    jax.ShapeDtypeStruct((BATCH, 1, H_B, SEQ, DV_B), F32)]
    return pl.pallas_call(
        _in_kernel,
        grid=(T // TM,),
        in_specs=_stream_specs() + [
            pl.BlockSpec((1, 1, 6, D), lambda i: (l, _mod_row(i), 0, 0)),
            pl.BlockSpec((1, D), lambda i: (0, 0)),
            pl.BlockSpec((D, _C_END), lambda i: (0, 0)),
            rope_spec, rope_spec,
        ],
        out_specs=[tok(w) for w in widths] + cache_specs,
        out_shape=[jax.ShapeDtypeStruct((T, w), dt) for w, dt in zip(widths, dtypes)] + cache_shapes,
        scratch_shapes=[pltpu.VMEM((TM, QK_B), F32), pltpu.VMEM((TM, V_B), F32)],
        compiler_params=_params(("arbitrary",)),
        name="mixer_ab_in_proj",
    )(xp, xs, mod, g, w_pad, cos, sin)


def _log_sigmoid(x):
    return jnp.minimum(x, 0.0) - jnp.log(1.0 + jnp.exp(-jnp.abs(x)))


def _gla_kernel(af_ref, ab_ref, glf_ref, glb_ref, wg_ref, bg_ref, s0_ref,
                of_ref, ob_ref, sfin_ref, st_ref):
    i = pl.program_id(0)

    @pl.when(i < NSEG_P)
    def _():
        st_ref[...] = jnp.zeros_like(st_ref)

    @pl.when(jnp.logical_and(i >= NSEG_P, (i - NSEG_P) % SEG_PER_DEC == 0))
    def _():
        st_ref[...] = s0_ref[0]

    r = lax.broadcasted_iota(I32, (SEG, SEG), 0)
    c = lax.broadcasted_iota(I32, (SEG, SEG), 1)
    same = (r // GLA_CHUNK) == (c // GLA_CHUNK)
    nchunk = SEG // GLA_CHUNK
    own_head = (lax.broadcasted_iota(I32, (V_A, Q_A), 0) // DV_A) == (lax.broadcasted_iota(I32, (V_A, Q_A), 1) // DK_A)

    dirs = []
    for d, (a_ref, gl_ref, o_ref) in enumerate(((af_ref, glf_ref, of_ref), (ab_ref, glb_ref, ob_ref))):
        fwd = d == 0
        gcol = gl_ref[:, d * GATE_RANK:(d + 1) * GATE_RANK]
        la = _log_sigmoid(_bdot(gcol, wg_ref[d]) + bg_ref[d]) / GATE_TAU
        causal = jnp.logical_and(same, (c <= r) if fwd else (c >= r))
        tri = jnp.where(causal, 1.0, 0.0).astype(BF16)
        l1, l2, l3 = _split3(la)
        b_all = (jnp.dot(tri, l1, preferred_element_type=F32)
                 + jnp.dot(tri, l2, preferred_element_type=F32)
                 + jnp.dot(tri, l3, preferred_element_type=F32))
        q_in_all = a_ref[:, 0:Q_A] * (DK_A ** -0.5) * jnp.exp(b_all)
        kd_all = a_ref[:, Q_A:2 * Q_A] * jnp.exp(-b_all)
        intra = []
        for h in range(H_A):
            kc = slice(h * DK_A, (h + 1) * DK_A)
            attn = jnp.where(causal, _bdot_nt(q_in_all[:, kc], kd_all[:, kc]), 0.0)
            intra.append(_bdot(attn, a_ref[:, 2 * Q_A + h * DV_A:2 * Q_A + (h + 1) * DV_A]))
        intra = jnp.concatenate(intra, axis=1)
        dirs.append(dict(fwd=fwd, a=a_ref, o=o_ref, b=b_all, q=q_in_all, intra=intra, state=st_ref[d]))

    for step in range(nchunk):
        for dd in dirs:
            ch = step if dd["fwd"] else nchunk - 1 - step
            r0 = ch * GLA_CHUNK
            rows = slice(r0, r0 + GLA_CHUNK)
            end = r0 + GLA_CHUNK - 1 if dd["fwd"] else r0
            b_end = dd["b"][end:end + 1, :]
            kw = dd["a"][rows, Q_A:2 * Q_A] * jnp.exp(b_end - dd["b"][rows, :])
            dd["o"][rows, :] = (dd["intra"][rows, :] + _bdot_nt(dd["q"][rows, :], dd["state"])).astype(MIX_DTYPE)
            kv_t = _bdot_tn(dd["a"][rows, 2 * Q_A:2 * Q_A + V_A], kw)
            dd["state"] = dd["state"] * jnp.exp(b_end) + jnp.where(own_head, kv_t, 0.0)
    for d, dd in enumerate(dirs):
        st_ref[d] = dd["state"]

    @pl.when(i < NSEG_P)
    def _():
        for d in range(2):
            for h in range(H_A):
                sfin_ref[0, d, h] = st_ref[d, h * DV_A:(h + 1) * DV_A, h * DK_A:(h + 1) * DK_A]


def _seg_bwd(i):
    j = i - NSEG_P
    return jnp.where(i < NSEG_P, i, NSEG_P + (j // SEG_PER_DEC) * SEG_PER_DEC + (SEG_PER_DEC - 1 - j % SEG_PER_DEC))


def _gla(a, gl, wg, bg, s0_t):
    seg = lambda width, f: pl.BlockSpec((SEG, width), lambda i: (f(i), 0))
    ident = lambda i: i
    st_block = (1, 2, H_A, DV_A, DK_A)
    return pl.pallas_call(
        _gla_kernel,
        grid=(NSEG,),
        in_specs=[
            seg(D, ident), seg(D, _seg_bwd), seg(GL_PAD, ident), seg(GL_PAD, _seg_bwd),
            pl.BlockSpec((2, GATE_RANK, Q_A), lambda i: (0, 0, 0)),
            pl.BlockSpec((2, 1, Q_A), lambda i: (0, 0, 0)),
            pl.BlockSpec((1, 2, V_A, Q_A), lambda i: (jnp.maximum(i - NSEG_P, 0) // SEG_PER_DEC, 0, 0, 0)),
        ],
        out_specs=[
            seg(V_A, ident), seg(V_A, _seg_bwd),
            pl.BlockSpec(st_block, lambda i: (jnp.minimum(i, NSEG_P - 1), 0, 0, 0, 0)),
        ],
        out_shape=[
            jax.ShapeDtypeStruct((T, V_A), MIX_DTYPE),
            jax.ShapeDtypeStruct((T, V_A), MIX_DTYPE),
            jax.ShapeDtypeStruct((BATCH, 2, H_A, DV_A, DK_A), F32),
        ],
        scratch_shapes=[pltpu.VMEM((2, V_A, Q_A), F32)],
        compiler_params=_params(("arbitrary",)),
        name="gla_bidir",
    )(a, a, gl, gl, wg, bg, s0_t)


def _diff_lambda(lam_ref, lam_init):
    lp = lam_ref[...]
    s01 = jnp.sum(lp[0:1] * lp[1:2], axis=1, keepdims=True)
    s23 = jnp.sum(lp[2:3] * lp[3:4], axis=1, keepdims=True)
    return jnp.exp(s01) - jnp.exp(s23) + lam_init


def _attn_prompt_kernel(lam_init, q_ref, k_ref, v_ref, lam_ref, o_ref):
    lam = _diff_lambda(lam_ref, lam_init)
    for h in range(H_B):
        ps = []
        for m in range(2):
            cols = slice((2 * h + m) * DQK_B, (2 * h + m + 1) * DQK_B)
            s = _bdot_nt(q_ref[:, cols] * (DQK_B ** -0.5), k_ref[:, cols])
            e = jnp.exp(s - jnp.max(s, axis=1, keepdims=True))
            ps.append(e * (1.0 / jnp.sum(e, axis=1, keepdims=True)))
        w = ps[0] - lam * ps[1]
        o_ref[:, h * DV_B:(h + 1) * DV_B] = _bdot(w, v_ref[:, h * DV_B:(h + 1) * DV_B]).astype(MIX_DTYPE)


def _attn_sample_kernel(lam_init, q_ref, k_ref, v_ref, ck_ref, cv_ref, lam_ref, o_ref):
    lam = _diff_lambda(lam_ref, lam_init)
    for h in range(H_B):
        parts = []
        for m in range(2):
            cols = slice((2 * h + m) * DQK_B, (2 * h + m + 1) * DQK_B)
            q = q_ref[:, cols] * (DQK_B ** -0.5)
            sc = _bdot_nt(q, ck_ref[0, 0, h, m])
            sn = _bdot_nt(q, k_ref[:, cols])
            mx = jnp.maximum(jnp.max(sc, axis=1, keepdims=True), jnp.max(sn, axis=1, keepdims=True))
            ec = jnp.exp(sc - mx)
            en = jnp.exp(sn - mx)
            inv = (1.0 if m == 0 else -lam) / (jnp.sum(ec, axis=1, keepdims=True) + jnp.sum(en, axis=1, keepdims=True))
            parts.append((ec * inv, en * inv))
        wc = parts[0][0] + parts[1][0]
        wn = parts[0][1] + parts[1][1]
        o_ref[:, h * DV_B:(h + 1) * DV_B] = (_bdot(wc, cv_ref[0, 0, h])
                                             + _bdot(wn, v_ref[:, h * DV_B:(h + 1) * DV_B])).astype(MIX_DTYPE)


QB = SEQ
NQB_DEC = DEC_SEQ // QB


def _attn_kernel(lam_init, q_ref, kp_ref, vp_ref, ks_ref, vs_ref, ck_ref, cv_ref, lam_ref, o_ref):
    i = pl.program_id(0)

    @pl.when(i < BATCH)
    def _():
        _attn_prompt_kernel(lam_init, q_ref, kp_ref, vp_ref, lam_ref, o_ref)

    @pl.when(i >= BATCH)
    def _():
        _attn_sample_kernel(lam_init, q_ref, ks_ref, vs_ref, ck_ref, cv_ref, lam_ref, o_ref)


def _diff_attention(q, k, v, cache_k, cache_v, lam_p, lam_init):
    blk = lambda rows, f: pl.BlockSpec((rows, QK_B), f)
    dec_b = lambda i: jnp.maximum(i - BATCH, 0) // NQB_DEC
    own = lambda i: (i, 0)
    prompt_kv = lambda i: (jnp.minimum(i, BATCH - 1), 0)
    dec_kv = lambda i: (TP // DEC_SEQ + dec_b(i), 0)
    return pl.pallas_call(
        functools.partial(_attn_kernel, lam_init),
        grid=(BATCH + DEC_BATCH * NQB_DEC,),
        in_specs=[
            blk(QB, own), blk(SEQ, prompt_kv), blk(SEQ, prompt_kv), blk(DEC_SEQ, dec_kv), blk(DEC_SEQ, dec_kv),
            pl.BlockSpec((1, 1, H_B, 2, PAST_LEN, DQK_B), lambda i: (dec_b(i), 0, 0, 0, 0, 0)),
            pl.BlockSpec((1, 1, H_B, PAST_LEN, DV_B), lambda i: (dec_b(i), 0, 0, 0, 0)),
            pl.BlockSpec((4, DQK_B), lambda i: (0, 0)),
        ],
        out_specs=blk(QB, own),
        out_shape=jax.ShapeDtypeStruct((T, V_B), MIX_DTYPE),
        compiler_params=_params(("arbitrary",)),
        name="diff_attention",
    )(q, k, v, k, v, cache_k, cache_v, lam_p)


def _head_rms(x, g, nheads, width):
    return jnp.concatenate([_rms(x[:, h * width:(h + 1) * width], g) for h in range(nheads)], axis=1)


def _mix_out_kernel(lam_init, of_ref, ob_ref, r_ref, oatt_ref, xp_ref, xs_ref, mod_ref,
                    gg_ref, dg_ref, wo_ref, gp_ref, gffn_ref, o_ref, hp_ref):
    m = mod_ref[0, 0]
    load = lambda ref: ref[...].astype(F32)
    o_a = _head_rms(load(of_ref) + load(ob_ref), gg_ref[...], H_A, DV_A) * _silu(load(r_ref))
    o_b = _head_rms(load(oatt_ref), dg_ref[...], H_B, DV_B) * (1.0 - lam_init)
    out = _bdot(o_a, wo_ref[0:V_A, :]) + _bdot(o_b, wo_ref[V_A:V_A + V_B, :])
    x1 = _stream_tile(xp_ref, xs_ref) + m[2:3] * _rms(out, gp_ref[...])
    o_ref[...] = x1
    _store_token_tiles(hp_ref, _ffn_input_rows(x1, m, gffn_ref[...]))


def _mix_out(lam_init, o_f, o_b, r_a, o_att, xp, xs, mod, l, gla_g, diff_g, w_o, g_post, g_ffn):
    tok = lambda width: pl.BlockSpec((TM, width), lambda i: (i, 0))
    vec = lambda width: pl.BlockSpec((1, width), lambda i: (0, 0))
    return pl.pallas_call(
        functools.partial(_mix_out_kernel, lam_init),
        grid=(T // TM,),
        in_specs=[
            tok(V_A), tok(V_A), tok(V_A), tok(V_B), *_stream_specs(),
            pl.BlockSpec((1, 1, 6, D), lambda i: (l, _mod_row(i), 0, 0)),
            vec(DV_A), vec(DV_B),
            pl.BlockSpec((V_A + V_B, D), lambda i: (0, 0)),
            vec(D), vec(D),
        ],
        out_specs=[tok(D), pl.BlockSpec((TM * ROWS_PER_TOKEN, LANES), lambda i: (i, 0))],
        out_shape=[jax.ShapeDtypeStruct((T, D), F32), jax.ShapeDtypeStruct((T * ROWS_PER_TOKEN, LANES), U32)],
        compiler_params=_params(("arbitrary",)),
        name="mixer_ab_out",
    )(o_f, o_b, r_a, o_att, xp, xs, mod, gla_g, diff_g, w_o, g_post, g_ffn)


def _gelu_tanh(x):
    return 0.5 * x * (1.0 + jnp.tanh(math.sqrt(2.0 / math.pi) * (x + 0.044715 * (x * x * x))))


def _sgu_kernel(xp_ref, xs_ref, mod_ref, gpre_ref, win_ref, bin_ref, vg_ref, ws_ref, bs_ref,
                wout_ref, gpost_ref, gffn_ref, o_ref, hp_ref, t_ref):
    m = mod_ref[0, 0]
    x = _stream_tile(xp_ref, xs_ref)
    h = _rms(x, gpre_ref[...]) * (1.0 + m[1:2]) + m[0:1]
    z = _gelu_tanh(_bdot(h, win_ref[...]) + bin_ref[...])
    v = _rms(z[:, SGU_DIM:], vg_ref[...])
    gw = SGU_DIM // SGU_GROUPS
    for ch in range(TM // SGU_CHUNK):
        rows = slice(ch * SGU_CHUNK, (ch + 1) * SGU_CHUNK)
        for g in range(SGU_GROUPS):
            cols = slice(g * gw, (g + 1) * gw)
            vs = _bdot(ws_ref[g], v[rows, cols]) + bs_ref[:, g:g + 1]
            t_ref[rows, cols] = (z[rows, cols] * vs).astype(BF16)
    out = jnp.dot(t_ref[...], wout_ref[...], preferred_element_type=F32)
    x1 = x + m[2:3] * _rms(out, gpost_ref[...])
    o_ref[...] = x1
    _store_token_tiles(hp_ref, _ffn_input_rows(x1, m, gffn_ref[...]))


def _sgu(xp, xs, mod, l, g_pre, w_in, b_in, v_g, w_s, b_s_t, w_out, g_post, g_ffn):
    tok = pl.BlockSpec((TM, D), lambda i: (i, 0))
    full = lambda *shape: pl.BlockSpec(shape, lambda i: (0,) * len(shape))
    return pl.pallas_call(
        _sgu_kernel,
        grid=(T // TM,),
        in_specs=_stream_specs() + [
            pl.BlockSpec((1, 1, 6, D), lambda i: (l, _mod_row(i), 0, 0)),
            full(1, D), full(D, 2 * SGU_DIM), full(1, 2 * SGU_DIM), full(1, SGU_DIM),
            full(SGU_GROUPS, SGU_CHUNK, SGU_CHUNK), full(SGU_CHUNK, SGU_GROUPS),
            full(SGU_DIM, D), full(1, D), full(1, D),
        ],
        out_specs=[tok, pl.BlockSpec((TM * ROWS_PER_TOKEN, LANES), lambda i: (i, 0))],
        out_shape=[jax.ShapeDtypeStruct((T, D), F32), jax.ShapeDtypeStruct((T * ROWS_PER_TOKEN, LANES), U32)],
        scratch_shapes=[pltpu.VMEM((TM, SGU_DIM), BF16)],
        compiler_params=_params(("arbitrary",)),
        name="sgu_mixer",
    )(xp, xs, mod, g_pre, w_in, b_in, v_g, w_s, b_s_t, w_out, g_post, g_ffn)


LANES = 128
U32 = jnp.uint32
PACKED = D // 2
ROWS_PER_TOKEN = PACKED // LANES
BF16_BITS = 16
HIGH_HALF = 0xFFFF0000


def _pack_rows(x):
    bits = lax.bitcast_convert_type(x.astype(BF16).astype(F32), U32)
    return bits[:, :PACKED] | (bits[:, PACKED:] >> BF16_BITS)


def _unpack_rows(u):
    return (lax.bitcast_convert_type(u & U32(HIGH_HALF), F32), lax.bitcast_convert_type(u << BF16_BITS, F32))


def _store_token_tiles(ref, u, first=0):
    n = u.shape[0]
    for c in range(ROWS_PER_TOKEN):
        ref[pl.ds(first * ROWS_PER_TOKEN + c, n, stride=ROWS_PER_TOKEN), :] = u[:, c * LANES:(c + 1) * LANES]


def _load_token_tiles(ref, n, first=0):
    return jnp.concatenate([ref[pl.ds(first * ROWS_PER_TOKEN + c, n, stride=ROWS_PER_TOKEN), :]
                            for c in range(ROWS_PER_TOKEN)], axis=1)


def _ffn_input_rows(x, m, g):
    return _pack_rows(_rms(x, g) * (1.0 + m[4:5]) + m[3:4])


def _router_kernel(hp_ref, wr_ref, eb_ref, te_ref, wn_ref, rk_ref, cnt_ref, carry_ref, upper_ref):
    i = pl.program_id(0)

    @pl.when(i == 0)
    def _():
        carry_ref[...] = jnp.zeros_like(carry_ref)
        tj = lax.broadcasted_iota(I32, (TR, TR), 0)
        ti = lax.broadcasted_iota(I32, (TR, TR), 1)
        upper_ref[...] = jnp.where(tj < ti, 1.0, 0.0).astype(BF16)

    h_hi, h_lo = (t.astype(BF16) for t in _unpack_rows(_load_token_tiles(hp_ref, TR)))
    w1, w2, _ = _split3(wr_ref[...])
    nt = lambda a, b: lax.dot_general(a, b, (((1,), (1,)), ((), ())), preferred_element_type=F32)
    logits = (nt(w1[:, :PACKED], h_hi) + nt(w1[:, PACKED:], h_lo)
              + nt(w2[:, :PACKED], h_hi) + nt(w2[:, PACKED:], h_lo))
    scores = jax.nn.sigmoid(logits)
    sel = scores + eb_ref[...]

    row8 = lax.broadcasted_iota(I32, (GROUP_SIZE, TR), 0)
    gscore = []
    for g in range(N_GROUPS):
        xg = sel[g * GROUP_SIZE:(g + 1) * GROUP_SIZE]
        m1 = jnp.max(xg, axis=0, keepdims=True)
        i1 = jnp.min(jnp.where(xg == m1, row8, GROUP_SIZE), axis=0, keepdims=True)
        m2 = jnp.max(jnp.where(row8 == i1, NEG_INF, xg), axis=0, keepdims=True)
        gscore.append(m1 + m2)
    pieces = []
    for g in range(N_GROUPS):
        rank = jnp.zeros((1, TR), I32)
        for g2 in range(N_GROUPS):
            if g2 == g:
                continue
            beats = (gscore[g2] >= gscore[g]) if g2 < g else (gscore[g2] > gscore[g])
            rank = rank + beats.astype(I32)
        pieces.append(jnp.where(rank < TOPK_GROUPS, sel[g * GROUP_SIZE:(g + 1) * GROUP_SIZE], NEG_INF))
    cur = jnp.concatenate(pieces, axis=0)

    row = lax.broadcasted_iota(I32, (N_EXPERTS, TR), 0)
    idxs, ws = [], []
    for _ in range(TOP_K):
        mx = jnp.max(cur, axis=0, keepdims=True)
        idx = jnp.min(jnp.where(cur == mx, row, N_EXPERTS), axis=0, keepdims=True)
        hit = row == idx
        ws.append(jnp.sum(jnp.where(hit, scores, 0.0), axis=0, keepdims=True))
        cur = jnp.where(hit, NEG_INF, cur)
        idxs.append(idx)
    mask = jnp.zeros((N_EXPERTS, TR), F32)
    for idx in idxs:
        mask = mask + (row == idx).astype(F32)
    wsum = ws[0]
    for wk in ws[1:]:
        wsum = wsum + wk

    pos = carry_ref[...] + jnp.dot(mask.astype(BF16), upper_ref[...], preferred_element_type=F32)
    for k in range(TOP_K):
        hit = row == idxs[k]
        te_ref[k:k + 1, :] = idxs[k]
        wn_ref[k:k + 1, :] = ws[k] / wsum * ROUTED_SCALE
        rk_ref[k:k + 1, :] = jnp.sum(jnp.where(hit, pos, 0.0), axis=0, keepdims=True).astype(I32)
    carry_ref[...] = carry_ref[...] + jnp.sum(mask, axis=1, keepdims=True)
    cnt_ref[...] = carry_ref[...]


def _router(hp, wr_t, e_bias):
    kt = lambda dtype: jax.ShapeDtypeStruct((TOP_K, T), dtype)
    kt_spec = pl.BlockSpec((TOP_K, TR), lambda i: (0, i))
    return pl.pallas_call(
        _router_kernel,
        grid=(T // TR,),
        in_specs=[
            pl.BlockSpec((TR * ROWS_PER_TOKEN, LANES), lambda i: (i, 0)),
            pl.BlockSpec((N_EXPERTS, D), lambda i: (0, 0)),
            pl.BlockSpec((N_EXPERTS, 1), lambda i: (0, 0)),
        ],
        out_specs=[
            kt_spec, kt_spec, kt_spec,
            pl.BlockSpec((N_EXPERTS, 1), lambda i: (0, 0)),
        ],
        out_shape=[
            kt(I32), kt(F32), kt(I32),
            jax.ShapeDtypeStruct((N_EXPERTS, 1), F32),
        ],
        scratch_shapes=[pltpu.VMEM((N_EXPERTS, 1), F32), pltpu.VMEM((TR, TR), BF16)],
        compiler_params=_params(("arbitrary",)),
        name="moe_router",
    )(hp, wr_t, e_bias)


_PAD_BITS = tuple(1 << b for b in range(GM_SUB.bit_length() - 1))


def _pad_fill_kernel(pad_start_ref, pad_len_ref, xg_in_ref, xg_ref, zero_ref, sem):
    del xg_in_ref
    zero_ref[...] = jnp.zeros_like(zero_ref)

    def pad_copies(e):
        start = pad_start_ref[e]
        n = pad_len_ref[e]
        copies = []
        for bit in _PAD_BITS:
            first = start + (n & ~(2 * bit - 1))
            copies.append(((n & bit) != 0, pltpu.make_async_copy(
                zero_ref.at[pl.ds(0, bit)], xg_ref.at[pl.ds(first, bit)], sem)))
        return copies

    def start_e(e, carry):
        for on, cp in pad_copies(e):
            @pl.when(on)
            def _():
                cp.start()
        return carry

    def wait_e(e, carry):
        for on, cp in pad_copies(e):
            @pl.when(on)
            def _():
                cp.wait()
        return carry

    lax.fori_loop(0, N_EXPERTS, start_e, 0)
    lax.fori_loop(0, N_EXPERTS, wait_e, 0)


def _pad_fill(pad_start, pad_len, xg):
    grid_spec = pltpu.PrefetchScalarGridSpec(
        num_scalar_prefetch=2,
        grid=(1,),
        in_specs=[pl.BlockSpec(memory_space=pl.ANY)],
        out_specs=pl.BlockSpec(memory_space=pl.ANY),
        scratch_shapes=[pltpu.VMEM((GM_SUB // 2, ROWS_PER_TOKEN, LANES), xg.dtype), pltpu.SemaphoreType.DMA],
    )
    return pl.pallas_call(
        _pad_fill_kernel,
        grid_spec=grid_spec,
        out_shape=jax.ShapeDtypeStruct(xg.shape, xg.dtype),
        input_output_aliases={2: 0},
        compiler_params=_params(("arbitrary",)),
        name="moe_pad_fill",
    )(pad_start, pad_len, xg)


SC_CORES, SC_SUBCORES = 2, 16
SC_WORKERS = SC_CORES * SC_SUBCORES
SC_W = 64


def _sc_worker_id():
    return lax.axis_index("s") * SC_CORES + lax.axis_index("c")


def _sc_dispatch(h3, slot3):
    nchunk = T // SC_WORKERS // SC_W
    mesh = plsc.VectorSubcoreMesh(core_axis_name="c", subcore_axis_name="s")
    tile = (SC_W, ROWS_PER_TOKEN, LANES)

    @functools.partial(
        pl.kernel, mesh=mesh,
        out_type=jax.ShapeDtypeStruct((SP, ROWS_PER_TOKEN, LANES), h3.dtype),
        scratch_types=[pltpu.VMEM((TOP_K, SC_W), I32), pltpu.VMEM((TOP_K, SC_W), I32),
                       pltpu.VMEM(tile, h3.dtype), pltpu.VMEM(tile, h3.dtype),
                       pltpu.SemaphoreType.DMA((2,)), pltpu.SemaphoreType.DMA((2,))],
    )
    def k(h_hbm, slot_hbm, xg_hbm, idx0, idx1, rows0, rows1, lsem, ssem):
        first = _sc_worker_id() * nchunk
        idx = (idx0, idx1)
        rows = (rows0, rows1)

        def loads(j, b):
            blk = first + j
            tok = pl.multiple_of(blk * SC_W, SC_W)
            return (pltpu.make_async_copy(slot_hbm.at[blk], idx[b], lsem.at[b]),
                    pltpu.make_async_copy(h_hbm.at[pl.ds(tok, SC_W)], rows[b], lsem.at[b]))

        def scatters(b):
            return [pltpu.make_async_copy(rows[b], xg_hbm.at[idx[b].at[kk]], ssem.at[b]) for kk in range(TOP_K)]

        for cp in loads(0, 0):
            cp.start()

        @pl.loop(0, nchunk, step=2)
        def _(j):
            for b in (0, 1):
                jj = j + b
                for cp in loads(jj, b):
                    cp.wait()
                for cp in scatters(b):
                    cp.start()

                @pl.when(jj + 1 < nchunk)
                def _():
                    @pl.when(jj >= 1)
                    def _():
                        for cp in scatters(1 - b):
                            cp.wait()
                    for cp in loads(jj + 1, 1 - b):
                        cp.start()

        for b in (0, 1):
            for cp in scatters(b):
                cp.wait()

    return k(h3, slot3)


NSUB = GM // GM_SUB
SUB_ROWS = GM_SUB * ROWS_PER_TOKEN


def _per_block_count(count, fn):
    for n in range(1, NSUB + 1):
        @pl.when(count == n)
        def _(n=n):
            fn(n)


def _gmm_kernel(layer, tile_e_ref, tile_blk_ref, tile_nsub_ref, tile_run_ref, tile_next_ref,
                xg_hbm, wg_hbm, wu_hbm, wd_hbm, yg_hbm,
                x_st, y_st, wg_st, wu_st, wd_st, wgu_scr, wd_scr, sems, xsems, ysems):
    j = pl.program_id(0)
    last_step = pl.num_programs(0) - 1
    nsub = tile_nsub_ref[j]
    run = tile_run_ref[j]
    slot = j % 2
    x_ref = x_st.at[slot]
    y_ref = y_st.at[slot]

    def tile_rows(t, n):
        return pl.ds(pl.multiple_of(tile_blk_ref[t] * SUB_ROWS, SUB_ROWS), n * SUB_ROWS)

    def x_copy(t, n):
        return pltpu.make_async_copy(xg_hbm.at[tile_rows(t, n)], x_st.at[t % 2, pl.ds(0, n * SUB_ROWS)], xsems.at[t % 2])

    def y_copy(t, n):
        return pltpu.make_async_copy(y_st.at[t % 2, pl.ds(0, n * SUB_ROWS)], yg_hbm.at[tile_rows(t, n)], ysems.at[t % 2])

    @pl.when(j == 0)
    def _():
        _per_block_count(nsub, lambda n: x_copy(j, n).start())

    _per_block_count(nsub, lambda n: x_copy(j, n).wait())

    @pl.when(j < last_step)
    def _():
        _per_block_count(tile_nsub_ref[j + 1], lambda n: x_copy(j + 1, n).start())

    @pl.when(j >= 2)
    def _():
        _per_block_count(tile_nsub_ref[j - 2], lambda n: y_copy(j - 2, n).wait())

    def weight_copies(e, slot):
        return [pltpu.make_async_copy(src.at[layer, e], dst.at[slot], sems.at[slot])
                for src, dst in ((wg_hbm, wg_st), (wu_hbm, wu_st), (wd_hbm, wd_st))]

    @pl.when(run >= 0)
    def _():
        @pl.when(j == 0)
        def _():
            for cp in weight_copies(tile_e_ref[j], run):
                cp.start()

        for cp in weight_copies(tile_e_ref[j], run):
            cp.wait()

        nxt = tile_next_ref[j]

        @pl.when(nxt >= 0)
        def _():
            for cp in weight_copies(nxt, 1 - run):
                cp.start()

        wgu_scr[:, 0:D_EXPERT] = wg_st[run].astype(BF16)
        wgu_scr[:, D_EXPERT:2 * D_EXPERT] = wu_st[run].astype(BF16)
        wd_scr[...] = wd_st[run].astype(BF16)

    def expert_mlp(s):
        x_hi, x_lo = _unpack_rows(_load_token_tiles(x_ref, GM_SUB, s * GM_SUB))
        gu = (jnp.dot(x_hi.astype(BF16), wgu_scr[0:PACKED, :], preferred_element_type=F32)
              + jnp.dot(x_lo.astype(BF16), wgu_scr[PACKED:D, :], preferred_element_type=F32))
        hid = _silu(gu[:, 0:D_EXPERT]) * gu[:, D_EXPERT:2 * D_EXPERT]
        y = jnp.dot(hid.astype(BF16), wd_scr[...], preferred_element_type=F32)
        _store_token_tiles(y_ref, _pack_rows(y), s * GM_SUB)

    def compute_and_send(n):
        for s in range(n):
            expert_mlp(s)
        y_copy(j, n).start()

    _per_block_count(nsub, compute_and_send)

    @pl.when(j == last_step)
    def _():
        _per_block_count(tile_nsub_ref[jnp.maximum(j - 1, 0)], lambda n: y_copy(j - 1, n).wait())
        _per_block_count(nsub, lambda n: y_copy(j, n).wait())


def _gmm(tile_e, tile_blk, tile_nsub, tile_run, tile_next, xg, l, w_gate, w_up, w_down):
    hbm = pl.BlockSpec(memory_space=pl.ANY)
    stage = pltpu.VMEM((2, GM * ROWS_PER_TOKEN, LANES), U32)
    grid_spec = pltpu.PrefetchScalarGridSpec(
        num_scalar_prefetch=5,
        grid=(NT_MAX,),
        in_specs=[hbm, hbm, hbm, hbm],
        out_specs=hbm,
        scratch_shapes=[stage, stage,
                        pltpu.VMEM((2, D, D_EXPERT), F32), pltpu.VMEM((2, D, D_EXPERT), F32),
                        pltpu.VMEM((2, D_EXPERT, D), F32),
                        pltpu.VMEM((D, 2 * D_EXPERT), BF16), pltpu.VMEM((D_EXPERT, D), BF16),
                        pltpu.SemaphoreType.DMA((2,)), pltpu.SemaphoreType.DMA((2,)), pltpu.SemaphoreType.DMA((2,))],
    )
    return pl.pallas_call(
        functools.partial(_gmm_kernel, l),
        grid_spec=grid_spec,
        out_shape=jax.ShapeDtypeStruct((SP * ROWS_PER_TOKEN, LANES), U32),
        compiler_params=_params(("arbitrary",)),
        name="moe_grouped_matmul",
    )(tile_e, tile_blk, tile_nsub, tile_run, tile_next, xg, w_gate, w_up, w_down)


def _sc_gather(table3, idx):
    n_idx = idx.shape[0]
    per_w = n_idx // SC_WORKERS
    nchunk = per_w // SC_W
    mesh = plsc.VectorSubcoreMesh(core_axis_name="c", subcore_axis_name="s")
    tile = (SC_W, ROWS_PER_TOKEN, LANES)

    @functools.partial(
        pl.kernel, mesh=mesh,
        out_type=jax.ShapeDtypeStruct((n_idx, ROWS_PER_TOKEN, LANES), table3.dtype),
        scratch_types=[pltpu.VMEM((per_w,), I32), pltpu.VMEM(tile, table3.dtype), pltpu.VMEM(tile, table3.dtype),
                       pltpu.SemaphoreType.DMA((2,)), pltpu.SemaphoreType.DMA((2,))],
    )
    def k(table_hbm, idx_hbm, out_hbm, idx_v, rows0, rows1, gsem, wsem):
        base = pl.multiple_of(_sc_worker_id() * per_w, per_w)
        rows = (rows0, rows1)
        pltpu.sync_copy(idx_hbm.at[pl.ds(base, per_w)], idx_v)

        def gather(j, b):
            ids = idx_v.at[pl.ds(pl.multiple_of(j * SC_W, SC_W), SC_W)]
            return pltpu.make_async_copy(table_hbm.at[ids], rows[b], gsem.at[b])

        def write(j, b):
            dst = out_hbm.at[pl.ds(pl.multiple_of(base + j * SC_W, SC_W), SC_W)]
            return pltpu.make_async_copy(rows[b], dst, wsem.at[b])

        gather(0, 0).start()

        @pl.loop(0, nchunk, step=2)
        def _(j):
            for b in (0, 1):
                jj = j + b
                gather(jj, b).wait()
                write(jj, b).start()

                @pl.when(jj + 1 < nchunk)
                def _():
                    @pl.when(jj >= 1)
                    def _():
                        write(jj - 1, 1 - b).wait()
                    gather(jj + 1, 1 - b).start()

        write(nchunk - 2, 0).wait()
        write(nchunk - 1, 1).wait()

    return k(table3, idx)


def _combine_kernel(wn_ref, x_ref, mod_ref, gpre_ref, gp_ref, wsg_ref, wsu_ref, wsd_ref, y_ref, o_ref, eye_ref):
    @pl.when(pl.program_id(0) == 0)
    def _():
        r = lax.broadcasted_iota(I32, (TD, TD), 0)
        c = lax.broadcasted_iota(I32, (TD, TD), 1)
        eye_ref[...] = jnp.where(r == c, 1.0, 0.0).astype(BF16)

    m = mod_ref[0, 0]
    x = x_ref[...]
    hb = (_rms(x, gpre_ref[...]) * (1.0 + m[4:5]) + m[3:4]).astype(BF16)
    hid = (_silu(jnp.dot(hb, wsg_ref[...], preferred_element_type=F32))
           * jnp.dot(hb, wsu_ref[...], preferred_element_type=F32))
    acc = jnp.dot(hid.astype(BF16), wsd_ref[...], preferred_element_type=F32)

    eye = eye_ref[...]
    nt = lambda a, b: lax.dot_general(a, b, (((1,), (1,)), ((), ())), preferred_element_type=F32)
    w1, w2, w3 = _split3(wn_ref[...])
    w_t = nt(eye, w1) + nt(eye, w2) + nt(eye, w3)

    acc_hi = acc[:, :PACKED]
    acc_lo = acc[:, PACKED:]
    for k in range(TOP_K):
        y_hi, y_lo = _unpack_rows(_load_token_tiles(y_ref, TD, k * TD))
        acc_hi = acc_hi + y_hi * w_t[:, k:k + 1]
        acc_lo = acc_lo + y_lo * w_t[:, k:k + 1]
    acc = jnp.concatenate([acc_hi, acc_lo], axis=1)
    o_ref[...] = x + m[5:6] * _rms(acc, gp_ref[...])


def _combine(wn, x, mod, l, g_pre, g_post, ws_gate, ws_up, ws_down, ybuf, first_tok, n_tok):
    off = first_tok // TD
    tiles_per_dec = DEC_SEQ // TD
    npd = TP // TD
    mod_row = lambda i: jnp.where(i + off < npd, 0, 1 + (i + off - npd) // tiles_per_dec)
    full = lambda *shape: pl.BlockSpec(shape, lambda i: (0,) * len(shape))
    y_spec = pl.BlockSpec((TOP_K * TD * ROWS_PER_TOKEN, LANES), lambda i: (i, 0))
    return pl.pallas_call(
        _combine_kernel,
        grid=(n_tok // TD,),
        in_specs=[
            pl.BlockSpec((TOP_K, TD), lambda i: (0, i + off)),
            pl.BlockSpec((TD, D), lambda i: (i + off, 0)),
            pl.BlockSpec((1, 1, 6, D), lambda i: (l, mod_row(i), 0, 0)),
            full(1, D), full(1, D), full(D, D_SHARED), full(D, D_SHARED), full(D_SHARED, D),
            y_spec,
        ],
        out_specs=pl.BlockSpec((TD, D), lambda i: (i, 0)),
        out_shape=jax.ShapeDtypeStruct((n_tok, D), F32),
        scratch_shapes=[pltpu.VMEM((TD, TD), BF16)],
        compiler_params=_params(("arbitrary",)),
        name="moe_combine",
    )(wn, x, mod, g_pre, g_post, ws_gate, ws_up, ws_down, ybuf)


def _moe_layer(x, h, mod, l, g_pre, g_post, w_router, e_bias, w_gate, w_up, w_down,
               ws_gate, ws_up, ws_down):
    top_e, wn, rk, cnt = _router(h, w_router.T, e_bias.reshape(N_EXPERTS, 1))
    cnt = cnt.reshape(N_EXPERTS).astype(I32)
    padded = (cnt + GM_SUB - 1) // GM_SUB * GM_SUB
    ends = jnp.cumsum(padded)
    offs = ends - padded
    eid = jnp.arange(N_EXPERTS, dtype=I32)[:, None, None]
    slot = rk + jnp.sum(jnp.where(top_e[None] == eid, offs[:, None, None], 0), axis=0)
    ntile = (cnt + GM - 1) // GM
    tile_ends = jnp.cumsum(ntile)
    tid = jnp.arange(NT_MAX, dtype=I32)
    last = jnp.maximum(tile_ends[-1] - 1, 0)
    tile_e = jnp.minimum(jnp.sum((tid[:, None] >= tile_ends[None, :]).astype(I32), axis=1), N_EXPERTS - 1)
    tile_e = jnp.where(tid <= last, tile_e, tile_e[last])
    own = tile_e[:, None] == jnp.arange(N_EXPERTS, dtype=I32)[None, :]
    pick = lambda per_expert: jnp.sum(jnp.where(own, per_expert[None, :], 0), axis=1)
    t_in = tid - pick(tile_ends - ntile)
    tile_blk = jnp.where(tid <= last, (pick(offs) + GM * t_in) // GM_SUB, 0)
    tile_rows = jnp.clip(pick(cnt) - GM * t_in, 0, GM)
    tile_nsub = jnp.where(tid <= last, (tile_rows + GM_SUB - 1) // GM_SUB, 0)
    starts = jnp.logical_and(jnp.arange(NT_MAX) <= last,
                             jnp.concatenate([jnp.ones((1,), bool), tile_e[1:] != tile_e[:-1]]))
    tile_run = jnp.where(starts, (jnp.cumsum(starts.astype(I32)) - 1) % 2, -1)
    later = jnp.logical_and(jnp.arange(N_EXPERTS, dtype=I32)[None, :] > tile_e[:, None], (cnt > 0)[None, :])
    tile_next = jnp.min(jnp.where(later, jnp.arange(N_EXPERTS, dtype=I32)[None, :], N_EXPERTS), axis=1)
    tile_next = jnp.where(tile_next < N_EXPERTS, tile_next, -1)
    slot3 = slot.reshape(TOP_K, T // SC_W, SC_W).transpose(1, 0, 2)
    xg = _sc_dispatch(h.reshape(T, ROWS_PER_TOKEN, LANES), slot3)
    xg = _pad_fill(offs + cnt, padded - cnt, xg).reshape(SP * ROWS_PER_TOKEN, LANES)
    yg = _gmm(tile_e, tile_blk, tile_nsub.astype(I32), tile_run.astype(I32), tile_next.astype(I32),
              xg, l, w_gate, w_up, w_down)
    yg3 = yg.reshape(SP, ROWS_PER_TOKEN, LANES)
    ws = (ws_gate.astype(BF16), ws_up.astype(BF16), ws_down.astype(BF16))
    outs = []
    for first_tok, n_tok in ((0, TP), (TP, TS)):
        ids = slot[:, first_tok:first_tok + n_tok].reshape(TOP_K, n_tok // TD, TD).transpose(1, 0, 2)
        ybuf = _sc_gather(yg3, ids.reshape(TOP_K * n_tok))
        outs.append(_combine(wn, x, mod, l, g_pre, g_post, *ws,
                             ybuf.reshape(TOP_K * n_tok * ROWS_PER_TOKEN, LANES), first_tok, n_tok))
    return outs


def _rope_tables():
    n = DEC_SEQ
    rows = n // GRID_W
    row = jnp.repeat(jnp.arange(rows), GRID_W).astype(F32)
    col = jnp.tile(jnp.arange(GRID_W), rows).astype(F32)
    half = DQK_B // 2
    inv = ROPE_BASE ** (-jnp.arange(0, half, 2, dtype=F32) / half)
    ang_r = row[:, None] * inv
    ang_c = col[:, None] * inv
    ang = jnp.concatenate([ang_r, ang_r, ang_c, ang_c], axis=-1)
    reps = QK_B // DQK_B
    return jnp.tile(jnp.cos(ang), (1, reps)), jnp.tile(jnp.sin(ang), (1, reps))


def _pad_in_proj(w):
    s = [0, Q_A, 2 * Q_A, 2 * Q_A + V_A, 2 * Q_A + 2 * V_A]
    s += [s[-1] + GATE_RANK, s[-1] + 2 * GATE_RANK]
    s += [s[-1] + QK_B, s[-1] + 2 * QK_B, s[-1] + 2 * QK_B + V_B]
    gates = jnp.pad(w[:, s[4]:s[6]], ((0, 0), (0, GL_PAD - 2 * GATE_RANK)))
    return jnp.concatenate([w[:, s[0]:s[4]], gates, w[:, s[6]:s[9]]], axis=1).astype(BF16)


def kernel(x_prompt, x_sample, c, c_ctx, state_gla, cache_k, cache_v, ada_w, ada_b, norm_pre_mix, norm_post_mix, norm_pre_ffn, norm_post_ffn, ab_w_in, gla_w_g2, gla_b_g2, gla_norm_g, diff_lambda, diff_norm_g, ab_w_out, sgu_w_in, sgu_b_in, sgu_norm_g, sgu_w_s, sgu_b_s, sgu_w_out, moe_w_router, moe_e_bias, moe_w_gate, moe_w_up, moe_w_down, moe_ws_gate, moe_ws_up, moe_ws_down):
    depth = ada_w.shape[0]
    assert x_prompt.shape == (BATCH, SEQ, D) and x_sample.shape == (DEC_BATCH, DEC_SEQ, D)
    assert state_gla.shape == (DEC_BATCH, (depth + 1) // 2, 2, H_A, DK_A, DV_A)
    assert cache_k.shape == (DEC_BATCH, 1, H_B, 2, PAST_LEN, DQK_B) and cache_v.shape == (DEC_BATCH, 1, H_B, PAST_LEN, DV_B)
    assert depth == 2 and moe_w_gate.shape == (depth, N_EXPERTS, D, D_EXPERT)
    xp, xs = x_prompt.reshape(TP, D), x_sample.reshape(TS, D)
    cond = jnp.concatenate([c_ctx[None, :], c, jnp.zeros((8 - 1 - DEC_BATCH, D), F32)], axis=0)
    mod = _modulation(cond, ada_w, ada_b)
    cos, sin = _rope_tables()
    vec = lambda a: a.reshape(1, -1)
    new_s = new_k = new_v = None
    for l in range(depth):
        if l % 2 == 0:
            e = l // 2
            lam_init = 0.8 - 0.6 * math.exp(-0.3 * l)
            a, r_a, gl, q_b, k_b, v_b, new_k, new_v = _in_proj(xp, xs, mod, l, vec(norm_pre_mix[l]),
                                                               _pad_in_proj(ab_w_in[e]), cos, sin)
            s0_t = jnp.swapaxes(state_gla[:, e], -1, -2)
            same_head = jnp.eye(H_A, dtype=bool)[None, None, :, None, :, None]
            s0_t = jnp.where(same_head, s0_t[:, :, :, :, None, :], 0.0).reshape(DEC_BATCH, 2, V_A, Q_A)
            o_f, o_bw, s_fin_t = _gla(a, gl, gla_w_g2[e], gla_b_g2[e].reshape(2, 1, Q_A), s0_t)
            o_att = _diff_attention(q_b, k_b, v_b, cache_k, cache_v, diff_lambda[e], lam_init)
            x, h = _mix_out(lam_init, o_f, o_bw, r_a, o_att, xp, xs, mod, l, vec(gla_norm_g[e]), vec(diff_norm_g[e]),
                            ab_w_out[e].astype(BF16), vec(norm_post_mix[l]), vec(norm_pre_ffn[l]))
            new_s = jnp.swapaxes(s_fin_t, -1, -2)[:, None]
        else:
            o = l // 2
            x, h = _sgu(xp, xs, mod, l, vec(norm_pre_mix[l]), sgu_w_in[o].astype(BF16), vec(sgu_b_in[o]),
                        vec(sgu_norm_g[o]), sgu_w_s[o], sgu_b_s[o].T, sgu_w_out[o].astype(BF16),
                        vec(norm_post_mix[l]), vec(norm_pre_ffn[l]))
        xp, xs = _moe_layer(x, h, mod, l, vec(norm_pre_ffn[l]), vec(norm_post_ffn[l]), moe_w_router[l], moe_e_bias[l],
                            moe_w_gate, moe_w_up, moe_w_down, moe_ws_gate[l], moe_ws_up[l], moe_ws_down[l])
    y_prompt = xp.reshape(BATCH, SEQ, D)
    y_sample = xs.reshape(DEC_BATCH, DEC_SEQ, D)
    return (y_prompt, y_sample, new_s, new_k, new_v)
```

```python
import functools
import math

import jax
import jax.numpy as jnp
from jax import lax
from jax.experimental import pallas as pl
from jax.experimental.pallas import tpu as pltpu
from jax.experimental.pallas import tpu_sc as plsc

F32 = jnp.float32
BF16 = jnp.bfloat16
I32 = jnp.int32

D = 1024
BATCH, SEQ = 32, 256
DEC_BATCH, DEC_SEQ = 4, 2048
PAST_LEN = 256
GRID_W = 64
EPS = 1e-6
TP = BATCH * SEQ
TS = DEC_BATCH * DEC_SEQ
T = TP + TS
H_A, DK_A, DV_A = 4, 64, 128
Q_A, V_A = H_A * DK_A, H_A * DV_A
GATE_RANK, GATE_TAU, GLA_CHUNK = 16, 16.0, 64
H_B, DQK_B, DV_B = 4, 64, 128
QK_B, V_B = H_B * 2 * DQK_B, H_B * DV_B
ROPE_BASE = 10000.0
SGU_DIM, SGU_GROUPS, SGU_CHUNK = 1024, 4, 128
N_EXPERTS, TOP_K, N_GROUPS, TOPK_GROUPS = 64, 8, 8, 4
GROUP_SIZE = N_EXPERTS // N_GROUPS
D_EXPERT, D_SHARED = 256, 256
ROUTED_SCALE = 2.5

TM = 512
TM_SUB = 256
MIX_DTYPE = jnp.bfloat16
NPT = TP // TM
TILES_PER_DEC = DEC_SEQ // TM
SEG = 256
NSEG = T // SEG
NSEG_P = TP // SEG
SEG_PER_DEC = DEC_SEQ // SEG
TR = 512
TD = 512
GM = 2048
GM_SUB = 256
NT_MAX = T * TOP_K // GM + N_EXPERTS
SP = T * TOP_K + N_EXPERTS * GM_SUB
GL_PAD = 128
VMEM_LIMIT = 56 * 1024 * 1024
NEG_INF = float("-inf")


def _bdot(a, b):
    return jnp.dot(a.astype(BF16), b.astype(BF16), preferred_element_type=F32)


def _bdot_nt(a, b):
    return lax.dot_general(a.astype(BF16), b.astype(BF16), (((1,), (1,)), ((), ())),
                           preferred_element_type=F32)


def _bdot_tn(a, b):
    return lax.dot_general(a.astype(BF16), b.astype(BF16), (((0,), (0,)), ((), ())),
                           preferred_element_type=F32)


def _split3(x):
    x1 = x.astype(BF16)
    r1 = x - x1.astype(F32)
    x2 = r1.astype(BF16)
    x3 = (r1 - x2.astype(F32)).astype(BF16)
    return x1, x2, x3


def _rms(x, g):
    return x * lax.rsqrt(jnp.mean(x * x, axis=-1, keepdims=True) + EPS) * g


def _silu(x):
    return x * jax.nn.sigmoid(x)


def _mod_row(i):
    return jnp.where(i < NPT, 0, 1 + (i - NPT) // TILES_PER_DEC)


def _params(sem, limit=VMEM_LIMIT):
    return pltpu.CompilerParams(dimension_semantics=sem, vmem_limit_bytes=limit)


def _mod_kernel(c_ref, w_ref, b_ref, o_ref):
    o_ref[0] = _bdot(_silu(c_ref[...]), w_ref[0]) + b_ref[0]


def _modulation(cond, ada_w, ada_b):
    depth = ada_w.shape[0]
    nj = 6
    out = pl.pallas_call(
        _mod_kernel,
        grid=(depth, nj),
        in_specs=[
            pl.BlockSpec((8, D), lambda l, j: (0, 0)),
            pl.BlockSpec((1, D, D), lambda l, j: (l, 0, j)),
            pl.BlockSpec((1, 1, D), lambda l, j: (l, 0, j)),
        ],
        out_specs=pl.BlockSpec((1, 8, D), lambda l, j: (l, 0, j)),
        out_shape=jax.ShapeDtypeStruct((depth, 8, 6 * D), F32),
        compiler_params=_params(("arbitrary", "arbitrary")),
        name="adaln_modulation",
    )(cond, ada_w, ada_b.reshape(depth, 1, 6 * D))
    return out.reshape(depth, 8, 6, D)


_C_A = 0
_C_R = _C_A + 2 * Q_A + V_A
_C_GL = _C_R + V_A
_C_Q = _C_GL + GL_PAD
_C_K = _C_Q + QK_B
_C_V = _C_K + QK_B
_C_END = _C_V + V_B
ROT_PAIR = DQK_B // 4


def _rope(x, cos, sin):
    lane = lax.broadcasted_iota(I32, x.shape, 1)
    first = (lane % (2 * ROT_PAIR)) < ROT_PAIR
    n = x.shape[1]
    xr = jnp.where(first, -pltpu.roll(x, n - ROT_PAIR, 1), pltpu.roll(x, ROT_PAIR, 1))
    return x * cos + xr * sin


def _stream_specs():
    return [pl.BlockSpec((TM, D), lambda i: (jnp.minimum(i, NPT - 1), 0)),
            pl.BlockSpec((TM, D), lambda i: (jnp.maximum(i - NPT, 0), 0))]


def _stream_tile(xp_ref, xs_ref):
    return jnp.where(pl.program_id(0) < NPT, xp_ref[...], xs_ref[...])


def _in_kernel(xp_ref, xs_ref, mod_ref, g_ref, w_ref, cos_ref, sin_ref,
               a_ref, r_ref, gl_ref, q_ref, k_ref, v_ref, ck_ref, cv_ref, k32_ref, v32_ref):
    latent = pl.program_id(0) >= NPT
    m = mod_ref[0, 0]
    x = _stream_tile(xp_ref, xs_ref)
    for s in range(TM // TM_SUB):
        rows = slice(s * TM_SUB, (s + 1) * TM_SUB)
        h = _rms(x[rows], g_ref[...]) * (1.0 + m[1:2]) + m[0:1]
        hb = h.astype(BF16)

        def proj(c0, c1):
            return jnp.dot(hb, w_ref[:, c0:c1], preferred_element_type=F32)

        a_ref[rows, :] = proj(_C_A, _C_R)
        r_ref[rows, :] = proj(_C_R, _C_GL).astype(MIX_DTYPE)
        gl_ref[rows, :] = proj(_C_GL, _C_Q)
        v = proj(_C_V, _C_END)
        q = proj(_C_Q, _C_K)
        k = proj(_C_K, _C_V)
        cos = cos_ref[rows, :]
        sin = sin_ref[rows, :]
        q_ref[rows, :] = jnp.where(latent, _rope(q, cos, sin), q).astype(BF16)
        k_ref[rows, :] = jnp.where(latent, _rope(k, cos, sin), k).astype(BF16)
        v_ref[rows, :] = v.astype(BF16)
        k32_ref[rows, :] = k
        v32_ref[rows, :] = v

    @pl.when(jnp.logical_not(latent))
    def _():
        for s in range(TM // SEQ):
            rows = slice(s * SEQ, (s + 1) * SEQ)
            for h in range(H_B):
                cv_ref[s, 0, h] = v32_ref[rows, h * DV_B:(h + 1) * DV_B]
                for mp in range(2):
                    ck_ref[s, 0, h, mp] = k32_ref[rows, (2 * h + mp) * DQK_B:(2 * h + mp + 1) * DQK_B]


def _in_proj(xp, xs, mod, l, g, w_pad, cos, sin):
    tok = lambda width: pl.BlockSpec((TM, width), lambda i: (i, 0))
    rope_spec = pl.BlockSpec((TM, QK_B), lambda i: (jnp.maximum(i - NPT, 0) % TILES_PER_DEC, 0))
    widths = (2 * Q_A + V_A, V_A, GL_PAD, QK_B, QK_B, V_B)
    dtypes = (F32, MIX_DTYPE, F32, BF16, BF16, BF16)
    seqs = TM // SEQ
    prompt_tile = lambda i: jnp.minimum(i, NPT - 1)
    cache_specs = [pl.BlockSpec((seqs, 1, H_B, 2, SEQ, DQK_B), lambda i: (prompt_tile(i), 0, 0, 0, 0, 0)),
                   pl.BlockSpec((seqs, 1, H_B, SEQ, DV_B), lambda i: (prompt_tile(i), 0, 0, 0, 0))]
    cache_shapes = [jax.ShapeDtypeStruct((BATCH, 1, H_B, 2, SEQ, DQK_B), F32),
                    jax.ShapeDtypeStruct((BATCH, 1, H_B, SEQ, DV_B), F32)]
    return pl.pallas_call(
        _in_kernel,
        grid=(T // TM,),
        in_specs=_stream_specs() + [
            pl.BlockSpec((1, 1, 6, D), lambda i: (l, _mod_row(i), 0, 0)),
            pl.BlockSpec((1, D), lambda i: (0, 0)),
            pl.BlockSpec((D, _C_END), lambda i: (0, 0)),
            rope_spec, rope_spec,
        ],
        out_specs=[tok(w) for w in widths] + cache_specs,
        out_shape=[jax.ShapeDtypeStruct((T, w), dt) for w, dt in zip(widths, dtypes)] + cache_shapes,
        scratch_shapes=[pltpu.VMEM((TM, QK_B), F32), pltpu.VMEM((TM, V_B), F32)],
        compiler_params=_params(("arbitrary",)),
        name="mixer_ab_in_proj",
    )(xp, xs, mod, g, w_pad, cos, sin)


def _log_sigmoid(x):
    return jnp.minimum(x, 0.0) - jnp.log(1.0 + jnp.exp(-jnp.abs(x)))


def _gla_kernel(af_ref, ab_ref, glf_ref, glb_ref, wg_ref, bg_ref, s0_ref,
                of_ref, ob_ref, sfin_ref, st_ref):
    i = pl.program_id(0)

    @pl.when(i < NSEG_P)
    def _():
        st_ref[...] = jnp.zeros_like(st_ref)

    @pl.when(jnp.logical_and(i >= NSEG_P, (i - NSEG_P) % SEG_PER_DEC == 0))
    def _():
        st_ref[...] = s0_ref[0]

    r = lax.broadcasted_iota(I32, (SEG, SEG), 0)
    c = lax.broadcasted_iota(I32, (SEG, SEG), 1)
    same = (r // GLA_CHUNK) == (c // GLA_CHUNK)
    nchunk = SEG // GLA_CHUNK
    own_head = (lax.broadcasted_iota(I32, (V_A, Q_A), 0) // DV_A) == (lax.broadcasted_iota(I32, (V_A, Q_A), 1) // DK_A)

    dirs = []
    for d, (a_ref, gl_ref, o_ref) in enumerate(((af_ref, glf_ref, of_ref), (ab_ref, glb_ref, ob_ref))):
        fwd = d == 0
        gcol = gl_ref[:, d * GATE_RANK:(d + 1) * GATE_RANK]
        la = _log_sigmoid(_bdot(gcol, wg_ref[d]) + bg_ref[d]) / GATE_TAU
        causal = jnp.logical_and(same, (c <= r) if fwd else (c >= r))
        tri = jnp.where(causal, 1.0, 0.0).astype(BF16)
        l1, l2, l3 = _split3(la)
        b_all = (jnp.dot(tri, l1, preferred_element_type=F32)
                 + jnp.dot(tri, l2, preferred_element_type=F32)
                 + jnp.dot(tri, l3, preferred_element_type=F32))
        q_in_all = a_ref[:, 0:Q_A] * (DK_A ** -0.5) * jnp.exp(b_all)
        kd_all = a_ref[:, Q_A:2 * Q_A] * jnp.exp(-b_all)
        intra = []
        for h in range(H_A):
            kc = slice(h * DK_A, (h + 1) * DK_A)
            attn = jnp.where(causal, _bdot_nt(q_in_all[:, kc], kd_all[:, kc]), 0.0)
            intra.append(_bdot(attn, a_ref[:, 2 * Q_A + h * DV_A:2 * Q_A + (h + 1) * DV_A]))
        intra = jnp.concatenate(intra, axis=1)
        dirs.append(dict(fwd=fwd, a=a_ref, o=o_ref, b=b_all, q=q_in_all, intra=intra, state=st_ref[d]))

    for step in range(nchunk):
        for dd in dirs:
            ch = step if dd["fwd"] else nchunk - 1 - step
            r0 = ch * GLA_CHUNK
            rows = slice(r0, r0 + GLA_CHUNK)
            end = r0 + GLA_CHUNK - 1 if dd["fwd"] else r0
            b_end = dd["b"][end:end + 1, :]
            kw = dd["a"][rows, Q_A:2 * Q_A] * jnp.exp(b_end - dd["b"][rows, :])
            dd["o"][rows, :] = (dd["intra"][rows, :] + _bdot_nt(dd["q"][rows, :], dd["state"])).astype(MIX_DTYPE)
            kv_t = _bdot_tn(dd["a"][rows, 2 * Q_A:2 * Q_A + V_A], kw)
            dd["state"] = dd["state"] * jnp.exp(b_end) + jnp.where(own_head, kv_t, 0.0)
    for d, dd in enumerate(dirs):
        st_ref[d] = dd["state"]

    @pl.when(i < NSEG_P)
    def _():
        for d in range(2):
            for h in range(H_A):
                sfin_ref[0, d, h] = st_ref[d, h * DV_A:(h + 1) * DV_A, h * DK_A:(h + 1) * DK_A]


def _seg_bwd(i):
    j = i - NSEG_P
    return jnp.where(i < NSEG_P, i, NSEG_P + (j // SEG_PER_DEC) * SEG_PER_DEC + (SEG_PER_DEC - 1 - j % SEG_PER_DEC))


def _gla(a, gl, wg, bg, s0_t):
    seg = lambda width, f: pl.BlockSpec((SEG, width), lambda i: (f(i), 0))
    ident = lambda i: i
    st_block = (1, 2, H_A, DV_A, DK_A)
    return pl.pallas_call(
        _gla_kernel,
        grid=(NSEG,),
        in_specs=[
            seg(D, ident), seg(D, _seg_bwd), seg(GL_PAD, ident), seg(GL_PAD, _seg_bwd),
            pl.BlockSpec((2, GATE_RANK, Q_A), lambda i: (0, 0, 0)),
            pl.BlockSpec((2, 1, Q_A), lambda i: (0, 0, 0)),
            pl.BlockSpec((1, 2, V_A, Q_A), lambda i: (jnp.maximum(i - NSEG_P, 0) // SEG_PER_DEC, 0, 0, 0)),
        ],
        out_specs=[
            seg(V_A, ident), seg(V_A, _seg_bwd),
            pl.BlockSpec(st_block, lambda i: (jnp.minimum(i, NSEG_P - 1), 0, 0, 0, 0)),
        ],
        out_shape=[
            jax.ShapeDtypeStruct((T, V_A), MIX_DTYPE),
            jax.ShapeDtypeStruct((T, V_A), MIX_DTYPE),
            jax.ShapeDtypeStruct((BATCH, 2, H_A, DV_A, DK_A), F32),
        ],
        scratch_shapes=[pltpu.VMEM((2, V_A, Q_A), F32)],
        compiler_params=_params(("arbitrary",)),
        name="gla_bidir",
    )(a, a, gl, gl, wg, bg, s0_t)


def _diff_lambda(lam_ref, lam_init):
    lp = lam_ref[...]
    s01 = jnp.sum(lp[0:1] * lp[1:2], axis=1, keepdims=True)
    s23 = jnp.sum(lp[2:3] * lp[3:4], axis=1, keepdims=True)
    return jnp.exp(s01) - jnp.exp(s23) + lam_init


def _attn_prompt_kernel(lam_init, q_ref, k_ref, v_ref, lam_ref, o_ref):
    lam = _diff_lambda(lam_ref, lam_init)
    for h in range(H_B):
        ps = []
        for m in range(2):
            cols = slice((2 * h + m) * DQK_B, (2 * h + m + 1) * DQK_B)
            s = _bdot_nt(q_ref[:, cols] * (DQK_B ** -0.5), k_ref[:, cols])
            e = jnp.exp(s - jnp.max(s, axis=1, keepdims=True))
            ps.append(e * (1.0 / jnp.sum(e, axis=1, keepdims=True)))
        w = ps[0] - lam * ps[1]
        o_ref[:, h * DV_B:(h + 1) * DV_B] = _bdot(w, v_ref[:, h * DV_B:(h + 1) * DV_B]).astype(MIX_DTYPE)


def _attn_sample_kernel(lam_init, q_ref, k_ref, v_ref, ck_ref, cv_ref, lam_ref, o_ref):
    lam = _diff_lambda(lam_ref, lam_init)

    def scores(h):
        out = []
        for m in range(2):
            cols = slice((2 * h + m) * DQK_B, (2 * h + m + 1) * DQK_B)
            q = q_ref[:, cols] * (DQK_B ** -0.5)
            out.append((_bdot_nt(q, ck_ref[0, 0, h, m]), _bdot_nt(q, k_ref[:, cols])))
        return out

    def finish(h, head_scores):
        parts = []
        for m, (sc, sn) in enumerate(head_scores):
            mx = jnp.maximum(jnp.max(sc, axis=1, keepdims=True), jnp.max(sn, axis=1, keepdims=True))
            ec = jnp.exp(sc - mx)
            en = jnp.exp(sn - mx)
            inv = (1.0 if m == 0 else -lam) / (jnp.sum(ec, axis=1, keepdims=True) + jnp.sum(en, axis=1, keepdims=True))
            parts.append((ec * inv, en * inv))
        wc = parts[0][0] + parts[1][0]
        wn = parts[0][1] + parts[1][1]
        o_ref[:, h * DV_B:(h + 1) * DV_B] = (_bdot(wc, cv_ref[0, 0, h])
                                             + _bdot(wn, v_ref[:, h * DV_B:(h + 1) * DV_B])).astype(MIX_DTYPE)

    pending = scores(0)
    for h in range(H_B):
        ahead = scores(h + 1) if h + 1 < H_B else None
        finish(h, pending)
        pending = ahead


QB = SEQ
NQB_DEC = DEC_SEQ // QB


def _attn_kernel(lam_init, q_ref, kp_ref, vp_ref, ks_ref, vs_ref, ck_ref, cv_ref, lam_ref, o_ref):
    i = pl.program_id(0)

    @pl.when(i < BATCH)
    def _():
        _attn_prompt_kernel(lam_init, q_ref, kp_ref, vp_ref, lam_ref, o_ref)

    @pl.when(i >= BATCH)
    def _():
        _attn_sample_kernel(lam_init, q_ref, ks_ref, vs_ref, ck_ref, cv_ref, lam_ref, o_ref)


def _diff_attention(q, k, v, cache_k, cache_v, lam_p, lam_init):
    blk = lambda rows, f: pl.BlockSpec((rows, QK_B), f)
    dec_b = lambda i: jnp.maximum(i - BATCH, 0) // NQB_DEC
    own = lambda i: (i, 0)
    prompt_kv = lambda i: (jnp.minimum(i, BATCH - 1), 0)
    dec_kv = lambda i: (TP // DEC_SEQ + dec_b(i), 0)
    return pl.pallas_call(
        functools.partial(_attn_kernel, lam_init),
        grid=(BATCH + DEC_BATCH * NQB_DEC,),
        in_specs=[
            blk(QB, own), blk(SEQ, prompt_kv), blk(SEQ, prompt_kv), blk(DEC_SEQ, dec_kv), blk(DEC_SEQ, dec_kv),
            pl.BlockSpec((1, 1, H_B, 2, PAST_LEN, DQK_B), lambda i: (dec_b(i), 0, 0, 0, 0, 0)),
            pl.BlockSpec((1, 1, H_B, PAST_LEN, DV_B), lambda i: (dec_b(i), 0, 0, 0, 0)),
            pl.BlockSpec((4, DQK_B), lambda i: (0, 0)),
        ],
        out_specs=blk(QB, own),
        out_shape=jax.ShapeDtypeStruct((T, V_B), MIX_DTYPE),
        compiler_params=_params(("arbitrary",)),
        name="diff_attention",
    )(q, k, v, k, v, cache_k, cache_v, lam_p)


def _head_rms(x, g, nheads, width):
    return jnp.concatenate([_rms(x[:, h * width:(h + 1) * width], g) for h in range(nheads)], axis=1)


def _mix_out_kernel(lam_init, of_ref, ob_ref, r_ref, oatt_ref, xp_ref, xs_ref, mod_ref,
                    gg_ref, dg_ref, wo_ref, gp_ref, gffn_ref, o_ref, hp_ref):
    m = mod_ref[0, 0]
    load = lambda ref: ref[...].astype(F32)
    o_a = _head_rms(load(of_ref) + load(ob_ref), gg_ref[...], H_A, DV_A) * _silu(load(r_ref))
    o_b = _head_rms(load(oatt_ref), dg_ref[...], H_B, DV_B) * (1.0 - lam_init)
    out = _bdot(o_a, wo_ref[0:V_A, :]) + _bdot(o_b, wo_ref[V_A:V_A + V_B, :])
    x1 = _stream_tile(xp_ref, xs_ref) + m[2:3] * _rms(out, gp_ref[...])
    o_ref[...] = x1
    _store_token_tiles(hp_ref, _ffn_input_rows(x1, m, gffn_ref[...]))


def _mix_out(lam_init, o_f, o_b, r_a, o_att, xp, xs, mod, l, gla_g, diff_g, w_o, g_post, g_ffn):
    tok = lambda width: pl.BlockSpec((TM, width), lambda i: (i, 0))
    vec = lambda width: pl.BlockSpec((1, width), lambda i: (0, 0))
    return pl.pallas_call(
        functools.partial(_mix_out_kernel, lam_init),
        grid=(T // TM,),
        in_specs=[
            tok(V_A), tok(V_A), tok(V_A), tok(V_B), *_stream_specs(),
            pl.BlockSpec((1, 1, 6, D), lambda i: (l, _mod_row(i), 0, 0)),
            vec(DV_A), vec(DV_B),
            pl.BlockSpec((V_A + V_B, D), lambda i: (0, 0)),
            vec(D), vec(D),
        ],
        out_specs=[tok(D), pl.BlockSpec((TM * ROWS_PER_TOKEN, LANES), lambda i: (i, 0))],
        out_shape=[jax.ShapeDtypeStruct((T, D), F32), jax.ShapeDtypeStruct((T * ROWS_PER_TOKEN, LANES), U32)],
        compiler_params=_params(("arbitrary",)),
        name="mixer_ab_out",
    )(o_f, o_b, r_a, o_att, xp, xs, mod, gla_g, diff_g, w_o, g_post, g_ffn)


def _gelu_tanh(x):
    return 0.5 * x * (1.0 + jnp.tanh(math.sqrt(2.0 / math.pi) * (x + 0.044715 * (x * x * x))))


def _sgu_kernel(xp_ref, xs_ref, mod_ref, gpre_ref, win_ref, bin_ref, vg_ref, ws_ref, bs_ref,
                wout_ref, gpost_ref, gffn_ref, o_ref, hp_ref, t_ref):
    m = mod_ref[0, 0]
    x = _stream_tile(xp_ref, xs_ref)
    h = _rms(x, gpre_ref[...]) * (1.0 + m[1:2]) + m[0:1]
    z = _gelu_tanh(_bdot(h, win_ref[...]) + bin_ref[...])
    v = _rms(z[:, SGU_DIM:], vg_ref[...])
    gw = SGU_DIM // SGU_GROUPS
    for ch in range(TM // SGU_CHUNK):
        rows = slice(ch * SGU_CHUNK, (ch + 1) * SGU_CHUNK)
        for g in range(SGU_GROUPS):
            cols = slice(g * gw, (g + 1) * gw)
            vs = _bdot(ws_ref[g], v[rows, cols]) + bs_ref[:, g:g + 1]
            t_ref[rows, cols] = (z[rows, cols] * vs).astype(BF16)
    out = jnp.dot(t_ref[...], wout_ref[...], preferred_element_type=F32)
    x1 = x + m[2:3] * _rms(out, gpost_ref[...])
    o_ref[...] = x1
    _store_token_tiles(hp_ref, _ffn_input_rows(x1, m, gffn_ref[...]))


def _sgu(xp, xs, mod, l, g_pre, w_in, b_in, v_g, w_s, b_s_t, w_out, g_post, g_ffn):
    tok = pl.BlockSpec((TM, D), lambda i: (i, 0))
    full = lambda *shape: pl.BlockSpec(shape, lambda i: (0,) * len(shape))
    return pl.pallas_call(
        _sgu_kernel,
        grid=(T // TM,),
        in_specs=_stream_specs() + [
            pl.BlockSpec((1, 1, 6, D), lambda i: (l, _mod_row(i), 0, 0)),
            full(1, D), full(D, 2 * SGU_DIM), full(1, 2 * SGU_DIM), full(1, SGU_DIM),
            full(SGU_GROUPS, SGU_CHUNK, SGU_CHUNK), full(SGU_CHUNK, SGU_GROUPS),
            full(SGU_DIM, D), full(1, D), full(1, D),
        ],
        out_specs=[tok, pl.BlockSpec((TM * ROWS_PER_TOKEN, LANES), lambda i: (i, 0))],
        out_shape=[jax.ShapeDtypeStruct((T, D), F32), jax.ShapeDtypeStruct((T * ROWS_PER_TOKEN, LANES), U32)],
        scratch_shapes=[pltpu.VMEM((TM, SGU_DIM), BF16)],
        compiler_params=_params(("arbitrary",)),
        name="sgu_mixer",
    )(xp, xs, mod, g_pre, w_in, b_in, v_g, w_s, b_s_t, w_out, g_post, g_ffn)


LANES = 128
U32 = jnp.uint32
PACKED = D // 2
ROWS_PER_TOKEN = PACKED // LANES
BF16_BITS = 16
HIGH_HALF = 0xFFFF0000


def _pack_rows(x):
    bits = lax.bitcast_convert_type(x.astype(BF16).astype(F32), U32)
    return bits[:, :PACKED] | (bits[:, PACKED:] >> BF16_BITS)


def _unpack_rows(u):
    return (lax.bitcast_convert_type(u & U32(HIGH_HALF), F32), lax.bitcast_convert_type(u << BF16_BITS, F32))


def _store_token_tiles(ref, u, first=0):
    n = u.shape[0]
    for c in range(ROWS_PER_TOKEN):
        ref[pl.ds(first * ROWS_PER_TOKEN + c, n, stride=ROWS_PER_TOKEN), :] = u[:, c * LANES:(c + 1) * LANES]


def _load_token_tiles(ref, n, first=0):
    return jnp.concatenate([ref[pl.ds(first * ROWS_PER_TOKEN + c, n, stride=ROWS_PER_TOKEN), :]
                            for c in range(ROWS_PER_TOKEN)], axis=1)


def _ffn_input_rows(x, m, g):
    return _pack_rows(_rms(x, g) * (1.0 + m[4:5]) + m[3:4])


def _router_kernel(hp_ref, wr_ref, eb_ref, te_ref, wn_ref, rk_ref, cnt_ref, carry_ref, upper_ref):
    i = pl.program_id(0)

    @pl.when(i == 0)
    def _():
        carry_ref[...] = jnp.zeros_like(carry_ref)
        tj = lax.broadcasted_iota(I32, (TR, TR), 0)
        ti = lax.broadcasted_iota(I32, (TR, TR), 1)
        upper_ref[...] = jnp.where(tj < ti, 1.0, 0.0).astype(BF16)

    h_hi, h_lo = (t.astype(BF16) for t in _unpack_rows(_load_token_tiles(hp_ref, TR)))
    w1, w2, _ = _split3(wr_ref[...])
    nt = lambda a, b: lax.dot_general(a, b, (((1,), (1,)), ((), ())), preferred_element_type=F32)
    logits = (nt(w1[:, :PACKED], h_hi) + nt(w1[:, PACKED:], h_lo)
              + nt(w2[:, :PACKED], h_hi) + nt(w2[:, PACKED:], h_lo))
    scores = jax.nn.sigmoid(logits)
    sel = scores + eb_ref[...]

    row8 = lax.broadcasted_iota(I32, (GROUP_SIZE, TR), 0)
    gscore = []
    for g in range(N_GROUPS):
        xg = sel[g * GROUP_SIZE:(g + 1) * GROUP_SIZE]
        m1 = jnp.max(xg, axis=0, keepdims=True)
        i1 = jnp.min(jnp.where(xg == m1, row8, GROUP_SIZE), axis=0, keepdims=True)
        m2 = jnp.max(jnp.where(row8 == i1, NEG_INF, xg), axis=0, keepdims=True)
        gscore.append(m1 + m2)
    pieces = []
    for g in range(N_GROUPS):
        rank = jnp.zeros((1, TR), I32)
        for g2 in range(N_GROUPS):
            if g2 == g:
                continue
            beats = (gscore[g2] >= gscore[g]) if g2 < g else (gscore[g2] > gscore[g])
            rank = rank + beats.astype(I32)
        pieces.append(jnp.where(rank < TOPK_GROUPS, sel[g * GROUP_SIZE:(g + 1) * GROUP_SIZE], NEG_INF))
    cur = jnp.concatenate(pieces, axis=0)

    row = lax.broadcasted_iota(I32, (N_EXPERTS, TR), 0)
    idxs, ws = [], []
    for _ in range(TOP_K):
        mx = jnp.max(cur, axis=0, keepdims=True)
        idx = jnp.min(jnp.where(cur == mx, row, N_EXPERTS), axis=0, keepdims=True)
        hit = row == idx
        ws.append(jnp.sum(jnp.where(hit, scores, 0.0), axis=0, keepdims=True))
        cur = jnp.where(hit, NEG_INF, cur)
        idxs.append(idx)
    mask = jnp.zeros((N_EXPERTS, TR), F32)
    for idx in idxs:
        mask = mask + (row == idx).astype(F32)
    wsum = ws[0]
    for wk in ws[1:]:
        wsum = wsum + wk

    pos = carry_ref[...] + jnp.dot(mask.astype(BF16), upper_ref[...], preferred_element_type=F32)
    for k in range(TOP_K):
        hit = row == idxs[k]
        te_ref[k:k + 1, :] = idxs[k]
        wn_ref[k:k + 1, :] = ws[k] / wsum * ROUTED_SCALE
        rk_ref[k:k + 1, :] = jnp.sum(jnp.where(hit, pos, 0.0), axis=0, keepdims=True).astype(I32)
    carry_ref[...] = carry_ref[...] + jnp.sum(mask, axis=1, keepdims=True)
    cnt_ref[...] = carry_ref[...]


def _router(hp, wr_t, e_bias):
    kt = lambda dtype: jax.ShapeDtypeStruct((TOP_K, T), dtype)
    kt_spec = pl.BlockSpec((TOP_K, TR), lambda i: (0, i))
    return pl.pallas_call(
        _router_kernel,
        grid=(T // TR,),
        in_specs=[
            pl.BlockSpec((TR * ROWS_PER_TOKEN, LANES), lambda i: (i, 0)),
            pl.BlockSpec((N_EXPERTS, D), lambda i: (0, 0)),
            pl.BlockSpec((N_EXPERTS, 1), lambda i: (0, 0)),
        ],
        out_specs=[
            kt_spec, kt_spec, kt_spec,
            pl.BlockSpec((N_EXPERTS, 1), lambda i: (0, 0)),
        ],
        out_shape=[
            kt(I32), kt(F32), kt(I32),
            jax.ShapeDtypeStruct((N_EXPERTS, 1), F32),
        ],
        scratch_shapes=[pltpu.VMEM((N_EXPERTS, 1), F32), pltpu.VMEM((TR, TR), BF16)],
        compiler_params=_params(("arbitrary",)),
        name="moe_router",
    )(hp, wr_t, e_bias)


_PAD_BITS = tuple(1 << b for b in range(GM_SUB.bit_length() - 1))


def _pad_fill_kernel(pad_start_ref, pad_len_ref, xg_in_ref, xg_ref, zero_ref, sem):
    del xg_in_ref
    zero_ref[...] = jnp.zeros_like(zero_ref)

    def pad_copies(e):
        start = pad_start_ref[e]
        n = pad_len_ref[e]
        copies = []
        for bit in _PAD_BITS:
            first = start + (n & ~(2 * bit - 1))
            copies.append(((n & bit) != 0, pltpu.make_async_copy(
                zero_ref.at[pl.ds(0, bit)], xg_ref.at[pl.ds(first, bit)], sem)))
        return copies

    def start_e(e, carry):
        for on, cp in pad_copies(e):
            @pl.when(on)
            def _():
                cp.start()
        return carry

    def wait_e(e, carry):
        for on, cp in pad_copies(e):
            @pl.when(on)
            def _():
                cp.wait()
        return carry

    lax.fori_loop(0, N_EXPERTS, start_e, 0)
    lax.fori_loop(0, N_EXPERTS, wait_e, 0)


def _pad_fill(pad_start, pad_len, xg):
    grid_spec = pltpu.PrefetchScalarGridSpec(
        num_scalar_prefetch=2,
        grid=(1,),
        in_specs=[pl.BlockSpec(memory_space=pl.ANY)],
        out_specs=pl.BlockSpec(memory_space=pl.ANY),
        scratch_shapes=[pltpu.VMEM((GM_SUB // 2, ROWS_PER_TOKEN, LANES), xg.dtype), pltpu.SemaphoreType.DMA],
    )
    return pl.pallas_call(
        _pad_fill_kernel,
        grid_spec=grid_spec,
        out_shape=jax.ShapeDtypeStruct(xg.shape, xg.dtype),
        input_output_aliases={2: 0},
        compiler_params=_params(("arbitrary",)),
        name="moe_pad_fill",
    )(pad_start, pad_len, xg)


SC_CORES, SC_SUBCORES = 2, 16
SC_WORKERS = SC_CORES * SC_SUBCORES
SC_W = 64


def _sc_worker_id():
    return lax.axis_index("s") * SC_CORES + lax.axis_index("c")


def _sc_dispatch(h3, slot3):
    nchunk = T // SC_WORKERS // SC_W
    mesh = plsc.VectorSubcoreMesh(core_axis_name="c", subcore_axis_name="s")
    tile = (SC_W, ROWS_PER_TOKEN, LANES)

    @functools.partial(
        pl.kernel, mesh=mesh,
        out_type=jax.ShapeDtypeStruct((SP, ROWS_PER_TOKEN, LANES), h3.dtype),
        scratch_types=[pltpu.VMEM((TOP_K, SC_W), I32), pltpu.VMEM((TOP_K, SC_W), I32),
                       pltpu.VMEM(tile, h3.dtype), pltpu.VMEM(tile, h3.dtype),
                       pltpu.SemaphoreType.DMA((2,)), pltpu.SemaphoreType.DMA((2,))],
    )
    def k(h_hbm, slot_hbm, xg_hbm, idx0, idx1, rows0, rows1, lsem, ssem):
        first = _sc_worker_id() * nchunk
        idx = (idx0, idx1)
        rows = (rows0, rows1)

        def loads(j, b):
            blk = first + j
            tok = pl.multiple_of(blk * SC_W, SC_W)
            return (pltpu.make_async_copy(slot_hbm.at[blk], idx[b], lsem.at[b]),
                    pltpu.make_async_copy(h_hbm.at[pl.ds(tok, SC_W)], rows[b], lsem.at[b]))

        def scatters(b):
            return [pltpu.make_async_copy(rows[b], xg_hbm.at[idx[b].at[kk]], ssem.at[b]) for kk in range(TOP_K)]

        for cp in loads(0, 0):
            cp.start()

        @pl.loop(0, nchunk, step=2)
        def _(j):
            for b in (0, 1):
                jj = j + b
                for cp in loads(jj, b):
                    cp.wait()
                for cp in scatters(b):
                    cp.start()

                @pl.when(jj + 1 < nchunk)
                def _():
                    @pl.when(jj >= 1)
                    def _():
                        for cp in scatters(1 - b):
                            cp.wait()
                    for cp in loads(jj + 1, 1 - b):
                        cp.start()

        for b in (0, 1):
            for cp in scatters(b):
                cp.wait()

    return k(h3, slot3)


NSUB = GM // GM_SUB
SUB_ROWS = GM_SUB * ROWS_PER_TOKEN


def _per_block_count(count, fn):
    for n in range(1, NSUB + 1):
        @pl.when(count == n)
        def _(n=n):
            fn(n)


def _gmm_kernel(layer, tile_e_ref, tile_blk_ref, tile_nsub_ref, tile_run_ref, tile_next_ref,
                xg_hbm, wg_hbm, wu_hbm, wd_hbm, yg_hbm,
                x_st, y_st, wg_st, wu_st, wd_st, wgu_scr, wd_scr, sems, xsems, ysems):
    j = pl.program_id(0)
    last_step = pl.num_programs(0) - 1
    nsub = tile_nsub_ref[j]
    run = tile_run_ref[j]
    slot = j % 2
    x_ref = x_st.at[slot]
    y_ref = y_st.at[slot]

    def tile_rows(t, n):
        return pl.ds(pl.multiple_of(tile_blk_ref[t] * SUB_ROWS, SUB_ROWS), n * SUB_ROWS)

    def x_copy(t, n):
        return pltpu.make_async_copy(xg_hbm.at[tile_rows(t, n)], x_st.at[t % 2, pl.ds(0, n * SUB_ROWS)], xsems.at[t % 2])

    def y_copy(t, n):
        return pltpu.make_async_copy(y_st.at[t % 2, pl.ds(0, n * SUB_ROWS)], yg_hbm.at[tile_rows(t, n)], ysems.at[t % 2])

    @pl.when(j == 0)
    def _():
        _per_block_count(nsub, lambda n: x_copy(j, n).start())

    _per_block_count(nsub, lambda n: x_copy(j, n).wait())

    @pl.when(j < last_step)
    def _():
        _per_block_count(tile_nsub_ref[j + 1], lambda n: x_copy(j + 1, n).start())

    @pl.when(j >= 2)
    def _():
        _per_block_count(tile_nsub_ref[j - 2], lambda n: y_copy(j - 2, n).wait())

    def weight_copies(e, slot):
        return [pltpu.make_async_copy(src.at[layer, e], dst.at[slot], sems.at[slot])
                for src, dst in ((wg_hbm, wg_st), (wu_hbm, wu_st), (wd_hbm, wd_st))]

    @pl.when(run >= 0)
    def _():
        @pl.when(j == 0)
        def _():
            for cp in weight_copies(tile_e_ref[j], run):
                cp.start()

        for cp in weight_copies(tile_e_ref[j], run):
            cp.wait()

        nxt = tile_next_ref[j]

        @pl.when(nxt >= 0)
        def _():
            for cp in weight_copies(nxt, 1 - run):
                cp.start()

        wgu_scr[:, 0:D_EXPERT] = wg_st[run].astype(BF16)
        wgu_scr[:, D_EXPERT:2 * D_EXPERT] = wu_st[run].astype(BF16)
        wd_scr[...] = wd_st[run].astype(BF16)

    def expert_mlp(s):
        x_hi, x_lo = _unpack_rows(_load_token_tiles(x_ref, GM_SUB, s * GM_SUB))
        gu = (jnp.dot(x_hi.astype(BF16), wgu_scr[0:PACKED, :], preferred_element_type=F32)
              + jnp.dot(x_lo.astype(BF16), wgu_scr[PACKED:D, :], preferred_element_type=F32))
        hid = _silu(gu[:, 0:D_EXPERT]) * gu[:, D_EXPERT:2 * D_EXPERT]
        y = jnp.dot(hid.astype(BF16), wd_scr[...], preferred_element_type=F32)
        _store_token_tiles(y_ref, _pack_rows(y), s * GM_SUB)

    def compute_and_send(n):
        for s in range(n):
            expert_mlp(s)
        y_copy(j, n).start()

    _per_block_count(nsub, compute_and_send)

    @pl.when(j == last_step)
    def _():
        _per_block_count(tile_nsub_ref[jnp.maximum(j - 1, 0)], lambda n: y_copy(j - 1, n).wait())
        _per_block_count(nsub, lambda n: y_copy(j, n).wait())


def _gmm(tile_e, tile_blk, tile_nsub, tile_run, tile_next, xg, l, w_gate, w_up, w_down):
    hbm = pl.BlockSpec(memory_space=pl.ANY)
    stage = pltpu.VMEM((2, GM * ROWS_PER_TOKEN, LANES), U32)
    grid_spec = pltpu.PrefetchScalarGridSpec(
        num_scalar_prefetch=5,
        grid=(NT_MAX,),
        in_specs=[hbm, hbm, hbm, hbm],
        out_specs=hbm,
        scratch_shapes=[stage, stage,
                        pltpu.VMEM((2, D, D_EXPERT), F32), pltpu.VMEM((2, D, D_EXPERT), F32),
                        pltpu.VMEM((2, D_EXPERT, D), F32),
                        pltpu.VMEM((D, 2 * D_EXPERT), BF16), pltpu.VMEM((D_EXPERT, D), BF16),
                        pltpu.SemaphoreType.DMA((2,)), pltpu.SemaphoreType.DMA((2,)), pltpu.SemaphoreType.DMA((2,))],
    )
    return pl.pallas_call(
        functools.partial(_gmm_kernel, l),
        grid_spec=grid_spec,
        out_shape=jax.ShapeDtypeStruct((SP * ROWS_PER_TOKEN, LANES), U32),
        compiler_params=_params(("arbitrary",)),
        name="moe_grouped_matmul",
    )(tile_e, tile_blk, tile_nsub, tile_run, tile_next, xg, w_gate, w_up, w_down)


def _sc_gather(table3, idx):
    n_idx = idx.shape[0]
    per_w = n_idx // SC_WORKERS
    nchunk = per_w // SC_W
    mesh = plsc.VectorSubcoreMesh(core_axis_name="c", subcore_axis_name="s")
    tile = (SC_W, ROWS_PER_TOKEN, LANES)

    @functools.partial(
        pl.kernel, mesh=mesh,
        out_type=jax.ShapeDtypeStruct((n_idx, ROWS_PER_TOKEN, LANES), table3.dtype),
        scratch_types=[pltpu.VMEM((per_w,), I32), pltpu.VMEM(tile, table3.dtype), pltpu.VMEM(tile, table3.dtype),
                       pltpu.SemaphoreType.DMA((2,)), pltpu.SemaphoreType.DMA((2,))],
    )
    def k(table_hbm, idx_hbm, out_hbm, idx_v, rows0, rows1, gsem, wsem):
        base = pl.multiple_of(_sc_worker_id() * per_w, per_w)
        rows = (rows0, rows1)
        pltpu.sync_copy(idx_hbm.at[pl.ds(base, per_w)], idx_v)

        def gather(j, b):
            ids = idx_v.at[pl.ds(pl.multiple_of(j * SC_W, SC_W), SC_W)]
            return pltpu.make_async_copy(table_hbm.at[ids], rows[b], gsem.at[b])

        def write(j, b):
            dst = out_hbm.at[pl.ds(pl.multiple_of(base + j * SC_W, SC_W), SC_W)]
            return pltpu.make_async_copy(rows[b], dst, wsem.at[b])

        gather(0, 0).start()

        @pl.loop(0, nchunk, step=2)
        def _(j):
            for b in (0, 1):
                jj = j + b
                gather(jj, b).wait()
                write(jj, b).start()

                @pl.when(jj + 1 < nchunk)
                def _():
                    @pl.when(jj >= 1)
                    def _():
                        write(jj - 1, 1 - b).wait()
                    gather(jj + 1, 1 - b).start()

        write(nchunk - 2, 0).wait()
        write(nchunk - 1, 1).wait()

    return k(table3, idx)


def _combine_kernel(wn_ref, x_ref, mod_ref, gpre_ref, gp_ref, wsg_ref, wsu_ref, wsd_ref, y_ref, o_ref, eye_ref):
    @pl.when(pl.program_id(0) == 0)
    def _():
        r = lax.broadcasted_iota(I32, (TD, TD), 0)
        c = lax.broadcasted_iota(I32, (TD, TD), 1)
        eye_ref[...] = jnp.where(r == c, 1.0, 0.0).astype(BF16)

    m = mod_ref[0, 0]
    x = x_ref[...]
    hb = (_rms(x, gpre_ref[...]) * (1.0 + m[4:5]) + m[3:4]).astype(BF16)
    hid = (_silu(jnp.dot(hb, wsg_ref[...], preferred_element_type=F32))
           * jnp.dot(hb, wsu_ref[...], preferred_element_type=F32))
    acc = jnp.dot(hid.astype(BF16), wsd_ref[...], preferred_element_type=F32)

    eye = eye_ref[...]
    nt = lambda a, b: lax.dot_general(a, b, (((1,), (1,)), ((), ())), preferred_element_type=F32)
    w1, w2, w3 = _split3(wn_ref[...])
    w_t = nt(eye, w1) + nt(eye, w2) + nt(eye, w3)

    acc_hi = acc[:, :PACKED]
    acc_lo = acc[:, PACKED:]
    for k in range(TOP_K):
        y_hi, y_lo = _unpack_rows(_load_token_tiles(y_ref, TD, k * TD))
        acc_hi = acc_hi + y_hi * w_t[:, k:k + 1]
        acc_lo = acc_lo + y_lo * w_t[:, k:k + 1]
    acc = jnp.concatenate([acc_hi, acc_lo], axis=1)
    o_ref[...] = x + m[5:6] * _rms(acc, gp_ref[...])


def _combine(wn, x, mod, l, g_pre, g_post, ws_gate, ws_up, ws_down, ybuf, first_tok, n_tok):
    off = first_tok // TD
    tiles_per_dec = DEC_SEQ // TD
    npd = TP // TD
    mod_row = lambda i: jnp.where(i + off < npd, 0, 1 + (i + off - npd) // tiles_per_dec)
    full = lambda *shape: pl.BlockSpec(shape, lambda i: (0,) * len(shape))
    y_spec = pl.BlockSpec((TOP_K * TD * ROWS_PER_TOKEN, LANES), lambda i: (i, 0))
    return pl.pallas_call(
        _combine_kernel,
        grid=(n_tok // TD,),
        in_specs=[
            pl.BlockSpec((TOP_K, TD), lambda i: (0, i + off)),
            pl.BlockSpec((TD, D), lambda i: (i + off, 0)),
            pl.BlockSpec((1, 1, 6, D), lambda i: (l, mod_row(i), 0, 0)),
            full(1, D), full(1, D), full(D, D_SHARED), full(D, D_SHARED), full(D_SHARED, D),
            y_spec,
        ],
        out_specs=pl.BlockSpec((TD, D), lambda i: (i, 0)),
        out_shape=jax.ShapeDtypeStruct((n_tok, D), F32),
        scratch_shapes=[pltpu.VMEM((TD, TD), BF16)],
        compiler_params=_params(("arbitrary",)),
        name="moe_combine",
    )(wn, x, mod, g_pre, g_post, ws_gate, ws_up, ws_down, ybuf)


def _moe_layer(x, h, mod, l, g_pre, g_post, w_router, e_bias, w_gate, w_up, w_down,
               ws_gate, ws_up, ws_down):
    top_e, wn, rk, cnt = _router(h, w_router.T, e_bias.reshape(N_EXPERTS, 1))
    cnt = cnt.reshape(N_EXPERTS).astype(I32)
    padded = (cnt + GM_SUB - 1) // GM_SUB * GM_SUB
    ends = jnp.cumsum(padded)
    offs = ends - padded
    eid = jnp.arange(N_EXPERTS, dtype=I32)[:, None, None]
    slot = rk + jnp.sum(jnp.where(top_e[None] == eid, offs[:, None, None], 0), axis=0)
    ntile = (cnt + GM - 1) // GM
    tile_ends = jnp.cumsum(ntile)
    tid = jnp.arange(NT_MAX, dtype=I32)
    last = jnp.maximum(tile_ends[-1] - 1, 0)
    tile_e = jnp.minimum(jnp.sum((tid[:, None] >= tile_ends[None, :]).astype(I32), axis=1), N_EXPERTS - 1)
    tile_e = jnp.where(tid <= last, tile_e, tile_e[last])
    own = tile_e[:, None] == jnp.arange(N_EXPERTS, dtype=I32)[None, :]
    pick = lambda per_expert: jnp.sum(jnp.where(own, per_expert[None, :], 0), axis=1)
    t_in = tid - pick(tile_ends - ntile)
    tile_blk = jnp.where(tid <= last, (pick(offs) + GM * t_in) // GM_SUB, 0)
    tile_rows = jnp.clip(pick(cnt) - GM * t_in, 0, GM)
    tile_nsub = jnp.where(tid <= last, (tile_rows + GM_SUB - 1) // GM_SUB, 0)
    starts = jnp.logical_and(jnp.arange(NT_MAX) <= last,
                             jnp.concatenate([jnp.ones((1,), bool), tile_e[1:] != tile_e[:-1]]))
    tile_run = jnp.where(starts, (jnp.cumsum(starts.astype(I32)) - 1) % 2, -1)
    later = jnp.logical_and(jnp.arange(N_EXPERTS, dtype=I32)[None, :] > tile_e[:, None], (cnt > 0)[None, :])
    tile_next = jnp.min(jnp.where(later, jnp.arange(N_EXPERTS, dtype=I32)[None, :], N_EXPERTS), axis=1)
    tile_next = jnp.where(tile_next < N_EXPERTS, tile_next, -1)
    slot3 = slot.reshape(TOP_K, T // SC_W, SC_W).transpose(1, 0, 2)
    xg = _sc_dispatch(h.reshape(T, ROWS_PER_TOKEN, LANES), slot3)
    xg = _pad_fill(offs + cnt, padded - cnt, xg).reshape(SP * ROWS_PER_TOKEN, LANES)
    yg = _gmm(tile_e, tile_blk, tile_nsub.astype(I32), tile_run.astype(I32), tile_next.astype(I32),
              xg, l, w_gate, w_up, w_down)
    yg3 = yg.reshape(SP, ROWS_PER_TOKEN, LANES)
    ws = (ws_gate.astype(BF16), ws_up.astype(BF16), ws_down.astype(BF16))
    outs = []
    for first_tok, n_tok in ((0, TP), (TP, TS)):
        ids = slot[:, first_tok:first_tok + n_tok].reshape(TOP_K, n_tok // TD, TD).transpose(1, 0, 2)
        ybuf = _sc_gather(yg3, ids.reshape(TOP_K * n_tok))
        outs.append(_combine(wn, x, mod, l, g_pre, g_post, *ws,
                             ybuf.reshape(TOP_K * n_tok * ROWS_PER_TOKEN, LANES), first_tok, n_tok))
    return outs


def _rope_tables():
    n = DEC_SEQ
    rows = n // GRID_W
    row = jnp.repeat(jnp.arange(rows), GRID_W).astype(F32)
    col = jnp.tile(jnp.arange(GRID_W), rows).astype(F32)
    half = DQK_B // 2
    inv = ROPE_BASE ** (-jnp.arange(0, half, 2, dtype=F32) / half)
    ang_r = row[:, None] * inv
    ang_c = col[:, None] * inv
    ang = jnp.concatenate([ang_r, ang_r, ang_c, ang_c], axis=-1)
    reps = QK_B // DQK_B
    return jnp.tile(jnp.cos(ang), (1, reps)), jnp.tile(jnp.sin(ang), (1, reps))


def _pad_in_proj(w):
    s = [0, Q_A, 2 * Q_A, 2 * Q_A + V_A, 2 * Q_A + 2 * V_A]
    s += [s[-1] + GATE_RANK, s[-1] + 2 * GATE_RANK]
    s += [s[-1] + QK_B, s[-1] + 2 * QK_B, s[-1] + 2 * QK_B + V_B]
    gates = jnp.pad(w[:, s[4]:s[6]], ((0, 0), (0, GL_PAD - 2 * GATE_RANK)))
    return jnp.concatenate([w[:, s[0]:s[4]], gates, w[:, s[6]:s[9]]], axis=1).astype(BF16)


def kernel(x_prompt, x_sample, c, c_ctx, state_gla, cache_k, cache_v, ada_w, ada_b, norm_pre_mix, norm_post_mix, norm_pre_ffn, norm_post_ffn, ab_w_in, gla_w_g2, gla_b_g2, gla_norm_g, diff_lambda, diff_norm_g, ab_w_out, sgu_w_in, sgu_b_in, sgu_norm_g, sgu_w_s, sgu_b_s, sgu_w_out, moe_w_router, moe_e_bias, moe_w_gate, moe_w_up, moe_w_down, moe_ws_gate, moe_ws_up, moe_ws_down):
    depth = ada_w.shape[0]
    assert x_prompt.shape == (BATCH, SEQ, D) and x_sample.shape == (DEC_BATCH, DEC_SEQ, D)
    assert state_gla.shape == (DEC_BATCH, (depth + 1) // 2, 2, H_A, DK_A, DV_A)
    assert cache_k.shape == (DEC_BATCH, 1, H_B, 2, PAST_LEN, DQK_B) and cache_v.shape == (DEC_BATCH, 1, H_B, PAST_LEN, DV_B)
    assert depth == 2 and moe_w_gate.shape == (depth, N_EXPERTS, D, D_EXPERT)
    xp, xs = x_prompt.reshape(TP, D), x_sample.reshape(TS, D)
    cond = jnp.concatenate([c_ctx[None, :], c, jnp.zeros((8 - 1 - DEC_BATCH, D), F32)], axis=0)
    mod = _modulation(cond, ada_w, ada_b)
    cos, sin = _rope_tables()
    vec = lambda a: a.reshape(1, -1)
    new_s = new_k = new_v = None
    for l in range(depth):
        if l % 2 == 0:
            e = l // 2
            lam_init = 0.8 - 0.6 * math.exp(-0.3 * l)
            a, r_a, gl, q_b, k_b, v_b, new_k, new_v = _in_proj(xp, xs, mod, l, vec(norm_pre_mix[l]),
                                                               _pad_in_proj(ab_w_in[e]), cos, sin)
            s0_t = jnp.swapaxes(state_gla[:, e], -1, -2)
            same_head = jnp.eye(H_A, dtype=bool)[None, None, :, None, :, None]
            s0_t = jnp.where(same_head, s0_t[:, :, :, :, None, :], 0.0).reshape(DEC_BATCH, 2, V_A, Q_A)
            o_f, o_bw, s_fin_t = _gla(a, gl, gla_w_g2[e], gla_b_g2[e].reshape(2, 1, Q_A), s0_t)
            o_att = _diff_attention(q_b, k_b, v_b, cache_k, cache_v, diff_lambda[e], lam_init)
            x, h = _mix_out(lam_init, o_f, o_bw, r_a, o_att, xp, xs, mod, l, vec(gla_norm_g[e]), vec(diff_norm_g[e]),
                            ab_w_out[e].astype(BF16), vec(norm_post_mix[l]), vec(norm_pre_ffn[l]))
            new_s = jnp.swapaxes(s_fin_t, -1, -2)[:, None]
        else:
            o = l // 2
            x, h = _sgu(xp, xs, mod, l, vec(norm_pre_mix[l]), sgu_w_in[o].astype(BF16), vec(sgu_b_in[o]),
                        vec(sgu_norm_g[o]), sgu_w_s[o], sgu_b_s[o].T, sgu_w_out[o].astype(BF16),
                        vec(norm_post_mix[l]), vec(norm_pre_ffn[l]))
        xp, xs = _moe_layer(x, h, mod, l, vec(norm_pre_ffn[l]), vec(norm_post_ffn[l]), moe_w_router[l], moe_e_bias[l],
                            moe_w_gate, moe_w_up, moe_w_down, moe_ws_gate[l], moe_ws_up[l], moe_ws_down[l])
    y_prompt = xp.reshape(BATCH, SEQ, D)
    y_sample = xs.reshape(DEC_BATCH, DEC_SEQ, D)
    return (y_prompt, y_sample, new_s, new_k, new_v)
```

```python
import functools
import math

import jax
import jax.numpy as jnp
from jax import lax
from jax.experimental import pallas as pl
from jax.experimental.pallas import tpu as pltpu
from jax.experimental.pallas import tpu_sc as plsc

F32 = jnp.float32
BF16 = jnp.bfloat16
I32 = jnp.int32

D = 1024
BATCH, SEQ = 32, 256
DEC_BATCH, DEC_SEQ = 4, 2048
PAST_LEN = 256
GRID_W = 64
EPS = 1e-6
TP = BATCH * SEQ
TS = DEC_BATCH * DEC_SEQ
T = TP + TS
H_A, DK_A, DV_A = 4, 64, 128
Q_A, V_A = H_A * DK_A, H_A * DV_A
GATE_RANK, GATE_TAU, GLA_CHUNK = 16, 16.0, 64
H_B, DQK_B, DV_B = 4, 64, 128
QK_B, V_B = H_B * 2 * DQK_B, H_B * DV_B
ROPE_BASE = 10000.0
SGU_DIM, SGU_GROUPS, SGU_CHUNK = 1024, 4, 128
N_EXPERTS, TOP_K, N_GROUPS, TOPK_GROUPS = 64, 8, 8, 4
GROUP_SIZE = N_EXPERTS // N_GROUPS
D_EXPERT, D_SHARED = 256, 256
ROUTED_SCALE = 2.5

TM = 512
TM_SUB = 256
MIX_DTYPE = jnp.bfloat16
NPT = TP // TM
TILES_PER_DEC = DEC_SEQ // TM
SEG = 256
NSEG = T // SEG
NSEG_P = TP // SEG
SEG_PER_DEC = DEC_SEQ // SEG
TR = 512
TD = 512
GM = 2048
GM_SUB = 256
NT_MAX = T * TOP_K // GM + N_EXPERTS
SP = T * TOP_K + N_EXPERTS * GM_SUB
GL_PAD = 128
VMEM_LIMIT = 56 * 1024 * 1024
NEG_INF = float("-inf")


def _bdot(a, b):
    return jnp.dot(a.astype(BF16), b.astype(BF16), preferred_element_type=F32)


def _bdot_nt(a, b):
    return lax.dot_general(a.astype(BF16), b.astype(BF16), (((1,), (1,)), ((), ())),
                           preferred_element_type=F32)


def _bdot_tn(a, b):
    return lax.dot_general(a.astype(BF16), b.astype(BF16), (((0,), (0,)), ((), ())),
                           preferred_element_type=F32)


def _split3(x):
    x1 = x.astype(BF16)
    r1 = x - x1.astype(F32)
    x2 = r1.astype(BF16)
    x3 = (r1 - x2.astype(F32)).astype(BF16)
    return x1, x2, x3


def _rms(x, g):
    return x * lax.rsqrt(jnp.mean(x * x, axis=-1, keepdims=True) + EPS) * g


def _silu(x):
    return x * jax.nn.sigmoid(x)


def _mod_row(i):
    return jnp.where(i < NPT, 0, 1 + (i - NPT) // TILES_PER_DEC)


def _params(sem, limit=VMEM_LIMIT):
    return pltpu.CompilerParams(dimension_semantics=sem, vmem_limit_bytes=limit)


def _mod_kernel(c_ref, w_ref, b_ref, o_ref):
    o_ref[0] = _bdot(_silu(c_ref[...]), w_ref[0]) + b_ref[0]


def _modulation(cond, ada_w, ada_b):
    depth = ada_w.shape[0]
    nj = 6
    out = pl.pallas_call(
        _mod_kernel,
        grid=(depth, nj),
        in_specs=[
            pl.BlockSpec((8, D), lambda l, j: (0, 0)),
            pl.BlockSpec((1, D, D), lambda l, j: (l, 0, j)),
            pl.BlockSpec((1, 1, D), lambda l, j: (l, 0, j)),
        ],
        out_specs=pl.BlockSpec((1, 8, D), lambda l, j: (l, 0, j)),
        out_shape=jax.ShapeDtypeStruct((depth, 8, 6 * D), F32),
        compiler_params=_params(("arbitrary", "arbitrary")),
        name="adaln_modulation",
    )(cond, ada_w, ada_b.reshape(depth, 1, 6 * D))
    return out.reshape(depth, 8, 6, D)


_C_A = 0
_C_R = _C_A + 2 * Q_A + V_A
_C_GL = _C_R + V_A
_C_Q = _C_GL + GL_PAD
_C_K = _C_Q + QK_B
_C_V = _C_K + QK_B
_C_END = _C_V + V_B
ROT_PAIR = DQK_B // 4


def _rope(x, cos, sin):
    lane = lax.broadcasted_iota(I32, x.shape, 1)
    first = (lane % (2 * ROT_PAIR)) < ROT_PAIR
    n = x.shape[1]
    xr = jnp.where(first, -pltpu.roll(x, n - ROT_PAIR, 1), pltpu.roll(x, ROT_PAIR, 1))
    return x * cos + xr * sin


def _stream_specs():
    return [pl.BlockSpec((TM, D), lambda i: (jnp.minimum(i, NPT - 1), 0)),
            pl.BlockSpec((TM, D), lambda i: (jnp.maximum(i - NPT, 0), 0))]


def _stream_tile(xp_ref, xs_ref):
    return jnp.where(pl.program_id(0) < NPT, xp_ref[...], xs_ref[...])


def _in_kernel(xp_ref, xs_ref, mod_ref, g_ref, w_ref, cos_ref, sin_ref,
               a_ref, r_ref, gl_ref, q_ref, k_ref, v_ref, ck_ref, cv_ref, k32_ref, v32_ref):
    latent = pl.program_id(0) >= NPT
    m = mod_ref[0, 0]
    x = _stream_tile(xp_ref, xs_ref)
    for s in range(TM // TM_SUB):
        rows = slice(s * TM_SUB, (s + 1) * TM_SUB)
        h = _rms(x[rows], g_ref[...]) * (1.0 + m[1:2]) + m[0:1]
        hb = h.astype(BF16)

        def proj(c0, c1):
            return jnp.dot(hb, w_ref[:, c0:c1], preferred_element_type=F32)

        a_ref[rows, :] = proj(_C_A, _C_R)
        r_ref[rows, :] = proj(_C_R, _C_GL).astype(MIX_DTYPE)
        gl_ref[rows, :] = proj(_C_GL, _C_Q)
        v = proj(_C_V, _C_END)
        q = proj(_C_Q, _C_K)
        k = proj(_C_K, _C_V)
        cos = cos_ref[rows, :]
        sin = sin_ref[rows, :]
        q_ref[rows, :] = jnp.where(latent, _rope(q, cos, sin), q).astype(BF16)
        k_ref[rows, :] = jnp.where(latent, _rope(k, cos, sin), k).astype(BF16)
        v_ref[rows, :] = v.astype(BF16)
        k32_ref[rows, :] = k
        v32_ref[rows, :] = v

    @pl.when(jnp.logical_not(latent))
    def _():
        for s in range(TM // SEQ):
            rows = slice(s * SEQ, (s + 1) * SEQ)
            for h in range(H_B):
                cv_ref[s, 0, h] = v32_ref[rows, h * DV_B:(h + 1) * DV_B]
                for mp in range(2):
                    ck_ref[s, 0, h, mp] = k32_ref[rows, (2 * h + mp) * DQK_B:(2 * h + mp + 1) * DQK_B]


def _in_proj(xp, xs, mod, l, g, w_pad, cos, sin):
    tok = lambda width: pl.BlockSpec((TM, width), lambda i: (i, 0))
    rope_spec = pl.BlockSpec((TM, QK_B), lambda i: (jnp.maximum(i - NPT, 0) % TILES_PER_DEC, 0))
    widths = (2 * Q_A + V_A, V_A, GL_PAD, QK_B, QK_B, V_B)
    dtypes = (F32, MIX_DTYPE, F32, BF16, BF16, BF16)
    seqs = TM // SEQ
    prompt_tile = lambda i: jnp.minimum(i, NPT - 1)
    cache_specs = [pl.BlockSpec((seqs, 1, H_B, 2, SEQ, DQK_B), lambda i: (prompt_tile(i), 0, 0, 0, 0, 0)),
                   pl.BlockSpec((seqs, 1, H_B, SEQ, DV_B), lambda i: (prompt_tile(i), 0, 0, 0, 0))]
    cache_shapes = [jax.ShapeDtypeStruct((BATCH, 1, H_B, 2, SEQ, DQK_B), F32),
                    jax.ShapeDtypeStruct((BATCH, 1, H_B, SEQ, DV_B), F32)]
    return pl.pallas_call(
        _in_kernel,
        grid=(T // TM,),
        in_specs=_stream_specs() + [
            pl.BlockSpec((1, 1, 6, D), lambda i: (l, _mod_row(i), 0, 0)),
            pl.BlockSpec((1, D), lambda i: (0, 0)),
            pl.BlockSpec((D, _C_END), lambda i: (0, 0)),
            rope_spec, rope_spec,
        ],
        out_specs=[tok(w) for w in widths] + cache_specs,
        out_shape=[jax.ShapeDtypeStruct((T, w), dt) for w, dt in zip(widths, dtypes)] + cache_shapes,
        scratch_shapes=[pltpu.VMEM((TM, QK_B), F32), pltpu.VMEM((TM, V_B), F32)],
        compiler_params=_params(("arbitrary",)),
        name="mixer_ab_in_proj",
    )(xp, xs, mod, g, w_pad, cos, sin)


def _log_sigmoid(x):
    return jnp.minimum(x, 0.0) - jnp.log(1.0 + jnp.exp(-jnp.abs(x)))


def _gla_kernel(af_ref, ab_ref, glf_ref, glb_ref, wg_ref, bg_ref, s0_ref,
                of_ref, ob_ref, sfin_ref, st_ref):
    i = pl.program_id(0)

    @pl.when(i < NSEG_P)
    def _():
        st_ref[...] = jnp.zeros_like(st_ref)

    @pl.when(jnp.logical_and(i >= NSEG_P, (i - NSEG_P) % SEG_PER_DEC == 0))
    def _():
        st_ref[...] = s0_ref[0]

    r = lax.broadcasted_iota(I32, (SEG, SEG), 0)
    c = lax.broadcasted_iota(I32, (SEG, SEG), 1)
    same = (r // GLA_CHUNK) == (c // GLA_CHUNK)
    nchunk = SEG // GLA_CHUNK
    own_head = (lax.broadcasted_iota(I32, (V_A, Q_A), 0) // DV_A) == (lax.broadcasted_iota(I32, (V_A, Q_A), 1) // DK_A)

    dirs = []
    for d, (a_ref, gl_ref, o_ref) in enumerate(((af_ref, glf_ref, of_ref), (ab_ref, glb_ref, ob_ref))):
        fwd = d == 0
        gcol = gl_ref[:, d * GATE_RANK:(d + 1) * GATE_RANK]
        la = _log_sigmoid(_bdot(gcol, wg_ref[d]) + bg_ref[d]) / GATE_TAU
        causal = jnp.logical_and(same, (c <= r) if fwd else (c >= r))
        tri = jnp.where(causal, 1.0, 0.0).astype(BF16)
        l1, l2, l3 = _split3(la)
        b_all = (jnp.dot(tri, l1, preferred_element_type=F32)
                 + jnp.dot(tri, l2, preferred_element_type=F32)
                 + jnp.dot(tri, l3, preferred_element_type=F32))
        q_in_all = a_ref[:, 0:Q_A] * (DK_A ** -0.5) * jnp.exp(b_all)
        kd_all = a_ref[:, Q_A:2 * Q_A] * jnp.exp(-b_all)
        intra = []
        for h in range(H_A):
            kc = slice(h * DK_A, (h + 1) * DK_A)
            attn = jnp.where(causal, _bdot_nt(q_in_all[:, kc], kd_all[:, kc]), 0.0)
            intra.append(_bdot(attn, a_ref[:, 2 * Q_A + h * DV_A:2 * Q_A + (h + 1) * DV_A]))
        intra = jnp.concatenate(intra, axis=1)
        dirs.append(dict(fwd=fwd, a=a_ref, o=o_ref, b=b_all, q=q_in_all, intra=intra, state=st_ref[d]))

    for step in range(nchunk):
        for dd in dirs:
            ch = step if dd["fwd"] else nchunk - 1 - step
            r0 = ch * GLA_CHUNK
            rows = slice(r0, r0 + GLA_CHUNK)
            end = r0 + GLA_CHUNK - 1 if dd["fwd"] else r0
            b_end = dd["b"][end:end + 1, :]
            kw = dd["a"][rows, Q_A:2 * Q_A] * jnp.exp(b_end - dd["b"][rows, :])
            dd["o"][rows, :] = (dd["intra"][rows, :] + _bdot_nt(dd["q"][rows, :], dd["state"])).astype(MIX_DTYPE)
            kv_t = _bdot_tn(dd["a"][rows, 2 * Q_A:2 * Q_A + V_A], kw)
            dd["state"] = dd["state"] * jnp.exp(b_end) + jnp.where(own_head, kv_t, 0.0)
    for d, dd in enumerate(dirs):
        st_ref[d] = dd["state"]

    @pl.when(i < NSEG_P)
    def _():
        for d in range(2):
            for h in range(H_A):
                sfin_ref[0, d, h] = st_ref[d, h * DV_A:(h + 1) * DV_A, h * DK_A:(h + 1) * DK_A]


def _seg_bwd(i):
    j = i - NSEG_P
    return jnp.where(i < NSEG_P, i, NSEG_P + (j // SEG_PER_DEC) * SEG_PER_DEC + (SEG_PER_DEC - 1 - j % SEG_PER_DEC))


def _gla(a, gl, wg, bg, s0_t):
    seg = lambda width, f: pl.BlockSpec((SEG, width), lambda i: (f(i), 0))
    ident = lambda i: i
    st_block = (1, 2, H_A, DV_A, DK_A)
    return pl.pallas_call(
        _gla_kernel,
        grid=(NSEG,),
        in_specs=[
            seg(D, ident), seg(D, _seg_bwd), seg(GL_PAD, ident), seg(GL_PAD, _seg_bwd),
            pl.BlockSpec((2, GATE_RANK, Q_A), lambda i: (0, 0, 0)),
            pl.BlockSpec((2, 1, Q_A), lambda i: (0, 0, 0)),
            pl.BlockSpec((1, 2, V_A, Q_A), lambda i: (jnp.maximum(i - NSEG_P, 0) // SEG_PER_DEC, 0, 0, 0)),
        ],
        out_specs=[
            seg(V_A, ident), seg(V_A, _seg_bwd),
            pl.BlockSpec(st_block, lambda i: (jnp.minimum(i, NSEG_P - 1), 0, 0, 0, 0)),
        ],
        out_shape=[
            jax.ShapeDtypeStruct((T, V_A), MIX_DTYPE),
            jax.ShapeDtypeStruct((T, V_A), MIX_DTYPE),
            jax.ShapeDtypeStruct((BATCH, 2, H_A, DV_A, DK_A), F32),
        ],
        scratch_shapes=[pltpu.VMEM((2, V_A, Q_A), F32)],
        compiler_params=_params(("arbitrary",)),
        name="gla_bidir",
    )(a, a, gl, gl, wg, bg, s0_t)


def _diff_lambda(lam_ref, lam_init):
    lp = lam_ref[...]
    s01 = jnp.sum(lp[0:1] * lp[1:2], axis=1, keepdims=True)
    s23 = jnp.sum(lp[2:3] * lp[3:4], axis=1, keepdims=True)
    return jnp.exp(s01) - jnp.exp(s23) + lam_init


def _attn_prompt_kernel(lam_init, q_ref, k_ref, v_ref, lam_ref, o_ref):
    lam = _diff_lambda(lam_ref, lam_init)

    def scores(h):
        out = []
        for m in range(2):
            cols = slice((2 * h + m) * DQK_B, (2 * h + m + 1) * DQK_B)
            out.append(_bdot_nt(q_ref[:, cols] * (DQK_B ** -0.5), k_ref[:, cols]))
        return out

    def finish(h, head_scores):
        ps = []
        for s in head_scores:
            e = jnp.exp(s - jnp.max(s, axis=1, keepdims=True))
            ps.append(e * (1.0 / jnp.sum(e, axis=1, keepdims=True)))
        w = ps[0] - lam * ps[1]
        o_ref[:, h * DV_B:(h + 1) * DV_B] = _bdot(w, v_ref[:, h * DV_B:(h + 1) * DV_B]).astype(MIX_DTYPE)

    pending = scores(0)
    for h in range(H_B):
        ahead = scores(h + 1) if h + 1 < H_B else None
        finish(h, pending)
        pending = ahead


def _attn_sample_kernel(lam_init, q_ref, k_ref, v_ref, ck_ref, cv_ref, lam_ref, o_ref):
    lam = _diff_lambda(lam_ref, lam_init)

    def scores(h):
        out = []
        for m in range(2):
            cols = slice((2 * h + m) * DQK_B, (2 * h + m + 1) * DQK_B)
            q = q_ref[:, cols] * (DQK_B ** -0.5)
            out.append((_bdot_nt(q, ck_ref[0, 0, h, m]), _bdot_nt(q, k_ref[:, cols])))
        return out

    def finish(h, head_scores):
        parts = []
        for m, (sc, sn) in enumerate(head_scores):
            mx = jnp.maximum(jnp.max(sc, axis=1, keepdims=True), jnp.max(sn, axis=1, keepdims=True))
            ec = jnp.exp(sc - mx)
            en = jnp.exp(sn - mx)
            inv = (1.0 if m == 0 else -lam) / (jnp.sum(ec, axis=1, keepdims=True) + jnp.sum(en, axis=1, keepdims=True))
            parts.append((ec * inv, en * inv))
        wc = parts[0][0] + parts[1][0]
        wn = parts[0][1] + parts[1][1]
        o_ref[:, h * DV_B:(h + 1) * DV_B] = (_bdot(wc, cv_ref[0, 0, h])
                                             + _bdot(wn, v_ref[:, h * DV_B:(h + 1) * DV_B])).astype(MIX_DTYPE)

    pending = scores(0)
    for h in range(H_B):
        ahead = scores(h + 1) if h + 1 < H_B else None
        finish(h, pending)
        pending = ahead


QB = SEQ
NQB_DEC = DEC_SEQ // QB


def _attn_kernel(lam_init, q_ref, kp_ref, vp_ref, ks_ref, vs_ref, ck_ref, cv_ref, lam_ref, o_ref):
    i = pl.program_id(0)

    @pl.when(i < BATCH)
    def _():
        _attn_prompt_kernel(lam_init, q_ref, kp_ref, vp_ref, lam_ref, o_ref)

    @pl.when(i >= BATCH)
    def _():
        _attn_sample_kernel(lam_init, q_ref, ks_ref, vs_ref, ck_ref, cv_ref, lam_ref, o_ref)


def _diff_attention(q, k, v, cache_k, cache_v, lam_p, lam_init):
    blk = lambda rows, f: pl.BlockSpec((rows, QK_B), f)
    dec_b = lambda i: jnp.maximum(i - BATCH, 0) // NQB_DEC
    own = lambda i: (i, 0)
    prompt_kv = lambda i: (jnp.minimum(i, BATCH - 1), 0)
    dec_kv = lambda i: (TP // DEC_SEQ + dec_b(i), 0)
    return pl.pallas_call(
        functools.partial(_attn_kernel, lam_init),
        grid=(BATCH + DEC_BATCH * NQB_DEC,),
        in_specs=[
            blk(QB, own), blk(SEQ, prompt_kv), blk(SEQ, prompt_kv), blk(DEC_SEQ, dec_kv), blk(DEC_SEQ, dec_kv),
            pl.BlockSpec((1, 1, H_B, 2, PAST_LEN, DQK_B), lambda i: (dec_b(i), 0, 0, 0, 0, 0)),
            pl.BlockSpec((1, 1, H_B, PAST_LEN, DV_B), lambda i: (dec_b(i), 0, 0, 0, 0)),
            pl.BlockSpec((4, DQK_B), lambda i: (0, 0)),
        ],
        out_specs=blk(QB, own),
        out_shape=jax.ShapeDtypeStruct((T, V_B), MIX_DTYPE),
        compiler_params=_params(("arbitrary",)),
        name="diff_attention",
    )(q, k, v, k, v, cache_k, cache_v, lam_p)


def _head_rms(x, g, nheads, width):
    return jnp.concatenate([_rms(x[:, h * width:(h + 1) * width], g) for h in range(nheads)], axis=1)


def _mix_out_kernel(lam_init, of_ref, ob_ref, r_ref, oatt_ref, xp_ref, xs_ref, mod_ref,
                    gg_ref, dg_ref, wo_ref, gp_ref, gffn_ref, o_ref, hp_ref):
    m = mod_ref[0, 0]
    load = lambda ref: ref[...].astype(F32)
    o_a = _head_rms(load(of_ref) + load(ob_ref), gg_ref[...], H_A, DV_A) * _silu(load(r_ref))
    o_b = _head_rms(load(oatt_ref), dg_ref[...], H_B, DV_B) * (1.0 - lam_init)
    out = _bdot(o_a, wo_ref[0:V_A, :]) + _bdot(o_b, wo_ref[V_A:V_A + V_B, :])
    x1 = _stream_tile(xp_ref, xs_ref) + m[2:3] * _rms(out, gp_ref[...])
    o_ref[...] = x1
    _store_token_tiles(hp_ref, _ffn_input_rows(x1, m, gffn_ref[...]))


def _mix_out(lam_init, o_f, o_b, r_a, o_att, xp, xs, mod, l, gla_g, diff_g, w_o, g_post, g_ffn):
    tok = lambda width: pl.BlockSpec((TM, width), lambda i: (i, 0))
    vec = lambda width: pl.BlockSpec((1, width), lambda i: (0, 0))
    return pl.pallas_call(
        functools.partial(_mix_out_kernel, lam_init),
        grid=(T // TM,),
        in_specs=[
            tok(V_A), tok(V_A), tok(V_A), tok(V_B), *_stream_specs(),
            pl.BlockSpec((1, 1, 6, D), lambda i: (l, _mod_row(i), 0, 0)),
            vec(DV_A), vec(DV_B),
            pl.BlockSpec((V_A + V_B, D), lambda i: (0, 0)),
            vec(D), vec(D),
        ],
        out_specs=[tok(D), pl.BlockSpec((TM * ROWS_PER_TOKEN, LANES), lambda i: (i, 0))],
        out_shape=[jax.ShapeDtypeStruct((T, D), F32), jax.ShapeDtypeStruct((T * ROWS_PER_TOKEN, LANES), U32)],
        compiler_params=_params(("arbitrary",)),
        name="mixer_ab_out",
    )(o_f, o_b, r_a, o_att, xp, xs, mod, gla_g, diff_g, w_o, g_post, g_ffn)


def _gelu_tanh(x):
    return 0.5 * x * (1.0 + jnp.tanh(math.sqrt(2.0 / math.pi) * (x + 0.044715 * (x * x * x))))


def _sgu_kernel(xp_ref, xs_ref, mod_ref, gpre_ref, win_ref, bin_ref, vg_ref, ws_ref, bs_ref,
                wout_ref, gpost_ref, gffn_ref, o_ref, hp_ref, t_ref):
    m = mod_ref[0, 0]
    x = _stream_tile(xp_ref, xs_ref)
    h = _rms(x, gpre_ref[...]) * (1.0 + m[1:2]) + m[0:1]
    z = _gelu_tanh(_bdot(h, win_ref[...]) + bin_ref[...])
    v = _rms(z[:, SGU_DIM:], vg_ref[...])
    gw = SGU_DIM // SGU_GROUPS
    for ch in range(TM // SGU_CHUNK):
        rows = slice(ch * SGU_CHUNK, (ch + 1) * SGU_CHUNK)
        for g in range(SGU_GROUPS):
            cols = slice(g * gw, (g + 1) * gw)
            vs = _bdot(ws_ref[g], v[rows, cols]) + bs_ref[:, g:g + 1]
            t_ref[rows, cols] = (z[rows, cols] * vs).astype(BF16)
    out = jnp.dot(t_ref[...], wout_ref[...], preferred_element_type=F32)
    x1 = x + m[2:3] * _rms(out, gpost_ref[...])
    o_ref[...] = x1
    _store_token_tiles(hp_ref, _ffn_input_rows(x1, m, gffn_ref[...]))


def _sgu(xp, xs, mod, l, g_pre, w_in, b_in, v_g, w_s, b_s_t, w_out, g_post, g_ffn):
    tok = pl.BlockSpec((TM, D), lambda i: (i, 0))
    full = lambda *shape: pl.BlockSpec(shape, lambda i: (0,) * len(shape))
    return pl.pallas_call(
        _sgu_kernel,
        grid=(T // TM,),
        in_specs=_stream_specs() + [
            pl.BlockSpec((1, 1, 6, D), lambda i: (l, _mod_row(i), 0, 0)),
            full(1, D), full(D, 2 * SGU_DIM), full(1, 2 * SGU_DIM), full(1, SGU_DIM),
            full(SGU_GROUPS, SGU_CHUNK, SGU_CHUNK), full(SGU_CHUNK, SGU_GROUPS),
            full(SGU_DIM, D), full(1, D), full(1, D),
        ],
        out_specs=[tok, pl.BlockSpec((TM * ROWS_PER_TOKEN, LANES), lambda i: (i, 0))],
        out_shape=[jax.ShapeDtypeStruct((T, D), F32), jax.ShapeDtypeStruct((T * ROWS_PER_TOKEN, LANES), U32)],
        scratch_shapes=[pltpu.VMEM((TM, SGU_DIM), BF16)],
        compiler_params=_params(("arbitrary",)),
        name="sgu_mixer",
    )(xp, xs, mod, g_pre, w_in, b_in, v_g, w_s, b_s_t, w_out, g_post, g_ffn)


LANES = 128
U32 = jnp.uint32
PACKED = D // 2
ROWS_PER_TOKEN = PACKED // LANES
BF16_BITS = 16
HIGH_HALF = 0xFFFF0000


def _pack_rows(x):
    bits = lax.bitcast_convert_type(x.astype(BF16).astype(F32), U32)
    return bits[:, :PACKED] | (bits[:, PACKED:] >> BF16_BITS)


def _unpack_rows(u):
    return (lax.bitcast_convert_type(u & U32(HIGH_HALF), F32), lax.bitcast_convert_type(u << BF16_BITS, F32))


def _store_token_tiles(ref, u, first=0):
    n = u.shape[0]
    for c in range(ROWS_PER_TOKEN):
        ref[pl.ds(first * ROWS_PER_TOKEN + c, n, stride=ROWS_PER_TOKEN), :] = u[:, c * LANES:(c + 1) * LANES]


def _load_token_tiles(ref, n, first=0):
    return jnp.concatenate([ref[pl.ds(first * ROWS_PER_TOKEN + c, n, stride=ROWS_PER_TOKEN), :]
                            for c in range(ROWS_PER_TOKEN)], axis=1)


def _ffn_input_rows(x, m, g):
    return _pack_rows(_rms(x, g) * (1.0 + m[4:5]) + m[3:4])


def _router_kernel(hp_ref, wr_ref, eb_ref, te_ref, wn_ref, rk_ref, cnt_ref, carry_ref, upper_ref):
    i = pl.program_id(0)

    @pl.when(i == 0)
    def _():
        carry_ref[...] = jnp.zeros_like(carry_ref)
        tj = lax.broadcasted_iota(I32, (TR, TR), 0)
        ti = lax.broadcasted_iota(I32, (TR, TR), 1)
        upper_ref[...] = jnp.where(tj < ti, 1.0, 0.0).astype(BF16)

    h_hi, h_lo = (t.astype(BF16) for t in _unpack_rows(_load_token_tiles(hp_ref, TR)))
    w1, w2, _ = _split3(wr_ref[...])
    nt = lambda a, b: lax.dot_general(a, b, (((1,), (1,)), ((), ())), preferred_element_type=F32)
    logits = (nt(w1[:, :PACKED], h_hi) + nt(w1[:, PACKED:], h_lo)
              + nt(w2[:, :PACKED], h_hi) + nt(w2[:, PACKED:], h_lo))
    scores = jax.nn.sigmoid(logits)
    sel = scores + eb_ref[...]

    row8 = lax.broadcasted_iota(I32, (GROUP_SIZE, TR), 0)
    gscore = []
    for g in range(N_GROUPS):
        xg = sel[g * GROUP_SIZE:(g + 1) * GROUP_SIZE]
        m1 = jnp.max(xg, axis=0, keepdims=True)
        i1 = jnp.min(jnp.where(xg == m1, row8, GROUP_SIZE), axis=0, keepdims=True)
        m2 = jnp.max(jnp.where(row8 == i1, NEG_INF, xg), axis=0, keepdims=True)
        gscore.append(m1 + m2)
    pieces = []
    for g in range(N_GROUPS):
        rank = jnp.zeros((1, TR), I32)
        for g2 in range(N_GROUPS):
            if g2 == g:
                continue
            beats = (gscore[g2] >= gscore[g]) if g2 < g else (gscore[g2] > gscore[g])
            rank = rank + beats.astype(I32)
        pieces.append(jnp.where(rank < TOPK_GROUPS, sel[g * GROUP_SIZE:(g + 1) * GROUP_SIZE], NEG_INF))
    cur = jnp.concatenate(pieces, axis=0)

    row = lax.broadcasted_iota(I32, (N_EXPERTS, TR), 0)
    idxs, ws = [], []
    for _ in range(TOP_K):
        mx = jnp.max(cur, axis=0, keepdims=True)
        idx = jnp.min(jnp.where(cur == mx, row, N_EXPERTS), axis=0, keepdims=True)
        hit = row == idx
        ws.append(jnp.sum(jnp.where(hit, scores, 0.0), axis=0, keepdims=True))
        cur = jnp.where(hit, NEG_INF, cur)
        idxs.append(idx)
    mask = jnp.zeros((N_EXPERTS, TR), F32)
    for idx in idxs:
        mask = mask + (row == idx).astype(F32)
    wsum = ws[0]
    for wk in ws[1:]:
        wsum = wsum + wk

    pos = carry_ref[...] + jnp.dot(mask.astype(BF16), upper_ref[...], preferred_element_type=F32)
    for k in range(TOP_K):
        hit = row == idxs[k]
        te_ref[k:k + 1, :] = idxs[k]
        wn_ref[k:k + 1, :] = ws[k] / wsum * ROUTED_SCALE
        rk_ref[k:k + 1, :] = jnp.sum(jnp.where(hit, pos, 0.0), axis=0, keepdims=True).astype(I32)
    carry_ref[...] = carry_ref[...] + jnp.sum(mask, axis=1, keepdims=True)
    cnt_ref[...] = carry_ref[...]


def _router(hp, wr_t, e_bias):
    kt = lambda dtype: jax.ShapeDtypeStruct((TOP_K, T), dtype)
    kt_spec = pl.BlockSpec((TOP_K, TR), lambda i: (0, i))
    return pl.pallas_call(
        _router_kernel,
        grid=(T // TR,),
        in_specs=[
            pl.BlockSpec((TR * ROWS_PER_TOKEN, LANES), lambda i: (i, 0)),
            pl.BlockSpec((N_EXPERTS, D), lambda i: (0, 0)),
            pl.BlockSpec((N_EXPERTS, 1), lambda i: (0, 0)),
        ],
        out_specs=[
            kt_spec, kt_spec, kt_spec,
            pl.BlockSpec((N_EXPERTS, 1), lambda i: (0, 0)),
        ],
        out_shape=[
            kt(I32), kt(F32), kt(I32),
            jax.ShapeDtypeStruct((N_EXPERTS, 1), F32),
        ],
        scratch_shapes=[pltpu.VMEM((N_EXPERTS, 1), F32), pltpu.VMEM((TR, TR), BF16)],
        compiler_params=_params(("arbitrary",)),
        name="moe_router",
    )(hp, wr_t, e_bias)


_PAD_BITS = tuple(1 << b for b in range(GM_SUB.bit_length() - 1))


def _pad_fill_kernel(pad_start_ref, pad_len_ref, xg_in_ref, xg_ref, zero_ref, sem):
    del xg_in_ref
    zero_ref[...] = jnp.zeros_like(zero_ref)

    def pad_copies(e):
        start = pad_start_ref[e]
        n = pad_len_ref[e]
        copies = []
        for bit in _PAD_BITS:
            first = start + (n & ~(2 * bit - 1))
            copies.append(((n & bit) != 0, pltpu.make_async_copy(
                zero_ref.at[pl.ds(0, bit)], xg_ref.at[pl.ds(first, bit)], sem)))
        return copies

    def start_e(e, carry):
        for on, cp in pad_copies(e):
            @pl.when(on)
            def _():
                cp.start()
        return carry

    def wait_e(e, carry):
        for on, cp in pad_copies(e):
            @pl.when(on)
            def _():
                cp.wait()
        return carry

    lax.fori_loop(0, N_EXPERTS, start_e, 0)
    lax.fori_loop(0, N_EXPERTS, wait_e, 0)


def _pad_fill(pad_start, pad_len, xg):
    grid_spec = pltpu.PrefetchScalarGridSpec(
        num_scalar_prefetch=2,
        grid=(1,),
        in_specs=[pl.BlockSpec(memory_space=pl.ANY)],
        out_specs=pl.BlockSpec(memory_space=pl.ANY),
        scratch_shapes=[pltpu.VMEM((GM_SUB // 2, ROWS_PER_TOKEN, LANES), xg.dtype), pltpu.SemaphoreType.DMA],
    )
    return pl.pallas_call(
        _pad_fill_kernel,
        grid_spec=grid_spec,
        out_shape=jax.ShapeDtypeStruct(xg.shape, xg.dtype),
        input_output_aliases={2: 0},
        compiler_params=_params(("arbitrary",)),
        name="moe_pad_fill",
    )(pad_start, pad_len, xg)


SC_CORES, SC_SUBCORES = 2, 16
SC_WORKERS = SC_CORES * SC_SUBCORES
SC_W = 64


def _sc_worker_id():
    return lax.axis_index("s") * SC_CORES + lax.axis_index("c")


def _sc_dispatch(h3, slot3):
    nchunk = T // SC_WORKERS // SC_W
    mesh = plsc.VectorSubcoreMesh(core_axis_name="c", subcore_axis_name="s")
    tile = (SC_W, ROWS_PER_TOKEN, LANES)

    @functools.partial(
        pl.kernel, mesh=mesh,
        out_type=jax.ShapeDtypeStruct((SP, ROWS_PER_TOKEN, LANES), h3.dtype),
        scratch_types=[pltpu.VMEM((TOP_K, SC_W), I32), pltpu.VMEM((TOP_K, SC_W), I32),
                       pltpu.VMEM(tile, h3.dtype), pltpu.VMEM(tile, h3.dtype),
                       pltpu.SemaphoreType.DMA((2,)), pltpu.SemaphoreType.DMA((2,))],
    )
    def k(h_hbm, slot_hbm, xg_hbm, idx0, idx1, rows0, rows1, lsem, ssem):
        first = _sc_worker_id() * nchunk
        idx = (idx0, idx1)
        rows = (rows0, rows1)

        def loads(j, b):
            blk = first + j
            tok = pl.multiple_of(blk * SC_W, SC_W)
            return (pltpu.make_async_copy(slot_hbm.at[blk], idx[b], lsem.at[b]),
                    pltpu.make_async_copy(h_hbm.at[pl.ds(tok, SC_W)], rows[b], lsem.at[b]))

        def scatters(b):
            return [pltpu.make_async_copy(rows[b], xg_hbm.at[idx[b].at[kk]], ssem.at[b]) for kk in range(TOP_K)]

        for cp in loads(0, 0):
            cp.start()

        @pl.loop(0, nchunk, step=2)
        def _(j):
            for b in (0, 1):
                jj = j + b
                for cp in loads(jj, b):
                    cp.wait()
                for cp in scatters(b):
                    cp.start()

                @pl.when(jj + 1 < nchunk)
                def _():
                    @pl.when(jj >= 1)
                    def _():
                        for cp in scatters(1 - b):
                            cp.wait()
                    for cp in loads(jj + 1, 1 - b):
                        cp.start()

        for b in (0, 1):
            for cp in scatters(b):
                cp.wait()

    return k(h3, slot3)


NSUB = GM // GM_SUB
SUB_ROWS = GM_SUB * ROWS_PER_TOKEN


def _per_block_count(count, fn):
    for n in range(1, NSUB + 1):
        @pl.when(count == n)
        def _(n=n):
            fn(n)


def _gmm_kernel(layer, tile_e_ref, tile_blk_ref, tile_nsub_ref, tile_run_ref, tile_next_ref,
                xg_hbm, wg_hbm, wu_hbm, wd_hbm, yg_hbm,
                x_st, y_st, wg_st, wu_st, wd_st, wgu_scr, wd_scr, sems, xsems, ysems):
    j = pl.program_id(0)
    last_step = pl.num_programs(0) - 1
    nsub = tile_nsub_ref[j]
    run = tile_run_ref[j]
    slot = j % 2
    x_ref = x_st.at[slot]
    y_ref = y_st.at[slot]

    def tile_rows(t, n):
        return pl.ds(pl.multiple_of(tile_blk_ref[t] * SUB_ROWS, SUB_ROWS), n * SUB_ROWS)

    def x_copy(t, n):
        return pltpu.make_async_copy(xg_hbm.at[tile_rows(t, n)], x_st.at[t % 2, pl.ds(0, n * SUB_ROWS)], xsems.at[t % 2])

    def y_copy(t, n):
        return pltpu.make_async_copy(y_st.at[t % 2, pl.ds(0, n * SUB_ROWS)], yg_hbm.at[tile_rows(t, n)], ysems.at[t % 2])

    @pl.when(j == 0)
    def _():
        _per_block_count(nsub, lambda n: x_copy(j, n).start())

    _per_block_count(nsub, lambda n: x_copy(j, n).wait())

    @pl.when(j < last_step)
    def _():
        _per_block_count(tile_nsub_ref[j + 1], lambda n: x_copy(j + 1, n).start())

    @pl.when(j >= 2)
    def _():
        _per_block_count(tile_nsub_ref[j - 2], lambda n: y_copy(j - 2, n).wait())

    def weight_copies(e, slot):
        return [pltpu.make_async_copy(src.at[layer, e], dst.at[slot], sems.at[slot])
                for src, dst in ((wg_hbm, wg_st), (wu_hbm, wu_st), (wd_hbm, wd_st))]

    @pl.when(run >= 0)
    def _():
        @pl.when(j == 0)
        def _():
            for cp in weight_copies(tile_e_ref[j], run):
                cp.start()

        for cp in weight_copies(tile_e_ref[j], run):
            cp.wait()

        nxt = tile_next_ref[j]

        @pl.when(nxt >= 0)
        def _():
            for cp in weight_copies(nxt, 1 - run):
                cp.start()

        wgu_scr[:, 0:D_EXPERT] = wg_st[run].astype(BF16)
        wgu_scr[:, D_EXPERT:2 * D_EXPERT] = wu_st[run].astype(BF16)
        wd_scr[...] = wd_st[run].astype(BF16)

    def gate_up(s):
        x_hi, x_lo = _unpack_rows(_load_token_tiles(x_ref, GM_SUB, s * GM_SUB))
        return (jnp.dot(x_hi.astype(BF16), wgu_scr[0:PACKED, :], preferred_element_type=F32)
                + jnp.dot(x_lo.astype(BF16), wgu_scr[PACKED:D, :], preferred_element_type=F32))

    def down(s, gu):
        hid = _silu(gu[:, 0:D_EXPERT]) * gu[:, D_EXPERT:2 * D_EXPERT]
        y = jnp.dot(hid.astype(BF16), wd_scr[...], preferred_element_type=F32)
        _store_token_tiles(y_ref, _pack_rows(y), s * GM_SUB)

    def compute_and_send(n):
        pending = gate_up(0)
        for s in range(n):
            ahead = gate_up(s + 1) if s + 1 < n else None
            down(s, pending)
            pending = ahead
        y_copy(j, n).start()

    _per_block_count(nsub, compute_and_send)

    @pl.when(j == last_step)
    def _():
        _per_block_count(tile_nsub_ref[jnp.maximum(j - 1, 0)], lambda n: y_copy(j - 1, n).wait())
        _per_block_count(nsub, lambda n: y_copy(j, n).wait())


def _gmm(tile_e, tile_blk, tile_nsub, tile_run, tile_next, xg, l, w_gate, w_up, w_down):
    hbm = pl.BlockSpec(memory_space=pl.ANY)
    stage = pltpu.VMEM((2, GM * ROWS_PER_TOKEN, LANES), U32)
    grid_spec = pltpu.PrefetchScalarGridSpec(
        num_scalar_prefetch=5,
        grid=(NT_MAX,),
        in_specs=[hbm, hbm, hbm, hbm],
        out_specs=hbm,
        scratch_shapes=[stage, stage,
                        pltpu.VMEM((2, D, D_EXPERT), F32), pltpu.VMEM((2, D, D_EXPERT), F32),
                        pltpu.VMEM((2, D_EXPERT, D), F32),
                        pltpu.VMEM((D, 2 * D_EXPERT), BF16), pltpu.VMEM((D_EXPERT, D), BF16),
                        pltpu.SemaphoreType.DMA((2,)), pltpu.SemaphoreType.DMA((2,)), pltpu.SemaphoreType.DMA((2,))],
    )
    return pl.pallas_call(
        functools.partial(_gmm_kernel, l),
        grid_spec=grid_spec,
        out_shape=jax.ShapeDtypeStruct((SP * ROWS_PER_TOKEN, LANES), U32),
        compiler_params=_params(("arbitrary",)),
        name="moe_grouped_matmul",
    )(tile_e, tile_blk, tile_nsub, tile_run, tile_next, xg, w_gate, w_up, w_down)


def _sc_gather(table3, idx):
    n_idx = idx.shape[0]
    per_w = n_idx // SC_WORKERS
    nchunk = per_w // SC_W
    mesh = plsc.VectorSubcoreMesh(core_axis_name="c", subcore_axis_name="s")
    tile = (SC_W, ROWS_PER_TOKEN, LANES)

    @functools.partial(
        pl.kernel, mesh=mesh,
        out_type=jax.ShapeDtypeStruct((n_idx, ROWS_PER_TOKEN, LANES), table3.dtype),
        scratch_types=[pltpu.VMEM((per_w,), I32), pltpu.VMEM(tile, table3.dtype), pltpu.VMEM(tile, table3.dtype),
                       pltpu.SemaphoreType.DMA((2,)), pltpu.SemaphoreType.DMA((2,))],
    )
    def k(table_hbm, idx_hbm, out_hbm, idx_v, rows0, rows1, gsem, wsem):
        base = pl.multiple_of(_sc_worker_id() * per_w, per_w)
        rows = (rows0, rows1)
        pltpu.sync_copy(idx_hbm.at[pl.ds(base, per_w)], idx_v)

        def gather(j, b):
            ids = idx_v.at[pl.ds(pl.multiple_of(j * SC_W, SC_W), SC_W)]
            return pltpu.make_async_copy(table_hbm.at[ids], rows[b], gsem.at[b])

        def write(j, b):
            dst = out_hbm.at[pl.ds(pl.multiple_of(base + j * SC_W, SC_W), SC_W)]
            return pltpu.make_async_copy(rows[b], dst, wsem.at[b])

        gather(0, 0).start()

        @pl.loop(0, nchunk, step=2)
        def _(j):
            for b in (0, 1):
                jj = j + b
                gather(jj, b).wait()
                write(jj, b).start()

                @pl.when(jj + 1 < nchunk)
                def _():
                    @pl.when(jj >= 1)
                    def _():
                        write(jj - 1, 1 - b).wait()
                    gather(jj + 1, 1 - b).start()

        write(nchunk - 2, 0).wait()
        write(nchunk - 1, 1).wait()

    return k(table3, idx)


def _combine_kernel(wn_ref, x_ref, mod_ref, gpre_ref, gp_ref, wsg_ref, wsu_ref, wsd_ref, y_ref, o_ref, eye_ref):
    @pl.when(pl.program_id(0) == 0)
    def _():
        r = lax.broadcasted_iota(I32, (TD, TD), 0)
        c = lax.broadcasted_iota(I32, (TD, TD), 1)
        eye_ref[...] = jnp.where(r == c, 1.0, 0.0).astype(BF16)

    m = mod_ref[0, 0]
    x = x_ref[...]
    hb = (_rms(x, gpre_ref[...]) * (1.0 + m[4:5]) + m[3:4]).astype(BF16)
    hid = (_silu(jnp.dot(hb, wsg_ref[...], preferred_element_type=F32))
           * jnp.dot(hb, wsu_ref[...], preferred_element_type=F32))
    acc = jnp.dot(hid.astype(BF16), wsd_ref[...], preferred_element_type=F32)

    eye = eye_ref[...]
    nt = lambda a, b: lax.dot_general(a, b, (((1,), (1,)), ((), ())), preferred_element_type=F32)
    w1, w2, w3 = _split3(wn_ref[...])
    w_t = nt(eye, w1) + nt(eye, w2) + nt(eye, w3)

    acc_hi = acc[:, :PACKED]
    acc_lo = acc[:, PACKED:]
    for k in range(TOP_K):
        y_hi, y_lo = _unpack_rows(_load_token_tiles(y_ref, TD, k * TD))
        acc_hi = acc_hi + y_hi * w_t[:, k:k + 1]
        acc_lo = acc_lo + y_lo * w_t[:, k:k + 1]
    acc = jnp.concatenate([acc_hi, acc_lo], axis=1)
    o_ref[...] = x + m[5:6] * _rms(acc, gp_ref[...])


def _combine(wn, x, mod, l, g_pre, g_post, ws_gate, ws_up, ws_down, ybuf, first_tok, n_tok):
    off = first_tok // TD
    tiles_per_dec = DEC_SEQ // TD
    npd = TP // TD
    mod_row = lambda i: jnp.where(i + off < npd, 0, 1 + (i + off - npd) // tiles_per_dec)
    full = lambda *shape: pl.BlockSpec(shape, lambda i: (0,) * len(shape))
    y_spec = pl.BlockSpec((TOP_K * TD * ROWS_PER_TOKEN, LANES), lambda i: (i, 0))
    return pl.pallas_call(
        _combine_kernel,
        grid=(n_tok // TD,),
        in_specs=[
            pl.BlockSpec((TOP_K, TD), lambda i: (0, i + off)),
            pl.BlockSpec((TD, D), lambda i: (i + off, 0)),
            pl.BlockSpec((1, 1, 6, D), lambda i: (l, mod_row(i), 0, 0)),
            full(1, D), full(1, D), full(D, D_SHARED), full(D, D_SHARED), full(D_SHARED, D),
            y_spec,
        ],
        out_specs=pl.BlockSpec((TD, D), lambda i: (i, 0)),
        out_shape=jax.ShapeDtypeStruct((n_tok, D), F32),
        scratch_shapes=[pltpu.VMEM((TD, TD), BF16)],
        compiler_params=_params(("arbitrary",)),
        name="moe_combine",
    )(wn, x, mod, g_pre, g_post, ws_gate, ws_up, ws_down, ybuf)


def _moe_layer(x, h, mod, l, g_pre, g_post, w_router, e_bias, w_gate, w_up, w_down,
               ws_gate, ws_up, ws_down):
    top_e, wn, rk, cnt = _router(h, w_router.T, e_bias.reshape(N_EXPERTS, 1))
    cnt = cnt.reshape(N_EXPERTS).astype(I32)
    padded = (cnt + GM_SUB - 1) // GM_SUB * GM_SUB
    ends = jnp.cumsum(padded)
    offs = ends - padded
    eid = jnp.arange(N_EXPERTS, dtype=I32)[:, None, None]
    slot = rk + jnp.sum(jnp.where(top_e[None] == eid, offs[:, None, None], 0), axis=0)
    ntile = (cnt + GM - 1) // GM
    tile_ends = jnp.cumsum(ntile)
    tid = jnp.arange(NT_MAX, dtype=I32)
    last = jnp.maximum(tile_ends[-1] - 1, 0)
    tile_e = jnp.minimum(jnp.sum((tid[:, None] >= tile_ends[None, :]).astype(I32), axis=1), N_EXPERTS - 1)
    tile_e = jnp.where(tid <= last, tile_e, tile_e[last])
    own = tile_e[:, None] == jnp.arange(N_EXPERTS, dtype=I32)[None, :]
    pick = lambda per_expert: jnp.sum(jnp.where(own, per_expert[None, :], 0), axis=1)
    t_in = tid - pick(tile_ends - ntile)
    tile_blk = jnp.where(tid <= last, (pick(offs) + GM * t_in) // GM_SUB, 0)
    tile_rows = jnp.clip(pick(cnt) - GM * t_in, 0, GM)
    tile_nsub = jnp.where(tid <= last, (tile_rows + GM_SUB - 1) // GM_SUB, 0)
    starts = jnp.logical_and(jnp.arange(NT_MAX) <= last,
                             jnp.concatenate([jnp.ones((1,), bool), tile_e[1:] != tile_e[:-1]]))
    tile_run = jnp.where(starts, (jnp.cumsum(starts.astype(I32)) - 1) % 2, -1)
    later = jnp.logical_and(jnp.arange(N_EXPERTS, dtype=I32)[None, :] > tile_e[:, None], (cnt > 0)[None, :])
    tile_next = jnp.min(jnp.where(later, jnp.arange(N_EXPERTS, dtype=I32)[None, :], N_EXPERTS), axis=1)
    tile_next = jnp.where(tile_next < N_EXPERTS, tile_next, -1)
    slot3 = slot.reshape(TOP_K, T // SC_W, SC_W).transpose(1, 0, 2)
    xg = _sc_dispatch(h.reshape(T, ROWS_PER_TOKEN, LANES), slot3)
    xg = _pad_fill(offs + cnt, padded - cnt, xg).reshape(SP * ROWS_PER_TOKEN, LANES)
    yg = _gmm(tile_e, tile_blk, tile_nsub.astype(I32), tile_run.astype(I32), tile_next.astype(I32),
              xg, l, w_gate, w_up, w_down)
    yg3 = yg.reshape(SP, ROWS_PER_TOKEN, LANES)
    ws = (ws_gate.astype(BF16), ws_up.astype(BF16), ws_down.astype(BF16))
    outs = []
    for first_tok, n_tok in ((0, TP), (TP, TS)):
        ids = slot[:, first_tok:first_tok + n_tok].reshape(TOP_K, n_tok // TD, TD).transpose(1, 0, 2)
        ybuf = _sc_gather(yg3, ids.reshape(TOP_K * n_tok))
        outs.append(_combine(wn, x, mod, l, g_pre, g_post, *ws,
                             ybuf.reshape(TOP_K * n_tok * ROWS_PER_TOKEN, LANES), first_tok, n_tok))
    return outs


def _rope_tables():
    n = DEC_SEQ
    rows = n // GRID_W
    row = jnp.repeat(jnp.arange(rows), GRID_W).astype(F32)
    col = jnp.tile(jnp.arange(GRID_W), rows).astype(F32)
    half = DQK_B // 2
    inv = ROPE_BASE ** (-jnp.arange(0, half, 2, dtype=F32) / half)
    ang_r = row[:, None] * inv
    ang_c = col[:, None] * inv
    ang = jnp.concatenate([ang_r, ang_r, ang_c, ang_c], axis=-1)
    reps = QK_B // DQK_B
    return jnp.tile(jnp.cos(ang), (1, reps)), jnp.tile(jnp.sin(ang), (1, reps))


def _pad_in_proj(w):
    s = [0, Q_A, 2 * Q_A, 2 * Q_A + V_A, 2 * Q_A + 2 * V_A]
    s += [s[-1] + GATE_RANK, s[-1] + 2 * GATE_RANK]
    s += [s[-1] + QK_B, s[-1] + 2 * QK_B, s[-1] + 2 * QK_B + V_B]
    gates = jnp.pad(w[:, s[4]:s[6]], ((0, 0), (0, GL_PAD - 2 * GATE_RANK)))
    return jnp.concatenate([w[:, s[0]:s[4]], gates, w[:, s[6]:s[9]]], axis=1).astype(BF16)


def kernel(x_prompt, x_sample, c, c_ctx, state_gla, cache_k, cache_v, ada_w, ada_b, norm_pre_mix, norm_post_mix, norm_pre_ffn, norm_post_ffn, ab_w_in, gla_w_g2, gla_b_g2, gla_norm_g, diff_lambda, diff_norm_g, ab_w_out, sgu_w_in, sgu_b_in, sgu_norm_g, sgu_w_s, sgu_b_s, sgu_w_out, moe_w_router, moe_e_bias, moe_w_gate, moe_w_up, moe_w_down, moe_ws_gate, moe_ws_up, moe_ws_down):
    depth = ada_w.shape[0]
    assert x_prompt.shape == (BATCH, SEQ, D) and x_sample.shape == (DEC_BATCH, DEC_SEQ, D)
    assert state_gla.shape == (DEC_BATCH, (depth + 1) // 2, 2, H_A, DK_A, DV_A)
    assert cache_k.shape == (DEC_BATCH, 1, H_B, 2, PAST_LEN, DQK_B) and cache_v.shape == (DEC_BATCH, 1, H_B, PAST_LEN, DV_B)
    assert depth == 2 and moe_w_gate.shape == (depth, N_EXPERTS, D, D_EXPERT)
    xp, xs = x_prompt.reshape(TP, D), x_sample.reshape(TS, D)
    cond = jnp.concatenate([c_ctx[None, :], c, jnp.zeros((8 - 1 - DEC_BATCH, D), F32)], axis=0)
    mod = _modulation(cond, ada_w, ada_b)
    cos, sin = _rope_tables()
    vec = lambda a: a.reshape(1, -1)
    new_s = new_k = new_v = None
    for l in range(depth):
        if l % 2 == 0:
            e = l // 2
            lam_init = 0.8 - 0.6 * math.exp(-0.3 * l)
            a, r_a, gl, q_b, k_b, v_b, new_k, new_v = _in_proj(xp, xs, mod, l, vec(norm_pre_mix[l]),
                                                               _pad_in_proj(ab_w_in[e]), cos, sin)
            s0_t = jnp.swapaxes(state_gla[:, e], -1, -2)
            same_head = jnp.eye(H_A, dtype=bool)[None, None, :, None, :, None]
            s0_t = jnp.where(same_head, s0_t[:, :, :, :, None, :], 0.0).reshape(DEC_BATCH, 2, V_A, Q_A)
            o_f, o_bw, s_fin_t = _gla(a, gl, gla_w_g2[e], gla_b_g2[e].reshape(2, 1, Q_A), s0_t)
            o_att = _diff_attention(q_b, k_b, v_b, cache_k, cache_v, diff_lambda[e], lam_init)
            x, h = _mix_out(lam_init, o_f, o_bw, r_a, o_att, xp, xs, mod, l, vec(gla_norm_g[e]), vec(diff_norm_g[e]),
                            ab_w_out[e].astype(BF16), vec(norm_post_mix[l]), vec(norm_pre_ffn[l]))
            new_s = jnp.swapaxes(s_fin_t, -1, -2)[:, None]
        else:
            o = l // 2
            x, h = _sgu(xp, xs, mod, l, vec(norm_pre_mix[l]), sgu_w_in[o].astype(BF16), vec(sgu_b_in[o]),
                        vec(sgu_norm_g[o]), sgu_w_s[o], sgu_b_s[o].T, sgu_w_out[o].astype(BF16),
                        vec(norm_post_mix[l]), vec(norm_pre_ffn[l]))
        xp, xs = _moe_layer(x, h, mod, l, vec(norm_pre_ffn[l]), vec(norm_post_ffn[l]), moe_w_router[l], moe_e_bias[l],
                            moe_w_gate, moe_w_up, moe_w_down, moe_ws_gate[l], moe_ws_up[l], moe_ws_down[l])
    y_prompt = xp.reshape(BATCH, SEQ, D)
    y_sample = xs.reshape(DEC_BATCH, DEC_SEQ, D)
    return (y_prompt, y_sample, new_s, new_k, new_v)
```

```python
import functools
import math

import jax
import jax.numpy as jnp
from jax import lax
from jax.experimental import pallas as pl
from jax.experimental.pallas import tpu as pltpu
from jax.experimental.pallas import tpu_sc as plsc

F32 = jnp.float32
BF16 = jnp.bfloat16
I32 = jnp.int32

D = 1024
BATCH, SEQ = 32, 256
DEC_BATCH, DEC_SEQ = 4, 2048
PAST_LEN = 256
GRID_W = 64
EPS = 1e-6
TP = BATCH * SEQ
TS = DEC_BATCH * DEC_SEQ
T = TP + TS
H_A, DK_A, DV_A = 4, 64, 128
Q_A, V_A = H_A * DK_A, H_A * DV_A
GATE_RANK, GATE_TAU, GLA_CHUNK = 16, 16.0, 64
H_B, DQK_B, DV_B = 4, 64, 128
QK_B, V_B = H_B * 2 * DQK_B, H_B * DV_B
ROPE_BASE = 10000.0
SGU_DIM, SGU_GROUPS, SGU_CHUNK = 1024, 4, 128
N_EXPERTS, TOP_K, N_GROUPS, TOPK_GROUPS = 64, 8, 8, 4
GROUP_SIZE = N_EXPERTS // N_GROUPS
D_EXPERT, D_SHARED = 256, 256
ROUTED_SCALE = 2.5

TM = 512
TM_SUB = 256
MIX_DTYPE = jnp.bfloat16
NPT = TP // TM
TILES_PER_DEC = DEC_SEQ // TM
SEG = 256
NSEG = T // SEG
NSEG_P = TP // SEG
SEG_PER_DEC = DEC_SEQ // SEG
TR = 512
TD = 512
GM = 2048
GM_SUB = 256
NT_MAX = T * TOP_K // GM + N_EXPERTS
SP = T * TOP_K + N_EXPERTS * GM_SUB
GL_PAD = 128
VMEM_LIMIT = 56 * 1024 * 1024
NEG_INF = float("-inf")


def _bdot(a, b):
    return jnp.dot(a.astype(BF16), b.astype(BF16), preferred_element_type=F32)


def _bdot_nt(a, b):
    return lax.dot_general(a.astype(BF16), b.astype(BF16), (((1,), (1,)), ((), ())),
                           preferred_element_type=F32)


def _bdot_tn(a, b):
    return lax.dot_general(a.astype(BF16), b.astype(BF16), (((0,), (0,)), ((), ())),
                           preferred_element_type=F32)


def _split3(x):
    x1 = x.astype(BF16)
    r1 = x - x1.astype(F32)
    x2 = r1.astype(BF16)
    x3 = (r1 - x2.astype(F32)).astype(BF16)
    return x1, x2, x3


def _rms(x, g):
    return x * lax.rsqrt(jnp.mean(x * x, axis=-1, keepdims=True) + EPS) * g


def _silu(x):
    return x * jax.nn.sigmoid(x)


def _mod_row(i):
    return jnp.where(i < NPT, 0, 1 + (i - NPT) // TILES_PER_DEC)


def _params(sem, limit=VMEM_LIMIT):
    return pltpu.CompilerParams(dimension_semantics=sem, vmem_limit_bytes=limit)


def _mod_kernel(c_ref, w_ref, b_ref, o_ref):
    o_ref[0] = _bdot(_silu(c_ref[...]), w_ref[0]) + b_ref[0]


def _modulation(cond, ada_w, ada_b):
    depth = ada_w.shape[0]
    nj = 6
    out = pl.pallas_call(
        _mod_kernel,
        grid=(depth, nj),
        in_specs=[
            pl.BlockSpec((8, D), lambda l, j: (0, 0)),
            pl.BlockSpec((1, D, D), lambda l, j: (l, 0, j)),
            pl.BlockSpec((1, 1, D), lambda l, j: (l, 0, j)),
        ],
        out_specs=pl.BlockSpec((1, 8, D), lambda l, j: (l, 0, j)),
        out_shape=jax.ShapeDtypeStruct((depth, 8, 6 * D), F32),
        compiler_params=_params(("arbitrary", "arbitrary")),
        name="adaln_modulation",
    )(cond, ada_w, ada_b.reshape(depth, 1, 6 * D))
    return out.reshape(depth, 8, 6, D)


_C_A = 0
_C_R = _C_A + 2 * Q_A + V_A
_C_GL = _C_R + V_A
_C_Q = _C_GL + GL_PAD
_C_K = _C_Q + QK_B
_C_V = _C_K + QK_B
_C_END = _C_V + V_B
ROT_PAIR = DQK_B // 4


def _rope(x, cos, sin):
    lane = lax.broadcasted_iota(I32, x.shape, 1)
    first = (lane % (2 * ROT_PAIR)) < ROT_PAIR
    n = x.shape[1]
    xr = jnp.where(first, -pltpu.roll(x, n - ROT_PAIR, 1), pltpu.roll(x, ROT_PAIR, 1))
    return x * cos + xr * sin


def _stream_specs():
    return [pl.BlockSpec((TM, D), lambda i: (jnp.minimum(i, NPT - 1), 0)),
            pl.BlockSpec((TM, D), lambda i: (jnp.maximum(i - NPT, 0), 0))]


def _stream_tile(xp_ref, xs_ref):
    return jnp.where(pl.program_id(0) < NPT, xp_ref[...], xs_ref[...])


def _in_kernel(xp_ref, xs_ref, mod_ref, g_ref, w_ref, cos_ref, sin_ref,
               a_ref, r_ref, gl_ref, q_ref, k_ref, v_ref, ck_ref, cv_ref, k32_ref, v32_ref):
    latent = pl.program_id(0) >= NPT
    m = mod_ref[0, 0]
    x = _stream_tile(xp_ref, xs_ref)
    for s in range(TM // TM_SUB):
        rows = slice(s * TM_SUB, (s + 1) * TM_SUB)
        h = _rms(x[rows], g_ref[...]) * (1.0 + m[1:2]) + m[0:1]
        hb = h.astype(BF16)

        def proj(c0, c1):
            return jnp.dot(hb, w_ref[:, c0:c1], preferred_element_type=F32)

        a_ref[rows, :] = proj(_C_A, _C_R)
        r_ref[rows, :] = proj(_C_R, _C_GL).astype(MIX_DTYPE)
        gl_ref[rows, :] = proj(_C_GL, _C_Q)
        v = proj(_C_V, _C_END)
        q = proj(_C_Q, _C_K)
        k = proj(_C_K, _C_V)
        cos = cos_ref[rows, :]
        sin = sin_ref[rows, :]
        q_ref[rows, :] = jnp.where(latent, _rope(q, cos, sin), q).astype(BF16)
        k_ref[rows, :] = jnp.where(latent, _rope(k, cos, sin), k).astype(BF16)
        v_ref[rows, :] = v.astype(BF16)
        k32_ref[rows, :] = k
        v32_ref[rows, :] = v

    @pl.when(jnp.logical_not(latent))
    def _():
        for s in range(TM // SEQ):
            rows = slice(s * SEQ, (s + 1) * SEQ)
            for h in range(H_B):
                cv_ref[s, 0, h] = v32_ref[rows, h * DV_B:(h + 1) * DV_B]
                for mp in range(2):
                    ck_ref[s, 0, h, mp] = k32_ref[rows, (2 * h + mp) * DQK_B:(2 * h + mp + 1) * DQK_B]


def _in_proj(xp, xs, mod, l, g, w_pad, cos, sin):
    tok = lambda width: pl.BlockSpec((TM, width), lambda i: (i, 0))
    rope_spec = pl.BlockSpec((TM, QK_B), lambda i: (jnp.maximum(i - NPT, 0) % TILES_PER_DEC, 0))
    widths = (2 * Q_A + V_A, V_A, GL_PAD, QK_B, QK_B, V_B)
    dtypes = (F32, MIX_DTYPE, F32, BF16, BF16, BF16)
    seqs = TM // SEQ
    prompt_tile = lambda i: jnp.minimum(i, NPT - 1)
    cache_specs = [pl.BlockSpec((seqs, 1, H_B, 2, SEQ, DQK_B), lambda i: (prompt_tile(i), 0, 0, 0, 0, 0)),
                   pl.BlockSpec((seqs, 1, H_B, SEQ, DV_B), lambda i: (prompt_tile(i), 0, 0, 0, 0))]
    cache_shapes = [jax.ShapeDtypeStruct((BATCH, 1, H_B, 2, SEQ, DQK_B), F32),
                    jax.ShapeDtypeStruct((BATCH, 1, H_B, SEQ, DV_B), F32)]
    return pl.pallas_call(
        _in_kernel,
        grid=(T // TM,),
        in_specs=_stream_specs() + [
            pl.BlockSpec((1, 1, 6, D), lambda i: (l, _mod_row(i), 0, 0)),
            pl.BlockSpec((1, D), lambda i: (0, 0)),
            pl.BlockSpec((D, _C_END), lambda i: (0, 0)),
            rope_spec, rope_spec,
        ],
        out_specs=[tok(w) for w in widths] + cache_specs,
        out_shape=[jax.ShapeDtypeStruct((T, w), dt) for w, dt in zip(widths, dtypes)] + cache_shapes,
        scratch_shapes=[pltpu.VMEM((TM, QK_B), F32), pltpu.VMEM((TM, V_B), F32)],
        compiler_params=_params(("arbitrary",)),
        name="mixer_ab_in_proj",
    )(xp, xs, mod, g, w_pad, cos, sin)


def _log_sigmoid(x):
    return jnp.minimum(x, 0.0) - jnp.log(1.0 + jnp.exp(-jnp.abs(x)))


def _gla_kernel(af_ref, ab_ref, glf_ref, glb_ref, wg_ref, bg_ref, s0_ref,
                of_ref, ob_ref, sfin_ref, st_ref):
    i = pl.program_id(0)

    @pl.when(i < NSEG_P)
    def _():
        st_ref[...] = jnp.zeros_like(st_ref)

    @pl.when(jnp.logical_and(i >= NSEG_P, (i - NSEG_P) % SEG_PER_DEC == 0))
    def _():
        st_ref[...] = s0_ref[0]

    r = lax.broadcasted_iota(I32, (SEG, SEG), 0)
    c = lax.broadcasted_iota(I32, (SEG, SEG), 1)
    same = (r // GLA_CHUNK) == (c // GLA_CHUNK)
    nchunk = SEG // GLA_CHUNK
    own_head = (lax.broadcasted_iota(I32, (V_A, Q_A), 0) // DV_A) == (lax.broadcasted_iota(I32, (V_A, Q_A), 1) // DK_A)

    dirs = []
    for d, (a_ref, gl_ref, o_ref) in enumerate(((af_ref, glf_ref, of_ref), (ab_ref, glb_ref, ob_ref))):
        fwd = d == 0
        gcol = gl_ref[:, d * GATE_RANK:(d + 1) * GATE_RANK]
        la = _log_sigmoid(_bdot(gcol, wg_ref[d]) + bg_ref[d]) / GATE_TAU
        causal = jnp.logical_and(same, (c <= r) if fwd else (c >= r))
        tri = jnp.where(causal, 1.0, 0.0).astype(BF16)
        l1, l2, l3 = _split3(la)
        b_all = (jnp.dot(tri, l1, preferred_element_type=F32)
                 + jnp.dot(tri, l2, preferred_element_type=F32)
                 + jnp.dot(tri, l3, preferred_element_type=F32))
        q_in_all = a_ref[:, 0:Q_A] * (DK_A ** -0.5) * jnp.exp(b_all)
        kd_all = a_ref[:, Q_A:2 * Q_A] * jnp.exp(-b_all)
        dirs.append(dict(fwd=fwd, a=a_ref, o=o_ref, b=b_all, q=q_in_all, kd=kd_all, causal=causal,
                         heads=[], state=st_ref[d]))

    for h in range(H_A):
        kc = slice(h * DK_A, (h + 1) * DK_A)
        for dd in dirs:
            attn = jnp.where(dd["causal"], _bdot_nt(dd["q"][:, kc], dd["kd"][:, kc]), 0.0)
            dd["heads"].append(_bdot(attn, dd["a"][:, 2 * Q_A + h * DV_A:2 * Q_A + (h + 1) * DV_A]))
    for dd in dirs:
        dd["intra"] = jnp.concatenate(dd["heads"], axis=1)

    for step in range(nchunk):
        for dd in dirs:
            ch = step if dd["fwd"] else nchunk - 1 - step
            r0 = ch * GLA_CHUNK
            rows = slice(r0, r0 + GLA_CHUNK)
            end = r0 + GLA_CHUNK - 1 if dd["fwd"] else r0
            b_end = dd["b"][end:end + 1, :]
            kw = dd["a"][rows, Q_A:2 * Q_A] * jnp.exp(b_end - dd["b"][rows, :])
            dd["o"][rows, :] = (dd["intra"][rows, :] + _bdot_nt(dd["q"][rows, :], dd["state"])).astype(MIX_DTYPE)
            kv_t = _bdot_tn(dd["a"][rows, 2 * Q_A:2 * Q_A + V_A], kw)
            dd["state"] = dd["state"] * jnp.exp(b_end) + jnp.where(own_head, kv_t, 0.0)
    for d, dd in enumerate(dirs):
        st_ref[d] = dd["state"]

    @pl.when(i < NSEG_P)
    def _():
        for d in range(2):
            for h in range(H_A):
                sfin_ref[0, d, h] = st_ref[d, h * DV_A:(h + 1) * DV_A, h * DK_A:(h + 1) * DK_A]


def _seg_bwd(i):
    j = i - NSEG_P
    return jnp.where(i < NSEG_P, i, NSEG_P + (j // SEG_PER_DEC) * SEG_PER_DEC + (SEG_PER_DEC - 1 - j % SEG_PER_DEC))


def _gla(a, gl, wg, bg, s0_t):
    seg = lambda width, f: pl.BlockSpec((SEG, width), lambda i: (f(i), 0))
    ident = lambda i: i
    st_block = (1, 2, H_A, DV_A, DK_A)
    return pl.pallas_call(
        _gla_kernel,
        grid=(NSEG,),
        in_specs=[
            seg(D, ident), seg(D, _seg_bwd), seg(GL_PAD, ident), seg(GL_PAD, _seg_bwd),
            pl.BlockSpec((2, GATE_RANK, Q_A), lambda i: (0, 0, 0)),
            pl.BlockSpec((2, 1, Q_A), lambda i: (0, 0, 0)),
            pl.BlockSpec((1, 2, V_A, Q_A), lambda i: (jnp.maximum(i - NSEG_P, 0) // SEG_PER_DEC, 0, 0, 0)),
        ],
        out_specs=[
            seg(V_A, ident), seg(V_A, _seg_bwd),
            pl.BlockSpec(st_block, lambda i: (jnp.minimum(i, NSEG_P - 1), 0, 0, 0, 0)),
        ],
        out_shape=[
            jax.ShapeDtypeStruct((T, V_A), MIX_DTYPE),
            jax.ShapeDtypeStruct((T, V_A), MIX_DTYPE),
            jax.ShapeDtypeStruct((BATCH, 2, H_A, DV_A, DK_A), F32),
        ],
        scratch_shapes=[pltpu.VMEM((2, V_A, Q_A), F32)],
        compiler_params=_params(("arbitrary",)),
        name="gla_bidir",
    )(a, a, gl, gl, wg, bg, s0_t)


def _diff_lambda(lam_ref, lam_init):
    lp = lam_ref[...]
    s01 = jnp.sum(lp[0:1] * lp[1:2], axis=1, keepdims=True)
    s23 = jnp.sum(lp[2:3] * lp[3:4], axis=1, keepdims=True)
    return jnp.exp(s01) - jnp.exp(s23) + lam_init


def _attn_prompt_kernel(lam_init, q_ref, k_ref, v_ref, lam_ref, o_ref):
    lam = _diff_lambda(lam_ref, lam_init)

    def scores(h):
        out = []
        for m in range(2):
            cols = slice((2 * h + m) * DQK_B, (2 * h + m + 1) * DQK_B)
            out.append(_bdot_nt(q_ref[:, cols] * (DQK_B ** -0.5), k_ref[:, cols]))
        return out

    def finish(h, head_scores):
        ps = []
        for s in head_scores:
            e = jnp.exp(s - jnp.max(s, axis=1, keepdims=True))
            ps.append(e * (1.0 / jnp.sum(e, axis=1, keepdims=True)))
        w = ps[0] - lam * ps[1]
        o_ref[:, h * DV_B:(h + 1) * DV_B] = _bdot(w, v_ref[:, h * DV_B:(h + 1) * DV_B]).astype(MIX_DTYPE)

    pending = scores(0)
    for h in range(H_B):
        ahead = scores(h + 1) if h + 1 < H_B else None
        finish(h, pending)
        pending = ahead


def _attn_sample_kernel(lam_init, q_ref, k_ref, v_ref, ck_ref, cv_ref, lam_ref, o_ref):
    lam = _diff_lambda(lam_ref, lam_init)

    def scores(h):
        out = []
        for m in range(2):
            cols = slice((2 * h + m) * DQK_B, (2 * h + m + 1) * DQK_B)
            q = q_ref[:, cols] * (DQK_B ** -0.5)
            out.append((_bdot_nt(q, ck_ref[0, 0, h, m]), _bdot_nt(q, k_ref[:, cols])))
        return out

    def finish(h, head_scores):
        parts = []
        for m, (sc, sn) in enumerate(head_scores):
            mx = jnp.maximum(jnp.max(sc, axis=1, keepdims=True), jnp.max(sn, axis=1, keepdims=True))
            ec = jnp.exp(sc - mx)
            en = jnp.exp(sn - mx)
            inv = (1.0 if m == 0 else -lam) / (jnp.sum(ec, axis=1, keepdims=True) + jnp.sum(en, axis=1, keepdims=True))
            parts.append((ec * inv, en * inv))
        wc = parts[0][0] + parts[1][0]
        wn = parts[0][1] + parts[1][1]
        o_ref[:, h * DV_B:(h + 1) * DV_B] = (_bdot(wc, cv_ref[0, 0, h])
                                             + _bdot(wn, v_ref[:, h * DV_B:(h + 1) * DV_B])).astype(MIX_DTYPE)

    pending = scores(0)
    for h in range(H_B):
        ahead = scores(h + 1) if h + 1 < H_B else None
        finish(h, pending)
        pending = ahead


QB = SEQ
NQB_DEC = DEC_SEQ // QB


def _attn_kernel(lam_init, q_ref, kp_ref, vp_ref, ks_ref, vs_ref, ck_ref, cv_ref, lam_ref, o_ref):
    i = pl.program_id(0)

    @pl.when(i < BATCH)
    def _():
        _attn_prompt_kernel(lam_init, q_ref, kp_ref, vp_ref, lam_ref, o_ref)

    @pl.when(i >= BATCH)
    def _():
        _attn_sample_kernel(lam_init, q_ref, ks_ref, vs_ref, ck_ref, cv_ref, lam_ref, o_ref)


def _diff_attention(q, k, v, cache_k, cache_v, lam_p, lam_init):
    blk = lambda rows, f: pl.BlockSpec((rows, QK_B), f)
    dec_b = lambda i: jnp.maximum(i - BATCH, 0) // NQB_DEC
    own = lambda i: (i, 0)
    prompt_kv = lambda i: (jnp.minimum(i, BATCH - 1), 0)
    dec_kv = lambda i: (TP // DEC_SEQ + dec_b(i), 0)
    return pl.pallas_call(
        functools.partial(_attn_kernel, lam_init),
        grid=(BATCH + DEC_BATCH * NQB_DEC,),
        in_specs=[
            blk(QB, own), blk(SEQ, prompt_kv), blk(SEQ, prompt_kv), blk(DEC_SEQ, dec_kv), blk(DEC_SEQ, dec_kv),
            pl.BlockSpec((1, 1, H_B, 2, PAST_LEN, DQK_B), lambda i: (dec_b(i), 0, 0, 0, 0, 0)),
            pl.BlockSpec((1, 1, H_B, PAST_LEN, DV_B), lambda i: (dec_b(i), 0, 0, 0, 0)),
            pl.BlockSpec((4, DQK_B), lambda i: (0, 0)),
        ],
        out_specs=blk(QB, own),
        out_shape=jax.ShapeDtypeStruct((T, V_B), MIX_DTYPE),
        compiler_params=_params(("arbitrary",)),
        name="diff_attention",
    )(q, k, v, k, v, cache_k, cache_v, lam_p)


def _head_rms(x, g, nheads, width):
    return jnp.concatenate([_rms(x[:, h * width:(h + 1) * width], g) for h in range(nheads)], axis=1)


def _mix_out_kernel(lam_init, of_ref, ob_ref, r_ref, oatt_ref, xp_ref, xs_ref, mod_ref,
                    gg_ref, dg_ref, wo_ref, gp_ref, gffn_ref, o_ref, hp_ref):
    m = mod_ref[0, 0]
    load = lambda ref: ref[...].astype(F32)
    o_a = _head_rms(load(of_ref) + load(ob_ref), gg_ref[...], H_A, DV_A) * _silu(load(r_ref))
    o_b = _head_rms(load(oatt_ref), dg_ref[...], H_B, DV_B) * (1.0 - lam_init)
    out = _bdot(o_a, wo_ref[0:V_A, :]) + _bdot(o_b, wo_ref[V_A:V_A + V_B, :])
    x1 = _stream_tile(xp_ref, xs_ref) + m[2:3] * _rms(out, gp_ref[...])
    o_ref[...] = x1
    _store_token_tiles(hp_ref, _ffn_input_rows(x1, m, gffn_ref[...]))


def _mix_out(lam_init, o_f, o_b, r_a, o_att, xp, xs, mod, l, gla_g, diff_g, w_o, g_post, g_ffn):
    tok = lambda width: pl.BlockSpec((TM, width), lambda i: (i, 0))
    vec = lambda width: pl.BlockSpec((1, width), lambda i: (0, 0))
    return pl.pallas_call(
        functools.partial(_mix_out_kernel, lam_init),
        grid=(T // TM,),
        in_specs=[
            tok(V_A), tok(V_A), tok(V_A), tok(V_B), *_stream_specs(),
            pl.BlockSpec((1, 1, 6, D), lambda i: (l, _mod_row(i), 0, 0)),
            vec(DV_A), vec(DV_B),
            pl.BlockSpec((V_A + V_B, D), lambda i: (0, 0)),
            vec(D), vec(D),
        ],
        out_specs=[tok(D), pl.BlockSpec((TM * ROWS_PER_TOKEN, LANES), lambda i: (i, 0))],
        out_shape=[jax.ShapeDtypeStruct((T, D), F32), jax.ShapeDtypeStruct((T * ROWS_PER_TOKEN, LANES), U32)],
        compiler_params=_params(("arbitrary",)),
        name="mixer_ab_out",
    )(o_f, o_b, r_a, o_att, xp, xs, mod, gla_g, diff_g, w_o, g_post, g_ffn)


def _gelu_tanh(x):
    return 0.5 * x * (1.0 + jnp.tanh(math.sqrt(2.0 / math.pi) * (x + 0.044715 * (x * x * x))))


def _sgu_kernel(xp_ref, xs_ref, mod_ref, gpre_ref, win_ref, bin_ref, vg_ref, ws_ref, bs_ref,
                wout_ref, gpost_ref, gffn_ref, o_ref, hp_ref, t_ref):
    m = mod_ref[0, 0]
    x = _stream_tile(xp_ref, xs_ref)
    h = _rms(x, gpre_ref[...]) * (1.0 + m[1:2]) + m[0:1]
    z = _gelu_tanh(_bdot(h, win_ref[...]) + bin_ref[...])
    v = _rms(z[:, SGU_DIM:], vg_ref[...])
    gw = SGU_DIM // SGU_GROUPS
    for ch in range(TM // SGU_CHUNK):
        rows = slice(ch * SGU_CHUNK, (ch + 1) * SGU_CHUNK)
        for g in range(SGU_GROUPS):
            cols = slice(g * gw, (g + 1) * gw)
            vs = _bdot(ws_ref[g], v[rows, cols]) + bs_ref[:, g:g + 1]
            t_ref[rows, cols] = (z[rows, cols] * vs).astype(BF16)
    out = jnp.dot(t_ref[...], wout_ref[...], preferred_element_type=F32)
    x1 = x + m[2:3] * _rms(out, gpost_ref[...])
    o_ref[...] = x1
    _store_token_tiles(hp_ref, _ffn_input_rows(x1, m, gffn_ref[...]))


def _sgu(xp, xs, mod, l, g_pre, w_in, b_in, v_g, w_s, b_s_t, w_out, g_post, g_ffn):
    tok = pl.BlockSpec((TM, D), lambda i: (i, 0))
    full = lambda *shape: pl.BlockSpec(shape, lambda i: (0,) * len(shape))
    return pl.pallas_call(
        _sgu_kernel,
        grid=(T // TM,),
        in_specs=_stream_specs() + [
            pl.BlockSpec((1, 1, 6, D), lambda i: (l, _mod_row(i), 0, 0)),
            full(1, D), full(D, 2 * SGU_DIM), full(1, 2 * SGU_DIM), full(1, SGU_DIM),
            full(SGU_GROUPS, SGU_CHUNK, SGU_CHUNK), full(SGU_CHUNK, SGU_GROUPS),
            full(SGU_DIM, D), full(1, D), full(1, D),
        ],
        out_specs=[tok, pl.BlockSpec((TM * ROWS_PER_TOKEN, LANES), lambda i: (i, 0))],
        out_shape=[jax.ShapeDtypeStruct((T, D), F32), jax.ShapeDtypeStruct((T * ROWS_PER_TOKEN, LANES), U32)],
        scratch_shapes=[pltpu.VMEM((TM, SGU_DIM), BF16)],
        compiler_params=_params(("arbitrary",)),
        name="sgu_mixer",
    )(xp, xs, mod, g_pre, w_in, b_in, v_g, w_s, b_s_t, w_out, g_post, g_ffn)


LANES = 128
U32 = jnp.uint32
PACKED = D // 2
ROWS_PER_TOKEN = PACKED // LANES
BF16_BITS = 16
HIGH_HALF = 0xFFFF0000


def _pack_rows(x):
    bits = lax.bitcast_convert_type(x.astype(BF16).astype(F32), U32)
    return bits[:, :PACKED] | (bits[:, PACKED:] >> BF16_BITS)


def _unpack_rows(u):
    return (lax.bitcast_convert_type(u & U32(HIGH_HALF), F32), lax.bitcast_convert_type(u << BF16_BITS, F32))


def _store_token_tiles(ref, u, first=0):
    n = u.shape[0]
    for c in range(ROWS_PER_TOKEN):
        ref[pl.ds(first * ROWS_PER_TOKEN + c, n, stride=ROWS_PER_TOKEN), :] = u[:, c * LANES:(c + 1) * LANES]


def _load_token_tiles(ref, n, first=0):
    return jnp.concatenate([ref[pl.ds(first * ROWS_PER_TOKEN + c, n, stride=ROWS_PER_TOKEN), :]
                            for c in range(ROWS_PER_TOKEN)], axis=1)


def _ffn_input_rows(x, m, g):
    return _pack_rows(_rms(x, g) * (1.0 + m[4:5]) + m[3:4])


def _router_kernel(hp_ref, wr_ref, eb_ref, te_ref, wn_ref, rk_ref, cnt_ref, carry_ref, upper_ref):
    i = pl.program_id(0)

    @pl.when(i == 0)
    def _():
        carry_ref[...] = jnp.zeros_like(carry_ref)
        tj = lax.broadcasted_iota(I32, (TR, TR), 0)
        ti = lax.broadcasted_iota(I32, (TR, TR), 1)
        upper_ref[...] = jnp.where(tj < ti, 1.0, 0.0).astype(BF16)

    h_hi, h_lo = (t.astype(BF16) for t in _unpack_rows(_load_token_tiles(hp_ref, TR)))
    w1, w2, _ = _split3(wr_ref[...])
    nt = lambda a, b: lax.dot_general(a, b, (((1,), (1,)), ((), ())), preferred_element_type=F32)
    logits = (nt(w1[:, :PACKED], h_hi) + nt(w1[:, PACKED:], h_lo)
              + nt(w2[:, :PACKED], h_hi) + nt(w2[:, PACKED:], h_lo))
    scores = jax.nn.sigmoid(logits)
    sel = scores + eb_ref[...]

    row8 = lax.broadcasted_iota(I32, (GROUP_SIZE, TR), 0)
    gscore = []
    for g in range(N_GROUPS):
        xg = sel[g * GROUP_SIZE:(g + 1) * GROUP_SIZE]
        m1 = jnp.max(xg, axis=0, keepdims=True)
        i1 = jnp.min(jnp.where(xg == m1, row8, GROUP_SIZE), axis=0, keepdims=True)
        m2 = jnp.max(jnp.where(row8 == i1, NEG_INF, xg), axis=0, keepdims=True)
        gscore.append(m1 + m2)
    pieces = []
    for g in range(N_GROUPS):
        rank = jnp.zeros((1, TR), I32)
        for g2 in range(N_GROUPS):
            if g2 == g:
                continue
            beats = (gscore[g2] >= gscore[g]) if g2 < g else (gscore[g2] > gscore[g])
            rank = rank + beats.astype(I32)
        pieces.append(jnp.where(rank < TOPK_GROUPS, sel[g * GROUP_SIZE:(g + 1) * GROUP_SIZE], NEG_INF))
    cur = jnp.concatenate(pieces, axis=0)

    row = lax.broadcasted_iota(I32, (N_EXPERTS, TR), 0)
    idxs, ws = [], []
    for _ in range(TOP_K):
        mx = jnp.max(cur, axis=0, keepdims=True)
        idx = jnp.min(jnp.where(cur == mx, row, N_EXPERTS), axis=0, keepdims=True)
        hit = row == idx
        ws.append(jnp.sum(jnp.where(hit, scores, 0.0), axis=0, keepdims=True))
        cur = jnp.where(hit, NEG_INF, cur)
        idxs.append(idx)
    mask = jnp.zeros((N_EXPERTS, TR), F32)
    for idx in idxs:
        mask = mask + (row == idx).astype(F32)
    wsum = ws[0]
    for wk in ws[1:]:
        wsum = wsum + wk

    pos = carry_ref[...] + jnp.dot(mask.astype(BF16), upper_ref[...], preferred_element_type=F32)
    for k in range(TOP_K):
        hit = row == idxs[k]
        te_ref[k:k + 1, :] = idxs[k]
        wn_ref[k:k + 1, :] = ws[k] / wsum * ROUTED_SCALE
        rk_ref[k:k + 1, :] = jnp.sum(jnp.where(hit, pos, 0.0), axis=0, keepdims=True).astype(I32)
    carry_ref[...] = carry_ref[...] + jnp.sum(mask, axis=1, keepdims=True)
    cnt_ref[...] = carry_ref[...]


def _router(hp, wr_t, e_bias):
    kt = lambda dtype: jax.ShapeDtypeStruct((TOP_K, T), dtype)
    kt_spec = pl.BlockSpec((TOP_K, TR), lambda i: (0, i))
    return pl.pallas_call(
        _router_kernel,
        grid=(T // TR,),
        in_specs=[
            pl.BlockSpec((TR * ROWS_PER_TOKEN, LANES), lambda i: (i, 0)),
            pl.BlockSpec((N_EXPERTS, D), lambda i: (0, 0)),
            pl.BlockSpec((N_EXPERTS, 1), lambda i: (0, 0)),
        ],
        out_specs=[
            kt_spec, kt_spec, kt_spec,
            pl.BlockSpec((N_EXPERTS, 1), lambda i: (0, 0)),
        ],
        out_shape=[
            kt(I32), kt(F32), kt(I32),
            jax.ShapeDtypeStruct((N_EXPERTS, 1), F32),
        ],
        scratch_shapes=[pltpu.VMEM((N_EXPERTS, 1), F32), pltpu.VMEM((TR, TR), BF16)],
        compiler_params=_params(("arbitrary",)),
        name="moe_router",
    )(hp, wr_t, e_bias)


_PAD_BITS = tuple(1 << b for b in range(GM_SUB.bit_length() - 1))


def _pad_fill_kernel(pad_start_ref, pad_len_ref, xg_in_ref, xg_ref, zero_ref, sem):
    del xg_in_ref
    zero_ref[...] = jnp.zeros_like(zero_ref)

    def pad_copies(e):
        start = pad_start_ref[e]
        n = pad_len_ref[e]
        copies = []
        for bit in _PAD_BITS:
            first = start + (n & ~(2 * bit - 1))
            copies.append(((n & bit) != 0, pltpu.make_async_copy(
                zero_ref.at[pl.ds(0, bit)], xg_ref.at[pl.ds(first, bit)], sem)))
        return copies

    def start_e(e, carry):
        for on, cp in pad_copies(e):
            @pl.when(on)
            def _():
                cp.start()
        return carry

    def wait_e(e, carry):
        for on, cp in pad_copies(e):
            @pl.when(on)
            def _():
                cp.wait()
        return carry

    lax.fori_loop(0, N_EXPERTS, start_e, 0)
    lax.fori_loop(0, N_EXPERTS, wait_e, 0)


def _pad_fill(pad_start, pad_len, xg):
    grid_spec = pltpu.PrefetchScalarGridSpec(
        num_scalar_prefetch=2,
        grid=(1,),
        in_specs=[pl.BlockSpec(memory_space=pl.ANY)],
        out_specs=pl.BlockSpec(memory_space=pl.ANY),
        scratch_shapes=[pltpu.VMEM((GM_SUB // 2, ROWS_PER_TOKEN, LANES), xg.dtype), pltpu.SemaphoreType.DMA],
    )
    return pl.pallas_call(
        _pad_fill_kernel,
        grid_spec=grid_spec,
        out_shape=jax.ShapeDtypeStruct(xg.shape, xg.dtype),
        input_output_aliases={2: 0},
        compiler_params=_params(("arbitrary",)),
        name="moe_pad_fill",
    )(pad_start, pad_len, xg)


SC_CORES, SC_SUBCORES = 2, 16
SC_WORKERS = SC_CORES * SC_SUBCORES
SC_W = 64


def _sc_worker_id():
    return lax.axis_index("s") * SC_CORES + lax.axis_index("c")


def _sc_dispatch(h3, slot3):
    nchunk = T // SC_WORKERS // SC_W
    mesh = plsc.VectorSubcoreMesh(core_axis_name="c", subcore_axis_name="s")
    tile = (SC_W, ROWS_PER_TOKEN, LANES)

    @functools.partial(
        pl.kernel, mesh=mesh,
        out_type=jax.ShapeDtypeStruct((SP, ROWS_PER_TOKEN, LANES), h3.dtype),
        scratch_types=[pltpu.VMEM((TOP_K, SC_W), I32), pltpu.VMEM((TOP_K, SC_W), I32),
                       pltpu.VMEM(tile, h3.dtype), pltpu.VMEM(tile, h3.dtype),
                       pltpu.SemaphoreType.DMA((2,)), pltpu.SemaphoreType.DMA((2,))],
    )
    def k(h_hbm, slot_hbm, xg_hbm, idx0, idx1, rows0, rows1, lsem, ssem):
        first = _sc_worker_id() * nchunk
        idx = (idx0, idx1)
        rows = (rows0, rows1)

        def loads(j, b):
            blk = first + j
            tok = pl.multiple_of(blk * SC_W, SC_W)
            return (pltpu.make_async_copy(slot_hbm.at[blk], idx[b], lsem.at[b]),
                    pltpu.make_async_copy(h_hbm.at[pl.ds(tok, SC_W)], rows[b], lsem.at[b]))

        def scatters(b):
            return [pltpu.make_async_copy(rows[b], xg_hbm.at[idx[b].at[kk]], ssem.at[b]) for kk in range(TOP_K)]

        for cp in loads(0, 0):
            cp.start()

        @pl.loop(0, nchunk, step=2)
        def _(j):
            for b in (0, 1):
                jj = j + b
                for cp in loads(jj, b):
                    cp.wait()
                for cp in scatters(b):
                    cp.start()

                @pl.when(jj + 1 < nchunk)
                def _():
                    @pl.when(jj >= 1)
                    def _():
                        for cp in scatters(1 - b):
                            cp.wait()
                    for cp in loads(jj + 1, 1 - b):
                        cp.start()

        for b in (0, 1):
            for cp in scatters(b):
                cp.wait()

    return k(h3, slot3)


NSUB = GM // GM_SUB
SUB_ROWS = GM_SUB * ROWS_PER_TOKEN


def _per_block_count(count, fn):
    for n in range(1, NSUB + 1):
        @pl.when(count == n)
        def _(n=n):
            fn(n)


def _gmm_kernel(layer, tile_e_ref, tile_blk_ref, tile_nsub_ref, tile_run_ref, tile_next_ref,
                xg_hbm, wg_hbm, wu_hbm, wd_hbm, yg_hbm,
                x_st, y_st, wg_st, wu_st, wd_st, wgu_scr, wd_scr, sems, xsems, ysems):
    j = pl.program_id(0)
    last_step = pl.num_programs(0) - 1
    nsub = tile_nsub_ref[j]
    run = tile_run_ref[j]
    slot = j % 2
    x_ref = x_st.at[slot]
    y_ref = y_st.at[slot]

    def tile_rows(t, n):
        return pl.ds(pl.multiple_of(tile_blk_ref[t] * SUB_ROWS, SUB_ROWS), n * SUB_ROWS)

    def x_copy(t, n):
        return pltpu.make_async_copy(xg_hbm.at[tile_rows(t, n)], x_st.at[t % 2, pl.ds(0, n * SUB_ROWS)], xsems.at[t % 2])

    def y_copy(t, n):
        return pltpu.make_async_copy(y_st.at[t % 2, pl.ds(0, n * SUB_ROWS)], yg_hbm.at[tile_rows(t, n)], ysems.at[t % 2])

    @pl.when(j == 0)
    def _():
        _per_block_count(nsub, lambda n: x_copy(j, n).start())

    _per_block_count(nsub, lambda n: x_copy(j, n).wait())

    @pl.when(j < last_step)
    def _():
        _per_block_count(tile_nsub_ref[j + 1], lambda n: x_copy(j + 1, n).start())

    @pl.when(j >= 2)
    def _():
        _per_block_count(tile_nsub_ref[j - 2], lambda n: y_copy(j - 2, n).wait())

    def weight_copies(e, slot):
        return [pltpu.make_async_copy(src.at[layer, e], dst.at[slot], sems.at[slot])
                for src, dst in ((wg_hbm, wg_st), (wu_hbm, wu_st), (wd_hbm, wd_st))]

    @pl.when(run >= 0)
    def _():
        @pl.when(j == 0)
        def _():
            for cp in weight_copies(tile_e_ref[j], run):
                cp.start()

        for cp in weight_copies(tile_e_ref[j], run):
            cp.wait()

        nxt = tile_next_ref[j]

        @pl.when(nxt >= 0)
        def _():
            for cp in weight_copies(nxt, 1 - run):
                cp.start()

        wgu_scr[:, 0:D_EXPERT] = wg_st[run].astype(BF16)
        wgu_scr[:, D_EXPERT:2 * D_EXPERT] = wu_st[run].astype(BF16)
        wd_scr[...] = wd_st[run].astype(BF16)

    def gate_up(s):
        x_hi, x_lo = _unpack_rows(_load_token_tiles(x_ref, GM_SUB, s * GM_SUB))
        return (jnp.dot(x_hi.astype(BF16), wgu_scr[0:PACKED, :], preferred_element_type=F32)
                + jnp.dot(x_lo.astype(BF16), wgu_scr[PACKED:D, :], preferred_element_type=F32))

    def down(s, gu):
        hid = _silu(gu[:, 0:D_EXPERT]) * gu[:, D_EXPERT:2 * D_EXPERT]
        y = jnp.dot(hid.astype(BF16), wd_scr[...], preferred_element_type=F32)
        _store_token_tiles(y_ref, _pack_rows(y), s * GM_SUB)

    def compute_and_send(n):
        pending = gate_up(0)
        for s in range(n):
            ahead = gate_up(s + 1) if s + 1 < n else None
            down(s, pending)
            pending = ahead
        y_copy(j, n).start()

    _per_block_count(nsub, compute_and_send)

    @pl.when(j == last_step)
    def _():
        _per_block_count(tile_nsub_ref[jnp.maximum(j - 1, 0)], lambda n: y_copy(j - 1, n).wait())
        _per_block_count(nsub, lambda n: y_copy(j, n).wait())


def _gmm(tile_e, tile_blk, tile_nsub, tile_run, tile_next, xg, l, w_gate, w_up, w_down):
    hbm = pl.BlockSpec(memory_space=pl.ANY)
    stage = pltpu.VMEM((2, GM * ROWS_PER_TOKEN, LANES), U32)
    grid_spec = pltpu.PrefetchScalarGridSpec(
        num_scalar_prefetch=5,
        grid=(NT_MAX,),
        in_specs=[hbm, hbm, hbm, hbm],
        out_specs=hbm,
        scratch_shapes=[stage, stage,
                        pltpu.VMEM((2, D, D_EXPERT), F32), pltpu.VMEM((2, D, D_EXPERT), F32),
                        pltpu.VMEM((2, D_EXPERT, D), F32),
                        pltpu.VMEM((D, 2 * D_EXPERT), BF16), pltpu.VMEM((D_EXPERT, D), BF16),
                        pltpu.SemaphoreType.DMA((2,)), pltpu.SemaphoreType.DMA((2,)), pltpu.SemaphoreType.DMA((2,))],
    )
    return pl.pallas_call(
        functools.partial(_gmm_kernel, l),
        grid_spec=grid_spec,
        out_shape=jax.ShapeDtypeStruct((SP * ROWS_PER_TOKEN, LANES), U32),
        compiler_params=_params(("arbitrary",)),
        name="moe_grouped_matmul",
    )(tile_e, tile_blk, tile_nsub, tile_run, tile_next, xg, w_gate, w_up, w_down)


def _sc_gather(table3, idx):
    n_idx = idx.shape[0]
    per_w = n_idx // SC_WORKERS
    nchunk = per_w // SC_W
    mesh = plsc.VectorSubcoreMesh(core_axis_name="c", subcore_axis_name="s")
    tile = (SC_W, ROWS_PER_TOKEN, LANES)

    @functools.partial(
        pl.kernel, mesh=mesh,
        out_type=jax.ShapeDtypeStruct((n_idx, ROWS_PER_TOKEN, LANES), table3.dtype),
        scratch_types=[pltpu.VMEM((per_w,), I32), pltpu.VMEM(tile, table3.dtype), pltpu.VMEM(tile, table3.dtype),
                       pltpu.SemaphoreType.DMA((2,)), pltpu.SemaphoreType.DMA((2,))],
    )
    def k(table_hbm, idx_hbm, out_hbm, idx_v, rows0, rows1, gsem, wsem):
        base = pl.multiple_of(_sc_worker_id() * per_w, per_w)
        rows = (rows0, rows1)
        pltpu.sync_copy(idx_hbm.at[pl.ds(base, per_w)], idx_v)

        def gather(j, b):
            ids = idx_v.at[pl.ds(pl.multiple_of(j * SC_W, SC_W), SC_W)]
            return pltpu.make_async_copy(table_hbm.at[ids], rows[b], gsem.at[b])

        def write(j, b):
            dst = out_hbm.at[pl.ds(pl.multiple_of(base + j * SC_W, SC_W), SC_W)]
            return pltpu.make_async_copy(rows[b], dst, wsem.at[b])

        gather(0, 0).start()

        @pl.loop(0, nchunk, step=2)
        def _(j):
            for b in (0, 1):
                jj = j + b
                gather(jj, b).wait()
                write(jj, b).start()

                @pl.when(jj + 1 < nchunk)
                def _():
                    @pl.when(jj >= 1)
                    def _():
                        write(jj - 1, 1 - b).wait()
                    gather(jj + 1, 1 - b).start()

        write(nchunk - 2, 0).wait()
        write(nchunk - 1, 1).wait()

    return k(table3, idx)


def _combine_kernel(wn_ref, x_ref, mod_ref, gpre_ref, gp_ref, wsg_ref, wsu_ref, wsd_ref, y_ref, o_ref, eye_ref):
    @pl.when(pl.program_id(0) == 0)
    def _():
        r = lax.broadcasted_iota(I32, (TD, TD), 0)
        c = lax.broadcasted_iota(I32, (TD, TD), 1)
        eye_ref[...] = jnp.where(r == c, 1.0, 0.0).astype(BF16)

    m = mod_ref[0, 0]
    x = x_ref[...]
    hb = (_rms(x, gpre_ref[...]) * (1.0 + m[4:5]) + m[3:4]).astype(BF16)
    hid = (_silu(jnp.dot(hb, wsg_ref[...], preferred_element_type=F32))
           * jnp.dot(hb, wsu_ref[...], preferred_element_type=F32))
    acc = jnp.dot(hid.astype(BF16), wsd_ref[...], preferred_element_type=F32)

    eye = eye_ref[...]
    nt = lambda a, b: lax.dot_general(a, b, (((1,), (1,)), ((), ())), preferred_element_type=F32)
    w1, w2, w3 = _split3(wn_ref[...])
    w_t = nt(eye, w1) + nt(eye, w2) + nt(eye, w3)

    acc_hi = acc[:, :PACKED]
    acc_lo = acc[:, PACKED:]
    for k in range(TOP_K):
        y_hi, y_lo = _unpack_rows(_load_token_tiles(y_ref, TD, k * TD))
        acc_hi = acc_hi + y_hi * w_t[:, k:k + 1]
        acc_lo = acc_lo + y_lo * w_t[:, k:k + 1]
    acc = jnp.concatenate([acc_hi, acc_lo], axis=1)
    o_ref[...] = x + m[5:6] * _rms(acc, gp_ref[...])


def _combine(wn, x, mod, l, g_pre, g_post, ws_gate, ws_up, ws_down, ybuf, first_tok, n_tok):
    off = first_tok // TD
    tiles_per_dec = DEC_SEQ // TD
    npd = TP // TD
    mod_row = lambda i: jnp.where(i + off < npd, 0, 1 + (i + off - npd) // tiles_per_dec)
    full = lambda *shape: pl.BlockSpec(shape, lambda i: (0,) * len(shape))
    y_spec = pl.BlockSpec((TOP_K * TD * ROWS_PER_TOKEN, LANES), lambda i: (i, 0))
    return pl.pallas_call(
        _combine_kernel,
        grid=(n_tok // TD,),
        in_specs=[
            pl.BlockSpec((TOP_K, TD), lambda i: (0, i + off)),
            pl.BlockSpec((TD, D), lambda i: (i + off, 0)),
            pl.BlockSpec((1, 1, 6, D), lambda i: (l, mod_row(i), 0, 0)),
            full(1, D), full(1, D), full(D, D_SHARED), full(D, D_SHARED), full(D_SHARED, D),
            y_spec,
        ],
        out_specs=pl.BlockSpec((TD, D), lambda i: (i, 0)),
        out_shape=jax.ShapeDtypeStruct((n_tok, D), F32),
        scratch_shapes=[pltpu.VMEM((TD, TD), BF16)],
        compiler_params=_params(("arbitrary",)),
        name="moe_combine",
    )(wn, x, mod, g_pre, g_post, ws_gate, ws_up, ws_down, ybuf)


def _moe_layer(x, h, mod, l, g_pre, g_post, w_router, e_bias, w_gate, w_up, w_down,
               ws_gate, ws_up, ws_down):
    top_e, wn, rk, cnt = _router(h, w_router.T, e_bias.reshape(N_EXPERTS, 1))
    cnt = cnt.reshape(N_EXPERTS).astype(I32)
    padded = (cnt + GM_SUB - 1) // GM_SUB * GM_SUB
    ends = jnp.cumsum(padded)
    offs = ends - padded
    eid = jnp.arange(N_EXPERTS, dtype=I32)[:, None, None]
    slot = rk + jnp.sum(jnp.where(top_e[None] == eid, offs[:, None, None], 0), axis=0)
    ntile = (cnt + GM - 1) // GM
    tile_ends = jnp.cumsum(ntile)
    tid = jnp.arange(NT_MAX, dtype=I32)
    last = jnp.maximum(tile_ends[-1] - 1, 0)
    tile_e = jnp.minimum(jnp.sum((tid[:, None] >= tile_ends[None, :]).astype(I32), axis=1), N_EXPERTS - 1)
    tile_e = jnp.where(tid <= last, tile_e, tile_e[last])
    own = tile_e[:, None] == jnp.arange(N_EXPERTS, dtype=I32)[None, :]
    pick = lambda per_expert: jnp.sum(jnp.where(own, per_expert[None, :], 0), axis=1)
    t_in = tid - pick(tile_ends - ntile)
    tile_blk = jnp.where(tid <= last, (pick(offs) + GM * t_in) // GM_SUB, 0)
    tile_rows = jnp.clip(pick(cnt) - GM * t_in, 0, GM)
    tile_nsub = jnp.where(tid <= last, (tile_rows + GM_SUB - 1) // GM_SUB, 0)
    starts = jnp.logical_and(jnp.arange(NT_MAX) <= last,
                             jnp.concatenate([jnp.ones((1,), bool), tile_e[1:] != tile_e[:-1]]))
    tile_run = jnp.where(starts, (jnp.cumsum(starts.astype(I32)) - 1) % 2, -1)
    later = jnp.logical_and(jnp.arange(N_EXPERTS, dtype=I32)[None, :] > tile_e[:, None], (cnt > 0)[None, :])
    tile_next = jnp.min(jnp.where(later, jnp.arange(N_EXPERTS, dtype=I32)[None, :], N_EXPERTS), axis=1)
    tile_next = jnp.where(tile_next < N_EXPERTS, tile_next, -1)
    slot3 = slot.reshape(TOP_K, T // SC_W, SC_W).transpose(1, 0, 2)
    xg = _sc_dispatch(h.reshape(T, ROWS_PER_TOKEN, LANES), slot3)
    xg = _pad_fill(offs + cnt, padded - cnt, xg).reshape(SP * ROWS_PER_TOKEN, LANES)
    yg = _gmm(tile_e, tile_blk, tile_nsub.astype(I32), tile_run.astype(I32), tile_next.astype(I32),
              xg, l, w_gate, w_up, w_down)
    yg3 = yg.reshape(SP, ROWS_PER_TOKEN, LANES)
    ws = (ws_gate.astype(BF16), ws_up.astype(BF16), ws_down.astype(BF16))
    outs = []
    for first_tok, n_tok in ((0, TP), (TP, TS)):
        ids = slot[:, first_tok:first_tok + n_tok].reshape(TOP_K, n_tok // TD, TD).transpose(1, 0, 2)
        ybuf = _sc_gather(yg3, ids.reshape(TOP_K * n_tok))
        outs.append(_combine(wn, x, mod, l, g_pre, g_post, *ws,
                             ybuf.reshape(TOP_K * n_tok * ROWS_PER_TOKEN, LANES), first_tok, n_tok))
    return outs


def _rope_tables():
    n = DEC_SEQ
    rows = n // GRID_W
    row = jnp.repeat(jnp.arange(rows), GRID_W).astype(F32)
    col = jnp.tile(jnp.arange(GRID_W), rows).astype(F32)
    half = DQK_B // 2
    inv = ROPE_BASE ** (-jnp.arange(0, half, 2, dtype=F32) / half)
    ang_r = row[:, None] * inv
    ang_c = col[:, None] * inv
    ang = jnp.concatenate([ang_r, ang_r, ang_c, ang_c], axis=-1)
    reps = QK_B // DQK_B
    return jnp.tile(jnp.cos(ang), (1, reps)), jnp.tile(jnp.sin(ang), (1, reps))


def _pad_in_proj(w):
    s = [0, Q_A, 2 * Q_A, 2 * Q_A + V_A, 2 * Q_A + 2 * V_A]
    s += [s[-1] + GATE_RANK, s[-1] + 2 * GATE_RANK]
    s += [s[-1] + QK_B, s[-1] + 2 * QK_B, s[-1] + 2 * QK_B + V_B]
    gates = jnp.pad(w[:, s[4]:s[6]], ((0, 0), (0, GL_PAD - 2 * GATE_RANK)))
    return jnp.concatenate([w[:, s[0]:s[4]], gates, w[:, s[6]:s[9]]], axis=1).astype(BF16)


def kernel(x_prompt, x_sample, c, c_ctx, state_gla, cache_k, cache_v, ada_w, ada_b, norm_pre_mix, norm_post_mix, norm_pre_ffn, norm_post_ffn, ab_w_in, gla_w_g2, gla_b_g2, gla_norm_g, diff_lambda, diff_norm_g, ab_w_out, sgu_w_in, sgu_b_in, sgu_norm_g, sgu_w_s, sgu_b_s, sgu_w_out, moe_w_router, moe_e_bias, moe_w_gate, moe_w_up, moe_w_down, moe_ws_gate, moe_ws_up, moe_ws_down):
    depth = ada_w.shape[0]
    assert x_prompt.shape == (BATCH, SEQ, D) and x_sample.shape == (DEC_BATCH, DEC_SEQ, D)
    assert state_gla.shape == (DEC_BATCH, (depth + 1) // 2, 2, H_A, DK_A, DV_A)
    assert cache_k.shape == (DEC_BATCH, 1, H_B, 2, PAST_LEN, DQK_B) and cache_v.shape == (DEC_BATCH, 1, H_B, PAST_LEN, DV_B)
    assert depth == 2 and moe_w_gate.shape == (depth, N_EXPERTS, D, D_EXPERT)
    xp, xs = x_prompt.reshape(TP, D), x_sample.reshape(TS, D)
    cond = jnp.concatenate([c_ctx[None, :], c, jnp.zeros((8 - 1 - DEC_BATCH, D), F32)], axis=0)
    mod = _modulation(cond, ada_w, ada_b)
    cos, sin = _rope_tables()
    vec = lambda a: a.reshape(1, -1)
    new_s = new_k = new_v = None
    for l in range(depth):
        if l % 2 == 0:
            e = l // 2
            lam_init = 0.8 - 0.6 * math.exp(-0.3 * l)
            a, r_a, gl, q_b, k_b, v_b, new_k, new_v = _in_proj(xp, xs, mod, l, vec(norm_pre_mix[l]),
                                                               _pad_in_proj(ab_w_in[e]), cos, sin)
            s0_t = jnp.swapaxes(state_gla[:, e], -1, -2)
            same_head = jnp.eye(H_A, dtype=bool)[None, None, :, None, :, None]
            s0_t = jnp.where(same_head, s0_t[:, :, :, :, None, :], 0.0).reshape(DEC_BATCH, 2, V_A, Q_A)
            o_f, o_bw, s_fin_t = _gla(a, gl, gla_w_g2[e], gla_b_g2[e].reshape(2, 1, Q_A), s0_t)
            o_att = _diff_attention(q_b, k_b, v_b, cache_k, cache_v, diff_lambda[e], lam_init)
            x, h = _mix_out(lam_init, o_f, o_bw, r_a, o_att, xp, xs, mod, l, vec(gla_norm_g[e]), vec(diff_norm_g[e]),
                            ab_w_out[e].astype(BF16), vec(norm_post_mix[l]), vec(norm_pre_ffn[l]))
            new_s = jnp.swapaxes(s_fin_t, -1, -2)[:, None]
        else:
            o = l // 2
            x, h = _sgu(xp, xs, mod, l, vec(norm_pre_mix[l]), sgu_w_in[o].astype(BF16), vec(sgu_b_in[o]),
                        vec(sgu_norm_g[o]), sgu_w_s[o], sgu_b_s[o].T, sgu_w_out[o].astype(BF16),
                        vec(norm_post_mix[l]), vec(norm_pre_ffn[l]))
        xp, xs = _moe_layer(x, h, mod, l, vec(norm_pre_ffn[l]), vec(norm_post_ffn[l]), moe_w_router[l], moe_e_bias[l],
                            moe_w_gate, moe_w_up, moe_w_down, moe_ws_gate[l], moe_ws_up[l], moe_ws_down[l])
    y_prompt = xp.reshape(BATCH, SEQ, D)
    y_sample = xs.reshape(DEC_BATCH, DEC_SEQ, D)
    return (y_prompt, y_sample, new_s, new_k, new_v)
```
